```python
import jax, jax.numpy as jnp
from jax import lax
import numpy as np

D_MODEL = 1024
BATCH = 8
SEQ = 4096
DEPTH = 1

HGRN_HEADS = 4
HGRN_KEY_DIM = 128
HGRN_VAL_DIM = (D_MODEL // 2) // HGRN_HEADS
HGRN_KEY_WIDTH = HGRN_HEADS * HGRN_KEY_DIM
HGRN_WIDTH = HGRN_HEADS * HGRN_VAL_DIM
CHUNK = 64
POOL_WINDOWS = (2, 4, 8, 16)
POOL_GROUPS = len(POOL_WINDOWS)
POOL_WIDTH = D_MODEL - HGRN_WIDTH
POOL_GROUP_DIM = POOL_WIDTH // POOL_GROUPS
POOL_MAX_W = max(POOL_WINDOWS)
MIX_WIDTH = HGRN_WIDTH + POOL_WIDTH
IN_WIDTH = 2 * HGRN_KEY_WIDTH + 2 * HGRN_WIDTH + POOL_WIDTH
MEM_LEN = 256
XATTN_HEADS = 4
XATTN_HEAD_DIM = D_MODEL // XATTN_HEADS
D_FF = 4 * D_MODEL
EPS = 1e-6

kernel_name = "hymba_style_hgrn2_pool_hybrid"


def rmsnorm(x, g):
    xf = x.astype(jnp.float32)
    y = xf * lax.rsqrt(jnp.mean(xf * xf, axis=-1, keepdims=True) + EPS)
    return (y * g.astype(jnp.float32)).astype(x.dtype)


def hgrn2_chunkwise(q, k, v, log_f):
    B, S, H, DK = q.shape
    DV = v.shape[-1]
    n = S // CHUNK

    def to_chunks(a):
        return a.reshape(B, n, CHUNK, H, a.shape[-1]).transpose(1, 0, 3, 2, 4)

    qc, kc, vc, fc = to_chunks(q), to_chunks(k), to_chunks(v), to_chunks(log_f)
    causal = jnp.tril(jnp.ones((CHUNK, CHUNK), dtype=bool))

    def step(state, inp):
        q_, k_, v_, lf = inp
        b = jnp.cumsum(lf, axis=2)
        diff = b[:, :, :, None, :] - b[:, :, None, :, :]
        decay = jnp.exp(jnp.where(causal[None, None, :, :, None], diff, -jnp.inf))
        scores = jnp.einsum('bhtk,bhsk,bhtsk->bhts', q_, k_, decay)
        o_intra = jnp.einsum('bhts,bhsv->bhtv', scores, v_)
        o_inter = jnp.einsum('bhtk,bhkv->bhtv', q_ * jnp.exp(b), state)
        b_last = b[:, :, -1:, :]
        k_dec = k_ * jnp.exp(b_last - b)
        state = jnp.exp(b_last[:, :, 0, :])[..., None] * state + jnp.einsum('bhsk,bhsv->bhkv', k_dec, v_)
        return state, o_intra + o_inter

    s0 = jnp.zeros((B, H, DK, DV), jnp.float32)
    _, o = lax.scan(step, s0, (qc, kc, vc, fc))
    return o.transpose(1, 0, 3, 2, 4).reshape(B, S, H, DV)


def multiscale_pool(p, w_pool, pool_scale):
    B, S, _ = p.shape
    pg = p.astype(jnp.float32).reshape(B, S, POOL_GROUPS, POOL_GROUP_DIM)
    cs = jnp.cumsum(pg, axis=1)
    cs = jnp.pad(cs, ((0, 0), (POOL_MAX_W, 0), (0, 0), (0, 0)))
    pos = (jnp.arange(S) + 1)
    outs = []
    for gi, w in enumerate(POOL_WINDOWS):
        win = cs[:, POOL_MAX_W:, gi] - cs[:, POOL_MAX_W - w:POOL_MAX_W - w + S, gi]
        cnt = jnp.minimum(pos, w).astype(jnp.float32)[None, :, None]
        outs.append(win / cnt - pg[:, :, gi])
    pooled = jnp.stack(outs, axis=2)
    y = jnp.einsum('bsgc,gcd->bsgd', pooled, w_pool.astype(jnp.float32))
    return y.reshape(B, S, POOL_WIDTH) * pool_scale.astype(jnp.float32)


def _fwd_setup_inputs(seed: int = 0) -> dict:
    key = jax.random.key(seed)
    ks = jax.random.split(key, 24)
    f32 = jnp.float32

    def dense(k, shape, fan_in):
        return jax.random.normal(k, shape, f32) * (fan_in ** -0.5)

    def gain(k, shape):
        return 1.0 + 0.02 * jax.random.normal(k, shape, f32)

    return {
        "x": jax.random.normal(ks[0], (BATCH, SEQ, D_MODEL), f32),
        "mem": jax.random.normal(ks[1], (BATCH, MEM_LEN, D_MODEL), f32),
        "norm_mix_g": gain(ks[2], (DEPTH, D_MODEL)),
        "w_in": dense(ks[3], (DEPTH, D_MODEL, IN_WIDTH), D_MODEL),
        "lb_logits": 0.5 * jax.random.normal(ks[4], (DEPTH + 1, HGRN_KEY_WIDTH), f32),
        "hgrn_norm_g": gain(ks[5], (DEPTH, HGRN_HEADS, HGRN_VAL_DIM)),
        "w_pool": dense(ks[6], (DEPTH, POOL_GROUPS, POOL_GROUP_DIM, POOL_GROUP_DIM), POOL_GROUP_DIM),
        "pool_scale": gain(ks[7], (DEPTH, POOL_WIDTH)),
        "w_out": dense(ks[8], (DEPTH, MIX_WIDTH, D_MODEL), MIX_WIDTH),
        "norm_x_g": gain(ks[9], (DEPTH, D_MODEL)),
        "norm_mem_g": gain(ks[10], (DEPTH, D_MODEL)),
        "w_xq": dense(ks[11], (DEPTH, D_MODEL, XATTN_HEADS, XATTN_HEAD_DIM), D_MODEL),
        "w_xk": dense(ks[12], (DEPTH, D_MODEL, XATTN_HEADS, XATTN_HEAD_DIM), D_MODEL),
        "w_xv": dense(ks[13], (DEPTH, D_MODEL, XATTN_HEADS, XATTN_HEAD_DIM), D_MODEL),
        "w_xo": dense(ks[14], (DEPTH, XATTN_HEADS, XATTN_HEAD_DIM, D_MODEL), D_MODEL),
        "norm_ffn_g": gain(ks[15], (DEPTH, D_MODEL)),
        "w_ff1": dense(ks[16], (DEPTH, D_MODEL, D_FF), D_MODEL),
        "w_ff2": dense(ks[17], (DEPTH, D_FF, D_MODEL), D_FF),
        "final_norm_g": gain(ks[18], (D_MODEL,)),
    }


def _fwd_reference(x, mem, norm_mix_g, w_in, lb_logits, hgrn_norm_g, w_pool, pool_scale, w_out,
              norm_x_g, norm_mem_g, w_xq, w_xk, w_xv, w_xo, norm_ffn_g, w_ff1, w_ff2, final_norm_g):
    B, S, _ = x.shape
    f32 = jnp.float32
    lower_bounds = jnp.cumsum(jax.nn.softmax(lb_logits.astype(f32), axis=0), axis=0)
    split_at = [HGRN_KEY_WIDTH, 2 * HGRN_KEY_WIDTH,
                2 * HGRN_KEY_WIDTH + HGRN_WIDTH, 2 * HGRN_KEY_WIDTH + 2 * HGRN_WIDTH]
    for l in range(DEPTH):
        h = rmsnorm(x, norm_mix_g[l])
        z = jnp.einsum('bsd,de->bse', h, w_in[l])
        q_pre, f_pre, i_pre, g_pre, p = jnp.split(z, split_at, axis=-1)

        lb = lower_bounds[l]
        f = lb + (1.0 - lb) * jax.nn.sigmoid(f_pre.astype(f32))
        log_f = jnp.log(f)
        k = 1.0 - f
        q = jax.nn.silu(q_pre.astype(f32))
        hs = lambda a, d: a.reshape(B, S, HGRN_HEADS, d)
        o = hgrn2_chunkwise(hs(q, HGRN_KEY_DIM), hs(k, HGRN_KEY_DIM),
                            hs(i_pre.astype(f32), HGRN_VAL_DIM), hs(log_f, HGRN_KEY_DIM))
        o = o * lax.rsqrt(jnp.mean(o * o, axis=-1, keepdims=True) + EPS) * hgrn_norm_g[l].astype(f32)
        o_a = o.reshape(B, S, HGRN_WIDTH) * jax.nn.silu(g_pre.astype(f32))

        o_b = multiscale_pool(p, w_pool[l], pool_scale[l])

        mixed = jnp.concatenate([o_a, o_b], axis=-1).astype(x.dtype)
        x = x + jnp.einsum('bse,ed->bsd', mixed, w_out[l])

        hq = rmsnorm(x, norm_x_g[l])
        hm = rmsnorm(mem, norm_mem_g[l])
        xq = jnp.einsum('bsd,dhe->bshe', hq, w_xq[l])
        xk = jnp.einsum('bmd,dhe->bmhe', hm, w_xk[l])
        xv = jnp.einsum('bmd,dhe->bmhe', hm, w_xv[l])
        scores = jnp.einsum('bshe,bmhe->bhsm', xq, xk).astype(f32) * (XATTN_HEAD_DIM ** -0.5)
        probs = jax.nn.softmax(scores, axis=-1).astype(x.dtype)
        att = jnp.einsum('bhsm,bmhe->bshe', probs, xv)
        x = x + jnp.einsum('bshe,hed->bsd', att, w_xo[l])

        hf = rmsnorm(x, norm_ffn_g[l])
        u = jnp.square(jax.nn.relu(jnp.einsum('bsd,df->bsf', hf, w_ff1[l])))
        x = x + jnp.einsum('bsf,fd->bsd', u, w_ff2[l])
    return rmsnorm(x, final_norm_g)


import jax as _jax
import jax.numpy as _jnp

TWIN_FORMAT = 'train_step'
FWD_PARAMS = ['x', 'mem', 'norm_mix_g', 'w_in', 'lb_logits', 'hgrn_norm_g', 'w_pool', 'pool_scale', 'w_out', 'norm_x_g', 'norm_mem_g', 'w_xq', 'w_xk', 'w_xv', 'w_xo', 'norm_ffn_g', 'w_ff1', 'w_ff2', 'final_norm_g']
TWIN_WEIGHTS = ['norm_mix_g', 'w_in', 'lb_logits', 'hgrn_norm_g', 'w_pool', 'pool_scale', 'w_out', 'norm_x_g', 'norm_mem_g', 'w_xq', 'w_xk', 'w_xv', 'w_xo', 'norm_ffn_g', 'w_ff1', 'w_ff2', 'final_norm_g']
TWIN_DIFF_INPUT = 'x'
TWIN_INPUTS = ['x', 'mem', 'norm_mix_g', 'w_in', 'lb_logits', 'hgrn_norm_g', 'w_pool', 'pool_scale', 'w_out', 'norm_x_g', 'norm_mem_g', 'w_xq', 'w_xk', 'w_xv', 'w_xo', 'norm_ffn_g', 'w_ff1', 'w_ff2', 'final_norm_g', 'loss_target', 'm_norm_mix_g', 'm_w_in', 'm_lb_logits', 'm_hgrn_norm_g', 'm_w_pool', 'm_pool_scale', 'm_w_out', 'm_norm_x_g', 'm_norm_mem_g', 'm_w_xq', 'm_w_xk', 'm_w_xv', 'm_w_xo', 'm_norm_ffn_g', 'm_w_ff1', 'm_w_ff2', 'm_final_norm_g', 'v_norm_mix_g', 'v_w_in', 'v_lb_logits', 'v_hgrn_norm_g', 'v_w_pool', 'v_pool_scale', 'v_w_out', 'v_norm_x_g', 'v_norm_mem_g', 'v_w_xq', 'v_w_xk', 'v_w_xv', 'v_w_xo', 'v_norm_ffn_g', 'v_w_ff1', 'v_w_ff2', 'v_final_norm_g']
TWIN_OUTPUTS = ['loss', 'grad_x', 'grad_norm_mix_g', 'grad_w_in', 'grad_lb_logits', 'grad_hgrn_norm_g', 'grad_w_pool', 'grad_pool_scale', 'grad_w_out', 'grad_norm_x_g', 'grad_norm_mem_g', 'grad_w_xq', 'grad_w_xk', 'grad_w_xv', 'grad_w_xo', 'grad_norm_ffn_g', 'grad_w_ff1', 'grad_w_ff2', 'grad_final_norm_g', 'delta_norm_mix_g', 'delta_w_in', 'delta_lb_logits', 'delta_hgrn_norm_g', 'delta_w_pool', 'delta_pool_scale', 'delta_w_out', 'delta_norm_x_g', 'delta_norm_mem_g', 'delta_w_xq', 'delta_w_xk', 'delta_w_xv', 'delta_w_xo', 'delta_norm_ffn_g', 'delta_w_ff1', 'delta_w_ff2', 'delta_final_norm_g', 'new_m_norm_mix_g', 'new_m_w_in', 'new_m_lb_logits', 'new_m_hgrn_norm_g', 'new_m_w_pool', 'new_m_pool_scale', 'new_m_w_out', 'new_m_norm_x_g', 'new_m_norm_mem_g', 'new_m_w_xq', 'new_m_w_xk', 'new_m_w_xv', 'new_m_w_xo', 'new_m_norm_ffn_g', 'new_m_w_ff1', 'new_m_w_ff2', 'new_m_final_norm_g', 'new_v_norm_mix_g', 'new_v_w_in', 'new_v_lb_logits', 'new_v_hgrn_norm_g', 'new_v_w_pool', 'new_v_pool_scale', 'new_v_w_out', 'new_v_norm_x_g', 'new_v_norm_mem_g', 'new_v_w_xq', 'new_v_w_xk', 'new_v_w_xv', 'new_v_w_xo', 'new_v_norm_ffn_g', 'new_v_w_ff1', 'new_v_w_ff2', 'new_v_final_norm_g']
TWIN_LEAF_KINDS = {'loss': 'loss', 'grad_x': 'grad_x', 'grad_norm_mix_g': 'grad_w', 'grad_w_in': 'grad_w', 'grad_lb_logits': 'grad_w', 'grad_hgrn_norm_g': 'grad_w', 'grad_w_pool': 'grad_w', 'grad_pool_scale': 'grad_w', 'grad_w_out': 'grad_w', 'grad_norm_x_g': 'grad_w', 'grad_norm_mem_g': 'grad_w', 'grad_w_xq': 'grad_w', 'grad_w_xk': 'grad_w', 'grad_w_xv': 'grad_w', 'grad_w_xo': 'grad_w', 'grad_norm_ffn_g': 'grad_w', 'grad_w_ff1': 'grad_w', 'grad_w_ff2': 'grad_w', 'grad_final_norm_g': 'grad_w', 'delta_norm_mix_g': 'delta_w', 'delta_w_in': 'delta_w', 'delta_lb_logits': 'delta_w', 'delta_hgrn_norm_g': 'delta_w', 'delta_w_pool': 'delta_w', 'delta_pool_scale': 'delta_w', 'delta_w_out': 'delta_w', 'delta_norm_x_g': 'delta_w', 'delta_norm_mem_g': 'delta_w', 'delta_w_xq': 'delta_w', 'delta_w_xk': 'delta_w', 'delta_w_xv': 'delta_w', 'delta_w_xo': 'delta_w', 'delta_norm_ffn_g': 'delta_w', 'delta_w_ff1': 'delta_w', 'delta_w_ff2': 'delta_w', 'delta_final_norm_g': 'delta_w', 'new_m_norm_mix_g': 'new_m', 'new_m_w_in': 'new_m', 'new_m_lb_logits': 'new_m', 'new_m_hgrn_norm_g': 'new_m', 'new_m_w_pool': 'new_m', 'new_m_pool_scale': 'new_m', 'new_m_w_out': 'new_m', 'new_m_norm_x_g': 'new_m', 'new_m_norm_mem_g': 'new_m', 'new_m_w_xq': 'new_m', 'new_m_w_xk': 'new_m', 'new_m_w_xv': 'new_m', 'new_m_w_xo': 'new_m', 'new_m_norm_ffn_g': 'new_m', 'new_m_w_ff1': 'new_m', 'new_m_w_ff2': 'new_m', 'new_m_final_norm_g': 'new_m', 'new_v_norm_mix_g': 'new_v', 'new_v_w_in': 'new_v', 'new_v_lb_logits': 'new_v', 'new_v_hgrn_norm_g': 'new_v', 'new_v_w_pool': 'new_v', 'new_v_pool_scale': 'new_v', 'new_v_w_out': 'new_v', 'new_v_norm_x_g': 'new_v', 'new_v_norm_mem_g': 'new_v', 'new_v_w_xq': 'new_v', 'new_v_w_xk': 'new_v', 'new_v_w_xv': 'new_v', 'new_v_w_xo': 'new_v', 'new_v_norm_ffn_g': 'new_v', 'new_v_w_ff1': 'new_v', 'new_v_w_ff2': 'new_v', 'new_v_final_norm_g': 'new_v'}


def _forward(args):
    return _fwd_reference(*[args[k] for k in FWD_PARAMS])


def _output_shape():
    def fwd():
        inp = _fwd_setup_inputs(0)
        return _fwd_reference(*[inp[k] for k in FWD_PARAMS])
    out = _jax.eval_shape(fwd)
    return out.shape, out.dtype

N_MICROBATCH = 1
ADAM_LR = 0.001
ADAM_B1 = 0.9
ADAM_B2 = 0.999
ADAM_EPS = 1e-08
ADAM_WD = 0.01
ADAM_STEP = 10
PER_EXAMPLE_BATCH_AXIS = {'x': 0, 'mem': 0, 'loss_target': 0}
SHARED_INPUTS = []
_WEIGHT_DTYPES = {'norm_mix_g': _jnp.float32, 'w_in': _jnp.float32, 'lb_logits': _jnp.float32, 'hgrn_norm_g': _jnp.float32, 'w_pool': _jnp.float32, 'pool_scale': _jnp.float32, 'w_out': _jnp.float32, 'norm_x_g': _jnp.float32, 'norm_mem_g': _jnp.float32, 'w_xq': _jnp.float32, 'w_xk': _jnp.float32, 'w_xv': _jnp.float32, 'w_xo': _jnp.float32, 'norm_ffn_g': _jnp.float32, 'w_ff1': _jnp.float32, 'w_ff2': _jnp.float32, 'final_norm_g': _jnp.float32}
MOMENT_SCALE = {'norm_mix_g': 1.353683e-01, 'w_in': 8.434803e-02, 'lb_logits': 7.981346e-03, 'hgrn_norm_g': 9.122658e-02, 'w_pool': 1.352918e-01, 'pool_scale': 1.334384e-01, 'w_out': 1.153276e-01, 'norm_x_g': 1.628051e-02, 'norm_mem_g': 2.412716e-02, 'w_xq': 1.594866e-02, 'w_xk': 1.600230e-02, 'w_xv': 1.643548e-02, 'w_xo': 1.655193e-02, 'norm_ffn_g': 1.437071e-01, 'w_ff1': 7.193671e-02, 'w_ff2': 1.429889e-01, 'final_norm_g': 3.223504e+01}


def _to_microbatches(a, axis):
    t = _jnp.moveaxis(a, axis, 0)
    t = t.reshape((N_MICROBATCH, t.shape[0] // N_MICROBATCH) + t.shape[1:])
    return _jnp.moveaxis(t, 1, axis + 1)


def setup_inputs(seed: int = 0) -> dict:
    inp = _fwd_setup_inputs(seed)
    key = _jax.random.fold_in(_jax.random.key(seed), 7919)
    shape, _ = _output_shape()
    out = dict(inp)
    out["loss_target"] = _jax.random.normal(_jax.random.fold_in(key, 0), shape, _jnp.float32)
    for i, name in enumerate(TWIN_WEIGHTS):
        w = inp[name].astype(_jnp.float32)
        if MOMENT_SCALE is None:
            s = _jnp.sqrt(_jnp.mean(_jnp.square(w)) + 1e-30)
        else:
            s = MOMENT_SCALE[name]
        km, kv = _jax.random.split(_jax.random.fold_in(key, i + 1))
        out[name] = w
        out["m_" + name] = s * _jax.random.normal(km, w.shape, _jnp.float32)
        out["v_" + name] = (s * s) * _jax.random.uniform(kv, w.shape, _jnp.float32, 0.5, 1.5)
    if N_MICROBATCH > 1:
        for name, axis in PER_EXAMPLE_BATCH_AXIS.items():
            out[name] = _to_microbatches(out[name], axis)
    return {'x': out['x'], 'mem': out['mem'], 'norm_mix_g': out['norm_mix_g'], 'w_in': out['w_in'], 'lb_logits': out['lb_logits'], 'hgrn_norm_g': out['hgrn_norm_g'], 'w_pool': out['w_pool'], 'pool_scale': out['pool_scale'], 'w_out': out['w_out'], 'norm_x_g': out['norm_x_g'], 'norm_mem_g': out['norm_mem_g'], 'w_xq': out['w_xq'], 'w_xk': out['w_xk'], 'w_xv': out['w_xv'], 'w_xo': out['w_xo'], 'norm_ffn_g': out['norm_ffn_g'], 'w_ff1': out['w_ff1'], 'w_ff2': out['w_ff2'], 'final_norm_g': out['final_norm_g'], 'loss_target': out['loss_target'], 'm_norm_mix_g': out['m_norm_mix_g'], 'm_w_in': out['m_w_in'], 'm_lb_logits': out['m_lb_logits'], 'm_hgrn_norm_g': out['m_hgrn_norm_g'], 'm_w_pool': out['m_w_pool'], 'm_pool_scale': out['m_pool_scale'], 'm_w_out': out['m_w_out'], 'm_norm_x_g': out['m_norm_x_g'], 'm_norm_mem_g': out['m_norm_mem_g'], 'm_w_xq': out['m_w_xq'], 'm_w_xk': out['m_w_xk'], 'm_w_xv': out['m_w_xv'], 'm_w_xo': out['m_w_xo'], 'm_norm_ffn_g': out['m_norm_ffn_g'], 'm_w_ff1': out['m_w_ff1'], 'm_w_ff2': out['m_w_ff2'], 'm_final_norm_g': out['m_final_norm_g'], 'v_norm_mix_g': out['v_norm_mix_g'], 'v_w_in': out['v_w_in'], 'v_lb_logits': out['v_lb_logits'], 'v_hgrn_norm_g': out['v_hgrn_norm_g'], 'v_w_pool': out['v_w_pool'], 'v_pool_scale': out['v_pool_scale'], 'v_w_out': out['v_w_out'], 'v_norm_x_g': out['v_norm_x_g'], 'v_norm_mem_g': out['v_norm_mem_g'], 'v_w_xq': out['v_w_xq'], 'v_w_xk': out['v_w_xk'], 'v_w_xv': out['v_w_xv'], 'v_w_xo': out['v_w_xo'], 'v_norm_ffn_g': out['v_norm_ffn_g'], 'v_w_ff1': out['v_w_ff1'], 'v_w_ff2': out['v_w_ff2'], 'v_final_norm_g': out['v_final_norm_g']}


def _loss(weights, diff, rest, loss_target):
    with _jax.named_scope("forward"):
        args = {**rest, TWIN_DIFF_INPUT: diff, **{k: w.astype(_WEIGHT_DTYPES[k]) for k, w in weights.items()}}
        y = _forward(args)
    with _jax.named_scope("loss_head"):
        err = _jnp.square(y.astype(_jnp.float32) - loss_target)
        return 0.5 * _jnp.sum(_jnp.mean(err, axis=-1)) if err.ndim else 0.5 * err


def _adamw(w, g, m, v):
    m = ADAM_B1 * m + (1.0 - ADAM_B1) * g
    v = ADAM_B2 * v + (1.0 - ADAM_B2) * _jnp.square(g)
    m_hat = m / (1.0 - ADAM_B1 ** ADAM_STEP)
    v_hat = v / (1.0 - ADAM_B2 ** ADAM_STEP)
    delta = -ADAM_LR * (m_hat / (_jnp.sqrt(v_hat) + ADAM_EPS) + ADAM_WD * w)
    return delta, m, v


def reference(x, mem, norm_mix_g, w_in, lb_logits, hgrn_norm_g, w_pool, pool_scale, w_out, norm_x_g, norm_mem_g, w_xq, w_xk, w_xv, w_xo, norm_ffn_g, w_ff1, w_ff2, final_norm_g, loss_target, m_norm_mix_g, m_w_in, m_lb_logits, m_hgrn_norm_g, m_w_pool, m_pool_scale, m_w_out, m_norm_x_g, m_norm_mem_g, m_w_xq, m_w_xk, m_w_xv, m_w_xo, m_norm_ffn_g, m_w_ff1, m_w_ff2, m_final_norm_g, v_norm_mix_g, v_w_in, v_lb_logits, v_hgrn_norm_g, v_w_pool, v_pool_scale, v_w_out, v_norm_x_g, v_norm_mem_g, v_w_xq, v_w_xk, v_w_xv, v_w_xo, v_norm_ffn_g, v_w_ff1, v_w_ff2, v_final_norm_g):
    given = dict(x=x, mem=mem, norm_mix_g=norm_mix_g, w_in=w_in, lb_logits=lb_logits, hgrn_norm_g=hgrn_norm_g, w_pool=w_pool, pool_scale=pool_scale, w_out=w_out, norm_x_g=norm_x_g, norm_mem_g=norm_mem_g, w_xq=w_xq, w_xk=w_xk, w_xv=w_xv, w_xo=w_xo, norm_ffn_g=norm_ffn_g, w_ff1=w_ff1, w_ff2=w_ff2, final_norm_g=final_norm_g, loss_target=loss_target, m_norm_mix_g=m_norm_mix_g, m_w_in=m_w_in, m_lb_logits=m_lb_logits, m_hgrn_norm_g=m_hgrn_norm_g, m_w_pool=m_w_pool, m_pool_scale=m_pool_scale, m_w_out=m_w_out, m_norm_x_g=m_norm_x_g, m_norm_mem_g=m_norm_mem_g, m_w_xq=m_w_xq, m_w_xk=m_w_xk, m_w_xv=m_w_xv, m_w_xo=m_w_xo, m_norm_ffn_g=m_norm_ffn_g, m_w_ff1=m_w_ff1, m_w_ff2=m_w_ff2, m_final_norm_g=m_final_norm_g, v_norm_mix_g=v_norm_mix_g, v_w_in=v_w_in, v_lb_logits=v_lb_logits, v_hgrn_norm_g=v_hgrn_norm_g, v_w_pool=v_w_pool, v_pool_scale=v_pool_scale, v_w_out=v_w_out, v_norm_x_g=v_norm_x_g, v_norm_mem_g=v_norm_mem_g, v_w_xq=v_w_xq, v_w_xk=v_w_xk, v_w_xv=v_w_xv, v_w_xo=v_w_xo, v_norm_ffn_g=v_norm_ffn_g, v_w_ff1=v_w_ff1, v_w_ff2=v_w_ff2, v_final_norm_g=v_final_norm_g)
    weights = {n: given[n] for n in TWIN_WEIGHTS}
    shared = {n: given[n] for n in SHARED_INPUTS}
    per_example = {n: given[n] for n in ['x', 'mem']}
    grad_fn = _jax.value_and_grad(_loss, argnums=(0, 1))

    def one_microbatch(ex, loss_target):
        ex = dict(ex)
        diff = ex.pop(TWIN_DIFF_INPUT)
        return grad_fn(weights, diff, {**shared, **ex}, loss_target)

    if N_MICROBATCH == 1:
        loss, (grad_w, grad_x) = one_microbatch(per_example, given["loss_target"])
    else:
        def body(carry, xs):
            loss_sum, grad_sum = carry
            l_k, (gw_k, gx_k) = one_microbatch(xs[0], xs[1])
            with _jax.named_scope("update"):
                return (loss_sum + l_k, _jax.tree.map(_jnp.add, grad_sum, gw_k)), gx_k

        init = (_jnp.zeros((), _jnp.float32), _jax.tree.map(_jnp.zeros_like, weights))
        (loss, grad_w), grad_x = _jax.lax.scan(body, init, (per_example, given["loss_target"]))
    with _jax.named_scope("update"):
        delta_w, new_m, new_v = {}, {}, {}
        for n in TWIN_WEIGHTS:
            delta_w[n], new_m[n], new_v[n] = _adamw(weights[n], grad_w[n], given["m_" + n], given["v_" + n])
    return (loss, grad_x, *[grad_w[n] for n in TWIN_WEIGHTS], *[delta_w[n] for n in TWIN_WEIGHTS],
            *[new_m[n] for n in TWIN_WEIGHTS], *[new_v[n] for n in TWIN_WEIGHTS])
```

```python
import functools

import jax
import jax.numpy as jnp
from jax import lax
from jax.experimental import pallas as pl
from jax.experimental.pallas import tpu as pltpu

F32 = jnp.float32
BF16 = jnp.bfloat16
HIGHEST = lax.Precision.HIGHEST
MESH = pl.DeviceIdType.MESH
ANY = pl.BlockSpec(memory_space=pl.ANY)
VMEM = pl.BlockSpec(memory_space=pltpu.VMEM)

D_MODEL = 1024
N_CHIPS = 4
HGRN_HEADS = 4
HEAD_DIM = 128
HGRN_WIDTH = HGRN_HEADS * HEAD_DIM
POOL_WINDOWS = (2, 4, 8, 16)
POOL_HALO = 16
SUB = 16
XATTN_HEADS = 4
XATTN_HEAD_DIM = 256
EPS = 1e-6
ADAM_LR, ADAM_B1, ADAM_B2, ADAM_EPS, ADAM_WD, ADAM_STEP = 0.001, 0.9, 0.999, 1e-08, 0.01, 10

V7X_VMEM_BYTES = 64 * 1024 * 1024
VMEM_LIMIT = V7X_VMEM_BYTES - 8 * 1024 * 1024

NN = (((1,), (0,)), ((), ()))
NT = (((1,), (1,)), ((), ()))
TN = (((0,), (0,)), ((), ()))

ROW_GMIX, ROW_GX, ROW_GMEM, ROW_GFFN, ROW_GFIN, ROW_LB_HGN, ROW_PSCALE, ROW_LOSS = range(8)


def _dot(a, b, dims=NN):
    return lax.dot_general(a, b, dims, preferred_element_type=F32)


def _sigmoid(x):
    return 1.0 / (1.0 + jnp.exp(-x))


def _rms_fwd(x, g):
    r = lax.rsqrt(jnp.mean(x * x, axis=-1, keepdims=True) + EPS)
    n = x * r
    return n * g, n, r


def _rms_bwd(dh, n, r, g):
    dn = dh * g
    dx = r * (dn - n * jnp.mean(dn * n, axis=-1, keepdims=True))
    return dx, jnp.sum(dh * n, axis=0, keepdims=True)


def _params(sem=None):
    return pltpu.CompilerParams(dimension_semantics=sem, vmem_limit_bytes=VMEM_LIMIT)


def _const(shape):
    nd = len(shape)
    return pl.BlockSpec(shape, lambda *_: (0,) * nd, pipeline_mode=pl.Buffered(1))


def _const_out(shape):
    nd = len(shape)
    return pl.BlockSpec(shape, lambda *_: (0,) * nd)


def _acc_rows(ref, t, rows):
    upd = jnp.concatenate(rows + [jnp.zeros((8 - len(rows), rows[0].shape[1]), F32)], axis=0)

    @pl.when(t == 0)
    def _():
        ref[...] = upd

    @pl.when(t > 0)
    def _():
        ref[...] = ref[...] + upd


def _in_proj(x, g, win_g, tm):
    s, d = x.shape
    nsh, _, wc = win_g.shape

    def body(x_ref, g_ref, w_ref, z_ref, h_ref):
        h, _, _ = _rms_fwd(x_ref[...], g_ref[...])
        hb = h.astype(BF16)
        h_ref[...] = hb
        for j in range(nsh):
            z_ref[:, j * wc:(j + 1) * wc] = _dot(hb, w_ref[j])

    return pl.pallas_call(
        body, name="in_proj", grid=(s // tm,),
        in_specs=[pl.BlockSpec((tm, d), lambda t: (t, 0)), _const((1, d)), _const((nsh, d, wc))],
        out_specs=[pl.BlockSpec((tm, nsh * wc), lambda t: (t, 0)), pl.BlockSpec((tm, d), lambda t: (t, 0))],
        out_shape=[jax.ShapeDtypeStruct((s, nsh * wc), F32), jax.ShapeDtypeStruct((s, d), BF16)],
        compiler_params=_params(("parallel",)),
    )(x, g, win_g)


def _lower_bound(l0, l1):
    m = jnp.maximum(l0, l1)
    e0, e1 = jnp.exp(l0 - m), jnp.exp(l1 - m)
    return e0 / (e0 + e1)


def _block_tri(n, upper):
    r = lax.broadcasted_iota(jnp.int32, (n, n), 0)
    c = lax.broadcasted_iota(jnp.int32, (n, n), 1)
    keep = (r // SUB == c // SUB) & ((c >= r) if upper else (c <= r))
    return keep.astype(F32)


def _hgrn_gates(qp, fp, lb):
    sq = _sigmoid(qp)
    sf = _sigmoid(fp)
    f = lb + (1.0 - lb) * sf
    return qp * sq, sq, f, sf


def _hgrn_fwd(z, l0, l1, gn, tc):
    s = z.shape[0]
    nsub = tc // SUB
    hd = HEAD_DIM

    def body(q_ref, f_ref, v_ref, g_ref, l0_ref, l1_ref, gn_ref, o_ref, oa_ref, st_ref, state, qs, ks, bs, os_):
        @pl.when(pl.program_id(1) == 0)
        def _():
            state[...] = jnp.zeros_like(state)

        lb = _lower_bound(l0_ref[...], l1_ref[...])
        q, _, f, _ = _hgrn_gates(q_ref[...], f_ref[...], lb)
        qs[...] = q
        ks[...] = 1.0 - f
        bs[...] = jnp.dot(_block_tri(tc, False), jnp.log(f), precision=HIGHEST, preferred_element_type=F32)
        rows = lax.broadcasted_iota(jnp.int32, (SUB, 1), 0)

        def step(i, carry):
            r0 = pl.multiple_of(i * SUB, SUB)
            q_ = qs[pl.ds(r0, SUB), :]
            k_ = ks[pl.ds(r0, SUB), :]
            b_ = bs[pl.ds(r0, SUB), :]
            v_ = v_ref[pl.ds(r0, SUB), :]
            st = state[...]
            st_ref[i] = st
            bl = b_[SUB - 1:SUB, :]
            o = _dot((q_ * jnp.exp(b_)).astype(BF16), st.astype(BF16), NT)
            for j in range(SUB):
                e = jnp.exp(jnp.minimum(b_ - b_[j:j + 1, :], 0.0))
                col = jnp.sum(q_ * e * k_[j:j + 1, :], axis=-1, keepdims=True)
                o = o + jnp.where(rows >= j, col, 0.0) * v_[j:j + 1, :]
            os_[pl.ds(r0, SUB), :] = o
            kt = (k_ * jnp.exp(bl - b_)).astype(BF16)
            state[...] = st * jnp.exp(bl) + _dot(v_.astype(BF16), kt, TN)
            return carry

        lax.fori_loop(0, nsub, step, 0)
        o = os_[...]
        o_ref[...] = o
        r = lax.rsqrt(jnp.mean(o * o, axis=-1, keepdims=True) + EPS)
        gp = g_ref[...]
        oa_ref[...] = (o * r * gn_ref[...] * (gp * _sigmoid(gp))).astype(BF16)

    col = lambda k: pl.BlockSpec((tc, hd), lambda h, t: (t, k * HGRN_HEADS + h))
    vec = pl.BlockSpec((None, 1, hd), lambda h, t: (h, 0, 0))
    return pl.pallas_call(
        body, name="hgrn_fwd", grid=(HGRN_HEADS, s // tc),
        in_specs=[col(0), col(1), col(2), col(3), vec, vec, vec],
        out_specs=[pl.BlockSpec((tc, hd), lambda h, t: (t, h)), pl.BlockSpec((tc, hd), lambda h, t: (t, h)),
                   pl.BlockSpec((None, nsub, hd, hd), lambda h, t: (h, t, 0, 0))],
        out_shape=[jax.ShapeDtypeStruct((s, HGRN_WIDTH), F32), jax.ShapeDtypeStruct((s, HGRN_WIDTH), BF16),
                   jax.ShapeDtypeStruct((HGRN_HEADS, s // SUB, hd, hd), F32)],
        scratch_shapes=[pltpu.VMEM((hd, hd), F32)] + [pltpu.VMEM((tc, hd), F32)] * 4,
        compiler_params=_params(("parallel", "arbitrary")),
    )(z, z, z, z, l0, l1, gn)


def _pooled(p, ext, tok0):
    tm = p.shape[0]
    tok = tok0 + lax.broadcasted_iota(jnp.int32, (tm, 1), 0)
    outs = []
    for g, w in enumerate(POOL_WINDOWS):
        acc = ext[:, g * HEAD_DIM:(g + 1) * HEAD_DIM]
        sh = 1
        while sh < w:
            acc = acc + pltpu.roll(acc, sh, axis=0)
            sh *= 2
        cnt = jnp.minimum(tok + 1, w).astype(F32)
        outs.append(acc[POOL_HALO:, :] / cnt - p[:, g * HEAD_DIM:(g + 1) * HEAD_DIM])
    return outs


def _pool_fwd(z, wp, scale, tm):
    s = z.shape[0]
    pw = len(POOL_WINDOWS) * HEAD_DIM
    nb = tm // POOL_HALO

    def body(p_ref, prev_ref, wp_ref, sc_ref, ob_ref):
        t = pl.program_id(0)
        p = p_ref[...]
        prev = jnp.where(t > 0, prev_ref[...], 0.0)
        pooled = _pooled(p, jnp.concatenate([prev, p], axis=0), t * tm)
        ys = [_dot(pooled[g].astype(BF16), wp_ref[g].astype(BF16)) for g in range(len(POOL_WINDOWS))]
        ob_ref[...] = (jnp.concatenate(ys, axis=1) * sc_ref[...]).astype(BF16)

    return pl.pallas_call(
        body, name="pool_fwd", grid=(s // tm,),
        in_specs=[pl.BlockSpec((tm, pw), lambda t: (t, 4)),
                  pl.BlockSpec((POOL_HALO, pw), lambda t: (jnp.maximum(t * nb - 1, 0), 4)),
                  _const(wp.shape), _const((1, pw))],
        out_specs=pl.BlockSpec((tm, pw), lambda t: (t, 0)),
        out_shape=jax.ShapeDtypeStruct((s, pw), BF16),
        compiler_params=_params(("parallel",)),
    )(z, z, wp, scale)


def _kv_proj(mem, g, slab_g):
    m, d = mem.shape
    rows = d // N_CHIPS

    def body(mem_ref, g_ref, wk_ref, wv_ref, xk_ref, xv_ref):
        hm, _, _ = _rms_fwd(mem_ref[...], g_ref[...])
        hb = hm.astype(BF16)
        xk_ref[...] = _dot(hb, wk_ref[...].reshape(d, d)).astype(BF16)
        xv_ref[...] = _dot(hb, wv_ref[...].reshape(d, d)).astype(BF16)

    blk = lambda k: pl.BlockSpec((N_CHIPS, rows, d), lambda i: (0, k, 0))
    return pl.pallas_call(
        body, name="kv_proj", grid=(1,),
        in_specs=[_const((m, d)), _const((1, d)), blk(2), blk(3)],
        out_specs=[_const_out((m, d)), _const_out((m, d))],
        out_shape=[jax.ShapeDtypeStruct((m, d), BF16)] * 2,
        compiler_params=_params(("arbitrary",)),
    )(mem, g, slab_g, slab_g)


def _softmax_rows(sc):
    e = jnp.exp(sc - jnp.max(sc, axis=-1, keepdims=True))
    return e / jnp.sum(e, axis=-1, keepdims=True)


def _mix_xattn_fwd(x, oa, ob, gx, slab_g, wo_g, xk, xv, tm):
    s, d = x.shape
    m = xk.shape[0]
    rows = d // N_CHIPS
    hw = oa.shape[1]
    e = XATTN_HEAD_DIM

    def body(x_ref, oa_ref, ob_ref, gx_ref, wout_ref, wq_ref, wo_ref, xk_ref, xv_ref,
             x1_ref, mixed_ref, hq_ref, xq_ref, att_ref, x2_ref):
        mixed = jnp.concatenate([oa_ref[...], ob_ref[...]], axis=1)
        mixed_ref[...] = mixed
        x1 = x_ref[...] + _dot(mixed, wout_ref[...].reshape(d, d))
        x1_ref[...] = x1
        hq, _, _ = _rms_fwd(x1, gx_ref[...])
        hqb = hq.astype(BF16)
        hq_ref[...] = hqb
        xq = _dot(hqb, wq_ref[...].reshape(d, d)).astype(BF16)
        xq_ref[...] = xq
        atts = []
        for h in range(XATTN_HEADS):
            cs = slice(h * e, (h + 1) * e)
            p = _softmax_rows(_dot(xq[:, cs], xk_ref[:, cs], NT) * (e ** -0.5))
            atts.append(_dot(p.astype(BF16), xv_ref[:, cs]).astype(BF16))
        att = jnp.concatenate(atts, axis=1)
        att_ref[...] = att
        for j in range(N_CHIPS):
            x2_ref[:, j * rows:(j + 1) * rows] = x1[:, j * rows:(j + 1) * rows] + _dot(att, wo_ref[j])

    tile = lambda w: pl.BlockSpec((tm, w), lambda t: (t, 0))
    blk = lambda k: pl.BlockSpec((N_CHIPS, rows, d), lambda t: (0, k, 0), pipeline_mode=pl.Buffered(1))
    return pl.pallas_call(
        body, name="mix_xattn_fwd", grid=(s // tm,),
        in_specs=[tile(d), tile(hw), tile(hw), _const((1, d)), blk(0), blk(1), _const(wo_g.shape),
                  _const((m, d)), _const((m, d))],
        out_specs=[tile(d)] * 6,
        out_shape=[jax.ShapeDtypeStruct((s, d), F32)] + [jax.ShapeDtypeStruct((s, d), BF16)] * 4
                  + [jax.ShapeDtypeStruct((s, d), F32)],
        compiler_params=_params(("parallel",)),
    )(x, oa, ob, gx, slab_g, slab_g, wo_g, xk, xv)


def _mlp_loss_fwd(x2, gffn, gfin, slab_g, target, tm):
    s, d = x2.shape
    wr = slab_g.shape[1] // 3

    def body(x2_ref, gffn_ref, gfin_ref, w1_ref, w2_ref, tg_ref, a_ref, hf_ref, dx3_ref, dx3b_ref, st_ref):
        x2v = x2_ref[...]
        hf, _, _ = _rms_fwd(x2v, gffn_ref[...])
        hfb = hf.astype(BF16)
        hf_ref[...] = hfb
        acc = x2v
        for j in range(N_CHIPS):
            a = _dot(hfb, w1_ref[j])
            a_ref[:, j * wr:(j + 1) * wr] = a
            r = jnp.maximum(a, 0.0)
            acc = acc + _dot((r * r).astype(BF16), w2_ref[j])
        gf = gfin_ref[...]
        y, n, r3 = _rms_fwd(acc, gf)
        err = y - tg_ref[...]
        loss = 0.5 * jnp.sum(jnp.sum(err * err, axis=-1, keepdims=True) * (1.0 / d), axis=0, keepdims=True)
        dy = err * (1.0 / d)
        dx3, dgf = _rms_bwd(dy, n, r3, gf)
        dx3_ref[...] = dx3
        dx3b_ref[...] = dx3.astype(BF16)
        _acc_rows(st_ref, pl.program_id(0), [dgf, jnp.broadcast_to(loss, (1, d))])

    tile = lambda w: pl.BlockSpec((tm, w), lambda t: (t, 0))
    blk = lambda k: pl.BlockSpec((N_CHIPS, wr, d), lambda t: (0, k, 0), pipeline_mode=pl.Buffered(1))
    return pl.pallas_call(
        body, name="mlp_loss_fwd", grid=(s // tm,),
        in_specs=[tile(d), _const((1, d)), _const((1, d)), blk(1), blk(2), tile(d)],
        out_specs=[tile(N_CHIPS * wr), tile(d), tile(d), tile(d), _const_out((8, d))],
        out_shape=[jax.ShapeDtypeStruct((s, N_CHIPS * wr), F32), jax.ShapeDtypeStruct((s, d), BF16),
                   jax.ShapeDtypeStruct((s, d), F32), jax.ShapeDtypeStruct((s, d), BF16),
                   jax.ShapeDtypeStruct((8, d), F32)],
        compiler_params=_params(("arbitrary",)),
    )(x2, gffn, gfin, slab_g, slab_g, target)


def _mlp_bwd(dx3, dx3b, a, x2, gffn, slab_g, tm):
    s, d = x2.shape
    wr = slab_g.shape[1] // 3

    def body(dx3_ref, dx3b_ref, a_ref, x2_ref, g_ref, w1_ref, w2_ref, da_ref, u_ref, dx2_ref, dx2b_ref, st_ref):
        dyb = dx3b_ref[...]
        dhf = jnp.zeros((tm, d), F32)
        for j in range(N_CHIPS):
            r = jnp.maximum(a_ref[:, j * wr:(j + 1) * wr], 0.0)
            da = (_dot(dyb, w2_ref[j], NT) * (2.0 * r)).astype(BF16)
            da_ref[:, j * wr:(j + 1) * wr] = da
            u_ref[:, j * wr:(j + 1) * wr] = (r * r).astype(BF16)
            dhf = dhf + _dot(da, w1_ref[j], NT)
        g = g_ref[...]
        _, n, r2 = _rms_fwd(x2_ref[...], g)
        dxn, dg = _rms_bwd(dhf, n, r2, g)
        dx2 = dx3_ref[...] + dxn
        dx2_ref[...] = dx2
        dx2b_ref[...] = dx2.astype(BF16)
        _acc_rows(st_ref, pl.program_id(0), [dg])

    tile = lambda w: pl.BlockSpec((tm, w), lambda t: (t, 0))
    blk = lambda k: pl.BlockSpec((N_CHIPS, wr, d), lambda t: (0, k, 0), pipeline_mode=pl.Buffered(1))
    nf = N_CHIPS * wr
    return pl.pallas_call(
        body, name="mlp_bwd", grid=(s // tm,),
        in_specs=[tile(d), tile(d), tile(nf), tile(d), _const((1, d)), blk(1), blk(2)],
        out_specs=[tile(nf), tile(nf), tile(d), tile(d), _const_out((8, d))],
        out_shape=[jax.ShapeDtypeStruct((s, nf), BF16), jax.ShapeDtypeStruct((s, nf), BF16),
                   jax.ShapeDtypeStruct((s, d), F32), jax.ShapeDtypeStruct((s, d), BF16),
                   jax.ShapeDtypeStruct((8, d), F32)],
        compiler_params=_params(("arbitrary",)),
    )(dx3, dx3b, a, x2, gffn, slab_g, slab_g)


def _xattn_mix_bwd(dx2, x1, xq, xk, xv, gx, slab_g, wo_g, tm):
    s, d = x1.shape
    m = xk.shape[0]
    rows = d // N_CHIPS
    e = XATTN_HEAD_DIM

    def body(dx2_ref, x1_ref, xq_ref, xk_ref, xv_ref, gx_ref, wout_ref, wq_ref, wo_ref,
             dx1_ref, dx1b_ref, dxq_ref, dmix_ref, dxk_ref, dxv_ref, st_ref):
        t = pl.program_id(0)
        dx2 = dx2_ref[...]
        dx2b = dx2.astype(BF16)
        datt = jnp.zeros((tm, d), F32)
        for j in range(N_CHIPS):
            datt = datt + _dot(dx2b[:, j * rows:(j + 1) * rows], wo_ref[j], NT)
        dattb = datt.astype(BF16)
        dxqs, dxks, dxvs = [], [], []
        for h in range(XATTN_HEADS):
            cs = slice(h * e, (h + 1) * e)
            xq_h, xk_h, xv_h = xq_ref[:, cs], xk_ref[:, cs], xv_ref[:, cs]
            p = _softmax_rows(_dot(xq_h, xk_h, NT) * (e ** -0.5))
            dp = _dot(dattb[:, cs], xv_h, NT)
            ds = (p * (dp - jnp.sum(dp * p, axis=-1, keepdims=True)) * (e ** -0.5)).astype(BF16)
            dxqs.append(_dot(ds, xk_h).astype(BF16))
            dxks.append(_dot(ds, xq_h, TN))
            dxvs.append(_dot(p.astype(BF16), dattb[:, cs], TN))
        dxq = jnp.concatenate(dxqs, axis=1)
        dxq_ref[...] = dxq
        dxk = jnp.concatenate(dxks, axis=1)
        dxv = jnp.concatenate(dxvs, axis=1)

        @pl.when(t == 0)
        def _():
            dxk_ref[...] = dxk
            dxv_ref[...] = dxv

        @pl.when(t > 0)
        def _():
            dxk_ref[...] = dxk_ref[...] + dxk
            dxv_ref[...] = dxv_ref[...] + dxv

        dhq = jnp.concatenate([_dot(dxq, wq_ref[j], NT) for j in range(N_CHIPS)], axis=1)
        g = gx_ref[...]
        _, n, r1 = _rms_fwd(x1_ref[...], g)
        dxn, dg = _rms_bwd(dhq, n, r1, g)
        dx1 = dx2 + dxn
        dx1_ref[...] = dx1
        dx1b = dx1.astype(BF16)
        dx1b_ref[...] = dx1b
        for j in range(N_CHIPS):
            dmix_ref[:, j * rows:(j + 1) * rows] = _dot(dx1b, wout_ref[j], NT)
        _acc_rows(st_ref, t, [dg])

    tile = lambda: pl.BlockSpec((tm, d), lambda t: (t, 0))
    blk = lambda k: pl.BlockSpec((N_CHIPS, rows, d), lambda t: (0, k, 0), pipeline_mode=pl.Buffered(1))
    return pl.pallas_call(
        body, name="xattn_mix_bwd", grid=(s // tm,),
        in_specs=[tile(), tile(), tile(), _const((m, d)), _const((m, d)), _const((1, d)), blk(0), blk(1),
                  _const(wo_g.shape)],
        out_specs=[tile(), tile(), tile(), tile(), _const_out((m, d)), _const_out((m, d)), _const_out((8, d))],
        out_shape=[jax.ShapeDtypeStruct((s, d), F32), jax.ShapeDtypeStruct((s, d), BF16),
                   jax.ShapeDtypeStruct((s, d), BF16), jax.ShapeDtypeStruct((s, d), F32),
                   jax.ShapeDtypeStruct((m, d), F32), jax.ShapeDtypeStruct((m, d), F32),
                   jax.ShapeDtypeStruct((8, d), F32)],
        compiler_params=_params(("arbitrary",)),
    )(dx2, x1, xq, xk, xv, gx, slab_g, slab_g, wo_g)


def _kv_bwd(mem, g, dxk, dxv, slab_g):
    m, d = mem.shape
    rows = d // N_CHIPS

    def body(mem_ref, g_ref, dxk_ref, dxv_ref, wk_ref, wv_ref, dwk_ref, dwv_ref, st_ref):
        gv = g_ref[...]
        hm, n, _ = _rms_fwd(mem_ref[...], gv)
        hb = hm.astype(BF16)
        dkb = dxk_ref[...].astype(BF16)
        dvb = dxv_ref[...].astype(BF16)
        dhm = []
        for j in range(N_CHIPS):
            hj = hb[:, j * rows:(j + 1) * rows]
            dwk_ref[j] = _dot(hj, dkb, TN)
            dwv_ref[j] = _dot(hj, dvb, TN)
            dhm.append(_dot(dkb, wk_ref[j], NT) + _dot(dvb, wv_ref[j], NT))
        dg = jnp.sum(jnp.concatenate(dhm, axis=1) * n, axis=0, keepdims=True)
        st_ref[...] = jnp.concatenate([dg, jnp.zeros((7, d), F32)], axis=0)

    blk = lambda k: pl.BlockSpec((N_CHIPS, rows, d), lambda i: (0, k, 0))
    return pl.pallas_call(
        body, name="kv_bwd", grid=(1,),
        in_specs=[_const((m, d)), _const((1, d)), _const((m, d)), _const((m, d)), blk(2), blk(3)],
        out_specs=[_const_out((N_CHIPS, rows, d)), _const_out((N_CHIPS, rows, d)), _const_out((8, d))],
        out_shape=[jax.ShapeDtypeStruct((N_CHIPS, rows, d), F32)] * 2 + [jax.ShapeDtypeStruct((8, d), F32)],
        compiler_params=_params(("arbitrary",)),
    )(mem, g, dxk, dxv, slab_g, slab_g)


def _pool_bwd(z, dmix, wp, scale, tm):
    s = z.shape[0]
    ng = len(POOL_WINDOWS)
    pw = ng * HEAD_DIM
    nb = tm // POOL_HALO
    nt = s // tm
    n_ext = tm + POOL_HALO

    def body(p_ref, prev_ref, dm_ref, dmn_ref, wp_ref, sc_ref, dp_ref, dwp_ref, st_ref):
        t = pl.program_id(0)
        p = p_ref[...]
        prev = jnp.where(t > 0, prev_ref[...], 0.0)
        pooled = _pooled(p, jnp.concatenate([prev, p], axis=0), t * tm)
        dm = dm_ref[...]
        dme = jnp.concatenate([dm, jnp.where(t < nt - 1, dmn_ref[...], 0.0)], axis=0) * sc_ref[...]
        tok = t * tm + lax.broadcasted_iota(jnp.int32, (n_ext, 1), 0)
        dsc, dps, dwps = [], [], []
        for g, w in enumerate(POOL_WINDOWS):
            cs = slice(g * HEAD_DIM, (g + 1) * HEAD_DIM)
            wpb = wp_ref[g].astype(BF16)
            pb = pooled[g].astype(BF16)
            dsc.append(jnp.sum(dm[:, cs] * _dot(pb, wpb), axis=0, keepdims=True))
            dye = dme[:, cs].astype(BF16)
            dwps.append(_dot(pb, dye[:tm], TN))
            dpe = _dot(dye, wpb, NT)
            acc = dpe / jnp.minimum(tok + 1, w).astype(F32)
            sh = 1
            while sh < w:
                acc = acc + pltpu.roll(acc, n_ext - sh, axis=0)
                sh *= 2
            dps.append(acc[:tm] - dpe[:tm])
        dp_ref[...] = jnp.concatenate(dps, axis=1)
        dsc_row = jnp.concatenate(dsc, axis=1)

        @pl.when(t == 0)
        def _():
            for g in range(ng):
                dwp_ref[g] = dwps[g]

        @pl.when(t > 0)
        def _():
            for g in range(ng):
                dwp_ref[g] = dwp_ref[g] + dwps[g]

        _acc_rows(st_ref, t, [dsc_row])

    return pl.pallas_call(
        body, name="pool_bwd", grid=(nt,),
        in_specs=[pl.BlockSpec((tm, pw), lambda t: (t, 4)),
                  pl.BlockSpec((POOL_HALO, pw), lambda t: (jnp.maximum(t * nb - 1, 0), 4)),
                  pl.BlockSpec((tm, pw), lambda t: (t, 1)),
                  pl.BlockSpec((POOL_HALO, pw), lambda t: (jnp.minimum((t + 1) * nb, s // POOL_HALO - 1), 1)),
                  _const(wp.shape), _const((1, pw))],
        out_specs=[pl.BlockSpec((tm, pw), lambda t: (t, 0)), _const_out(wp.shape), _const_out((8, pw))],
        out_shape=[jax.ShapeDtypeStruct((s, pw), F32), jax.ShapeDtypeStruct(wp.shape, F32),
                   jax.ShapeDtypeStruct((8, pw), F32)],
        compiler_params=_params(("arbitrary",)),
    )(z, z, dmix, dmix, wp, scale)


def _hgrn_bwd(z, o, dmix, st, l0, l1, gn, tc):
    s = z.shape[0]
    nsub = tc // SUB
    nt = s // tc
    hd = HEAD_DIM

    def body(q_ref, f_ref, v_ref, g_ref, l0_ref, l1_ref, gn_ref, o_ref, dm_ref, st_ref,
             dq_ref, df_ref, di_ref, dg_ref, stat_ref, dstate, qs, ks, bs, dos, dqs, dks, dbs):
        t = pl.program_id(1)

        @pl.when(t == 0)
        def _():
            dstate[...] = jnp.zeros_like(dstate)

        lb = _lower_bound(l0_ref[...], l1_ref[...])
        qp = q_ref[...]
        q, sq, f, sf = _hgrn_gates(qp, f_ref[...], lb)
        qs[...] = q
        ks[...] = 1.0 - f
        bs[...] = jnp.dot(_block_tri(tc, False), jnp.log(f), precision=HIGHEST, preferred_element_type=F32)

        o = o_ref[...]
        r = lax.rsqrt(jnp.mean(o * o, axis=-1, keepdims=True) + EPS)
        n = o * r
        gnv = gn_ref[...]
        gp = g_ref[...]
        sg = _sigmoid(gp)
        dm = dm_ref[...]
        dg_ref[...] = dm * (n * gnv) * (sg * (1.0 + gp * (1.0 - sg)))
        don = dm * (gp * sg)
        dgn = jnp.sum(don * n, axis=0, keepdims=True)
        dn = don * gnv
        dos[...] = r * (dn - n * jnp.mean(dn * n, axis=-1, keepdims=True))
        rows = lax.broadcasted_iota(jnp.int32, (SUB, 1), 0)

        def step(i, carry):
            ii = nsub - 1 - i
            r0 = pl.multiple_of(ii * SUB, SUB)
            q_ = qs[pl.ds(r0, SUB), :]
            k_ = ks[pl.ds(r0, SUB), :]
            b_ = bs[pl.ds(r0, SUB), :]
            v_ = v_ref[pl.ds(r0, SUB), :]
            do_ = dos[pl.ds(r0, SUB), :]
            stp = st_ref[ii]
            dst = dstate[...]
            bl = b_[SUB - 1:SUB, :]
            eb = jnp.exp(b_)
            ekl = jnp.exp(bl - b_)
            ebl = jnp.exp(bl)
            dob = do_.astype(BF16)
            dstb = dst.astype(BF16)
            kt = k_ * ekl
            dq = _dot(dob, stp.astype(BF16)) * eb
            dkt = _dot(v_.astype(BF16), dstb)
            dk = dkt * ekl
            dv = _dot(kt.astype(BF16), dstb, NT)
            extra = jnp.sum(kt * dkt, axis=0, keepdims=True) + ebl * jnp.sum(stp * dst, axis=0, keepdims=True)
            for j in range(SUB):
                e = jnp.exp(jnp.minimum(b_ - b_[j:j + 1, :], 0.0))
                pe = q_ * e
                kj = k_[j:j + 1, :]
                keep = rows >= j
                acol = jnp.where(keep, jnp.sum(pe * kj, axis=-1, keepdims=True), 0.0)
                dacol = jnp.where(keep, jnp.sum(do_ * v_[j:j + 1, :], axis=-1, keepdims=True), 0.0)
                dq = dq + dacol * (e * kj)
                at_j = rows == j
                dk = dk + jnp.where(at_j, jnp.sum(dacol * pe, axis=0, keepdims=True), 0.0)
                dv = dv + jnp.where(at_j, jnp.sum(acol * do_, axis=0, keepdims=True), 0.0)
            dqs[pl.ds(r0, SUB), :] = dq
            dks[pl.ds(r0, SUB), :] = dk
            di_ref[pl.ds(r0, SUB), :] = dv
            dbs[pl.ds(r0, SUB), :] = q_ * dq - k_ * dk + jnp.where(rows == SUB - 1, extra, 0.0)
            dstate[...] = dst * ebl + _dot(dob, (q_ * eb).astype(BF16), TN)
            return carry

        lax.fori_loop(0, nsub, step, 0)
        dlf = jnp.dot(_block_tri(tc, True), dbs[...], precision=HIGHEST, preferred_element_type=F32)
        dfv = dlf / f - dks[...]
        df_ref[...] = dfv * (1.0 - lb) * sf * (1.0 - sf)
        dlb = jnp.sum(dfv * (1.0 - sf), axis=0, keepdims=True)
        dq_ref[...] = dqs[...] * (sq * (1.0 + qp * (1.0 - sq)))
        _acc_rows(stat_ref, t, [dgn, dlb])

    rev = lambda t: nt - 1 - t
    col = lambda k: pl.BlockSpec((tc, hd), lambda h, t: (rev(t), k * HGRN_HEADS + h))
    vec = pl.BlockSpec((None, 1, hd), lambda h, t: (h, 0, 0))
    head = pl.BlockSpec((tc, hd), lambda h, t: (rev(t), h))
    return pl.pallas_call(
        body, name="hgrn_bwd", grid=(HGRN_HEADS, nt),
        in_specs=[col(0), col(1), col(2), col(3), vec, vec, vec, head, head,
                  pl.BlockSpec((None, nsub, hd, hd), lambda h, t: (h, rev(t), 0, 0))],
        out_specs=[head, head, head, head, pl.BlockSpec((None, 8, hd), lambda h, t: (h, 0, 0))],
        out_shape=[jax.ShapeDtypeStruct((s, HGRN_WIDTH), F32)] * 4 + [jax.ShapeDtypeStruct((HGRN_HEADS, 8, hd), F32)],
        scratch_shapes=[pltpu.VMEM((hd, hd), F32)] + [pltpu.VMEM((tc, hd), F32)] * 7,
        compiler_params=_params(("parallel", "arbitrary")),
    )(z, z, z, z, l0, l1, gn, o, dmix, st)


def _in_bwd(dparts, dx1, x, g, win_g, tm):
    s, d = x.shape
    nsh, _, wc = win_g.shape
    pw = dparts[0].shape[1]

    def body(dq_ref, df_ref, di_ref, dg_ref, dp_ref, dx1_ref, x_ref, g_ref, w_ref, gx_ref, dz_ref, st_ref):
        dz = jnp.concatenate([dq_ref[...], df_ref[...], di_ref[...], dg_ref[...], dp_ref[...]], axis=1).astype(BF16)
        dz_ref[...] = dz
        dh = jnp.zeros((tm, d), F32)
        for j in range(nsh):
            dh = dh + _dot(dz[:, j * wc:(j + 1) * wc], w_ref[j], NT)
        gv = g_ref[...]
        _, n, r = _rms_fwd(x_ref[...], gv)
        dxn, dg = _rms_bwd(dh, n, r, gv)
        gx_ref[...] = dx1_ref[...] + dxn
        _acc_rows(st_ref, pl.program_id(0), [dg])

    tile = lambda w: pl.BlockSpec((tm, w), lambda t: (t, 0))
    return pl.pallas_call(
        body, name="in_bwd", grid=(s // tm,),
        in_specs=[tile(pw)] * 5 + [tile(d), tile(d), _const((1, d)), _const(win_g.shape)],
        out_specs=[tile(d), tile(nsh * wc), _const_out((8, d))],
        out_shape=[jax.ShapeDtypeStruct((s, d), F32), jax.ShapeDtypeStruct((s, nsh * wc), BF16),
                   jax.ShapeDtypeStruct((8, d), F32)],
        compiler_params=_params(("arbitrary",)),
    )(*dparts, dx1, x, g, win_g)


def _tn_grad(name, a, b, out_rows, out_cols, a_sharded, tr, tc):
    s = a.shape[0]
    nr, nc = out_rows // tr, out_cols // tc

    def body(a_ref, b_ref, o_ref):
        o_ref[...] = _dot(a_ref[...], b_ref[...], TN)

    a_map = (lambda j, i, k: (0, j * nr + i)) if a_sharded else (lambda j, i, k: (0, i))
    b_map = (lambda j, i, k: (0, k)) if a_sharded else (lambda j, i, k: (0, j * nc + k))
    return pl.pallas_call(
        body, name=name, grid=(N_CHIPS, nr, nc),
        in_specs=[pl.BlockSpec((s, tr), a_map), pl.BlockSpec((s, tc), b_map)],
        out_specs=pl.BlockSpec((None, tr, tc), lambda j, i, k: (j, i, k)),
        out_shape=jax.ShapeDtypeStruct((N_CHIPS, out_rows, out_cols), F32),
        compiler_params=_params(("parallel", "parallel", "parallel")),
    )(a, b)


def _local_step(x, mem, target, small, slab_g, win_g, wo_g):
    d = x.shape[1]
    l0 = small["lb_logits"][0].reshape(HGRN_HEADS, 1, HEAD_DIM)
    l1 = small["lb_logits"][1].reshape(HGRN_HEADS, 1, HEAD_DIM)
    gn = small["hgrn_norm_g"].reshape(HGRN_HEADS, 1, HEAD_DIM)
    wp = small["w_pool"].reshape(len(POOL_WINDOWS), HEAD_DIM, HEAD_DIM)
    psc = small["pool_scale"].reshape(1, -1)
    gmix, gx, gmem, gffn = (small[k].reshape(1, d) for k in ("norm_mix_g", "norm_x_g", "norm_mem_g", "norm_ffn_g"))
    gfin = small["final_norm_g"].reshape(1, d)

    z, h = _in_proj(x, gmix, win_g, tm=512)
    o, oa, st = _hgrn_fwd(z, l0, l1, gn, tc=256)
    ob = _pool_fwd(z, wp, psc, tm=512)
    xk, xv = _kv_proj(mem, gmem, slab_g)
    x1, mixed, hq, xq, att, x2 = _mix_xattn_fwd(x, oa, ob, gx, slab_g, wo_g, xk, xv, tm=256)
    a, hf, dx3, dx3b, st_loss = _mlp_loss_fwd(x2, gffn, gfin, slab_g, target, tm=256)

    da, u, dx2, dx2b, st_ffn = _mlp_bwd(dx3, dx3b, a, x2, gffn, slab_g, tm=256)
    dw_ff1 = _tn_grad("dw_ff1", hf, da, d, d, False, 512, 512)
    dw_ff2 = _tn_grad("dw_ff2", u, dx3b, d, d, True, 512, 512)
    dx1, dx1b, dxq, dmix, dxk, dxv, st_x = _xattn_mix_bwd(dx2, x1, xq, xk, xv, gx, slab_g, wo_g, tm=256)
    dw_xo = _tn_grad("dw_xo", att, dx2b, d, d // N_CHIPS, False, 512, 256)
    dw_xq = _tn_grad("dw_xq", hq, dxq, d // N_CHIPS, d, True, 256, 512)
    dw_out = _tn_grad("dw_out", mixed, dx1b, d // N_CHIPS, d, True, 256, 512)
    dw_xk, dw_xv, st_mem = _kv_bwd(mem, gmem, dxk, dxv, slab_g)
    dp, d_wp, st_pool = _pool_bwd(z, dmix, wp, psc, tm=512)
    dq, df, di, dg, st_hgrn = _hgrn_bwd(z, o, dmix, st, l0, l1, gn, tc=256)
    grad_x, dz, st_mix = _in_bwd([dq, df, di, dg, dp], dx1, x, gmix, win_g, tm=256)
    dw_in = _tn_grad("dw_in", h, dz, d, win_g.shape[2], False, 512, win_g.shape[2])
    return dict(
        loss=st_loss[1, 0], grad_x=grad_x, dw_in=dw_in, dw_out=dw_out, dw_xq=dw_xq, dw_xk=dw_xk, dw_xv=dw_xv,
        dw_xo=dw_xo, dw_ff1=dw_ff1, dw_ff2=dw_ff2, d_w_pool=d_wp,
        d_norm_mix_g=st_mix[0], d_norm_x_g=st_x[0], d_norm_mem_g=st_mem[0], d_norm_ffn_g=st_ffn[0],
        d_final_norm_g=st_loss[0], d_hgrn_norm_g=st_hgrn[:, 0], d_lb=st_hgrn[:, 1], d_pool_scale=st_pool[0],
        stats=dict(mix=st_mix, x=st_x, mem=st_mem, ffn=st_ffn, loss=st_loss, hgrn=st_hgrn, pool=st_pool))


def _place():
    x, y, c = lax.axis_index("x"), lax.axis_index("y"), lax.axis_index("c")
    return x, y, c, [(x, 1 - y), (1 - x, y), (1 - x, 1 - y)]


def _rcopy(src, dst, ssem, rsem, dev):
    return pltpu.make_async_remote_copy(src_ref=src, dst_ref=dst, send_sem=ssem, recv_sem=rsem,
                                        device_id=dev, device_id_type=MESH)


def _all_gather_weights(shards):
    n = len(shards)

    def body(*refs):
        ins, outs = refs[:n], refs[n:2 * n]
        lsem, ssem, rsem, fsem, frsem = refs[2 * n:]
        x, y, c, peers = _place()
        chip = 2 * x + y
        sib = (x, y, 1 - c)

        def half(a, who, hc):
            hr = shards[a].shape[0] // 2
            return outs[a].at[who, pl.ds(hc * hr, hr), :]

        local = [pltpu.make_async_copy(ins[a], outs[a].at[chip], lsem.at[a]) for a in range(n)]
        for cp in local:
            cp.start()
        sends = []
        for a in range(n):
            hr = shards[a].shape[0] // 2
            for r, (px, py) in enumerate(peers):
                cp = _rcopy(ins[a].at[pl.ds(c * hr, hr), :], half(a, chip, c), ssem.at[a, r], rsem.at[a, r], (px, py, c))
                cp.start()
                sends.append(cp)
        fwds = []
        for a in range(n):
            for r, (px, py) in enumerate(peers):
                got = half(a, 2 * px + py, c)
                _rcopy(got, got, ssem.at[a, r], rsem.at[a, r], (px, py, c)).wait_recv()
                cp = _rcopy(got, got, fsem.at[a, r], frsem.at[a, r], sib)
                cp.start()
                fwds.append(cp)
        for a in range(n):
            for r, (px, py) in enumerate(peers):
                got = half(a, 2 * px + py, 1 - c)
                _rcopy(got, got, fsem.at[a, r], frsem.at[a, r], sib).wait_recv()
        for cp in sends + fwds:
            cp.wait_send()
        for cp in local:
            cp.wait()

    return pl.pallas_call(
        body, name="all_gather_weights",
        in_specs=[ANY] * n, out_specs=[ANY] * n,
        out_shape=[jax.ShapeDtypeStruct((N_CHIPS,) + w.shape, w.dtype) for w in shards],
        scratch_shapes=[pltpu.SemaphoreType.DMA((n,))] + [pltpu.SemaphoreType.DMA((n, 3))] * 4,
    )(*shards)


def _grad_pair_exchange(gs):
    n = len(gs)

    def body(*refs):
        ins, outs, ssem, rsem = refs[:n], refs[n:2 * n], refs[2 * n], refs[2 * n + 1]
        x, y, c, _ = _place()
        cps = [_rcopy(ins[a].at[:, 1 - c], outs[a], ssem.at[a], rsem.at[a], (x, y, 1 - c)) for a in range(n)]
        for cp in cps:
            cp.start()
        for cp in cps:
            cp.wait()

    return pl.pallas_call(
        body, name="grad_pair_exchange",
        in_specs=[ANY] * n, out_specs=[ANY] * n,
        out_shape=[jax.ShapeDtypeStruct((g.shape[0],) + g.shape[2:], g.dtype) for g in gs],
        scratch_shapes=[pltpu.SemaphoreType.DMA((n,))] * 2,
    )(*gs)


def _grad_pair_add(name, g, got, idx, tr):
    _, _, hr, cc = g.shape

    def body(idx_ref, g0, g1, g2, g3, r0, r1, r2, r3, keep_ref, send_ref):
        keep_ref[...] = g0[...] + r0[...]
        for q, (gq, rq) in enumerate(((g1, r1), (g2, r2), (g3, r3))):
            send_ref[q] = (gq[...] + rq[...]).astype(BF16)

    gspec = lambda q: pl.BlockSpec((None, None, tr, cc), lambda i, idx: (idx[1 + q], idx[0], i, 0))
    rspec = lambda q: pl.BlockSpec((None, tr, cc), lambda i, idx: (idx[1 + q], i, 0))
    return pl.pallas_call(
        body, name=name,
        grid_spec=pltpu.PrefetchScalarGridSpec(
            num_scalar_prefetch=1, grid=(hr // tr,),
            in_specs=[gspec(q) for q in range(4)] + [rspec(q) for q in range(4)],
            out_specs=[pl.BlockSpec((tr, cc), lambda i, idx: (i, 0)), pl.BlockSpec((3, tr, cc), lambda i, idx: (0, i, 0))]),
        out_shape=[jax.ShapeDtypeStruct((hr, cc), F32), jax.ShapeDtypeStruct((3, hr, cc), BF16)],
        compiler_params=_params(("parallel",)),
    )(idx, g, g, g, g, got, got, got, got)


def _grad_chip_exchange(sends):
    n = len(sends)

    def body(*refs):
        ins, outs, ssem, rsem = refs[:n], refs[n:2 * n], refs[2 * n], refs[2 * n + 1]
        _, _, c, peers = _place()
        cps = [_rcopy(ins[a].at[r], outs[a].at[r], ssem.at[a, r], rsem.at[a, r], (px, py, c))
               for a in range(n) for r, (px, py) in enumerate(peers)]
        for cp in cps:
            cp.start()
        for cp in cps:
            cp.wait()

    return pl.pallas_call(
        body, name="grad_chip_exchange",
        in_specs=[ANY] * n, out_specs=[ANY] * n,
        out_shape=[jax.ShapeDtypeStruct(g.shape, g.dtype) for g in sends],
        scratch_shapes=[pltpu.SemaphoreType.DMA((n, 3))] * 2,
    )(*sends)


def _grad_chip_add(name, keep, got, tr):
    hr, cc = keep.shape

    def body(k_ref, g_ref, o_ref):
        o_ref[...] = ((k_ref[...] + g_ref[0].astype(F32)) + g_ref[1].astype(F32)) + g_ref[2].astype(F32)

    return pl.pallas_call(
        body, name=name, grid=(hr // tr,),
        in_specs=[pl.BlockSpec((tr, cc), lambda i: (i, 0)), pl.BlockSpec((3, tr, cc), lambda i: (0, i, 0))],
        out_specs=pl.BlockSpec((tr, cc), lambda i: (i, 0)),
        out_shape=jax.ShapeDtypeStruct((hr, cc), F32),
        compiler_params=_params(("parallel",)),
    )(keep, got)


def _grad_half_exchange(ts):
    n = len(ts)

    def body(*refs):
        ins, outs, lsem, ssem, rsem = refs[:n], refs[n:2 * n], refs[2 * n], refs[2 * n + 1], refs[2 * n + 2]
        x, y, c, _ = _place()
        local = [pltpu.make_async_copy(ins[a], outs[a].at[c], lsem.at[a]) for a in range(n)]
        cps = [_rcopy(ins[a], outs[a].at[c], ssem.at[a], rsem.at[a], (x, y, 1 - c)) for a in range(n)]
        for cp in local + cps:
            cp.start()
        for a in range(n):
            cps[a].wait_send()
            _rcopy(ins[a], outs[a].at[1 - c], ssem.at[a], rsem.at[a], (x, y, 1 - c)).wait_recv()
        for cp in local:
            cp.wait()

    return pl.pallas_call(
        body, name="grad_half_exchange",
        in_specs=[ANY] * n, out_specs=[ANY] * n,
        out_shape=[jax.ShapeDtypeStruct((2,) + t.shape, t.dtype) for t in ts],
        scratch_shapes=[pltpu.SemaphoreType.DMA((n,))] * 3,
    )(*ts)


def _small_allreduce(stats, d_wp):
    d = D_MODEL
    half = d // 2
    wps = d_wp.shape

    def body(mix_ref, x_ref, mem_ref, ffn_ref, loss_ref, hg_ref, pool_ref, wp_ref, slab_out, wp_out,
             slab_buf, wp_buf, sib_s, sib_w, ssem, rsem):
        x, y, c, peers = _place()
        chip = 2 * x + y
        sib = (x, y, 1 - c)
        hgn = jnp.concatenate([hg_ref[h, 0:1, :] for h in range(HGRN_HEADS)], axis=1)
        dlb = jnp.concatenate([hg_ref[h, 1:2, :] for h in range(HGRN_HEADS)], axis=1)
        slab_buf[0] = jnp.concatenate([
            mix_ref[0:1, :], x_ref[0:1, :], mem_ref[0:1, :], ffn_ref[0:1, :], loss_ref[0:1, :],
            jnp.concatenate([dlb, hgn], axis=1),
            jnp.concatenate([pool_ref[0:1, :], jnp.zeros((1, half), F32)], axis=1),
            loss_ref[1:2, :]], axis=0)
        wp_buf[0] = wp_ref[...]
        pair = [_rcopy(slab_buf.at[0], sib_s, ssem.at[0], rsem.at[0], sib),
                _rcopy(wp_buf.at[0], sib_w, ssem.at[1], rsem.at[1], sib)]
        for cp in pair:
            cp.start()
        for cp in pair:
            cp.wait()
        slab_buf[0] = slab_buf[0] + sib_s[...]
        wp_buf[0] = wp_buf[0] + sib_w[...]
        cps = []
        for r, (px, py) in enumerate(peers):
            cps.append(_rcopy(slab_buf.at[0], slab_buf.at[r + 1], ssem.at[2 + 2 * r], rsem.at[2 + 2 * r], (px, py, c)))
            cps.append(_rcopy(wp_buf.at[0], wp_buf.at[r + 1], ssem.at[3 + 2 * r], rsem.at[3 + 2 * r], (px, py, c)))
        for cp in cps:
            cp.start()
        for cp in cps:
            cp.wait()
        tot_s, tot_w = slab_buf[chip], wp_buf[chip]
        for j in range(1, N_CHIPS):
            tot_s = tot_s + slab_buf[jnp.bitwise_xor(j, chip)]
            tot_w = tot_w + wp_buf[jnp.bitwise_xor(j, chip)]
        slab_out[...] = tot_s
        wp_out[...] = tot_w

    return pl.pallas_call(
        body, name="small_allreduce",
        in_specs=[VMEM] * 8, out_specs=[VMEM] * 2,
        out_shape=[jax.ShapeDtypeStruct((8, d), F32), jax.ShapeDtypeStruct(wps, F32)],
        scratch_shapes=[pltpu.VMEM((N_CHIPS, 8, d), F32), pltpu.VMEM((N_CHIPS,) + wps, F32),
                        pltpu.VMEM((8, d), F32), pltpu.VMEM(wps, F32),
                        pltpu.SemaphoreType.DMA((8,)), pltpu.SemaphoreType.DMA((8,))],
    )(stats["mix"], stats["x"], stats["mem"], stats["ffn"], stats["loss"], stats["hgrn"], stats["pool"], d_wp)


def _adamw_math(w, g, m, v):
    m = ADAM_B1 * m + (1.0 - ADAM_B1) * g
    v = ADAM_B2 * v + (1.0 - ADAM_B2) * (g * g)
    m_hat = m / (1.0 - ADAM_B1 ** ADAM_STEP)
    v_hat = v / (1.0 - ADAM_B2 ** ADAM_STEP)
    delta = -ADAM_LR * (m_hat / (jnp.sqrt(v_hat) + ADAM_EPS) + ADAM_WD * w)
    return delta, m, v


def _adamw(name, g, w, m, v, tr):
    rows, cc = w.shape

    def body(g_ref, w_ref, m_ref, v_ref, d_out, m_out, v_out):
        d_out[...], m_out[...], v_out[...] = _adamw_math(w_ref[...], g_ref[...], m_ref[...], v_ref[...])

    spec = pl.BlockSpec((tr, cc), lambda i: (i, 0))
    return pl.pallas_call(
        body, name=name, grid=(rows // tr,),
        in_specs=[spec] * 4, out_specs=[spec] * 3,
        out_shape=[jax.ShapeDtypeStruct((rows, cc), F32)] * 3,
        compiler_params=_params(("parallel",)),
    )(g, w, m, v)


SMALL_NAMES = ("norm_mix_g", "lb_logits", "hgrn_norm_g", "w_pool", "pool_scale", "norm_x_g", "norm_mem_g",
               "norm_ffn_g", "final_norm_g")


def _small_update(slab, d_wp, ws, ms, vs):
    n = len(SMALL_NAMES)
    half = D_MODEL // 2

    def body(slab_ref, wp_ref, *refs):
        w_refs, m_refs, v_refs, outs = refs[:n], refs[n:2 * n], refs[2 * n:3 * n], refs[3 * n:]
        row = lambda k: slab_ref[k:k + 1, :]
        lbl = w_refs[SMALL_NAMES.index("lb_logits")][...]
        s0 = _lower_bound(lbl[0:1, :], lbl[1:2, :])
        dl0 = row(ROW_LB_HGN)[:, :half] * s0 * (1.0 - s0)
        grads = dict(norm_mix_g=row(ROW_GMIX), lb_logits=jnp.concatenate([dl0, -dl0], axis=0),
                     hgrn_norm_g=row(ROW_LB_HGN)[:, half:], w_pool=wp_ref[...], pool_scale=row(ROW_PSCALE)[:, :half],
                     norm_x_g=row(ROW_GX), norm_mem_g=row(ROW_GMEM), norm_ffn_g=row(ROW_GFFN),
                     final_norm_g=row(ROW_GFIN))
        outs[0][...] = row(ROW_LOSS)[:, :128]
        for i, name in enumerate(SMALL_NAMES):
            g = grads[name]
            delta, m2, v2 = _adamw_math(w_refs[i][...], g, m_refs[i][...], v_refs[i][...])
            for o, val in zip(outs[1 + 4 * i:5 + 4 * i], (g, delta, m2, v2)):
                o[...] = val

    args = [ws[k] for k in SMALL_NAMES] + [ms[k] for k in SMALL_NAMES] + [vs[k] for k in SMALL_NAMES]
    out_shape = [jax.ShapeDtypeStruct((1, 128), F32)]
    for k in SMALL_NAMES:
        out_shape += [jax.ShapeDtypeStruct(ws[k].shape, F32)] * 4
    res = pl.pallas_call(
        body, name="small_update",
        in_specs=[VMEM] * (2 + 3 * n), out_specs=[VMEM] * len(out_shape), out_shape=out_shape,
    )(slab, d_wp, *args)
    return res[0], {k: res[1 + 4 * i:5 + 4 * i] for i, k in enumerate(SMALL_NAMES)}


BIG_NAMES = ("w_in", "w_out", "w_xq", "w_xk", "w_xv", "w_xo", "w_ff1", "w_ff2")
ALL_NAMES = ("norm_mix_g", "w_in", "lb_logits", "hgrn_norm_g", "w_pool", "pool_scale", "w_out", "norm_x_g",
             "norm_mem_g", "w_xq", "w_xk", "w_xv", "w_xo", "norm_ffn_g", "w_ff1", "w_ff2", "final_norm_g")


def _shard_2d(name, a):
    a = a[0]
    if name in ("w_xq", "w_xk", "w_xv"):
        return a.reshape(a.shape[0], -1)
    if name == "w_xo":
        return a.reshape(-1, a.shape[-1])
    return a


def _small_2d(name, a):
    if name == "w_pool":
        return a.reshape(-1, HEAD_DIM)
    if name == "lb_logits":
        return a
    return a.reshape(1, -1)


def kernel(x, mem, norm_mix_g, w_in, lb_logits, hgrn_norm_g, w_pool, pool_scale, w_out, norm_x_g, norm_mem_g, w_xq, w_xk, w_xv, w_xo, norm_ffn_g, w_ff1, w_ff2, final_norm_g, loss_target, m_norm_mix_g, m_w_in, m_lb_logits, m_hgrn_norm_g, m_w_pool, m_pool_scale, m_w_out, m_norm_x_g, m_norm_mem_g, m_w_xq, m_w_xk, m_w_xv, m_w_xo, m_norm_ffn_g, m_w_ff1, m_w_ff2, m_final_norm_g, v_norm_mix_g, v_w_in, v_lb_logits, v_hgrn_norm_g, v_w_pool, v_pool_scale, v_w_out, v_norm_x_g, v_norm_mem_g, v_w_xq, v_w_xk, v_w_xv, v_w_xo, v_norm_ffn_g, v_w_ff1, v_w_ff2, v_final_norm_g):
    w = dict(norm_mix_g=norm_mix_g, w_in=w_in, lb_logits=lb_logits, hgrn_norm_g=hgrn_norm_g, w_pool=w_pool, pool_scale=pool_scale, w_out=w_out, norm_x_g=norm_x_g, norm_mem_g=norm_mem_g, w_xq=w_xq, w_xk=w_xk, w_xv=w_xv, w_xo=w_xo, norm_ffn_g=norm_ffn_g, w_ff1=w_ff1, w_ff2=w_ff2, final_norm_g=final_norm_g)
    m = dict(norm_mix_g=m_norm_mix_g, w_in=m_w_in, lb_logits=m_lb_logits, hgrn_norm_g=m_hgrn_norm_g, w_pool=m_w_pool, pool_scale=m_pool_scale, w_out=m_w_out, norm_x_g=m_norm_x_g, norm_mem_g=m_norm_mem_g, w_xq=m_w_xq, w_xk=m_w_xk, w_xv=m_w_xv, w_xo=m_w_xo, norm_ffn_g=m_norm_ffn_g, w_ff1=m_w_ff1, w_ff2=m_w_ff2, final_norm_g=m_final_norm_g)
    v = dict(norm_mix_g=v_norm_mix_g, w_in=v_w_in, lb_logits=v_lb_logits, hgrn_norm_g=v_hgrn_norm_g, w_pool=v_w_pool, pool_scale=v_pool_scale, w_out=v_w_out, norm_x_g=v_norm_x_g, norm_mem_g=v_norm_mem_g, w_xq=v_w_xq, w_xk=v_w_xk, w_xv=v_w_xv, w_xo=v_w_xo, norm_ffn_g=v_norm_ffn_g, w_ff1=v_w_ff1, w_ff2=v_w_ff2, final_norm_g=v_final_norm_g)

    big_w = {k: _shard_2d(k, w[k]) for k in BIG_NAMES}
    slab = jnp.concatenate([big_w[k] for k in ("w_out", "w_xq", "w_xk", "w_xv", "w_ff1", "w_ff2")], axis=0).astype(BF16)
    slab_g, win_g, wo_g = _all_gather_weights([slab, big_w["w_in"].astype(BF16), big_w["w_xo"].astype(BF16)])

    small = {k: w[k] for k in SMALL_NAMES}
    loc = _local_step(x[0], mem[0], loss_target[0], small, slab_g, win_g, wo_g)

    cx, cy, cc = lax.axis_index("x"), lax.axis_index("y"), lax.axis_index("c")
    chip = 2 * cx + cy
    idx = jnp.stack([cc, chip, chip ^ 1, chip ^ 2, chip ^ 3]).astype(jnp.int32)
    gs = [loc["d" + k] for k in BIG_NAMES]
    gs = [g.reshape(N_CHIPS, 2, g.shape[1] // 2, g.shape[2]) for g in gs]
    got = _grad_pair_exchange(gs)
    keeps, sends = zip(*[_grad_pair_add("grad_pair_add_" + k, g, r, idx, tr=128) for k, g, r in zip(BIG_NAMES, gs, got)])
    recvd = _grad_chip_exchange(list(sends))
    halves = [_grad_chip_add("grad_chip_add_" + k, kp, r, tr=128) for k, kp, r in zip(BIG_NAMES, keeps, recvd)]
    full = _grad_half_exchange(halves)

    grads, deltas, new_m, new_v = {}, {}, {}, {}
    for k, f in zip(BIG_NAMES, full):
        g2 = f.reshape(big_w[k].shape)
        d2, m2, v2 = _adamw("adamw_" + k, g2, big_w[k], _shard_2d(k, m[k]), _shard_2d(k, v[k]), tr=128)
        for store, val in ((grads, g2), (deltas, d2), (new_m, m2), (new_v, v2)):
            store[k] = val.reshape(w[k].shape)

    slab_sum, wp_sum = _small_allreduce(loc["stats"], loc["d_w_pool"].reshape(-1, HEAD_DIM))
    loss, upd = _small_update(slab_sum, wp_sum, {k: _small_2d(k, w[k]) for k in SMALL_NAMES},
                              {k: _small_2d(k, m[k]) for k in SMALL_NAMES}, {k: _small_2d(k, v[k]) for k in SMALL_NAMES})
    for k in SMALL_NAMES:
        for store, val in zip((grads, deltas, new_m, new_v), upd[k]):
            store[k] = val.reshape(w[k].shape)

    return (loss[0, 0], loc["grad_x"][None], *[grads[k] for k in ALL_NAMES], *[deltas[k] for k in ALL_NAMES],
            *[new_m[k] for k in ALL_NAMES], *[new_v[k] for k in ALL_NAMES])
```

```python
import functools

import jax
import jax.numpy as jnp
from jax import lax
from jax.experimental import pallas as pl
from jax.experimental.pallas import tpu as pltpu

F32 = jnp.float32
BF16 = jnp.bfloat16
HIGHEST = lax.Precision.HIGHEST
MESH = pl.DeviceIdType.MESH
ANY = pl.BlockSpec(memory_space=pl.ANY)
VMEM = pl.BlockSpec(memory_space=pltpu.VMEM)

D_MODEL = 1024
N_CHIPS = 4
HGRN_HEADS = 4
HEAD_DIM = 128
HGRN_WIDTH = HGRN_HEADS * HEAD_DIM
POOL_WINDOWS = (2, 4, 8, 16)
POOL_HALO = 16
SUB = 16
XATTN_HEADS = 4
XATTN_HEAD_DIM = 256
EPS = 1e-6
ADAM_LR, ADAM_B1, ADAM_B2, ADAM_EPS, ADAM_WD, ADAM_STEP = 0.001, 0.9, 0.999, 1e-08, 0.01, 10

V7X_VMEM_BYTES = 64 * 1024 * 1024
VMEM_LIMIT = V7X_VMEM_BYTES - 8 * 1024 * 1024

NN = (((1,), (0,)), ((), ()))
NT = (((1,), (1,)), ((), ()))
TN = (((0,), (0,)), ((), ()))

ROW_GMIX, ROW_GX, ROW_GMEM, ROW_GFFN, ROW_GFIN, ROW_LB_HGN, ROW_PSCALE, ROW_LOSS = range(8)


def _dot(a, b, dims=NN):
    return lax.dot_general(a, b, dims, preferred_element_type=F32)


def _sigmoid(x):
    return 1.0 / (1.0 + jnp.exp(-x))


def _rms_fwd(x, g):
    r = lax.rsqrt(jnp.mean(x * x, axis=-1, keepdims=True) + EPS)
    n = x * r
    return n * g, n, r


def _rms_bwd(dh, n, r, g):
    dn = dh * g
    dx = r * (dn - n * jnp.mean(dn * n, axis=-1, keepdims=True))
    return dx, jnp.sum(dh * n, axis=0, keepdims=True)


def _params(sem=None):
    return pltpu.CompilerParams(dimension_semantics=sem, vmem_limit_bytes=VMEM_LIMIT)


def _const(shape):
    nd = len(shape)
    return pl.BlockSpec(shape, lambda *_: (0,) * nd, pipeline_mode=pl.Buffered(1))


def _const_out(shape):
    nd = len(shape)
    return pl.BlockSpec(shape, lambda *_: (0,) * nd)


def _acc_rows(ref, t, rows):
    upd = jnp.concatenate(rows + [jnp.zeros((8 - len(rows), rows[0].shape[1]), F32)], axis=0)

    @pl.when(t == 0)
    def _():
        ref[...] = upd

    @pl.when(t > 0)
    def _():
        ref[...] = ref[...] + upd


def _in_proj(x, g, win_g, tm):
    s, d = x.shape
    nsh, _, wc = win_g.shape

    def body(x_ref, g_ref, w_ref, z_ref, h_ref):
        h, _, _ = _rms_fwd(x_ref[...], g_ref[...])
        hb = h.astype(BF16)
        h_ref[...] = hb
        for j in range(nsh):
            z_ref[:, j * wc:(j + 1) * wc] = _dot(hb, w_ref[j])

    return pl.pallas_call(
        body, name="in_proj", grid=(s // tm,),
        in_specs=[pl.BlockSpec((tm, d), lambda t: (t, 0)), _const((1, d)), _const((nsh, d, wc))],
        out_specs=[pl.BlockSpec((tm, nsh * wc), lambda t: (t, 0)), pl.BlockSpec((tm, d), lambda t: (t, 0))],
        out_shape=[jax.ShapeDtypeStruct((s, nsh * wc), F32), jax.ShapeDtypeStruct((s, d), BF16)],
        compiler_params=_params(("parallel",)),
    )(x, g, win_g)


def _lower_bound(l0, l1):
    m = jnp.maximum(l0, l1)
    e0, e1 = jnp.exp(l0 - m), jnp.exp(l1 - m)
    return e0 / (e0 + e1)


def _block_tri(n, upper):
    r = lax.broadcasted_iota(jnp.int32, (n, n), 0)
    c = lax.broadcasted_iota(jnp.int32, (n, n), 1)
    keep = (r // SUB == c // SUB) & ((c >= r) if upper else (c <= r))
    return keep.astype(F32)


def _hgrn_gates(qp, fp, lb):
    sq = _sigmoid(qp)
    sf = _sigmoid(fp)
    f = lb + (1.0 - lb) * sf
    return qp * sq, sq, f, sf


def _hgrn_fwd(z, l0, l1, gn, tc):
    s = z.shape[0]
    nsub = tc // SUB
    hd = HEAD_DIM

    def body(q_ref, f_ref, v_ref, g_ref, l0_ref, l1_ref, gn_ref, o_ref, oa_ref, st_ref, state, qs, ks, bs, os_):
        @pl.when(pl.program_id(1) == 0)
        def _():
            state[...] = jnp.zeros_like(state)

        lb = _lower_bound(l0_ref[...], l1_ref[...])
        q, _, f, _ = _hgrn_gates(q_ref[...], f_ref[...], lb)
        qs[...] = q
        ks[...] = 1.0 - f
        bs[...] = jnp.dot(_block_tri(tc, False), jnp.log(f), precision=HIGHEST, preferred_element_type=F32)
        rows = lax.broadcasted_iota(jnp.int32, (SUB, 1), 0)

        def step(i, carry):
            r0 = pl.multiple_of(i * SUB, SUB)
            q_ = qs[pl.ds(r0, SUB), :]
            k_ = ks[pl.ds(r0, SUB), :]
            b_ = bs[pl.ds(r0, SUB), :]
            v_ = v_ref[pl.ds(r0, SUB), :]
            st = state[...]
            st_ref[i] = st
            bl = b_[SUB - 1:SUB, :]
            o = _dot((q_ * jnp.exp(b_)).astype(BF16), st.astype(BF16), NT)
            for j in range(SUB):
                e = jnp.exp(jnp.minimum(b_ - b_[j:j + 1, :], 0.0))
                col = jnp.sum(q_ * e * k_[j:j + 1, :], axis=-1, keepdims=True)
                o = o + jnp.where(rows >= j, col, 0.0) * v_[j:j + 1, :]
            os_[pl.ds(r0, SUB), :] = o
            kt = (k_ * jnp.exp(bl - b_)).astype(BF16)
            state[...] = st * jnp.exp(bl) + _dot(v_.astype(BF16), kt, TN)
            return carry

        lax.fori_loop(0, nsub, step, 0)
        o = os_[...]
        o_ref[...] = o
        r = lax.rsqrt(jnp.mean(o * o, axis=-1, keepdims=True) + EPS)
        gp = g_ref[...]
        oa_ref[...] = (o * r * gn_ref[...] * (gp * _sigmoid(gp))).astype(BF16)

    col = lambda k: pl.BlockSpec((tc, hd), lambda h, t: (t, k * HGRN_HEADS + h))
    vec = pl.BlockSpec((None, 1, hd), lambda h, t: (h, 0, 0))
    return pl.pallas_call(
        body, name="hgrn_fwd", grid=(HGRN_HEADS, s // tc),
        in_specs=[col(0), col(1), col(2), col(3), vec, vec, vec],
        out_specs=[pl.BlockSpec((tc, hd), lambda h, t: (t, h)), pl.BlockSpec((tc, hd), lambda h, t: (t, h)),
                   pl.BlockSpec((None, nsub, hd, hd), lambda h, t: (h, t, 0, 0))],
        out_shape=[jax.ShapeDtypeStruct((s, HGRN_WIDTH), F32), jax.ShapeDtypeStruct((s, HGRN_WIDTH), BF16),
                   jax.ShapeDtypeStruct((HGRN_HEADS, s // SUB, hd, hd), F32)],
        scratch_shapes=[pltpu.VMEM((hd, hd), F32)] + [pltpu.VMEM((tc, hd), F32)] * 4,
        compiler_params=_params(("parallel", "arbitrary")),
    )(z, z, z, z, l0, l1, gn)


def _pooled(p, ext, tok0):
    tm = p.shape[0]
    tok = tok0 + lax.broadcasted_iota(jnp.int32, (tm, 1), 0)
    outs = []
    for g, w in enumerate(POOL_WINDOWS):
        acc = ext[:, g * HEAD_DIM:(g + 1) * HEAD_DIM]
        sh = 1
        while sh < w:
            acc = acc + pltpu.roll(acc, sh, axis=0)
            sh *= 2
        cnt = jnp.minimum(tok + 1, w).astype(F32)
        outs.append(acc[POOL_HALO:, :] / cnt - p[:, g * HEAD_DIM:(g + 1) * HEAD_DIM])
    return outs


def _pool_fwd(z, wp, scale, tm):
    s = z.shape[0]
    pw = len(POOL_WINDOWS) * HEAD_DIM
    nb = tm // POOL_HALO

    def body(p_ref, prev_ref, wp_ref, sc_ref, ob_ref):
        t = pl.program_id(0)
        p = p_ref[...]
        prev = jnp.where(t > 0, prev_ref[...], 0.0)
        pooled = _pooled(p, jnp.concatenate([prev, p], axis=0), t * tm)
        ys = [_dot(pooled[g].astype(BF16), wp_ref[g].astype(BF16)) for g in range(len(POOL_WINDOWS))]
        ob_ref[...] = (jnp.concatenate(ys, axis=1) * sc_ref[...]).astype(BF16)

    return pl.pallas_call(
        body, name="pool_fwd", grid=(s // tm,),
        in_specs=[pl.BlockSpec((tm, pw), lambda t: (t, 4)),
                  pl.BlockSpec((POOL_HALO, pw), lambda t: (jnp.maximum(t * nb - 1, 0), 4)),
                  _const(wp.shape), _const((1, pw))],
        out_specs=pl.BlockSpec((tm, pw), lambda t: (t, 0)),
        out_shape=jax.ShapeDtypeStruct((s, pw), BF16),
        compiler_params=_params(("parallel",)),
    )(z, z, wp, scale)


def _kv_proj(mem, g, slab_g):
    m, d = mem.shape
    rows = d // N_CHIPS

    def body(mem_ref, g_ref, wk_ref, wv_ref, xk_ref, xv_ref):
        hm, _, _ = _rms_fwd(mem_ref[...], g_ref[...])
        hb = hm.astype(BF16)
        xk_ref[...] = _dot(hb, wk_ref[...].reshape(d, d)).astype(BF16)
        xv_ref[...] = _dot(hb, wv_ref[...].reshape(d, d)).astype(BF16)

    blk = lambda k: pl.BlockSpec((N_CHIPS, rows, d), lambda i: (0, k, 0))
    return pl.pallas_call(
        body, name="kv_proj", grid=(1,),
        in_specs=[_const((m, d)), _const((1, d)), blk(2), blk(3)],
        out_specs=[_const_out((m, d)), _const_out((m, d))],
        out_shape=[jax.ShapeDtypeStruct((m, d), BF16)] * 2,
        compiler_params=_params(("arbitrary",)),
    )(mem, g, slab_g, slab_g)


def _softmax_rows(sc):
    e = jnp.exp(sc - jnp.max(sc, axis=-1, keepdims=True))
    return e / jnp.sum(e, axis=-1, keepdims=True)


def _mix_xattn_fwd(x, oa, ob, gx, slab_g, wo_g, xk, xv, tm):
    s, d = x.shape
    m = xk.shape[0]
    rows = d // N_CHIPS
    hw = oa.shape[1]
    e = XATTN_HEAD_DIM

    def body(x_ref, oa_ref, ob_ref, gx_ref, wout_ref, wq_ref, wo_ref, xk_ref, xv_ref,
             x1_ref, mixed_ref, hq_ref, xq_ref, att_ref, x2_ref):
        mixed = jnp.concatenate([oa_ref[...], ob_ref[...]], axis=1)
        mixed_ref[...] = mixed
        x1 = x_ref[...] + _dot(mixed, wout_ref[...].reshape(d, d))
        x1_ref[...] = x1
        hq, _, _ = _rms_fwd(x1, gx_ref[...])
        hqb = hq.astype(BF16)
        hq_ref[...] = hqb
        xq = _dot(hqb, wq_ref[...].reshape(d, d)).astype(BF16)
        xq_ref[...] = xq
        atts = []
        for h in range(XATTN_HEADS):
            cs = slice(h * e, (h + 1) * e)
            p = _softmax_rows(_dot(xq[:, cs], xk_ref[:, cs], NT) * (e ** -0.5))
            atts.append(_dot(p.astype(BF16), xv_ref[:, cs]).astype(BF16))
        att = jnp.concatenate(atts, axis=1)
        att_ref[...] = att
        for j in range(N_CHIPS):
            x2_ref[:, j * rows:(j + 1) * rows] = x1[:, j * rows:(j + 1) * rows] + _dot(att, wo_ref[j])

    tile = lambda w: pl.BlockSpec((tm, w), lambda t: (t, 0))
    blk = lambda k: pl.BlockSpec((N_CHIPS, rows, d), lambda t: (0, k, 0), pipeline_mode=pl.Buffered(1))
    return pl.pallas_call(
        body, name="mix_xattn_fwd", grid=(s // tm,),
        in_specs=[tile(d), tile(hw), tile(hw), _const((1, d)), blk(0), blk(1), _const(wo_g.shape),
                  _const((m, d)), _const((m, d))],
        out_specs=[tile(d)] * 6,
        out_shape=[jax.ShapeDtypeStruct((s, d), F32)] + [jax.ShapeDtypeStruct((s, d), BF16)] * 4
                  + [jax.ShapeDtypeStruct((s, d), F32)],
        compiler_params=_params(("parallel",)),
    )(x, oa, ob, gx, slab_g, slab_g, wo_g, xk, xv)


def _mlp_loss_fwd(x2, gffn, gfin, slab_g, target, tm):
    s, d = x2.shape
    wr = slab_g.shape[1] // 3

    def body(x2_ref, gffn_ref, gfin_ref, w1_ref, w2_ref, tg_ref, a_ref, hf_ref, dx3_ref, dx3b_ref, st_ref):
        x2v = x2_ref[...]
        hf, _, _ = _rms_fwd(x2v, gffn_ref[...])
        hfb = hf.astype(BF16)
        hf_ref[...] = hfb
        acc = x2v
        for j in range(N_CHIPS):
            a = _dot(hfb, w1_ref[j])
            a_ref[:, j * wr:(j + 1) * wr] = a
            r = jnp.maximum(a, 0.0)
            acc = acc + _dot((r * r).astype(BF16), w2_ref[j])
        gf = gfin_ref[...]
        y, n, r3 = _rms_fwd(acc, gf)
        err = y - tg_ref[...]
        loss = 0.5 * jnp.sum(jnp.sum(err * err, axis=-1, keepdims=True) * (1.0 / d), axis=0, keepdims=True)
        dy = err * (1.0 / d)
        dx3, dgf = _rms_bwd(dy, n, r3, gf)
        dx3_ref[...] = dx3
        dx3b_ref[...] = dx3.astype(BF16)
        _acc_rows(st_ref, pl.program_id(0), [dgf, jnp.broadcast_to(loss, (1, d))])

    tile = lambda w: pl.BlockSpec((tm, w), lambda t: (t, 0))
    blk = lambda k: pl.BlockSpec((N_CHIPS, wr, d), lambda t: (0, k, 0), pipeline_mode=pl.Buffered(1))
    return pl.pallas_call(
        body, name="mlp_loss_fwd", grid=(s // tm,),
        in_specs=[tile(d), _const((1, d)), _const((1, d)), blk(1), blk(2), tile(d)],
        out_specs=[tile(N_CHIPS * wr), tile(d), tile(d), tile(d), _const_out((8, d))],
        out_shape=[jax.ShapeDtypeStruct((s, N_CHIPS * wr), F32), jax.ShapeDtypeStruct((s, d), BF16),
                   jax.ShapeDtypeStruct((s, d), F32), jax.ShapeDtypeStruct((s, d), BF16),
                   jax.ShapeDtypeStruct((8, d), F32)],
        compiler_params=_params(("arbitrary",)),
    )(x2, gffn, gfin, slab_g, slab_g, target)


def _mlp_bwd(dx3, dx3b, a, x2, gffn, slab_g, tm):
    s, d = x2.shape
    wr = slab_g.shape[1] // 3

    def body(dx3_ref, dx3b_ref, a_ref, x2_ref, g_ref, w1_ref, w2_ref, da_ref, u_ref, dx2_ref, dx2b_ref, st_ref):
        dyb = dx3b_ref[...]
        dhf = jnp.zeros((tm, d), F32)
        for j in range(N_CHIPS):
            r = jnp.maximum(a_ref[:, j * wr:(j + 1) * wr], 0.0)
            da = (_dot(dyb, w2_ref[j], NT) * (2.0 * r)).astype(BF16)
            da_ref[:, j * wr:(j + 1) * wr] = da
            u_ref[:, j * wr:(j + 1) * wr] = (r * r).astype(BF16)
            dhf = dhf + _dot(da, w1_ref[j], NT)
        g = g_ref[...]
        _, n, r2 = _rms_fwd(x2_ref[...], g)
        dxn, dg = _rms_bwd(dhf, n, r2, g)
        dx2 = dx3_ref[...] + dxn
        dx2_ref[...] = dx2
        dx2b_ref[...] = dx2.astype(BF16)
        _acc_rows(st_ref, pl.program_id(0), [dg])

    tile = lambda w: pl.BlockSpec((tm, w), lambda t: (t, 0))
    blk = lambda k: pl.BlockSpec((N_CHIPS, wr, d), lambda t: (0, k, 0), pipeline_mode=pl.Buffered(1))
    nf = N_CHIPS * wr
    return pl.pallas_call(
        body, name="mlp_bwd", grid=(s // tm,),
        in_specs=[tile(d), tile(d), tile(nf), tile(d), _const((1, d)), blk(1), blk(2)],
        out_specs=[tile(nf), tile(nf), tile(d), tile(d), _const_out((8, d))],
        out_shape=[jax.ShapeDtypeStruct((s, nf), BF16), jax.ShapeDtypeStruct((s, nf), BF16),
                   jax.ShapeDtypeStruct((s, d), F32), jax.ShapeDtypeStruct((s, d), BF16),
                   jax.ShapeDtypeStruct((8, d), F32)],
        compiler_params=_params(("arbitrary",)),
    )(dx3, dx3b, a, x2, gffn, slab_g, slab_g)


def _xattn_mix_bwd(dx2, x1, xq, xk, xv, gx, slab_g, wo_g, tm):
    s, d = x1.shape
    m = xk.shape[0]
    rows = d // N_CHIPS
    e = XATTN_HEAD_DIM

    def body(dx2_ref, x1_ref, xq_ref, xk_ref, xv_ref, gx_ref, wout_ref, wq_ref, wo_ref,
             dx1_ref, dx1b_ref, dxq_ref, dmix_ref, dxk_ref, dxv_ref, st_ref):
        t = pl.program_id(0)
        dx2 = dx2_ref[...]
        dx2b = dx2.astype(BF16)
        datt = jnp.zeros((tm, d), F32)
        for j in range(N_CHIPS):
            datt = datt + _dot(dx2b[:, j * rows:(j + 1) * rows], wo_ref[j], NT)
        dattb = datt.astype(BF16)
        dxqs, dxks, dxvs = [], [], []
        for h in range(XATTN_HEADS):
            cs = slice(h * e, (h + 1) * e)
            xq_h, xk_h, xv_h = xq_ref[:, cs], xk_ref[:, cs], xv_ref[:, cs]
            p = _softmax_rows(_dot(xq_h, xk_h, NT) * (e ** -0.5))
            dp = _dot(dattb[:, cs], xv_h, NT)
            ds = (p * (dp - jnp.sum(dp * p, axis=-1, keepdims=True)) * (e ** -0.5)).astype(BF16)
            dxqs.append(_dot(ds, xk_h).astype(BF16))
            dxks.append(_dot(ds, xq_h, TN))
            dxvs.append(_dot(p.astype(BF16), dattb[:, cs], TN))
        dxq = jnp.concatenate(dxqs, axis=1)
        dxq_ref[...] = dxq
        dxk = jnp.concatenate(dxks, axis=1)
        dxv = jnp.concatenate(dxvs, axis=1)

        @pl.when(t == 0)
        def _():
            dxk_ref[...] = dxk
            dxv_ref[...] = dxv

        @pl.when(t > 0)
        def _():
            dxk_ref[...] = dxk_ref[...] + dxk
            dxv_ref[...] = dxv_ref[...] + dxv

        dhq = jnp.concatenate([_dot(dxq, wq_ref[j], NT) for j in range(N_CHIPS)], axis=1)
        g = gx_ref[...]
        _, n, r1 = _rms_fwd(x1_ref[...], g)
        dxn, dg = _rms_bwd(dhq, n, r1, g)
        dx1 = dx2 + dxn
        dx1_ref[...] = dx1
        dx1b = dx1.astype(BF16)
        dx1b_ref[...] = dx1b
        for j in range(N_CHIPS):
            dmix_ref[:, j * rows:(j + 1) * rows] = _dot(dx1b, wout_ref[j], NT)
        _acc_rows(st_ref, t, [dg])

    tile = lambda: pl.BlockSpec((tm, d), lambda t: (t, 0))
    blk = lambda k: pl.BlockSpec((N_CHIPS, rows, d), lambda t: (0, k, 0), pipeline_mode=pl.Buffered(1))
    return pl.pallas_call(
        body, name="xattn_mix_bwd", grid=(s // tm,),
        in_specs=[tile(), tile(), tile(), _const((m, d)), _const((m, d)), _const((1, d)), blk(0), blk(1),
                  _const(wo_g.shape)],
        out_specs=[tile(), tile(), tile(), tile(), _const_out((m, d)), _const_out((m, d)), _const_out((8, d))],
        out_shape=[jax.ShapeDtypeStruct((s, d), F32), jax.ShapeDtypeStruct((s, d), BF16),
                   jax.ShapeDtypeStruct((s, d), BF16), jax.ShapeDtypeStruct((s, d), F32),
                   jax.ShapeDtypeStruct((m, d), F32), jax.ShapeDtypeStruct((m, d), F32),
                   jax.ShapeDtypeStruct((8, d), F32)],
        compiler_params=_params(("arbitrary",)),
    )(dx2, x1, xq, xk, xv, gx, slab_g, slab_g, wo_g)


def _kv_bwd(mem, g, dxk, dxv, slab_g):
    m, d = mem.shape
    rows = d // N_CHIPS

    def body(mem_ref, g_ref, dxk_ref, dxv_ref, wk_ref, wv_ref, dwk_ref, dwv_ref, st_ref):
        gv = g_ref[...]
        hm, n, _ = _rms_fwd(mem_ref[...], gv)
        hb = hm.astype(BF16)
        dkb = dxk_ref[...].astype(BF16)
        dvb = dxv_ref[...].astype(BF16)
        dhm = []
        for j in range(N_CHIPS):
            hj = hb[:, j * rows:(j + 1) * rows]
            dwk_ref[j] = _dot(hj, dkb, TN)
            dwv_ref[j] = _dot(hj, dvb, TN)
            dhm.append(_dot(dkb, wk_ref[j], NT) + _dot(dvb, wv_ref[j], NT))
        dg = jnp.sum(jnp.concatenate(dhm, axis=1) * n, axis=0, keepdims=True)
        st_ref[...] = jnp.concatenate([dg, jnp.zeros((7, d), F32)], axis=0)

    blk = lambda k: pl.BlockSpec((N_CHIPS, rows, d), lambda i: (0, k, 0))
    return pl.pallas_call(
        body, name="kv_bwd", grid=(1,),
        in_specs=[_const((m, d)), _const((1, d)), _const((m, d)), _const((m, d)), blk(2), blk(3)],
        out_specs=[_const_out((N_CHIPS, rows, d)), _const_out((N_CHIPS, rows, d)), _const_out((8, d))],
        out_shape=[jax.ShapeDtypeStruct((N_CHIPS, rows, d), F32)] * 2 + [jax.ShapeDtypeStruct((8, d), F32)],
        compiler_params=_params(("arbitrary",)),
    )(mem, g, dxk, dxv, slab_g, slab_g)


def _pool_bwd(z, dmix, wp, scale, tm):
    s = z.shape[0]
    ng = len(POOL_WINDOWS)
    pw = ng * HEAD_DIM
    nb = tm // POOL_HALO
    nt = s // tm
    n_ext = tm + POOL_HALO

    def body(p_ref, prev_ref, dm_ref, dmn_ref, wp_ref, sc_ref, dp_ref, dwp_ref, st_ref):
        t = pl.program_id(0)
        p = p_ref[...]
        prev = jnp.where(t > 0, prev_ref[...], 0.0)
        pooled = _pooled(p, jnp.concatenate([prev, p], axis=0), t * tm)
        dm = dm_ref[...]
        dme = jnp.concatenate([dm, jnp.where(t < nt - 1, dmn_ref[...], 0.0)], axis=0) * sc_ref[...]
        tok = t * tm + lax.broadcasted_iota(jnp.int32, (n_ext, 1), 0)
        dsc, dps, dwps = [], [], []
        for g, w in enumerate(POOL_WINDOWS):
            cs = slice(g * HEAD_DIM, (g + 1) * HEAD_DIM)
            wpb = wp_ref[g].astype(BF16)
            pb = pooled[g].astype(BF16)
            dsc.append(jnp.sum(dm[:, cs] * _dot(pb, wpb), axis=0, keepdims=True))
            dye = dme[:, cs].astype(BF16)
            dwps.append(_dot(pb, dye[:tm], TN))
            dpe = _dot(dye, wpb, NT)
            acc = dpe / jnp.minimum(tok + 1, w).astype(F32)
            sh = 1
            while sh < w:
                acc = acc + pltpu.roll(acc, n_ext - sh, axis=0)
                sh *= 2
            dps.append(acc[:tm] - dpe[:tm])
        dp_ref[...] = jnp.concatenate(dps, axis=1)
        dsc_row = jnp.concatenate(dsc, axis=1)

        @pl.when(t == 0)
        def _():
            for g in range(ng):
                dwp_ref[g] = dwps[g]

        @pl.when(t > 0)
        def _():
            for g in range(ng):
                dwp_ref[g] = dwp_ref[g] + dwps[g]

        _acc_rows(st_ref, t, [dsc_row])

    return pl.pallas_call(
        body, name="pool_bwd", grid=(nt,),
        in_specs=[pl.BlockSpec((tm, pw), lambda t: (t, 4)),
                  pl.BlockSpec((POOL_HALO, pw), lambda t: (jnp.maximum(t * nb - 1, 0), 4)),
                  pl.BlockSpec((tm, pw), lambda t: (t, 1)),
                  pl.BlockSpec((POOL_HALO, pw), lambda t: (jnp.minimum((t + 1) * nb, s // POOL_HALO - 1), 1)),
                  _const(wp.shape), _const((1, pw))],
        out_specs=[pl.BlockSpec((tm, pw), lambda t: (t, 0)), _const_out(wp.shape), _const_out((8, pw))],
        out_shape=[jax.ShapeDtypeStruct((s, pw), F32), jax.ShapeDtypeStruct(wp.shape, F32),
                   jax.ShapeDtypeStruct((8, pw), F32)],
        compiler_params=_params(("arbitrary",)),
    )(z, z, dmix, dmix, wp, scale)


def _hgrn_bwd(z, o, dmix, st, l0, l1, gn, tc):
    s = z.shape[0]
    nsub = tc // SUB
    nt = s // tc
    hd = HEAD_DIM

    def body(q_ref, f_ref, v_ref, g_ref, l0_ref, l1_ref, gn_ref, o_ref, dm_ref, st_ref,
             dq_ref, df_ref, di_ref, dg_ref, stat_ref, dstate, qs, ks, bs, dos, dqs, dks, dbs):
        t = pl.program_id(1)

        @pl.when(t == 0)
        def _():
            dstate[...] = jnp.zeros_like(dstate)

        lb = _lower_bound(l0_ref[...], l1_ref[...])
        qp = q_ref[...]
        q, sq, f, sf = _hgrn_gates(qp, f_ref[...], lb)
        qs[...] = q
        ks[...] = 1.0 - f
        bs[...] = jnp.dot(_block_tri(tc, False), jnp.log(f), precision=HIGHEST, preferred_element_type=F32)

        o = o_ref[...]
        r = lax.rsqrt(jnp.mean(o * o, axis=-1, keepdims=True) + EPS)
        n = o * r
        gnv = gn_ref[...]
        gp = g_ref[...]
        sg = _sigmoid(gp)
        dm = dm_ref[...]
        dg_ref[...] = dm * (n * gnv) * (sg * (1.0 + gp * (1.0 - sg)))
        don = dm * (gp * sg)
        dgn = jnp.sum(don * n, axis=0, keepdims=True)
        dn = don * gnv
        dos[...] = r * (dn - n * jnp.mean(dn * n, axis=-1, keepdims=True))
        rows = lax.broadcasted_iota(jnp.int32, (SUB, 1), 0)

        def step(i, carry):
            ii = nsub - 1 - i
            r0 = pl.multiple_of(ii * SUB, SUB)
            q_ = qs[pl.ds(r0, SUB), :]
            k_ = ks[pl.ds(r0, SUB), :]
            b_ = bs[pl.ds(r0, SUB), :]
            v_ = v_ref[pl.ds(r0, SUB), :]
            do_ = dos[pl.ds(r0, SUB), :]
            stp = st_ref[ii]
            dst = dstate[...]
            bl = b_[SUB - 1:SUB, :]
            eb = jnp.exp(b_)
            ekl = jnp.exp(bl - b_)
            ebl = jnp.exp(bl)
            dob = do_.astype(BF16)
            dstb = dst.astype(BF16)
            kt = k_ * ekl
            dq = _dot(dob, stp.astype(BF16)) * eb
            dkt = _dot(v_.astype(BF16), dstb)
            dk = dkt * ekl
            dv = _dot(kt.astype(BF16), dstb, NT)
            extra = jnp.sum(kt * dkt, axis=0, keepdims=True) + ebl * jnp.sum(stp * dst, axis=0, keepdims=True)
            for j in range(SUB):
                e = jnp.exp(jnp.minimum(b_ - b_[j:j + 1, :], 0.0))
                pe = q_ * e
                kj = k_[j:j + 1, :]
                keep = rows >= j
                acol = jnp.where(keep, jnp.sum(pe * kj, axis=-1, keepdims=True), 0.0)
                dacol = jnp.where(keep, jnp.sum(do_ * v_[j:j + 1, :], axis=-1, keepdims=True), 0.0)
                dq = dq + dacol * (e * kj)
                at_j = rows == j
                dk = dk + jnp.where(at_j, jnp.sum(dacol * pe, axis=0, keepdims=True), 0.0)
                dv = dv + jnp.where(at_j, jnp.sum(acol * do_, axis=0, keepdims=True), 0.0)
            dqs[pl.ds(r0, SUB), :] = dq
            dks[pl.ds(r0, SUB), :] = dk
            di_ref[pl.ds(r0, SUB), :] = dv
            dbs[pl.ds(r0, SUB), :] = q_ * dq - k_ * dk + jnp.where(rows == SUB - 1, extra, 0.0)
            dstate[...] = dst * ebl + _dot(dob, (q_ * eb).astype(BF16), TN)
            return carry

        lax.fori_loop(0, nsub, step, 0)
        dlf = jnp.dot(_block_tri(tc, True), dbs[...], precision=HIGHEST, preferred_element_type=F32)
        dfv = dlf / f - dks[...]
        df_ref[...] = dfv * (1.0 - lb) * sf * (1.0 - sf)
        dlb = jnp.sum(dfv * (1.0 - sf), axis=0, keepdims=True)
        dq_ref[...] = dqs[...] * (sq * (1.0 + qp * (1.0 - sq)))
        _acc_rows(stat_ref, t, [dgn, dlb])

    rev = lambda t: nt - 1 - t
    col = lambda k: pl.BlockSpec((tc, hd), lambda h, t: (rev(t), k * HGRN_HEADS + h))
    vec = pl.BlockSpec((None, 1, hd), lambda h, t: (h, 0, 0))
    head = pl.BlockSpec((tc, hd), lambda h, t: (rev(t), h))
    return pl.pallas_call(
        body, name="hgrn_bwd", grid=(HGRN_HEADS, nt),
        in_specs=[col(0), col(1), col(2), col(3), vec, vec, vec, head, head,
                  pl.BlockSpec((None, nsub, hd, hd), lambda h, t: (h, rev(t), 0, 0))],
        out_specs=[head, head, head, head, pl.BlockSpec((None, 8, hd), lambda h, t: (h, 0, 0))],
        out_shape=[jax.ShapeDtypeStruct((s, HGRN_WIDTH), F32)] * 4 + [jax.ShapeDtypeStruct((HGRN_HEADS, 8, hd), F32)],
        scratch_shapes=[pltpu.VMEM((hd, hd), F32)] + [pltpu.VMEM((tc, hd), F32)] * 7,
        compiler_params=_params(("parallel", "arbitrary")),
    )(z, z, z, z, l0, l1, gn, o, dmix, st)


def _in_bwd(dparts, dx1, x, g, win_g, tm):
    s, d = x.shape
    nsh, _, wc = win_g.shape
    pw = dparts[0].shape[1]

    def body(dq_ref, df_ref, di_ref, dg_ref, dp_ref, dx1_ref, x_ref, g_ref, w_ref, gx_ref, dz_ref, st_ref):
        dz = jnp.concatenate([dq_ref[...], df_ref[...], di_ref[...], dg_ref[...], dp_ref[...]], axis=1).astype(BF16)
        dz_ref[...] = dz
        dh = jnp.zeros((tm, d), F32)
        for j in range(nsh):
            dh = dh + _dot(dz[:, j * wc:(j + 1) * wc], w_ref[j], NT)
        gv = g_ref[...]
        _, n, r = _rms_fwd(x_ref[...], gv)
        dxn, dg = _rms_bwd(dh, n, r, gv)
        gx_ref[...] = dx1_ref[...] + dxn
        _acc_rows(st_ref, pl.program_id(0), [dg])

    tile = lambda w: pl.BlockSpec((tm, w), lambda t: (t, 0))
    return pl.pallas_call(
        body, name="in_bwd", grid=(s // tm,),
        in_specs=[tile(pw)] * 5 + [tile(d), tile(d), _const((1, d)), _const(win_g.shape)],
        out_specs=[tile(d), tile(nsh * wc), _const_out((8, d))],
        out_shape=[jax.ShapeDtypeStruct((s, d), F32), jax.ShapeDtypeStruct((s, nsh * wc), BF16),
                   jax.ShapeDtypeStruct((8, d), F32)],
        compiler_params=_params(("arbitrary",)),
    )(*dparts, dx1, x, g, win_g)


def _tn_grad(name, a, b, out_rows, out_cols, a_sharded, tr, tc):
    s = a.shape[0]
    nr, nc = out_rows // tr, out_cols // tc

    def body(a_ref, b_ref, o_ref):
        o_ref[...] = _dot(a_ref[...], b_ref[...], TN)

    a_map = (lambda j, i, k: (0, j * nr + i)) if a_sharded else (lambda j, i, k: (0, i))
    b_map = (lambda j, i, k: (0, k)) if a_sharded else (lambda j, i, k: (0, j * nc + k))
    return pl.pallas_call(
        body, name=name, grid=(N_CHIPS, nr, nc),
        in_specs=[pl.BlockSpec((s, tr), a_map), pl.BlockSpec((s, tc), b_map)],
        out_specs=pl.BlockSpec((None, tr, tc), lambda j, i, k: (j, i, k)),
        out_shape=jax.ShapeDtypeStruct((N_CHIPS, out_rows, out_cols), F32),
        compiler_params=_params(("parallel", "parallel", "parallel")),
    )(a, b)


def _local_step(x, mem, target, small, slab_g, win_g, wo_g):
    d = x.shape[1]
    l0 = small["lb_logits"][0].reshape(HGRN_HEADS, 1, HEAD_DIM)
    l1 = small["lb_logits"][1].reshape(HGRN_HEADS, 1, HEAD_DIM)
    gn = small["hgrn_norm_g"].reshape(HGRN_HEADS, 1, HEAD_DIM)
    wp = small["w_pool"].reshape(len(POOL_WINDOWS), HEAD_DIM, HEAD_DIM)
    psc = small["pool_scale"].reshape(1, -1)
    gmix, gx, gmem, gffn = (small[k].reshape(1, d) for k in ("norm_mix_g", "norm_x_g", "norm_mem_g", "norm_ffn_g"))
    gfin = small["final_norm_g"].reshape(1, d)

    z, h = _in_proj(x, gmix, win_g, tm=512)
    o, oa, st = _hgrn_fwd(z, l0, l1, gn, tc=256)
    ob = _pool_fwd(z, wp, psc, tm=512)
    xk, xv = _kv_proj(mem, gmem, slab_g)
    x1, mixed, hq, xq, att, x2 = _mix_xattn_fwd(x, oa, ob, gx, slab_g, wo_g, xk, xv, tm=256)
    a, hf, dx3, dx3b, st_loss = _mlp_loss_fwd(x2, gffn, gfin, slab_g, target, tm=256)

    da, u, dx2, dx2b, st_ffn = _mlp_bwd(dx3, dx3b, a, x2, gffn, slab_g, tm=256)
    dw_ff1 = _tn_grad("dw_ff1", hf, da, d, d, False, 512, 512)
    dw_ff2 = _tn_grad("dw_ff2", u, dx3b, d, d, True, 512, 512)
    dx1, dx1b, dxq, dmix, dxk, dxv, st_x = _xattn_mix_bwd(dx2, x1, xq, xk, xv, gx, slab_g, wo_g, tm=256)
    dw_xo = _tn_grad("dw_xo", att, dx2b, d, d // N_CHIPS, False, 512, 256)
    dw_xq = _tn_grad("dw_xq", hq, dxq, d // N_CHIPS, d, True, 256, 512)
    dw_out = _tn_grad("dw_out", mixed, dx1b, d // N_CHIPS, d, True, 256, 512)
    dw_xk, dw_xv, st_mem = _kv_bwd(mem, gmem, dxk, dxv, slab_g)
    dp, d_wp, st_pool = _pool_bwd(z, dmix, wp, psc, tm=512)
    dq, df, di, dg, st_hgrn = _hgrn_bwd(z, o, dmix, st, l0, l1, gn, tc=256)
    grad_x, dz, st_mix = _in_bwd([dq, df, di, dg, dp], dx1, x, gmix, win_g, tm=256)
    dw_in = _tn_grad("dw_in", h, dz, d, win_g.shape[2], False, 512, win_g.shape[2])
    return dict(
        loss=st_loss[1, 0], grad_x=grad_x, dw_in=dw_in, dw_out=dw_out, dw_xq=dw_xq, dw_xk=dw_xk, dw_xv=dw_xv,
        dw_xo=dw_xo, dw_ff1=dw_ff1, dw_ff2=dw_ff2, d_w_pool=d_wp,
        d_norm_mix_g=st_mix[0], d_norm_x_g=st_x[0], d_norm_mem_g=st_mem[0], d_norm_ffn_g=st_ffn[0],
        d_final_norm_g=st_loss[0], d_hgrn_norm_g=st_hgrn[:, 0], d_lb=st_hgrn[:, 1], d_pool_scale=st_pool[0],
        stats=dict(mix=st_mix, x=st_x, mem=st_mem, ffn=st_ffn, loss=st_loss, hgrn=st_hgrn, pool=st_pool))


def _place():
    x, y, c = lax.axis_index("x"), lax.axis_index("y"), lax.axis_index("c")
    return x, y, c, [(x, 1 - y), (1 - x, y), (1 - x, 1 - y)]


def _rcopy(src, dst, ssem, rsem, dev):
    return pltpu.make_async_remote_copy(src_ref=src, dst_ref=dst, send_sem=ssem, recv_sem=rsem,
                                        device_id=dev, device_id_type=MESH)


def _all_gather_weights(shards):
    n = len(shards)

    def body(*refs):
        ins, outs = refs[:n], refs[n:2 * n]
        lsem, lrsem, ssem, rsem, fsem, frsem = refs[2 * n:]
        x, y, c, peers = _place()
        chip = 2 * x + y
        sib = (x, y, 1 - c)

        def half(a, who, hc):
            hr = shards[a].shape[0] // 2
            return outs[a].at[who, pl.ds(hc * hr, hr), :]

        local = [_rcopy(ins[a], outs[a].at[chip], lsem.at[a], lrsem.at[a], sib) for a in range(n)]
        for cp in local:
            cp.start()
        sends = []
        for a in range(n):
            hr = shards[a].shape[0] // 2
            for r, (px, py) in enumerate(peers):
                cp = _rcopy(ins[a].at[pl.ds(c * hr, hr), :], half(a, chip, c), ssem.at[a, r], rsem.at[a, r], (px, py, c))
                cp.start()
                sends.append(cp)
        fwds = []
        for a in range(n):
            for r, (px, py) in enumerate(peers):
                got = half(a, 2 * px + py, c)
                _rcopy(got, got, ssem.at[a, r], rsem.at[a, r], (px, py, c)).wait_recv()
                cp = _rcopy(got, got, fsem.at[a, r], frsem.at[a, r], sib)
                cp.start()
                fwds.append(cp)
        for a in range(n):
            for r, (px, py) in enumerate(peers):
                got = half(a, 2 * px + py, 1 - c)
                _rcopy(got, got, fsem.at[a, r], frsem.at[a, r], sib).wait_recv()
        for cp in sends + fwds:
            cp.wait_send()
        for cp in local:
            cp.wait()

    return pl.pallas_call(
        body, name="all_gather_weights",
        in_specs=[ANY] * n, out_specs=[ANY] * n,
        out_shape=[jax.ShapeDtypeStruct((N_CHIPS,) + w.shape, w.dtype) for w in shards],
        scratch_shapes=[pltpu.SemaphoreType.DMA((n,))] * 2 + [pltpu.SemaphoreType.DMA((n, 3))] * 4,
    )(*shards)


def _grad_pair_exchange(gs):
    n = len(gs)

    def body(*refs):
        ins, outs, ssem, rsem = refs[:n], refs[n:2 * n], refs[2 * n], refs[2 * n + 1]
        x, y, c, _ = _place()
        cps = [_rcopy(ins[a].at[:, 1 - c], outs[a], ssem.at[a], rsem.at[a], (x, y, 1 - c)) for a in range(n)]
        for cp in cps:
            cp.start()
        for cp in cps:
            cp.wait()

    return pl.pallas_call(
        body, name="grad_pair_exchange",
        in_specs=[ANY] * n, out_specs=[ANY] * n,
        out_shape=[jax.ShapeDtypeStruct((g.shape[0],) + g.shape[2:], g.dtype) for g in gs],
        scratch_shapes=[pltpu.SemaphoreType.DMA((n,))] * 2,
    )(*gs)


def _grad_pair_add(name, g, got, idx, tr):
    _, _, hr, cc = g.shape

    def body(idx_ref, g0, g1, g2, g3, r0, r1, r2, r3, keep_ref, send_ref):
        keep_ref[...] = g0[...] + r0[...]
        for q, (gq, rq) in enumerate(((g1, r1), (g2, r2), (g3, r3))):
            send_ref[q] = (gq[...] + rq[...]).astype(BF16)

    gspec = lambda q: pl.BlockSpec((None, None, tr, cc), lambda i, idx: (idx[1 + q], idx[0], i, 0))
    rspec = lambda q: pl.BlockSpec((None, tr, cc), lambda i, idx: (idx[1 + q], i, 0))
    return pl.pallas_call(
        body, name=name,
        grid_spec=pltpu.PrefetchScalarGridSpec(
            num_scalar_prefetch=1, grid=(hr // tr,),
            in_specs=[gspec(q) for q in range(4)] + [rspec(q) for q in range(4)],
            out_specs=[pl.BlockSpec((tr, cc), lambda i, idx: (i, 0)), pl.BlockSpec((3, tr, cc), lambda i, idx: (0, i, 0))]),
        out_shape=[jax.ShapeDtypeStruct((hr, cc), F32), jax.ShapeDtypeStruct((3, hr, cc), BF16)],
        compiler_params=_params(("parallel",)),
    )(idx, g, g, g, g, got, got, got, got)


def _grad_chip_exchange(sends):
    n = len(sends)

    def body(*refs):
        ins, outs, ssem, rsem = refs[:n], refs[n:2 * n], refs[2 * n], refs[2 * n + 1]
        _, _, c, peers = _place()
        cps = [_rcopy(ins[a].at[r], outs[a].at[r], ssem.at[a, r], rsem.at[a, r], (px, py, c))
               for a in range(n) for r, (px, py) in enumerate(peers)]
        for cp in cps:
            cp.start()
        for cp in cps:
            cp.wait()

    return pl.pallas_call(
        body, name="grad_chip_exchange",
        in_specs=[ANY] * n, out_specs=[ANY] * n,
        out_shape=[jax.ShapeDtypeStruct(g.shape, g.dtype) for g in sends],
        scratch_shapes=[pltpu.SemaphoreType.DMA((n, 3))] * 2,
    )(*sends)


def _grad_chip_add(name, keep, got, tr):
    hr, cc = keep.shape

    def body(k_ref, g_ref, o_ref):
        o_ref[...] = ((k_ref[...] + g_ref[0].astype(F32)) + g_ref[1].astype(F32)) + g_ref[2].astype(F32)

    return pl.pallas_call(
        body, name=name, grid=(hr // tr,),
        in_specs=[pl.BlockSpec((tr, cc), lambda i: (i, 0)), pl.BlockSpec((3, tr, cc), lambda i: (0, i, 0))],
        out_specs=pl.BlockSpec((tr, cc), lambda i: (i, 0)),
        out_shape=jax.ShapeDtypeStruct((hr, cc), F32),
        compiler_params=_params(("parallel",)),
    )(keep, got)


def _grad_half_exchange(ts):
    n = len(ts)

    def body(*refs):
        ins, outs, ssem, rsem = refs[:n], refs[n:2 * n], refs[2 * n], refs[2 * n + 1]
        x, y, c, _ = _place()
        cps = [_rcopy(ins[a], outs[a], ssem.at[a], rsem.at[a], (x, y, 1 - c)) for a in range(n)]
        for cp in cps:
            cp.start()
        for cp in cps:
            cp.wait()

    return pl.pallas_call(
        body, name="grad_half_exchange",
        in_specs=[ANY] * n, out_specs=[ANY] * n,
        out_shape=[jax.ShapeDtypeStruct(t.shape, t.dtype) for t in ts],
        scratch_shapes=[pltpu.SemaphoreType.DMA((n,))] * 2,
    )(*ts)


def _small_allreduce(stats, d_wp):
    d = D_MODEL
    half = d // 2
    wps = d_wp.shape

    def body(mix_ref, x_ref, mem_ref, ffn_ref, loss_ref, hg_ref, pool_ref, wp_ref, slab_out, wp_out,
             slab_buf, wp_buf, sib_s, sib_w, ssem, rsem):
        x, y, c, peers = _place()
        chip = 2 * x + y
        sib = (x, y, 1 - c)
        hgn = jnp.concatenate([hg_ref[h, 0:1, :] for h in range(HGRN_HEADS)], axis=1)
        dlb = jnp.concatenate([hg_ref[h, 1:2, :] for h in range(HGRN_HEADS)], axis=1)
        slab_buf[0] = jnp.concatenate([
            mix_ref[0:1, :], x_ref[0:1, :], mem_ref[0:1, :], ffn_ref[0:1, :], loss_ref[0:1, :],
            jnp.concatenate([dlb, hgn], axis=1),
            jnp.concatenate([pool_ref[0:1, :], jnp.zeros((1, half), F32)], axis=1),
            loss_ref[1:2, :]], axis=0)
        wp_buf[0] = wp_ref[...]
        pair = [_rcopy(slab_buf.at[0], sib_s, ssem.at[0], rsem.at[0], sib),
                _rcopy(wp_buf.at[0], sib_w, ssem.at[1], rsem.at[1], sib)]
        for cp in pair:
            cp.start()
        for cp in pair:
            cp.wait()
        slab_buf[0] = slab_buf[0] + sib_s[...]
        wp_buf[0] = wp_buf[0] + sib_w[...]
        cps = []
        for r, (px, py) in enumerate(peers):
            cps.append(_rcopy(slab_buf.at[0], slab_buf.at[r + 1], ssem.at[2 + 2 * r], rsem.at[2 + 2 * r], (px, py, c)))
            cps.append(_rcopy(wp_buf.at[0], wp_buf.at[r + 1], ssem.at[3 + 2 * r], rsem.at[3 + 2 * r], (px, py, c)))
        for cp in cps:
            cp.start()
        for cp in cps:
            cp.wait()
        tot_s, tot_w = slab_buf[chip], wp_buf[chip]
        for j in range(1, N_CHIPS):
            tot_s = tot_s + slab_buf[jnp.bitwise_xor(j, chip)]
            tot_w = tot_w + wp_buf[jnp.bitwise_xor(j, chip)]
        slab_out[...] = tot_s
        wp_out[...] = tot_w

    return pl.pallas_call(
        body, name="small_allreduce",
        in_specs=[VMEM] * 8, out_specs=[VMEM] * 2,
        out_shape=[jax.ShapeDtypeStruct((8, d), F32), jax.ShapeDtypeStruct(wps, F32)],
        scratch_shapes=[pltpu.VMEM((N_CHIPS, 8, d), F32), pltpu.VMEM((N_CHIPS,) + wps, F32),
                        pltpu.VMEM((8, d), F32), pltpu.VMEM(wps, F32),
                        pltpu.SemaphoreType.DMA((8,)), pltpu.SemaphoreType.DMA((8,))],
    )(stats["mix"], stats["x"], stats["mem"], stats["ffn"], stats["loss"], stats["hgrn"], stats["pool"], d_wp)


def _adamw_math(w, g, m, v):
    m = ADAM_B1 * m + (1.0 - ADAM_B1) * g
    v = ADAM_B2 * v + (1.0 - ADAM_B2) * (g * g)
    m_hat = m / (1.0 - ADAM_B1 ** ADAM_STEP)
    v_hat = v / (1.0 - ADAM_B2 ** ADAM_STEP)
    delta = -ADAM_LR * (m_hat / (jnp.sqrt(v_hat) + ADAM_EPS) + ADAM_WD * w)
    return delta, m, v


def _adamw(name, mine, theirs, w, m, v, idx, tr):
    rows, cc = w.shape
    nb = rows // 2 // tr

    def body(idx_ref, a_ref, b_ref, w_ref, m_ref, v_ref, g_out, d_out, m_out, v_out):
        g = jnp.where(pl.program_id(0) // nb == idx_ref[0], a_ref[...], b_ref[...])
        g_out[...] = g
        d_out[...], m_out[...], v_out[...] = _adamw_math(w_ref[...], g, m_ref[...], v_ref[...])

    hspec = pl.BlockSpec((tr, cc), lambda i, idx: (i % nb, 0))
    spec = pl.BlockSpec((tr, cc), lambda i, idx: (i, 0))
    return pl.pallas_call(
        body, name=name,
        grid_spec=pltpu.PrefetchScalarGridSpec(
            num_scalar_prefetch=1, grid=(rows // tr,),
            in_specs=[hspec, hspec, spec, spec, spec], out_specs=[spec] * 4),
        out_shape=[jax.ShapeDtypeStruct((rows, cc), F32)] * 4,
        compiler_params=_params(("parallel",)),
    )(idx, mine, theirs, w, m, v)


SMALL_NAMES = ("norm_mix_g", "lb_logits", "hgrn_norm_g", "w_pool", "pool_scale", "norm_x_g", "norm_mem_g",
               "norm_ffn_g", "final_norm_g")


def _small_update(slab, d_wp, ws, ms, vs):
    n = len(SMALL_NAMES)
    half = D_MODEL // 2

    def body(slab_ref, wp_ref, *refs):
        w_refs, m_refs, v_refs, outs = refs[:n], refs[n:2 * n], refs[2 * n:3 * n], refs[3 * n:]
        row = lambda k: slab_ref[k:k + 1, :]
        lbl = w_refs[SMALL_NAMES.index("lb_logits")][...]
        s0 = _lower_bound(lbl[0:1, :], lbl[1:2, :])
        dl0 = row(ROW_LB_HGN)[:, :half] * s0 * (1.0 - s0)
        grads = dict(norm_mix_g=row(ROW_GMIX), lb_logits=jnp.concatenate([dl0, -dl0], axis=0),
                     hgrn_norm_g=row(ROW_LB_HGN)[:, half:], w_pool=wp_ref[...], pool_scale=row(ROW_PSCALE)[:, :half],
                     norm_x_g=row(ROW_GX), norm_mem_g=row(ROW_GMEM), norm_ffn_g=row(ROW_GFFN),
                     final_norm_g=row(ROW_GFIN))
        outs[0][...] = row(ROW_LOSS)[:, :128]
        for i, name in enumerate(SMALL_NAMES):
            g = grads[name]
            delta, m2, v2 = _adamw_math(w_refs[i][...], g, m_refs[i][...], v_refs[i][...])
            for o, val in zip(outs[1 + 4 * i:5 + 4 * i], (g, delta, m2, v2)):
                o[...] = val

    args = [ws[k] for k in SMALL_NAMES] + [ms[k] for k in SMALL_NAMES] + [vs[k] for k in SMALL_NAMES]
    out_shape = [jax.ShapeDtypeStruct((1, 128), F32)]
    for k in SMALL_NAMES:
        out_shape += [jax.ShapeDtypeStruct(ws[k].shape, F32)] * 4
    res = pl.pallas_call(
        body, name="small_update",
        in_specs=[VMEM] * (2 + 3 * n), out_specs=[VMEM] * len(out_shape), out_shape=out_shape,
    )(slab, d_wp, *args)
    return res[0], {k: res[1 + 4 * i:5 + 4 * i] for i, k in enumerate(SMALL_NAMES)}


BIG_NAMES = ("w_in", "w_out", "w_xq", "w_xk", "w_xv", "w_xo", "w_ff1", "w_ff2")
ALL_NAMES = ("norm_mix_g", "w_in", "lb_logits", "hgrn_norm_g", "w_pool", "pool_scale", "w_out", "norm_x_g",
             "norm_mem_g", "w_xq", "w_xk", "w_xv", "w_xo", "norm_ffn_g", "w_ff1", "w_ff2", "final_norm_g")


def _shard_2d(name, a):
    a = a[0]
    if name in ("w_xq", "w_xk", "w_xv"):
        return a.reshape(a.shape[0], -1)
    if name == "w_xo":
        return a.reshape(-1, a.shape[-1])
    return a


def _small_2d(name, a):
    if name == "w_pool":
        return a.reshape(-1, HEAD_DIM)
    if name == "lb_logits":
        return a
    return a.reshape(1, -1)


def kernel(x, mem, norm_mix_g, w_in, lb_logits, hgrn_norm_g, w_pool, pool_scale, w_out, norm_x_g, norm_mem_g, w_xq, w_xk, w_xv, w_xo, norm_ffn_g, w_ff1, w_ff2, final_norm_g, loss_target, m_norm_mix_g, m_w_in, m_lb_logits, m_hgrn_norm_g, m_w_pool, m_pool_scale, m_w_out, m_norm_x_g, m_norm_mem_g, m_w_xq, m_w_xk, m_w_xv, m_w_xo, m_norm_ffn_g, m_w_ff1, m_w_ff2, m_final_norm_g, v_norm_mix_g, v_w_in, v_lb_logits, v_hgrn_norm_g, v_w_pool, v_pool_scale, v_w_out, v_norm_x_g, v_norm_mem_g, v_w_xq, v_w_xk, v_w_xv, v_w_xo, v_norm_ffn_g, v_w_ff1, v_w_ff2, v_final_norm_g):
    w = dict(norm_mix_g=norm_mix_g, w_in=w_in, lb_logits=lb_logits, hgrn_norm_g=hgrn_norm_g, w_pool=w_pool, pool_scale=pool_scale, w_out=w_out, norm_x_g=norm_x_g, norm_mem_g=norm_mem_g, w_xq=w_xq, w_xk=w_xk, w_xv=w_xv, w_xo=w_xo, norm_ffn_g=norm_ffn_g, w_ff1=w_ff1, w_ff2=w_ff2, final_norm_g=final_norm_g)
    m = dict(norm_mix_g=m_norm_mix_g, w_in=m_w_in, lb_logits=m_lb_logits, hgrn_norm_g=m_hgrn_norm_g, w_pool=m_w_pool, pool_scale=m_pool_scale, w_out=m_w_out, norm_x_g=m_norm_x_g, norm_mem_g=m_norm_mem_g, w_xq=m_w_xq, w_xk=m_w_xk, w_xv=m_w_xv, w_xo=m_w_xo, norm_ffn_g=m_norm_ffn_g, w_ff1=m_w_ff1, w_ff2=m_w_ff2, final_norm_g=m_final_norm_g)
    v = dict(norm_mix_g=v_norm_mix_g, w_in=v_w_in, lb_logits=v_lb_logits, hgrn_norm_g=v_hgrn_norm_g, w_pool=v_w_pool, pool_scale=v_pool_scale, w_out=v_w_out, norm_x_g=v_norm_x_g, norm_mem_g=v_norm_mem_g, w_xq=v_w_xq, w_xk=v_w_xk, w_xv=v_w_xv, w_xo=v_w_xo, norm_ffn_g=v_norm_ffn_g, w_ff1=v_w_ff1, w_ff2=v_w_ff2, final_norm_g=v_final_norm_g)

    big_w = {k: _shard_2d(k, w[k]) for k in BIG_NAMES}
    slab = jnp.concatenate([big_w[k] for k in ("w_out", "w_xq", "w_xk", "w_xv", "w_ff1", "w_ff2")], axis=0).astype(BF16)
    slab_g, win_g, wo_g = _all_gather_weights([slab, big_w["w_in"].astype(BF16), big_w["w_xo"].astype(BF16)])

    small = {k: w[k] for k in SMALL_NAMES}
    loc = _local_step(x[0], mem[0], loss_target[0], small, slab_g, win_g, wo_g)

    cx, cy, cc = lax.axis_index("x"), lax.axis_index("y"), lax.axis_index("c")
    chip = 2 * cx + cy
    idx = jnp.stack([cc, chip, chip ^ 1, chip ^ 2, chip ^ 3]).astype(jnp.int32)
    gs = [loc["d" + k] for k in BIG_NAMES]
    gs = [g.reshape(N_CHIPS, 2, g.shape[1] // 2, g.shape[2]) for g in gs]
    got = _grad_pair_exchange(gs)
    keeps, sends = zip(*[_grad_pair_add("grad_pair_add_" + k, g, r, idx, tr=128) for k, g, r in zip(BIG_NAMES, gs, got)])
    recvd = _grad_chip_exchange(list(sends))
    halves = [_grad_chip_add("grad_chip_add_" + k, kp, r, tr=128) for k, kp, r in zip(BIG_NAMES, keeps, recvd)]
    theirs = _grad_half_exchange(halves)

    grads, deltas, new_m, new_v = {}, {}, {}, {}
    for k, mine, other in zip(BIG_NAMES, halves, theirs):
        res = _adamw("adamw_" + k, mine, other, big_w[k], _shard_2d(k, m[k]), _shard_2d(k, v[k]), idx, tr=128)
        for store, val in zip((grads, deltas, new_m, new_v), res):
            store[k] = val.reshape(w[k].shape)

    slab_sum, wp_sum = _small_allreduce(loc["stats"], loc["d_w_pool"].reshape(-1, HEAD_DIM))
    loss, upd = _small_update(slab_sum, wp_sum, {k: _small_2d(k, w[k]) for k in SMALL_NAMES},
                              {k: _small_2d(k, m[k]) for k in SMALL_NAMES}, {k: _small_2d(k, v[k]) for k in SMALL_NAMES})
    for k in SMALL_NAMES:
        for store, val in zip((grads, deltas, new_m, new_v), upd[k]):
            store[k] = val.reshape(w[k].shape)

    return (loss[0, 0], loc["grad_x"][None], *[grads[k] for k in ALL_NAMES], *[deltas[k] for k in ALL_NAMES],
            *[new_m[k] for k in ALL_NAMES], *[new_v[k] for k in ALL_NAMES])
```

```python
import functools

import jax
import jax.numpy as jnp
from jax import lax
from jax.experimental import pallas as pl
from jax.experimental.pallas import tpu as pltpu

F32 = jnp.float32
BF16 = jnp.bfloat16
HIGHEST = lax.Precision.HIGHEST
MESH = pl.DeviceIdType.MESH
ANY = pl.BlockSpec(memory_space=pl.ANY)
VMEM = pl.BlockSpec(memory_space=pltpu.VMEM)

D_MODEL = 1024
N_CHIPS = 4
HGRN_HEADS = 4
HEAD_DIM = 128
HGRN_WIDTH = HGRN_HEADS * HEAD_DIM
POOL_WINDOWS = (2, 4, 8, 16)
POOL_HALO = 16
SUB = 16
XATTN_HEADS = 4
XATTN_HEAD_DIM = 256
EPS = 1e-6
ADAM_LR, ADAM_B1, ADAM_B2, ADAM_EPS, ADAM_WD, ADAM_STEP = 0.001, 0.9, 0.999, 1e-08, 0.01, 10

V7X_VMEM_BYTES = 64 * 1024 * 1024
VMEM_LIMIT = V7X_VMEM_BYTES - 8 * 1024 * 1024

NN = (((1,), (0,)), ((), ()))
NT = (((1,), (1,)), ((), ()))
TN = (((0,), (0,)), ((), ()))

ROW_GMIX, ROW_GX, ROW_GMEM, ROW_GFFN, ROW_GFIN, ROW_LB_HGN, ROW_PSCALE, ROW_LOSS = range(8)


def _dot(a, b, dims=NN):
    return lax.dot_general(a, b, dims, preferred_element_type=F32)


def _sigmoid(x):
    return 1.0 / (1.0 + jnp.exp(-x))


def _rms_fwd(x, g):
    r = lax.rsqrt(jnp.mean(x * x, axis=-1, keepdims=True) + EPS)
    n = x * r
    return n * g, n, r


def _rms_bwd(dh, n, r, g):
    dn = dh * g
    dx = r * (dn - n * jnp.mean(dn * n, axis=-1, keepdims=True))
    return dx, jnp.sum(dh * n, axis=0, keepdims=True)


def _params(sem=None):
    return pltpu.CompilerParams(dimension_semantics=sem, vmem_limit_bytes=VMEM_LIMIT)


def _const(shape):
    nd = len(shape)
    return pl.BlockSpec(shape, lambda *_: (0,) * nd, pipeline_mode=pl.Buffered(1))


def _const_out(shape):
    nd = len(shape)
    return pl.BlockSpec(shape, lambda *_: (0,) * nd)


def _acc_rows(ref, t, rows):
    upd = jnp.concatenate(rows + [jnp.zeros((8 - len(rows), rows[0].shape[1]), F32)], axis=0)

    @pl.when(t == 0)
    def _():
        ref[...] = upd

    @pl.when(t > 0)
    def _():
        ref[...] = ref[...] + upd


def _fuse_exchange(body, n_in, n_out, n_scratch, plan):
    if plan is None:
        return body
    n = plan.n

    def wrapped(*refs):
        ins, cin = refs[:n_in], refs[n_in:n_in + n]
        outs, cout = refs[n_in + n:n_in + n + n_out], refs[n_in + n + n_out:n_in + 2 * n + n_out]
        rest = refs[n_in + 2 * n + n_out:]
        scr, csem = rest[:n_scratch], rest[n_scratch:]
        ids = [(pl.program_id(i), pl.num_programs(i)) for i in range(2)]

        @pl.when((ids[0][0] == 0) & (ids[1][0] == 0))
        def _():
            plan.start(cin, cout, csem)

        body(*ins, *outs, *scr)

        @pl.when((ids[0][0] == ids[0][1] - 1) & (ids[1][0] == ids[1][1] - 1))
        def _():
            plan.finish(cin, cout, csem)

    return wrapped


def _plan_extras(plan):
    if plan is None:
        return [], [], []
    return [ANY] * plan.n, list(plan.out_shape), list(plan.scratch_shapes)


def _in_proj(x, g, win_g, tm):
    s, d = x.shape
    nsh, _, wc = win_g.shape

    def body(x_ref, g_ref, w_ref, z_ref, h_ref):
        h, _, _ = _rms_fwd(x_ref[...], g_ref[...])
        hb = h.astype(BF16)
        h_ref[...] = hb
        for j in range(nsh):
            z_ref[:, j * wc:(j + 1) * wc] = _dot(hb, w_ref[j])

    return pl.pallas_call(
        body, name="in_proj", grid=(s // tm,),
        in_specs=[pl.BlockSpec((tm, d), lambda t: (t, 0)), _const((1, d)), _const((nsh, d, wc))],
        out_specs=[pl.BlockSpec((tm, nsh * wc), lambda t: (t, 0)), pl.BlockSpec((tm, d), lambda t: (t, 0))],
        out_shape=[jax.ShapeDtypeStruct((s, nsh * wc), F32), jax.ShapeDtypeStruct((s, d), BF16)],
        compiler_params=_params(("parallel",)),
    )(x, g, win_g)


def _lower_bound(l0, l1):
    m = jnp.maximum(l0, l1)
    e0, e1 = jnp.exp(l0 - m), jnp.exp(l1 - m)
    return e0 / (e0 + e1)


def _block_tri(n, upper):
    r = lax.broadcasted_iota(jnp.int32, (n, n), 0)
    c = lax.broadcasted_iota(jnp.int32, (n, n), 1)
    keep = (r // SUB == c // SUB) & ((c >= r) if upper else (c <= r))
    return keep.astype(F32)


def _hgrn_gates(qp, fp, lb):
    sq = _sigmoid(qp)
    sf = _sigmoid(fp)
    f = lb + (1.0 - lb) * sf
    return qp * sq, sq, f, sf


def _hgrn_fwd(z, l0, l1, gn, tc, unroll=1, plan=None, plan_args=()):
    s = z.shape[0]
    nsub = tc // SUB
    hd = HEAD_DIM

    def body(q_ref, f_ref, v_ref, g_ref, l0_ref, l1_ref, gn_ref, o_ref, oa_ref, st_ref, state, qs, ks, bs, os_):
        @pl.when(pl.program_id(1) == 0)
        def _():
            state[...] = jnp.zeros_like(state)

        lb = _lower_bound(l0_ref[...], l1_ref[...])
        q, _, f, _ = _hgrn_gates(q_ref[...], f_ref[...], lb)
        qs[...] = q
        ks[...] = 1.0 - f
        bs[...] = jnp.dot(_block_tri(tc, False), jnp.log(f), precision=HIGHEST, preferred_element_type=F32)
        rows = lax.broadcasted_iota(jnp.int32, (SUB, 1), 0)

        def step(i, carry):
            r0 = pl.multiple_of(i * SUB, SUB)
            q_ = qs[pl.ds(r0, SUB), :]
            k_ = ks[pl.ds(r0, SUB), :]
            b_ = bs[pl.ds(r0, SUB), :]
            v_ = v_ref[pl.ds(r0, SUB), :]
            st = state[...]
            st_ref[i] = st
            bl = b_[SUB - 1:SUB, :]
            o = _dot((q_ * jnp.exp(b_)).astype(BF16), st.astype(BF16), NT)
            for j in range(SUB):
                e = jnp.exp(jnp.minimum(b_ - b_[j:j + 1, :], 0.0))
                col = jnp.sum(q_ * e * k_[j:j + 1, :], axis=-1, keepdims=True)
                o = o + jnp.where(rows >= j, col, 0.0) * v_[j:j + 1, :]
            os_[pl.ds(r0, SUB), :] = o
            kt = (k_ * jnp.exp(bl - b_)).astype(BF16)
            state[...] = st * jnp.exp(bl) + _dot(v_.astype(BF16), kt, TN)
            return carry

        lax.fori_loop(0, nsub, step, 0, unroll=unroll)
        o = os_[...]
        o_ref[...] = o
        r = lax.rsqrt(jnp.mean(o * o, axis=-1, keepdims=True) + EPS)
        gp = g_ref[...]
        oa_ref[...] = (o * r * gn_ref[...] * (gp * _sigmoid(gp))).astype(BF16)

    col = lambda k: pl.BlockSpec((tc, hd), lambda h, t: (t, k * HGRN_HEADS + h))
    vec = pl.BlockSpec((None, 1, hd), lambda h, t: (h, 0, 0))
    x_specs, x_shapes, x_scratch = _plan_extras(plan)
    return pl.pallas_call(
        _fuse_exchange(body, 7, 3, 5, plan), name="hgrn_fwd", grid=(HGRN_HEADS, s // tc),
        in_specs=[col(0), col(1), col(2), col(3), vec, vec, vec] + x_specs,
        out_specs=[pl.BlockSpec((tc, hd), lambda h, t: (t, h)), pl.BlockSpec((tc, hd), lambda h, t: (t, h)),
                   pl.BlockSpec((None, nsub, hd, hd), lambda h, t: (h, t, 0, 0))] + x_specs,
        out_shape=[jax.ShapeDtypeStruct((s, HGRN_WIDTH), F32), jax.ShapeDtypeStruct((s, HGRN_WIDTH), BF16),
                   jax.ShapeDtypeStruct((HGRN_HEADS, s // SUB, hd, hd), F32)] + x_shapes,
        scratch_shapes=[pltpu.VMEM((hd, hd), F32)] + [pltpu.VMEM((tc, hd), F32)] * 4 + x_scratch,
        compiler_params=_params(("arbitrary", "arbitrary")),
    )(z, z, z, z, l0, l1, gn, *plan_args)


def _pooled(p, ext, tok0):
    tm = p.shape[0]
    tok = tok0 + lax.broadcasted_iota(jnp.int32, (tm, 1), 0)
    outs = []
    for g, w in enumerate(POOL_WINDOWS):
        acc = ext[:, g * HEAD_DIM:(g + 1) * HEAD_DIM]
        sh = 1
        while sh < w:
            acc = acc + pltpu.roll(acc, sh, axis=0)
            sh *= 2
        cnt = jnp.minimum(tok + 1, w).astype(F32)
        outs.append(acc[POOL_HALO:, :] / cnt - p[:, g * HEAD_DIM:(g + 1) * HEAD_DIM])
    return outs


def _pool_fwd(z, wp, scale, tm):
    s = z.shape[0]
    pw = len(POOL_WINDOWS) * HEAD_DIM
    nb = tm // POOL_HALO

    def body(p_ref, prev_ref, wp_ref, sc_ref, ob_ref):
        t = pl.program_id(0)
        p = p_ref[...]
        prev = jnp.where(t > 0, prev_ref[...], 0.0)
        pooled = _pooled(p, jnp.concatenate([prev, p], axis=0), t * tm)
        ys = [_dot(pooled[g].astype(BF16), wp_ref[g].astype(BF16)) for g in range(len(POOL_WINDOWS))]
        ob_ref[...] = (jnp.concatenate(ys, axis=1) * sc_ref[...]).astype(BF16)

    return pl.pallas_call(
        body, name="pool_fwd", grid=(s // tm,),
        in_specs=[pl.BlockSpec((tm, pw), lambda t: (t, 4)),
                  pl.BlockSpec((POOL_HALO, pw), lambda t: (jnp.maximum(t * nb - 1, 0), 4)),
                  _const(wp.shape), _const((1, pw))],
        out_specs=pl.BlockSpec((tm, pw), lambda t: (t, 0)),
        out_shape=jax.ShapeDtypeStruct((s, pw), BF16),
        compiler_params=_params(("parallel",)),
    )(z, z, wp, scale)


def _kv_proj(mem, g, slab_g):
    m, d = mem.shape
    rows = d // N_CHIPS

    def body(mem_ref, g_ref, wk_ref, wv_ref, xk_ref, xv_ref):
        hm, _, _ = _rms_fwd(mem_ref[...], g_ref[...])
        hb = hm.astype(BF16)
        xk_ref[...] = _dot(hb, wk_ref[...].reshape(d, d)).astype(BF16)
        xv_ref[...] = _dot(hb, wv_ref[...].reshape(d, d)).astype(BF16)

    blk = lambda k: pl.BlockSpec((N_CHIPS, rows, d), lambda i: (0, k, 0))
    return pl.pallas_call(
        body, name="kv_proj", grid=(1,),
        in_specs=[_const((m, d)), _const((1, d)), blk(2), blk(3)],
        out_specs=[_const_out((m, d)), _const_out((m, d))],
        out_shape=[jax.ShapeDtypeStruct((m, d), BF16)] * 2,
        compiler_params=_params(("arbitrary",)),
    )(mem, g, slab_g, slab_g)


def _softmax_rows(sc):
    e = jnp.exp(sc - jnp.max(sc, axis=-1, keepdims=True))
    return e / jnp.sum(e, axis=-1, keepdims=True)


def _mix_xattn_fwd(x, oa, ob, gx, slab_g, wo_g, xk, xv, tm):
    s, d = x.shape
    m = xk.shape[0]
    rows = d // N_CHIPS
    hw = oa.shape[1]
    e = XATTN_HEAD_DIM

    def body(x_ref, oa_ref, ob_ref, gx_ref, wout_ref, wq_ref, wo_ref, xk_ref, xv_ref,
             x1_ref, mixed_ref, hq_ref, xq_ref, att_ref, x2_ref):
        mixed = jnp.concatenate([oa_ref[...], ob_ref[...]], axis=1)
        mixed_ref[...] = mixed
        x1 = x_ref[...] + _dot(mixed, wout_ref[...].reshape(d, d))
        x1_ref[...] = x1
        hq, _, _ = _rms_fwd(x1, gx_ref[...])
        hqb = hq.astype(BF16)
        hq_ref[...] = hqb
        xq = _dot(hqb, wq_ref[...].reshape(d, d)).astype(BF16)
        xq_ref[...] = xq
        atts = []
        for h in range(XATTN_HEADS):
            cs = slice(h * e, (h + 1) * e)
            p = _softmax_rows(_dot(xq[:, cs], xk_ref[:, cs], NT) * (e ** -0.5))
            atts.append(_dot(p.astype(BF16), xv_ref[:, cs]).astype(BF16))
        att = jnp.concatenate(atts, axis=1)
        att_ref[...] = att
        for j in range(N_CHIPS):
            x2_ref[:, j * rows:(j + 1) * rows] = x1[:, j * rows:(j + 1) * rows] + _dot(att, wo_ref[j])

    tile = lambda w: pl.BlockSpec((tm, w), lambda t: (t, 0))
    blk = lambda k: pl.BlockSpec((N_CHIPS, rows, d), lambda t: (0, k, 0), pipeline_mode=pl.Buffered(1))
    return pl.pallas_call(
        body, name="mix_xattn_fwd", grid=(s // tm,),
        in_specs=[tile(d), tile(hw), tile(hw), _const((1, d)), blk(0), blk(1), _const(wo_g.shape),
                  _const((m, d)), _const((m, d))],
        out_specs=[tile(d)] * 6,
        out_shape=[jax.ShapeDtypeStruct((s, d), F32)] + [jax.ShapeDtypeStruct((s, d), BF16)] * 4
                  + [jax.ShapeDtypeStruct((s, d), F32)],
        compiler_params=_params(("parallel",)),
    )(x, oa, ob, gx, slab_g, slab_g, wo_g, xk, xv)


def _mlp_loss_fwd(x2, gffn, gfin, slab_g, target, tm):
    s, d = x2.shape
    wr = slab_g.shape[1] // 3

    def body(x2_ref, gffn_ref, gfin_ref, w1_ref, w2_ref, tg_ref, a_ref, hf_ref, dx3_ref, dx3b_ref, st_ref):
        x2v = x2_ref[...]
        hf, _, _ = _rms_fwd(x2v, gffn_ref[...])
        hfb = hf.astype(BF16)
        hf_ref[...] = hfb
        acc = x2v
        for j in range(N_CHIPS):
            a = _dot(hfb, w1_ref[j])
            a_ref[:, j * wr:(j + 1) * wr] = a
            r = jnp.maximum(a, 0.0)
            acc = acc + _dot((r * r).astype(BF16), w2_ref[j])
        gf = gfin_ref[...]
        y, n, r3 = _rms_fwd(acc, gf)
        err = y - tg_ref[...]
        loss = 0.5 * jnp.sum(jnp.sum(err * err, axis=-1, keepdims=True) * (1.0 / d), axis=0, keepdims=True)
        dy = err * (1.0 / d)
        dx3, dgf = _rms_bwd(dy, n, r3, gf)
        dx3_ref[...] = dx3
        dx3b_ref[...] = dx3.astype(BF16)
        _acc_rows(st_ref, pl.program_id(0), [dgf, jnp.broadcast_to(loss, (1, d))])

    tile = lambda w: pl.BlockSpec((tm, w), lambda t: (t, 0))
    blk = lambda k: pl.BlockSpec((N_CHIPS, wr, d), lambda t: (0, k, 0), pipeline_mode=pl.Buffered(1))
    return pl.pallas_call(
        body, name="mlp_loss_fwd", grid=(s // tm,),
        in_specs=[tile(d), _const((1, d)), _const((1, d)), blk(1), blk(2), tile(d)],
        out_specs=[tile(N_CHIPS * wr), tile(d), tile(d), tile(d), _const_out((8, d))],
        out_shape=[jax.ShapeDtypeStruct((s, N_CHIPS * wr), F32), jax.ShapeDtypeStruct((s, d), BF16),
                   jax.ShapeDtypeStruct((s, d), F32), jax.ShapeDtypeStruct((s, d), BF16),
                   jax.ShapeDtypeStruct((8, d), F32)],
        compiler_params=_params(("arbitrary",)),
    )(x2, gffn, gfin, slab_g, slab_g, target)


def _mlp_bwd(dx3, dx3b, a, x2, gffn, slab_g, tm):
    s, d = x2.shape
    wr = slab_g.shape[1] // 3

    def body(dx3_ref, dx3b_ref, a_ref, x2_ref, g_ref, w1_ref, w2_ref, da_ref, u_ref, dx2_ref, dx2b_ref, st_ref):
        dyb = dx3b_ref[...]
        dhf = jnp.zeros((tm, d), F32)
        for j in range(N_CHIPS):
            r = jnp.maximum(a_ref[:, j * wr:(j + 1) * wr], 0.0)
            da = (_dot(dyb, w2_ref[j], NT) * (2.0 * r)).astype(BF16)
            da_ref[:, j * wr:(j + 1) * wr] = da
            u_ref[:, j * wr:(j + 1) * wr] = (r * r).astype(BF16)
            dhf = dhf + _dot(da, w1_ref[j], NT)
        g = g_ref[...]
        _, n, r2 = _rms_fwd(x2_ref[...], g)
        dxn, dg = _rms_bwd(dhf, n, r2, g)
        dx2 = dx3_ref[...] + dxn
        dx2_ref[...] = dx2
        dx2b_ref[...] = dx2.astype(BF16)
        _acc_rows(st_ref, pl.program_id(0), [dg])

    tile = lambda w: pl.BlockSpec((tm, w), lambda t: (t, 0))
    blk = lambda k: pl.BlockSpec((N_CHIPS, wr, d), lambda t: (0, k, 0), pipeline_mode=pl.Buffered(1))
    nf = N_CHIPS * wr
    return pl.pallas_call(
        body, name="mlp_bwd", grid=(s // tm,),
        in_specs=[tile(d), tile(d), tile(nf), tile(d), _const((1, d)), blk(1), blk(2)],
        out_specs=[tile(nf), tile(nf), tile(d), tile(d), _const_out((8, d))],
        out_shape=[jax.ShapeDtypeStruct((s, nf), BF16), jax.ShapeDtypeStruct((s, nf), BF16),
                   jax.ShapeDtypeStruct((s, d), F32), jax.ShapeDtypeStruct((s, d), BF16),
                   jax.ShapeDtypeStruct((8, d), F32)],
        compiler_params=_params(("arbitrary",)),
    )(dx3, dx3b, a, x2, gffn, slab_g, slab_g)


def _xattn_mix_bwd(dx2, x1, xq, xk, xv, gx, slab_g, wo_g, tm):
    s, d = x1.shape
    m = xk.shape[0]
    rows = d // N_CHIPS
    e = XATTN_HEAD_DIM

    def body(dx2_ref, x1_ref, xq_ref, xk_ref, xv_ref, gx_ref, wout_ref, wq_ref, wo_ref,
             dx1_ref, dx1b_ref, dxq_ref, dmix_ref, dxk_ref, dxv_ref, st_ref):
        t = pl.program_id(0)
        dx2 = dx2_ref[...]
        dx2b = dx2.astype(BF16)
        datt = jnp.zeros((tm, d), F32)
        for j in range(N_CHIPS):
            datt = datt + _dot(dx2b[:, j * rows:(j + 1) * rows], wo_ref[j], NT)
        dattb = datt.astype(BF16)
        dxqs, dxks, dxvs = [], [], []
        for h in range(XATTN_HEADS):
            cs = slice(h * e, (h + 1) * e)
            xq_h, xk_h, xv_h = xq_ref[:, cs], xk_ref[:, cs], xv_ref[:, cs]
            p = _softmax_rows(_dot(xq_h, xk_h, NT) * (e ** -0.5))
            dp = _dot(dattb[:, cs], xv_h, NT)
            ds = (p * (dp - jnp.sum(dp * p, axis=-1, keepdims=True)) * (e ** -0.5)).astype(BF16)
            dxqs.append(_dot(ds, xk_h).astype(BF16))
            dxks.append(_dot(ds, xq_h, TN))
            dxvs.append(_dot(p.astype(BF16), dattb[:, cs], TN))
        dxq = jnp.concatenate(dxqs, axis=1)
        dxq_ref[...] = dxq
        dxk = jnp.concatenate(dxks, axis=1)
        dxv = jnp.concatenate(dxvs, axis=1)

        @pl.when(t == 0)
        def _():
            dxk_ref[...] = dxk
            dxv_ref[...] = dxv

        @pl.when(t > 0)
        def _():
            dxk_ref[...] = dxk_ref[...] + dxk
            dxv_ref[...] = dxv_ref[...] + dxv

        dhq = jnp.concatenate([_dot(dxq, wq_ref[j], NT) for j in range(N_CHIPS)], axis=1)
        g = gx_ref[...]
        _, n, r1 = _rms_fwd(x1_ref[...], g)
        dxn, dg = _rms_bwd(dhq, n, r1, g)
        dx1 = dx2 + dxn
        dx1_ref[...] = dx1
        dx1b = dx1.astype(BF16)
        dx1b_ref[...] = dx1b
        for j in range(N_CHIPS):
            dmix_ref[:, j * rows:(j + 1) * rows] = _dot(dx1b, wout_ref[j], NT)
        _acc_rows(st_ref, t, [dg])

    tile = lambda: pl.BlockSpec((tm, d), lambda t: (t, 0))
    blk = lambda k: pl.BlockSpec((N_CHIPS, rows, d), lambda t: (0, k, 0), pipeline_mode=pl.Buffered(1))
    return pl.pallas_call(
        body, name="xattn_mix_bwd", grid=(s // tm,),
        in_specs=[tile(), tile(), tile(), _const((m, d)), _const((m, d)), _const((1, d)), blk(0), blk(1),
                  _const(wo_g.shape)],
        out_specs=[tile(), tile(), tile(), tile(), _const_out((m, d)), _const_out((m, d)), _const_out((8, d))],
        out_shape=[jax.ShapeDtypeStruct((s, d), F32), jax.ShapeDtypeStruct((s, d), BF16),
                   jax.ShapeDtypeStruct((s, d), BF16), jax.ShapeDtypeStruct((s, d), F32),
                   jax.ShapeDtypeStruct((m, d), F32), jax.ShapeDtypeStruct((m, d), F32),
                   jax.ShapeDtypeStruct((8, d), F32)],
        compiler_params=_params(("arbitrary",)),
    )(dx2, x1, xq, xk, xv, gx, slab_g, slab_g, wo_g)


def _kv_bwd(mem, g, dxk, dxv, slab_g):
    m, d = mem.shape
    rows = d // N_CHIPS

    def body(mem_ref, g_ref, dxk_ref, dxv_ref, wk_ref, wv_ref, dwk_ref, dwv_ref, st_ref):
        gv = g_ref[...]
        hm, n, _ = _rms_fwd(mem_ref[...], gv)
        hb = hm.astype(BF16)
        dkb = dxk_ref[...].astype(BF16)
        dvb = dxv_ref[...].astype(BF16)
        dhm = []
        for j in range(N_CHIPS):
            hj = hb[:, j * rows:(j + 1) * rows]
            dwk_ref[j] = _dot(hj, dkb, TN)
            dwv_ref[j] = _dot(hj, dvb, TN)
            dhm.append(_dot(dkb, wk_ref[j], NT) + _dot(dvb, wv_ref[j], NT))
        dg = jnp.sum(jnp.concatenate(dhm, axis=1) * n, axis=0, keepdims=True)
        st_ref[...] = jnp.concatenate([dg, jnp.zeros((7, d), F32)], axis=0)

    blk = lambda k: pl.BlockSpec((N_CHIPS, rows, d), lambda i: (0, k, 0))
    return pl.pallas_call(
        body, name="kv_bwd", grid=(1,),
        in_specs=[_const((m, d)), _const((1, d)), _const((m, d)), _const((m, d)), blk(2), blk(3)],
        out_specs=[_const_out((N_CHIPS, rows, d)), _const_out((N_CHIPS, rows, d)), _const_out((8, d))],
        out_shape=[jax.ShapeDtypeStruct((N_CHIPS, rows, d), F32)] * 2 + [jax.ShapeDtypeStruct((8, d), F32)],
        compiler_params=_params(("arbitrary",)),
    )(mem, g, dxk, dxv, slab_g, slab_g)


def _pool_bwd(z, dmix, wp, scale, tm):
    s = z.shape[0]
    ng = len(POOL_WINDOWS)
    pw = ng * HEAD_DIM
    nb = tm // POOL_HALO
    nt = s // tm
    n_ext = tm + POOL_HALO

    def body(p_ref, prev_ref, dm_ref, dmn_ref, wp_ref, sc_ref, dp_ref, dwp_ref, st_ref):
        t = pl.program_id(0)
        p = p_ref[...]
        prev = jnp.where(t > 0, prev_ref[...], 0.0)
        pooled = _pooled(p, jnp.concatenate([prev, p], axis=0), t * tm)
        dm = dm_ref[...]
        dme = jnp.concatenate([dm, jnp.where(t < nt - 1, dmn_ref[...], 0.0)], axis=0) * sc_ref[...]
        tok = t * tm + lax.broadcasted_iota(jnp.int32, (n_ext, 1), 0)
        dsc, dps, dwps = [], [], []
        for g, w in enumerate(POOL_WINDOWS):
            cs = slice(g * HEAD_DIM, (g + 1) * HEAD_DIM)
            wpb = wp_ref[g].astype(BF16)
            pb = pooled[g].astype(BF16)
            dsc.append(jnp.sum(dm[:, cs] * _dot(pb, wpb), axis=0, keepdims=True))
            dye = dme[:, cs].astype(BF16)
            dwps.append(_dot(pb, dye[:tm], TN))
            dpe = _dot(dye, wpb, NT)
            acc = dpe / jnp.minimum(tok + 1, w).astype(F32)
            sh = 1
            while sh < w:
                acc = acc + pltpu.roll(acc, n_ext - sh, axis=0)
                sh *= 2
            dps.append(acc[:tm] - dpe[:tm])
        dp_ref[...] = jnp.concatenate(dps, axis=1)
        dsc_row = jnp.concatenate(dsc, axis=1)

        @pl.when(t == 0)
        def _():
            for g in range(ng):
                dwp_ref[g] = dwps[g]

        @pl.when(t > 0)
        def _():
            for g in range(ng):
                dwp_ref[g] = dwp_ref[g] + dwps[g]

        _acc_rows(st_ref, t, [dsc_row])

    return pl.pallas_call(
        body, name="pool_bwd", grid=(nt,),
        in_specs=[pl.BlockSpec((tm, pw), lambda t: (t, 4)),
                  pl.BlockSpec((POOL_HALO, pw), lambda t: (jnp.maximum(t * nb - 1, 0), 4)),
                  pl.BlockSpec((tm, pw), lambda t: (t, 1)),
                  pl.BlockSpec((POOL_HALO, pw), lambda t: (jnp.minimum((t + 1) * nb, s // POOL_HALO - 1), 1)),
                  _const(wp.shape), _const((1, pw))],
        out_specs=[pl.BlockSpec((tm, pw), lambda t: (t, 0)), _const_out(wp.shape), _const_out((8, pw))],
        out_shape=[jax.ShapeDtypeStruct((s, pw), F32), jax.ShapeDtypeStruct(wp.shape, F32),
                   jax.ShapeDtypeStruct((8, pw), F32)],
        compiler_params=_params(("arbitrary",)),
    )(z, z, dmix, dmix, wp, scale)


def _hgrn_bwd(z, o, dmix, st, l0, l1, gn, tc, unroll=1, plan=None, plan_args=()):
    s = z.shape[0]
    nsub = tc // SUB
    nt = s // tc
    hd = HEAD_DIM

    def body(q_ref, f_ref, v_ref, g_ref, l0_ref, l1_ref, gn_ref, o_ref, dm_ref, st_ref,
             dq_ref, df_ref, di_ref, dg_ref, stat_ref, dstate, qs, ks, bs, dos, dqs, dks, dbs):
        t = pl.program_id(1)

        @pl.when(t == 0)
        def _():
            dstate[...] = jnp.zeros_like(dstate)

        lb = _lower_bound(l0_ref[...], l1_ref[...])
        qp = q_ref[...]
        q, sq, f, sf = _hgrn_gates(qp, f_ref[...], lb)
        qs[...] = q
        ks[...] = 1.0 - f
        bs[...] = jnp.dot(_block_tri(tc, False), jnp.log(f), precision=HIGHEST, preferred_element_type=F32)

        o = o_ref[...]
        r = lax.rsqrt(jnp.mean(o * o, axis=-1, keepdims=True) + EPS)
        n = o * r
        gnv = gn_ref[...]
        gp = g_ref[...]
        sg = _sigmoid(gp)
        dm = dm_ref[...]
        dg_ref[...] = dm * (n * gnv) * (sg * (1.0 + gp * (1.0 - sg)))
        don = dm * (gp * sg)
        dgn = jnp.sum(don * n, axis=0, keepdims=True)
        dn = don * gnv
        dos[...] = r * (dn - n * jnp.mean(dn * n, axis=-1, keepdims=True))
        rows = lax.broadcasted_iota(jnp.int32, (SUB, 1), 0)

        def step(i, carry):
            ii = nsub - 1 - i
            r0 = pl.multiple_of(ii * SUB, SUB)
            q_ = qs[pl.ds(r0, SUB), :]
            k_ = ks[pl.ds(r0, SUB), :]
            b_ = bs[pl.ds(r0, SUB), :]
            v_ = v_ref[pl.ds(r0, SUB), :]
            do_ = dos[pl.ds(r0, SUB), :]
            stp = st_ref[ii]
            dst = dstate[...]
            bl = b_[SUB - 1:SUB, :]
            eb = jnp.exp(b_)
            ekl = jnp.exp(bl - b_)
            ebl = jnp.exp(bl)
            dob = do_.astype(BF16)
            dstb = dst.astype(BF16)
            kt = k_ * ekl
            dq = _dot(dob, stp.astype(BF16)) * eb
            dkt = _dot(v_.astype(BF16), dstb)
            dk = dkt * ekl
            dv = _dot(kt.astype(BF16), dstb, NT)
            extra = jnp.sum(kt * dkt, axis=0, keepdims=True) + ebl * jnp.sum(stp * dst, axis=0, keepdims=True)
            for j in range(SUB):
                e = jnp.exp(jnp.minimum(b_ - b_[j:j + 1, :], 0.0))
                pe = q_ * e
                kj = k_[j:j + 1, :]
                keep = rows >= j
                acol = jnp.where(keep, jnp.sum(pe * kj, axis=-1, keepdims=True), 0.0)
                dacol = jnp.where(keep, jnp.sum(do_ * v_[j:j + 1, :], axis=-1, keepdims=True), 0.0)
                dq = dq + dacol * (e * kj)
                at_j = rows == j
                dk = dk + jnp.where(at_j, jnp.sum(dacol * pe, axis=0, keepdims=True), 0.0)
                dv = dv + jnp.where(at_j, jnp.sum(acol * do_, axis=0, keepdims=True), 0.0)
            dqs[pl.ds(r0, SUB), :] = dq
            dks[pl.ds(r0, SUB), :] = dk
            di_ref[pl.ds(r0, SUB), :] = dv
            dbs[pl.ds(r0, SUB), :] = q_ * dq - k_ * dk + jnp.where(rows == SUB - 1, extra, 0.0)
            dstate[...] = dst * ebl + _dot(dob, (q_ * eb).astype(BF16), TN)
            return carry

        lax.fori_loop(0, nsub, step, 0, unroll=unroll)
        dlf = jnp.dot(_block_tri(tc, True), dbs[...], precision=HIGHEST, preferred_element_type=F32)
        dfv = dlf / f - dks[...]
        df_ref[...] = dfv * (1.0 - lb) * sf * (1.0 - sf)
        dlb = jnp.sum(dfv * (1.0 - sf), axis=0, keepdims=True)
        dq_ref[...] = dqs[...] * (sq * (1.0 + qp * (1.0 - sq)))
        _acc_rows(stat_ref, t, [dgn, dlb])

    rev = lambda t: nt - 1 - t
    col = lambda k: pl.BlockSpec((tc, hd), lambda h, t: (rev(t), k * HGRN_HEADS + h))
    vec = pl.BlockSpec((None, 1, hd), lambda h, t: (h, 0, 0))
    head = pl.BlockSpec((tc, hd), lambda h, t: (rev(t), h))
    x_specs, x_shapes, x_scratch = _plan_extras(plan)
    return pl.pallas_call(
        _fuse_exchange(body, 10, 5, 8, plan), name="hgrn_bwd", grid=(HGRN_HEADS, nt),
        in_specs=[col(0), col(1), col(2), col(3), vec, vec, vec, head, head,
                  pl.BlockSpec((None, nsub, hd, hd), lambda h, t: (h, rev(t), 0, 0))] + x_specs,
        out_specs=[head, head, head, head, pl.BlockSpec((None, 8, hd), lambda h, t: (h, 0, 0))] + x_specs,
        out_shape=[jax.ShapeDtypeStruct((s, HGRN_WIDTH), F32)] * 4 + [jax.ShapeDtypeStruct((HGRN_HEADS, 8, hd), F32)]
                  + x_shapes,
        scratch_shapes=[pltpu.VMEM((hd, hd), F32)] + [pltpu.VMEM((tc, hd), F32)] * 7 + x_scratch,
        compiler_params=_params(("arbitrary", "arbitrary")),
    )(z, z, z, z, l0, l1, gn, o, dmix, st, *plan_args)


def _in_bwd(dparts, dx1, x, g, win_g, tm):
    s, d = x.shape
    nsh, _, wc = win_g.shape
    pw = dparts[0].shape[1]

    def body(dq_ref, df_ref, di_ref, dg_ref, dp_ref, dx1_ref, x_ref, g_ref, w_ref, gx_ref, dz_ref, st_ref):
        dz = jnp.concatenate([dq_ref[...], df_ref[...], di_ref[...], dg_ref[...], dp_ref[...]], axis=1).astype(BF16)
        dz_ref[...] = dz
        dh = jnp.zeros((tm, d), F32)
        for j in range(nsh):
            dh = dh + _dot(dz[:, j * wc:(j + 1) * wc], w_ref[j], NT)
        gv = g_ref[...]
        _, n, r = _rms_fwd(x_ref[...], gv)
        dxn, dg = _rms_bwd(dh, n, r, gv)
        gx_ref[...] = dx1_ref[...] + dxn
        _acc_rows(st_ref, pl.program_id(0), [dg])

    tile = lambda w: pl.BlockSpec((tm, w), lambda t: (t, 0))
    return pl.pallas_call(
        body, name="in_bwd", grid=(s // tm,),
        in_specs=[tile(pw)] * 5 + [tile(d), tile(d), _const((1, d)), _const(win_g.shape)],
        out_specs=[tile(d), tile(nsh * wc), _const_out((8, d))],
        out_shape=[jax.ShapeDtypeStruct((s, d), F32), jax.ShapeDtypeStruct((s, nsh * wc), BF16),
                   jax.ShapeDtypeStruct((8, d), F32)],
        compiler_params=_params(("arbitrary",)),
    )(*dparts, dx1, x, g, win_g)


def _tn_grad(name, a, b, out_rows, out_cols, a_sharded, tr, tc):
    s = a.shape[0]
    nr, nc = out_rows // tr, out_cols // tc

    def body(a_ref, b_ref, o_ref):
        o_ref[...] = _dot(a_ref[...], b_ref[...], TN)

    a_map = (lambda j, i, k: (0, j * nr + i)) if a_sharded else (lambda j, i, k: (0, i))
    b_map = (lambda j, i, k: (0, k)) if a_sharded else (lambda j, i, k: (0, j * nc + k))
    return pl.pallas_call(
        body, name=name, grid=(N_CHIPS, nr, nc),
        in_specs=[pl.BlockSpec((s, tr), a_map), pl.BlockSpec((s, tc), b_map)],
        out_specs=pl.BlockSpec((None, tr, tc), lambda j, i, k: (j, i, k)),
        out_shape=jax.ShapeDtypeStruct((N_CHIPS, out_rows, out_cols), F32),
        compiler_params=_params(("parallel", "parallel", "parallel")),
    )(a, b)


EARLY_NAMES = ("w_out", "w_xq", "w_xk", "w_xv", "w_xo", "w_ff1", "w_ff2")
BIG_NAMES = EARLY_NAMES + ("w_in",)


def _reduce_to_chip_partials(names, grads, idx):
    gs = [g.reshape(N_CHIPS, 2, g.shape[1] // 2, g.shape[2]) for g in grads]
    got = _grad_pair_exchange("grad_pair_exchange_" + names[0], gs)
    pairs = [_grad_pair_add("grad_pair_add_" + k, g, r, idx, tr=128) for k, g, r in zip(names, gs, got)]
    return [p[0] for p in pairs], [p[1] for p in pairs]


def _step(x, mem, target, small, shards, idx):
    d = x.shape[1]
    l0 = small["lb_logits"][0].reshape(HGRN_HEADS, 1, HEAD_DIM)
    l1 = small["lb_logits"][1].reshape(HGRN_HEADS, 1, HEAD_DIM)
    gn = small["hgrn_norm_g"].reshape(HGRN_HEADS, 1, HEAD_DIM)
    wp = small["w_pool"].reshape(len(POOL_WINDOWS), HEAD_DIM, HEAD_DIM)
    psc = small["pool_scale"].reshape(1, -1)
    gmix, gx, gmem, gffn = (small[k].reshape(1, d) for k in ("norm_mix_g", "norm_x_g", "norm_mem_g", "norm_ffn_g"))
    gfin = small["final_norm_g"].reshape(1, d)

    (win_g,) = _run_exchange("gather_w_in", _WeightGather([shards["w_in"]]), [shards["w_in"]])
    z, h = _in_proj(x, gmix, win_g, tm=512)
    rest = [shards["slab"], shards["w_xo"]]
    o, oa, st, slab_g, wo_g = _hgrn_fwd(z, l0, l1, gn, tc=256, unroll=8, plan=_WeightGather(rest), plan_args=rest)
    ob = _pool_fwd(z, wp, psc, tm=512)
    xk, xv = _kv_proj(mem, gmem, slab_g)
    x1, mixed, hq, xq, att, x2 = _mix_xattn_fwd(x, oa, ob, gx, slab_g, wo_g, xk, xv, tm=256)
    a, hf, dx3, dx3b, st_loss = _mlp_loss_fwd(x2, gffn, gfin, slab_g, target, tm=256)

    da, u, dx2, dx2b, st_ffn = _mlp_bwd(dx3, dx3b, a, x2, gffn, slab_g, tm=256)
    dw = {}
    dw["w_ff1"] = _tn_grad("dw_ff1", hf, da, d, d, False, 512, 512)
    dw["w_ff2"] = _tn_grad("dw_ff2", u, dx3b, d, d, True, 512, 512)
    dx1, dx1b, dxq, dmix, dxk, dxv, st_x = _xattn_mix_bwd(dx2, x1, xq, xk, xv, gx, slab_g, wo_g, tm=256)
    dw["w_xo"] = _tn_grad("dw_xo", att, dx2b, d, d // N_CHIPS, False, 512, 256)
    dw["w_xq"] = _tn_grad("dw_xq", hq, dxq, d // N_CHIPS, d, True, 256, 512)
    dw["w_out"] = _tn_grad("dw_out", mixed, dx1b, d // N_CHIPS, d, True, 256, 512)
    dw["w_xk"], dw["w_xv"], st_mem = _kv_bwd(mem, gmem, dxk, dxv, slab_g)

    keeps, sends = _reduce_to_chip_partials(EARLY_NAMES, [dw[k] for k in EARLY_NAMES], idx)
    dp, d_wp, st_pool = _pool_bwd(z, dmix, wp, psc, tm=512)
    dq, df, di, dg, st_hgrn, *received = _hgrn_bwd(z, o, dmix, st, l0, l1, gn, tc=256, unroll=4,
                                                    plan=_ChipExchange(sends), plan_args=sends)
    grad_x, dz, st_mix = _in_bwd([dq, df, di, dg, dp], dx1, x, gmix, win_g, tm=256)
    dw_in = _tn_grad("dw_in", h, dz, d, win_g.shape[2], False, 512, win_g.shape[2])
    keep_in, send_in = _reduce_to_chip_partials(("w_in",), [dw_in], idx)
    recv_in = _run_exchange("grad_chip_exchange_w_in", _ChipExchange(send_in), send_in)

    partials = dict(zip(BIG_NAMES, zip(keeps + keep_in, list(received) + list(recv_in))))
    stats = dict(mix=st_mix, x=st_x, mem=st_mem, ffn=st_ffn, loss=st_loss, hgrn=st_hgrn, pool=st_pool)
    return grad_x, stats, d_wp, partials


def _place():
    x, y, c = lax.axis_index("x"), lax.axis_index("y"), lax.axis_index("c")
    return x, y, c, [(x, 1 - y), (1 - x, y), (1 - x, 1 - y)]


def _rcopy(src, dst, ssem, rsem, dev):
    return pltpu.make_async_remote_copy(src_ref=src, dst_ref=dst, send_sem=ssem, recv_sem=rsem,
                                        device_id=dev, device_id_type=MESH)


class _WeightGather:
    def __init__(self, shards):
        self.n = len(shards)
        self.rows = [w.shape[0] for w in shards]
        self.out_shape = [jax.ShapeDtypeStruct((N_CHIPS,) + w.shape, w.dtype) for w in shards]
        self.scratch_shapes = [pltpu.SemaphoreType.DMA((self.n,))] * 2 + [pltpu.SemaphoreType.DMA((self.n, 3))] * 4

    def _copies(self, ins, outs, sems, with_pass_on):
        lsem, lrsem, ssem, rsem, fsem, frsem = sems
        x, y, c, peers = _place()
        chip = 2 * x + y
        sib = (x, y, 1 - c)
        own = [_rcopy(ins[a], outs[a].at[chip], lsem.at[a], lrsem.at[a], sib) for a in range(self.n)]
        sends, arrived, passed, passed_in = [], [], [], []
        for a in range(self.n):
            hr = self.rows[a] // 2
            half = lambda who, hc, a=a, hr=hr: outs[a].at[who, pl.ds(hc * hr, hr), :]
            for r, (px, py) in enumerate(peers):
                pc = 2 * px + py
                sends.append(_rcopy(ins[a].at[pl.ds(c * hr, hr), :], half(chip, c), ssem.at[a, r], rsem.at[a, r],
                                    (px, py, c)))
                if with_pass_on:
                    arrived.append(_rcopy(half(pc, c), half(pc, c), ssem.at[a, r], rsem.at[a, r], (px, py, c)))
                    passed.append(_rcopy(half(pc, c), half(pc, c), fsem.at[a, r], frsem.at[a, r], sib))
                    passed_in.append(_rcopy(half(pc, 1 - c), half(pc, 1 - c), fsem.at[a, r], frsem.at[a, r], sib))
        return own, sends, arrived, passed, passed_in

    def start(self, ins, outs, sems):
        own, sends, _, _, _ = self._copies(ins, outs, sems, False)
        for cp in own + sends:
            cp.start()

    def finish(self, ins, outs, sems):
        own, sends, arrived, passed, passed_in = self._copies(ins, outs, sems, True)
        for got, fwd in zip(arrived, passed):
            got.wait_recv()
            fwd.start()
        for cp in passed_in:
            cp.wait_recv()
        for cp in sends + passed:
            cp.wait_send()
        for cp in own:
            cp.wait()


class _ChipExchange:
    def __init__(self, sends):
        self.n = len(sends)
        self.out_shape = [jax.ShapeDtypeStruct(g.shape, g.dtype) for g in sends]
        self.scratch_shapes = [pltpu.SemaphoreType.DMA((self.n, 3))] * 2

    def _copies(self, ins, outs, sems):
        ssem, rsem = sems
        _, _, c, peers = _place()
        return [_rcopy(ins[a].at[r], outs[a].at[r], ssem.at[a, r], rsem.at[a, r], (px, py, c))
                for a in range(self.n) for r, (px, py) in enumerate(peers)]

    def start(self, ins, outs, sems):
        for cp in self._copies(ins, outs, sems):
            cp.start()

    def finish(self, ins, outs, sems):
        for cp in self._copies(ins, outs, sems):
            cp.wait()


def _run_exchange(name, plan, arrays):
    n = plan.n

    def body(*refs):
        ins, outs, sems = refs[:n], refs[n:2 * n], refs[2 * n:]
        plan.start(ins, outs, sems)
        plan.finish(ins, outs, sems)

    return pl.pallas_call(
        body, name=name, in_specs=[ANY] * n, out_specs=[ANY] * n,
        out_shape=plan.out_shape, scratch_shapes=plan.scratch_shapes,
    )(*arrays)


def _grad_pair_exchange(name, gs):
    n = len(gs)

    def body(*refs):
        ins, outs, ssem, rsem = refs[:n], refs[n:2 * n], refs[2 * n], refs[2 * n + 1]
        x, y, c, _ = _place()
        cps = [_rcopy(ins[a].at[:, 1 - c], outs[a], ssem.at[a], rsem.at[a], (x, y, 1 - c)) for a in range(n)]
        for cp in cps:
            cp.start()
        for cp in cps:
            cp.wait()

    return pl.pallas_call(
        body, name=name,
        in_specs=[ANY] * n, out_specs=[ANY] * n,
        out_shape=[jax.ShapeDtypeStruct((g.shape[0],) + g.shape[2:], g.dtype) for g in gs],
        scratch_shapes=[pltpu.SemaphoreType.DMA((n,))] * 2,
    )(*gs)


def _grad_pair_add(name, g, got, idx, tr):
    _, _, hr, cc = g.shape

    def body(idx_ref, g0, g1, g2, g3, r0, r1, r2, r3, keep_ref, send_ref):
        keep_ref[...] = g0[...] + r0[...]
        for q, (gq, rq) in enumerate(((g1, r1), (g2, r2), (g3, r3))):
            send_ref[q] = (gq[...] + rq[...]).astype(BF16)

    gspec = lambda q: pl.BlockSpec((None, None, tr, cc), lambda i, idx: (idx[1 + q], idx[0], i, 0))
    rspec = lambda q: pl.BlockSpec((None, tr, cc), lambda i, idx: (idx[1 + q], i, 0))
    return pl.pallas_call(
        body, name=name,
        grid_spec=pltpu.PrefetchScalarGridSpec(
            num_scalar_prefetch=1, grid=(hr // tr,),
            in_specs=[gspec(q) for q in range(4)] + [rspec(q) for q in range(4)],
            out_specs=[pl.BlockSpec((tr, cc), lambda i, idx: (i, 0)), pl.BlockSpec((3, tr, cc), lambda i, idx: (0, i, 0))]),
        out_shape=[jax.ShapeDtypeStruct((hr, cc), F32), jax.ShapeDtypeStruct((3, hr, cc), BF16)],
        compiler_params=_params(("parallel",)),
    )(idx, g, g, g, g, got, got, got, got)


def _grad_chip_add(name, keep, got, tr):
    hr, cc = keep.shape

    def body(k_ref, g_ref, o_ref):
        o_ref[...] = ((k_ref[...] + g_ref[0].astype(F32)) + g_ref[1].astype(F32)) + g_ref[2].astype(F32)

    return pl.pallas_call(
        body, name=name, grid=(hr // tr,),
        in_specs=[pl.BlockSpec((tr, cc), lambda i: (i, 0)), pl.BlockSpec((3, tr, cc), lambda i: (0, i, 0))],
        out_specs=pl.BlockSpec((tr, cc), lambda i: (i, 0)),
        out_shape=jax.ShapeDtypeStruct((hr, cc), F32),
        compiler_params=_params(("parallel",)),
    )(keep, got)


def _grad_half_exchange(ts):
    n = len(ts)

    def body(*refs):
        ins, outs, ssem, rsem = refs[:n], refs[n:2 * n], refs[2 * n], refs[2 * n + 1]
        x, y, c, _ = _place()
        cps = [_rcopy(ins[a], outs[a], ssem.at[a], rsem.at[a], (x, y, 1 - c)) for a in range(n)]
        for cp in cps:
            cp.start()
        for cp in cps:
            cp.wait()

    return pl.pallas_call(
        body, name="grad_half_exchange",
        in_specs=[ANY] * n, out_specs=[ANY] * n,
        out_shape=[jax.ShapeDtypeStruct(t.shape, t.dtype) for t in ts],
        scratch_shapes=[pltpu.SemaphoreType.DMA((n,))] * 2,
    )(*ts)


def _small_allreduce(stats, d_wp):
    d = D_MODEL
    half = d // 2
    wps = d_wp.shape

    def body(mix_ref, x_ref, mem_ref, ffn_ref, loss_ref, hg_ref, pool_ref, wp_ref, slab_out, wp_out,
             slab_buf, wp_buf, sib_s, sib_w, ssem, rsem):
        x, y, c, peers = _place()
        chip = 2 * x + y
        sib = (x, y, 1 - c)
        hgn = jnp.concatenate([hg_ref[h, 0:1, :] for h in range(HGRN_HEADS)], axis=1)
        dlb = jnp.concatenate([hg_ref[h, 1:2, :] for h in range(HGRN_HEADS)], axis=1)
        slab_buf[0] = jnp.concatenate([
            mix_ref[0:1, :], x_ref[0:1, :], mem_ref[0:1, :], ffn_ref[0:1, :], loss_ref[0:1, :],
            jnp.concatenate([dlb, hgn], axis=1),
            jnp.concatenate([pool_ref[0:1, :], jnp.zeros((1, half), F32)], axis=1),
            loss_ref[1:2, :]], axis=0)
        wp_buf[0] = wp_ref[...]
        pair = [_rcopy(slab_buf.at[0], sib_s, ssem.at[0], rsem.at[0], sib),
                _rcopy(wp_buf.at[0], sib_w, ssem.at[1], rsem.at[1], sib)]
        for cp in pair:
            cp.start()
        for cp in pair:
            cp.wait()
        slab_buf[0] = slab_buf[0] + sib_s[...]
        wp_buf[0] = wp_buf[0] + sib_w[...]
        cps = []
        for r, (px, py) in enumerate(peers):
            cps.append(_rcopy(slab_buf.at[0], slab_buf.at[r + 1], ssem.at[2 + 2 * r], rsem.at[2 + 2 * r], (px, py, c)))
            cps.append(_rcopy(wp_buf.at[0], wp_buf.at[r + 1], ssem.at[3 + 2 * r], rsem.at[3 + 2 * r], (px, py, c)))
        for cp in cps:
            cp.start()
        for cp in cps:
            cp.wait()
        tot_s, tot_w = slab_buf[chip], wp_buf[chip]
        for j in range(1, N_CHIPS):
            tot_s = tot_s + slab_buf[jnp.bitwise_xor(j, chip)]
            tot_w = tot_w + wp_buf[jnp.bitwise_xor(j, chip)]
        slab_out[...] = tot_s
        wp_out[...] = tot_w

    return pl.pallas_call(
        body, name="small_allreduce",
        in_specs=[VMEM] * 8, out_specs=[VMEM] * 2,
        out_shape=[jax.ShapeDtypeStruct((8, d), F32), jax.ShapeDtypeStruct(wps, F32)],
        scratch_shapes=[pltpu.VMEM((N_CHIPS, 8, d), F32), pltpu.VMEM((N_CHIPS,) + wps, F32),
                        pltpu.VMEM((8, d), F32), pltpu.VMEM(wps, F32),
                        pltpu.SemaphoreType.DMA((8,)), pltpu.SemaphoreType.DMA((8,))],
    )(stats["mix"], stats["x"], stats["mem"], stats["ffn"], stats["loss"], stats["hgrn"], stats["pool"], d_wp)


def _adamw_math(w, g, m, v):
    m = ADAM_B1 * m + (1.0 - ADAM_B1) * g
    v = ADAM_B2 * v + (1.0 - ADAM_B2) * (g * g)
    m_hat = m / (1.0 - ADAM_B1 ** ADAM_STEP)
    v_hat = v / (1.0 - ADAM_B2 ** ADAM_STEP)
    delta = -ADAM_LR * (m_hat / (jnp.sqrt(v_hat) + ADAM_EPS) + ADAM_WD * w)
    return delta, m, v


def _adamw(name, mine, theirs, w, m, v, idx, tr):
    rows, cc = w.shape
    nb = rows // 2 // tr

    def body(idx_ref, a_ref, b_ref, w_ref, m_ref, v_ref, g_out, d_out, m_out, v_out):
        g = jnp.where(pl.program_id(0) // nb == idx_ref[0], a_ref[...], b_ref[...])
        g_out[...] = g
        d_out[...], m_out[...], v_out[...] = _adamw_math(w_ref[...], g, m_ref[...], v_ref[...])

    hspec = pl.BlockSpec((tr, cc), lambda i, idx: (i % nb, 0))
    spec = pl.BlockSpec((tr, cc), lambda i, idx: (i, 0))
    return pl.pallas_call(
        body, name=name,
        grid_spec=pltpu.PrefetchScalarGridSpec(
            num_scalar_prefetch=1, grid=(rows // tr,),
            in_specs=[hspec, hspec, spec, spec, spec], out_specs=[spec] * 4),
        out_shape=[jax.ShapeDtypeStruct((rows, cc), F32)] * 4,
        compiler_params=_params(("parallel",)),
    )(idx, mine, theirs, w, m, v)


SMALL_NAMES = ("norm_mix_g", "lb_logits", "hgrn_norm_g", "w_pool", "pool_scale", "norm_x_g", "norm_mem_g",
               "norm_ffn_g", "final_norm_g")


def _small_update(slab, d_wp, ws, ms, vs):
    n = len(SMALL_NAMES)
    half = D_MODEL // 2

    def body(slab_ref, wp_ref, *refs):
        w_refs, m_refs, v_refs, outs = refs[:n], refs[n:2 * n], refs[2 * n:3 * n], refs[3 * n:]
        row = lambda k: slab_ref[k:k + 1, :]
        lbl = w_refs[SMALL_NAMES.index("lb_logits")][...]
        s0 = _lower_bound(lbl[0:1, :], lbl[1:2, :])
        dl0 = row(ROW_LB_HGN)[:, :half] * s0 * (1.0 - s0)
        grads = dict(norm_mix_g=row(ROW_GMIX), lb_logits=jnp.concatenate([dl0, -dl0], axis=0),
                     hgrn_norm_g=row(ROW_LB_HGN)[:, half:], w_pool=wp_ref[...], pool_scale=row(ROW_PSCALE)[:, :half],
                     norm_x_g=row(ROW_GX), norm_mem_g=row(ROW_GMEM), norm_ffn_g=row(ROW_GFFN),
                     final_norm_g=row(ROW_GFIN))
        outs[0][...] = row(ROW_LOSS)[:, :128]
        for i, name in enumerate(SMALL_NAMES):
            g = grads[name]
            delta, m2, v2 = _adamw_math(w_refs[i][...], g, m_refs[i][...], v_refs[i][...])
            for o, val in zip(outs[1 + 4 * i:5 + 4 * i], (g, delta, m2, v2)):
                o[...] = val

    args = [ws[k] for k in SMALL_NAMES] + [ms[k] for k in SMALL_NAMES] + [vs[k] for k in SMALL_NAMES]
    out_shape = [jax.ShapeDtypeStruct((1, 128), F32)]
    for k in SMALL_NAMES:
        out_shape += [jax.ShapeDtypeStruct(ws[k].shape, F32)] * 4
    res = pl.pallas_call(
        body, name="small_update",
        in_specs=[VMEM] * (2 + 3 * n), out_specs=[VMEM] * len(out_shape), out_shape=out_shape,
    )(slab, d_wp, *args)
    return res[0], {k: res[1 + 4 * i:5 + 4 * i] for i, k in enumerate(SMALL_NAMES)}


ALL_NAMES = ("norm_mix_g", "w_in", "lb_logits", "hgrn_norm_g", "w_pool", "pool_scale", "w_out", "norm_x_g",
             "norm_mem_g", "w_xq", "w_xk", "w_xv", "w_xo", "norm_ffn_g", "w_ff1", "w_ff2", "final_norm_g")


def _shard_2d(name, a):
    a = a[0]
    if name in ("w_xq", "w_xk", "w_xv"):
        return a.reshape(a.shape[0], -1)
    if name == "w_xo":
        return a.reshape(-1, a.shape[-1])
    return a


def _small_2d(name, a):
    if name == "w_pool":
        return a.reshape(-1, HEAD_DIM)
    if name == "lb_logits":
        return a
    return a.reshape(1, -1)


def kernel(x, mem, norm_mix_g, w_in, lb_logits, hgrn_norm_g, w_pool, pool_scale, w_out, norm_x_g, norm_mem_g, w_xq, w_xk, w_xv, w_xo, norm_ffn_g, w_ff1, w_ff2, final_norm_g, loss_target, m_norm_mix_g, m_w_in, m_lb_logits, m_hgrn_norm_g, m_w_pool, m_pool_scale, m_w_out, m_norm_x_g, m_norm_mem_g, m_w_xq, m_w_xk, m_w_xv, m_w_xo, m_norm_ffn_g, m_w_ff1, m_w_ff2, m_final_norm_g, v_norm_mix_g, v_w_in, v_lb_logits, v_hgrn_norm_g, v_w_pool, v_pool_scale, v_w_out, v_norm_x_g, v_norm_mem_g, v_w_xq, v_w_xk, v_w_xv, v_w_xo, v_norm_ffn_g, v_w_ff1, v_w_ff2, v_final_norm_g):
    w = dict(norm_mix_g=norm_mix_g, w_in=w_in, lb_logits=lb_logits, hgrn_norm_g=hgrn_norm_g, w_pool=w_pool, pool_scale=pool_scale, w_out=w_out, norm_x_g=norm_x_g, norm_mem_g=norm_mem_g, w_xq=w_xq, w_xk=w_xk, w_xv=w_xv, w_xo=w_xo, norm_ffn_g=norm_ffn_g, w_ff1=w_ff1, w_ff2=w_ff2, final_norm_g=final_norm_g)
    m = dict(norm_mix_g=m_norm_mix_g, w_in=m_w_in, lb_logits=m_lb_logits, hgrn_norm_g=m_hgrn_norm_g, w_pool=m_w_pool, pool_scale=m_pool_scale, w_out=m_w_out, norm_x_g=m_norm_x_g, norm_mem_g=m_norm_mem_g, w_xq=m_w_xq, w_xk=m_w_xk, w_xv=m_w_xv, w_xo=m_w_xo, norm_ffn_g=m_norm_ffn_g, w_ff1=m_w_ff1, w_ff2=m_w_ff2, final_norm_g=m_final_norm_g)
    v = dict(norm_mix_g=v_norm_mix_g, w_in=v_w_in, lb_logits=v_lb_logits, hgrn_norm_g=v_hgrn_norm_g, w_pool=v_w_pool, pool_scale=v_pool_scale, w_out=v_w_out, norm_x_g=v_norm_x_g, norm_mem_g=v_norm_mem_g, w_xq=v_w_xq, w_xk=v_w_xk, w_xv=v_w_xv, w_xo=v_w_xo, norm_ffn_g=v_norm_ffn_g, w_ff1=v_w_ff1, w_ff2=v_w_ff2, final_norm_g=v_final_norm_g)

    big_w = {k: _shard_2d(k, w[k]) for k in BIG_NAMES}
    slab = jnp.concatenate([big_w[k] for k in ("w_out", "w_xq", "w_xk", "w_xv", "w_ff1", "w_ff2")], axis=0).astype(BF16)
    shards = dict(slab=slab, w_in=big_w["w_in"].astype(BF16), w_xo=big_w["w_xo"].astype(BF16))

    cx, cy, cc = lax.axis_index("x"), lax.axis_index("y"), lax.axis_index("c")
    chip = 2 * cx + cy
    idx = jnp.stack([cc, chip, chip ^ 1, chip ^ 2, chip ^ 3]).astype(jnp.int32)
    small = {k: w[k] for k in SMALL_NAMES}
    grad_x, stats, d_wp, partials = _step(x[0], mem[0], loss_target[0], small, shards, idx)

    halves = [_grad_chip_add("grad_chip_add_" + k, *partials[k], tr=128) for k in BIG_NAMES]
    theirs = _grad_half_exchange(halves)

    grads, deltas, new_m, new_v = {}, {}, {}, {}
    for k, mine, other in zip(BIG_NAMES, halves, theirs):
        res = _adamw("adamw_" + k, mine, other, big_w[k], _shard_2d(k, m[k]), _shard_2d(k, v[k]), idx, tr=128)
        for store, val in zip((grads, deltas, new_m, new_v), res):
            store[k] = val.reshape(w[k].shape)

    slab_sum, wp_sum = _small_allreduce(stats, d_wp.reshape(-1, HEAD_DIM))
    loss, upd = _small_update(slab_sum, wp_sum, {k: _small_2d(k, w[k]) for k in SMALL_NAMES},
                              {k: _small_2d(k, m[k]) for k in SMALL_NAMES}, {k: _small_2d(k, v[k]) for k in SMALL_NAMES})
    for k in SMALL_NAMES:
        for store, val in zip((grads, deltas, new_m, new_v), upd[k]):
            store[k] = val.reshape(w[k].shape)

    return (loss[0, 0], grad_x[None], *[grads[k] for k in ALL_NAMES], *[deltas[k] for k in ALL_NAMES],
            *[new_m[k] for k in ALL_NAMES], *[new_v[k] for k in ALL_NAMES])
```

```python
import functools

import jax
import jax.numpy as jnp
from jax import lax
from jax.experimental import pallas as pl
from jax.experimental.pallas import tpu as pltpu

F32 = jnp.float32
BF16 = jnp.bfloat16
HIGHEST = lax.Precision.HIGHEST
MESH = pl.DeviceIdType.MESH
ANY = pl.BlockSpec(memory_space=pl.ANY)
VMEM = pl.BlockSpec(memory_space=pltpu.VMEM)

D_MODEL = 1024
N_CHIPS = 4
HGRN_HEADS = 4
HEAD_DIM = 128
HGRN_WIDTH = HGRN_HEADS * HEAD_DIM
POOL_WINDOWS = (2, 4, 8, 16)
POOL_HALO = 16
SUB = 16
XATTN_HEADS = 4
XATTN_HEAD_DIM = 256
EPS = 1e-6
ADAM_LR, ADAM_B1, ADAM_B2, ADAM_EPS, ADAM_WD, ADAM_STEP = 0.001, 0.9, 0.999, 1e-08, 0.01, 10

V7X_VMEM_BYTES = 64 * 1024 * 1024
VMEM_LIMIT = V7X_VMEM_BYTES - 8 * 1024 * 1024

NN = (((1,), (0,)), ((), ()))
NT = (((1,), (1,)), ((), ()))
TN = (((0,), (0,)), ((), ()))

ROW_GMIX, ROW_GX, ROW_GMEM, ROW_GFFN, ROW_GFIN, ROW_LB_HGN, ROW_PSCALE, ROW_LOSS = range(8)


def _dot(a, b, dims=NN):
    return lax.dot_general(a, b, dims, preferred_element_type=F32)


def _sigmoid(x):
    return 1.0 / (1.0 + jnp.exp(-x))


def _rms_fwd(x, g):
    r = lax.rsqrt(jnp.mean(x * x, axis=-1, keepdims=True) + EPS)
    n = x * r
    return n * g, n, r


def _rms_bwd(dh, n, r, g):
    dn = dh * g
    dx = r * (dn - n * jnp.mean(dn * n, axis=-1, keepdims=True))
    return dx, jnp.sum(dh * n, axis=0, keepdims=True)


def _params(sem=None):
    return pltpu.CompilerParams(dimension_semantics=sem, vmem_limit_bytes=VMEM_LIMIT)


def _const(shape):
    nd = len(shape)
    return pl.BlockSpec(shape, lambda *_: (0,) * nd, pipeline_mode=pl.Buffered(1))


def _const_out(shape):
    nd = len(shape)
    return pl.BlockSpec(shape, lambda *_: (0,) * nd)


def _acc_rows(ref, t, rows):
    upd = jnp.concatenate(rows + [jnp.zeros((8 - len(rows), rows[0].shape[1]), F32)], axis=0)

    @pl.when(t == 0)
    def _():
        ref[...] = upd

    @pl.when(t > 0)
    def _():
        ref[...] = ref[...] + upd


def _fuse_exchange(body, n_in, n_out, n_scratch, plan, ndim):
    if plan is None:
        return body
    n = plan.n

    def wrapped(*refs):
        ins, cin = refs[:n_in], refs[n_in:n_in + n]
        outs, cout = refs[n_in + n:n_in + n + n_out], refs[n_in + n + n_out:n_in + 2 * n + n_out]
        rest = refs[n_in + 2 * n + n_out:]
        scr, csem = rest[:n_scratch], rest[n_scratch:]
        first = pl.program_id(0) == 0
        last = pl.program_id(0) == pl.num_programs(0) - 1
        for i in range(1, ndim):
            first = first & (pl.program_id(i) == 0)
            last = last & (pl.program_id(i) == pl.num_programs(i) - 1)

        @pl.when(first)
        def _():
            plan.start(cin, cout, csem)

        body(*ins, *outs, *scr)

        @pl.when(last)
        def _():
            plan.finish(cin, cout, csem)

    return wrapped


def _plan_extras(plan):
    if plan is None:
        return [], [], []
    return [ANY] * plan.n, list(plan.out_shape), list(plan.scratch_shapes)


def _in_proj(x, g, win_g, tm):
    s, d = x.shape
    nsh, _, wc = win_g.shape

    def body(x_ref, g_ref, w_ref, z_ref, h_ref):
        h, _, _ = _rms_fwd(x_ref[...], g_ref[...])
        hb = h.astype(BF16)
        h_ref[...] = hb
        for j in range(nsh):
            z_ref[:, j * wc:(j + 1) * wc] = _dot(hb, w_ref[j])

    return pl.pallas_call(
        body, name="in_proj", grid=(s // tm,),
        in_specs=[pl.BlockSpec((tm, d), lambda t: (t, 0)), _const((1, d)), _const((nsh, d, wc))],
        out_specs=[pl.BlockSpec((tm, nsh * wc), lambda t: (t, 0)), pl.BlockSpec((tm, d), lambda t: (t, 0))],
        out_shape=[jax.ShapeDtypeStruct((s, nsh * wc), F32), jax.ShapeDtypeStruct((s, d), BF16)],
        compiler_params=_params(("parallel",)),
    )(x, g, win_g)


def _lower_bound(l0, l1):
    m = jnp.maximum(l0, l1)
    e0, e1 = jnp.exp(l0 - m), jnp.exp(l1 - m)
    return e0 / (e0 + e1)


def _block_tri(n, upper):
    r = lax.broadcasted_iota(jnp.int32, (n, n), 0)
    c = lax.broadcasted_iota(jnp.int32, (n, n), 1)
    keep = (r // SUB == c // SUB) & ((c >= r) if upper else (c <= r))
    return keep.astype(F32)


def _hgrn_gates(qp, fp, lb):
    sq = _sigmoid(qp)
    sf = _sigmoid(fp)
    f = lb + (1.0 - lb) * sf
    return qp * sq, sq, f, sf


def _hgrn_fwd(z, l0, l1, gn, tc, unroll=1, plan=None, plan_args=()):
    s = z.shape[0]
    nsub = tc // SUB
    hd = HEAD_DIM

    def body(q_ref, f_ref, v_ref, g_ref, l0_ref, l1_ref, gn_ref, o_ref, oa_ref, st_ref, state, qs, ks, bs, os_):
        @pl.when(pl.program_id(1) == 0)
        def _():
            state[...] = jnp.zeros_like(state)

        lb = _lower_bound(l0_ref[...], l1_ref[...])
        q, _, f, _ = _hgrn_gates(q_ref[...], f_ref[...], lb)
        qs[...] = q
        ks[...] = 1.0 - f
        bs[...] = jnp.dot(_block_tri(tc, False), jnp.log(f), precision=HIGHEST, preferred_element_type=F32)
        rows = lax.broadcasted_iota(jnp.int32, (SUB, 1), 0)

        def step(i, carry):
            r0 = pl.multiple_of(i * SUB, SUB)
            q_ = qs[pl.ds(r0, SUB), :]
            k_ = ks[pl.ds(r0, SUB), :]
            b_ = bs[pl.ds(r0, SUB), :]
            v_ = v_ref[pl.ds(r0, SUB), :]
            st = state[...]
            st_ref[i] = st
            bl = b_[SUB - 1:SUB, :]
            o = _dot((q_ * jnp.exp(b_)).astype(BF16), st.astype(BF16), NT)
            for j in range(SUB):
                e = jnp.exp(jnp.minimum(b_ - b_[j:j + 1, :], 0.0))
                col = jnp.sum(q_ * e * k_[j:j + 1, :], axis=-1, keepdims=True)
                o = o + jnp.where(rows >= j, col, 0.0) * v_[j:j + 1, :]
            os_[pl.ds(r0, SUB), :] = o
            kt = (k_ * jnp.exp(bl - b_)).astype(BF16)
            state[...] = st * jnp.exp(bl) + _dot(v_.astype(BF16), kt, TN)
            return carry

        lax.fori_loop(0, nsub, step, 0, unroll=unroll)
        o = os_[...]
        o_ref[...] = o
        r = lax.rsqrt(jnp.mean(o * o, axis=-1, keepdims=True) + EPS)
        gp = g_ref[...]
        oa_ref[...] = (o * r * gn_ref[...] * (gp * _sigmoid(gp))).astype(BF16)

    col = lambda k: pl.BlockSpec((tc, hd), lambda h, t: (t, k * HGRN_HEADS + h))
    vec = pl.BlockSpec((None, 1, hd), lambda h, t: (h, 0, 0))
    x_specs, x_shapes, x_scratch = _plan_extras(plan)
    return pl.pallas_call(
        _fuse_exchange(body, 7, 3, 5, plan, 2), name="hgrn_fwd", grid=(HGRN_HEADS, s // tc),
        in_specs=[col(0), col(1), col(2), col(3), vec, vec, vec] + x_specs,
        out_specs=[pl.BlockSpec((tc, hd), lambda h, t: (t, h)), pl.BlockSpec((tc, hd), lambda h, t: (t, h)),
                   pl.BlockSpec((None, nsub, hd, hd), lambda h, t: (h, t, 0, 0))] + x_specs,
        out_shape=[jax.ShapeDtypeStruct((s, HGRN_WIDTH), F32), jax.ShapeDtypeStruct((s, HGRN_WIDTH), BF16),
                   jax.ShapeDtypeStruct((HGRN_HEADS, s // SUB, hd, hd), F32)] + x_shapes,
        scratch_shapes=[pltpu.VMEM((hd, hd), F32)] + [pltpu.VMEM((tc, hd), F32)] * 4 + x_scratch,
        compiler_params=_params(("arbitrary", "arbitrary")),
    )(z, z, z, z, l0, l1, gn, *plan_args)


def _pooled(p, ext, tok0):
    tm = p.shape[0]
    tok = tok0 + lax.broadcasted_iota(jnp.int32, (tm, 1), 0)
    outs = []
    for g, w in enumerate(POOL_WINDOWS):
        acc = ext[:, g * HEAD_DIM:(g + 1) * HEAD_DIM]
        sh = 1
        while sh < w:
            acc = acc + pltpu.roll(acc, sh, axis=0)
            sh *= 2
        cnt = jnp.minimum(tok + 1, w).astype(F32)
        outs.append(acc[POOL_HALO:, :] / cnt - p[:, g * HEAD_DIM:(g + 1) * HEAD_DIM])
    return outs


def _pool_fwd(z, wp, scale, tm):
    s = z.shape[0]
    pw = len(POOL_WINDOWS) * HEAD_DIM
    nb = tm // POOL_HALO

    def body(p_ref, prev_ref, wp_ref, sc_ref, ob_ref):
        t = pl.program_id(0)
        p = p_ref[...]
        prev = jnp.where(t > 0, prev_ref[...], 0.0)
        pooled = _pooled(p, jnp.concatenate([prev, p], axis=0), t * tm)
        ys = [_dot(pooled[g].astype(BF16), wp_ref[g].astype(BF16)) for g in range(len(POOL_WINDOWS))]
        ob_ref[...] = (jnp.concatenate(ys, axis=1) * sc_ref[...]).astype(BF16)

    return pl.pallas_call(
        body, name="pool_fwd", grid=(s // tm,),
        in_specs=[pl.BlockSpec((tm, pw), lambda t: (t, 4)),
                  pl.BlockSpec((POOL_HALO, pw), lambda t: (jnp.maximum(t * nb - 1, 0), 4)),
                  _const(wp.shape), _const((1, pw))],
        out_specs=pl.BlockSpec((tm, pw), lambda t: (t, 0)),
        out_shape=jax.ShapeDtypeStruct((s, pw), BF16),
        compiler_params=_params(("parallel",)),
    )(z, z, wp, scale)


def _kv_proj(mem, g, slab_g):
    m, d = mem.shape
    rows = d // N_CHIPS

    def body(mem_ref, g_ref, wk_ref, wv_ref, xk_ref, xv_ref):
        hm, _, _ = _rms_fwd(mem_ref[...], g_ref[...])
        hb = hm.astype(BF16)
        xk_ref[...] = _dot(hb, wk_ref[...].reshape(d, d)).astype(BF16)
        xv_ref[...] = _dot(hb, wv_ref[...].reshape(d, d)).astype(BF16)

    blk = lambda k: pl.BlockSpec((N_CHIPS, rows, d), lambda i: (0, k, 0))
    return pl.pallas_call(
        body, name="kv_proj", grid=(1,),
        in_specs=[_const((m, d)), _const((1, d)), blk(2), blk(3)],
        out_specs=[_const_out((m, d)), _const_out((m, d))],
        out_shape=[jax.ShapeDtypeStruct((m, d), BF16)] * 2,
        compiler_params=_params(("arbitrary",)),
    )(mem, g, slab_g, slab_g)


def _softmax_rows(sc):
    e = jnp.exp(sc - jnp.max(sc, axis=-1, keepdims=True))
    return e / jnp.sum(e, axis=-1, keepdims=True)


def _mix_xattn_fwd(x, oa, ob, gx, slab_g, wo_g, xk, xv, tm):
    s, d = x.shape
    m = xk.shape[0]
    rows = d // N_CHIPS
    hw = oa.shape[1]
    e = XATTN_HEAD_DIM

    def body(x_ref, oa_ref, ob_ref, gx_ref, wout_ref, wq_ref, wo_ref, xk_ref, xv_ref,
             x1_ref, mixed_ref, hq_ref, xq_ref, att_ref, x2_ref):
        mixed = jnp.concatenate([oa_ref[...], ob_ref[...]], axis=1)
        mixed_ref[...] = mixed
        x1 = x_ref[...] + _dot(mixed, wout_ref[...].reshape(d, d))
        x1_ref[...] = x1
        hq, _, _ = _rms_fwd(x1, gx_ref[...])
        hqb = hq.astype(BF16)
        hq_ref[...] = hqb
        xq = _dot(hqb, wq_ref[...].reshape(d, d)).astype(BF16)
        xq_ref[...] = xq
        atts = []
        for h in range(XATTN_HEADS):
            cs = slice(h * e, (h + 1) * e)
            p = _softmax_rows(_dot(xq[:, cs], xk_ref[:, cs], NT) * (e ** -0.5))
            atts.append(_dot(p.astype(BF16), xv_ref[:, cs]).astype(BF16))
        att = jnp.concatenate(atts, axis=1)
        att_ref[...] = att
        for j in range(N_CHIPS):
            x2_ref[:, j * rows:(j + 1) * rows] = x1[:, j * rows:(j + 1) * rows] + _dot(att, wo_ref[j])

    tile = lambda w: pl.BlockSpec((tm, w), lambda t: (t, 0))
    blk = lambda k: pl.BlockSpec((N_CHIPS, rows, d), lambda t: (0, k, 0), pipeline_mode=pl.Buffered(1))
    return pl.pallas_call(
        body, name="mix_xattn_fwd", grid=(s // tm,),
        in_specs=[tile(d), tile(hw), tile(hw), _const((1, d)), blk(0), blk(1), _const(wo_g.shape),
                  _const((m, d)), _const((m, d))],
        out_specs=[tile(d)] * 6,
        out_shape=[jax.ShapeDtypeStruct((s, d), F32)] + [jax.ShapeDtypeStruct((s, d), BF16)] * 4
                  + [jax.ShapeDtypeStruct((s, d), F32)],
        compiler_params=_params(("parallel",)),
    )(x, oa, ob, gx, slab_g, slab_g, wo_g, xk, xv)


def _mlp_loss_fwd(x2, gffn, gfin, slab_g, target, tm):
    s, d = x2.shape
    wr = slab_g.shape[1] // 3

    def body(x2_ref, gffn_ref, gfin_ref, w1_ref, w2_ref, tg_ref, a_ref, hf_ref, dx3_ref, dx3b_ref, st_ref):
        x2v = x2_ref[...]
        hf, _, _ = _rms_fwd(x2v, gffn_ref[...])
        hfb = hf.astype(BF16)
        hf_ref[...] = hfb
        acc = x2v
        for j in range(N_CHIPS):
            a = _dot(hfb, w1_ref[j])
            a_ref[:, j * wr:(j + 1) * wr] = a
            r = jnp.maximum(a, 0.0)
            acc = acc + _dot((r * r).astype(BF16), w2_ref[j])
        gf = gfin_ref[...]
        y, n, r3 = _rms_fwd(acc, gf)
        err = y - tg_ref[...]
        loss = 0.5 * jnp.sum(jnp.sum(err * err, axis=-1, keepdims=True) * (1.0 / d), axis=0, keepdims=True)
        dy = err * (1.0 / d)
        dx3, dgf = _rms_bwd(dy, n, r3, gf)
        dx3_ref[...] = dx3
        dx3b_ref[...] = dx3.astype(BF16)
        _acc_rows(st_ref, pl.program_id(0), [dgf, jnp.broadcast_to(loss, (1, d))])

    tile = lambda w: pl.BlockSpec((tm, w), lambda t: (t, 0))
    blk = lambda k: pl.BlockSpec((N_CHIPS, wr, d), lambda t: (0, k, 0), pipeline_mode=pl.Buffered(1))
    return pl.pallas_call(
        body, name="mlp_loss_fwd", grid=(s // tm,),
        in_specs=[tile(d), _const((1, d)), _const((1, d)), blk(1), blk(2), tile(d)],
        out_specs=[tile(N_CHIPS * wr), tile(d), tile(d), tile(d), _const_out((8, d))],
        out_shape=[jax.ShapeDtypeStruct((s, N_CHIPS * wr), F32), jax.ShapeDtypeStruct((s, d), BF16),
                   jax.ShapeDtypeStruct((s, d), F32), jax.ShapeDtypeStruct((s, d), BF16),
                   jax.ShapeDtypeStruct((8, d), F32)],
        compiler_params=_params(("arbitrary",)),
    )(x2, gffn, gfin, slab_g, slab_g, target)


def _mlp_bwd(dx3, dx3b, a, x2, gffn, slab_g, tm):
    s, d = x2.shape
    wr = slab_g.shape[1] // 3

    def body(dx3_ref, dx3b_ref, a_ref, x2_ref, g_ref, w1_ref, w2_ref, da_ref, u_ref, dx2_ref, dx2b_ref, st_ref):
        dyb = dx3b_ref[...]
        dhf = jnp.zeros((tm, d), F32)
        for j in range(N_CHIPS):
            r = jnp.maximum(a_ref[:, j * wr:(j + 1) * wr], 0.0)
            da = (_dot(dyb, w2_ref[j], NT) * (2.0 * r)).astype(BF16)
            da_ref[:, j * wr:(j + 1) * wr] = da
            u_ref[:, j * wr:(j + 1) * wr] = (r * r).astype(BF16)
            dhf = dhf + _dot(da, w1_ref[j], NT)
        g = g_ref[...]
        _, n, r2 = _rms_fwd(x2_ref[...], g)
        dxn, dg = _rms_bwd(dhf, n, r2, g)
        dx2 = dx3_ref[...] + dxn
        dx2_ref[...] = dx2
        dx2b_ref[...] = dx2.astype(BF16)
        _acc_rows(st_ref, pl.program_id(0), [dg])

    tile = lambda w: pl.BlockSpec((tm, w), lambda t: (t, 0))
    blk = lambda k: pl.BlockSpec((N_CHIPS, wr, d), lambda t: (0, k, 0), pipeline_mode=pl.Buffered(1))
    nf = N_CHIPS * wr
    return pl.pallas_call(
        body, name="mlp_bwd", grid=(s // tm,),
        in_specs=[tile(d), tile(d), tile(nf), tile(d), _const((1, d)), blk(1), blk(2)],
        out_specs=[tile(nf), tile(nf), tile(d), tile(d), _const_out((8, d))],
        out_shape=[jax.ShapeDtypeStruct((s, nf), BF16), jax.ShapeDtypeStruct((s, nf), BF16),
                   jax.ShapeDtypeStruct((s, d), F32), jax.ShapeDtypeStruct((s, d), BF16),
                   jax.ShapeDtypeStruct((8, d), F32)],
        compiler_params=_params(("arbitrary",)),
    )(dx3, dx3b, a, x2, gffn, slab_g, slab_g)


def _xattn_mix_bwd(dx2, x1, xq, xk, xv, gx, slab_g, wo_g, tm, plan=None, plan_args=()):
    s, d = x1.shape
    m = xk.shape[0]
    rows = d // N_CHIPS
    e = XATTN_HEAD_DIM

    def body(dx2_ref, x1_ref, xq_ref, xk_ref, xv_ref, gx_ref, wout_ref, wq_ref, wo_ref,
             dx1_ref, dx1b_ref, dxq_ref, dmix_ref, dxk_ref, dxv_ref, st_ref):
        t = pl.program_id(0)
        dx2 = dx2_ref[...]
        dx2b = dx2.astype(BF16)
        datt = jnp.zeros((tm, d), F32)
        for j in range(N_CHIPS):
            datt = datt + _dot(dx2b[:, j * rows:(j + 1) * rows], wo_ref[j], NT)
        dattb = datt.astype(BF16)
        dxqs, dxks, dxvs = [], [], []
        for h in range(XATTN_HEADS):
            cs = slice(h * e, (h + 1) * e)
            xq_h, xk_h, xv_h = xq_ref[:, cs], xk_ref[:, cs], xv_ref[:, cs]
            p = _softmax_rows(_dot(xq_h, xk_h, NT) * (e ** -0.5))
            dp = _dot(dattb[:, cs], xv_h, NT)
            ds = (p * (dp - jnp.sum(dp * p, axis=-1, keepdims=True)) * (e ** -0.5)).astype(BF16)
            dxqs.append(_dot(ds, xk_h).astype(BF16))
            dxks.append(_dot(ds, xq_h, TN))
            dxvs.append(_dot(p.astype(BF16), dattb[:, cs], TN))
        dxq = jnp.concatenate(dxqs, axis=1)
        dxq_ref[...] = dxq
        dxk = jnp.concatenate(dxks, axis=1)
        dxv = jnp.concatenate(dxvs, axis=1)

        @pl.when(t == 0)
        def _():
            dxk_ref[...] = dxk
            dxv_ref[...] = dxv

        @pl.when(t > 0)
        def _():
            dxk_ref[...] = dxk_ref[...] + dxk
            dxv_ref[...] = dxv_ref[...] + dxv

        dhq = jnp.concatenate([_dot(dxq, wq_ref[j], NT) for j in range(N_CHIPS)], axis=1)
        g = gx_ref[...]
        _, n, r1 = _rms_fwd(x1_ref[...], g)
        dxn, dg = _rms_bwd(dhq, n, r1, g)
        dx1 = dx2 + dxn
        dx1_ref[...] = dx1
        dx1b = dx1.astype(BF16)
        dx1b_ref[...] = dx1b
        for j in range(N_CHIPS):
            dmix_ref[:, j * rows:(j + 1) * rows] = _dot(dx1b, wout_ref[j], NT)
        _acc_rows(st_ref, t, [dg])

    tile = lambda: pl.BlockSpec((tm, d), lambda t: (t, 0))
    blk = lambda k: pl.BlockSpec((N_CHIPS, rows, d), lambda t: (0, k, 0), pipeline_mode=pl.Buffered(1))
    x_specs, x_shapes, x_scratch = _plan_extras(plan)
    return pl.pallas_call(
        _fuse_exchange(body, 9, 7, 0, plan, 1), name="xattn_mix_bwd", grid=(s // tm,),
        in_specs=[tile(), tile(), tile(), _const((m, d)), _const((m, d)), _const((1, d)), blk(0), blk(1),
                  _const(wo_g.shape)] + x_specs,
        out_specs=[tile(), tile(), tile(), tile(), _const_out((m, d)), _const_out((m, d)), _const_out((8, d))]
                  + x_specs,
        out_shape=[jax.ShapeDtypeStruct((s, d), F32), jax.ShapeDtypeStruct((s, d), BF16),
                   jax.ShapeDtypeStruct((s, d), BF16), jax.ShapeDtypeStruct((s, d), F32),
                   jax.ShapeDtypeStruct((m, d), F32), jax.ShapeDtypeStruct((m, d), F32),
                   jax.ShapeDtypeStruct((8, d), F32)] + x_shapes,
        scratch_shapes=x_scratch,
        compiler_params=_params(("arbitrary",)),
    )(dx2, x1, xq, xk, xv, gx, slab_g, slab_g, wo_g, *plan_args)


def _kv_bwd(mem, g, dxk, dxv, slab_g):
    m, d = mem.shape
    rows = d // N_CHIPS

    def body(mem_ref, g_ref, dxk_ref, dxv_ref, wk_ref, wv_ref, dwk_ref, dwv_ref, st_ref):
        gv = g_ref[...]
        hm, n, _ = _rms_fwd(mem_ref[...], gv)
        hb = hm.astype(BF16)
        dkb = dxk_ref[...].astype(BF16)
        dvb = dxv_ref[...].astype(BF16)
        dhm = []
        for j in range(N_CHIPS):
            hj = hb[:, j * rows:(j + 1) * rows]
            dwk_ref[j] = _dot(hj, dkb, TN)
            dwv_ref[j] = _dot(hj, dvb, TN)
            dhm.append(_dot(dkb, wk_ref[j], NT) + _dot(dvb, wv_ref[j], NT))
        dg = jnp.sum(jnp.concatenate(dhm, axis=1) * n, axis=0, keepdims=True)
        st_ref[...] = jnp.concatenate([dg, jnp.zeros((7, d), F32)], axis=0)

    blk = lambda k: pl.BlockSpec((N_CHIPS, rows, d), lambda i: (0, k, 0))
    return pl.pallas_call(
        body, name="kv_bwd", grid=(1,),
        in_specs=[_const((m, d)), _const((1, d)), _const((m, d)), _const((m, d)), blk(2), blk(3)],
        out_specs=[_const_out((N_CHIPS, rows, d)), _const_out((N_CHIPS, rows, d)), _const_out((8, d))],
        out_shape=[jax.ShapeDtypeStruct((N_CHIPS, rows, d), F32)] * 2 + [jax.ShapeDtypeStruct((8, d), F32)],
        compiler_params=_params(("arbitrary",)),
    )(mem, g, dxk, dxv, slab_g, slab_g)


def _pool_bwd(z, dmix, wp, scale, tm, plan=None, plan_args=()):
    s = z.shape[0]
    ng = len(POOL_WINDOWS)
    pw = ng * HEAD_DIM
    nb = tm // POOL_HALO
    nt = s // tm
    n_ext = tm + POOL_HALO

    def body(p_ref, prev_ref, dm_ref, dmn_ref, wp_ref, sc_ref, dp_ref, dwp_ref, st_ref):
        t = pl.program_id(0)
        p = p_ref[...]
        prev = jnp.where(t > 0, prev_ref[...], 0.0)
        pooled = _pooled(p, jnp.concatenate([prev, p], axis=0), t * tm)
        dm = dm_ref[...]
        dme = jnp.concatenate([dm, jnp.where(t < nt - 1, dmn_ref[...], 0.0)], axis=0) * sc_ref[...]
        tok = t * tm + lax.broadcasted_iota(jnp.int32, (n_ext, 1), 0)
        dsc, dps, dwps = [], [], []
        for g, w in enumerate(POOL_WINDOWS):
            cs = slice(g * HEAD_DIM, (g + 1) * HEAD_DIM)
            wpb = wp_ref[g].astype(BF16)
            pb = pooled[g].astype(BF16)
            dsc.append(jnp.sum(dm[:, cs] * _dot(pb, wpb), axis=0, keepdims=True))
            dye = dme[:, cs].astype(BF16)
            dwps.append(_dot(pb, dye[:tm], TN))
            dpe = _dot(dye, wpb, NT)
            acc = dpe / jnp.minimum(tok + 1, w).astype(F32)
            sh = 1
            while sh < w:
                acc = acc + pltpu.roll(acc, n_ext - sh, axis=0)
                sh *= 2
            dps.append(acc[:tm] - dpe[:tm])
        dp_ref[...] = jnp.concatenate(dps, axis=1)
        dsc_row = jnp.concatenate(dsc, axis=1)

        @pl.when(t == 0)
        def _():
            for g in range(ng):
                dwp_ref[g] = dwps[g]

        @pl.when(t > 0)
        def _():
            for g in range(ng):
                dwp_ref[g] = dwp_ref[g] + dwps[g]

        _acc_rows(st_ref, t, [dsc_row])

    x_specs, x_shapes, x_scratch = _plan_extras(plan)
    return pl.pallas_call(
        _fuse_exchange(body, 6, 3, 0, plan, 1), name="pool_bwd", grid=(nt,),
        in_specs=[pl.BlockSpec((tm, pw), lambda t: (t, 4)),
                  pl.BlockSpec((POOL_HALO, pw), lambda t: (jnp.maximum(t * nb - 1, 0), 4)),
                  pl.BlockSpec((tm, pw), lambda t: (t, 1)),
                  pl.BlockSpec((POOL_HALO, pw), lambda t: (jnp.minimum((t + 1) * nb, s // POOL_HALO - 1), 1)),
                  _const(wp.shape), _const((1, pw))] + x_specs,
        out_specs=[pl.BlockSpec((tm, pw), lambda t: (t, 0)), _const_out(wp.shape), _const_out((8, pw))] + x_specs,
        out_shape=[jax.ShapeDtypeStruct((s, pw), F32), jax.ShapeDtypeStruct(wp.shape, F32),
                   jax.ShapeDtypeStruct((8, pw), F32)] + x_shapes,
        scratch_shapes=x_scratch,
        compiler_params=_params(("arbitrary",)),
    )(z, z, dmix, dmix, wp, scale, *plan_args)


def _hgrn_bwd(z, o, dmix, st, l0, l1, gn, tc, unroll=1, plan=None, plan_args=()):
    s = z.shape[0]
    nsub = tc // SUB
    nt = s // tc
    hd = HEAD_DIM

    def body(q_ref, f_ref, v_ref, g_ref, l0_ref, l1_ref, gn_ref, o_ref, dm_ref, st_ref,
             dq_ref, df_ref, di_ref, dg_ref, stat_ref, dstate, qs, ks, bs, dos, dqs, dks, dbs):
        t = pl.program_id(1)

        @pl.when(t == 0)
        def _():
            dstate[...] = jnp.zeros_like(dstate)

        lb = _lower_bound(l0_ref[...], l1_ref[...])
        qp = q_ref[...]
        q, sq, f, sf = _hgrn_gates(qp, f_ref[...], lb)
        qs[...] = q
        ks[...] = 1.0 - f
        bs[...] = jnp.dot(_block_tri(tc, False), jnp.log(f), precision=HIGHEST, preferred_element_type=F32)

        o = o_ref[...]
        r = lax.rsqrt(jnp.mean(o * o, axis=-1, keepdims=True) + EPS)
        n = o * r
        gnv = gn_ref[...]
        gp = g_ref[...]
        sg = _sigmoid(gp)
        dm = dm_ref[...]
        dg_ref[...] = dm * (n * gnv) * (sg * (1.0 + gp * (1.0 - sg)))
        don = dm * (gp * sg)
        dgn = jnp.sum(don * n, axis=0, keepdims=True)
        dn = don * gnv
        dos[...] = r * (dn - n * jnp.mean(dn * n, axis=-1, keepdims=True))
        rows = lax.broadcasted_iota(jnp.int32, (SUB, 1), 0)

        def step(i, carry):
            ii = nsub - 1 - i
            r0 = pl.multiple_of(ii * SUB, SUB)
            q_ = qs[pl.ds(r0, SUB), :]
            k_ = ks[pl.ds(r0, SUB), :]
            b_ = bs[pl.ds(r0, SUB), :]
            v_ = v_ref[pl.ds(r0, SUB), :]
            do_ = dos[pl.ds(r0, SUB), :]
            stp = st_ref[ii]
            dst = dstate[...]
            bl = b_[SUB - 1:SUB, :]
            eb = jnp.exp(b_)
            ekl = jnp.exp(bl - b_)
            ebl = jnp.exp(bl)
            dob = do_.astype(BF16)
            dstb = dst.astype(BF16)
            kt = k_ * ekl
            dq = _dot(dob, stp.astype(BF16)) * eb
            dkt = _dot(v_.astype(BF16), dstb)
            dk = dkt * ekl
            dv = _dot(kt.astype(BF16), dstb, NT)
            extra = jnp.sum(kt * dkt, axis=0, keepdims=True) + ebl * jnp.sum(stp * dst, axis=0, keepdims=True)
            for j in range(SUB):
                e = jnp.exp(jnp.minimum(b_ - b_[j:j + 1, :], 0.0))
                pe = q_ * e
                kj = k_[j:j + 1, :]
                keep = rows >= j
                acol = jnp.where(keep, jnp.sum(pe * kj, axis=-1, keepdims=True), 0.0)
                dacol = jnp.where(keep, jnp.sum(do_ * v_[j:j + 1, :], axis=-1, keepdims=True), 0.0)
                dq = dq + dacol * (e * kj)
                at_j = rows == j
                dk = dk + jnp.where(at_j, jnp.sum(dacol * pe, axis=0, keepdims=True), 0.0)
                dv = dv + jnp.where(at_j, jnp.sum(acol * do_, axis=0, keepdims=True), 0.0)
            dqs[pl.ds(r0, SUB), :] = dq
            dks[pl.ds(r0, SUB), :] = dk
            di_ref[pl.ds(r0, SUB), :] = dv
            dbs[pl.ds(r0, SUB), :] = q_ * dq - k_ * dk + jnp.where(rows == SUB - 1, extra, 0.0)
            dstate[...] = dst * ebl + _dot(dob, (q_ * eb).astype(BF16), TN)
            return carry

        lax.fori_loop(0, nsub, step, 0, unroll=unroll)
        dlf = jnp.dot(_block_tri(tc, True), dbs[...], precision=HIGHEST, preferred_element_type=F32)
        dfv = dlf / f - dks[...]
        df_ref[...] = dfv * (1.0 - lb) * sf * (1.0 - sf)
        dlb = jnp.sum(dfv * (1.0 - sf), axis=0, keepdims=True)
        dq_ref[...] = dqs[...] * (sq * (1.0 + qp * (1.0 - sq)))
        _acc_rows(stat_ref, t, [dgn, dlb])

    rev = lambda t: nt - 1 - t
    col = lambda k: pl.BlockSpec((tc, hd), lambda h, t: (rev(t), k * HGRN_HEADS + h))
    vec = pl.BlockSpec((None, 1, hd), lambda h, t: (h, 0, 0))
    head = pl.BlockSpec((tc, hd), lambda h, t: (rev(t), h))
    x_specs, x_shapes, x_scratch = _plan_extras(plan)
    return pl.pallas_call(
        _fuse_exchange(body, 10, 5, 8, plan, 2), name="hgrn_bwd", grid=(HGRN_HEADS, nt),
        in_specs=[col(0), col(1), col(2), col(3), vec, vec, vec, head, head,
                  pl.BlockSpec((None, nsub, hd, hd), lambda h, t: (h, rev(t), 0, 0))] + x_specs,
        out_specs=[head, head, head, head, pl.BlockSpec((None, 8, hd), lambda h, t: (h, 0, 0))] + x_specs,
        out_shape=[jax.ShapeDtypeStruct((s, HGRN_WIDTH), F32)] * 4 + [jax.ShapeDtypeStruct((HGRN_HEADS, 8, hd), F32)]
                  + x_shapes,
        scratch_shapes=[pltpu.VMEM((hd, hd), F32)] + [pltpu.VMEM((tc, hd), F32)] * 7 + x_scratch,
        compiler_params=_params(("arbitrary", "arbitrary")),
    )(z, z, z, z, l0, l1, gn, o, dmix, st, *plan_args)


def _in_bwd(dparts, dx1, x, g, win_g, tm):
    s, d = x.shape
    nsh, _, wc = win_g.shape
    pw = dparts[0].shape[1]

    def body(dq_ref, df_ref, di_ref, dg_ref, dp_ref, dx1_ref, x_ref, g_ref, w_ref, gx_ref, dz_ref, st_ref):
        dz = jnp.concatenate([dq_ref[...], df_ref[...], di_ref[...], dg_ref[...], dp_ref[...]], axis=1).astype(BF16)
        dz_ref[...] = dz
        dh = jnp.zeros((tm, d), F32)
        for j in range(nsh):
            dh = dh + _dot(dz[:, j * wc:(j + 1) * wc], w_ref[j], NT)
        gv = g_ref[...]
        _, n, r = _rms_fwd(x_ref[...], gv)
        dxn, dg = _rms_bwd(dh, n, r, gv)
        gx_ref[...] = dx1_ref[...] + dxn
        _acc_rows(st_ref, pl.program_id(0), [dg])

    tile = lambda w: pl.BlockSpec((tm, w), lambda t: (t, 0))
    return pl.pallas_call(
        body, name="in_bwd", grid=(s // tm,),
        in_specs=[tile(pw)] * 5 + [tile(d), tile(d), _const((1, d)), _const(win_g.shape)],
        out_specs=[tile(d), tile(nsh * wc), _const_out((8, d))],
        out_shape=[jax.ShapeDtypeStruct((s, d), F32), jax.ShapeDtypeStruct((s, nsh * wc), BF16),
                   jax.ShapeDtypeStruct((8, d), F32)],
        compiler_params=_params(("arbitrary",)),
    )(*dparts, dx1, x, g, win_g)


def _tn_grad(name, a, b, out_rows, out_cols, a_sharded, tr, tc):
    s = a.shape[0]
    nr, nc = out_rows // tr, out_cols // tc

    def body(a_ref, b_ref, o_ref):
        o_ref[...] = _dot(a_ref[...], b_ref[...], TN)

    a_map = (lambda j, i, k: (0, j * nr + i)) if a_sharded else (lambda j, i, k: (0, i))
    b_map = (lambda j, i, k: (0, k)) if a_sharded else (lambda j, i, k: (0, j * nc + k))
    return pl.pallas_call(
        body, name=name, grid=(N_CHIPS, nr, nc),
        in_specs=[pl.BlockSpec((s, tr), a_map), pl.BlockSpec((s, tc), b_map)],
        out_specs=pl.BlockSpec((None, tr, tc), lambda j, i, k: (j, i, k)),
        out_shape=jax.ShapeDtypeStruct((N_CHIPS, out_rows, out_cols), F32),
        compiler_params=_params(("parallel", "parallel", "parallel")),
    )(a, b)


FFN_NAMES = ("w_ff1", "w_ff2")
ATTN_NAMES = ("w_xo", "w_xq", "w_out", "w_xk", "w_xv")
EARLY_NAMES = FFN_NAMES + ATTN_NAMES
BIG_NAMES = EARLY_NAMES + ("w_in",)


def _halved(g):
    return g.reshape(N_CHIPS, 2, g.shape[1] // 2, g.shape[2])


def _pair_adds(names, gs, got, idx):
    pairs = [_grad_pair_add("grad_pair_add_" + k, g, r, idx, tr=min(256, g.shape[2])) for k, g, r in zip(names, gs, got)]
    return [p[0] for p in pairs], [p[1] for p in pairs]


def _step(x, mem, target, small, shards, idx):
    d = x.shape[1]
    l0 = small["lb_logits"][0].reshape(HGRN_HEADS, 1, HEAD_DIM)
    l1 = small["lb_logits"][1].reshape(HGRN_HEADS, 1, HEAD_DIM)
    gn = small["hgrn_norm_g"].reshape(HGRN_HEADS, 1, HEAD_DIM)
    wp = small["w_pool"].reshape(len(POOL_WINDOWS), HEAD_DIM, HEAD_DIM)
    psc = small["pool_scale"].reshape(1, -1)
    gmix, gx, gmem, gffn = (small[k].reshape(1, d) for k in ("norm_mix_g", "norm_x_g", "norm_mem_g", "norm_ffn_g"))
    gfin = small["final_norm_g"].reshape(1, d)

    (win_g,) = _run_exchange("gather_w_in", _WeightGather([shards["w_in"]]), [shards["w_in"]])
    z, h = _in_proj(x, gmix, win_g, tm=512)
    rest = [shards["slab"], shards["w_xo"]]
    o, oa, st, slab_g, wo_g = _hgrn_fwd(z, l0, l1, gn, tc=256, unroll=8, plan=_WeightGather(rest), plan_args=rest)
    ob = _pool_fwd(z, wp, psc, tm=512)
    xk, xv = _kv_proj(mem, gmem, slab_g)
    x1, mixed, hq, xq, att, x2 = _mix_xattn_fwd(x, oa, ob, gx, slab_g, wo_g, xk, xv, tm=256)
    a, hf, dx3, dx3b, st_loss = _mlp_loss_fwd(x2, gffn, gfin, slab_g, target, tm=256)

    da, u, dx2, dx2b, st_ffn = _mlp_bwd(dx3, dx3b, a, x2, gffn, slab_g, tm=256)
    dw = {}
    dw["w_ff1"] = _tn_grad("dw_ff1", hf, da, d, d, False, 512, 512)
    dw["w_ff2"] = _tn_grad("dw_ff2", u, dx3b, d, d, True, 512, 512)
    gs_ffn = [_halved(dw[k]) for k in FFN_NAMES]
    dx1, dx1b, dxq, dmix, dxk, dxv, st_x, *got_ffn = _xattn_mix_bwd(
        dx2, x1, xq, xk, xv, gx, slab_g, wo_g, tm=256, plan=_PairExchange(gs_ffn), plan_args=gs_ffn)
    dw["w_xo"] = _tn_grad("dw_xo", att, dx2b, d, d // N_CHIPS, False, 512, 256)
    dw["w_xq"] = _tn_grad("dw_xq", hq, dxq, d // N_CHIPS, d, True, 256, 512)
    dw["w_out"] = _tn_grad("dw_out", mixed, dx1b, d // N_CHIPS, d, True, 256, 512)
    dw["w_xk"], dw["w_xv"], st_mem = _kv_bwd(mem, gmem, dxk, dxv, slab_g)
    gs_attn = [_halved(dw[k]) for k in ATTN_NAMES]
    dp, d_wp, st_pool, *got_attn = _pool_bwd(z, dmix, wp, psc, tm=512, plan=_PairExchange(gs_attn), plan_args=gs_attn)
    keeps, sends = _pair_adds(EARLY_NAMES, gs_ffn + gs_attn, got_ffn + got_attn, idx)

    dq, df, di, dg, st_hgrn, *received = _hgrn_bwd(z, o, dmix, st, l0, l1, gn, tc=256, unroll=4,
                                                    plan=_ChipExchange(sends), plan_args=sends)
    grad_x, dz, st_mix = _in_bwd([dq, df, di, dg, dp], dx1, x, gmix, win_g, tm=256)
    gs_in = [_halved(_tn_grad("dw_in", h, dz, d, win_g.shape[2], False, 512, win_g.shape[2]))]
    got_in = _run_exchange("grad_pair_exchange_w_in", _PairExchange(gs_in), gs_in)
    keep_in, send_in = _pair_adds(("w_in",), gs_in, got_in, idx)
    recv_in = _run_exchange("grad_chip_exchange_w_in", _ChipExchange(send_in), send_in)

    partials = dict(zip(BIG_NAMES, zip(keeps + keep_in, list(received) + list(recv_in))))
    stats = dict(mix=st_mix, x=st_x, mem=st_mem, ffn=st_ffn, loss=st_loss, hgrn=st_hgrn, pool=st_pool)
    return grad_x, stats, d_wp, partials


def _place():
    x, y, c = lax.axis_index("x"), lax.axis_index("y"), lax.axis_index("c")
    return x, y, c, [(x, 1 - y), (1 - x, y), (1 - x, 1 - y)]


def _rcopy(src, dst, ssem, rsem, dev):
    return pltpu.make_async_remote_copy(src_ref=src, dst_ref=dst, send_sem=ssem, recv_sem=rsem,
                                        device_id=dev, device_id_type=MESH)


class _WeightGather:
    def __init__(self, shards):
        self.n = len(shards)
        self.rows = [w.shape[0] for w in shards]
        self.out_shape = [jax.ShapeDtypeStruct((N_CHIPS,) + w.shape, w.dtype) for w in shards]
        self.scratch_shapes = [pltpu.SemaphoreType.DMA((self.n,))] * 2 + [pltpu.SemaphoreType.DMA((self.n, 3))] * 4

    def _copies(self, ins, outs, sems, with_pass_on):
        lsem, lrsem, ssem, rsem, fsem, frsem = sems
        x, y, c, peers = _place()
        chip = 2 * x + y
        sib = (x, y, 1 - c)
        own = [_rcopy(ins[a], outs[a].at[chip], lsem.at[a], lrsem.at[a], sib) for a in range(self.n)]
        sends, arrived, passed, passed_in = [], [], [], []
        for a in range(self.n):
            hr = self.rows[a] // 2
            half = lambda who, hc, a=a, hr=hr: outs[a].at[who, pl.ds(hc * hr, hr), :]
            for r, (px, py) in enumerate(peers):
                pc = 2 * px + py
                sends.append(_rcopy(ins[a].at[pl.ds(c * hr, hr), :], half(chip, c), ssem.at[a, r], rsem.at[a, r],
                                    (px, py, c)))
                if with_pass_on:
                    arrived.append(_rcopy(half(pc, c), half(pc, c), ssem.at[a, r], rsem.at[a, r], (px, py, c)))
                    passed.append(_rcopy(half(pc, c), half(pc, c), fsem.at[a, r], frsem.at[a, r], sib))
                    passed_in.append(_rcopy(half(pc, 1 - c), half(pc, 1 - c), fsem.at[a, r], frsem.at[a, r], sib))
        return own, sends, arrived, passed, passed_in

    def start(self, ins, outs, sems):
        own, sends, _, _, _ = self._copies(ins, outs, sems, False)
        for cp in own + sends:
            cp.start()

    def finish(self, ins, outs, sems):
        own, sends, arrived, passed, passed_in = self._copies(ins, outs, sems, True)
        for got, fwd in zip(arrived, passed):
            got.wait_recv()
            fwd.start()
        for cp in passed_in:
            cp.wait_recv()
        for cp in sends + passed:
            cp.wait_send()
        for cp in own:
            cp.wait()


class _ChipExchange:
    def __init__(self, sends):
        self.n = len(sends)
        self.out_shape = [jax.ShapeDtypeStruct(g.shape, g.dtype) for g in sends]
        self.scratch_shapes = [pltpu.SemaphoreType.DMA((self.n, 3))] * 2

    def _copies(self, ins, outs, sems):
        ssem, rsem = sems
        _, _, c, peers = _place()
        return [_rcopy(ins[a].at[r], outs[a].at[r], ssem.at[a, r], rsem.at[a, r], (px, py, c))
                for a in range(self.n) for r, (px, py) in enumerate(peers)]

    def start(self, ins, outs, sems):
        for cp in self._copies(ins, outs, sems):
            cp.start()

    def finish(self, ins, outs, sems):
        for cp in self._copies(ins, outs, sems):
            cp.wait()


def _run_exchange(name, plan, arrays):
    n = plan.n

    def body(*refs):
        ins, outs, sems = refs[:n], refs[n:2 * n], refs[2 * n:]
        plan.start(ins, outs, sems)
        plan.finish(ins, outs, sems)

    return pl.pallas_call(
        body, name=name, in_specs=[ANY] * n, out_specs=[ANY] * n,
        out_shape=plan.out_shape, scratch_shapes=plan.scratch_shapes,
    )(*arrays)


class _PairExchange:
    def __init__(self, gs):
        self.n = len(gs)
        self.out_shape = [jax.ShapeDtypeStruct((g.shape[0],) + g.shape[2:], g.dtype) for g in gs]
        self.scratch_shapes = [pltpu.SemaphoreType.DMA((self.n,))] * 2

    def _copies(self, ins, outs, sems):
        ssem, rsem = sems
        x, y, c, _ = _place()
        return [_rcopy(ins[a].at[:, 1 - c], outs[a], ssem.at[a], rsem.at[a], (x, y, 1 - c)) for a in range(self.n)]

    def start(self, ins, outs, sems):
        for cp in self._copies(ins, outs, sems):
            cp.start()

    def finish(self, ins, outs, sems):
        for cp in self._copies(ins, outs, sems):
            cp.wait()


def _grad_pair_add(name, g, got, idx, tr):
    _, _, hr, cc = g.shape

    def body(idx_ref, g0, g1, g2, g3, r0, r1, r2, r3, keep_ref, send_ref):
        keep_ref[...] = g0[...] + r0[...]
        for q, (gq, rq) in enumerate(((g1, r1), (g2, r2), (g3, r3))):
            send_ref[q] = (gq[...] + rq[...]).astype(BF16)

    gspec = lambda q: pl.BlockSpec((None, None, tr, cc), lambda i, idx: (idx[1 + q], idx[0], i, 0))
    rspec = lambda q: pl.BlockSpec((None, tr, cc), lambda i, idx: (idx[1 + q], i, 0))
    return pl.pallas_call(
        body, name=name,
        grid_spec=pltpu.PrefetchScalarGridSpec(
            num_scalar_prefetch=1, grid=(hr // tr,),
            in_specs=[gspec(q) for q in range(4)] + [rspec(q) for q in range(4)],
            out_specs=[pl.BlockSpec((tr, cc), lambda i, idx: (i, 0)), pl.BlockSpec((3, tr, cc), lambda i, idx: (0, i, 0))]),
        out_shape=[jax.ShapeDtypeStruct((hr, cc), F32), jax.ShapeDtypeStruct((3, hr, cc), BF16)],
        compiler_params=_params(("parallel",)),
    )(idx, g, g, g, g, got, got, got, got)


def _grad_chip_add(name, keep, got, tr):
    hr, cc = keep.shape

    def body(k_ref, g_ref, o_ref):
        o_ref[...] = ((k_ref[...] + g_ref[0].astype(F32)) + g_ref[1].astype(F32)) + g_ref[2].astype(F32)

    return pl.pallas_call(
        body, name=name, grid=(hr // tr,),
        in_specs=[pl.BlockSpec((tr, cc), lambda i: (i, 0)), pl.BlockSpec((3, tr, cc), lambda i: (0, i, 0))],
        out_specs=pl.BlockSpec((tr, cc), lambda i: (i, 0)),
        out_shape=jax.ShapeDtypeStruct((hr, cc), F32),
        compiler_params=_params(("parallel",)),
    )(keep, got)


def _grad_half_exchange(ts):
    n = len(ts)

    def body(*refs):
        ins, outs, ssem, rsem = refs[:n], refs[n:2 * n], refs[2 * n], refs[2 * n + 1]
        x, y, c, _ = _place()
        cps = [_rcopy(ins[a], outs[a], ssem.at[a], rsem.at[a], (x, y, 1 - c)) for a in range(n)]
        for cp in cps:
            cp.start()
        for cp in cps:
            cp.wait()

    return pl.pallas_call(
        body, name="grad_half_exchange",
        in_specs=[ANY] * n, out_specs=[ANY] * n,
        out_shape=[jax.ShapeDtypeStruct(t.shape, t.dtype) for t in ts],
        scratch_shapes=[pltpu.SemaphoreType.DMA((n,))] * 2,
    )(*ts)


def _small_allreduce(stats, d_wp):
    d = D_MODEL
    half = d // 2
    wps = d_wp.shape

    def body(mix_ref, x_ref, mem_ref, ffn_ref, loss_ref, hg_ref, pool_ref, wp_ref, slab_out, wp_out,
             slab_buf, wp_buf, sib_s, sib_w, ssem, rsem):
        x, y, c, peers = _place()
        chip = 2 * x + y
        sib = (x, y, 1 - c)
        hgn = jnp.concatenate([hg_ref[h, 0:1, :] for h in range(HGRN_HEADS)], axis=1)
        dlb = jnp.concatenate([hg_ref[h, 1:2, :] for h in range(HGRN_HEADS)], axis=1)
        slab_buf[0] = jnp.concatenate([
            mix_ref[0:1, :], x_ref[0:1, :], mem_ref[0:1, :], ffn_ref[0:1, :], loss_ref[0:1, :],
            jnp.concatenate([dlb, hgn], axis=1),
            jnp.concatenate([pool_ref[0:1, :], jnp.zeros((1, half), F32)], axis=1),
            loss_ref[1:2, :]], axis=0)
        wp_buf[0] = wp_ref[...]
        pair = [_rcopy(slab_buf.at[0], sib_s, ssem.at[0], rsem.at[0], sib),
                _rcopy(wp_buf.at[0], sib_w, ssem.at[1], rsem.at[1], sib)]
        for cp in pair:
            cp.start()
        for cp in pair:
            cp.wait()
        slab_buf[0] = slab_buf[0] + sib_s[...]
        wp_buf[0] = wp_buf[0] + sib_w[...]
        cps = []
        for r, (px, py) in enumerate(peers):
            cps.append(_rcopy(slab_buf.at[0], slab_buf.at[r + 1], ssem.at[2 + 2 * r], rsem.at[2 + 2 * r], (px, py, c)))
            cps.append(_rcopy(wp_buf.at[0], wp_buf.at[r + 1], ssem.at[3 + 2 * r], rsem.at[3 + 2 * r], (px, py, c)))
        for cp in cps:
            cp.start()
        for cp in cps:
            cp.wait()
        tot_s, tot_w = slab_buf[chip], wp_buf[chip]
        for j in range(1, N_CHIPS):
            tot_s = tot_s + slab_buf[jnp.bitwise_xor(j, chip)]
            tot_w = tot_w + wp_buf[jnp.bitwise_xor(j, chip)]
        slab_out[...] = tot_s
        wp_out[...] = tot_w

    return pl.pallas_call(
        body, name="small_allreduce",
        in_specs=[VMEM] * 8, out_specs=[VMEM] * 2,
        out_shape=[jax.ShapeDtypeStruct((8, d), F32), jax.ShapeDtypeStruct(wps, F32)],
        scratch_shapes=[pltpu.VMEM((N_CHIPS, 8, d), F32), pltpu.VMEM((N_CHIPS,) + wps, F32),
                        pltpu.VMEM((8, d), F32), pltpu.VMEM(wps, F32),
                        pltpu.SemaphoreType.DMA((8,)), pltpu.SemaphoreType.DMA((8,))],
    )(stats["mix"], stats["x"], stats["mem"], stats["ffn"], stats["loss"], stats["hgrn"], stats["pool"], d_wp)


def _adamw_math(w, g, m, v):
    m = ADAM_B1 * m + (1.0 - ADAM_B1) * g
    v = ADAM_B2 * v + (1.0 - ADAM_B2) * (g * g)
    m_hat = m / (1.0 - ADAM_B1 ** ADAM_STEP)
    v_hat = v / (1.0 - ADAM_B2 ** ADAM_STEP)
    delta = -ADAM_LR * (m_hat / (jnp.sqrt(v_hat) + ADAM_EPS) + ADAM_WD * w)
    return delta, m, v


def _adamw(name, mine, theirs, w, m, v, idx, tr):
    rows = w.shape[0]
    cc = mine.shape[1]
    nb = rows // 2 // tr
    heads = w.shape[1] if w.ndim == 3 else 1
    e = cc // heads

    def body(idx_ref, a_ref, b_ref, w_ref, m_ref, v_ref, g_out, d_out, m_out, v_out):
        g = jnp.where(pl.program_id(0) // nb == idx_ref[0], a_ref[...], b_ref[...])
        if w.ndim == 2:
            g_out[...] = g
            d_out[...], m_out[...], v_out[...] = _adamw_math(w_ref[...], g, m_ref[...], v_ref[...])
        else:
            for h in range(heads):
                gh = g[:, h * e:(h + 1) * e]
                g_out[:, h, :] = gh
                d_out[:, h, :], m_out[:, h, :], v_out[:, h, :] = _adamw_math(
                    w_ref[:, h, :], gh, m_ref[:, h, :], v_ref[:, h, :])

    hspec = pl.BlockSpec((tr, cc), lambda i, idx: (i % nb, 0))
    spec = pl.BlockSpec((tr,) + w.shape[1:], lambda i, idx: (i,) + (0,) * (w.ndim - 1))
    return pl.pallas_call(
        body, name=name,
        grid_spec=pltpu.PrefetchScalarGridSpec(
            num_scalar_prefetch=1, grid=(rows // tr,),
            in_specs=[hspec, hspec, spec, spec, spec], out_specs=[spec] * 4),
        out_shape=[jax.ShapeDtypeStruct(w.shape, F32)] * 4,
        compiler_params=_params(("parallel",)),
    )(idx, mine, theirs, w, m, v)


SMALL_NAMES = ("norm_mix_g", "lb_logits", "hgrn_norm_g", "w_pool", "pool_scale", "norm_x_g", "norm_mem_g",
               "norm_ffn_g", "final_norm_g")


def _small_update(slab, d_wp, ws, ms, vs):
    n = len(SMALL_NAMES)
    half = D_MODEL // 2

    def body(slab_ref, wp_ref, *refs):
        w_refs, m_refs, v_refs, outs = refs[:n], refs[n:2 * n], refs[2 * n:3 * n], refs[3 * n:]
        row = lambda k: slab_ref[k:k + 1, :]
        lbl = w_refs[SMALL_NAMES.index("lb_logits")][...]
        s0 = _lower_bound(lbl[0:1, :], lbl[1:2, :])
        dl0 = row(ROW_LB_HGN)[:, :half] * s0 * (1.0 - s0)
        grads = dict(norm_mix_g=row(ROW_GMIX), lb_logits=jnp.concatenate([dl0, -dl0], axis=0),
                     hgrn_norm_g=row(ROW_LB_HGN)[:, half:], w_pool=wp_ref[...], pool_scale=row(ROW_PSCALE)[:, :half],
                     norm_x_g=row(ROW_GX), norm_mem_g=row(ROW_GMEM), norm_ffn_g=row(ROW_GFFN),
                     final_norm_g=row(ROW_GFIN))
        outs[0][...] = row(ROW_LOSS)[:, :128]
        for i, name in enumerate(SMALL_NAMES):
            g = grads[name]
            delta, m2, v2 = _adamw_math(w_refs[i][...], g, m_refs[i][...], v_refs[i][...])
            for o, val in zip(outs[1 + 4 * i:5 + 4 * i], (g, delta, m2, v2)):
                o[...] = val

    args = [ws[k] for k in SMALL_NAMES] + [ms[k] for k in SMALL_NAMES] + [vs[k] for k in SMALL_NAMES]
    out_shape = [jax.ShapeDtypeStruct((1, 128), F32)]
    for k in SMALL_NAMES:
        out_shape += [jax.ShapeDtypeStruct(ws[k].shape, F32)] * 4
    res = pl.pallas_call(
        body, name="small_update",
        in_specs=[VMEM] * (2 + 3 * n), out_specs=[VMEM] * len(out_shape), out_shape=out_shape,
    )(slab, d_wp, *args)
    return res[0], {k: res[1 + 4 * i:5 + 4 * i] for i, k in enumerate(SMALL_NAMES)}


ALL_NAMES = ("norm_mix_g", "w_in", "lb_logits", "hgrn_norm_g", "w_pool", "pool_scale", "w_out", "norm_x_g",
             "norm_mem_g", "w_xq", "w_xk", "w_xv", "w_xo", "norm_ffn_g", "w_ff1", "w_ff2", "final_norm_g")


def _shard_2d(name, a):
    a = a[0]
    if name in ("w_xq", "w_xk", "w_xv"):
        return a.reshape(a.shape[0], -1)
    if name == "w_xo":
        return a.reshape(-1, a.shape[-1])
    return a


def _small_2d(name, a):
    if name == "w_pool":
        return a.reshape(-1, HEAD_DIM)
    if name == "lb_logits":
        return a
    return a.reshape(1, -1)


def kernel(x, mem, norm_mix_g, w_in, lb_logits, hgrn_norm_g, w_pool, pool_scale, w_out, norm_x_g, norm_mem_g, w_xq, w_xk, w_xv, w_xo, norm_ffn_g, w_ff1, w_ff2, final_norm_g, loss_target, m_norm_mix_g, m_w_in, m_lb_logits, m_hgrn_norm_g, m_w_pool, m_pool_scale, m_w_out, m_norm_x_g, m_norm_mem_g, m_w_xq, m_w_xk, m_w_xv, m_w_xo, m_norm_ffn_g, m_w_ff1, m_w_ff2, m_final_norm_g, v_norm_mix_g, v_w_in, v_lb_logits, v_hgrn_norm_g, v_w_pool, v_pool_scale, v_w_out, v_norm_x_g, v_norm_mem_g, v_w_xq, v_w_xk, v_w_xv, v_w_xo, v_norm_ffn_g, v_w_ff1, v_w_ff2, v_final_norm_g):
    w = dict(norm_mix_g=norm_mix_g, w_in=w_in, lb_logits=lb_logits, hgrn_norm_g=hgrn_norm_g, w_pool=w_pool, pool_scale=pool_scale, w_out=w_out, norm_x_g=norm_x_g, norm_mem_g=norm_mem_g, w_xq=w_xq, w_xk=w_xk, w_xv=w_xv, w_xo=w_xo, norm_ffn_g=norm_ffn_g, w_ff1=w_ff1, w_ff2=w_ff2, final_norm_g=final_norm_g)
    m = dict(norm_mix_g=m_norm_mix_g, w_in=m_w_in, lb_logits=m_lb_logits, hgrn_norm_g=m_hgrn_norm_g, w_pool=m_w_pool, pool_scale=m_pool_scale, w_out=m_w_out, norm_x_g=m_norm_x_g, norm_mem_g=m_norm_mem_g, w_xq=m_w_xq, w_xk=m_w_xk, w_xv=m_w_xv, w_xo=m_w_xo, norm_ffn_g=m_norm_ffn_g, w_ff1=m_w_ff1, w_ff2=m_w_ff2, final_norm_g=m_final_norm_g)
    v = dict(norm_mix_g=v_norm_mix_g, w_in=v_w_in, lb_logits=v_lb_logits, hgrn_norm_g=v_hgrn_norm_g, w_pool=v_w_pool, pool_scale=v_pool_scale, w_out=v_w_out, norm_x_g=v_norm_x_g, norm_mem_g=v_norm_mem_g, w_xq=v_w_xq, w_xk=v_w_xk, w_xv=v_w_xv, w_xo=v_w_xo, norm_ffn_g=v_norm_ffn_g, w_ff1=v_w_ff1, w_ff2=v_w_ff2, final_norm_g=v_final_norm_g)

    big_w = {k: _shard_2d(k, w[k]) for k in BIG_NAMES}
    slab = jnp.concatenate([big_w[k] for k in ("w_out", "w_xq", "w_xk", "w_xv", "w_ff1", "w_ff2")], axis=0).astype(BF16)
    shards = dict(slab=slab, w_in=big_w["w_in"].astype(BF16), w_xo=big_w["w_xo"].astype(BF16))

    cx, cy, cc = lax.axis_index("x"), lax.axis_index("y"), lax.axis_index("c")
    chip = 2 * cx + cy
    idx = jnp.stack([cc, chip, chip ^ 1, chip ^ 2, chip ^ 3]).astype(jnp.int32)
    small = {k: w[k] for k in SMALL_NAMES}
    grad_x, stats, d_wp, partials = _step(x[0], mem[0], loss_target[0], small, shards, idx)

    halves = [_grad_chip_add("grad_chip_add_" + k, *partials[k], tr=min(256, partials[k][0].shape[0]))
              for k in BIG_NAMES]
    theirs = _grad_half_exchange(halves)

    grads, deltas, new_m, new_v = {}, {}, {}, {}
    for k, mine, other in zip(BIG_NAMES, halves, theirs):
        as_held = (lambda a: a[0]) if k in ("w_xq", "w_xk", "w_xv") else functools.partial(_shard_2d, k)
        res = _adamw("adamw_" + k, mine, other, as_held(w[k]), as_held(m[k]), as_held(v[k]), idx,
                     tr=min(256, mine.shape[0]))
        for store, val in zip((grads, deltas, new_m, new_v), res):
            store[k] = val.reshape(w[k].shape)

    slab_sum, wp_sum = _small_allreduce(stats, d_wp.reshape(-1, HEAD_DIM))
    loss, upd = _small_update(slab_sum, wp_sum, {k: _small_2d(k, w[k]) for k in SMALL_NAMES},
                              {k: _small_2d(k, m[k]) for k in SMALL_NAMES}, {k: _small_2d(k, v[k]) for k in SMALL_NAMES})
    for k in SMALL_NAMES:
        for store, val in zip((grads, deltas, new_m, new_v), upd[k]):
            store[k] = val.reshape(w[k].shape)

    return (loss[0, 0], grad_x[None], *[grads[k] for k in ALL_NAMES], *[deltas[k] for k in ALL_NAMES],
            *[new_m[k] for k in ALL_NAMES], *[new_v[k] for k in ALL_NAMES])
```

```python
import functools

import jax
import jax.numpy as jnp
from jax import lax
from jax.experimental import pallas as pl
from jax.experimental.pallas import tpu as pltpu

F32 = jnp.float32
BF16 = jnp.bfloat16
LOG2E = 1.4426950408889634
NEG_BIG = -1e30
MESH = pl.DeviceIdType.MESH
ANY = pl.BlockSpec(memory_space=pl.ANY)
VMEM = pl.BlockSpec(memory_space=pltpu.VMEM)

D_MODEL = 1024
N_CHIPS = 4
HGRN_HEADS = 4
HEAD_DIM = 128
HGRN_WIDTH = HGRN_HEADS * HEAD_DIM
POOL_WINDOWS = (2, 4, 8, 16)
POOL_HALO = 16
SUB = 16
HALF = SUB // 2
XATTN_HEADS = 4
XATTN_HEAD_DIM = 256
EPS = 1e-6
ADAM_LR, ADAM_B1, ADAM_B2, ADAM_EPS, ADAM_WD, ADAM_STEP = 0.001, 0.9, 0.999, 1e-08, 0.01, 10

V7X_VMEM_BYTES = 64 * 1024 * 1024
VMEM_LIMIT = V7X_VMEM_BYTES - 8 * 1024 * 1024

NN = (((1,), (0,)), ((), ()))
NT = (((1,), (1,)), ((), ()))
TN = (((0,), (0,)), ((), ()))

ROW_GMIX, ROW_GX, ROW_GMEM, ROW_GFFN, ROW_GFIN, ROW_LB_HGN, ROW_PSCALE, ROW_LOSS = range(8)


def _dot(a, b, dims=NN):
    return lax.dot_general(a, b, dims, preferred_element_type=F32)


def _sigmoid(x):
    return 1.0 / (1.0 + jnp.exp(-x))


def _rms_fwd(x, g):
    r = lax.rsqrt(jnp.mean(x * x, axis=-1, keepdims=True) + EPS)
    n = x * r
    return n * g, n, r


def _rms_bwd(dh, n, r, g):
    dn = dh * g
    dx = r * (dn - n * jnp.mean(dn * n, axis=-1, keepdims=True))
    return dx, jnp.sum(dh * n, axis=0, keepdims=True)


def _params(sem=None):
    return pltpu.CompilerParams(dimension_semantics=sem, vmem_limit_bytes=VMEM_LIMIT)


def _const(shape):
    nd = len(shape)
    return pl.BlockSpec(shape, lambda *_: (0,) * nd, pipeline_mode=pl.Buffered(1))


def _const_out(shape):
    nd = len(shape)
    return pl.BlockSpec(shape, lambda *_: (0,) * nd)


def _acc_rows(ref, t, rows):
    upd = jnp.concatenate(rows + [jnp.zeros((8 - len(rows), rows[0].shape[1]), F32)], axis=0)

    @pl.when(t == 0)
    def _():
        ref[...] = upd

    @pl.when(t > 0)
    def _():
        ref[...] = ref[...] + upd


def _fuse_exchange(body, n_in, n_out, n_scratch, plan, ndim):
    if plan is None:
        return body
    n = plan.n

    def wrapped(*refs):
        ins, cin = refs[:n_in], refs[n_in:n_in + n]
        outs, cout = refs[n_in + n:n_in + n + n_out], refs[n_in + n + n_out:n_in + 2 * n + n_out]
        rest = refs[n_in + 2 * n + n_out:]
        scr, csem = rest[:n_scratch], rest[n_scratch:]
        first = pl.program_id(0) == 0
        last = pl.program_id(0) == pl.num_programs(0) - 1
        for i in range(1, ndim):
            first = first & (pl.program_id(i) == 0)
            last = last & (pl.program_id(i) == pl.num_programs(i) - 1)

        @pl.when(first)
        def _():
            plan.start(cin, cout, csem)

        body(*ins, *outs, *scr)

        @pl.when(last)
        def _():
            plan.finish(cin, cout, csem)

    return wrapped


def _plan_extras(plan):
    if plan is None:
        return [], [], []
    return [ANY] * plan.n, list(plan.out_shape), list(plan.scratch_shapes)


def _in_proj(x, g, win_g, tm):
    s, d = x.shape
    nsh, _, wc = win_g.shape

    def body(x_ref, g_ref, w_ref, z_ref, h_ref):
        h, _, _ = _rms_fwd(x_ref[...], g_ref[...])
        hb = h.astype(BF16)
        h_ref[...] = hb
        for j in range(nsh):
            z_ref[:, j * wc:(j + 1) * wc] = _dot(hb, w_ref[j])

    return pl.pallas_call(
        body, name="in_proj", grid=(s // tm,),
        in_specs=[pl.BlockSpec((tm, d), lambda t: (t, 0)), _const((1, d)), _const((nsh, d, wc))],
        out_specs=[pl.BlockSpec((tm, nsh * wc), lambda t: (t, 0)), pl.BlockSpec((tm, d), lambda t: (t, 0))],
        out_shape=[jax.ShapeDtypeStruct((s, nsh * wc), F32), jax.ShapeDtypeStruct((s, d), BF16)],
        compiler_params=_params(("parallel",)),
    )(x, g, win_g)


def _lower_bound(l0, l1):
    m = jnp.maximum(l0, l1)
    e0, e1 = jnp.exp(l0 - m), jnp.exp(l1 - m)
    return e0 / (e0 + e1)


def _block_tri(n, upper):
    r = lax.broadcasted_iota(jnp.int32, (n, n), 0)
    c = lax.broadcasted_iota(jnp.int32, (n, n), 1)
    keep = (r // SUB == c // SUB) & ((c >= r) if upper else (c <= r))
    return keep.astype(BF16)


def _group_cumsum(tri, x):
    hi = x.astype(BF16)
    rest = x - hi.astype(F32)
    mid = rest.astype(BF16)
    lo = (rest - mid.astype(F32)).astype(BF16)
    return (_dot(tri, hi) + _dot(tri, mid)) + _dot(tri, lo)


def _decay(b, bj, rows, first):
    d = b - bj
    if first:
        d = jnp.where(rows >= first, d, NEG_BIG)
    return jnp.exp2(d)


class _RowSums:
    ORDER = (0, 4, 2, 6, 1, 5, 3, 7)

    def __init__(self, rows):
        self.rows = rows
        self.level = {4: {}, 2: {}, 1: {}}

    def _pair(self, p, q, d):
        return jnp.where((self.rows & d) != 0, p + pltpu.roll(p, d, axis=0), q + pltpu.roll(q, HALF - d, axis=0))

    def push(self, j, y, d=4):
        if d == 0:
            self.out = y
            return
        slot = self.level[d]
        key = j % d
        if key not in slot:
            slot[key] = (j, y)
            return
        j0, y0 = slot.pop(key)
        p, q = (y, y0) if j & d else (y0, y)
        self.push(key, self._pair(p, q, d), d // 2)

    def result(self):
        return self.out


def _hgrn_gates(qp, fp, lb):
    sq = _sigmoid(qp)
    sf = _sigmoid(fp)
    f = lb + (1.0 - lb) * sf
    return qp * sq, sq, f, sf


def _hgrn_fwd(z, l0, l1, gn, tc, unroll=1, plan=None, plan_args=()):
    s = z.shape[0]
    nsub = tc // SUB
    hd = HEAD_DIM

    def body(q_ref, f_ref, v_ref, g_ref, l0_ref, l1_ref, gn_ref, tri_ref, o_ref, oa_ref, st_ref,
             state, qs, ks, bs, os_):
        @pl.when(pl.program_id(1) == 0)
        def _():
            state[...] = jnp.zeros_like(state)

        lb = _lower_bound(l0_ref[...], l1_ref[...])
        q, _, f, _ = _hgrn_gates(q_ref[...], f_ref[...], lb)
        qs[...] = q
        ks[...] = 1.0 - f
        bs[...] = _group_cumsum(tri_ref[...], jnp.log(f) * LOG2E)
        rows = lax.broadcasted_iota(jnp.int32, (HALF, 1), 0)

        def step(i, carry):
            r0 = pl.multiple_of(i * SUB, SUB)
            q_ = qs[pl.ds(r0, SUB), :]
            k_ = ks[pl.ds(r0, SUB), :]
            b_ = bs[pl.ds(r0, SUB), :]
            v_ = v_ref[pl.ds(r0, SUB), :]
            st = state[...]
            st_ref[i] = st
            bl = b_[SUB - 1:SUB, :]
            o = _dot((q_ * jnp.exp2(b_)).astype(BF16), st.astype(BF16), NT)
            (q_lo, q_hi), (b_lo, b_hi), (o_lo, o_hi) = ((a[:HALF], a[HALF:]) for a in (q_, b_, o))
            for j in range(SUB):
                bj, kj, vj = b_[j:j + 1, :], k_[j:j + 1, :], v_[j:j + 1, :]
                if j < HALF:
                    e = _decay(b_lo, bj, rows, j)
                    o_lo = o_lo + jnp.sum(q_lo * e * kj, axis=-1, keepdims=True) * vj
                e = _decay(b_hi, bj, rows, j - HALF if j > HALF else None)
                o_hi = o_hi + jnp.sum(q_hi * e * kj, axis=-1, keepdims=True) * vj
            os_[pl.ds(r0, HALF), :] = o_lo
            os_[pl.ds(r0 + HALF, HALF), :] = o_hi
            kt = (k_ * jnp.exp2(bl - b_)).astype(BF16)
            state[...] = st * jnp.exp2(bl) + _dot(v_.astype(BF16), kt, TN)
            return carry

        lax.fori_loop(0, nsub, step, 0, unroll=unroll)
        o = os_[...]
        o_ref[...] = o
        r = lax.rsqrt(jnp.mean(o * o, axis=-1, keepdims=True) + EPS)
        gp = g_ref[...]
        oa_ref[...] = (o * r * gn_ref[...] * (gp * _sigmoid(gp))).astype(BF16)

    col = lambda k: pl.BlockSpec((tc, hd), lambda h, t: (t, k * HGRN_HEADS + h))
    vec = pl.BlockSpec((None, 1, hd), lambda h, t: (h, 0, 0))
    x_specs, x_shapes, x_scratch = _plan_extras(plan)
    return pl.pallas_call(
        _fuse_exchange(body, 8, 3, 5, plan, 2), name="hgrn_fwd", grid=(HGRN_HEADS, s // tc),
        in_specs=[col(0), col(1), col(2), col(3), vec, vec, vec, _const((tc, tc))] + x_specs,
        out_specs=[pl.BlockSpec((tc, hd), lambda h, t: (t, h)), pl.BlockSpec((tc, hd), lambda h, t: (t, h)),
                   pl.BlockSpec((None, nsub, hd, hd), lambda h, t: (h, t, 0, 0))] + x_specs,
        out_shape=[jax.ShapeDtypeStruct((s, HGRN_WIDTH), F32), jax.ShapeDtypeStruct((s, HGRN_WIDTH), BF16),
                   jax.ShapeDtypeStruct((HGRN_HEADS, s // SUB, hd, hd), F32)] + x_shapes,
        scratch_shapes=[pltpu.VMEM((hd, hd), F32)] + [pltpu.VMEM((tc, hd), F32)] * 4 + x_scratch,
        compiler_params=_params(("arbitrary", "arbitrary")),
    )(z, z, z, z, l0, l1, gn, _block_tri(tc, False), *plan_args)


def _pooled(p, ext, tok0):
    tm = p.shape[0]
    tok = tok0 + lax.broadcasted_iota(jnp.int32, (tm, 1), 0)
    outs = []
    for g, w in enumerate(POOL_WINDOWS):
        acc = ext[:, g * HEAD_DIM:(g + 1) * HEAD_DIM]
        sh = 1
        while sh < w:
            acc = acc + pltpu.roll(acc, sh, axis=0)
            sh *= 2
        cnt = jnp.minimum(tok + 1, w).astype(F32)
        outs.append(acc[POOL_HALO:, :] / cnt - p[:, g * HEAD_DIM:(g + 1) * HEAD_DIM])
    return outs


def _pool_fwd(z, wp, scale, tm):
    s = z.shape[0]
    pw = len(POOL_WINDOWS) * HEAD_DIM
    nb = tm // POOL_HALO

    def body(p_ref, prev_ref, wp_ref, sc_ref, ob_ref):
        t = pl.program_id(0)
        p = p_ref[...]
        prev = jnp.where(t > 0, prev_ref[...], 0.0)
        pooled = _pooled(p, jnp.concatenate([prev, p], axis=0), t * tm)
        ys = [_dot(pooled[g].astype(BF16), wp_ref[g].astype(BF16)) for g in range(len(POOL_WINDOWS))]
        ob_ref[...] = (jnp.concatenate(ys, axis=1) * sc_ref[...]).astype(BF16)

    return pl.pallas_call(
        body, name="pool_fwd", grid=(s // tm,),
        in_specs=[pl.BlockSpec((tm, pw), lambda t: (t, 4)),
                  pl.BlockSpec((POOL_HALO, pw), lambda t: (jnp.maximum(t * nb - 1, 0), 4)),
                  _const(wp.shape), _const((1, pw))],
        out_specs=pl.BlockSpec((tm, pw), lambda t: (t, 0)),
        out_shape=jax.ShapeDtypeStruct((s, pw), BF16),
        compiler_params=_params(("parallel",)),
    )(z, z, wp, scale)


def _kv_proj(mem, g, slab_g):
    m, d = mem.shape
    rows = d // N_CHIPS

    def body(mem_ref, g_ref, wk_ref, wv_ref, xk_ref, xv_ref):
        hm, _, _ = _rms_fwd(mem_ref[...], g_ref[...])
        hb = hm.astype(BF16)
        xk_ref[...] = _dot(hb, wk_ref[...].reshape(d, d)).astype(BF16)
        xv_ref[...] = _dot(hb, wv_ref[...].reshape(d, d)).astype(BF16)

    blk = lambda k: pl.BlockSpec((N_CHIPS, rows, d), lambda i: (0, k, 0))
    return pl.pallas_call(
        body, name="kv_proj", grid=(1,),
        in_specs=[_const((m, d)), _const((1, d)), blk(2), blk(3)],
        out_specs=[_const_out((m, d)), _const_out((m, d))],
        out_shape=[jax.ShapeDtypeStruct((m, d), BF16)] * 2,
        compiler_params=_params(("arbitrary",)),
    )(mem, g, slab_g, slab_g)


def _softmax_rows(sc):
    e = jnp.exp(sc - jnp.max(sc, axis=-1, keepdims=True))
    return e / jnp.sum(e, axis=-1, keepdims=True)


def _mix_xattn_fwd(x, oa, ob, gx, slab_g, wo_g, xk, xv, tm):
    s, d = x.shape
    m = xk.shape[0]
    rows = d // N_CHIPS
    hw = oa.shape[1]
    e = XATTN_HEAD_DIM

    def body(x_ref, oa_ref, ob_ref, gx_ref, wout_ref, wq_ref, wo_ref, xk_ref, xv_ref,
             x1_ref, mixed_ref, hq_ref, xq_ref, att_ref, x2_ref):
        mixed = jnp.concatenate([oa_ref[...], ob_ref[...]], axis=1)
        mixed_ref[...] = mixed
        x1 = x_ref[...] + _dot(mixed, wout_ref[...].reshape(d, d))
        x1_ref[...] = x1
        hq, _, _ = _rms_fwd(x1, gx_ref[...])
        hqb = hq.astype(BF16)
        hq_ref[...] = hqb
        xq = _dot(hqb, wq_ref[...].reshape(d, d)).astype(BF16)
        xq_ref[...] = xq
        atts = []
        for h in range(XATTN_HEADS):
            cs = slice(h * e, (h + 1) * e)
            p = _softmax_rows(_dot(xq[:, cs], xk_ref[:, cs], NT) * (e ** -0.5))
            atts.append(_dot(p.astype(BF16), xv_ref[:, cs]).astype(BF16))
        att = jnp.concatenate(atts, axis=1)
        att_ref[...] = att
        for j in range(N_CHIPS):
            x2_ref[:, j * rows:(j + 1) * rows] = x1[:, j * rows:(j + 1) * rows] + _dot(att, wo_ref[j])

    tile = lambda w: pl.BlockSpec((tm, w), lambda t: (t, 0))
    blk = lambda k: pl.BlockSpec((N_CHIPS, rows, d), lambda t: (0, k, 0), pipeline_mode=pl.Buffered(1))
    return pl.pallas_call(
        body, name="mix_xattn_fwd", grid=(s // tm,),
        in_specs=[tile(d), tile(hw), tile(hw), _const((1, d)), blk(0), blk(1), _const(wo_g.shape),
                  _const((m, d)), _const((m, d))],
        out_specs=[tile(d)] * 6,
        out_shape=[jax.ShapeDtypeStruct((s, d), F32)] + [jax.ShapeDtypeStruct((s, d), BF16)] * 4
                  + [jax.ShapeDtypeStruct((s, d), F32)],
        compiler_params=_params(("parallel",)),
    )(x, oa, ob, gx, slab_g, slab_g, wo_g, xk, xv)


def _mlp_loss_fwd(x2, gffn, gfin, slab_g, target, tm):
    s, d = x2.shape
    wr = slab_g.shape[1] // 3

    def body(x2_ref, gffn_ref, gfin_ref, w1_ref, w2_ref, tg_ref, a_ref, hf_ref, dx3_ref, dx3b_ref, st_ref):
        x2v = x2_ref[...]
        hf, _, _ = _rms_fwd(x2v, gffn_ref[...])
        hfb = hf.astype(BF16)
        hf_ref[...] = hfb
        acc = x2v
        for j in range(N_CHIPS):
            a = _dot(hfb, w1_ref[j])
            a_ref[:, j * wr:(j + 1) * wr] = a
            r = jnp.maximum(a, 0.0)
            acc = acc + _dot((r * r).astype(BF16), w2_ref[j])
        gf = gfin_ref[...]
        y, n, r3 = _rms_fwd(acc, gf)
        err = y - tg_ref[...]
        loss = 0.5 * jnp.sum(jnp.sum(err * err, axis=-1, keepdims=True) * (1.0 / d), axis=0, keepdims=True)
        dy = err * (1.0 / d)
        dx3, dgf = _rms_bwd(dy, n, r3, gf)
        dx3_ref[...] = dx3
        dx3b_ref[...] = dx3.astype(BF16)
        _acc_rows(st_ref, pl.program_id(0), [dgf, jnp.broadcast_to(loss, (1, d))])

    tile = lambda w: pl.BlockSpec((tm, w), lambda t: (t, 0))
    blk = lambda k: pl.BlockSpec((N_CHIPS, wr, d), lambda t: (0, k, 0), pipeline_mode=pl.Buffered(1))
    return pl.pallas_call(
        body, name="mlp_loss_fwd", grid=(s // tm,),
        in_specs=[tile(d), _const((1, d)), _const((1, d)), blk(1), blk(2), tile(d)],
        out_specs=[tile(N_CHIPS * wr), tile(d), tile(d), tile(d), _const_out((8, d))],
        out_shape=[jax.ShapeDtypeStruct((s, N_CHIPS * wr), F32), jax.ShapeDtypeStruct((s, d), BF16),
                   jax.ShapeDtypeStruct((s, d), F32), jax.ShapeDtypeStruct((s, d), BF16),
                   jax.ShapeDtypeStruct((8, d), F32)],
        compiler_params=_params(("arbitrary",)),
    )(x2, gffn, gfin, slab_g, slab_g, target)


def _mlp_bwd(dx3, dx3b, a, x2, gffn, slab_g, tm):
    s, d = x2.shape
    wr = slab_g.shape[1] // 3

    def body(dx3_ref, dx3b_ref, a_ref, x2_ref, g_ref, w1_ref, w2_ref, da_ref, u_ref, dx2_ref, dx2b_ref, st_ref):
        dyb = dx3b_ref[...]
        dhf = jnp.zeros((tm, d), F32)
        for j in range(N_CHIPS):
            r = jnp.maximum(a_ref[:, j * wr:(j + 1) * wr], 0.0)
            da = (_dot(dyb, w2_ref[j], NT) * (2.0 * r)).astype(BF16)
            da_ref[:, j * wr:(j + 1) * wr] = da
            u_ref[:, j * wr:(j + 1) * wr] = (r * r).astype(BF16)
            dhf = dhf + _dot(da, w1_ref[j], NT)
        g = g_ref[...]
        _, n, r2 = _rms_fwd(x2_ref[...], g)
        dxn, dg = _rms_bwd(dhf, n, r2, g)
        dx2 = dx3_ref[...] + dxn
        dx2_ref[...] = dx2
        dx2b_ref[...] = dx2.astype(BF16)
        _acc_rows(st_ref, pl.program_id(0), [dg])

    tile = lambda w: pl.BlockSpec((tm, w), lambda t: (t, 0))
    blk = lambda k: pl.BlockSpec((N_CHIPS, wr, d), lambda t: (0, k, 0), pipeline_mode=pl.Buffered(1))
    nf = N_CHIPS * wr
    return pl.pallas_call(
        body, name="mlp_bwd", grid=(s // tm,),
        in_specs=[tile(d), tile(d), tile(nf), tile(d), _const((1, d)), blk(1), blk(2)],
        out_specs=[tile(nf), tile(nf), tile(d), tile(d), _const_out((8, d))],
        out_shape=[jax.ShapeDtypeStruct((s, nf), BF16), jax.ShapeDtypeStruct((s, nf), BF16),
                   jax.ShapeDtypeStruct((s, d), F32), jax.ShapeDtypeStruct((s, d), BF16),
                   jax.ShapeDtypeStruct((8, d), F32)],
        compiler_params=_params(("arbitrary",)),
    )(dx3, dx3b, a, x2, gffn, slab_g, slab_g)


def _xattn_mix_bwd(dx2, x1, xq, xk, xv, gx, slab_g, wo_g, tm, plan=None, plan_args=()):
    s, d = x1.shape
    m = xk.shape[0]
    rows = d // N_CHIPS
    e = XATTN_HEAD_DIM

    def body(dx2_ref, x1_ref, xq_ref, xk_ref, xv_ref, gx_ref, wout_ref, wq_ref, wo_ref,
             dx1_ref, dx1b_ref, dxq_ref, dmix_ref, dxk_ref, dxv_ref, st_ref):
        t = pl.program_id(0)
        dx2 = dx2_ref[...]
        dx2b = dx2.astype(BF16)
        datt = jnp.zeros((tm, d), F32)
        for j in range(N_CHIPS):
            datt = datt + _dot(dx2b[:, j * rows:(j + 1) * rows], wo_ref[j], NT)
        dattb = datt.astype(BF16)
        dxqs, dxks, dxvs = [], [], []
        for h in range(XATTN_HEADS):
            cs = slice(h * e, (h + 1) * e)
            xq_h, xk_h, xv_h = xq_ref[:, cs], xk_ref[:, cs], xv_ref[:, cs]
            p = _softmax_rows(_dot(xq_h, xk_h, NT) * (e ** -0.5))
            dp = _dot(dattb[:, cs], xv_h, NT)
            ds = (p * (dp - jnp.sum(dp * p, axis=-1, keepdims=True)) * (e ** -0.5)).astype(BF16)
            dxqs.append(_dot(ds, xk_h).astype(BF16))
            dxks.append(_dot(ds, xq_h, TN))
            dxvs.append(_dot(p.astype(BF16), dattb[:, cs], TN))
        dxq = jnp.concatenate(dxqs, axis=1)
        dxq_ref[...] = dxq
        dxk = jnp.concatenate(dxks, axis=1)
        dxv = jnp.concatenate(dxvs, axis=1)

        @pl.when(t == 0)
        def _():
            dxk_ref[...] = dxk
            dxv_ref[...] = dxv

        @pl.when(t > 0)
        def _():
            dxk_ref[...] = dxk_ref[...] + dxk
            dxv_ref[...] = dxv_ref[...] + dxv

        dhq = jnp.concatenate([_dot(dxq, wq_ref[j], NT) for j in range(N_CHIPS)], axis=1)
        g = gx_ref[...]
        _, n, r1 = _rms_fwd(x1_ref[...], g)
        dxn, dg = _rms_bwd(dhq, n, r1, g)
        dx1 = dx2 + dxn
        dx1_ref[...] = dx1
        dx1b = dx1.astype(BF16)
        dx1b_ref[...] = dx1b
        for j in range(N_CHIPS):
            dmix_ref[:, j * rows:(j + 1) * rows] = _dot(dx1b, wout_ref[j], NT)
        _acc_rows(st_ref, t, [dg])

    tile = lambda: pl.BlockSpec((tm, d), lambda t: (t, 0))
    blk = lambda k: pl.BlockSpec((N_CHIPS, rows, d), lambda t: (0, k, 0), pipeline_mode=pl.Buffered(1))
    x_specs, x_shapes, x_scratch = _plan_extras(plan)
    return pl.pallas_call(
        _fuse_exchange(body, 9, 7, 0, plan, 1), name="xattn_mix_bwd", grid=(s // tm,),
        in_specs=[tile(), tile(), tile(), _const((m, d)), _const((m, d)), _const((1, d)), blk(0), blk(1),
                  _const(wo_g.shape)] + x_specs,
        out_specs=[tile(), tile(), tile(), tile(), _const_out((m, d)), _const_out((m, d)), _const_out((8, d))]
                  + x_specs,
        out_shape=[jax.ShapeDtypeStruct((s, d), F32), jax.ShapeDtypeStruct((s, d), BF16),
                   jax.ShapeDtypeStruct((s, d), BF16), jax.ShapeDtypeStruct((s, d), F32),
                   jax.ShapeDtypeStruct((m, d), F32), jax.ShapeDtypeStruct((m, d), F32),
                   jax.ShapeDtypeStruct((8, d), F32)] + x_shapes,
        scratch_shapes=x_scratch,
        compiler_params=_params(("arbitrary",)),
    )(dx2, x1, xq, xk, xv, gx, slab_g, slab_g, wo_g, *plan_args)


def _kv_bwd(mem, g, dxk, dxv, slab_g):
    m, d = mem.shape
    rows = d // N_CHIPS

    def body(mem_ref, g_ref, dxk_ref, dxv_ref, wk_ref, wv_ref, dwk_ref, dwv_ref, st_ref):
        gv = g_ref[...]
        hm, n, _ = _rms_fwd(mem_ref[...], gv)
        hb = hm.astype(BF16)
        dkb = dxk_ref[...].astype(BF16)
        dvb = dxv_ref[...].astype(BF16)
        dhm = []
        for j in range(N_CHIPS):
            hj = hb[:, j * rows:(j + 1) * rows]
            dwk_ref[j] = _dot(hj, dkb, TN)
            dwv_ref[j] = _dot(hj, dvb, TN)
            dhm.append(_dot(dkb, wk_ref[j], NT) + _dot(dvb, wv_ref[j], NT))
        dg = jnp.sum(jnp.concatenate(dhm, axis=1) * n, axis=0, keepdims=True)
        st_ref[...] = jnp.concatenate([dg, jnp.zeros((7, d), F32)], axis=0)

    blk = lambda k: pl.BlockSpec((N_CHIPS, rows, d), lambda i: (0, k, 0))
    return pl.pallas_call(
        body, name="kv_bwd", grid=(1,),
        in_specs=[_const((m, d)), _const((1, d)), _const((m, d)), _const((m, d)), blk(2), blk(3)],
        out_specs=[_const_out((N_CHIPS, rows, d)), _const_out((N_CHIPS, rows, d)), _const_out((8, d))],
        out_shape=[jax.ShapeDtypeStruct((N_CHIPS, rows, d), F32)] * 2 + [jax.ShapeDtypeStruct((8, d), F32)],
        compiler_params=_params(("arbitrary",)),
    )(mem, g, dxk, dxv, slab_g, slab_g)


def _pool_bwd(z, dmix, wp, scale, tm, plan=None, plan_args=()):
    s = z.shape[0]
    ng = len(POOL_WINDOWS)
    pw = ng * HEAD_DIM
    nb = tm // POOL_HALO
    nt = s // tm
    n_ext = tm + POOL_HALO

    def body(p_ref, prev_ref, dm_ref, dmn_ref, wp_ref, sc_ref, dp_ref, dwp_ref, st_ref):
        t = pl.program_id(0)
        p = p_ref[...]
        prev = jnp.where(t > 0, prev_ref[...], 0.0)
        pooled = _pooled(p, jnp.concatenate([prev, p], axis=0), t * tm)
        dm = dm_ref[...]
        dme = jnp.concatenate([dm, jnp.where(t < nt - 1, dmn_ref[...], 0.0)], axis=0) * sc_ref[...]
        tok = t * tm + lax.broadcasted_iota(jnp.int32, (n_ext, 1), 0)
        dsc, dps, dwps = [], [], []
        for g, w in enumerate(POOL_WINDOWS):
            cs = slice(g * HEAD_DIM, (g + 1) * HEAD_DIM)
            wpb = wp_ref[g].astype(BF16)
            pb = pooled[g].astype(BF16)
            dsc.append(jnp.sum(dm[:, cs] * _dot(pb, wpb), axis=0, keepdims=True))
            dye = dme[:, cs].astype(BF16)
            dwps.append(_dot(pb, dye[:tm], TN))
            dpe = _dot(dye, wpb, NT)
            acc = dpe / jnp.minimum(tok + 1, w).astype(F32)
            sh = 1
            while sh < w:
                acc = acc + pltpu.roll(acc, n_ext - sh, axis=0)
                sh *= 2
            dps.append(acc[:tm] - dpe[:tm])
        dp_ref[...] = jnp.concatenate(dps, axis=1)
        dsc_row = jnp.concatenate(dsc, axis=1)

        @pl.when(t == 0)
        def _():
            for g in range(ng):
                dwp_ref[g] = dwps[g]

        @pl.when(t > 0)
        def _():
            for g in range(ng):
                dwp_ref[g] = dwp_ref[g] + dwps[g]

        _acc_rows(st_ref, t, [dsc_row])

    x_specs, x_shapes, x_scratch = _plan_extras(plan)
    return pl.pallas_call(
        _fuse_exchange(body, 6, 3, 0, plan, 1), name="pool_bwd", grid=(nt,),
        in_specs=[pl.BlockSpec((tm, pw), lambda t: (t, 4)),
                  pl.BlockSpec((POOL_HALO, pw), lambda t: (jnp.maximum(t * nb - 1, 0), 4)),
                  pl.BlockSpec((tm, pw), lambda t: (t, 1)),
                  pl.BlockSpec((POOL_HALO, pw), lambda t: (jnp.minimum((t + 1) * nb, s // POOL_HALO - 1), 1)),
                  _const(wp.shape), _const((1, pw))] + x_specs,
        out_specs=[pl.BlockSpec((tm, pw), lambda t: (t, 0)), _const_out(wp.shape), _const_out((8, pw))] + x_specs,
        out_shape=[jax.ShapeDtypeStruct((s, pw), F32), jax.ShapeDtypeStruct(wp.shape, F32),
                   jax.ShapeDtypeStruct((8, pw), F32)] + x_shapes,
        scratch_shapes=x_scratch,
        compiler_params=_params(("arbitrary",)),
    )(z, z, dmix, dmix, wp, scale, *plan_args)


def _hgrn_bwd(z, o, dmix, st, l0, l1, gn, tc, unroll=1, plan=None, plan_args=()):
    s = z.shape[0]
    nsub = tc // SUB
    nt = s // tc
    hd = HEAD_DIM

    def body(q_ref, f_ref, v_ref, g_ref, l0_ref, l1_ref, gn_ref, o_ref, dm_ref, st_ref, tril_ref, triu_ref,
             dq_ref, df_ref, di_ref, dg_ref, stat_ref, dstate, qs, ks, bs, dos, dqs, dks, dbs):
        t = pl.program_id(1)

        @pl.when(t == 0)
        def _():
            dstate[...] = jnp.zeros_like(dstate)

        lb = _lower_bound(l0_ref[...], l1_ref[...])
        qp = q_ref[...]
        q, sq, f, sf = _hgrn_gates(qp, f_ref[...], lb)
        qs[...] = q
        ks[...] = 1.0 - f
        bs[...] = _group_cumsum(tril_ref[...], jnp.log(f) * LOG2E)

        o = o_ref[...]
        r = lax.rsqrt(jnp.mean(o * o, axis=-1, keepdims=True) + EPS)
        n = o * r
        gnv = gn_ref[...]
        gp = g_ref[...]
        sg = _sigmoid(gp)
        dm = dm_ref[...]
        dg_ref[...] = dm * (n * gnv) * (sg * (1.0 + gp * (1.0 - sg)))
        don = dm * (gp * sg)
        dgn = jnp.sum(don * n, axis=0, keepdims=True)
        dn = don * gnv
        dos[...] = r * (dn - n * jnp.mean(dn * n, axis=-1, keepdims=True))
        rows = lax.broadcasted_iota(jnp.int32, (HALF, 1), 0)

        def step(i, carry):
            ii = nsub - 1 - i
            r0 = pl.multiple_of(ii * SUB, SUB)
            q_ = qs[pl.ds(r0, SUB), :]
            k_ = ks[pl.ds(r0, SUB), :]
            b_ = bs[pl.ds(r0, SUB), :]
            v_ = v_ref[pl.ds(r0, SUB), :]
            do_ = dos[pl.ds(r0, SUB), :]
            stp = st_ref[ii]
            dst = dstate[...]
            bl = b_[SUB - 1:SUB, :]
            eb = jnp.exp2(b_)
            ekl = jnp.exp2(bl - b_)
            ebl = jnp.exp2(bl)
            dob = do_.astype(BF16)
            dstb = dst.astype(BF16)
            kt = k_ * ekl
            dq = _dot(dob, stp.astype(BF16)) * eb
            dkt = _dot(v_.astype(BF16), dstb)
            dk = dkt * ekl
            dv = _dot(kt.astype(BF16), dstb, NT)
            extra = jnp.sum(kt * dkt, axis=0, keepdims=True) + ebl * jnp.sum(stp * dst, axis=0, keepdims=True)
            halves = lambda a: [a[:HALF], a[HALF:]]
            q_h, b_h, do_h, dq_h, dk_h, dv_h = (halves(a) for a in (q_, b_, do_, dq, dk, dv))
            for own in range(2):
                dk_rows, dv_rows = _RowSums(rows), _RowSums(rows)
                for jj in _RowSums.ORDER:
                    j = own * HALF + jj
                    bj, kj, vj = b_[j:j + 1, :], k_[j:j + 1, :], v_[j:j + 1, :]
                    dk_sum = dv_sum = None
                    for h in range(own, 2):
                        e = _decay(b_h[h], bj, rows, jj if h == own else None)
                        pe = q_h[h] * e
                        acol = jnp.sum(pe * kj, axis=-1, keepdims=True)
                        dacol = jnp.sum(do_h[h] * vj, axis=-1, keepdims=True)
                        dq_h[h] = dq_h[h] + dacol * (e * kj)
                        dk_sum = dacol * pe if dk_sum is None else dk_sum + dacol * pe
                        dv_sum = acol * do_h[h] if dv_sum is None else dv_sum + acol * do_h[h]
                    dk_rows.push(jj, dk_sum)
                    dv_rows.push(jj, dv_sum)
                dk_h[own] = dk_h[own] + dk_rows.result()
                dv_h[own] = dv_h[own] + dv_rows.result()
            last = jnp.where(rows == HALF - 1, extra, 0.0)
            for h in range(2):
                rh = r0 + h * HALF
                dqs[pl.ds(rh, HALF), :] = dq_h[h]
                dks[pl.ds(rh, HALF), :] = dk_h[h]
                di_ref[pl.ds(rh, HALF), :] = dv_h[h]
                db = q_h[h] * dq_h[h] - halves(k_)[h] * dk_h[h]
                dbs[pl.ds(rh, HALF), :] = db + last if h == 1 else db
            dstate[...] = dst * ebl + _dot(dob, (q_ * eb).astype(BF16), TN)
            return carry

        lax.fori_loop(0, nsub, step, 0, unroll=unroll)
        dlf = _group_cumsum(triu_ref[...], dbs[...])
        dfv = dlf / f - dks[...]
        df_ref[...] = dfv * (1.0 - lb) * sf * (1.0 - sf)
        dlb = jnp.sum(dfv * (1.0 - sf), axis=0, keepdims=True)
        dq_ref[...] = dqs[...] * (sq * (1.0 + qp * (1.0 - sq)))
        _acc_rows(stat_ref, t, [dgn, dlb])

    rev = lambda t: nt - 1 - t
    col = lambda k: pl.BlockSpec((tc, hd), lambda h, t: (rev(t), k * HGRN_HEADS + h))
    vec = pl.BlockSpec((None, 1, hd), lambda h, t: (h, 0, 0))
    head = pl.BlockSpec((tc, hd), lambda h, t: (rev(t), h))
    x_specs, x_shapes, x_scratch = _plan_extras(plan)
    return pl.pallas_call(
        _fuse_exchange(body, 12, 5, 8, plan, 2), name="hgrn_bwd", grid=(HGRN_HEADS, nt),
        in_specs=[col(0), col(1), col(2), col(3), vec, vec, vec, head, head,
                  pl.BlockSpec((None, nsub, hd, hd), lambda h, t: (h, rev(t), 0, 0)),
                  _const((tc, tc)), _const((tc, tc))] + x_specs,
        out_specs=[head, head, head, head, pl.BlockSpec((None, 8, hd), lambda h, t: (h, 0, 0))] + x_specs,
        out_shape=[jax.ShapeDtypeStruct((s, HGRN_WIDTH), F32)] * 4 + [jax.ShapeDtypeStruct((HGRN_HEADS, 8, hd), F32)]
                  + x_shapes,
        scratch_shapes=[pltpu.VMEM((hd, hd), F32)] + [pltpu.VMEM((tc, hd), F32)] * 7 + x_scratch,
        compiler_params=_params(("arbitrary", "arbitrary")),
    )(z, z, z, z, l0, l1, gn, o, dmix, st, _block_tri(tc, False), _block_tri(tc, True), *plan_args)


def _in_bwd(dparts, dx1, x, g, win_g, tm):
    s, d = x.shape
    nsh, _, wc = win_g.shape
    pw = dparts[0].shape[1]

    def body(dq_ref, df_ref, di_ref, dg_ref, dp_ref, dx1_ref, x_ref, g_ref, w_ref, gx_ref, dz_ref, st_ref):
        dz = jnp.concatenate([dq_ref[...], df_ref[...], di_ref[...], dg_ref[...], dp_ref[...]], axis=1).astype(BF16)
        dz_ref[...] = dz
        dh = jnp.zeros((tm, d), F32)
        for j in range(nsh):
            dh = dh + _dot(dz[:, j * wc:(j + 1) * wc], w_ref[j], NT)
        gv = g_ref[...]
        _, n, r = _rms_fwd(x_ref[...], gv)
        dxn, dg = _rms_bwd(dh, n, r, gv)
        gx_ref[...] = dx1_ref[...] + dxn
        _acc_rows(st_ref, pl.program_id(0), [dg])

    tile = lambda w: pl.BlockSpec((tm, w), lambda t: (t, 0))
    return pl.pallas_call(
        body, name="in_bwd", grid=(s // tm,),
        in_specs=[tile(pw)] * 5 + [tile(d), tile(d), _const((1, d)), _const(win_g.shape)],
        out_specs=[tile(d), tile(nsh * wc), _const_out((8, d))],
        out_shape=[jax.ShapeDtypeStruct((s, d), F32), jax.ShapeDtypeStruct((s, nsh * wc), BF16),
                   jax.ShapeDtypeStruct((8, d), F32)],
        compiler_params=_params(("arbitrary",)),
    )(*dparts, dx1, x, g, win_g)


def _tn_grad(name, a, b, out_rows, out_cols, a_sharded, tr, tc):
    s = a.shape[0]
    nr, nc = out_rows // tr, out_cols // tc

    def body(a_ref, b_ref, o_ref):
        o_ref[...] = _dot(a_ref[...], b_ref[...], TN)

    a_map = (lambda j, i, k: (0, j * nr + i)) if a_sharded else (lambda j, i, k: (0, i))
    b_map = (lambda j, i, k: (0, k)) if a_sharded else (lambda j, i, k: (0, j * nc + k))
    return pl.pallas_call(
        body, name=name, grid=(N_CHIPS, nr, nc),
        in_specs=[pl.BlockSpec((s, tr), a_map), pl.BlockSpec((s, tc), b_map)],
        out_specs=pl.BlockSpec((None, tr, tc), lambda j, i, k: (j, i, k)),
        out_shape=jax.ShapeDtypeStruct((N_CHIPS, out_rows, out_cols), F32),
        compiler_params=_params(("parallel", "parallel", "parallel")),
    )(a, b)


FFN_NAMES = ("w_ff1", "w_ff2")
ATTN_NAMES = ("w_xo", "w_xq", "w_out", "w_xk", "w_xv")
EARLY_NAMES = FFN_NAMES + ATTN_NAMES
BIG_NAMES = EARLY_NAMES + ("w_in",)


def _halved(g):
    return g.reshape(N_CHIPS, 2, g.shape[1] // 2, g.shape[2])


def _pair_adds(names, gs, got, idx):
    pairs = [_grad_pair_add("grad_pair_add_" + k, g, r, idx, tr=min(256, g.shape[2])) for k, g, r in zip(names, gs, got)]
    return [p[0] for p in pairs], [p[1] for p in pairs]


def _step(x, mem, target, small, shards, idx):
    d = x.shape[1]
    l0 = small["lb_logits"][0].reshape(HGRN_HEADS, 1, HEAD_DIM)
    l1 = small["lb_logits"][1].reshape(HGRN_HEADS, 1, HEAD_DIM)
    gn = small["hgrn_norm_g"].reshape(HGRN_HEADS, 1, HEAD_DIM)
    wp = small["w_pool"].reshape(len(POOL_WINDOWS), HEAD_DIM, HEAD_DIM)
    psc = small["pool_scale"].reshape(1, -1)
    gmix, gx, gmem, gffn = (small[k].reshape(1, d) for k in ("norm_mix_g", "norm_x_g", "norm_mem_g", "norm_ffn_g"))
    gfin = small["final_norm_g"].reshape(1, d)

    (win_g,) = _run_exchange("gather_w_in", _WeightGather([shards["w_in"]]), [shards["w_in"]])
    z, h = _in_proj(x, gmix, win_g, tm=512)
    rest = [shards["slab"], shards["w_xo"]]
    o, oa, st, slab_g, wo_g = _hgrn_fwd(z, l0, l1, gn, tc=256, unroll=8, plan=_WeightGather(rest), plan_args=rest)
    ob = _pool_fwd(z, wp, psc, tm=512)
    xk, xv = _kv_proj(mem, gmem, slab_g)
    x1, mixed, hq, xq, att, x2 = _mix_xattn_fwd(x, oa, ob, gx, slab_g, wo_g, xk, xv, tm=512)
    a, hf, dx3, dx3b, st_loss = _mlp_loss_fwd(x2, gffn, gfin, slab_g, target, tm=512)

    da, u, dx2, dx2b, st_ffn = _mlp_bwd(dx3, dx3b, a, x2, gffn, slab_g, tm=256)
    dw = {}
    dw["w_ff1"] = _tn_grad("dw_ff1", hf, da, d, d, False, 1024, 1024)
    dw["w_ff2"] = _tn_grad("dw_ff2", u, dx3b, d, d, True, 1024, 1024)
    gs_ffn = [_halved(dw[k]) for k in FFN_NAMES]
    dx1, dx1b, dxq, dmix, dxk, dxv, st_x, *got_ffn = _xattn_mix_bwd(
        dx2, x1, xq, xk, xv, gx, slab_g, wo_g, tm=512, plan=_PairExchange(gs_ffn), plan_args=gs_ffn)
    dw["w_xo"] = _tn_grad("dw_xo", att, dx2b, d, d // N_CHIPS, False, 1024, 256)
    dw["w_xq"] = _tn_grad("dw_xq", hq, dxq, d // N_CHIPS, d, True, 256, 1024)
    dw["w_out"] = _tn_grad("dw_out", mixed, dx1b, d // N_CHIPS, d, True, 256, 1024)
    dw["w_xk"], dw["w_xv"], st_mem = _kv_bwd(mem, gmem, dxk, dxv, slab_g)
    gs_attn = [_halved(dw[k]) for k in ATTN_NAMES]
    dp, d_wp, st_pool, *got_attn = _pool_bwd(z, dmix, wp, psc, tm=512, plan=_PairExchange(gs_attn), plan_args=gs_attn)
    keeps, sends = _pair_adds(EARLY_NAMES, gs_ffn + gs_attn, got_ffn + got_attn, idx)

    dq, df, di, dg, st_hgrn, *received = _hgrn_bwd(z, o, dmix, st, l0, l1, gn, tc=256, unroll=4,
                                                    plan=_ChipExchange(sends), plan_args=sends)
    grad_x, dz, st_mix = _in_bwd([dq, df, di, dg, dp], dx1, x, gmix, win_g, tm=512)
    gs_in = [_halved(_tn_grad("dw_in", h, dz, d, win_g.shape[2], False, 1024, win_g.shape[2]))]
    got_in = _run_exchange("grad_pair_exchange_w_in", _PairExchange(gs_in), gs_in)
    keep_in, send_in = _pair_adds(("w_in",), gs_in, got_in, idx)
    recv_in = _run_exchange("grad_chip_exchange_w_in", _ChipExchange(send_in), send_in)

    partials = dict(zip(BIG_NAMES, zip(keeps + keep_in, list(received) + list(recv_in))))
    stats = dict(mix=st_mix, x=st_x, mem=st_mem, ffn=st_ffn, loss=st_loss, hgrn=st_hgrn, pool=st_pool)
    return grad_x, stats, d_wp, partials


def _place():
    x, y, c = lax.axis_index("x"), lax.axis_index("y"), lax.axis_index("c")
    return x, y, c, [(x, 1 - y), (1 - x, y), (1 - x, 1 - y)]


def _rcopy(src, dst, ssem, rsem, dev):
    return pltpu.make_async_remote_copy(src_ref=src, dst_ref=dst, send_sem=ssem, recv_sem=rsem,
                                        device_id=dev, device_id_type=MESH)


class _WeightGather:
    def __init__(self, shards):
        self.n = len(shards)
        self.rows = [w.shape[0] for w in shards]
        self.out_shape = [jax.ShapeDtypeStruct((N_CHIPS,) + w.shape, w.dtype) for w in shards]
        self.scratch_shapes = [pltpu.SemaphoreType.DMA((self.n,))] * 2 + [pltpu.SemaphoreType.DMA((self.n, 3))] * 4

    def _copies(self, ins, outs, sems, with_pass_on):
        lsem, lrsem, ssem, rsem, fsem, frsem = sems
        x, y, c, peers = _place()
        chip = 2 * x + y
        sib = (x, y, 1 - c)
        own = [_rcopy(ins[a], outs[a].at[chip], lsem.at[a], lrsem.at[a], sib) for a in range(self.n)]
        sends, arrived, passed, passed_in = [], [], [], []
        for a in range(self.n):
            hr = self.rows[a] // 2
            half = lambda who, hc, a=a, hr=hr: outs[a].at[who, pl.ds(hc * hr, hr), :]
            for r, (px, py) in enumerate(peers):
                pc = 2 * px + py
                sends.append(_rcopy(ins[a].at[pl.ds(c * hr, hr), :], half(chip, c), ssem.at[a, r], rsem.at[a, r],
                                    (px, py, c)))
                if with_pass_on:
                    arrived.append(_rcopy(half(pc, c), half(pc, c), ssem.at[a, r], rsem.at[a, r], (px, py, c)))
                    passed.append(_rcopy(half(pc, c), half(pc, c), fsem.at[a, r], frsem.at[a, r], sib))
                    passed_in.append(_rcopy(half(pc, 1 - c), half(pc, 1 - c), fsem.at[a, r], frsem.at[a, r], sib))
        return own, sends, arrived, passed, passed_in

    def start(self, ins, outs, sems):
        own, sends, _, _, _ = self._copies(ins, outs, sems, False)
        for cp in own + sends:
            cp.start()

    def finish(self, ins, outs, sems):
        own, sends, arrived, passed, passed_in = self._copies(ins, outs, sems, True)
        for got, fwd in zip(arrived, passed):
            got.wait_recv()
            fwd.start()
        for cp in passed_in:
            cp.wait_recv()
        for cp in sends + passed:
            cp.wait_send()
        for cp in own:
            cp.wait()


class _ChipExchange:
    def __init__(self, sends):
        self.n = len(sends)
        self.out_shape = [jax.ShapeDtypeStruct(g.shape, g.dtype) for g in sends]
        self.scratch_shapes = [pltpu.SemaphoreType.DMA((self.n, 3))] * 2

    def _copies(self, ins, outs, sems):
        ssem, rsem = sems
        _, _, c, peers = _place()
        return [_rcopy(ins[a].at[r], outs[a].at[r], ssem.at[a, r], rsem.at[a, r], (px, py, c))
                for a in range(self.n) for r, (px, py) in enumerate(peers)]

    def start(self, ins, outs, sems):
        for cp in self._copies(ins, outs, sems):
            cp.start()

    def finish(self, ins, outs, sems):
        for cp in self._copies(ins, outs, sems):
            cp.wait()


def _run_exchange(name, plan, arrays):
    n = plan.n

    def body(*refs):
        ins, outs, sems = refs[:n], refs[n:2 * n], refs[2 * n:]
        plan.start(ins, outs, sems)
        plan.finish(ins, outs, sems)

    return pl.pallas_call(
        body, name=name, in_specs=[ANY] * n, out_specs=[ANY] * n,
        out_shape=plan.out_shape, scratch_shapes=plan.scratch_shapes,
    )(*arrays)


class _PairExchange:
    def __init__(self, gs):
        self.n = len(gs)
        self.out_shape = [jax.ShapeDtypeStruct((g.shape[0],) + g.shape[2:], g.dtype) for g in gs]
        self.scratch_shapes = [pltpu.SemaphoreType.DMA((self.n,))] * 2

    def _copies(self, ins, outs, sems):
        ssem, rsem = sems
        x, y, c, _ = _place()
        return [_rcopy(ins[a].at[:, 1 - c], outs[a], ssem.at[a], rsem.at[a], (x, y, 1 - c)) for a in range(self.n)]

    def start(self, ins, outs, sems):
        for cp in self._copies(ins, outs, sems):
            cp.start()

    def finish(self, ins, outs, sems):
        for cp in self._copies(ins, outs, sems):
            cp.wait()


def _grad_pair_add(name, g, got, idx, tr):
    _, _, hr, cc = g.shape

    def body(idx_ref, g0, g1, g2, g3, r0, r1, r2, r3, keep_ref, send_ref):
        keep_ref[...] = g0[...] + r0[...]
        for q, (gq, rq) in enumerate(((g1, r1), (g2, r2), (g3, r3))):
            send_ref[q] = (gq[...] + rq[...]).astype(BF16)

    gspec = lambda q: pl.BlockSpec((None, None, tr, cc), lambda i, idx: (idx[1 + q], idx[0], i, 0))
    rspec = lambda q: pl.BlockSpec((None, tr, cc), lambda i, idx: (idx[1 + q], i, 0))
    return pl.pallas_call(
        body, name=name,
        grid_spec=pltpu.PrefetchScalarGridSpec(
            num_scalar_prefetch=1, grid=(hr // tr,),
            in_specs=[gspec(q) for q in range(4)] + [rspec(q) for q in range(4)],
            out_specs=[pl.BlockSpec((tr, cc), lambda i, idx: (i, 0)), pl.BlockSpec((3, tr, cc), lambda i, idx: (0, i, 0))]),
        out_shape=[jax.ShapeDtypeStruct((hr, cc), F32), jax.ShapeDtypeStruct((3, hr, cc), BF16)],
        compiler_params=_params(("parallel",)),
    )(idx, g, g, g, g, got, got, got, got)


def _grad_chip_add(name, keep, got, tr):
    hr, cc = keep.shape

    def body(k_ref, g_ref, o_ref):
        o_ref[...] = ((k_ref[...] + g_ref[0].astype(F32)) + g_ref[1].astype(F32)) + g_ref[2].astype(F32)

    return pl.pallas_call(
        body, name=name, grid=(hr // tr,),
        in_specs=[pl.BlockSpec((tr, cc), lambda i: (i, 0)), pl.BlockSpec((3, tr, cc), lambda i: (0, i, 0))],
        out_specs=pl.BlockSpec((tr, cc), lambda i: (i, 0)),
        out_shape=jax.ShapeDtypeStruct((hr, cc), F32),
        compiler_params=_params(("parallel",)),
    )(keep, got)


def _grad_half_exchange(ts):
    n = len(ts)

    def body(*refs):
        ins, outs, ssem, rsem = refs[:n], refs[n:2 * n], refs[2 * n], refs[2 * n + 1]
        x, y, c, _ = _place()
        cps = [_rcopy(ins[a], outs[a], ssem.at[a], rsem.at[a], (x, y, 1 - c)) for a in range(n)]
        for cp in cps:
            cp.start()
        for cp in cps:
            cp.wait()

    return pl.pallas_call(
        body, name="grad_half_exchange",
        in_specs=[ANY] * n, out_specs=[ANY] * n,
        out_shape=[jax.ShapeDtypeStruct(t.shape, t.dtype) for t in ts],
        scratch_shapes=[pltpu.SemaphoreType.DMA((n,))] * 2,
    )(*ts)


def _small_allreduce(stats, d_wp):
    d = D_MODEL
    half = d // 2
    wps = d_wp.shape

    def body(mix_ref, x_ref, mem_ref, ffn_ref, loss_ref, hg_ref, pool_ref, wp_ref, slab_out, wp_out,
             slab_buf, wp_buf, sib_s, sib_w, ssem, rsem):
        x, y, c, peers = _place()
        chip = 2 * x + y
        sib = (x, y, 1 - c)
        hgn = jnp.concatenate([hg_ref[h, 0:1, :] for h in range(HGRN_HEADS)], axis=1)
        dlb = jnp.concatenate([hg_ref[h, 1:2, :] for h in range(HGRN_HEADS)], axis=1)
        slab_buf[0] = jnp.concatenate([
            mix_ref[0:1, :], x_ref[0:1, :], mem_ref[0:1, :], ffn_ref[0:1, :], loss_ref[0:1, :],
            jnp.concatenate([dlb, hgn], axis=1),
            jnp.concatenate([pool_ref[0:1, :], jnp.zeros((1, half), F32)], axis=1),
            loss_ref[1:2, :]], axis=0)
        wp_buf[0] = wp_ref[...]
        pair = [_rcopy(slab_buf.at[0], sib_s, ssem.at[0], rsem.at[0], sib),
                _rcopy(wp_buf.at[0], sib_w, ssem.at[1], rsem.at[1], sib)]
        for cp in pair:
            cp.start()
        for cp in pair:
            cp.wait()
        slab_buf[0] = slab_buf[0] + sib_s[...]
        wp_buf[0] = wp_buf[0] + sib_w[...]
        cps = []
        for r, (px, py) in enumerate(peers):
            cps.append(_rcopy(slab_buf.at[0], slab_buf.at[r + 1], ssem.at[2 + 2 * r], rsem.at[2 + 2 * r], (px, py, c)))
            cps.append(_rcopy(wp_buf.at[0], wp_buf.at[r + 1], ssem.at[3 + 2 * r], rsem.at[3 + 2 * r], (px, py, c)))
        for cp in cps:
            cp.start()
        for cp in cps:
            cp.wait()
        tot_s, tot_w = slab_buf[chip], wp_buf[chip]
        for j in range(1, N_CHIPS):
            tot_s = tot_s + slab_buf[jnp.bitwise_xor(j, chip)]
            tot_w = tot_w + wp_buf[jnp.bitwise_xor(j, chip)]
        slab_out[...] = tot_s
        wp_out[...] = tot_w

    return pl.pallas_call(
        body, name="small_allreduce",
        in_specs=[VMEM] * 8, out_specs=[VMEM] * 2,
        out_shape=[jax.ShapeDtypeStruct((8, d), F32), jax.ShapeDtypeStruct(wps, F32)],
        scratch_shapes=[pltpu.VMEM((N_CHIPS, 8, d), F32), pltpu.VMEM((N_CHIPS,) + wps, F32),
                        pltpu.VMEM((8, d), F32), pltpu.VMEM(wps, F32),
                        pltpu.SemaphoreType.DMA((8,)), pltpu.SemaphoreType.DMA((8,))],
    )(stats["mix"], stats["x"], stats["mem"], stats["ffn"], stats["loss"], stats["hgrn"], stats["pool"], d_wp)


def _adamw_math(w, g, m, v):
    m = ADAM_B1 * m + (1.0 - ADAM_B1) * g
    v = ADAM_B2 * v + (1.0 - ADAM_B2) * (g * g)
    m_hat = m / (1.0 - ADAM_B1 ** ADAM_STEP)
    v_hat = v / (1.0 - ADAM_B2 ** ADAM_STEP)
    delta = -ADAM_LR * (m_hat / (jnp.sqrt(v_hat) + ADAM_EPS) + ADAM_WD * w)
    return delta, m, v


def _adamw(name, mine, theirs, w, m, v, idx, tr):
    rows = w.shape[0]
    cc = mine.shape[1]
    nb = rows // 2 // tr
    heads = w.shape[1] if w.ndim == 3 else 1
    e = cc // heads

    def body(idx_ref, a_ref, b_ref, w_ref, m_ref, v_ref, g_out, d_out, m_out, v_out):
        g = jnp.where(pl.program_id(0) // nb == idx_ref[0], a_ref[...], b_ref[...])
        if w.ndim == 2:
            g_out[...] = g
            d_out[...], m_out[...], v_out[...] = _adamw_math(w_ref[...], g, m_ref[...], v_ref[...])
        else:
            for h in range(heads):
                gh = g[:, h * e:(h + 1) * e]
                g_out[:, h, :] = gh
                d_out[:, h, :], m_out[:, h, :], v_out[:, h, :] = _adamw_math(
                    w_ref[:, h, :], gh, m_ref[:, h, :], v_ref[:, h, :])

    hspec = pl.BlockSpec((tr, cc), lambda i, idx: (i % nb, 0))
    spec = pl.BlockSpec((tr,) + w.shape[1:], lambda i, idx: (i,) + (0,) * (w.ndim - 1))
    return pl.pallas_call(
        body, name=name,
        grid_spec=pltpu.PrefetchScalarGridSpec(
            num_scalar_prefetch=1, grid=(rows // tr,),
            in_specs=[hspec, hspec, spec, spec, spec], out_specs=[spec] * 4),
        out_shape=[jax.ShapeDtypeStruct(w.shape, F32)] * 4,
        compiler_params=_params(("parallel",)),
    )(idx, mine, theirs, w, m, v)


SMALL_NAMES = ("norm_mix_g", "lb_logits", "hgrn_norm_g", "w_pool", "pool_scale", "norm_x_g", "norm_mem_g",
               "norm_ffn_g", "final_norm_g")


def _small_update(slab, d_wp, ws, ms, vs):
    n = len(SMALL_NAMES)
    half = D_MODEL // 2

    def body(slab_ref, wp_ref, *refs):
        w_refs, m_refs, v_refs, outs = refs[:n], refs[n:2 * n], refs[2 * n:3 * n], refs[3 * n:]
        row = lambda k: slab_ref[k:k + 1, :]
        lbl = w_refs[SMALL_NAMES.index("lb_logits")][...]
        s0 = _lower_bound(lbl[0:1, :], lbl[1:2, :])
        dl0 = row(ROW_LB_HGN)[:, :half] * s0 * (1.0 - s0)
        grads = dict(norm_mix_g=row(ROW_GMIX), lb_logits=jnp.concatenate([dl0, -dl0], axis=0),
                     hgrn_norm_g=row(ROW_LB_HGN)[:, half:], w_pool=wp_ref[...], pool_scale=row(ROW_PSCALE)[:, :half],
                     norm_x_g=row(ROW_GX), norm_mem_g=row(ROW_GMEM), norm_ffn_g=row(ROW_GFFN),
                     final_norm_g=row(ROW_GFIN))
        outs[0][...] = row(ROW_LOSS)[:, :128]
        for i, name in enumerate(SMALL_NAMES):
            g = grads[name]
            delta, m2, v2 = _adamw_math(w_refs[i][...], g, m_refs[i][...], v_refs[i][...])
            for o, val in zip(outs[1 + 4 * i:5 + 4 * i], (g, delta, m2, v2)):
                o[...] = val

    args = [ws[k] for k in SMALL_NAMES] + [ms[k] for k in SMALL_NAMES] + [vs[k] for k in SMALL_NAMES]
    out_shape = [jax.ShapeDtypeStruct((1, 128), F32)]
    for k in SMALL_NAMES:
        out_shape += [jax.ShapeDtypeStruct(ws[k].shape, F32)] * 4
    res = pl.pallas_call(
        body, name="small_update",
        in_specs=[VMEM] * (2 + 3 * n), out_specs=[VMEM] * len(out_shape), out_shape=out_shape,
    )(slab, d_wp, *args)
    return res[0], {k: res[1 + 4 * i:5 + 4 * i] for i, k in enumerate(SMALL_NAMES)}


ALL_NAMES = ("norm_mix_g", "w_in", "lb_logits", "hgrn_norm_g", "w_pool", "pool_scale", "w_out", "norm_x_g",
             "norm_mem_g", "w_xq", "w_xk", "w_xv", "w_xo", "norm_ffn_g", "w_ff1", "w_ff2", "final_norm_g")


def _shard_2d(name, a):
    a = a[0]
    if name in ("w_xq", "w_xk", "w_xv"):
        return a.reshape(a.shape[0], -1)
    if name == "w_xo":
        return a.reshape(-1, a.shape[-1])
    return a


def _small_2d(name, a):
    if name == "w_pool":
        return a.reshape(-1, HEAD_DIM)
    if name == "lb_logits":
        return a
    return a.reshape(1, -1)


def kernel(x, mem, norm_mix_g, w_in, lb_logits, hgrn_norm_g, w_pool, pool_scale, w_out, norm_x_g, norm_mem_g, w_xq, w_xk, w_xv, w_xo, norm_ffn_g, w_ff1, w_ff2, final_norm_g, loss_target, m_norm_mix_g, m_w_in, m_lb_logits, m_hgrn_norm_g, m_w_pool, m_pool_scale, m_w_out, m_norm_x_g, m_norm_mem_g, m_w_xq, m_w_xk, m_w_xv, m_w_xo, m_norm_ffn_g, m_w_ff1, m_w_ff2, m_final_norm_g, v_norm_mix_g, v_w_in, v_lb_logits, v_hgrn_norm_g, v_w_pool, v_pool_scale, v_w_out, v_norm_x_g, v_norm_mem_g, v_w_xq, v_w_xk, v_w_xv, v_w_xo, v_norm_ffn_g, v_w_ff1, v_w_ff2, v_final_norm_g):
    w = dict(norm_mix_g=norm_mix_g, w_in=w_in, lb_logits=lb_logits, hgrn_norm_g=hgrn_norm_g, w_pool=w_pool, pool_scale=pool_scale, w_out=w_out, norm_x_g=norm_x_g, norm_mem_g=norm_mem_g, w_xq=w_xq, w_xk=w_xk, w_xv=w_xv, w_xo=w_xo, norm_ffn_g=norm_ffn_g, w_ff1=w_ff1, w_ff2=w_ff2, final_norm_g=final_norm_g)
    m = dict(norm_mix_g=m_norm_mix_g, w_in=m_w_in, lb_logits=m_lb_logits, hgrn_norm_g=m_hgrn_norm_g, w_pool=m_w_pool, pool_scale=m_pool_scale, w_out=m_w_out, norm_x_g=m_norm_x_g, norm_mem_g=m_norm_mem_g, w_xq=m_w_xq, w_xk=m_w_xk, w_xv=m_w_xv, w_xo=m_w_xo, norm_ffn_g=m_norm_ffn_g, w_ff1=m_w_ff1, w_ff2=m_w_ff2, final_norm_g=m_final_norm_g)
    v = dict(norm_mix_g=v_norm_mix_g, w_in=v_w_in, lb_logits=v_lb_logits, hgrn_norm_g=v_hgrn_norm_g, w_pool=v_w_pool, pool_scale=v_pool_scale, w_out=v_w_out, norm_x_g=v_norm_x_g, norm_mem_g=v_norm_mem_g, w_xq=v_w_xq, w_xk=v_w_xk, w_xv=v_w_xv, w_xo=v_w_xo, norm_ffn_g=v_norm_ffn_g, w_ff1=v_w_ff1, w_ff2=v_w_ff2, final_norm_g=v_final_norm_g)

    big_w = {k: _shard_2d(k, w[k]) for k in BIG_NAMES}
    slab = jnp.concatenate([big_w[k] for k in ("w_out", "w_xq", "w_xk", "w_xv", "w_ff1", "w_ff2")], axis=0).astype(BF16)
    shards = dict(slab=slab, w_in=big_w["w_in"].astype(BF16), w_xo=big_w["w_xo"].astype(BF16))

    cx, cy, cc = lax.axis_index("x"), lax.axis_index("y"), lax.axis_index("c")
    chip = 2 * cx + cy
    idx = jnp.stack([cc, chip, chip ^ 1, chip ^ 2, chip ^ 3]).astype(jnp.int32)
    small = {k: w[k] for k in SMALL_NAMES}
    grad_x, stats, d_wp, partials = _step(x[0], mem[0], loss_target[0], small, shards, idx)

    halves = [_grad_chip_add("grad_chip_add_" + k, *partials[k], tr=min(256, partials[k][0].shape[0]))
              for k in BIG_NAMES]
    theirs = _grad_half_exchange(halves)

    grads, deltas, new_m, new_v = {}, {}, {}, {}
    for k, mine, other in zip(BIG_NAMES, halves, theirs):
        as_held = (lambda a: a[0]) if k in ("w_xq", "w_xk", "w_xv") else functools.partial(_shard_2d, k)
        res = _adamw("adamw_" + k, mine, other, as_held(w[k]), as_held(m[k]), as_held(v[k]), idx,
                     tr=min(256, mine.shape[0]))
        for store, val in zip((grads, deltas, new_m, new_v), res):
            store[k] = val.reshape(w[k].shape)

    slab_sum, wp_sum = _small_allreduce(stats, d_wp.reshape(-1, HEAD_DIM))
    loss, upd = _small_update(slab_sum, wp_sum, {k: _small_2d(k, w[k]) for k in SMALL_NAMES},
                              {k: _small_2d(k, m[k]) for k in SMALL_NAMES}, {k: _small_2d(k, v[k]) for k in SMALL_NAMES})
    for k in SMALL_NAMES:
        for store, val in zip((grads, deltas, new_m, new_v), upd[k]):
            store[k] = val.reshape(w[k].shape)

    return (loss[0, 0], grad_x[None], *[grads[k] for k in ALL_NAMES], *[deltas[k] for k in ALL_NAMES],
            *[new_m[k] for k in ALL_NAMES], *[new_v[k] for k in ALL_NAMES])
```

```python
import functools

import jax
import jax.numpy as jnp
from jax import lax
from jax.experimental import pallas as pl
from jax.experimental.pallas import tpu as pltpu

F32 = jnp.float32
BF16 = jnp.bfloat16
LOG2E = 1.4426950408889634
NEG_BIG = -1e30
MAX_LOG2_GROWTH = 100.0
MESH = pl.DeviceIdType.MESH
ANY = pl.BlockSpec(memory_space=pl.ANY)
VMEM = pl.BlockSpec(memory_space=pltpu.VMEM)

D_MODEL = 1024
N_CHIPS = 4
HGRN_HEADS = 4
HEAD_DIM = 128
HGRN_WIDTH = HGRN_HEADS * HEAD_DIM
POOL_WINDOWS = (2, 4, 8, 16)
POOL_HALO = 16
SUB = 16
HALF = SUB // 2
CHUNK = 64
XATTN_HEADS = 4
XATTN_HEAD_DIM = 256
EPS = 1e-6
ADAM_LR, ADAM_B1, ADAM_B2, ADAM_EPS, ADAM_WD, ADAM_STEP = 0.001, 0.9, 0.999, 1e-08, 0.01, 10

V7X_VMEM_BYTES = 64 * 1024 * 1024
VMEM_LIMIT = V7X_VMEM_BYTES - 8 * 1024 * 1024

NN = (((1,), (0,)), ((), ()))
NT = (((1,), (1,)), ((), ()))
TN = (((0,), (0,)), ((), ()))

ROW_GMIX, ROW_GX, ROW_GMEM, ROW_GFFN, ROW_GFIN, ROW_LB_HGN, ROW_PSCALE, ROW_LOSS = range(8)


def _dot(a, b, dims=NN):
    return lax.dot_general(a, b, dims, preferred_element_type=F32)


def _sigmoid(x):
    return 1.0 / (1.0 + jnp.exp(-x))


def _rms_fwd(x, g):
    r = lax.rsqrt(jnp.mean(x * x, axis=-1, keepdims=True) + EPS)
    n = x * r
    return n * g, n, r


def _rms_bwd(dh, n, r, g):
    dn = dh * g
    dx = r * (dn - n * jnp.mean(dn * n, axis=-1, keepdims=True))
    return dx, jnp.sum(dh * n, axis=0, keepdims=True)


def _params(sem=None):
    return pltpu.CompilerParams(dimension_semantics=sem, vmem_limit_bytes=VMEM_LIMIT)


def _const(shape):
    nd = len(shape)
    return pl.BlockSpec(shape, lambda *_: (0,) * nd, pipeline_mode=pl.Buffered(1))


def _const_out(shape):
    nd = len(shape)
    return pl.BlockSpec(shape, lambda *_: (0,) * nd)


def _acc_rows(ref, t, rows):
    upd = jnp.concatenate(rows + [jnp.zeros((8 - len(rows), rows[0].shape[1]), F32)], axis=0)

    @pl.when(t == 0)
    def _():
        ref[...] = upd

    @pl.when(t > 0)
    def _():
        ref[...] = ref[...] + upd


def _fuse_exchange(body, n_in, n_out, n_scratch, plan, ndim):
    if plan is None:
        return body
    n = plan.n

    def wrapped(*refs):
        ins, cin = refs[:n_in], refs[n_in:n_in + n]
        outs, cout = refs[n_in + n:n_in + n + n_out], refs[n_in + n + n_out:n_in + 2 * n + n_out]
        rest = refs[n_in + 2 * n + n_out:]
        scr, csem = rest[:n_scratch], rest[n_scratch:]
        first = pl.program_id(0) == 0
        last = pl.program_id(0) == pl.num_programs(0) - 1
        for i in range(1, ndim):
            first = first & (pl.program_id(i) == 0)
            last = last & (pl.program_id(i) == pl.num_programs(i) - 1)

        @pl.when(first)
        def _():
            plan.start(cin, cout, csem)

        body(*ins, *outs, *scr)

        @pl.when(last)
        def _():
            plan.finish(cin, cout, csem)

    return wrapped


def _plan_extras(plan):
    if plan is None:
        return [], [], []
    return [ANY] * plan.n, list(plan.out_shape), list(plan.scratch_shapes)


def _in_proj(x, g, win_g, tm):
    s, d = x.shape
    nsh, _, wc = win_g.shape

    def body(x_ref, g_ref, w_ref, z_ref, h_ref):
        h, _, _ = _rms_fwd(x_ref[...], g_ref[...])
        hb = h.astype(BF16)
        h_ref[...] = hb
        for j in range(nsh):
            z_ref[:, j * wc:(j + 1) * wc] = _dot(hb, w_ref[j])

    return pl.pallas_call(
        body, name="in_proj", grid=(s // tm,),
        in_specs=[pl.BlockSpec((tm, d), lambda t: (t, 0)), _const((1, d)), _const((nsh, d, wc))],
        out_specs=[pl.BlockSpec((tm, nsh * wc), lambda t: (t, 0)), pl.BlockSpec((tm, d), lambda t: (t, 0))],
        out_shape=[jax.ShapeDtypeStruct((s, nsh * wc), F32), jax.ShapeDtypeStruct((s, d), BF16)],
        compiler_params=_params(("parallel",)),
    )(x, g, win_g)


def _lower_bound(l0, l1):
    m = jnp.maximum(l0, l1)
    e0, e1 = jnp.exp(l0 - m), jnp.exp(l1 - m)
    return e0 / (e0 + e1)


def _block_tri(n, group, upper):
    r = lax.broadcasted_iota(jnp.int32, (n, n), 0)
    c = lax.broadcasted_iota(jnp.int32, (n, n), 1)
    keep = (r // group == c // group) & ((c >= r) if upper else (c <= r))
    return keep.astype(BF16)


def _group_cumsum(tri, x):
    hi = x.astype(BF16)
    rest = x - hi.astype(F32)
    mid = rest.astype(BF16)
    lo = (rest - mid.astype(F32)).astype(BF16)
    return (_dot(tri, hi) + _dot(tri, mid)) + _dot(tri, lo)


def _decay(b, bj, rows, first):
    d = b - bj
    if first:
        d = jnp.where(rows >= first, d, NEG_BIG)
    return jnp.exp2(d)


class _RowSums:
    ORDER = (0, 4, 2, 6, 1, 5, 3, 7)

    def __init__(self, rows):
        self.rows = rows
        self.level = {4: {}, 2: {}, 1: {}}

    def _pair(self, p, q, d):
        return jnp.where((self.rows & d) != 0, p + pltpu.roll(p, d, axis=0), q + pltpu.roll(q, HALF - d, axis=0))

    def push(self, j, y, d=4):
        if d == 0:
            self.out = y
            return
        slot = self.level[d]
        key = j % d
        if key not in slot:
            slot[key] = (j, y)
            return
        j0, y0 = slot.pop(key)
        p, q = (y, y0) if j & d else (y0, y)
        self.push(key, self._pair(p, q, d), d // 2)

    def result(self):
        return self.out


def _hgrn_gates(qp, fp, lb):
    sq = _sigmoid(qp)
    sf = _sigmoid(fp)
    f = lb + (1.0 - lb) * sf
    return qp * sq, sq, f, sf


def _hgrn_fwd(z, l0, l1, gn, tc, unroll=1, plan=None, plan_args=()):
    s = z.shape[0]
    nsub = tc // SUB
    hd = HEAD_DIM

    def body(q_ref, f_ref, v_ref, g_ref, l0_ref, l1_ref, gn_ref, tri_ref, tric_ref, o_ref, oa_ref, st_ref,
             state, qs, ks, bs, os_):
        @pl.when(pl.program_id(1) == 0)
        def _():
            state[...] = jnp.zeros_like(state)

        lb = _lower_bound(l0_ref[...], l1_ref[...])
        q, _, f, _ = _hgrn_gates(q_ref[...], f_ref[...], lb)
        k = 1.0 - f
        lf = jnp.log(f) * LOG2E
        bc = _group_cumsum(tric_ref[...], lf)
        bounded = jnp.min(bc) >= -MAX_LOG2_GROWTH

        @pl.when(bounded)
        def _():
            qt = (q * jnp.exp2(bc)).astype(BF16)
            ki = (k * jnp.exp2(-bc)).astype(BF16)
            vb = v_ref[...].astype(BF16)
            a = jnp.where(tric_ref[...] > 0, _dot(qt, ki, NT), 0.0).astype(BF16)
            o_in = _dot(a, vb)
            for c in range(tc // CHUNK):
                rs = slice(c * CHUNK, (c + 1) * CHUNK)
                st = state[...]
                st_ref[c] = st
                os_[rs, :] = o_in[rs] + _dot(qt[rs], st.astype(BF16), NT)
                bl = bc[(c + 1) * CHUNK - 1:(c + 1) * CHUNK, :]
                kt = (k[rs] * jnp.exp2(bl - bc[rs])).astype(BF16)
                state[...] = st * jnp.exp2(bl) + _dot(vb[rs], kt, TN)

        @pl.when(jnp.logical_not(bounded))
        def _():
            qs[...] = q
            ks[...] = k
            bs[...] = _group_cumsum(tri_ref[...], lf)
            rows = lax.broadcasted_iota(jnp.int32, (HALF, 1), 0)

            def step(i, carry):
                r0 = pl.multiple_of(i * SUB, SUB)
                q_ = qs[pl.ds(r0, SUB), :]
                k_ = ks[pl.ds(r0, SUB), :]
                b_ = bs[pl.ds(r0, SUB), :]
                v_ = v_ref[pl.ds(r0, SUB), :]
                st = state[...]

                @pl.when(i % (CHUNK // SUB) == 0)
                def _():
                    st_ref[i // (CHUNK // SUB)] = st

                bl = b_[SUB - 1:SUB, :]
                o = _dot((q_ * jnp.exp2(b_)).astype(BF16), st.astype(BF16), NT)
                (q_lo, q_hi), (b_lo, b_hi), (o_lo, o_hi) = ((x[:HALF], x[HALF:]) for x in (q_, b_, o))
                for j in range(SUB):
                    bj, kj, vj = b_[j:j + 1, :], k_[j:j + 1, :], v_[j:j + 1, :]
                    if j < HALF:
                        e = _decay(b_lo, bj, rows, j)
                        o_lo = o_lo + jnp.sum(q_lo * e * kj, axis=-1, keepdims=True) * vj
                    e = _decay(b_hi, bj, rows, j - HALF if j > HALF else None)
                    o_hi = o_hi + jnp.sum(q_hi * e * kj, axis=-1, keepdims=True) * vj
                os_[pl.ds(r0, HALF), :] = o_lo
                os_[pl.ds(r0 + HALF, HALF), :] = o_hi
                kt = (k_ * jnp.exp2(bl - b_)).astype(BF16)
                state[...] = st * jnp.exp2(bl) + _dot(v_.astype(BF16), kt, TN)
                return carry

            lax.fori_loop(0, nsub, step, 0, unroll=unroll)

        o = os_[...]
        o_ref[...] = o
        r = lax.rsqrt(jnp.mean(o * o, axis=-1, keepdims=True) + EPS)
        gp = g_ref[...]
        oa_ref[...] = (o * r * gn_ref[...] * (gp * _sigmoid(gp))).astype(BF16)

    col = lambda k: pl.BlockSpec((tc, hd), lambda h, t: (t, k * HGRN_HEADS + h))
    vec = pl.BlockSpec((None, 1, hd), lambda h, t: (h, 0, 0))
    x_specs, x_shapes, x_scratch = _plan_extras(plan)
    return pl.pallas_call(
        _fuse_exchange(body, 9, 3, 5, plan, 2), name="hgrn_fwd", grid=(HGRN_HEADS, s // tc),
        in_specs=[col(0), col(1), col(2), col(3), vec, vec, vec, _const((tc, tc)), _const((tc, tc))] + x_specs,
        out_specs=[pl.BlockSpec((tc, hd), lambda h, t: (t, h)), pl.BlockSpec((tc, hd), lambda h, t: (t, h)),
                   pl.BlockSpec((None, tc // CHUNK, hd, hd), lambda h, t: (h, t, 0, 0))] + x_specs,
        out_shape=[jax.ShapeDtypeStruct((s, HGRN_WIDTH), F32), jax.ShapeDtypeStruct((s, HGRN_WIDTH), BF16),
                   jax.ShapeDtypeStruct((HGRN_HEADS, s // CHUNK, hd, hd), F32)] + x_shapes,
        scratch_shapes=[pltpu.VMEM((hd, hd), F32)] + [pltpu.VMEM((tc, hd), F32)] * 4 + x_scratch,
        compiler_params=_params(("arbitrary", "arbitrary")),
    )(z, z, z, z, l0, l1, gn, _block_tri(tc, SUB, False), _block_tri(tc, CHUNK, False), *plan_args)


def _pooled(p, ext, tok0):
    tm = p.shape[0]
    tok = tok0 + lax.broadcasted_iota(jnp.int32, (tm, 1), 0)
    outs = []
    for g, w in enumerate(POOL_WINDOWS):
        acc = ext[:, g * HEAD_DIM:(g + 1) * HEAD_DIM]
        sh = 1
        while sh < w:
            acc = acc + pltpu.roll(acc, sh, axis=0)
            sh *= 2
        cnt = jnp.minimum(tok + 1, w).astype(F32)
        outs.append(acc[POOL_HALO:, :] / cnt - p[:, g * HEAD_DIM:(g + 1) * HEAD_DIM])
    return outs


def _pool_fwd(z, wp, scale, tm):
    s = z.shape[0]
    pw = len(POOL_WINDOWS) * HEAD_DIM
    nb = tm // POOL_HALO

    def body(p_ref, prev_ref, wp_ref, sc_ref, ob_ref):
        t = pl.program_id(0)
        p = p_ref[...]
        prev = jnp.where(t > 0, prev_ref[...], 0.0)
        pooled = _pooled(p, jnp.concatenate([prev, p], axis=0), t * tm)
        ys = [_dot(pooled[g].astype(BF16), wp_ref[g].astype(BF16)) for g in range(len(POOL_WINDOWS))]
        ob_ref[...] = (jnp.concatenate(ys, axis=1) * sc_ref[...]).astype(BF16)

    return pl.pallas_call(
        body, name="pool_fwd", grid=(s // tm,),
        in_specs=[pl.BlockSpec((tm, pw), lambda t: (t, 4)),
                  pl.BlockSpec((POOL_HALO, pw), lambda t: (jnp.maximum(t * nb - 1, 0), 4)),
                  _const(wp.shape), _const((1, pw))],
        out_specs=pl.BlockSpec((tm, pw), lambda t: (t, 0)),
        out_shape=jax.ShapeDtypeStruct((s, pw), BF16),
        compiler_params=_params(("parallel",)),
    )(z, z, wp, scale)


def _kv_proj(mem, g, slab_g):
    m, d = mem.shape
    rows = d // N_CHIPS

    def body(mem_ref, g_ref, wk_ref, wv_ref, xk_ref, xv_ref):
        hm, _, _ = _rms_fwd(mem_ref[...], g_ref[...])
        hb = hm.astype(BF16)
        xk_ref[...] = _dot(hb, wk_ref[...].reshape(d, d)).astype(BF16)
        xv_ref[...] = _dot(hb, wv_ref[...].reshape(d, d)).astype(BF16)

    blk = lambda k: pl.BlockSpec((N_CHIPS, rows, d), lambda i: (0, k, 0))
    return pl.pallas_call(
        body, name="kv_proj", grid=(1,),
        in_specs=[_const((m, d)), _const((1, d)), blk(2), blk(3)],
        out_specs=[_const_out((m, d)), _const_out((m, d))],
        out_shape=[jax.ShapeDtypeStruct((m, d), BF16)] * 2,
        compiler_params=_params(("arbitrary",)),
    )(mem, g, slab_g, slab_g)


def _softmax_rows(sc):
    e = jnp.exp(sc - jnp.max(sc, axis=-1, keepdims=True))
    return e / jnp.sum(e, axis=-1, keepdims=True)


def _mix_xattn_fwd(x, oa, ob, gx, slab_g, wo_g, xk, xv, tm):
    s, d = x.shape
    m = xk.shape[0]
    rows = d // N_CHIPS
    hw = oa.shape[1]
    e = XATTN_HEAD_DIM

    def body(x_ref, oa_ref, ob_ref, gx_ref, wout_ref, wq_ref, wo_ref, xk_ref, xv_ref,
             x1_ref, mixed_ref, hq_ref, xq_ref, att_ref, x2_ref):
        mixed = jnp.concatenate([oa_ref[...], ob_ref[...]], axis=1)
        mixed_ref[...] = mixed
        x1 = x_ref[...] + _dot(mixed, wout_ref[...].reshape(d, d))
        x1_ref[...] = x1
        hq, _, _ = _rms_fwd(x1, gx_ref[...])
        hqb = hq.astype(BF16)
        hq_ref[...] = hqb
        xq = _dot(hqb, wq_ref[...].reshape(d, d)).astype(BF16)
        xq_ref[...] = xq
        atts = []
        for h in range(XATTN_HEADS):
            cs = slice(h * e, (h + 1) * e)
            p = _softmax_rows(_dot(xq[:, cs], xk_ref[:, cs], NT) * (e ** -0.5))
            atts.append(_dot(p.astype(BF16), xv_ref[:, cs]).astype(BF16))
        att = jnp.concatenate(atts, axis=1)
        att_ref[...] = att
        for j in range(N_CHIPS):
            x2_ref[:, j * rows:(j + 1) * rows] = x1[:, j * rows:(j + 1) * rows] + _dot(att, wo_ref[j])

    tile = lambda w: pl.BlockSpec((tm, w), lambda t: (t, 0))
    blk = lambda k: pl.BlockSpec((N_CHIPS, rows, d), lambda t: (0, k, 0), pipeline_mode=pl.Buffered(1))
    return pl.pallas_call(
        body, name="mix_xattn_fwd", grid=(s // tm,),
        in_specs=[tile(d), tile(hw), tile(hw), _const((1, d)), blk(0), blk(1), _const(wo_g.shape),
                  _const((m, d)), _const((m, d))],
        out_specs=[tile(d)] * 6,
        out_shape=[jax.ShapeDtypeStruct((s, d), F32)] + [jax.ShapeDtypeStruct((s, d), BF16)] * 4
                  + [jax.ShapeDtypeStruct((s, d), F32)],
        compiler_params=_params(("parallel",)),
    )(x, oa, ob, gx, slab_g, slab_g, wo_g, xk, xv)


def _mlp_loss_fwd(x2, gffn, gfin, slab_g, target, tm):
    s, d = x2.shape
    wr = slab_g.shape[1] // 3

    def body(x2_ref, gffn_ref, gfin_ref, w1_ref, w2_ref, tg_ref, a_ref, hf_ref, dx3_ref, dx3b_ref, st_ref):
        x2v = x2_ref[...]
        hf, _, _ = _rms_fwd(x2v, gffn_ref[...])
        hfb = hf.astype(BF16)
        hf_ref[...] = hfb
        acc = x2v
        for j in range(N_CHIPS):
            a = _dot(hfb, w1_ref[j])
            a_ref[:, j * wr:(j + 1) * wr] = a
            r = jnp.maximum(a, 0.0)
            acc = acc + _dot((r * r).astype(BF16), w2_ref[j])
        gf = gfin_ref[...]
        y, n, r3 = _rms_fwd(acc, gf)
        err = y - tg_ref[...]
        loss = 0.5 * jnp.sum(jnp.sum(err * err, axis=-1, keepdims=True) * (1.0 / d), axis=0, keepdims=True)
        dy = err * (1.0 / d)
        dx3, dgf = _rms_bwd(dy, n, r3, gf)
        dx3_ref[...] = dx3
        dx3b_ref[...] = dx3.astype(BF16)
        _acc_rows(st_ref, pl.program_id(0), [dgf, jnp.broadcast_to(loss, (1, d))])

    tile = lambda w: pl.BlockSpec((tm, w), lambda t: (t, 0))
    blk = lambda k: pl.BlockSpec((N_CHIPS, wr, d), lambda t: (0, k, 0), pipeline_mode=pl.Buffered(1))
    return pl.pallas_call(
        body, name="mlp_loss_fwd", grid=(s // tm,),
        in_specs=[tile(d), _const((1, d)), _const((1, d)), blk(1), blk(2), tile(d)],
        out_specs=[tile(N_CHIPS * wr), tile(d), tile(d), tile(d), _const_out((8, d))],
        out_shape=[jax.ShapeDtypeStruct((s, N_CHIPS * wr), F32), jax.ShapeDtypeStruct((s, d), BF16),
                   jax.ShapeDtypeStruct((s, d), F32), jax.ShapeDtypeStruct((s, d), BF16),
                   jax.ShapeDtypeStruct((8, d), F32)],
        compiler_params=_params(("arbitrary",)),
    )(x2, gffn, gfin, slab_g, slab_g, target)


def _mlp_bwd(dx3, dx3b, a, x2, gffn, slab_g, tm):
    s, d = x2.shape
    wr = slab_g.shape[1] // 3

    def body(dx3_ref, dx3b_ref, a_ref, x2_ref, g_ref, w1_ref, w2_ref, da_ref, u_ref, dx2_ref, dx2b_ref, st_ref):
        dyb = dx3b_ref[...]
        dhf = jnp.zeros((tm, d), F32)
        for j in range(N_CHIPS):
            r = jnp.maximum(a_ref[:, j * wr:(j + 1) * wr], 0.0)
            da = (_dot(dyb, w2_ref[j], NT) * (2.0 * r)).astype(BF16)
            da_ref[:, j * wr:(j + 1) * wr] = da
            u_ref[:, j * wr:(j + 1) * wr] = (r * r).astype(BF16)
            dhf = dhf + _dot(da, w1_ref[j], NT)
        g = g_ref[...]
        _, n, r2 = _rms_fwd(x2_ref[...], g)
        dxn, dg = _rms_bwd(dhf, n, r2, g)
        dx2 = dx3_ref[...] + dxn
        dx2_ref[...] = dx2
        dx2b_ref[...] = dx2.astype(BF16)
        _acc_rows(st_ref, pl.program_id(0), [dg])

    tile = lambda w: pl.BlockSpec((tm, w), lambda t: (t, 0))
    blk = lambda k: pl.BlockSpec((N_CHIPS, wr, d), lambda t: (0, k, 0), pipeline_mode=pl.Buffered(1))
    nf = N_CHIPS * wr
    return pl.pallas_call(
        body, name="mlp_bwd", grid=(s // tm,),
        in_specs=[tile(d), tile(d), tile(nf), tile(d), _const((1, d)), blk(1), blk(2)],
        out_specs=[tile(nf), tile(nf), tile(d), tile(d), _const_out((8, d))],
        out_shape=[jax.ShapeDtypeStruct((s, nf), BF16), jax.ShapeDtypeStruct((s, nf), BF16),
                   jax.ShapeDtypeStruct((s, d), F32), jax.ShapeDtypeStruct((s, d), BF16),
                   jax.ShapeDtypeStruct((8, d), F32)],
        compiler_params=_params(("arbitrary",)),
    )(dx3, dx3b, a, x2, gffn, slab_g, slab_g)


def _xattn_mix_bwd(dx2, x1, xq, xk, xv, gx, slab_g, wo_g, tm, plan=None, plan_args=()):
    s, d = x1.shape
    m = xk.shape[0]
    rows = d // N_CHIPS
    e = XATTN_HEAD_DIM

    def body(dx2_ref, x1_ref, xq_ref, xk_ref, xv_ref, gx_ref, wout_ref, wq_ref, wo_ref,
             dx1_ref, dx1b_ref, dxq_ref, dmix_ref, dxk_ref, dxv_ref, st_ref):
        t = pl.program_id(0)
        dx2 = dx2_ref[...]
        dx2b = dx2.astype(BF16)
        datt = jnp.zeros((tm, d), F32)
        for j in range(N_CHIPS):
            datt = datt + _dot(dx2b[:, j * rows:(j + 1) * rows], wo_ref[j], NT)
        dattb = datt.astype(BF16)
        dxqs, dxks, dxvs = [], [], []
        for h in range(XATTN_HEADS):
            cs = slice(h * e, (h + 1) * e)
            xq_h, xk_h, xv_h = xq_ref[:, cs], xk_ref[:, cs], xv_ref[:, cs]
            p = _softmax_rows(_dot(xq_h, xk_h, NT) * (e ** -0.5))
            dp = _dot(dattb[:, cs], xv_h, NT)
            ds = (p * (dp - jnp.sum(dp * p, axis=-1, keepdims=True)) * (e ** -0.5)).astype(BF16)
            dxqs.append(_dot(ds, xk_h).astype(BF16))
            dxks.append(_dot(ds, xq_h, TN))
            dxvs.append(_dot(p.astype(BF16), dattb[:, cs], TN))
        dxq = jnp.concatenate(dxqs, axis=1)
        dxq_ref[...] = dxq
        dxk = jnp.concatenate(dxks, axis=1)
        dxv = jnp.concatenate(dxvs, axis=1)

        @pl.when(t == 0)
        def _():
            dxk_ref[...] = dxk
            dxv_ref[...] = dxv

        @pl.when(t > 0)
        def _():
            dxk_ref[...] = dxk_ref[...] + dxk
            dxv_ref[...] = dxv_ref[...] + dxv

        dhq = jnp.concatenate([_dot(dxq, wq_ref[j], NT) for j in range(N_CHIPS)], axis=1)
        g = gx_ref[...]
        _, n, r1 = _rms_fwd(x1_ref[...], g)
        dxn, dg = _rms_bwd(dhq, n, r1, g)
        dx1 = dx2 + dxn
        dx1_ref[...] = dx1
        dx1b = dx1.astype(BF16)
        dx1b_ref[...] = dx1b
        for j in range(N_CHIPS):
            dmix_ref[:, j * rows:(j + 1) * rows] = _dot(dx1b, wout_ref[j], NT)
        _acc_rows(st_ref, t, [dg])

    tile = lambda: pl.BlockSpec((tm, d), lambda t: (t, 0))
    blk = lambda k: pl.BlockSpec((N_CHIPS, rows, d), lambda t: (0, k, 0), pipeline_mode=pl.Buffered(1))
    x_specs, x_shapes, x_scratch = _plan_extras(plan)
    return pl.pallas_call(
        _fuse_exchange(body, 9, 7, 0, plan, 1), name="xattn_mix_bwd", grid=(s // tm,),
        in_specs=[tile(), tile(), tile(), _const((m, d)), _const((m, d)), _const((1, d)), blk(0), blk(1),
                  _const(wo_g.shape)] + x_specs,
        out_specs=[tile(), tile(), tile(), tile(), _const_out((m, d)), _const_out((m, d)), _const_out((8, d))]
                  + x_specs,
        out_shape=[jax.ShapeDtypeStruct((s, d), F32), jax.ShapeDtypeStruct((s, d), BF16),
                   jax.ShapeDtypeStruct((s, d), BF16), jax.ShapeDtypeStruct((s, d), F32),
                   jax.ShapeDtypeStruct((m, d), F32), jax.ShapeDtypeStruct((m, d), F32),
                   jax.ShapeDtypeStruct((8, d), F32)] + x_shapes,
        scratch_shapes=x_scratch,
        compiler_params=_params(("arbitrary",)),
    )(dx2, x1, xq, xk, xv, gx, slab_g, slab_g, wo_g, *plan_args)


def _kv_bwd(mem, g, dxk, dxv, slab_g):
    m, d = mem.shape
    rows = d // N_CHIPS

    def body(mem_ref, g_ref, dxk_ref, dxv_ref, wk_ref, wv_ref, dwk_ref, dwv_ref, st_ref):
        gv = g_ref[...]
        hm, n, _ = _rms_fwd(mem_ref[...], gv)
        hb = hm.astype(BF16)
        dkb = dxk_ref[...].astype(BF16)
        dvb = dxv_ref[...].astype(BF16)
        dhm = []
        for j in range(N_CHIPS):
            hj = hb[:, j * rows:(j + 1) * rows]
            dwk_ref[j] = _dot(hj, dkb, TN)
            dwv_ref[j] = _dot(hj, dvb, TN)
            dhm.append(_dot(dkb, wk_ref[j], NT) + _dot(dvb, wv_ref[j], NT))
        dg = jnp.sum(jnp.concatenate(dhm, axis=1) * n, axis=0, keepdims=True)
        st_ref[...] = jnp.concatenate([dg, jnp.zeros((7, d), F32)], axis=0)

    blk = lambda k: pl.BlockSpec((N_CHIPS, rows, d), lambda i: (0, k, 0))
    return pl.pallas_call(
        body, name="kv_bwd", grid=(1,),
        in_specs=[_const((m, d)), _const((1, d)), _const((m, d)), _const((m, d)), blk(2), blk(3)],
        out_specs=[_const_out((N_CHIPS, rows, d)), _const_out((N_CHIPS, rows, d)), _const_out((8, d))],
        out_shape=[jax.ShapeDtypeStruct((N_CHIPS, rows, d), F32)] * 2 + [jax.ShapeDtypeStruct((8, d), F32)],
        compiler_params=_params(("arbitrary",)),
    )(mem, g, dxk, dxv, slab_g, slab_g)


def _pool_bwd(z, dmix, wp, scale, tm, plan=None, plan_args=()):
    s = z.shape[0]
    ng = len(POOL_WINDOWS)
    pw = ng * HEAD_DIM
    nb = tm // POOL_HALO
    nt = s // tm
    n_ext = tm + POOL_HALO

    def body(p_ref, prev_ref, dm_ref, dmn_ref, wp_ref, sc_ref, dp_ref, dwp_ref, st_ref):
        t = pl.program_id(0)
        p = p_ref[...]
        prev = jnp.where(t > 0, prev_ref[...], 0.0)
        pooled = _pooled(p, jnp.concatenate([prev, p], axis=0), t * tm)
        dm = dm_ref[...]
        dme = jnp.concatenate([dm, jnp.where(t < nt - 1, dmn_ref[...], 0.0)], axis=0) * sc_ref[...]
        tok = t * tm + lax.broadcasted_iota(jnp.int32, (n_ext, 1), 0)
        dsc, dps, dwps = [], [], []
        for g, w in enumerate(POOL_WINDOWS):
            cs = slice(g * HEAD_DIM, (g + 1) * HEAD_DIM)
            wpb = wp_ref[g].astype(BF16)
            pb = pooled[g].astype(BF16)
            dsc.append(jnp.sum(dm[:, cs] * _dot(pb, wpb), axis=0, keepdims=True))
            dye = dme[:, cs].astype(BF16)
            dwps.append(_dot(pb, dye[:tm], TN))
            dpe = _dot(dye, wpb, NT)
            acc = dpe / jnp.minimum(tok + 1, w).astype(F32)
            sh = 1
            while sh < w:
                acc = acc + pltpu.roll(acc, n_ext - sh, axis=0)
                sh *= 2
            dps.append(acc[:tm] - dpe[:tm])
        dp_ref[...] = jnp.concatenate(dps, axis=1)
        dsc_row = jnp.concatenate(dsc, axis=1)

        @pl.when(t == 0)
        def _():
            for g in range(ng):
                dwp_ref[g] = dwps[g]

        @pl.when(t > 0)
        def _():
            for g in range(ng):
                dwp_ref[g] = dwp_ref[g] + dwps[g]

        _acc_rows(st_ref, t, [dsc_row])

    x_specs, x_shapes, x_scratch = _plan_extras(plan)
    return pl.pallas_call(
        _fuse_exchange(body, 6, 3, 0, plan, 1), name="pool_bwd", grid=(nt,),
        in_specs=[pl.BlockSpec((tm, pw), lambda t: (t, 4)),
                  pl.BlockSpec((POOL_HALO, pw), lambda t: (jnp.maximum(t * nb - 1, 0), 4)),
                  pl.BlockSpec((tm, pw), lambda t: (t, 1)),
                  pl.BlockSpec((POOL_HALO, pw), lambda t: (jnp.minimum((t + 1) * nb, s // POOL_HALO - 1), 1)),
                  _const(wp.shape), _const((1, pw))] + x_specs,
        out_specs=[pl.BlockSpec((tm, pw), lambda t: (t, 0)), _const_out(wp.shape), _const_out((8, pw))] + x_specs,
        out_shape=[jax.ShapeDtypeStruct((s, pw), F32), jax.ShapeDtypeStruct(wp.shape, F32),
                   jax.ShapeDtypeStruct((8, pw), F32)] + x_shapes,
        scratch_shapes=x_scratch,
        compiler_params=_params(("arbitrary",)),
    )(z, z, dmix, dmix, wp, scale, *plan_args)


def _hgrn_bwd(z, o, dmix, st, l0, l1, gn, tc, unroll=1, plan=None, plan_args=()):
    s = z.shape[0]
    nsub = tc // SUB
    nt = s // tc
    hd = HEAD_DIM

    def body(q_ref, f_ref, v_ref, g_ref, l0_ref, l1_ref, gn_ref, o_ref, dm_ref, st_ref,
             tril_ref, triu_ref, trilc_ref, triuc_ref,
             dq_ref, df_ref, di_ref, dg_ref, stat_ref, dstate, qs, ks, bs, dos, dqs, dks, dbs, sts):
        t = pl.program_id(1)

        @pl.when(t == 0)
        def _():
            dstate[...] = jnp.zeros_like(dstate)

        lb = _lower_bound(l0_ref[...], l1_ref[...])
        qp = q_ref[...]
        q, sq, f, sf = _hgrn_gates(qp, f_ref[...], lb)
        k = 1.0 - f
        lf = jnp.log(f) * LOG2E
        bc = _group_cumsum(trilc_ref[...], lf)
        bounded = jnp.min(bc) >= -MAX_LOG2_GROWTH

        o = o_ref[...]
        r = lax.rsqrt(jnp.mean(o * o, axis=-1, keepdims=True) + EPS)
        n = o * r
        gnv = gn_ref[...]
        gp = g_ref[...]
        sg = _sigmoid(gp)
        dm = dm_ref[...]
        dg_ref[...] = dm * (n * gnv) * (sg * (1.0 + gp * (1.0 - sg)))
        don = dm * (gp * sg)
        dgn = jnp.sum(don * n, axis=0, keepdims=True)
        dn = don * gnv
        do_all = r * (dn - n * jnp.mean(dn * n, axis=-1, keepdims=True))

        @pl.when(bounded)
        def _():
            eb = jnp.exp2(bc)
            eib = jnp.exp2(-bc)
            qt = (q * eb).astype(BF16)
            ki = (k * eib).astype(BF16)
            vb = v_ref[...].astype(BF16)
            dob = do_all.astype(BF16)
            mask = trilc_ref[...] > 0
            a = jnp.where(mask, _dot(qt, ki, NT), 0.0).astype(BF16)
            da = jnp.where(mask, _dot(dob, vb, NT), 0.0).astype(BF16)
            dq_in = _dot(da, ki)
            dk_in = _dot(da, qt, TN)
            dv_in = _dot(a, dob, TN)
            last_row = lax.broadcasted_iota(jnp.int32, (CHUNK, 1), 0) == CHUNK - 1
            for c in reversed(range(tc // CHUNK)):
                rs = slice(c * CHUNK, (c + 1) * CHUNK)
                stp = st_ref[c]
                dst = dstate[...]
                dstb = dst.astype(BF16)
                bl = bc[(c + 1) * CHUNK - 1:(c + 1) * CHUNK, :]
                ekl = jnp.exp2(bl - bc[rs])
                ebl = jnp.exp2(bl)
                kt = k[rs] * ekl
                dq_st = _dot(dob[rs], stp.astype(BF16)) * eb[rs]
                dkt = _dot(vb[rs], dstb)
                extra = jnp.sum(kt * dkt, axis=0, keepdims=True) + ebl * jnp.sum(stp * dst, axis=0, keepdims=True)
                dqs[rs, :] = dq_st + dq_in[rs] * eb[rs]
                dks[rs, :] = dkt * ekl + dk_in[rs] * eib[rs]
                di_ref[rs, :] = _dot(kt.astype(BF16), dstb, NT) + dv_in[rs]
                dbs[rs, :] = (q[rs] * dq_st - kt * dkt + jnp.where(last_row, extra, 0.0)
                              + (qt[rs].astype(F32) * dq_in[rs] - ki[rs].astype(F32) * dk_in[rs]))
                dstate[...] = dst * ebl + _dot(dob[rs], qt[rs], TN)
            dbs[...] = _group_cumsum(triuc_ref[...], dbs[...])

        @pl.when(jnp.logical_not(bounded))
        def _():
            qs[...] = q
            ks[...] = k
            bs[...] = _group_cumsum(tril_ref[...], lf)
            dos[...] = do_all
            per = CHUNK // SUB

            def restore(i, carry):
                @pl.when(i % per == 0)
                def _():
                    sts[i] = st_ref[i // per]

                @pl.when(i % per != 0)
                def _():
                    rp = pl.multiple_of((i - 1) * SUB, SUB)
                    b_ = bs[pl.ds(rp, SUB), :]
                    bl = b_[SUB - 1:SUB, :]
                    kt = (ks[pl.ds(rp, SUB), :] * jnp.exp2(bl - b_)).astype(BF16)
                    sts[i] = sts[i - 1] * jnp.exp2(bl) + _dot(v_ref[pl.ds(rp, SUB), :].astype(BF16), kt, TN)

                return carry

            lax.fori_loop(0, nsub, restore, 0)
            rows = lax.broadcasted_iota(jnp.int32, (HALF, 1), 0)
            last_row = lax.broadcasted_iota(jnp.int32, (SUB, 1), 0) == SUB - 1

            def step(i, carry):
                ii = nsub - 1 - i
                r0 = pl.multiple_of(ii * SUB, SUB)
                q_ = qs[pl.ds(r0, SUB), :]
                k_ = ks[pl.ds(r0, SUB), :]
                b_ = bs[pl.ds(r0, SUB), :]
                v_ = v_ref[pl.ds(r0, SUB), :]
                do_ = dos[pl.ds(r0, SUB), :]
                stp = sts[ii]
                dst = dstate[...]
                bl = b_[SUB - 1:SUB, :]
                eb = jnp.exp2(b_)
                ekl = jnp.exp2(bl - b_)
                ebl = jnp.exp2(bl)
                dob = do_.astype(BF16)
                dstb = dst.astype(BF16)
                kt = k_ * ekl
                dq = _dot(dob, stp.astype(BF16)) * eb
                dkt = _dot(v_.astype(BF16), dstb)
                dk = dkt * ekl
                dv = _dot(kt.astype(BF16), dstb, NT)
                extra = jnp.sum(kt * dkt, axis=0, keepdims=True) + ebl * jnp.sum(stp * dst, axis=0, keepdims=True)
                halves = lambda x: [x[:HALF], x[HALF:]]
                q_h, b_h, do_h, dq_h, dk_h, dv_h = (halves(x) for x in (q_, b_, do_, dq, dk, dv))
                for own in range(2):
                    dk_rows, dv_rows = _RowSums(rows), _RowSums(rows)
                    for jj in _RowSums.ORDER:
                        j = own * HALF + jj
                        bj, kj, vj = b_[j:j + 1, :], k_[j:j + 1, :], v_[j:j + 1, :]
                        dk_sum = dv_sum = None
                        for h in range(own, 2):
                            e = _decay(b_h[h], bj, rows, jj if h == own else None)
                            pe = q_h[h] * e
                            acol = jnp.sum(pe * kj, axis=-1, keepdims=True)
                            dacol = jnp.sum(do_h[h] * vj, axis=-1, keepdims=True)
                            dq_h[h] = dq_h[h] + dacol * (e * kj)
                            dk_sum = dacol * pe if dk_sum is None else dk_sum + dacol * pe
                            dv_sum = acol * do_h[h] if dv_sum is None else dv_sum + acol * do_h[h]
                        dk_rows.push(jj, dk_sum)
                        dv_rows.push(jj, dv_sum)
                    dk_h[own] = dk_h[own] + dk_rows.result()
                    dv_h[own] = dv_h[own] + dv_rows.result()
                dq, dk, dv = (jnp.concatenate(x, axis=0) for x in (dq_h, dk_h, dv_h))
                dqs[pl.ds(r0, SUB), :] = dq
                dks[pl.ds(r0, SUB), :] = dk
                di_ref[pl.ds(r0, SUB), :] = dv
                dbs[pl.ds(r0, SUB), :] = q_ * dq - k_ * dk + jnp.where(last_row, extra, 0.0)
                dstate[...] = dst * ebl + _dot(dob, (q_ * eb).astype(BF16), TN)
                return carry

            lax.fori_loop(0, nsub, step, 0, unroll=unroll)
            dbs[...] = _group_cumsum(triu_ref[...], dbs[...])

        dlf = dbs[...]
        dfv = dlf / f - dks[...]
        df_ref[...] = dfv * (1.0 - lb) * sf * (1.0 - sf)
        dlb = jnp.sum(dfv * (1.0 - sf), axis=0, keepdims=True)
        dq_ref[...] = dqs[...] * (sq * (1.0 + qp * (1.0 - sq)))
        _acc_rows(stat_ref, t, [dgn, dlb])

    rev = lambda t: nt - 1 - t
    col = lambda k: pl.BlockSpec((tc, hd), lambda h, t: (rev(t), k * HGRN_HEADS + h))
    vec = pl.BlockSpec((None, 1, hd), lambda h, t: (h, 0, 0))
    head = pl.BlockSpec((tc, hd), lambda h, t: (rev(t), h))
    x_specs, x_shapes, x_scratch = _plan_extras(plan)
    return pl.pallas_call(
        _fuse_exchange(body, 14, 5, 9, plan, 2), name="hgrn_bwd", grid=(HGRN_HEADS, nt),
        in_specs=[col(0), col(1), col(2), col(3), vec, vec, vec, head, head,
                  pl.BlockSpec((None, tc // CHUNK, hd, hd), lambda h, t: (h, rev(t), 0, 0))]
                 + [_const((tc, tc))] * 4 + x_specs,
        out_specs=[head, head, head, head, pl.BlockSpec((None, 8, hd), lambda h, t: (h, 0, 0))] + x_specs,
        out_shape=[jax.ShapeDtypeStruct((s, HGRN_WIDTH), F32)] * 4 + [jax.ShapeDtypeStruct((HGRN_HEADS, 8, hd), F32)]
                  + x_shapes,
        scratch_shapes=[pltpu.VMEM((hd, hd), F32)] + [pltpu.VMEM((tc, hd), F32)] * 7
                       + [pltpu.VMEM((nsub, hd, hd), F32)] + x_scratch,
        compiler_params=_params(("arbitrary", "arbitrary")),
    )(z, z, z, z, l0, l1, gn, o, dmix, st, _block_tri(tc, SUB, False), _block_tri(tc, SUB, True),
      _block_tri(tc, CHUNK, False), _block_tri(tc, CHUNK, True), *plan_args)


def _in_bwd(dparts, dx1, x, g, win_g, tm):
    s, d = x.shape
    nsh, _, wc = win_g.shape
    pw = dparts[0].shape[1]

    def body(dq_ref, df_ref, di_ref, dg_ref, dp_ref, dx1_ref, x_ref, g_ref, w_ref, gx_ref, dz_ref, st_ref):
        dz = jnp.concatenate([dq_ref[...], df_ref[...], di_ref[...], dg_ref[...], dp_ref[...]], axis=1).astype(BF16)
        dz_ref[...] = dz
        dh = jnp.zeros((tm, d), F32)
        for j in range(nsh):
            dh = dh + _dot(dz[:, j * wc:(j + 1) * wc], w_ref[j], NT)
        gv = g_ref[...]
        _, n, r = _rms_fwd(x_ref[...], gv)
        dxn, dg = _rms_bwd(dh, n, r, gv)
        gx_ref[...] = dx1_ref[...] + dxn
        _acc_rows(st_ref, pl.program_id(0), [dg])

    tile = lambda w: pl.BlockSpec((tm, w), lambda t: (t, 0))
    return pl.pallas_call(
        body, name="in_bwd", grid=(s // tm,),
        in_specs=[tile(pw)] * 5 + [tile(d), tile(d), _const((1, d)), _const(win_g.shape)],
        out_specs=[tile(d), tile(nsh * wc), _const_out((8, d))],
        out_shape=[jax.ShapeDtypeStruct((s, d), F32), jax.ShapeDtypeStruct((s, nsh * wc), BF16),
                   jax.ShapeDtypeStruct((8, d), F32)],
        compiler_params=_params(("arbitrary",)),
    )(*dparts, dx1, x, g, win_g)


def _tn_grad(name, a, b, out_rows, out_cols, a_sharded, tr, tc):
    s = a.shape[0]
    nr, nc = out_rows // tr, out_cols // tc

    def body(a_ref, b_ref, o_ref):
        o_ref[...] = _dot(a_ref[...], b_ref[...], TN)

    a_map = (lambda j, i, k: (0, j * nr + i)) if a_sharded else (lambda j, i, k: (0, i))
    b_map = (lambda j, i, k: (0, k)) if a_sharded else (lambda j, i, k: (0, j * nc + k))
    return pl.pallas_call(
        body, name=name, grid=(N_CHIPS, nr, nc),
        in_specs=[pl.BlockSpec((s, tr), a_map), pl.BlockSpec((s, tc), b_map)],
        out_specs=pl.BlockSpec((None, tr, tc), lambda j, i, k: (j, i, k)),
        out_shape=jax.ShapeDtypeStruct((N_CHIPS, out_rows, out_cols), F32),
        compiler_params=_params(("parallel", "parallel", "parallel")),
    )(a, b)


FFN_NAMES = ("w_ff1", "w_ff2")
ATTN_NAMES = ("w_xo", "w_xq", "w_out", "w_xk", "w_xv")
EARLY_NAMES = FFN_NAMES + ATTN_NAMES
BIG_NAMES = EARLY_NAMES + ("w_in",)


def _halved(g):
    return g.reshape(N_CHIPS, 2, g.shape[1] // 2, g.shape[2])


def _pair_adds(names, gs, got, idx):
    pairs = [_grad_pair_add("grad_pair_add_" + k, g, r, idx, tr=min(256, g.shape[2])) for k, g, r in zip(names, gs, got)]
    return [p[0] for p in pairs], [p[1] for p in pairs]


def _step(x, mem, target, small, shards, idx):
    d = x.shape[1]
    l0 = small["lb_logits"][0].reshape(HGRN_HEADS, 1, HEAD_DIM)
    l1 = small["lb_logits"][1].reshape(HGRN_HEADS, 1, HEAD_DIM)
    gn = small["hgrn_norm_g"].reshape(HGRN_HEADS, 1, HEAD_DIM)
    wp = small["w_pool"].reshape(len(POOL_WINDOWS), HEAD_DIM, HEAD_DIM)
    psc = small["pool_scale"].reshape(1, -1)
    gmix, gx, gmem, gffn = (small[k].reshape(1, d) for k in ("norm_mix_g", "norm_x_g", "norm_mem_g", "norm_ffn_g"))
    gfin = small["final_norm_g"].reshape(1, d)

    (win_g,) = _run_exchange("gather_w_in", _WeightGather([shards["w_in"]]), [shards["w_in"]])
    z, h = _in_proj(x, gmix, win_g, tm=512)
    rest = [shards["slab"], shards["w_xo"]]
    o, oa, st, slab_g, wo_g = _hgrn_fwd(z, l0, l1, gn, tc=256, unroll=8, plan=_WeightGather(rest), plan_args=rest)
    ob = _pool_fwd(z, wp, psc, tm=512)
    xk, xv = _kv_proj(mem, gmem, slab_g)
    x1, mixed, hq, xq, att, x2 = _mix_xattn_fwd(x, oa, ob, gx, slab_g, wo_g, xk, xv, tm=512)
    a, hf, dx3, dx3b, st_loss = _mlp_loss_fwd(x2, gffn, gfin, slab_g, target, tm=512)

    da, u, dx2, dx2b, st_ffn = _mlp_bwd(dx3, dx3b, a, x2, gffn, slab_g, tm=256)
    dw = {}
    dw["w_ff1"] = _tn_grad("dw_ff1", hf, da, d, d, False, 1024, 1024)
    dw["w_ff2"] = _tn_grad("dw_ff2", u, dx3b, d, d, True, 1024, 1024)
    gs_ffn = [_halved(dw[k]) for k in FFN_NAMES]
    dx1, dx1b, dxq, dmix, dxk, dxv, st_x, *got_ffn = _xattn_mix_bwd(
        dx2, x1, xq, xk, xv, gx, slab_g, wo_g, tm=512, plan=_PairExchange(gs_ffn), plan_args=gs_ffn)
    dw["w_xo"] = _tn_grad("dw_xo", att, dx2b, d, d // N_CHIPS, False, 1024, 256)
    dw["w_xq"] = _tn_grad("dw_xq", hq, dxq, d // N_CHIPS, d, True, 256, 1024)
    dw["w_out"] = _tn_grad("dw_out", mixed, dx1b, d // N_CHIPS, d, True, 256, 1024)
    dw["w_xk"], dw["w_xv"], st_mem = _kv_bwd(mem, gmem, dxk, dxv, slab_g)
    gs_attn = [_halved(dw[k]) for k in ATTN_NAMES]
    dp, d_wp, st_pool, *got_attn = _pool_bwd(z, dmix, wp, psc, tm=512, plan=_PairExchange(gs_attn), plan_args=gs_attn)
    keeps, sends = _pair_adds(EARLY_NAMES, gs_ffn + gs_attn, got_ffn + got_attn, idx)

    dq, df, di, dg, st_hgrn, *received = _hgrn_bwd(z, o, dmix, st, l0, l1, gn, tc=256, unroll=4,
                                                    plan=_ChipExchange(sends), plan_args=sends)
    grad_x, dz, st_mix = _in_bwd([dq, df, di, dg, dp], dx1, x, gmix, win_g, tm=512)
    gs_in = [_halved(_tn_grad("dw_in", h, dz, d, win_g.shape[2], False, 1024, win_g.shape[2]))]
    got_in = _run_exchange("grad_pair_exchange_w_in", _PairExchange(gs_in), gs_in)
    keep_in, send_in = _pair_adds(("w_in",), gs_in, got_in, idx)
    recv_in = _run_exchange("grad_chip_exchange_w_in", _ChipExchange(send_in), send_in)

    partials = dict(zip(BIG_NAMES, zip(keeps + keep_in, list(received) + list(recv_in))))
    stats = dict(mix=st_mix, x=st_x, mem=st_mem, ffn=st_ffn, loss=st_loss, hgrn=st_hgrn, pool=st_pool)
    return grad_x, stats, d_wp, partials


def _place():
    x, y, c = lax.axis_index("x"), lax.axis_index("y"), lax.axis_index("c")
    return x, y, c, [(x, 1 - y), (1 - x, y), (1 - x, 1 - y)]


def _rcopy(src, dst, ssem, rsem, dev):
    return pltpu.make_async_remote_copy(src_ref=src, dst_ref=dst, send_sem=ssem, recv_sem=rsem,
                                        device_id=dev, device_id_type=MESH)


class _WeightGather:
    def __init__(self, shards):
        self.n = len(shards)
        self.rows = [w.shape[0] for w in shards]
        self.out_shape = [jax.ShapeDtypeStruct((N_CHIPS,) + w.shape, w.dtype) for w in shards]
        self.scratch_shapes = [pltpu.SemaphoreType.DMA((self.n,))] * 2 + [pltpu.SemaphoreType.DMA((self.n, 3))] * 4

    def _copies(self, ins, outs, sems, with_pass_on):
        lsem, lrsem, ssem, rsem, fsem, frsem = sems
        x, y, c, peers = _place()
        chip = 2 * x + y
        sib = (x, y, 1 - c)
        own = [_rcopy(ins[a], outs[a].at[chip], lsem.at[a], lrsem.at[a], sib) for a in range(self.n)]
        sends, arrived, passed, passed_in = [], [], [], []
        for a in range(self.n):
            hr = self.rows[a] // 2
            half = lambda who, hc, a=a, hr=hr: outs[a].at[who, pl.ds(hc * hr, hr), :]
            for r, (px, py) in enumerate(peers):
                pc = 2 * px + py
                sends.append(_rcopy(ins[a].at[pl.ds(c * hr, hr), :], half(chip, c), ssem.at[a, r], rsem.at[a, r],
                                    (px, py, c)))
                if with_pass_on:
                    arrived.append(_rcopy(half(pc, c), half(pc, c), ssem.at[a, r], rsem.at[a, r], (px, py, c)))
                    passed.append(_rcopy(half(pc, c), half(pc, c), fsem.at[a, r], frsem.at[a, r], sib))
                    passed_in.append(_rcopy(half(pc, 1 - c), half(pc, 1 - c), fsem.at[a, r], frsem.at[a, r], sib))
        return own, sends, arrived, passed, passed_in

    def start(self, ins, outs, sems):
        own, sends, _, _, _ = self._copies(ins, outs, sems, False)
        for cp in own + sends:
            cp.start()

    def finish(self, ins, outs, sems):
        own, sends, arrived, passed, passed_in = self._copies(ins, outs, sems, True)
        for got, fwd in zip(arrived, passed):
            got.wait_recv()
            fwd.start()
        for cp in passed_in:
            cp.wait_recv()
        for cp in sends + passed:
            cp.wait_send()
        for cp in own:
            cp.wait()


class _ChipExchange:
    def __init__(self, sends):
        self.n = len(sends)
        self.out_shape = [jax.ShapeDtypeStruct(g.shape, g.dtype) for g in sends]
        self.scratch_shapes = [pltpu.SemaphoreType.DMA((self.n, 3))] * 2

    def _copies(self, ins, outs, sems):
        ssem, rsem = sems
        _, _, c, peers = _place()
        return [_rcopy(ins[a].at[r], outs[a].at[r], ssem.at[a, r], rsem.at[a, r], (px, py, c))
                for a in range(self.n) for r, (px, py) in enumerate(peers)]

    def start(self, ins, outs, sems):
        for cp in self._copies(ins, outs, sems):
            cp.start()

    def finish(self, ins, outs, sems):
        for cp in self._copies(ins, outs, sems):
            cp.wait()


def _run_exchange(name, plan, arrays):
    n = plan.n

    def body(*refs):
        ins, outs, sems = refs[:n], refs[n:2 * n], refs[2 * n:]
        plan.start(ins, outs, sems)
        plan.finish(ins, outs, sems)

    return pl.pallas_call(
        body, name=name, in_specs=[ANY] * n, out_specs=[ANY] * n,
        out_shape=plan.out_shape, scratch_shapes=plan.scratch_shapes,
    )(*arrays)


class _PairExchange:
    def __init__(self, gs):
        self.n = len(gs)
        self.out_shape = [jax.ShapeDtypeStruct((g.shape[0],) + g.shape[2:], g.dtype) for g in gs]
        self.scratch_shapes = [pltpu.SemaphoreType.DMA((self.n,))] * 2

    def _copies(self, ins, outs, sems):
        ssem, rsem = sems
        x, y, c, _ = _place()
        return [_rcopy(ins[a].at[:, 1 - c], outs[a], ssem.at[a], rsem.at[a], (x, y, 1 - c)) for a in range(self.n)]

    def start(self, ins, outs, sems):
        for cp in self._copies(ins, outs, sems):
            cp.start()

    def finish(self, ins, outs, sems):
        for cp in self._copies(ins, outs, sems):
            cp.wait()


def _grad_pair_add(name, g, got, idx, tr):
    _, _, hr, cc = g.shape

    def body(idx_ref, g0, g1, g2, g3, r0, r1, r2, r3, keep_ref, send_ref):
        keep_ref[...] = g0[...] + r0[...]
        for q, (gq, rq) in enumerate(((g1, r1), (g2, r2), (g3, r3))):
            send_ref[q] = (gq[...] + rq[...]).astype(BF16)

    gspec = lambda q: pl.BlockSpec((None, None, tr, cc), lambda i, idx: (idx[1 + q], idx[0], i, 0))
    rspec = lambda q: pl.BlockSpec((None, tr, cc), lambda i, idx: (idx[1 + q], i, 0))
    return pl.pallas_call(
        body, name=name,
        grid_spec=pltpu.PrefetchScalarGridSpec(
            num_scalar_prefetch=1, grid=(hr // tr,),
            in_specs=[gspec(q) for q in range(4)] + [rspec(q) for q in range(4)],
            out_specs=[pl.BlockSpec((tr, cc), lambda i, idx: (i, 0)), pl.BlockSpec((3, tr, cc), lambda i, idx: (0, i, 0))]),
        out_shape=[jax.ShapeDtypeStruct((hr, cc), F32), jax.ShapeDtypeStruct((3, hr, cc), BF16)],
        compiler_params=_params(("parallel",)),
    )(idx, g, g, g, g, got, got, got, got)


def _grad_chip_add(name, keep, got, tr):
    hr, cc = keep.shape

    def body(k_ref, g_ref, o_ref):
        o_ref[...] = ((k_ref[...] + g_ref[0].astype(F32)) + g_ref[1].astype(F32)) + g_ref[2].astype(F32)

    return pl.pallas_call(
        body, name=name, grid=(hr // tr,),
        in_specs=[pl.BlockSpec((tr, cc), lambda i: (i, 0)), pl.BlockSpec((3, tr, cc), lambda i: (0, i, 0))],
        out_specs=pl.BlockSpec((tr, cc), lambda i: (i, 0)),
        out_shape=jax.ShapeDtypeStruct((hr, cc), F32),
        compiler_params=_params(("parallel",)),
    )(keep, got)


def _grad_half_exchange(ts):
    n = len(ts)

    def body(*refs):
        ins, outs, ssem, rsem = refs[:n], refs[n:2 * n], refs[2 * n], refs[2 * n + 1]
        x, y, c, _ = _place()
        cps = [_rcopy(ins[a], outs[a], ssem.at[a], rsem.at[a], (x, y, 1 - c)) for a in range(n)]
        for cp in cps:
            cp.start()
        for cp in cps:
            cp.wait()

    return pl.pallas_call(
        body, name="grad_half_exchange",
        in_specs=[ANY] * n, out_specs=[ANY] * n,
        out_shape=[jax.ShapeDtypeStruct(t.shape, t.dtype) for t in ts],
        scratch_shapes=[pltpu.SemaphoreType.DMA((n,))] * 2,
    )(*ts)


def _small_allreduce(stats, d_wp):
    d = D_MODEL
    half = d // 2
    wps = d_wp.shape

    def body(mix_ref, x_ref, mem_ref, ffn_ref, loss_ref, hg_ref, pool_ref, wp_ref, slab_out, wp_out,
             slab_buf, wp_buf, sib_s, sib_w, ssem, rsem):
        x, y, c, peers = _place()
        chip = 2 * x + y
        sib = (x, y, 1 - c)
        hgn = jnp.concatenate([hg_ref[h, 0:1, :] for h in range(HGRN_HEADS)], axis=1)
        dlb = jnp.concatenate([hg_ref[h, 1:2, :] for h in range(HGRN_HEADS)], axis=1)
        slab_buf[0] = jnp.concatenate([
            mix_ref[0:1, :], x_ref[0:1, :], mem_ref[0:1, :], ffn_ref[0:1, :], loss_ref[0:1, :],
            jnp.concatenate([dlb, hgn], axis=1),
            jnp.concatenate([pool_ref[0:1, :], jnp.zeros((1, half), F32)], axis=1),
            loss_ref[1:2, :]], axis=0)
        wp_buf[0] = wp_ref[...]
        pair = [_rcopy(slab_buf.at[0], sib_s, ssem.at[0], rsem.at[0], sib),
                _rcopy(wp_buf.at[0], sib_w, ssem.at[1], rsem.at[1], sib)]
        for cp in pair:
            cp.start()
        for cp in pair:
            cp.wait()
        slab_buf[0] = slab_buf[0] + sib_s[...]
        wp_buf[0] = wp_buf[0] + sib_w[...]
        cps = []
        for r, (px, py) in enumerate(peers):
            cps.append(_rcopy(slab_buf.at[0], slab_buf.at[r + 1], ssem.at[2 + 2 * r], rsem.at[2 + 2 * r], (px, py, c)))
            cps.append(_rcopy(wp_buf.at[0], wp_buf.at[r + 1], ssem.at[3 + 2 * r], rsem.at[3 + 2 * r], (px, py, c)))
        for cp in cps:
            cp.start()
        for cp in cps:
            cp.wait()
        tot_s, tot_w = slab_buf[chip], wp_buf[chip]
        for j in range(1, N_CHIPS):
            tot_s = tot_s + slab_buf[jnp.bitwise_xor(j, chip)]
            tot_w = tot_w + wp_buf[jnp.bitwise_xor(j, chip)]
        slab_out[...] = tot_s
        wp_out[...] = tot_w

    return pl.pallas_call(
        body, name="small_allreduce",
        in_specs=[VMEM] * 8, out_specs=[VMEM] * 2,
        out_shape=[jax.ShapeDtypeStruct((8, d), F32), jax.ShapeDtypeStruct(wps, F32)],
        scratch_shapes=[pltpu.VMEM((N_CHIPS, 8, d), F32), pltpu.VMEM((N_CHIPS,) + wps, F32),
                        pltpu.VMEM((8, d), F32), pltpu.VMEM(wps, F32),
                        pltpu.SemaphoreType.DMA((8,)), pltpu.SemaphoreType.DMA((8,))],
    )(stats["mix"], stats["x"], stats["mem"], stats["ffn"], stats["loss"], stats["hgrn"], stats["pool"], d_wp)


def _adamw_math(w, g, m, v):
    m = ADAM_B1 * m + (1.0 - ADAM_B1) * g
    v = ADAM_B2 * v + (1.0 - ADAM_B2) * (g * g)
    m_hat = m / (1.0 - ADAM_B1 ** ADAM_STEP)
    v_hat = v / (1.0 - ADAM_B2 ** ADAM_STEP)
    delta = -ADAM_LR * (m_hat / (jnp.sqrt(v_hat) + ADAM_EPS) + ADAM_WD * w)
    return delta, m, v


def _adamw(name, mine, theirs, w, m, v, idx, tr):
    rows = w.shape[0]
    cc = mine.shape[1]
    nb = rows // 2 // tr
    heads = w.shape[1] if w.ndim == 3 else 1
    e = cc // heads

    def body(idx_ref, a_ref, b_ref, w_ref, m_ref, v_ref, g_out, d_out, m_out, v_out):
        g = jnp.where(pl.program_id(0) // nb == idx_ref[0], a_ref[...], b_ref[...])
        if w.ndim == 2:
            g_out[...] = g
            d_out[...], m_out[...], v_out[...] = _adamw_math(w_ref[...], g, m_ref[...], v_ref[...])
        else:
            for h in range(heads):
                gh = g[:, h * e:(h + 1) * e]
                g_out[:, h, :] = gh
                d_out[:, h, :], m_out[:, h, :], v_out[:, h, :] = _adamw_math(
                    w_ref[:, h, :], gh, m_ref[:, h, :], v_ref[:, h, :])

    hspec = pl.BlockSpec((tr, cc), lambda i, idx: (i % nb, 0))
    spec = pl.BlockSpec((tr,) + w.shape[1:], lambda i, idx: (i,) + (0,) * (w.ndim - 1))
    return pl.pallas_call(
        body, name=name,
        grid_spec=pltpu.PrefetchScalarGridSpec(
            num_scalar_prefetch=1, grid=(rows // tr,),
            in_specs=[hspec, hspec, spec, spec, spec], out_specs=[spec] * 4),
        out_shape=[jax.ShapeDtypeStruct(w.shape, F32)] * 4,
        compiler_params=_params(("parallel",)),
    )(idx, mine, theirs, w, m, v)


SMALL_NAMES = ("norm_mix_g", "lb_logits", "hgrn_norm_g", "w_pool", "pool_scale", "norm_x_g", "norm_mem_g",
               "norm_ffn_g", "final_norm_g")


def _small_update(slab, d_wp, ws, ms, vs):
    n = len(SMALL_NAMES)
    half = D_MODEL // 2

    def body(slab_ref, wp_ref, *refs):
        w_refs, m_refs, v_refs, outs = refs[:n], refs[n:2 * n], refs[2 * n:3 * n], refs[3 * n:]
        row = lambda k: slab_ref[k:k + 1, :]
        lbl = w_refs[SMALL_NAMES.index("lb_logits")][...]
        s0 = _lower_bound(lbl[0:1, :], lbl[1:2, :])
        dl0 = row(ROW_LB_HGN)[:, :half] * s0 * (1.0 - s0)
        grads = dict(norm_mix_g=row(ROW_GMIX), lb_logits=jnp.concatenate([dl0, -dl0], axis=0),
                     hgrn_norm_g=row(ROW_LB_HGN)[:, half:], w_pool=wp_ref[...], pool_scale=row(ROW_PSCALE)[:, :half],
                     norm_x_g=row(ROW_GX), norm_mem_g=row(ROW_GMEM), norm_ffn_g=row(ROW_GFFN),
                     final_norm_g=row(ROW_GFIN))
        outs[0][...] = row(ROW_LOSS)[:, :128]
        for i, name in enumerate(SMALL_NAMES):
            g = grads[name]
            delta, m2, v2 = _adamw_math(w_refs[i][...], g, m_refs[i][...], v_refs[i][...])
            for o, val in zip(outs[1 + 4 * i:5 + 4 * i], (g, delta, m2, v2)):
                o[...] = val

    args = [ws[k] for k in SMALL_NAMES] + [ms[k] for k in SMALL_NAMES] + [vs[k] for k in SMALL_NAMES]
    out_shape = [jax.ShapeDtypeStruct((1, 128), F32)]
    for k in SMALL_NAMES:
        out_shape += [jax.ShapeDtypeStruct(ws[k].shape, F32)] * 4
    res = pl.pallas_call(
        body, name="small_update",
        in_specs=[VMEM] * (2 + 3 * n), out_specs=[VMEM] * len(out_shape), out_shape=out_shape,
    )(slab, d_wp, *args)
    return res[0], {k: res[1 + 4 * i:5 + 4 * i] for i, k in enumerate(SMALL_NAMES)}


ALL_NAMES = ("norm_mix_g", "w_in", "lb_logits", "hgrn_norm_g", "w_pool", "pool_scale", "w_out", "norm_x_g",
             "norm_mem_g", "w_xq", "w_xk", "w_xv", "w_xo", "norm_ffn_g", "w_ff1", "w_ff2", "final_norm_g")


def _shard_2d(name, a):
    a = a[0]
    if name in ("w_xq", "w_xk", "w_xv"):
        return a.reshape(a.shape[0], -1)
    if name == "w_xo":
        return a.reshape(-1, a.shape[-1])
    return a


def _small_2d(name, a):
    if name == "w_pool":
        return a.reshape(-1, HEAD_DIM)
    if name == "lb_logits":
        return a
    return a.reshape(1, -1)


def kernel(x, mem, norm_mix_g, w_in, lb_logits, hgrn_norm_g, w_pool, pool_scale, w_out, norm_x_g, norm_mem_g, w_xq, w_xk, w_xv, w_xo, norm_ffn_g, w_ff1, w_ff2, final_norm_g, loss_target, m_norm_mix_g, m_w_in, m_lb_logits, m_hgrn_norm_g, m_w_pool, m_pool_scale, m_w_out, m_norm_x_g, m_norm_mem_g, m_w_xq, m_w_xk, m_w_xv, m_w_xo, m_norm_ffn_g, m_w_ff1, m_w_ff2, m_final_norm_g, v_norm_mix_g, v_w_in, v_lb_logits, v_hgrn_norm_g, v_w_pool, v_pool_scale, v_w_out, v_norm_x_g, v_norm_mem_g, v_w_xq, v_w_xk, v_w_xv, v_w_xo, v_norm_ffn_g, v_w_ff1, v_w_ff2, v_final_norm_g):
    w = dict(norm_mix_g=norm_mix_g, w_in=w_in, lb_logits=lb_logits, hgrn_norm_g=hgrn_norm_g, w_pool=w_pool, pool_scale=pool_scale, w_out=w_out, norm_x_g=norm_x_g, norm_mem_g=norm_mem_g, w_xq=w_xq, w_xk=w_xk, w_xv=w_xv, w_xo=w_xo, norm_ffn_g=norm_ffn_g, w_ff1=w_ff1, w_ff2=w_ff2, final_norm_g=final_norm_g)
    m = dict(norm_mix_g=m_norm_mix_g, w_in=m_w_in, lb_logits=m_lb_logits, hgrn_norm_g=m_hgrn_norm_g, w_pool=m_w_pool, pool_scale=m_pool_scale, w_out=m_w_out, norm_x_g=m_norm_x_g, norm_mem_g=m_norm_mem_g, w_xq=m_w_xq, w_xk=m_w_xk, w_xv=m_w_xv, w_xo=m_w_xo, norm_ffn_g=m_norm_ffn_g, w_ff1=m_w_ff1, w_ff2=m_w_ff2, final_norm_g=m_final_norm_g)
    v = dict(norm_mix_g=v_norm_mix_g, w_in=v_w_in, lb_logits=v_lb_logits, hgrn_norm_g=v_hgrn_norm_g, w_pool=v_w_pool, pool_scale=v_pool_scale, w_out=v_w_out, norm_x_g=v_norm_x_g, norm_mem_g=v_norm_mem_g, w_xq=v_w_xq, w_xk=v_w_xk, w_xv=v_w_xv, w_xo=v_w_xo, norm_ffn_g=v_norm_ffn_g, w_ff1=v_w_ff1, w_ff2=v_w_ff2, final_norm_g=v_final_norm_g)

    big_w = {k: _shard_2d(k, w[k]) for k in BIG_NAMES}
    slab = jnp.concatenate([big_w[k] for k in ("w_out", "w_xq", "w_xk", "w_xv", "w_ff1", "w_ff2")], axis=0).astype(BF16)
    shards = dict(slab=slab, w_in=big_w["w_in"].astype(BF16), w_xo=big_w["w_xo"].astype(BF16))

    cx, cy, cc = lax.axis_index("x"), lax.axis_index("y"), lax.axis_index("c")
    chip = 2 * cx + cy
    idx = jnp.stack([cc, chip, chip ^ 1, chip ^ 2, chip ^ 3]).astype(jnp.int32)
    small = {k: w[k] for k in SMALL_NAMES}
    grad_x, stats, d_wp, partials = _step(x[0], mem[0], loss_target[0], small, shards, idx)

    halves = [_grad_chip_add("grad_chip_add_" + k, *partials[k], tr=min(256, partials[k][0].shape[0]))
              for k in BIG_NAMES]
    theirs = _grad_half_exchange(halves)

    grads, deltas, new_m, new_v = {}, {}, {}, {}
    for k, mine, other in zip(BIG_NAMES, halves, theirs):
        as_held = (lambda a: a[0]) if k in ("w_xq", "w_xk", "w_xv") else functools.partial(_shard_2d, k)
        res = _adamw("adamw_" + k, mine, other, as_held(w[k]), as_held(m[k]), as_held(v[k]), idx,
                     tr=min(256, mine.shape[0]))
        for store, val in zip((grads, deltas, new_m, new_v), res):
            store[k] = val.reshape(w[k].shape)

    slab_sum, wp_sum = _small_allreduce(stats, d_wp.reshape(-1, HEAD_DIM))
    loss, upd = _small_update(slab_sum, wp_sum, {k: _small_2d(k, w[k]) for k in SMALL_NAMES},
                              {k: _small_2d(k, m[k]) for k in SMALL_NAMES}, {k: _small_2d(k, v[k]) for k in SMALL_NAMES})
    for k in SMALL_NAMES:
        for store, val in zip((grads, deltas, new_m, new_v), upd[k]):
            store[k] = val.reshape(w[k].shape)

    return (loss[0, 0], grad_x[None], *[grads[k] for k in ALL_NAMES], *[deltas[k] for k in ALL_NAMES],
            *[new_m[k] for k in ALL_NAMES], *[new_v[k] for k in ALL_NAMES])
```

```python
import functools

import jax
import jax.numpy as jnp
from jax import lax
from jax.experimental import pallas as pl
from jax.experimental.pallas import tpu as pltpu

F32 = jnp.float32
BF16 = jnp.bfloat16
LOG2E = 1.4426950408889634
NEG_BIG = -1e30
MAX_LOG2_GROWTH = 100.0
MESH = pl.DeviceIdType.MESH
ANY = pl.BlockSpec(memory_space=pl.ANY)
VMEM = pl.BlockSpec(memory_space=pltpu.VMEM)

D_MODEL = 1024
N_CHIPS = 4
HGRN_HEADS = 4
HEAD_DIM = 128
HGRN_WIDTH = HGRN_HEADS * HEAD_DIM
POOL_WINDOWS = (2, 4, 8, 16)
POOL_HALO = 16
SUB = 16
HALF = SUB // 2
CHUNK = 64
XATTN_HEADS = 4
XATTN_HEAD_DIM = 256
EPS = 1e-6
ADAM_LR, ADAM_B1, ADAM_B2, ADAM_EPS, ADAM_WD, ADAM_STEP = 0.001, 0.9, 0.999, 1e-08, 0.01, 10

V7X_VMEM_BYTES = 64 * 1024 * 1024
VMEM_LIMIT = V7X_VMEM_BYTES - 8 * 1024 * 1024

NN = (((1,), (0,)), ((), ()))
NT = (((1,), (1,)), ((), ()))
TN = (((0,), (0,)), ((), ()))

ROW_GMIX, ROW_GX, ROW_GMEM, ROW_GFFN, ROW_GFIN, ROW_LB_HGN, ROW_PSCALE, ROW_LOSS = range(8)


def _dot(a, b, dims=NN):
    return lax.dot_general(a, b, dims, preferred_element_type=F32)


def _sigmoid(x):
    return 1.0 / (1.0 + jnp.exp(-x))


def _rms_fwd(x, g):
    r = lax.rsqrt(jnp.mean(x * x, axis=-1, keepdims=True) + EPS)
    n = x * r
    return n * g, n, r


def _rms_bwd(dh, n, r, g):
    dn = dh * g
    dx = r * (dn - n * jnp.mean(dn * n, axis=-1, keepdims=True))
    return dx, jnp.sum(dh * n, axis=0, keepdims=True)


def _params(sem=None):
    return pltpu.CompilerParams(dimension_semantics=sem, vmem_limit_bytes=VMEM_LIMIT)


def _const(shape):
    nd = len(shape)
    return pl.BlockSpec(shape, lambda *_: (0,) * nd, pipeline_mode=pl.Buffered(1))


def _const_out(shape):
    nd = len(shape)
    return pl.BlockSpec(shape, lambda *_: (0,) * nd)


def _acc_rows(ref, t, rows):
    upd = jnp.concatenate(rows + [jnp.zeros((8 - len(rows), rows[0].shape[1]), F32)], axis=0)

    @pl.when(t == 0)
    def _():
        ref[...] = upd

    @pl.when(t > 0)
    def _():
        ref[...] = ref[...] + upd


def _fuse_exchange(body, n_in, n_out, n_scratch, plan, ndim):
    if plan is None:
        return body
    n = plan.n

    def wrapped(*refs):
        ins, cin = refs[:n_in], refs[n_in:n_in + n]
        outs, cout = refs[n_in + n:n_in + n + n_out], refs[n_in + n + n_out:n_in + 2 * n + n_out]
        rest = refs[n_in + 2 * n + n_out:]
        scr, csem = rest[:n_scratch], rest[n_scratch:]
        first = pl.program_id(0) == 0
        last = pl.program_id(0) == pl.num_programs(0) - 1
        for i in range(1, ndim):
            first = first & (pl.program_id(i) == 0)
            last = last & (pl.program_id(i) == pl.num_programs(i) - 1)

        @pl.when(first)
        def _():
            plan.start(cin, cout, csem)

        body(*ins, *outs, *scr)

        @pl.when(last)
        def _():
            plan.finish(cin, cout, csem)

    return wrapped


def _plan_extras(plan):
    if plan is None:
        return [], [], []
    return [ANY] * plan.n, list(plan.out_shape), list(plan.scratch_shapes)


def _in_proj(x, g, win_g, tm, plan=None, plan_args=()):
    s, d = x.shape
    nsh, _, wc = win_g.shape

    def body(x_ref, g_ref, w_ref, z_ref, h_ref):
        h, _, _ = _rms_fwd(x_ref[...], g_ref[...])
        hb = h.astype(BF16)
        h_ref[...] = hb
        for j in range(nsh):
            z_ref[:, j * wc:(j + 1) * wc] = _dot(hb, w_ref[j])

    x_specs, x_shapes, x_scratch = _plan_extras(plan)
    return pl.pallas_call(
        _fuse_exchange(body, 3, 2, 0, plan, 1), name="in_proj", grid=(s // tm,),
        in_specs=[pl.BlockSpec((tm, d), lambda t: (t, 0)), _const((1, d)), _const((nsh, d, wc))] + x_specs,
        out_specs=[pl.BlockSpec((tm, nsh * wc), lambda t: (t, 0)), pl.BlockSpec((tm, d), lambda t: (t, 0))] + x_specs,
        out_shape=[jax.ShapeDtypeStruct((s, nsh * wc), F32), jax.ShapeDtypeStruct((s, d), BF16)] + x_shapes,
        scratch_shapes=x_scratch,
        compiler_params=_params(("arbitrary",)),
    )(x, g, win_g, *plan_args)


def _lower_bound(l0, l1):
    m = jnp.maximum(l0, l1)
    e0, e1 = jnp.exp(l0 - m), jnp.exp(l1 - m)
    return e0 / (e0 + e1)


def _block_tri(n, group, upper):
    r = lax.broadcasted_iota(jnp.int32, (n, n), 0)
    c = lax.broadcasted_iota(jnp.int32, (n, n), 1)
    keep = (r // group == c // group) & ((c >= r) if upper else (c <= r))
    return keep.astype(BF16)


def _group_cumsum(tri, x):
    hi = x.astype(BF16)
    rest = x - hi.astype(F32)
    mid = rest.astype(BF16)
    lo = (rest - mid.astype(F32)).astype(BF16)
    return (_dot(tri, hi) + _dot(tri, mid)) + _dot(tri, lo)


def _decay(b, bj, rows, first):
    d = b - bj
    if first:
        d = jnp.where(rows >= first, d, NEG_BIG)
    return jnp.exp2(d)


class _RowSums:
    ORDER = (0, 4, 2, 6, 1, 5, 3, 7)

    def __init__(self, rows):
        self.rows = rows
        self.level = {4: {}, 2: {}, 1: {}}

    def _pair(self, p, q, d):
        return jnp.where((self.rows & d) != 0, p + pltpu.roll(p, d, axis=0), q + pltpu.roll(q, HALF - d, axis=0))

    def push(self, j, y, d=4):
        if d == 0:
            self.out = y
            return
        slot = self.level[d]
        key = j % d
        if key not in slot:
            slot[key] = (j, y)
            return
        j0, y0 = slot.pop(key)
        p, q = (y, y0) if j & d else (y0, y)
        self.push(key, self._pair(p, q, d), d // 2)

    def result(self):
        return self.out


def _hgrn_gates(qp, fp, lb):
    sq = _sigmoid(qp)
    sf = _sigmoid(fp)
    f = lb + (1.0 - lb) * sf
    return qp * sq, sq, f, sf


def _hgrn_fwd(z, l0, l1, gn, tc, unroll=1, plan=None, plan_args=()):
    s = z.shape[0]
    nsub = tc // SUB
    hd = HEAD_DIM

    def body(q_ref, f_ref, v_ref, g_ref, l0_ref, l1_ref, gn_ref, tri_ref, tric_ref, o_ref, oa_ref, st_ref,
             state, qs, ks, bs, os_):
        @pl.when(pl.program_id(1) == 0)
        def _():
            state[...] = jnp.zeros_like(state)

        lb = _lower_bound(l0_ref[...], l1_ref[...])
        q, _, f, _ = _hgrn_gates(q_ref[...], f_ref[...], lb)
        k = 1.0 - f
        lf = jnp.log(f) * LOG2E
        bc = _group_cumsum(tric_ref[...], lf)
        bounded = jnp.min(bc) >= -MAX_LOG2_GROWTH

        @pl.when(bounded)
        def _():
            qt = (q * jnp.exp2(bc)).astype(BF16)
            ki = (k * jnp.exp2(-bc)).astype(BF16)
            vb = v_ref[...].astype(BF16)
            a = jnp.where(tric_ref[...] > 0, _dot(qt, ki, NT), 0.0).astype(BF16)
            o_in = _dot(a, vb)
            for c in range(tc // CHUNK):
                rs = slice(c * CHUNK, (c + 1) * CHUNK)
                st = state[...]
                st_ref[c] = st
                os_[rs, :] = o_in[rs] + _dot(qt[rs], st.astype(BF16), NT)
                bl = bc[(c + 1) * CHUNK - 1:(c + 1) * CHUNK, :]
                kt = (k[rs] * jnp.exp2(bl - bc[rs])).astype(BF16)
                state[...] = st * jnp.exp2(bl) + _dot(vb[rs], kt, TN)

        @pl.when(jnp.logical_not(bounded))
        def _():
            qs[...] = q
            ks[...] = k
            bs[...] = _group_cumsum(tri_ref[...], lf)
            rows = lax.broadcasted_iota(jnp.int32, (HALF, 1), 0)

            def step(i, carry):
                r0 = pl.multiple_of(i * SUB, SUB)
                q_ = qs[pl.ds(r0, SUB), :]
                k_ = ks[pl.ds(r0, SUB), :]
                b_ = bs[pl.ds(r0, SUB), :]
                v_ = v_ref[pl.ds(r0, SUB), :]
                st = state[...]

                @pl.when(i % (CHUNK // SUB) == 0)
                def _():
                    st_ref[i // (CHUNK // SUB)] = st

                bl = b_[SUB - 1:SUB, :]
                o = _dot((q_ * jnp.exp2(b_)).astype(BF16), st.astype(BF16), NT)
                (q_lo, q_hi), (b_lo, b_hi), (o_lo, o_hi) = ((x[:HALF], x[HALF:]) for x in (q_, b_, o))
                for j in range(SUB):
                    bj, kj, vj = b_[j:j + 1, :], k_[j:j + 1, :], v_[j:j + 1, :]
                    if j < HALF:
                        e = _decay(b_lo, bj, rows, j)
                        o_lo = o_lo + jnp.sum(q_lo * e * kj, axis=-1, keepdims=True) * vj
                    e = _decay(b_hi, bj, rows, j - HALF if j > HALF else None)
                    o_hi = o_hi + jnp.sum(q_hi * e * kj, axis=-1, keepdims=True) * vj
                os_[pl.ds(r0, HALF), :] = o_lo
                os_[pl.ds(r0 + HALF, HALF), :] = o_hi
                kt = (k_ * jnp.exp2(bl - b_)).astype(BF16)
                state[...] = st * jnp.exp2(bl) + _dot(v_.astype(BF16), kt, TN)
                return carry

            lax.fori_loop(0, nsub, step, 0, unroll=unroll)

        o = os_[...]
        o_ref[...] = o
        r = lax.rsqrt(jnp.mean(o * o, axis=-1, keepdims=True) + EPS)
        gp = g_ref[...]
        oa_ref[...] = (o * r * gn_ref[...] * (gp * _sigmoid(gp))).astype(BF16)

    col = lambda k: pl.BlockSpec((tc, hd), lambda h, t: (t, k * HGRN_HEADS + h))
    vec = pl.BlockSpec((None, 1, hd), lambda h, t: (h, 0, 0))
    x_specs, x_shapes, x_scratch = _plan_extras(plan)
    return pl.pallas_call(
        _fuse_exchange(body, 9, 3, 5, plan, 2), name="hgrn_fwd", grid=(HGRN_HEADS, s // tc),
        in_specs=[col(0), col(1), col(2), col(3), vec, vec, vec, _const((tc, tc)), _const((tc, tc))] + x_specs,
        out_specs=[pl.BlockSpec((tc, hd), lambda h, t: (t, h)), pl.BlockSpec((tc, hd), lambda h, t: (t, h)),
                   pl.BlockSpec((None, tc // CHUNK, hd, hd), lambda h, t: (h, t, 0, 0))] + x_specs,
        out_shape=[jax.ShapeDtypeStruct((s, HGRN_WIDTH), F32), jax.ShapeDtypeStruct((s, HGRN_WIDTH), BF16),
                   jax.ShapeDtypeStruct((HGRN_HEADS, s // CHUNK, hd, hd), F32)] + x_shapes,
        scratch_shapes=[pltpu.VMEM((hd, hd), F32)] + [pltpu.VMEM((tc, hd), F32)] * 4 + x_scratch,
        compiler_params=_params(("arbitrary", "arbitrary")),
    )(z, z, z, z, l0, l1, gn, _block_tri(tc, SUB, False), _block_tri(tc, CHUNK, False), *plan_args)


def _pooled(p, ext, tok0):
    tm = p.shape[0]
    tok = tok0 + lax.broadcasted_iota(jnp.int32, (tm, 1), 0)
    outs = []
    for g, w in enumerate(POOL_WINDOWS):
        acc = ext[:, g * HEAD_DIM:(g + 1) * HEAD_DIM]
        sh = 1
        while sh < w:
            acc = acc + pltpu.roll(acc, sh, axis=0)
            sh *= 2
        cnt = jnp.minimum(tok + 1, w).astype(F32)
        outs.append(acc[POOL_HALO:, :] / cnt - p[:, g * HEAD_DIM:(g + 1) * HEAD_DIM])
    return outs


def _pool_fwd(z, wp, scale, tm):
    s = z.shape[0]
    pw = len(POOL_WINDOWS) * HEAD_DIM
    nb = tm // POOL_HALO

    def body(p_ref, prev_ref, wp_ref, sc_ref, ob_ref):
        t = pl.program_id(0)
        p = p_ref[...]
        prev = jnp.where(t > 0, prev_ref[...], 0.0)
        pooled = _pooled(p, jnp.concatenate([prev, p], axis=0), t * tm)
        ys = [_dot(pooled[g].astype(BF16), wp_ref[g].astype(BF16)) for g in range(len(POOL_WINDOWS))]
        ob_ref[...] = (jnp.concatenate(ys, axis=1) * sc_ref[...]).astype(BF16)

    return pl.pallas_call(
        body, name="pool_fwd", grid=(s // tm,),
        in_specs=[pl.BlockSpec((tm, pw), lambda t: (t, 4)),
                  pl.BlockSpec((POOL_HALO, pw), lambda t: (jnp.maximum(t * nb - 1, 0), 4)),
                  _const(wp.shape), _const((1, pw))],
        out_specs=pl.BlockSpec((tm, pw), lambda t: (t, 0)),
        out_shape=jax.ShapeDtypeStruct((s, pw), BF16),
        compiler_params=_params(("parallel",)),
    )(z, z, wp, scale)


def _kv_proj(mem, g, slab_g):
    m, d = mem.shape
    rows = d // N_CHIPS

    def body(mem_ref, g_ref, wk_ref, wv_ref, xk_ref, xv_ref):
        hm, _, _ = _rms_fwd(mem_ref[...], g_ref[...])
        hb = hm.astype(BF16)
        xk_ref[...] = _dot(hb, wk_ref[...].reshape(d, d)).astype(BF16)
        xv_ref[...] = _dot(hb, wv_ref[...].reshape(d, d)).astype(BF16)

    blk = lambda k: pl.BlockSpec((N_CHIPS, rows, d), lambda i: (0, k, 0))
    return pl.pallas_call(
        body, name="kv_proj", grid=(1,),
        in_specs=[_const((m, d)), _const((1, d)), blk(2), blk(3)],
        out_specs=[_const_out((m, d)), _const_out((m, d))],
        out_shape=[jax.ShapeDtypeStruct((m, d), BF16)] * 2,
        compiler_params=_params(("arbitrary",)),
    )(mem, g, slab_g, slab_g)


def _softmax_rows(sc):
    e = jnp.exp(sc - jnp.max(sc, axis=-1, keepdims=True))
    return e / jnp.sum(e, axis=-1, keepdims=True)


def _mix_xattn_fwd(x, oa, ob, gx, slab_g, wo_g, xk, xv, tm):
    s, d = x.shape
    m = xk.shape[0]
    rows = d // N_CHIPS
    hw = oa.shape[1]
    e = XATTN_HEAD_DIM

    def body(x_ref, oa_ref, ob_ref, gx_ref, wout_ref, wq_ref, wo_ref, xk_ref, xv_ref,
             x1_ref, mixed_ref, hq_ref, xq_ref, att_ref, x2_ref):
        mixed = jnp.concatenate([oa_ref[...], ob_ref[...]], axis=1)
        mixed_ref[...] = mixed
        x1 = x_ref[...] + _dot(mixed, wout_ref[...].reshape(d, d))
        x1_ref[...] = x1
        hq, _, _ = _rms_fwd(x1, gx_ref[...])
        hqb = hq.astype(BF16)
        hq_ref[...] = hqb
        xq = _dot(hqb, wq_ref[...].reshape(d, d)).astype(BF16)
        xq_ref[...] = xq
        atts = []
        for h in range(XATTN_HEADS):
            cs = slice(h * e, (h + 1) * e)
            p = _softmax_rows(_dot(xq[:, cs], xk_ref[:, cs], NT) * (e ** -0.5))
            atts.append(_dot(p.astype(BF16), xv_ref[:, cs]).astype(BF16))
        att = jnp.concatenate(atts, axis=1)
        att_ref[...] = att
        for j in range(N_CHIPS):
            x2_ref[:, j * rows:(j + 1) * rows] = x1[:, j * rows:(j + 1) * rows] + _dot(att, wo_ref[j])

    tile = lambda w: pl.BlockSpec((tm, w), lambda t: (t, 0))
    blk = lambda k: pl.BlockSpec((N_CHIPS, rows, d), lambda t: (0, k, 0), pipeline_mode=pl.Buffered(1))
    return pl.pallas_call(
        body, name="mix_xattn_fwd", grid=(s // tm,),
        in_specs=[tile(d), tile(hw), tile(hw), _const((1, d)), blk(0), blk(1), _const(wo_g.shape),
                  _const((m, d)), _const((m, d))],
        out_specs=[tile(d)] * 6,
        out_shape=[jax.ShapeDtypeStruct((s, d), F32)] + [jax.ShapeDtypeStruct((s, d), BF16)] * 4
                  + [jax.ShapeDtypeStruct((s, d), F32)],
        compiler_params=_params(("parallel",)),
    )(x, oa, ob, gx, slab_g, slab_g, wo_g, xk, xv)


def _mlp_loss_fwd(x2, gffn, gfin, slab_g, target, tm):
    s, d = x2.shape
    wr = slab_g.shape[1] // 2

    def body(x2_ref, gffn_ref, gfin_ref, w1_ref, w2_ref, tg_ref, a_ref, hf_ref, dx3_ref, dx3b_ref, st_ref):
        x2v = x2_ref[...]
        hf, _, _ = _rms_fwd(x2v, gffn_ref[...])
        hfb = hf.astype(BF16)
        hf_ref[...] = hfb
        acc = x2v
        for j in range(N_CHIPS):
            a = _dot(hfb, w1_ref[j])
            a_ref[:, j * wr:(j + 1) * wr] = a
            r = jnp.maximum(a, 0.0)
            acc = acc + _dot((r * r).astype(BF16), w2_ref[j])
        gf = gfin_ref[...]
        y, n, r3 = _rms_fwd(acc, gf)
        err = y - tg_ref[...]
        loss = 0.5 * jnp.sum(jnp.sum(err * err, axis=-1, keepdims=True) * (1.0 / d), axis=0, keepdims=True)
        dy = err * (1.0 / d)
        dx3, dgf = _rms_bwd(dy, n, r3, gf)
        dx3_ref[...] = dx3
        dx3b_ref[...] = dx3.astype(BF16)
        _acc_rows(st_ref, pl.program_id(0), [dgf, jnp.broadcast_to(loss, (1, d))])

    tile = lambda w: pl.BlockSpec((tm, w), lambda t: (t, 0))
    blk = lambda k: pl.BlockSpec((N_CHIPS, wr, d), lambda t: (0, k, 0), pipeline_mode=pl.Buffered(1))
    return pl.pallas_call(
        body, name="mlp_loss_fwd", grid=(s // tm,),
        in_specs=[tile(d), _const((1, d)), _const((1, d)), blk(0), blk(1), tile(d)],
        out_specs=[tile(N_CHIPS * wr), tile(d), tile(d), tile(d), _const_out((8, d))],
        out_shape=[jax.ShapeDtypeStruct((s, N_CHIPS * wr), F32), jax.ShapeDtypeStruct((s, d), BF16),
                   jax.ShapeDtypeStruct((s, d), F32), jax.ShapeDtypeStruct((s, d), BF16),
                   jax.ShapeDtypeStruct((8, d), F32)],
        compiler_params=_params(("arbitrary",)),
    )(x2, gffn, gfin, slab_g, slab_g, target)


def _mlp_bwd(dx3, dx3b, a, x2, gffn, slab_g, tm):
    s, d = x2.shape
    wr = slab_g.shape[1] // 2

    def body(dx3_ref, dx3b_ref, a_ref, x2_ref, g_ref, w1_ref, w2_ref, da_ref, u_ref, dx2_ref, dx2b_ref, st_ref):
        dyb = dx3b_ref[...]
        dhf = jnp.zeros((tm, d), F32)
        for j in range(N_CHIPS):
            r = jnp.maximum(a_ref[:, j * wr:(j + 1) * wr], 0.0)
            da = (_dot(dyb, w2_ref[j], NT) * (2.0 * r)).astype(BF16)
            da_ref[:, j * wr:(j + 1) * wr] = da
            u_ref[:, j * wr:(j + 1) * wr] = (r * r).astype(BF16)
            dhf = dhf + _dot(da, w1_ref[j], NT)
        g = g_ref[...]
        _, n, r2 = _rms_fwd(x2_ref[...], g)
        dxn, dg = _rms_bwd(dhf, n, r2, g)
        dx2 = dx3_ref[...] + dxn
        dx2_ref[...] = dx2
        dx2b_ref[...] = dx2.astype(BF16)
        _acc_rows(st_ref, pl.program_id(0), [dg])

    tile = lambda w: pl.BlockSpec((tm, w), lambda t: (t, 0))
    blk = lambda k: pl.BlockSpec((N_CHIPS, wr, d), lambda t: (0, k, 0), pipeline_mode=pl.Buffered(1))
    nf = N_CHIPS * wr
    return pl.pallas_call(
        body, name="mlp_bwd", grid=(s // tm,),
        in_specs=[tile(d), tile(d), tile(nf), tile(d), _const((1, d)), blk(0), blk(1)],
        out_specs=[tile(nf), tile(nf), tile(d), tile(d), _const_out((8, d))],
        out_shape=[jax.ShapeDtypeStruct((s, nf), BF16), jax.ShapeDtypeStruct((s, nf), BF16),
                   jax.ShapeDtypeStruct((s, d), F32), jax.ShapeDtypeStruct((s, d), BF16),
                   jax.ShapeDtypeStruct((8, d), F32)],
        compiler_params=_params(("arbitrary",)),
    )(dx3, dx3b, a, x2, gffn, slab_g, slab_g)


def _xattn_mix_bwd(dx2, x1, xq, xk, xv, gx, slab_g, wo_g, tm, plan=None, plan_args=()):
    s, d = x1.shape
    m = xk.shape[0]
    rows = d // N_CHIPS
    e = XATTN_HEAD_DIM

    def body(dx2_ref, x1_ref, xq_ref, xk_ref, xv_ref, gx_ref, wout_ref, wq_ref, wo_ref,
             dx1_ref, dx1b_ref, dxq_ref, dmix_ref, dxk_ref, dxv_ref, st_ref):
        t = pl.program_id(0)
        dx2 = dx2_ref[...]
        dx2b = dx2.astype(BF16)
        datt = jnp.zeros((tm, d), F32)
        for j in range(N_CHIPS):
            datt = datt + _dot(dx2b[:, j * rows:(j + 1) * rows], wo_ref[j], NT)
        dattb = datt.astype(BF16)
        dxqs, dxks, dxvs = [], [], []
        for h in range(XATTN_HEADS):
            cs = slice(h * e, (h + 1) * e)
            xq_h, xk_h, xv_h = xq_ref[:, cs], xk_ref[:, cs], xv_ref[:, cs]
            p = _softmax_rows(_dot(xq_h, xk_h, NT) * (e ** -0.5))
            dp = _dot(dattb[:, cs], xv_h, NT)
            ds = (p * (dp - jnp.sum(dp * p, axis=-1, keepdims=True)) * (e ** -0.5)).astype(BF16)
            dxqs.append(_dot(ds, xk_h).astype(BF16))
            dxks.append(_dot(ds, xq_h, TN))
            dxvs.append(_dot(p.astype(BF16), dattb[:, cs], TN))
        dxq = jnp.concatenate(dxqs, axis=1)
        dxq_ref[...] = dxq
        dxk = jnp.concatenate(dxks, axis=1)
        dxv = jnp.concatenate(dxvs, axis=1)

        @pl.when(t == 0)
        def _():
            dxk_ref[...] = dxk
            dxv_ref[...] = dxv

        @pl.when(t > 0)
        def _():
            dxk_ref[...] = dxk_ref[...] + dxk
            dxv_ref[...] = dxv_ref[...] + dxv

        dhq = jnp.concatenate([_dot(dxq, wq_ref[j], NT) for j in range(N_CHIPS)], axis=1)
        g = gx_ref[...]
        _, n, r1 = _rms_fwd(x1_ref[...], g)
        dxn, dg = _rms_bwd(dhq, n, r1, g)
        dx1 = dx2 + dxn
        dx1_ref[...] = dx1
        dx1b = dx1.astype(BF16)
        dx1b_ref[...] = dx1b
        for j in range(N_CHIPS):
            dmix_ref[:, j * rows:(j + 1) * rows] = _dot(dx1b, wout_ref[j], NT)
        _acc_rows(st_ref, t, [dg])

    tile = lambda: pl.BlockSpec((tm, d), lambda t: (t, 0))
    blk = lambda k: pl.BlockSpec((N_CHIPS, rows, d), lambda t: (0, k, 0), pipeline_mode=pl.Buffered(1))
    x_specs, x_shapes, x_scratch = _plan_extras(plan)
    return pl.pallas_call(
        _fuse_exchange(body, 9, 7, 0, plan, 1), name="xattn_mix_bwd", grid=(s // tm,),
        in_specs=[tile(), tile(), tile(), _const((m, d)), _const((m, d)), _const((1, d)), blk(0), blk(1),
                  _const(wo_g.shape)] + x_specs,
        out_specs=[tile(), tile(), tile(), tile(), _const_out((m, d)), _const_out((m, d)), _const_out((8, d))]
                  + x_specs,
        out_shape=[jax.ShapeDtypeStruct((s, d), F32), jax.ShapeDtypeStruct((s, d), BF16),
                   jax.ShapeDtypeStruct((s, d), BF16), jax.ShapeDtypeStruct((s, d), F32),
                   jax.ShapeDtypeStruct((m, d), F32), jax.ShapeDtypeStruct((m, d), F32),
                   jax.ShapeDtypeStruct((8, d), F32)] + x_shapes,
        scratch_shapes=x_scratch,
        compiler_params=_params(("arbitrary",)),
    )(dx2, x1, xq, xk, xv, gx, slab_g, slab_g, wo_g, *plan_args)


def _kv_bwd(mem, g, dxk, dxv, slab_g):
    m, d = mem.shape
    rows = d // N_CHIPS

    def body(mem_ref, g_ref, dxk_ref, dxv_ref, wk_ref, wv_ref, dwk_ref, dwv_ref, st_ref):
        gv = g_ref[...]
        hm, n, _ = _rms_fwd(mem_ref[...], gv)
        hb = hm.astype(BF16)
        dkb = dxk_ref[...].astype(BF16)
        dvb = dxv_ref[...].astype(BF16)
        dhm = []
        for j in range(N_CHIPS):
            hj = hb[:, j * rows:(j + 1) * rows]
            dwk_ref[j] = _dot(hj, dkb, TN)
            dwv_ref[j] = _dot(hj, dvb, TN)
            dhm.append(_dot(dkb, wk_ref[j], NT) + _dot(dvb, wv_ref[j], NT))
        dg = jnp.sum(jnp.concatenate(dhm, axis=1) * n, axis=0, keepdims=True)
        st_ref[...] = jnp.concatenate([dg, jnp.zeros((7, d), F32)], axis=0)

    blk = lambda k: pl.BlockSpec((N_CHIPS, rows, d), lambda i: (0, k, 0))
    return pl.pallas_call(
        body, name="kv_bwd", grid=(1,),
        in_specs=[_const((m, d)), _const((1, d)), _const((m, d)), _const((m, d)), blk(2), blk(3)],
        out_specs=[_const_out((N_CHIPS, rows, d)), _const_out((N_CHIPS, rows, d)), _const_out((8, d))],
        out_shape=[jax.ShapeDtypeStruct((N_CHIPS, rows, d), F32)] * 2 + [jax.ShapeDtypeStruct((8, d), F32)],
        compiler_params=_params(("arbitrary",)),
    )(mem, g, dxk, dxv, slab_g, slab_g)


def _pool_bwd(z, dmix, wp, scale, tm, plan=None, plan_args=()):
    s = z.shape[0]
    ng = len(POOL_WINDOWS)
    pw = ng * HEAD_DIM
    nb = tm // POOL_HALO
    nt = s // tm
    n_ext = tm + POOL_HALO

    def body(p_ref, prev_ref, dm_ref, dmn_ref, wp_ref, sc_ref, dp_ref, dwp_ref, st_ref):
        t = pl.program_id(0)
        p = p_ref[...]
        prev = jnp.where(t > 0, prev_ref[...], 0.0)
        pooled = _pooled(p, jnp.concatenate([prev, p], axis=0), t * tm)
        dm = dm_ref[...]
        dme = jnp.concatenate([dm, jnp.where(t < nt - 1, dmn_ref[...], 0.0)], axis=0) * sc_ref[...]
        tok = t * tm + lax.broadcasted_iota(jnp.int32, (n_ext, 1), 0)
        dsc, dps, dwps = [], [], []
        for g, w in enumerate(POOL_WINDOWS):
            cs = slice(g * HEAD_DIM, (g + 1) * HEAD_DIM)
            wpb = wp_ref[g].astype(BF16)
            pb = pooled[g].astype(BF16)
            dsc.append(jnp.sum(dm[:, cs] * _dot(pb, wpb), axis=0, keepdims=True))
            dye = dme[:, cs].astype(BF16)
            dwps.append(_dot(pb, dye[:tm], TN))
            dpe = _dot(dye, wpb, NT)
            acc = dpe / jnp.minimum(tok + 1, w).astype(F32)
            sh = 1
            while sh < w:
                acc = acc + pltpu.roll(acc, n_ext - sh, axis=0)
                sh *= 2
            dps.append(acc[:tm] - dpe[:tm])
        dp_ref[...] = jnp.concatenate(dps, axis=1)
        dsc_row = jnp.concatenate(dsc, axis=1)

        @pl.when(t == 0)
        def _():
            for g in range(ng):
                dwp_ref[g] = dwps[g]

        @pl.when(t > 0)
        def _():
            for g in range(ng):
                dwp_ref[g] = dwp_ref[g] + dwps[g]

        _acc_rows(st_ref, t, [dsc_row])

    x_specs, x_shapes, x_scratch = _plan_extras(plan)
    return pl.pallas_call(
        _fuse_exchange(body, 6, 3, 0, plan, 1), name="pool_bwd", grid=(nt,),
        in_specs=[pl.BlockSpec((tm, pw), lambda t: (t, 4)),
                  pl.BlockSpec((POOL_HALO, pw), lambda t: (jnp.maximum(t * nb - 1, 0), 4)),
                  pl.BlockSpec((tm, pw), lambda t: (t, 1)),
                  pl.BlockSpec((POOL_HALO, pw), lambda t: (jnp.minimum((t + 1) * nb, s // POOL_HALO - 1), 1)),
                  _const(wp.shape), _const((1, pw))] + x_specs,
        out_specs=[pl.BlockSpec((tm, pw), lambda t: (t, 0)), _const_out(wp.shape), _const_out((8, pw))] + x_specs,
        out_shape=[jax.ShapeDtypeStruct((s, pw), F32), jax.ShapeDtypeStruct(wp.shape, F32),
                   jax.ShapeDtypeStruct((8, pw), F32)] + x_shapes,
        scratch_shapes=x_scratch,
        compiler_params=_params(("arbitrary",)),
    )(z, z, dmix, dmix, wp, scale, *plan_args)


def _hgrn_bwd(z, o, dmix, st, l0, l1, gn, tc, unroll=1, plan=None, plan_args=()):
    s = z.shape[0]
    nsub = tc // SUB
    nt = s // tc
    hd = HEAD_DIM

    def body(q_ref, f_ref, v_ref, g_ref, l0_ref, l1_ref, gn_ref, o_ref, dm_ref, st_ref,
             tril_ref, triu_ref, trilc_ref, triuc_ref,
             dq_ref, df_ref, di_ref, dg_ref, stat_ref, dstate, qs, ks, bs, dos, dqs, dks, dbs, sts):
        t = pl.program_id(1)

        @pl.when(t == 0)
        def _():
            dstate[...] = jnp.zeros_like(dstate)

        lb = _lower_bound(l0_ref[...], l1_ref[...])
        qp = q_ref[...]
        q, sq, f, sf = _hgrn_gates(qp, f_ref[...], lb)
        k = 1.0 - f
        lf = jnp.log(f) * LOG2E
        bc = _group_cumsum(trilc_ref[...], lf)
        bounded = jnp.min(bc) >= -MAX_LOG2_GROWTH

        o = o_ref[...]
        r = lax.rsqrt(jnp.mean(o * o, axis=-1, keepdims=True) + EPS)
        n = o * r
        gnv = gn_ref[...]
        gp = g_ref[...]
        sg = _sigmoid(gp)
        dm = dm_ref[...]
        dg_ref[...] = dm * (n * gnv) * (sg * (1.0 + gp * (1.0 - sg)))
        don = dm * (gp * sg)
        dgn = jnp.sum(don * n, axis=0, keepdims=True)
        dn = don * gnv
        do_all = r * (dn - n * jnp.mean(dn * n, axis=-1, keepdims=True))

        @pl.when(bounded)
        def _():
            eb = jnp.exp2(bc)
            eib = jnp.exp2(-bc)
            qt = (q * eb).astype(BF16)
            ki = (k * eib).astype(BF16)
            vb = v_ref[...].astype(BF16)
            dob = do_all.astype(BF16)
            mask = trilc_ref[...] > 0
            a = jnp.where(mask, _dot(qt, ki, NT), 0.0).astype(BF16)
            da = jnp.where(mask, _dot(dob, vb, NT), 0.0).astype(BF16)
            dq_in = _dot(da, ki)
            dk_in = _dot(da, qt, TN)
            dv_in = _dot(a, dob, TN)
            last_row = lax.broadcasted_iota(jnp.int32, (CHUNK, 1), 0) == CHUNK - 1
            for c in reversed(range(tc // CHUNK)):
                rs = slice(c * CHUNK, (c + 1) * CHUNK)
                stp = st_ref[c]
                dst = dstate[...]
                dstb = dst.astype(BF16)
                bl = bc[(c + 1) * CHUNK - 1:(c + 1) * CHUNK, :]
                ekl = jnp.exp2(bl - bc[rs])
                ebl = jnp.exp2(bl)
                kt = k[rs] * ekl
                dq_st = _dot(dob[rs], stp.astype(BF16)) * eb[rs]
                dkt = _dot(vb[rs], dstb)
                extra = jnp.sum(kt * dkt, axis=0, keepdims=True) + ebl * jnp.sum(stp * dst, axis=0, keepdims=True)
                dqs[rs, :] = dq_st + dq_in[rs] * eb[rs]
                dks[rs, :] = dkt * ekl + dk_in[rs] * eib[rs]
                di_ref[rs, :] = _dot(kt.astype(BF16), dstb, NT) + dv_in[rs]
                dbs[rs, :] = (q[rs] * dq_st - kt * dkt + jnp.where(last_row, extra, 0.0)
                              + (qt[rs].astype(F32) * dq_in[rs] - ki[rs].astype(F32) * dk_in[rs]))
                dstate[...] = dst * ebl + _dot(dob[rs], qt[rs], TN)
            dbs[...] = _group_cumsum(triuc_ref[...], dbs[...])

        @pl.when(jnp.logical_not(bounded))
        def _():
            qs[...] = q
            ks[...] = k
            bs[...] = _group_cumsum(tril_ref[...], lf)
            dos[...] = do_all
            per = CHUNK // SUB

            def restore(i, carry):
                @pl.when(i % per == 0)
                def _():
                    sts[i] = st_ref[i // per]

                @pl.when(i % per != 0)
                def _():
                    rp = pl.multiple_of((i - 1) * SUB, SUB)
                    b_ = bs[pl.ds(rp, SUB), :]
                    bl = b_[SUB - 1:SUB, :]
                    kt = (ks[pl.ds(rp, SUB), :] * jnp.exp2(bl - b_)).astype(BF16)
                    sts[i] = sts[i - 1] * jnp.exp2(bl) + _dot(v_ref[pl.ds(rp, SUB), :].astype(BF16), kt, TN)

                return carry

            lax.fori_loop(0, nsub, restore, 0)
            rows = lax.broadcasted_iota(jnp.int32, (HALF, 1), 0)
            last_row = lax.broadcasted_iota(jnp.int32, (SUB, 1), 0) == SUB - 1

            def step(i, carry):
                ii = nsub - 1 - i
                r0 = pl.multiple_of(ii * SUB, SUB)
                q_ = qs[pl.ds(r0, SUB), :]
                k_ = ks[pl.ds(r0, SUB), :]
                b_ = bs[pl.ds(r0, SUB), :]
                v_ = v_ref[pl.ds(r0, SUB), :]
                do_ = dos[pl.ds(r0, SUB), :]
                stp = sts[ii]
                dst = dstate[...]
                bl = b_[SUB - 1:SUB, :]
                eb = jnp.exp2(b_)
                ekl = jnp.exp2(bl - b_)
                ebl = jnp.exp2(bl)
                dob = do_.astype(BF16)
                dstb = dst.astype(BF16)
                kt = k_ * ekl
                dq = _dot(dob, stp.astype(BF16)) * eb
                dkt = _dot(v_.astype(BF16), dstb)
                dk = dkt * ekl
                dv = _dot(kt.astype(BF16), dstb, NT)
                extra = jnp.sum(kt * dkt, axis=0, keepdims=True) + ebl * jnp.sum(stp * dst, axis=0, keepdims=True)
                halves = lambda x: [x[:HALF], x[HALF:]]
                q_h, b_h, do_h, dq_h, dk_h, dv_h = (halves(x) for x in (q_, b_, do_, dq, dk, dv))
                for own in range(2):
                    dk_rows, dv_rows = _RowSums(rows), _RowSums(rows)
                    for jj in _RowSums.ORDER:
                        j = own * HALF + jj
                        bj, kj, vj = b_[j:j + 1, :], k_[j:j + 1, :], v_[j:j + 1, :]
                        dk_sum = dv_sum = None
                        for h in range(own, 2):
                            e = _decay(b_h[h], bj, rows, jj if h == own else None)
                            pe = q_h[h] * e
                            acol = jnp.sum(pe * kj, axis=-1, keepdims=True)
                            dacol = jnp.sum(do_h[h] * vj, axis=-1, keepdims=True)
                            dq_h[h] = dq_h[h] + dacol * (e * kj)
                            dk_sum = dacol * pe if dk_sum is None else dk_sum + dacol * pe
                            dv_sum = acol * do_h[h] if dv_sum is None else dv_sum + acol * do_h[h]
                        dk_rows.push(jj, dk_sum)
                        dv_rows.push(jj, dv_sum)
                    dk_h[own] = dk_h[own] + dk_rows.result()
                    dv_h[own] = dv_h[own] + dv_rows.result()
                dq, dk, dv = (jnp.concatenate(x, axis=0) for x in (dq_h, dk_h, dv_h))
                dqs[pl.ds(r0, SUB), :] = dq
                dks[pl.ds(r0, SUB), :] = dk
                di_ref[pl.ds(r0, SUB), :] = dv
                dbs[pl.ds(r0, SUB), :] = q_ * dq - k_ * dk + jnp.where(last_row, extra, 0.0)
                dstate[...] = dst * ebl + _dot(dob, (q_ * eb).astype(BF16), TN)
                return carry

            lax.fori_loop(0, nsub, step, 0, unroll=unroll)
            dbs[...] = _group_cumsum(triu_ref[...], dbs[...])

        dlf = dbs[...]
        dfv = dlf / f - dks[...]
        df_ref[...] = dfv * (1.0 - lb) * sf * (1.0 - sf)
        dlb = jnp.sum(dfv * (1.0 - sf), axis=0, keepdims=True)
        dq_ref[...] = dqs[...] * (sq * (1.0 + qp * (1.0 - sq)))
        _acc_rows(stat_ref, t, [dgn, dlb])

    rev = lambda t: nt - 1 - t
    col = lambda k: pl.BlockSpec((tc, hd), lambda h, t: (rev(t), k * HGRN_HEADS + h))
    vec = pl.BlockSpec((None, 1, hd), lambda h, t: (h, 0, 0))
    head = pl.BlockSpec((tc, hd), lambda h, t: (rev(t), h))
    x_specs, x_shapes, x_scratch = _plan_extras(plan)
    return pl.pallas_call(
        _fuse_exchange(body, 14, 5, 9, plan, 2), name="hgrn_bwd", grid=(HGRN_HEADS, nt),
        in_specs=[col(0), col(1), col(2), col(3), vec, vec, vec, head, head,
                  pl.BlockSpec((None, tc // CHUNK, hd, hd), lambda h, t: (h, rev(t), 0, 0))]
                 + [_const((tc, tc))] * 4 + x_specs,
        out_specs=[head, head, head, head, pl.BlockSpec((None, 8, hd), lambda h, t: (h, 0, 0))] + x_specs,
        out_shape=[jax.ShapeDtypeStruct((s, HGRN_WIDTH), F32)] * 4 + [jax.ShapeDtypeStruct((HGRN_HEADS, 8, hd), F32)]
                  + x_shapes,
        scratch_shapes=[pltpu.VMEM((hd, hd), F32)] + [pltpu.VMEM((tc, hd), F32)] * 7
                       + [pltpu.VMEM((nsub, hd, hd), F32)] + x_scratch,
        compiler_params=_params(("arbitrary", "arbitrary")),
    )(z, z, z, z, l0, l1, gn, o, dmix, st, _block_tri(tc, SUB, False), _block_tri(tc, SUB, True),
      _block_tri(tc, CHUNK, False), _block_tri(tc, CHUNK, True), *plan_args)


def _in_bwd(dparts, dx1, x, g, win_g, tm):
    s, d = x.shape
    nsh, _, wc = win_g.shape
    pw = dparts[0].shape[1]

    def body(dq_ref, df_ref, di_ref, dg_ref, dp_ref, dx1_ref, x_ref, g_ref, w_ref, gx_ref, dz_ref, st_ref):
        dz = jnp.concatenate([dq_ref[...], df_ref[...], di_ref[...], dg_ref[...], dp_ref[...]], axis=1).astype(BF16)
        dz_ref[...] = dz
        dh = jnp.zeros((tm, d), F32)
        for j in range(nsh):
            dh = dh + _dot(dz[:, j * wc:(j + 1) * wc], w_ref[j], NT)
        gv = g_ref[...]
        _, n, r = _rms_fwd(x_ref[...], gv)
        dxn, dg = _rms_bwd(dh, n, r, gv)
        gx_ref[...] = dx1_ref[...] + dxn
        _acc_rows(st_ref, pl.program_id(0), [dg])

    tile = lambda w: pl.BlockSpec((tm, w), lambda t: (t, 0))
    return pl.pallas_call(
        body, name="in_bwd", grid=(s // tm,),
        in_specs=[tile(pw)] * 5 + [tile(d), tile(d), _const((1, d)), _const(win_g.shape)],
        out_specs=[tile(d), tile(nsh * wc), _const_out((8, d))],
        out_shape=[jax.ShapeDtypeStruct((s, d), F32), jax.ShapeDtypeStruct((s, nsh * wc), BF16),
                   jax.ShapeDtypeStruct((8, d), F32)],
        compiler_params=_params(("arbitrary",)),
    )(*dparts, dx1, x, g, win_g)


def _tn_grad(name, a, b, out_rows, out_cols, a_sharded, tr, tc, plan=None, plan_args=()):
    s = a.shape[0]
    nr, nc = out_rows // tr, out_cols // tc

    def body(a_ref, b_ref, o_ref):
        o_ref[...] = _dot(a_ref[...], b_ref[...], TN)

    a_map = (lambda j, i, k: (0, j * nr + i)) if a_sharded else (lambda j, i, k: (0, i))
    b_map = (lambda j, i, k: (0, k)) if a_sharded else (lambda j, i, k: (0, j * nc + k))
    x_specs, x_shapes, x_scratch = _plan_extras(plan)
    res = pl.pallas_call(
        _fuse_exchange(body, 2, 1, 0, plan, 3), name=name, grid=(N_CHIPS, nr, nc),
        in_specs=[pl.BlockSpec((s, tr), a_map), pl.BlockSpec((s, tc), b_map)] + x_specs,
        out_specs=[pl.BlockSpec((None, tr, tc), lambda j, i, k: (j, i, k))] + x_specs,
        out_shape=[jax.ShapeDtypeStruct((N_CHIPS, out_rows, out_cols), F32)] + x_shapes,
        scratch_shapes=x_scratch,
        compiler_params=_params(("arbitrary", "arbitrary", "arbitrary")),
    )(a, b, *plan_args)
    return res if plan else res[0]


FFN_NAMES = ("w_ff1", "w_ff2")
ATTN_NAMES = ("w_xo", "w_xq", "w_out", "w_xk", "w_xv")
EARLY_NAMES = FFN_NAMES + ATTN_NAMES
BIG_NAMES = EARLY_NAMES + ("w_in",)


def _halved(g):
    return g.reshape(N_CHIPS, 2, g.shape[1] // 2, g.shape[2])


def _pair_adds(names, gs, got, idx):
    pairs = [_grad_pair_add("grad_pair_add_" + k, g, r, idx, tr=min(256, g.shape[2])) for k, g, r in zip(names, gs, got)]
    return [p[0] for p in pairs], [p[1] for p in pairs]


def _step(x, mem, target, small, shards, idx):
    d = x.shape[1]
    l0 = small["lb_logits"][0].reshape(HGRN_HEADS, 1, HEAD_DIM)
    l1 = small["lb_logits"][1].reshape(HGRN_HEADS, 1, HEAD_DIM)
    gn = small["hgrn_norm_g"].reshape(HGRN_HEADS, 1, HEAD_DIM)
    wp = small["w_pool"].reshape(len(POOL_WINDOWS), HEAD_DIM, HEAD_DIM)
    psc = small["pool_scale"].reshape(1, -1)
    gmix, gx, gmem, gffn = (small[k].reshape(1, d) for k in ("norm_mix_g", "norm_x_g", "norm_mem_g", "norm_ffn_g"))
    gfin = small["final_norm_g"].reshape(1, d)

    (win_g,) = _run_exchange("gather_w_in", _WeightGather([shards["w_in"]]), [shards["w_in"]])
    attn_w = [shards["slab_attn"], shards["w_xo"]]
    z, h, slab_g, wo_g = _in_proj(x, gmix, win_g, tm=512, plan=_WeightGather(attn_w), plan_args=attn_w)
    ffn_w = [shards["slab_ffn"]]
    o, oa, st, ffn_g = _hgrn_fwd(z, l0, l1, gn, tc=256, unroll=8, plan=_WeightGather(ffn_w), plan_args=ffn_w)
    ob = _pool_fwd(z, wp, psc, tm=512)
    xk, xv = _kv_proj(mem, gmem, slab_g)
    x1, mixed, hq, xq, att, x2 = _mix_xattn_fwd(x, oa, ob, gx, slab_g, wo_g, xk, xv, tm=512)
    a, hf, dx3, dx3b, st_loss = _mlp_loss_fwd(x2, gffn, gfin, ffn_g, target, tm=512)

    da, u, dx2, dx2b, st_ffn = _mlp_bwd(dx3, dx3b, a, x2, gffn, ffn_g, tm=256)
    g_ff1 = [_halved(_tn_grad("dw_ff1", hf, da, d, d, False, 1024, 1024))]
    dw_ff2, *got = _tn_grad("dw_ff2", u, dx3b, d, d, True, 1024, 1024, plan=_PairExchange(g_ff1), plan_args=g_ff1)
    keep_ff1, send_ff1 = _pair_adds(("w_ff1",), g_ff1, got, idx)
    g_ff2 = [_halved(dw_ff2)]
    dx1, dx1b, dxq, dmix, dxk, dxv, st_x, *got = _xattn_mix_bwd(
        dx2, x1, xq, xk, xv, gx, slab_g, wo_g, tm=512,
        plan=_Plans([_ChipExchange(send_ff1), _PairExchange(g_ff2)]), plan_args=send_ff1 + g_ff2)
    recv_ff1 = got[:1]
    keep_ff2, send_ff2 = _pair_adds(("w_ff2",), g_ff2, got[1:], idx)
    dw = {}
    dw["w_xo"] = _tn_grad("dw_xo", att, dx2b, d, d // N_CHIPS, False, 1024, 256)
    dw["w_xq"] = _tn_grad("dw_xq", hq, dxq, d // N_CHIPS, d, True, 256, 1024)
    dw["w_out"] = _tn_grad("dw_out", mixed, dx1b, d // N_CHIPS, d, True, 256, 1024)
    dw["w_xk"], dw["w_xv"], st_mem = _kv_bwd(mem, gmem, dxk, dxv, slab_g)
    gs_attn = [_halved(dw[k]) for k in ATTN_NAMES]
    dp, d_wp, st_pool, *got_attn = _pool_bwd(z, dmix, wp, psc, tm=512, plan=_PairExchange(gs_attn), plan_args=gs_attn)
    keep_attn, send_attn = _pair_adds(ATTN_NAMES, gs_attn, got_attn, idx)
    sends = send_ff2 + send_attn
    dq, df, di, dg, st_hgrn, *received = _hgrn_bwd(z, o, dmix, st, l0, l1, gn, tc=256, unroll=4,
                                                    plan=_ChipExchange(sends), plan_args=sends)
    keeps = keep_ff1 + keep_ff2 + keep_attn
    received = recv_ff1 + list(received)
    grad_x, dz, st_mix = _in_bwd([dq, df, di, dg, dp], dx1, x, gmix, win_g, tm=512)
    gs_in = [_halved(_tn_grad("dw_in", h, dz, d, win_g.shape[2], False, 1024, win_g.shape[2]))]
    got_in = _run_exchange("grad_pair_exchange_w_in", _PairExchange(gs_in), gs_in)
    keep_in, send_in = _pair_adds(("w_in",), gs_in, got_in, idx)
    recv_in = _run_exchange("grad_chip_exchange_w_in", _ChipExchange(send_in), send_in)

    partials = dict(zip(BIG_NAMES, zip(keeps + keep_in, list(received) + list(recv_in))))
    stats = dict(mix=st_mix, x=st_x, mem=st_mem, ffn=st_ffn, loss=st_loss, hgrn=st_hgrn, pool=st_pool)
    return grad_x, stats, d_wp, partials


def _place():
    x, y, c = lax.axis_index("x"), lax.axis_index("y"), lax.axis_index("c")
    return x, y, c, [(x, 1 - y), (1 - x, y), (1 - x, 1 - y)]


def _rcopy(src, dst, ssem, rsem, dev):
    return pltpu.make_async_remote_copy(src_ref=src, dst_ref=dst, send_sem=ssem, recv_sem=rsem,
                                        device_id=dev, device_id_type=MESH)


class _WeightGather:
    def __init__(self, shards):
        self.n = len(shards)
        self.rows = [w.shape[0] for w in shards]
        self.out_shape = [jax.ShapeDtypeStruct((N_CHIPS,) + w.shape, w.dtype) for w in shards]
        self.scratch_shapes = [pltpu.SemaphoreType.DMA((self.n,))] * 2 + [pltpu.SemaphoreType.DMA((self.n, 3))] * 4

    def _copies(self, ins, outs, sems, with_pass_on):
        lsem, lrsem, ssem, rsem, fsem, frsem = sems
        x, y, c, peers = _place()
        chip = 2 * x + y
        sib = (x, y, 1 - c)
        own = [_rcopy(ins[a], outs[a].at[chip], lsem.at[a], lrsem.at[a], sib) for a in range(self.n)]
        sends, arrived, passed, passed_in = [], [], [], []
        for a in range(self.n):
            hr = self.rows[a] // 2
            half = lambda who, hc, a=a, hr=hr: outs[a].at[who, pl.ds(hc * hr, hr), :]
            for r, (px, py) in enumerate(peers):
                pc = 2 * px + py
                sends.append(_rcopy(ins[a].at[pl.ds(c * hr, hr), :], half(chip, c), ssem.at[a, r], rsem.at[a, r],
                                    (px, py, c)))
                if with_pass_on:
                    arrived.append(_rcopy(half(pc, c), half(pc, c), ssem.at[a, r], rsem.at[a, r], (px, py, c)))
                    passed.append(_rcopy(half(pc, c), half(pc, c), fsem.at[a, r], frsem.at[a, r], sib))
                    passed_in.append(_rcopy(half(pc, 1 - c), half(pc, 1 - c), fsem.at[a, r], frsem.at[a, r], sib))
        return own, sends, arrived, passed, passed_in

    def start(self, ins, outs, sems):
        own, sends, _, _, _ = self._copies(ins, outs, sems, False)
        for cp in own + sends:
            cp.start()

    def finish(self, ins, outs, sems):
        own, sends, arrived, passed, passed_in = self._copies(ins, outs, sems, True)
        for got, fwd in zip(arrived, passed):
            got.wait_recv()
            fwd.start()
        for cp in passed_in:
            cp.wait_recv()
        for cp in sends + passed:
            cp.wait_send()
        for cp in own:
            cp.wait()


class _ChipExchange:
    def __init__(self, sends):
        self.n = len(sends)
        self.out_shape = [jax.ShapeDtypeStruct(g.shape, g.dtype) for g in sends]
        self.scratch_shapes = [pltpu.SemaphoreType.DMA((self.n, 3))] * 2

    def _copies(self, ins, outs, sems):
        ssem, rsem = sems
        _, _, c, peers = _place()
        return [_rcopy(ins[a].at[r], outs[a].at[r], ssem.at[a, r], rsem.at[a, r], (px, py, c))
                for a in range(self.n) for r, (px, py) in enumerate(peers)]

    def start(self, ins, outs, sems):
        for cp in self._copies(ins, outs, sems):
            cp.start()

    def finish(self, ins, outs, sems):
        for cp in self._copies(ins, outs, sems):
            cp.wait()


class _Plans:
    def __init__(self, plans):
        self.plans = plans
        self.n = sum(p.n for p in plans)
        self.out_shape = [s for p in plans for s in p.out_shape]
        self.scratch_shapes = [s for p in plans for s in p.scratch_shapes]

    def _each(self, ins, outs, sems):
        a = b = 0
        for p in self.plans:
            ns = len(p.scratch_shapes)
            yield p, ins[a:a + p.n], outs[a:a + p.n], sems[b:b + ns]
            a, b = a + p.n, b + ns

    def start(self, ins, outs, sems):
        for p, i, o, s in self._each(ins, outs, sems):
            p.start(i, o, s)

    def finish(self, ins, outs, sems):
        for p, i, o, s in self._each(ins, outs, sems):
            p.finish(i, o, s)


def _run_exchange(name, plan, arrays):
    n = plan.n

    def body(*refs):
        ins, outs, sems = refs[:n], refs[n:2 * n], refs[2 * n:]
        plan.start(ins, outs, sems)
        plan.finish(ins, outs, sems)

    return pl.pallas_call(
        body, name=name, in_specs=[ANY] * n, out_specs=[ANY] * n,
        out_shape=plan.out_shape, scratch_shapes=plan.scratch_shapes,
    )(*arrays)


class _PairExchange:
    def __init__(self, gs):
        self.n = len(gs)
        self.out_shape = [jax.ShapeDtypeStruct((g.shape[0],) + g.shape[2:], g.dtype) for g in gs]
        self.scratch_shapes = [pltpu.SemaphoreType.DMA((self.n,))] * 2

    def _copies(self, ins, outs, sems):
        ssem, rsem = sems
        x, y, c, _ = _place()
        return [_rcopy(ins[a].at[:, 1 - c], outs[a], ssem.at[a], rsem.at[a], (x, y, 1 - c)) for a in range(self.n)]

    def start(self, ins, outs, sems):
        for cp in self._copies(ins, outs, sems):
            cp.start()

    def finish(self, ins, outs, sems):
        for cp in self._copies(ins, outs, sems):
            cp.wait()


def _grad_pair_add(name, g, got, idx, tr):
    _, _, hr, cc = g.shape

    def body(idx_ref, g0, g1, g2, g3, r0, r1, r2, r3, keep_ref, send_ref):
        keep_ref[...] = g0[...] + r0[...]
        for q, (gq, rq) in enumerate(((g1, r1), (g2, r2), (g3, r3))):
            send_ref[q] = (gq[...] + rq[...]).astype(BF16)

    gspec = lambda q: pl.BlockSpec((None, None, tr, cc), lambda i, idx: (idx[1 + q], idx[0], i, 0))
    rspec = lambda q: pl.BlockSpec((None, tr, cc), lambda i, idx: (idx[1 + q], i, 0))
    return pl.pallas_call(
        body, name=name,
        grid_spec=pltpu.PrefetchScalarGridSpec(
            num_scalar_prefetch=1, grid=(hr // tr,),
            in_specs=[gspec(q) for q in range(4)] + [rspec(q) for q in range(4)],
            out_specs=[pl.BlockSpec((tr, cc), lambda i, idx: (i, 0)), pl.BlockSpec((3, tr, cc), lambda i, idx: (0, i, 0))]),
        out_shape=[jax.ShapeDtypeStruct((hr, cc), F32), jax.ShapeDtypeStruct((3, hr, cc), BF16)],
        compiler_params=_params(("parallel",)),
    )(idx, g, g, g, g, got, got, got, got)


def _grad_chip_add(name, keep, got, tr):
    hr, cc = keep.shape

    def body(k_ref, g_ref, o_ref):
        o_ref[...] = ((k_ref[...] + g_ref[0].astype(F32)) + g_ref[1].astype(F32)) + g_ref[2].astype(F32)

    return pl.pallas_call(
        body, name=name, grid=(hr // tr,),
        in_specs=[pl.BlockSpec((tr, cc), lambda i: (i, 0)), pl.BlockSpec((3, tr, cc), lambda i: (0, i, 0))],
        out_specs=pl.BlockSpec((tr, cc), lambda i: (i, 0)),
        out_shape=jax.ShapeDtypeStruct((hr, cc), F32),
        compiler_params=_params(("parallel",)),
    )(keep, got)


def _grad_half_exchange(ts):
    n = len(ts)

    def body(*refs):
        ins, outs, ssem, rsem = refs[:n], refs[n:2 * n], refs[2 * n], refs[2 * n + 1]
        x, y, c, _ = _place()
        cps = [_rcopy(ins[a], outs[a], ssem.at[a], rsem.at[a], (x, y, 1 - c)) for a in range(n)]
        for cp in cps:
            cp.start()
        for cp in cps:
            cp.wait()

    return pl.pallas_call(
        body, name="grad_half_exchange",
        in_specs=[ANY] * n, out_specs=[ANY] * n,
        out_shape=[jax.ShapeDtypeStruct(t.shape, t.dtype) for t in ts],
        scratch_shapes=[pltpu.SemaphoreType.DMA((n,))] * 2,
    )(*ts)


def _small_allreduce(stats, d_wp):
    d = D_MODEL
    half = d // 2
    wps = d_wp.shape

    def body(mix_ref, x_ref, mem_ref, ffn_ref, loss_ref, hg_ref, pool_ref, wp_ref, slab_out, wp_out,
             slab_buf, wp_buf, sib_s, sib_w, ssem, rsem):
        x, y, c, peers = _place()
        chip = 2 * x + y
        sib = (x, y, 1 - c)
        hgn = jnp.concatenate([hg_ref[h, 0:1, :] for h in range(HGRN_HEADS)], axis=1)
        dlb = jnp.concatenate([hg_ref[h, 1:2, :] for h in range(HGRN_HEADS)], axis=1)
        slab_buf[0] = jnp.concatenate([
            mix_ref[0:1, :], x_ref[0:1, :], mem_ref[0:1, :], ffn_ref[0:1, :], loss_ref[0:1, :],
            jnp.concatenate([dlb, hgn], axis=1),
            jnp.concatenate([pool_ref[0:1, :], jnp.zeros((1, half), F32)], axis=1),
            loss_ref[1:2, :]], axis=0)
        wp_buf[0] = wp_ref[...]
        pair = [_rcopy(slab_buf.at[0], sib_s, ssem.at[0], rsem.at[0], sib),
                _rcopy(wp_buf.at[0], sib_w, ssem.at[1], rsem.at[1], sib)]
        for cp in pair:
            cp.start()
        for cp in pair:
            cp.wait()
        slab_buf[0] = slab_buf[0] + sib_s[...]
        wp_buf[0] = wp_buf[0] + sib_w[...]
        cps = []
        for r, (px, py) in enumerate(peers):
            cps.append(_rcopy(slab_buf.at[0], slab_buf.at[r + 1], ssem.at[2 + 2 * r], rsem.at[2 + 2 * r], (px, py, c)))
            cps.append(_rcopy(wp_buf.at[0], wp_buf.at[r + 1], ssem.at[3 + 2 * r], rsem.at[3 + 2 * r], (px, py, c)))
        for cp in cps:
            cp.start()
        for cp in cps:
            cp.wait()
        tot_s, tot_w = slab_buf[chip], wp_buf[chip]
        for j in range(1, N_CHIPS):
            tot_s = tot_s + slab_buf[jnp.bitwise_xor(j, chip)]
            tot_w = tot_w + wp_buf[jnp.bitwise_xor(j, chip)]
        slab_out[...] = tot_s
        wp_out[...] = tot_w

    return pl.pallas_call(
        body, name="small_allreduce",
        in_specs=[VMEM] * 8, out_specs=[VMEM] * 2,
        out_shape=[jax.ShapeDtypeStruct((8, d), F32), jax.ShapeDtypeStruct(wps, F32)],
        scratch_shapes=[pltpu.VMEM((N_CHIPS, 8, d), F32), pltpu.VMEM((N_CHIPS,) + wps, F32),
                        pltpu.VMEM((8, d), F32), pltpu.VMEM(wps, F32),
                        pltpu.SemaphoreType.DMA((8,)), pltpu.SemaphoreType.DMA((8,))],
    )(stats["mix"], stats["x"], stats["mem"], stats["ffn"], stats["loss"], stats["hgrn"], stats["pool"], d_wp)


def _adamw_math(w, g, m, v):
    m = ADAM_B1 * m + (1.0 - ADAM_B1) * g
    v = ADAM_B2 * v + (1.0 - ADAM_B2) * (g * g)
    m_hat = m / (1.0 - ADAM_B1 ** ADAM_STEP)
    v_hat = v / (1.0 - ADAM_B2 ** ADAM_STEP)
    delta = -ADAM_LR * (m_hat / (jnp.sqrt(v_hat) + ADAM_EPS) + ADAM_WD * w)
    return delta, m, v


def _adamw(name, mine, theirs, w, m, v, idx, tr):
    rows = w.shape[0]
    cc = mine.shape[1]
    nb = rows // 2 // tr
    heads = w.shape[1] if w.ndim == 3 else 1
    e = cc // heads

    def body(idx_ref, a_ref, b_ref, w_ref, m_ref, v_ref, g_out, d_out, m_out, v_out):
        g = jnp.where(pl.program_id(0) // nb == idx_ref[0], a_ref[...], b_ref[...])
        if w.ndim == 2:
            g_out[...] = g
            d_out[...], m_out[...], v_out[...] = _adamw_math(w_ref[...], g, m_ref[...], v_ref[...])
        else:
            for h in range(heads):
                gh = g[:, h * e:(h + 1) * e]
                g_out[:, h, :] = gh
                d_out[:, h, :], m_out[:, h, :], v_out[:, h, :] = _adamw_math(
                    w_ref[:, h, :], gh, m_ref[:, h, :], v_ref[:, h, :])

    hspec = pl.BlockSpec((tr, cc), lambda i, idx: (i % nb, 0))
    spec = pl.BlockSpec((tr,) + w.shape[1:], lambda i, idx: (i,) + (0,) * (w.ndim - 1))
    return pl.pallas_call(
        body, name=name,
        grid_spec=pltpu.PrefetchScalarGridSpec(
            num_scalar_prefetch=1, grid=(rows // tr,),
            in_specs=[hspec, hspec, spec, spec, spec], out_specs=[spec] * 4),
        out_shape=[jax.ShapeDtypeStruct(w.shape, F32)] * 4,
        compiler_params=_params(("parallel",)),
    )(idx, mine, theirs, w, m, v)


SMALL_NAMES = ("norm_mix_g", "lb_logits", "hgrn_norm_g", "w_pool", "pool_scale", "norm_x_g", "norm_mem_g",
               "norm_ffn_g", "final_norm_g")


def _small_update(slab, d_wp, ws, ms, vs):
    n = len(SMALL_NAMES)
    half = D_MODEL // 2

    def body(slab_ref, wp_ref, *refs):
        w_refs, m_refs, v_refs, outs = refs[:n], refs[n:2 * n], refs[2 * n:3 * n], refs[3 * n:]
        row = lambda k: slab_ref[k:k + 1, :]
        lbl = w_refs[SMALL_NAMES.index("lb_logits")][...]
        s0 = _lower_bound(lbl[0:1, :], lbl[1:2, :])
        dl0 = row(ROW_LB_HGN)[:, :half] * s0 * (1.0 - s0)
        grads = dict(norm_mix_g=row(ROW_GMIX), lb_logits=jnp.concatenate([dl0, -dl0], axis=0),
                     hgrn_norm_g=row(ROW_LB_HGN)[:, half:], w_pool=wp_ref[...], pool_scale=row(ROW_PSCALE)[:, :half],
                     norm_x_g=row(ROW_GX), norm_mem_g=row(ROW_GMEM), norm_ffn_g=row(ROW_GFFN),
                     final_norm_g=row(ROW_GFIN))
        outs[0][...] = row(ROW_LOSS)[:, :128]
        for i, name in enumerate(SMALL_NAMES):
            g = grads[name]
            delta, m2, v2 = _adamw_math(w_refs[i][...], g, m_refs[i][...], v_refs[i][...])
            for o, val in zip(outs[1 + 4 * i:5 + 4 * i], (g, delta, m2, v2)):
                o[...] = val

    args = [ws[k] for k in SMALL_NAMES] + [ms[k] for k in SMALL_NAMES] + [vs[k] for k in SMALL_NAMES]
    out_shape = [jax.ShapeDtypeStruct((1, 128), F32)]
    for k in SMALL_NAMES:
        out_shape += [jax.ShapeDtypeStruct(ws[k].shape, F32)] * 4
    res = pl.pallas_call(
        body, name="small_update",
        in_specs=[VMEM] * (2 + 3 * n), out_specs=[VMEM] * len(out_shape), out_shape=out_shape,
    )(slab, d_wp, *args)
    return res[0], {k: res[1 + 4 * i:5 + 4 * i] for i, k in enumerate(SMALL_NAMES)}


ALL_NAMES = ("norm_mix_g", "w_in", "lb_logits", "hgrn_norm_g", "w_pool", "pool_scale", "w_out", "norm_x_g",
             "norm_mem_g", "w_xq", "w_xk", "w_xv", "w_xo", "norm_ffn_g", "w_ff1", "w_ff2", "final_norm_g")


def _shard_2d(name, a):
    a = a[0]
    if name in ("w_xq", "w_xk", "w_xv"):
        return a.reshape(a.shape[0], -1)
    if name == "w_xo":
        return a.reshape(-1, a.shape[-1])
    return a


def _small_2d(name, a):
    if name == "w_pool":
        return a.reshape(-1, HEAD_DIM)
    if name == "lb_logits":
        return a
    return a.reshape(1, -1)


def kernel(x, mem, norm_mix_g, w_in, lb_logits, hgrn_norm_g, w_pool, pool_scale, w_out, norm_x_g, norm_mem_g, w_xq, w_xk, w_xv, w_xo, norm_ffn_g, w_ff1, w_ff2, final_norm_g, loss_target, m_norm_mix_g, m_w_in, m_lb_logits, m_hgrn_norm_g, m_w_pool, m_pool_scale, m_w_out, m_norm_x_g, m_norm_mem_g, m_w_xq, m_w_xk, m_w_xv, m_w_xo, m_norm_ffn_g, m_w_ff1, m_w_ff2, m_final_norm_g, v_norm_mix_g, v_w_in, v_lb_logits, v_hgrn_norm_g, v_w_pool, v_pool_scale, v_w_out, v_norm_x_g, v_norm_mem_g, v_w_xq, v_w_xk, v_w_xv, v_w_xo, v_norm_ffn_g, v_w_ff1, v_w_ff2, v_final_norm_g):
    w = dict(norm_mix_g=norm_mix_g, w_in=w_in, lb_logits=lb_logits, hgrn_norm_g=hgrn_norm_g, w_pool=w_pool, pool_scale=pool_scale, w_out=w_out, norm_x_g=norm_x_g, norm_mem_g=norm_mem_g, w_xq=w_xq, w_xk=w_xk, w_xv=w_xv, w_xo=w_xo, norm_ffn_g=norm_ffn_g, w_ff1=w_ff1, w_ff2=w_ff2, final_norm_g=final_norm_g)
    m = dict(norm_mix_g=m_norm_mix_g, w_in=m_w_in, lb_logits=m_lb_logits, hgrn_norm_g=m_hgrn_norm_g, w_pool=m_w_pool, pool_scale=m_pool_scale, w_out=m_w_out, norm_x_g=m_norm_x_g, norm_mem_g=m_norm_mem_g, w_xq=m_w_xq, w_xk=m_w_xk, w_xv=m_w_xv, w_xo=m_w_xo, norm_ffn_g=m_norm_ffn_g, w_ff1=m_w_ff1, w_ff2=m_w_ff2, final_norm_g=m_final_norm_g)
    v = dict(norm_mix_g=v_norm_mix_g, w_in=v_w_in, lb_logits=v_lb_logits, hgrn_norm_g=v_hgrn_norm_g, w_pool=v_w_pool, pool_scale=v_pool_scale, w_out=v_w_out, norm_x_g=v_norm_x_g, norm_mem_g=v_norm_mem_g, w_xq=v_w_xq, w_xk=v_w_xk, w_xv=v_w_xv, w_xo=v_w_xo, norm_ffn_g=v_norm_ffn_g, w_ff1=v_w_ff1, w_ff2=v_w_ff2, final_norm_g=v_final_norm_g)

    big_w = {k: _shard_2d(k, w[k]) for k in BIG_NAMES}
    slab_attn = jnp.concatenate([big_w[k] for k in ("w_out", "w_xq", "w_xk", "w_xv")], axis=0).astype(BF16)
    slab_ffn = jnp.concatenate([big_w[k] for k in ("w_ff1", "w_ff2")], axis=0).astype(BF16)
    shards = dict(slab_attn=slab_attn, slab_ffn=slab_ffn, w_in=big_w["w_in"].astype(BF16),
                  w_xo=big_w["w_xo"].astype(BF16))

    cx, cy, cc = lax.axis_index("x"), lax.axis_index("y"), lax.axis_index("c")
    chip = 2 * cx + cy
    idx = jnp.stack([cc, chip, chip ^ 1, chip ^ 2, chip ^ 3]).astype(jnp.int32)
    small = {k: w[k] for k in SMALL_NAMES}
    grad_x, stats, d_wp, partials = _step(x[0], mem[0], loss_target[0], small, shards, idx)

    halves = [_grad_chip_add("grad_chip_add_" + k, *partials[k], tr=min(256, partials[k][0].shape[0]))
              for k in BIG_NAMES]
    theirs = _grad_half_exchange(halves)

    grads, deltas, new_m, new_v = {}, {}, {}, {}
    for k, mine, other in zip(BIG_NAMES, halves, theirs):
        as_held = (lambda a: a[0]) if k in ("w_xq", "w_xk", "w_xv") else functools.partial(_shard_2d, k)
        res = _adamw("adamw_" + k, mine, other, as_held(w[k]), as_held(m[k]), as_held(v[k]), idx,
                     tr=min(256, mine.shape[0]))
        for store, val in zip((grads, deltas, new_m, new_v), res):
            store[k] = val.reshape(w[k].shape)

    slab_sum, wp_sum = _small_allreduce(stats, d_wp.reshape(-1, HEAD_DIM))
    loss, upd = _small_update(slab_sum, wp_sum, {k: _small_2d(k, w[k]) for k in SMALL_NAMES},
                              {k: _small_2d(k, m[k]) for k in SMALL_NAMES}, {k: _small_2d(k, v[k]) for k in SMALL_NAMES})
    for k in SMALL_NAMES:
        for store, val in zip((grads, deltas, new_m, new_v), upd[k]):
            store[k] = val.reshape(w[k].shape)

    return (loss[0, 0], grad_x[None], *[grads[k] for k in ALL_NAMES], *[deltas[k] for k in ALL_NAMES],
            *[new_m[k] for k in ALL_NAMES], *[new_v[k] for k in ALL_NAMES])
```

```python
import functools

import jax
import jax.numpy as jnp
from jax import lax
from jax.experimental import pallas as pl
from jax.experimental.pallas import tpu as pltpu

F32 = jnp.float32
BF16 = jnp.bfloat16
LOG2E = 1.4426950408889634
NEG_BIG = -1e30
MAX_LOG2_GROWTH = 100.0
MESH = pl.DeviceIdType.MESH
ANY = pl.BlockSpec(memory_space=pl.ANY)
VMEM = pl.BlockSpec(memory_space=pltpu.VMEM)

D_MODEL = 1024
N_CHIPS = 4
HGRN_HEADS = 4
HEAD_DIM = 128
HGRN_WIDTH = HGRN_HEADS * HEAD_DIM
POOL_WINDOWS = (2, 4, 8, 16)
POOL_HALO = 16
SUB = 16
HALF = SUB // 2
CHUNK = 64
XATTN_HEADS = 4
XATTN_HEAD_DIM = 256
EPS = 1e-6
ADAM_LR, ADAM_B1, ADAM_B2, ADAM_EPS, ADAM_WD, ADAM_STEP = 0.001, 0.9, 0.999, 1e-08, 0.01, 10

V7X_VMEM_BYTES = 64 * 1024 * 1024
VMEM_LIMIT = V7X_VMEM_BYTES - 8 * 1024 * 1024

NN = (((1,), (0,)), ((), ()))
NT = (((1,), (1,)), ((), ()))
TN = (((0,), (0,)), ((), ()))

ROW_GMIX, ROW_GX, ROW_GMEM, ROW_GFFN, ROW_GFIN, ROW_LB_HGN, ROW_PSCALE, ROW_LOSS = range(8)


def _dot(a, b, dims=NN):
    return lax.dot_general(a, b, dims, preferred_element_type=F32)


def _sigmoid(x):
    return 1.0 / (1.0 + jnp.exp(-x))


def _rms_fwd(x, g):
    r = lax.rsqrt(jnp.mean(x * x, axis=-1, keepdims=True) + EPS)
    n = x * r
    return n * g, n, r


def _rms_bwd(dh, n, r, g):
    dn = dh * g
    dx = r * (dn - n * jnp.mean(dn * n, axis=-1, keepdims=True))
    return dx, jnp.sum(dh * n, axis=0, keepdims=True)


def _params(sem=None):
    return pltpu.CompilerParams(dimension_semantics=sem, vmem_limit_bytes=VMEM_LIMIT)


def _const(shape):
    nd = len(shape)
    return pl.BlockSpec(shape, lambda *_: (0,) * nd, pipeline_mode=pl.Buffered(1))


def _const_out(shape):
    nd = len(shape)
    return pl.BlockSpec(shape, lambda *_: (0,) * nd)


def _acc_rows(ref, t, rows):
    upd = jnp.concatenate(rows + [jnp.zeros((8 - len(rows), rows[0].shape[1]), F32)], axis=0)

    @pl.when(t == 0)
    def _():
        ref[...] = upd

    @pl.when(t > 0)
    def _():
        ref[...] = ref[...] + upd


def _fuse_exchange(body, n_in, n_out, n_scratch, plan, ndim):
    if plan is None:
        return body
    n = plan.n

    def wrapped(*refs):
        ins, cin = refs[:n_in], refs[n_in:n_in + n]
        outs, cout = refs[n_in + n:n_in + n + n_out], refs[n_in + n + n_out:n_in + 2 * n + n_out]
        rest = refs[n_in + 2 * n + n_out:]
        scr, csem = rest[:n_scratch], rest[n_scratch:]
        first = pl.program_id(0) == 0
        last = pl.program_id(0) == pl.num_programs(0) - 1
        for i in range(1, ndim):
            first = first & (pl.program_id(i) == 0)
            last = last & (pl.program_id(i) == pl.num_programs(i) - 1)

        @pl.when(first)
        def _():
            plan.start(cin, cout, csem)

        body(*ins, *outs, *scr)

        @pl.when(last)
        def _():
            plan.finish(cin, cout, csem)

    return wrapped


def _plan_extras(plan):
    if plan is None:
        return [], [], []
    return [ANY] * plan.n, list(plan.out_shape), list(plan.scratch_shapes)


def _in_proj(x, g, win_g, tm, plan=None, plan_args=()):
    s, d = x.shape
    nsh, _, wc = win_g.shape

    def body(x_ref, g_ref, w_ref, z_ref, h_ref):
        h, _, _ = _rms_fwd(x_ref[...], g_ref[...])
        hb = h.astype(BF16)
        h_ref[...] = hb
        for j in range(nsh):
            z_ref[:, j * wc:(j + 1) * wc] = _dot(hb, w_ref[j])

    x_specs, x_shapes, x_scratch = _plan_extras(plan)
    return pl.pallas_call(
        _fuse_exchange(body, 3, 2, 0, plan, 1), name="in_proj", grid=(s // tm,),
        in_specs=[pl.BlockSpec((tm, d), lambda t: (t, 0)), _const((1, d)), _const((nsh, d, wc))] + x_specs,
        out_specs=[pl.BlockSpec((tm, nsh * wc), lambda t: (t, 0)), pl.BlockSpec((tm, d), lambda t: (t, 0))] + x_specs,
        out_shape=[jax.ShapeDtypeStruct((s, nsh * wc), F32), jax.ShapeDtypeStruct((s, d), BF16)] + x_shapes,
        scratch_shapes=x_scratch,
        compiler_params=_params(("arbitrary",)),
    )(x, g, win_g, *plan_args)


def _lower_bound(l0, l1):
    m = jnp.maximum(l0, l1)
    e0, e1 = jnp.exp(l0 - m), jnp.exp(l1 - m)
    return e0 / (e0 + e1)


def _block_tri(n, group, upper):
    r = lax.broadcasted_iota(jnp.int32, (n, n), 0)
    c = lax.broadcasted_iota(jnp.int32, (n, n), 1)
    keep = (r // group == c // group) & ((c >= r) if upper else (c <= r))
    return keep.astype(BF16)


def _group_cumsum(tri, x):
    hi = x.astype(BF16)
    rest = x - hi.astype(F32)
    mid = rest.astype(BF16)
    lo = (rest - mid.astype(F32)).astype(BF16)
    return (_dot(tri, hi) + _dot(tri, mid)) + _dot(tri, lo)


def _decay(b, bj, rows, first):
    d = b - bj
    if first:
        d = jnp.where(rows >= first, d, NEG_BIG)
    return jnp.exp2(d)


class _RowSums:
    ORDER = (0, 4, 2, 6, 1, 5, 3, 7)

    def __init__(self, rows):
        self.rows = rows
        self.level = {4: {}, 2: {}, 1: {}}

    def _pair(self, p, q, d):
        return jnp.where((self.rows & d) != 0, p + pltpu.roll(p, d, axis=0), q + pltpu.roll(q, HALF - d, axis=0))

    def push(self, j, y, d=4):
        if d == 0:
            self.out = y
            return
        slot = self.level[d]
        key = j % d
        if key not in slot:
            slot[key] = (j, y)
            return
        j0, y0 = slot.pop(key)
        p, q = (y, y0) if j & d else (y0, y)
        self.push(key, self._pair(p, q, d), d // 2)

    def result(self):
        return self.out


def _hgrn_gates(qp, fp, lb):
    sq = _sigmoid(qp)
    sf = _sigmoid(fp)
    f = lb + (1.0 - lb) * sf
    return qp * sq, sq, f, sf


def _hgrn_fwd(z, l0, l1, gn, tc, unroll=1, plan=None, plan_args=()):
    s = z.shape[0]
    nsub = tc // SUB
    hd = HEAD_DIM

    def body(q_ref, f_ref, v_ref, g_ref, l0_ref, l1_ref, gn_ref, tri_ref, tric_ref, o_ref, oa_ref, st_ref,
             state, qs, ks, bs, os_):
        @pl.when(pl.program_id(1) == 0)
        def _():
            state[...] = jnp.zeros_like(state)

        lb = _lower_bound(l0_ref[...], l1_ref[...])
        q, _, f, _ = _hgrn_gates(q_ref[...], f_ref[...], lb)
        k = 1.0 - f
        lf = jnp.log(f) * LOG2E
        bc = _group_cumsum(tric_ref[...], lf)
        bounded = jnp.min(bc) >= -MAX_LOG2_GROWTH

        @pl.when(bounded)
        def _():
            qt = (q * jnp.exp2(bc)).astype(BF16)
            ki = (k * jnp.exp2(-bc)).astype(BF16)
            vb = v_ref[...].astype(BF16)
            a = jnp.where(tric_ref[...] > 0, _dot(qt, ki, NT), 0.0).astype(BF16)
            o_in = _dot(a, vb)
            for c in range(tc // CHUNK):
                rs = slice(c * CHUNK, (c + 1) * CHUNK)
                st = state[...]
                st_ref[c] = st
                os_[rs, :] = o_in[rs] + _dot(qt[rs], st.astype(BF16), NT)
                bl = bc[(c + 1) * CHUNK - 1:(c + 1) * CHUNK, :]
                kt = (k[rs] * jnp.exp2(bl - bc[rs])).astype(BF16)
                state[...] = st * jnp.exp2(bl) + _dot(vb[rs], kt, TN)

        @pl.when(jnp.logical_not(bounded))
        def _():
            qs[...] = q
            ks[...] = k
            bs[...] = _group_cumsum(tri_ref[...], lf)
            rows = lax.broadcasted_iota(jnp.int32, (HALF, 1), 0)

            def step(i, carry):
                r0 = pl.multiple_of(i * SUB, SUB)
                q_ = qs[pl.ds(r0, SUB), :]
                k_ = ks[pl.ds(r0, SUB), :]
                b_ = bs[pl.ds(r0, SUB), :]
                v_ = v_ref[pl.ds(r0, SUB), :]
                st = state[...]

                @pl.when(i % (CHUNK // SUB) == 0)
                def _():
                    st_ref[i // (CHUNK // SUB)] = st

                bl = b_[SUB - 1:SUB, :]
                o = _dot((q_ * jnp.exp2(b_)).astype(BF16), st.astype(BF16), NT)
                (q_lo, q_hi), (b_lo, b_hi), (o_lo, o_hi) = ((x[:HALF], x[HALF:]) for x in (q_, b_, o))
                for j in range(SUB):
                    bj, kj, vj = b_[j:j + 1, :], k_[j:j + 1, :], v_[j:j + 1, :]
                    if j < HALF:
                        e = _decay(b_lo, bj, rows, j)
                        o_lo = o_lo + jnp.sum(q_lo * e * kj, axis=-1, keepdims=True) * vj
                    e = _decay(b_hi, bj, rows, j - HALF if j > HALF else None)
                    o_hi = o_hi + jnp.sum(q_hi * e * kj, axis=-1, keepdims=True) * vj
                os_[pl.ds(r0, HALF), :] = o_lo
                os_[pl.ds(r0 + HALF, HALF), :] = o_hi
                kt = (k_ * jnp.exp2(bl - b_)).astype(BF16)
                state[...] = st * jnp.exp2(bl) + _dot(v_.astype(BF16), kt, TN)
                return carry

            lax.fori_loop(0, nsub, step, 0, unroll=unroll)

        o = os_[...]
        o_ref[...] = o
        r = lax.rsqrt(jnp.mean(o * o, axis=-1, keepdims=True) + EPS)
        gp = g_ref[...]
        oa_ref[...] = (o * r * gn_ref[...] * (gp * _sigmoid(gp))).astype(BF16)

    col = lambda k: pl.BlockSpec((tc, hd), lambda h, t: (t, k * HGRN_HEADS + h))
    vec = pl.BlockSpec((None, 1, hd), lambda h, t: (h, 0, 0))
    x_specs, x_shapes, x_scratch = _plan_extras(plan)
    return pl.pallas_call(
        _fuse_exchange(body, 9, 3, 5, plan, 2), name="hgrn_fwd", grid=(HGRN_HEADS, s // tc),
        in_specs=[col(0), col(1), col(2), col(3), vec, vec, vec, _const((tc, tc)), _const((tc, tc))] + x_specs,
        out_specs=[pl.BlockSpec((tc, hd), lambda h, t: (t, h)), pl.BlockSpec((tc, hd), lambda h, t: (t, h)),
                   pl.BlockSpec((None, tc // CHUNK, hd, hd), lambda h, t: (h, t, 0, 0))] + x_specs,
        out_shape=[jax.ShapeDtypeStruct((s, HGRN_WIDTH), F32), jax.ShapeDtypeStruct((s, HGRN_WIDTH), BF16),
                   jax.ShapeDtypeStruct((HGRN_HEADS, s // CHUNK, hd, hd), F32)] + x_shapes,
        scratch_shapes=[pltpu.VMEM((hd, hd), F32)] + [pltpu.VMEM((tc, hd), F32)] * 4 + x_scratch,
        compiler_params=_params(("arbitrary", "arbitrary")),
    )(z, z, z, z, l0, l1, gn, _block_tri(tc, SUB, False), _block_tri(tc, CHUNK, False), *plan_args)


def _pooled(p, ext, tok0):
    tm = p.shape[0]
    tok = tok0 + lax.broadcasted_iota(jnp.int32, (tm, 1), 0)
    outs = []
    for g, w in enumerate(POOL_WINDOWS):
        acc = ext[:, g * HEAD_DIM:(g + 1) * HEAD_DIM]
        sh = 1
        while sh < w:
            acc = acc + pltpu.roll(acc, sh, axis=0)
            sh *= 2
        cnt = jnp.minimum(tok + 1, w).astype(F32)
        outs.append(acc[POOL_HALO:, :] / cnt - p[:, g * HEAD_DIM:(g + 1) * HEAD_DIM])
    return outs


def _pool_fwd(z, wp, scale, tm):
    s = z.shape[0]
    pw = len(POOL_WINDOWS) * HEAD_DIM
    nb = tm // POOL_HALO

    def body(p_ref, prev_ref, wp_ref, sc_ref, ob_ref):
        t = pl.program_id(0)
        p = p_ref[...]
        prev = jnp.where(t > 0, prev_ref[...], 0.0)
        pooled = _pooled(p, jnp.concatenate([prev, p], axis=0), t * tm)
        ys = [_dot(pooled[g].astype(BF16), wp_ref[g].astype(BF16)) for g in range(len(POOL_WINDOWS))]
        ob_ref[...] = (jnp.concatenate(ys, axis=1) * sc_ref[...]).astype(BF16)

    return pl.pallas_call(
        body, name="pool_fwd", grid=(s // tm,),
        in_specs=[pl.BlockSpec((tm, pw), lambda t: (t, 4)),
                  pl.BlockSpec((POOL_HALO, pw), lambda t: (jnp.maximum(t * nb - 1, 0), 4)),
                  _const(wp.shape), _const((1, pw))],
        out_specs=pl.BlockSpec((tm, pw), lambda t: (t, 0)),
        out_shape=jax.ShapeDtypeStruct((s, pw), BF16),
        compiler_params=_params(("parallel",)),
    )(z, z, wp, scale)


def _kv_proj(mem, g, slab_g):
    m, d = mem.shape
    rows = d // N_CHIPS

    def body(mem_ref, g_ref, wk_ref, wv_ref, xk_ref, xv_ref):
        hm, _, _ = _rms_fwd(mem_ref[...], g_ref[...])
        hb = hm.astype(BF16)
        xk_ref[...] = _dot(hb, wk_ref[...].reshape(d, d)).astype(BF16)
        xv_ref[...] = _dot(hb, wv_ref[...].reshape(d, d)).astype(BF16)

    blk = lambda k: pl.BlockSpec((N_CHIPS, rows, d), lambda i: (0, k, 0))
    return pl.pallas_call(
        body, name="kv_proj", grid=(1,),
        in_specs=[_const((m, d)), _const((1, d)), blk(2), blk(3)],
        out_specs=[_const_out((m, d)), _const_out((m, d))],
        out_shape=[jax.ShapeDtypeStruct((m, d), BF16)] * 2,
        compiler_params=_params(("arbitrary",)),
    )(mem, g, slab_g, slab_g)


def _softmax_rows(sc):
    e = jnp.exp(sc - jnp.max(sc, axis=-1, keepdims=True))
    return e / jnp.sum(e, axis=-1, keepdims=True)


def _mix_xattn_fwd(x, oa, ob, gx, slab_g, wo_g, xk, xv, tm, plan=None, plan_args=()):
    s, d = x.shape
    m = xk.shape[0]
    rows = d // N_CHIPS
    hw = oa.shape[1]
    e = XATTN_HEAD_DIM

    def body(x_ref, oa_ref, ob_ref, gx_ref, wout_ref, wq_ref, wo_ref, xk_ref, xv_ref,
             x1_ref, mixed_ref, hq_ref, xq_ref, att_ref, x2_ref):
        mixed = jnp.concatenate([oa_ref[...], ob_ref[...]], axis=1)
        mixed_ref[...] = mixed
        x1 = x_ref[...] + _dot(mixed, wout_ref[...].reshape(d, d))
        x1_ref[...] = x1
        hq, _, _ = _rms_fwd(x1, gx_ref[...])
        hqb = hq.astype(BF16)
        hq_ref[...] = hqb
        xq = _dot(hqb, wq_ref[...].reshape(d, d)).astype(BF16)
        xq_ref[...] = xq
        atts = []
        for h in range(XATTN_HEADS):
            cs = slice(h * e, (h + 1) * e)
            p = _softmax_rows(_dot(xq[:, cs], xk_ref[:, cs], NT) * (e ** -0.5))
            atts.append(_dot(p.astype(BF16), xv_ref[:, cs]).astype(BF16))
        att = jnp.concatenate(atts, axis=1)
        att_ref[...] = att
        for j in range(N_CHIPS):
            x2_ref[:, j * rows:(j + 1) * rows] = x1[:, j * rows:(j + 1) * rows] + _dot(att, wo_ref[j])

    tile = lambda w: pl.BlockSpec((tm, w), lambda t: (t, 0))
    blk = lambda k: pl.BlockSpec((N_CHIPS, rows, d), lambda t: (0, k, 0), pipeline_mode=pl.Buffered(1))
    x_specs, x_shapes, x_scratch = _plan_extras(plan)
    return pl.pallas_call(
        _fuse_exchange(body, 9, 6, 0, plan, 1), name="mix_xattn_fwd", grid=(s // tm,),
        in_specs=[tile(d), tile(hw), tile(hw), _const((1, d)), blk(0), blk(1), _const(wo_g.shape),
                  _const((m, d)), _const((m, d))] + x_specs,
        out_specs=[tile(d)] * 6 + x_specs,
        out_shape=[jax.ShapeDtypeStruct((s, d), F32)] + [jax.ShapeDtypeStruct((s, d), BF16)] * 4
                  + [jax.ShapeDtypeStruct((s, d), F32)] + x_shapes,
        scratch_shapes=x_scratch,
        compiler_params=_params(("arbitrary",)),
    )(x, oa, ob, gx, slab_g, slab_g, wo_g, xk, xv, *plan_args)


def _mlp_loss_fwd(x2, gffn, gfin, w1_g, w2_g, target, tm):
    s, d = x2.shape
    wr = w1_g.shape[1]

    def body(x2_ref, gffn_ref, gfin_ref, w1_ref, w2_ref, tg_ref, a_ref, hf_ref, dx3_ref, dx3b_ref, st_ref):
        x2v = x2_ref[...]
        hf, _, _ = _rms_fwd(x2v, gffn_ref[...])
        hfb = hf.astype(BF16)
        hf_ref[...] = hfb
        acc = x2v
        for j in range(N_CHIPS):
            a = _dot(hfb, w1_ref[j])
            a_ref[:, j * wr:(j + 1) * wr] = a
            r = jnp.maximum(a, 0.0)
            acc = acc + _dot((r * r).astype(BF16), w2_ref[j])
        gf = gfin_ref[...]
        y, n, r3 = _rms_fwd(acc, gf)
        err = y - tg_ref[...]
        loss = 0.5 * jnp.sum(jnp.sum(err * err, axis=-1, keepdims=True) * (1.0 / d), axis=0, keepdims=True)
        dy = err * (1.0 / d)
        dx3, dgf = _rms_bwd(dy, n, r3, gf)
        dx3_ref[...] = dx3
        dx3b_ref[...] = dx3.astype(BF16)
        _acc_rows(st_ref, pl.program_id(0), [dgf, jnp.broadcast_to(loss, (1, d))])

    tile = lambda w: pl.BlockSpec((tm, w), lambda t: (t, 0))
    blk = lambda k: pl.BlockSpec((N_CHIPS, wr, d), lambda t: (0, k, 0), pipeline_mode=pl.Buffered(1))
    return pl.pallas_call(
        body, name="mlp_loss_fwd", grid=(s // tm,),
        in_specs=[tile(d), _const((1, d)), _const((1, d)), blk(0), blk(0), tile(d)],
        out_specs=[tile(N_CHIPS * wr), tile(d), tile(d), tile(d), _const_out((8, d))],
        out_shape=[jax.ShapeDtypeStruct((s, N_CHIPS * wr), F32), jax.ShapeDtypeStruct((s, d), BF16),
                   jax.ShapeDtypeStruct((s, d), F32), jax.ShapeDtypeStruct((s, d), BF16),
                   jax.ShapeDtypeStruct((8, d), F32)],
        compiler_params=_params(("arbitrary",)),
    )(x2, gffn, gfin, w1_g, w2_g, target)


def _mlp_bwd(dx3, dx3b, a, x2, gffn, w1_g, w2_g, tm):
    s, d = x2.shape
    wr = w1_g.shape[1]

    def body(dx3_ref, dx3b_ref, a_ref, x2_ref, g_ref, w1_ref, w2_ref, da_ref, u_ref, dx2_ref, dx2b_ref, st_ref):
        dyb = dx3b_ref[...]
        dhf = jnp.zeros((tm, d), F32)
        for j in range(N_CHIPS):
            r = jnp.maximum(a_ref[:, j * wr:(j + 1) * wr], 0.0)
            da = (_dot(dyb, w2_ref[j], NT) * (2.0 * r)).astype(BF16)
            da_ref[:, j * wr:(j + 1) * wr] = da
            u_ref[:, j * wr:(j + 1) * wr] = (r * r).astype(BF16)
            dhf = dhf + _dot(da, w1_ref[j], NT)
        g = g_ref[...]
        _, n, r2 = _rms_fwd(x2_ref[...], g)
        dxn, dg = _rms_bwd(dhf, n, r2, g)
        dx2 = dx3_ref[...] + dxn
        dx2_ref[...] = dx2
        dx2b_ref[...] = dx2.astype(BF16)
        _acc_rows(st_ref, pl.program_id(0), [dg])

    tile = lambda w: pl.BlockSpec((tm, w), lambda t: (t, 0))
    blk = lambda k: pl.BlockSpec((N_CHIPS, wr, d), lambda t: (0, k, 0), pipeline_mode=pl.Buffered(1))
    nf = N_CHIPS * wr
    return pl.pallas_call(
        body, name="mlp_bwd", grid=(s // tm,),
        in_specs=[tile(d), tile(d), tile(nf), tile(d), _const((1, d)), blk(0), blk(0)],
        out_specs=[tile(nf), tile(nf), tile(d), tile(d), _const_out((8, d))],
        out_shape=[jax.ShapeDtypeStruct((s, nf), BF16), jax.ShapeDtypeStruct((s, nf), BF16),
                   jax.ShapeDtypeStruct((s, d), F32), jax.ShapeDtypeStruct((s, d), BF16),
                   jax.ShapeDtypeStruct((8, d), F32)],
        compiler_params=_params(("arbitrary",)),
    )(dx3, dx3b, a, x2, gffn, w1_g, w2_g)


def _xattn_mix_bwd(dx2, x1, xq, xk, xv, gx, slab_g, wo_g, tm, plan=None, plan_args=()):
    s, d = x1.shape
    m = xk.shape[0]
    rows = d // N_CHIPS
    e = XATTN_HEAD_DIM

    def body(dx2_ref, x1_ref, xq_ref, xk_ref, xv_ref, gx_ref, wout_ref, wq_ref, wo_ref,
             dx1_ref, dx1b_ref, dxq_ref, dmix_ref, dxk_ref, dxv_ref, st_ref):
        t = pl.program_id(0)
        dx2 = dx2_ref[...]
        dx2b = dx2.astype(BF16)
        datt = jnp.zeros((tm, d), F32)
        for j in range(N_CHIPS):
            datt = datt + _dot(dx2b[:, j * rows:(j + 1) * rows], wo_ref[j], NT)
        dattb = datt.astype(BF16)
        dxqs, dxks, dxvs = [], [], []
        for h in range(XATTN_HEADS):
            cs = slice(h * e, (h + 1) * e)
            xq_h, xk_h, xv_h = xq_ref[:, cs], xk_ref[:, cs], xv_ref[:, cs]
            p = _softmax_rows(_dot(xq_h, xk_h, NT) * (e ** -0.5))
            dp = _dot(dattb[:, cs], xv_h, NT)
            ds = (p * (dp - jnp.sum(dp * p, axis=-1, keepdims=True)) * (e ** -0.5)).astype(BF16)
            dxqs.append(_dot(ds, xk_h).astype(BF16))
            dxks.append(_dot(ds, xq_h, TN))
            dxvs.append(_dot(p.astype(BF16), dattb[:, cs], TN))
        dxq = jnp.concatenate(dxqs, axis=1)
        dxq_ref[...] = dxq
        dxk = jnp.concatenate(dxks, axis=1)
        dxv = jnp.concatenate(dxvs, axis=1)

        @pl.when(t == 0)
        def _():
            dxk_ref[...] = dxk
            dxv_ref[...] = dxv

        @pl.when(t > 0)
        def _():
            dxk_ref[...] = dxk_ref[...] + dxk
            dxv_ref[...] = dxv_ref[...] + dxv

        dhq = jnp.concatenate([_dot(dxq, wq_ref[j], NT) for j in range(N_CHIPS)], axis=1)
        g = gx_ref[...]
        _, n, r1 = _rms_fwd(x1_ref[...], g)
        dxn, dg = _rms_bwd(dhq, n, r1, g)
        dx1 = dx2 + dxn
        dx1_ref[...] = dx1
        dx1b = dx1.astype(BF16)
        dx1b_ref[...] = dx1b
        for j in range(N_CHIPS):
            dmix_ref[:, j * rows:(j + 1) * rows] = _dot(dx1b, wout_ref[j], NT)
        _acc_rows(st_ref, t, [dg])

    tile = lambda: pl.BlockSpec((tm, d), lambda t: (t, 0))
    blk = lambda k: pl.BlockSpec((N_CHIPS, rows, d), lambda t: (0, k, 0), pipeline_mode=pl.Buffered(1))
    x_specs, x_shapes, x_scratch = _plan_extras(plan)
    return pl.pallas_call(
        _fuse_exchange(body, 9, 7, 0, plan, 1), name="xattn_mix_bwd", grid=(s // tm,),
        in_specs=[tile(), tile(), tile(), _const((m, d)), _const((m, d)), _const((1, d)), blk(0), blk(1),
                  _const(wo_g.shape)] + x_specs,
        out_specs=[tile(), tile(), tile(), tile(), _const_out((m, d)), _const_out((m, d)), _const_out((8, d))]
                  + x_specs,
        out_shape=[jax.ShapeDtypeStruct((s, d), F32), jax.ShapeDtypeStruct((s, d), BF16),
                   jax.ShapeDtypeStruct((s, d), BF16), jax.ShapeDtypeStruct((s, d), F32),
                   jax.ShapeDtypeStruct((m, d), F32), jax.ShapeDtypeStruct((m, d), F32),
                   jax.ShapeDtypeStruct((8, d), F32)] + x_shapes,
        scratch_shapes=x_scratch,
        compiler_params=_params(("arbitrary",)),
    )(dx2, x1, xq, xk, xv, gx, slab_g, slab_g, wo_g, *plan_args)


def _kv_bwd(mem, g, dxk, dxv, slab_g):
    m, d = mem.shape
    rows = d // N_CHIPS

    def body(mem_ref, g_ref, dxk_ref, dxv_ref, wk_ref, wv_ref, dwk_ref, dwv_ref, st_ref):
        gv = g_ref[...]
        hm, n, _ = _rms_fwd(mem_ref[...], gv)
        hb = hm.astype(BF16)
        dkb = dxk_ref[...].astype(BF16)
        dvb = dxv_ref[...].astype(BF16)
        dhm = []
        for j in range(N_CHIPS):
            hj = hb[:, j * rows:(j + 1) * rows]
            dwk_ref[j] = _dot(hj, dkb, TN)
            dwv_ref[j] = _dot(hj, dvb, TN)
            dhm.append(_dot(dkb, wk_ref[j], NT) + _dot(dvb, wv_ref[j], NT))
        dg = jnp.sum(jnp.concatenate(dhm, axis=1) * n, axis=0, keepdims=True)
        st_ref[...] = jnp.concatenate([dg, jnp.zeros((7, d), F32)], axis=0)

    blk = lambda k: pl.BlockSpec((N_CHIPS, rows, d), lambda i: (0, k, 0))
    return pl.pallas_call(
        body, name="kv_bwd", grid=(1,),
        in_specs=[_const((m, d)), _const((1, d)), _const((m, d)), _const((m, d)), blk(2), blk(3)],
        out_specs=[_const_out((N_CHIPS, rows, d)), _const_out((N_CHIPS, rows, d)), _const_out((8, d))],
        out_shape=[jax.ShapeDtypeStruct((N_CHIPS, rows, d), F32)] * 2 + [jax.ShapeDtypeStruct((8, d), F32)],
        compiler_params=_params(("arbitrary",)),
    )(mem, g, dxk, dxv, slab_g, slab_g)


def _pool_bwd(z, dmix, wp, scale, tm, plan=None, plan_args=()):
    s = z.shape[0]
    ng = len(POOL_WINDOWS)
    pw = ng * HEAD_DIM
    nb = tm // POOL_HALO
    nt = s // tm
    n_ext = tm + POOL_HALO

    def body(p_ref, prev_ref, dm_ref, dmn_ref, wp_ref, sc_ref, dp_ref, dwp_ref, st_ref):
        t = pl.program_id(0)
        p = p_ref[...]
        prev = jnp.where(t > 0, prev_ref[...], 0.0)
        pooled = _pooled(p, jnp.concatenate([prev, p], axis=0), t * tm)
        dm = dm_ref[...]
        dme = jnp.concatenate([dm, jnp.where(t < nt - 1, dmn_ref[...], 0.0)], axis=0) * sc_ref[...]
        tok = t * tm + lax.broadcasted_iota(jnp.int32, (n_ext, 1), 0)
        dsc, dps, dwps = [], [], []
        for g, w in enumerate(POOL_WINDOWS):
            cs = slice(g * HEAD_DIM, (g + 1) * HEAD_DIM)
            wpb = wp_ref[g].astype(BF16)
            pb = pooled[g].astype(BF16)
            dsc.append(jnp.sum(dm[:, cs] * _dot(pb, wpb), axis=0, keepdims=True))
            dye = dme[:, cs].astype(BF16)
            dwps.append(_dot(pb, dye[:tm], TN))
            dpe = _dot(dye, wpb, NT)
            acc = dpe / jnp.minimum(tok + 1, w).astype(F32)
            sh = 1
            while sh < w:
                acc = acc + pltpu.roll(acc, n_ext - sh, axis=0)
                sh *= 2
            dps.append(acc[:tm] - dpe[:tm])
        dp_ref[...] = jnp.concatenate(dps, axis=1)
        dsc_row = jnp.concatenate(dsc, axis=1)

        @pl.when(t == 0)
        def _():
            for g in range(ng):
                dwp_ref[g] = dwps[g]

        @pl.when(t > 0)
        def _():
            for g in range(ng):
                dwp_ref[g] = dwp_ref[g] + dwps[g]

        _acc_rows(st_ref, t, [dsc_row])

    x_specs, x_shapes, x_scratch = _plan_extras(plan)
    return pl.pallas_call(
        _fuse_exchange(body, 6, 3, 0, plan, 1), name="pool_bwd", grid=(nt,),
        in_specs=[pl.BlockSpec((tm, pw), lambda t: (t, 4)),
                  pl.BlockSpec((POOL_HALO, pw), lambda t: (jnp.maximum(t * nb - 1, 0), 4)),
                  pl.BlockSpec((tm, pw), lambda t: (t, 1)),
                  pl.BlockSpec((POOL_HALO, pw), lambda t: (jnp.minimum((t + 1) * nb, s // POOL_HALO - 1), 1)),
                  _const(wp.shape), _const((1, pw))] + x_specs,
        out_specs=[pl.BlockSpec((tm, pw), lambda t: (t, 0)), _const_out(wp.shape), _const_out((8, pw))] + x_specs,
        out_shape=[jax.ShapeDtypeStruct((s, pw), F32), jax.ShapeDtypeStruct(wp.shape, F32),
                   jax.ShapeDtypeStruct((8, pw), F32)] + x_shapes,
        scratch_shapes=x_scratch,
        compiler_params=_params(("arbitrary",)),
    )(z, z, dmix, dmix, wp, scale, *plan_args)


def _hgrn_bwd(z, o, dmix, st, l0, l1, gn, tc, unroll=1, plan=None, plan_args=()):
    s = z.shape[0]
    nsub = tc // SUB
    nt = s // tc
    hd = HEAD_DIM

    def body(q_ref, f_ref, v_ref, g_ref, l0_ref, l1_ref, gn_ref, o_ref, dm_ref, st_ref,
             tril_ref, triu_ref, trilc_ref, triuc_ref,
             dq_ref, df_ref, di_ref, dg_ref, stat_ref, dstate, qs, ks, bs, dos, dqs, dks, dbs, sts):
        t = pl.program_id(1)

        @pl.when(t == 0)
        def _():
            dstate[...] = jnp.zeros_like(dstate)

        lb = _lower_bound(l0_ref[...], l1_ref[...])
        qp = q_ref[...]
        q, sq, f, sf = _hgrn_gates(qp, f_ref[...], lb)
        k = 1.0 - f
        lf = jnp.log(f) * LOG2E
        bc = _group_cumsum(trilc_ref[...], lf)
        bounded = jnp.min(bc) >= -MAX_LOG2_GROWTH

        o = o_ref[...]
        r = lax.rsqrt(jnp.mean(o * o, axis=-1, keepdims=True) + EPS)
        n = o * r
        gnv = gn_ref[...]
        gp = g_ref[...]
        sg = _sigmoid(gp)
        dm = dm_ref[...]
        dg_ref[...] = dm * (n * gnv) * (sg * (1.0 + gp * (1.0 - sg)))
        don = dm * (gp * sg)
        dgn = jnp.sum(don * n, axis=0, keepdims=True)
        dn = don * gnv
        do_all = r * (dn - n * jnp.mean(dn * n, axis=-1, keepdims=True))

        @pl.when(bounded)
        def _():
            eb = jnp.exp2(bc)
            eib = jnp.exp2(-bc)
            qt = (q * eb).astype(BF16)
            ki = (k * eib).astype(BF16)
            vb = v_ref[...].astype(BF16)
            dob = do_all.astype(BF16)
            mask = trilc_ref[...] > 0
            a = jnp.where(mask, _dot(qt, ki, NT), 0.0).astype(BF16)
            da = jnp.where(mask, _dot(dob, vb, NT), 0.0).astype(BF16)
            dq_in = _dot(da, ki)
            dk_in = _dot(da, qt, TN)
            dv_in = _dot(a, dob, TN)
            last_row = lax.broadcasted_iota(jnp.int32, (CHUNK, 1), 0) == CHUNK - 1
            for c in reversed(range(tc // CHUNK)):
                rs = slice(c * CHUNK, (c + 1) * CHUNK)
                stp = st_ref[c]
                dst = dstate[...]
                dstb = dst.astype(BF16)
                bl = bc[(c + 1) * CHUNK - 1:(c + 1) * CHUNK, :]
                ekl = jnp.exp2(bl - bc[rs])
                ebl = jnp.exp2(bl)
                kt = k[rs] * ekl
                dq_st = _dot(dob[rs], stp.astype(BF16)) * eb[rs]
                dkt = _dot(vb[rs], dstb)
                extra = jnp.sum(kt * dkt, axis=0, keepdims=True) + ebl * jnp.sum(stp * dst, axis=0, keepdims=True)
                dqs[rs, :] = dq_st + dq_in[rs] * eb[rs]
                dks[rs, :] = dkt * ekl + dk_in[rs] * eib[rs]
                di_ref[rs, :] = _dot(kt.astype(BF16), dstb, NT) + dv_in[rs]
                dbs[rs, :] = (q[rs] * dq_st - kt * dkt + jnp.where(last_row, extra, 0.0)
                              + (qt[rs].astype(F32) * dq_in[rs] - ki[rs].astype(F32) * dk_in[rs]))
                dstate[...] = dst * ebl + _dot(dob[rs], qt[rs], TN)
            dbs[...] = _group_cumsum(triuc_ref[...], dbs[...])

        @pl.when(jnp.logical_not(bounded))
        def _():
            qs[...] = q
            ks[...] = k
            bs[...] = _group_cumsum(tril_ref[...], lf)
            dos[...] = do_all
            per = CHUNK // SUB

            def restore(i, carry):
                @pl.when(i % per == 0)
                def _():
                    sts[i] = st_ref[i // per]

                @pl.when(i % per != 0)
                def _():
                    rp = pl.multiple_of((i - 1) * SUB, SUB)
                    b_ = bs[pl.ds(rp, SUB), :]
                    bl = b_[SUB - 1:SUB, :]
                    kt = (ks[pl.ds(rp, SUB), :] * jnp.exp2(bl - b_)).astype(BF16)
                    sts[i] = sts[i - 1] * jnp.exp2(bl) + _dot(v_ref[pl.ds(rp, SUB), :].astype(BF16), kt, TN)

                return carry

            lax.fori_loop(0, nsub, restore, 0)
            rows = lax.broadcasted_iota(jnp.int32, (HALF, 1), 0)
            last_row = lax.broadcasted_iota(jnp.int32, (SUB, 1), 0) == SUB - 1

            def step(i, carry):
                ii = nsub - 1 - i
                r0 = pl.multiple_of(ii * SUB, SUB)
                q_ = qs[pl.ds(r0, SUB), :]
                k_ = ks[pl.ds(r0, SUB), :]
                b_ = bs[pl.ds(r0, SUB), :]
                v_ = v_ref[pl.ds(r0, SUB), :]
                do_ = dos[pl.ds(r0, SUB), :]
                stp = sts[ii]
                dst = dstate[...]
                bl = b_[SUB - 1:SUB, :]
                eb = jnp.exp2(b_)
                ekl = jnp.exp2(bl - b_)
                ebl = jnp.exp2(bl)
                dob = do_.astype(BF16)
                dstb = dst.astype(BF16)
                kt = k_ * ekl
                dq = _dot(dob, stp.astype(BF16)) * eb
                dkt = _dot(v_.astype(BF16), dstb)
                dk = dkt * ekl
                dv = _dot(kt.astype(BF16), dstb, NT)
                extra = jnp.sum(kt * dkt, axis=0, keepdims=True) + ebl * jnp.sum(stp * dst, axis=0, keepdims=True)
                halves = lambda x: [x[:HALF], x[HALF:]]
                q_h, b_h, do_h, dq_h, dk_h, dv_h = (halves(x) for x in (q_, b_, do_, dq, dk, dv))
                for own in range(2):
                    dk_rows, dv_rows = _RowSums(rows), _RowSums(rows)
                    for jj in _RowSums.ORDER:
                        j = own * HALF + jj
                        bj, kj, vj = b_[j:j + 1, :], k_[j:j + 1, :], v_[j:j + 1, :]
                        dk_sum = dv_sum = None
                        for h in range(own, 2):
                            e = _decay(b_h[h], bj, rows, jj if h == own else None)
                            pe = q_h[h] * e
                            acol = jnp.sum(pe * kj, axis=-1, keepdims=True)
                            dacol = jnp.sum(do_h[h] * vj, axis=-1, keepdims=True)
                            dq_h[h] = dq_h[h] + dacol * (e * kj)
                            dk_sum = dacol * pe if dk_sum is None else dk_sum + dacol * pe
                            dv_sum = acol * do_h[h] if dv_sum is None else dv_sum + acol * do_h[h]
                        dk_rows.push(jj, dk_sum)
                        dv_rows.push(jj, dv_sum)
                    dk_h[own] = dk_h[own] + dk_rows.result()
                    dv_h[own] = dv_h[own] + dv_rows.result()
                dq, dk, dv = (jnp.concatenate(x, axis=0) for x in (dq_h, dk_h, dv_h))
                dqs[pl.ds(r0, SUB), :] = dq
                dks[pl.ds(r0, SUB), :] = dk
                di_ref[pl.ds(r0, SUB), :] = dv
                dbs[pl.ds(r0, SUB), :] = q_ * dq - k_ * dk + jnp.where(last_row, extra, 0.0)
                dstate[...] = dst * ebl + _dot(dob, (q_ * eb).astype(BF16), TN)
                return carry

            lax.fori_loop(0, nsub, step, 0, unroll=unroll)
            dbs[...] = _group_cumsum(triu_ref[...], dbs[...])

        dlf = dbs[...]
        dfv = dlf / f - dks[...]
        df_ref[...] = dfv * (1.0 - lb) * sf * (1.0 - sf)
        dlb = jnp.sum(dfv * (1.0 - sf), axis=0, keepdims=True)
        dq_ref[...] = dqs[...] * (sq * (1.0 + qp * (1.0 - sq)))
        _acc_rows(stat_ref, t, [dgn, dlb])

    rev = lambda t: nt - 1 - t
    col = lambda k: pl.BlockSpec((tc, hd), lambda h, t: (rev(t), k * HGRN_HEADS + h))
    vec = pl.BlockSpec((None, 1, hd), lambda h, t: (h, 0, 0))
    head = pl.BlockSpec((tc, hd), lambda h, t: (rev(t), h))
    x_specs, x_shapes, x_scratch = _plan_extras(plan)
    return pl.pallas_call(
        _fuse_exchange(body, 14, 5, 9, plan, 2), name="hgrn_bwd", grid=(HGRN_HEADS, nt),
        in_specs=[col(0), col(1), col(2), col(3), vec, vec, vec, head, head,
                  pl.BlockSpec((None, tc // CHUNK, hd, hd), lambda h, t: (h, rev(t), 0, 0))]
                 + [_const((tc, tc))] * 4 + x_specs,
        out_specs=[head, head, head, head, pl.BlockSpec((None, 8, hd), lambda h, t: (h, 0, 0))] + x_specs,
        out_shape=[jax.ShapeDtypeStruct((s, HGRN_WIDTH), F32)] * 4 + [jax.ShapeDtypeStruct((HGRN_HEADS, 8, hd), F32)]
                  + x_shapes,
        scratch_shapes=[pltpu.VMEM((hd, hd), F32)] + [pltpu.VMEM((tc, hd), F32)] * 7
                       + [pltpu.VMEM((nsub, hd, hd), F32)] + x_scratch,
        compiler_params=_params(("arbitrary", "arbitrary")),
    )(z, z, z, z, l0, l1, gn, o, dmix, st, _block_tri(tc, SUB, False), _block_tri(tc, SUB, True),
      _block_tri(tc, CHUNK, False), _block_tri(tc, CHUNK, True), *plan_args)


def _in_bwd(dparts, dx1, x, g, win_g, tm, plan=None, plan_args=()):
    s, d = x.shape
    nsh, _, wc = win_g.shape
    pw = dparts[0].shape[1]

    def body(dq_ref, df_ref, di_ref, dg_ref, dp_ref, dx1_ref, x_ref, g_ref, w_ref, gx_ref, dz_ref, st_ref):
        dz = jnp.concatenate([dq_ref[...], df_ref[...], di_ref[...], dg_ref[...], dp_ref[...]], axis=1).astype(BF16)
        dz_ref[...] = dz
        dh = jnp.zeros((tm, d), F32)
        for j in range(nsh):
            dh = dh + _dot(dz[:, j * wc:(j + 1) * wc], w_ref[j], NT)
        gv = g_ref[...]
        _, n, r = _rms_fwd(x_ref[...], gv)
        dxn, dg = _rms_bwd(dh, n, r, gv)
        gx_ref[...] = dx1_ref[...] + dxn
        _acc_rows(st_ref, pl.program_id(0), [dg])

    tile = lambda w: pl.BlockSpec((tm, w), lambda t: (t, 0))
    x_specs, x_shapes, x_scratch = _plan_extras(plan)
    return pl.pallas_call(
        _fuse_exchange(body, 9, 3, 0, plan, 1), name="in_bwd", grid=(s // tm,),
        in_specs=[tile(pw)] * 5 + [tile(d), tile(d), _const((1, d)), _const(win_g.shape)] + x_specs,
        out_specs=[tile(d), tile(nsh * wc), _const_out((8, d))] + x_specs,
        out_shape=[jax.ShapeDtypeStruct((s, d), F32), jax.ShapeDtypeStruct((s, nsh * wc), BF16),
                   jax.ShapeDtypeStruct((8, d), F32)] + x_shapes,
        scratch_shapes=x_scratch,
        compiler_params=_params(("arbitrary",)),
    )(*dparts, dx1, x, g, win_g, *plan_args)


def _tn_grad(name, a, b, out_rows, out_cols, a_sharded, tr, tc, plan=None, plan_args=()):
    s = a.shape[0]
    nr, nc = out_rows // tr, out_cols // tc

    def body(a_ref, b_ref, o_ref):
        o_ref[...] = _dot(a_ref[...], b_ref[...], TN)

    a_map = (lambda j, i, k: (0, j * nr + i)) if a_sharded else (lambda j, i, k: (0, i))
    b_map = (lambda j, i, k: (0, k)) if a_sharded else (lambda j, i, k: (0, j * nc + k))
    x_specs, x_shapes, x_scratch = _plan_extras(plan)
    res = pl.pallas_call(
        _fuse_exchange(body, 2, 1, 0, plan, 3), name=name, grid=(N_CHIPS, nr, nc),
        in_specs=[pl.BlockSpec((s, tr), a_map), pl.BlockSpec((s, tc), b_map)] + x_specs,
        out_specs=[pl.BlockSpec((None, tr, tc), lambda j, i, k: (j, i, k))] + x_specs,
        out_shape=[jax.ShapeDtypeStruct((N_CHIPS, out_rows, out_cols), F32)] + x_shapes,
        scratch_shapes=x_scratch,
        compiler_params=_params(("arbitrary", "arbitrary", "arbitrary")),
    )(a, b, *plan_args)
    return res if plan else res[0]


FFN_NAMES = ("w_ff1", "w_ff2")
ATTN_NAMES = ("w_xo", "w_xq", "w_out", "w_xk", "w_xv")
EARLY_NAMES = FFN_NAMES + ATTN_NAMES
BIG_NAMES = EARLY_NAMES + ("w_in",)


def _halved(g):
    return g.reshape(N_CHIPS, 2, g.shape[1] // 2, g.shape[2])


def _pair_adds(names, gs, got, idx):
    pairs = [_grad_pair_add("grad_pair_add_" + k, g, r, idx, tr=min(256, g.shape[2])) for k, g, r in zip(names, gs, got)]
    return [p[0] for p in pairs], [p[1] for p in pairs]


def _step(x, mem, target, small, shards, idx):
    d = x.shape[1]
    l0 = small["lb_logits"][0].reshape(HGRN_HEADS, 1, HEAD_DIM)
    l1 = small["lb_logits"][1].reshape(HGRN_HEADS, 1, HEAD_DIM)
    gn = small["hgrn_norm_g"].reshape(HGRN_HEADS, 1, HEAD_DIM)
    wp = small["w_pool"].reshape(len(POOL_WINDOWS), HEAD_DIM, HEAD_DIM)
    psc = small["pool_scale"].reshape(1, -1)
    gmix, gx, gmem, gffn = (small[k].reshape(1, d) for k in ("norm_mix_g", "norm_x_g", "norm_mem_g", "norm_ffn_g"))
    gfin = small["final_norm_g"].reshape(1, d)

    (win_g,) = _run_exchange("gather_w_in", _WeightGather([shards["w_in"]]), [shards["w_in"]])
    z, h = _in_proj(x, gmix, win_g, tm=512)
    early_w = [shards["slab_attn"], shards["w_xo"], shards["w_ff1"]]
    o, oa, st, slab_g, wo_g, w1_g = _hgrn_fwd(z, l0, l1, gn, tc=256, unroll=8,
                                              plan=_WeightGather(early_w), plan_args=early_w)
    ob = _pool_fwd(z, wp, psc, tm=512)
    xk, xv = _kv_proj(mem, gmem, slab_g)
    late_w = [shards["w_ff2"]]
    x1, mixed, hq, xq, att, x2, w2_g = _mix_xattn_fwd(x, oa, ob, gx, slab_g, wo_g, xk, xv, tm=512,
                                                      plan=_WeightGather(late_w), plan_args=late_w)
    a, hf, dx3, dx3b, st_loss = _mlp_loss_fwd(x2, gffn, gfin, w1_g, w2_g, target, tm=512)

    da, u, dx2, dx2b, st_ffn = _mlp_bwd(dx3, dx3b, a, x2, gffn, w1_g, w2_g, tm=256)
    g_ff1 = [_halved(_tn_grad("dw_ff1", hf, da, d, d, False, 1024, 1024))]
    dw_ff2, *got = _tn_grad("dw_ff2", u, dx3b, d, d, True, 1024, 1024, plan=_PairExchange(g_ff1), plan_args=g_ff1)
    keep_ff1, send_ff1 = _pair_adds(("w_ff1",), g_ff1, got, idx)
    g_ff2 = [_halved(dw_ff2)]
    dx1, dx1b, dxq, dmix, dxk, dxv, st_x, *got = _xattn_mix_bwd(
        dx2, x1, xq, xk, xv, gx, slab_g, wo_g, tm=512,
        plan=_Plans([_ChipExchange(send_ff1), _PairExchange(g_ff2)]), plan_args=send_ff1 + g_ff2)
    recv_ff1 = got[:1]
    keep_ff2, send_ff2 = _pair_adds(("w_ff2",), g_ff2, got[1:], idx)
    dw = {}
    dw["w_xo"] = _tn_grad("dw_xo", att, dx2b, d, d // N_CHIPS, False, 1024, 256)
    dw["w_xq"] = _tn_grad("dw_xq", hq, dxq, d // N_CHIPS, d, True, 256, 1024)
    dw["w_out"] = _tn_grad("dw_out", mixed, dx1b, d // N_CHIPS, d, True, 256, 1024)
    dw["w_xk"], dw["w_xv"], st_mem = _kv_bwd(mem, gmem, dxk, dxv, slab_g)
    gs_attn = [_halved(dw[k]) for k in ATTN_NAMES]
    dp, d_wp, st_pool, *got_attn = _pool_bwd(z, dmix, wp, psc, tm=512, plan=_PairExchange(gs_attn), plan_args=gs_attn)
    keep_attn, send_attn = _pair_adds(ATTN_NAMES, gs_attn, got_attn, idx)
    dq, df, di, dg, st_hgrn, *recv_ff2 = _hgrn_bwd(z, o, dmix, st, l0, l1, gn, tc=256, unroll=4,
                                                    plan=_ChipExchange(send_ff2), plan_args=send_ff2)
    grad_x, dz, st_mix, *recv_attn = _in_bwd([dq, df, di, dg, dp], dx1, x, gmix, win_g, tm=512,
                                             plan=_ChipExchange(send_attn), plan_args=send_attn)
    keeps = keep_ff1 + keep_ff2 + keep_attn
    received = recv_ff1 + recv_ff2 + recv_attn
    gs_in = [_halved(_tn_grad("dw_in", h, dz, d, win_g.shape[2], False, 1024, win_g.shape[2]))]
    got_in = _run_exchange("grad_pair_exchange_w_in", _PairExchange(gs_in), gs_in)
    keep_in, send_in = _pair_adds(("w_in",), gs_in, got_in, idx)
    recv_in = _run_exchange("grad_chip_exchange_w_in", _ChipExchange(send_in), send_in)

    partials = dict(zip(BIG_NAMES, zip(keeps + keep_in, list(received) + list(recv_in))))
    stats = dict(mix=st_mix, x=st_x, mem=st_mem, ffn=st_ffn, loss=st_loss, hgrn=st_hgrn, pool=st_pool)
    return grad_x, stats, d_wp, partials


def _place():
    x, y, c = lax.axis_index("x"), lax.axis_index("y"), lax.axis_index("c")
    return x, y, c, [(x, 1 - y), (1 - x, y), (1 - x, 1 - y)]


def _rcopy(src, dst, ssem, rsem, dev):
    return pltpu.make_async_remote_copy(src_ref=src, dst_ref=dst, send_sem=ssem, recv_sem=rsem,
                                        device_id=dev, device_id_type=MESH)


class _WeightGather:
    def __init__(self, shards):
        self.n = len(shards)
        self.rows = [w.shape[0] for w in shards]
        self.out_shape = [jax.ShapeDtypeStruct((N_CHIPS,) + w.shape, w.dtype) for w in shards]
        self.scratch_shapes = [pltpu.SemaphoreType.DMA((self.n,))] * 2 + [pltpu.SemaphoreType.DMA((self.n, 3))] * 4

    def _copies(self, ins, outs, sems, with_pass_on):
        lsem, lrsem, ssem, rsem, fsem, frsem = sems
        x, y, c, peers = _place()
        chip = 2 * x + y
        sib = (x, y, 1 - c)
        own = [_rcopy(ins[a], outs[a].at[chip], lsem.at[a], lrsem.at[a], sib) for a in range(self.n)]
        sends, arrived, passed, passed_in = [], [], [], []
        for a in range(self.n):
            hr = self.rows[a] // 2
            half = lambda who, hc, a=a, hr=hr: outs[a].at[who, pl.ds(hc * hr, hr), :]
            for r, (px, py) in enumerate(peers):
                pc = 2 * px + py
                sends.append(_rcopy(ins[a].at[pl.ds(c * hr, hr), :], half(chip, c), ssem.at[a, r], rsem.at[a, r],
                                    (px, py, c)))
                if with_pass_on:
                    arrived.append(_rcopy(half(pc, c), half(pc, c), ssem.at[a, r], rsem.at[a, r], (px, py, c)))
                    passed.append(_rcopy(half(pc, c), half(pc, c), fsem.at[a, r], frsem.at[a, r], sib))
                    passed_in.append(_rcopy(half(pc, 1 - c), half(pc, 1 - c), fsem.at[a, r], frsem.at[a, r], sib))
        return own, sends, arrived, passed, passed_in

    def start(self, ins, outs, sems):
        own, sends, _, _, _ = self._copies(ins, outs, sems, False)
        for cp in own + sends:
            cp.start()

    def finish(self, ins, outs, sems):
        own, sends, arrived, passed, passed_in = self._copies(ins, outs, sems, True)
        for got, fwd in zip(arrived, passed):
            got.wait_recv()
            fwd.start()
        for cp in passed_in:
            cp.wait_recv()
        for cp in sends + passed:
            cp.wait_send()
        for cp in own:
            cp.wait()


class _ChipExchange:
    def __init__(self, sends):
        self.n = len(sends)
        self.out_shape = [jax.ShapeDtypeStruct(g.shape, g.dtype) for g in sends]
        self.scratch_shapes = [pltpu.SemaphoreType.DMA((self.n, 3))] * 2

    def _copies(self, ins, outs, sems):
        ssem, rsem = sems
        _, _, c, peers = _place()
        return [_rcopy(ins[a].at[r], outs[a].at[r], ssem.at[a, r], rsem.at[a, r], (px, py, c))
                for a in range(self.n) for r, (px, py) in enumerate(peers)]

    def start(self, ins, outs, sems):
        for cp in self._copies(ins, outs, sems):
            cp.start()

    def finish(self, ins, outs, sems):
        for cp in self._copies(ins, outs, sems):
            cp.wait()


class _Plans:
    def __init__(self, plans):
        self.plans = plans
        self.n = sum(p.n for p in plans)
        self.out_shape = [s for p in plans for s in p.out_shape]
        self.scratch_shapes = [s for p in plans for s in p.scratch_shapes]

    def _each(self, ins, outs, sems):
        a = b = 0
        for p in self.plans:
            ns = len(p.scratch_shapes)
            yield p, ins[a:a + p.n], outs[a:a + p.n], sems[b:b + ns]
            a, b = a + p.n, b + ns

    def start(self, ins, outs, sems):
        for p, i, o, s in self._each(ins, outs, sems):
            p.start(i, o, s)

    def finish(self, ins, outs, sems):
        for p, i, o, s in self._each(ins, outs, sems):
            p.finish(i, o, s)


def _run_exchange(name, plan, arrays):
    n = plan.n

    def body(*refs):
        ins, outs, sems = refs[:n], refs[n:2 * n], refs[2 * n:]
        plan.start(ins, outs, sems)
        plan.finish(ins, outs, sems)

    return pl.pallas_call(
        body, name=name, in_specs=[ANY] * n, out_specs=[ANY] * n,
        out_shape=plan.out_shape, scratch_shapes=plan.scratch_shapes,
    )(*arrays)


class _PairExchange:
    def __init__(self, gs):
        self.n = len(gs)
        self.out_shape = [jax.ShapeDtypeStruct((g.shape[0],) + g.shape[2:], g.dtype) for g in gs]
        self.scratch_shapes = [pltpu.SemaphoreType.DMA((self.n,))] * 2

    def _copies(self, ins, outs, sems):
        ssem, rsem = sems
        x, y, c, _ = _place()
        return [_rcopy(ins[a].at[:, 1 - c], outs[a], ssem.at[a], rsem.at[a], (x, y, 1 - c)) for a in range(self.n)]

    def start(self, ins, outs, sems):
        for cp in self._copies(ins, outs, sems):
            cp.start()

    def finish(self, ins, outs, sems):
        for cp in self._copies(ins, outs, sems):
            cp.wait()


def _grad_pair_add(name, g, got, idx, tr):
    _, _, hr, cc = g.shape

    def body(idx_ref, g0, g1, g2, g3, r0, r1, r2, r3, keep_ref, send_ref):
        keep_ref[...] = g0[...] + r0[...]
        for q, (gq, rq) in enumerate(((g1, r1), (g2, r2), (g3, r3))):
            send_ref[q] = (gq[...] + rq[...]).astype(BF16)

    gspec = lambda q: pl.BlockSpec((None, None, tr, cc), lambda i, idx: (idx[1 + q], idx[0], i, 0))
    rspec = lambda q: pl.BlockSpec((None, tr, cc), lambda i, idx: (idx[1 + q], i, 0))
    return pl.pallas_call(
        body, name=name,
        grid_spec=pltpu.PrefetchScalarGridSpec(
            num_scalar_prefetch=1, grid=(hr // tr,),
            in_specs=[gspec(q) for q in range(4)] + [rspec(q) for q in range(4)],
            out_specs=[pl.BlockSpec((tr, cc), lambda i, idx: (i, 0)), pl.BlockSpec((3, tr, cc), lambda i, idx: (0, i, 0))]),
        out_shape=[jax.ShapeDtypeStruct((hr, cc), F32), jax.ShapeDtypeStruct((3, hr, cc), BF16)],
        compiler_params=_params(("parallel",)),
    )(idx, g, g, g, g, got, got, got, got)


def _grad_chip_add(name, keep, got, tr):
    hr, cc = keep.shape

    def body(k_ref, g_ref, o_ref):
        o_ref[...] = ((k_ref[...] + g_ref[0].astype(F32)) + g_ref[1].astype(F32)) + g_ref[2].astype(F32)

    return pl.pallas_call(
        body, name=name, grid=(hr // tr,),
        in_specs=[pl.BlockSpec((tr, cc), lambda i: (i, 0)), pl.BlockSpec((3, tr, cc), lambda i: (0, i, 0))],
        out_specs=pl.BlockSpec((tr, cc), lambda i: (i, 0)),
        out_shape=jax.ShapeDtypeStruct((hr, cc), F32),
        compiler_params=_params(("parallel",)),
    )(keep, got)


def _grad_half_exchange(ts):
    n = len(ts)

    def body(*refs):
        ins, outs, ssem, rsem = refs[:n], refs[n:2 * n], refs[2 * n], refs[2 * n + 1]
        x, y, c, _ = _place()
        cps = [_rcopy(ins[a], outs[a], ssem.at[a], rsem.at[a], (x, y, 1 - c)) for a in range(n)]
        for cp in cps:
            cp.start()
        for cp in cps:
            cp.wait()

    return pl.pallas_call(
        body, name="grad_half_exchange",
        in_specs=[ANY] * n, out_specs=[ANY] * n,
        out_shape=[jax.ShapeDtypeStruct(t.shape, t.dtype) for t in ts],
        scratch_shapes=[pltpu.SemaphoreType.DMA((n,))] * 2,
    )(*ts)


def _small_allreduce(stats, d_wp):
    d = D_MODEL
    half = d // 2
    wps = d_wp.shape

    def body(mix_ref, x_ref, mem_ref, ffn_ref, loss_ref, hg_ref, pool_ref, wp_ref, slab_out, wp_out,
             slab_buf, wp_buf, sib_s, sib_w, ssem, rsem):
        x, y, c, peers = _place()
        chip = 2 * x + y
        sib = (x, y, 1 - c)
        hgn = jnp.concatenate([hg_ref[h, 0:1, :] for h in range(HGRN_HEADS)], axis=1)
        dlb = jnp.concatenate([hg_ref[h, 1:2, :] for h in range(HGRN_HEADS)], axis=1)
        slab_buf[0] = jnp.concatenate([
            mix_ref[0:1, :], x_ref[0:1, :], mem_ref[0:1, :], ffn_ref[0:1, :], loss_ref[0:1, :],
            jnp.concatenate([dlb, hgn], axis=1),
            jnp.concatenate([pool_ref[0:1, :], jnp.zeros((1, half), F32)], axis=1),
            loss_ref[1:2, :]], axis=0)
        wp_buf[0] = wp_ref[...]
        pair = [_rcopy(slab_buf.at[0], sib_s, ssem.at[0], rsem.at[0], sib),
                _rcopy(wp_buf.at[0], sib_w, ssem.at[1], rsem.at[1], sib)]
        for cp in pair:
            cp.start()
        for cp in pair:
            cp.wait()
        slab_buf[0] = slab_buf[0] + sib_s[...]
        wp_buf[0] = wp_buf[0] + sib_w[...]
        cps = []
        for r, (px, py) in enumerate(peers):
            cps.append(_rcopy(slab_buf.at[0], slab_buf.at[r + 1], ssem.at[2 + 2 * r], rsem.at[2 + 2 * r], (px, py, c)))
            cps.append(_rcopy(wp_buf.at[0], wp_buf.at[r + 1], ssem.at[3 + 2 * r], rsem.at[3 + 2 * r], (px, py, c)))
        for cp in cps:
            cp.start()
        for cp in cps:
            cp.wait()
        tot_s, tot_w = slab_buf[chip], wp_buf[chip]
        for j in range(1, N_CHIPS):
            tot_s = tot_s + slab_buf[jnp.bitwise_xor(j, chip)]
            tot_w = tot_w + wp_buf[jnp.bitwise_xor(j, chip)]
        slab_out[...] = tot_s
        wp_out[...] = tot_w

    return pl.pallas_call(
        body, name="small_allreduce",
        in_specs=[VMEM] * 8, out_specs=[VMEM] * 2,
        out_shape=[jax.ShapeDtypeStruct((8, d), F32), jax.ShapeDtypeStruct(wps, F32)],
        scratch_shapes=[pltpu.VMEM((N_CHIPS, 8, d), F32), pltpu.VMEM((N_CHIPS,) + wps, F32),
                        pltpu.VMEM((8, d), F32), pltpu.VMEM(wps, F32),
                        pltpu.SemaphoreType.DMA((8,)), pltpu.SemaphoreType.DMA((8,))],
    )(stats["mix"], stats["x"], stats["mem"], stats["ffn"], stats["loss"], stats["hgrn"], stats["pool"], d_wp)


def _adamw_math(w, g, m, v):
    m = ADAM_B1 * m + (1.0 - ADAM_B1) * g
    v = ADAM_B2 * v + (1.0 - ADAM_B2) * (g * g)
    m_hat = m / (1.0 - ADAM_B1 ** ADAM_STEP)
    v_hat = v / (1.0 - ADAM_B2 ** ADAM_STEP)
    delta = -ADAM_LR * (m_hat / (jnp.sqrt(v_hat) + ADAM_EPS) + ADAM_WD * w)
    return delta, m, v


def _adamw(name, mine, theirs, w, m, v, idx, tr):
    rows = w.shape[0]
    cc = mine.shape[1]
    nb = rows // 2 // tr
    heads = w.shape[1] if w.ndim == 3 else 1
    e = cc // heads

    def body(idx_ref, a_ref, b_ref, w_ref, m_ref, v_ref, g_out, d_out, m_out, v_out):
        g = jnp.where(pl.program_id(0) // nb == idx_ref[0], a_ref[...], b_ref[...])
        if w.ndim == 2:
            g_out[...] = g
            d_out[...], m_out[...], v_out[...] = _adamw_math(w_ref[...], g, m_ref[...], v_ref[...])
        else:
            for h in range(heads):
                gh = g[:, h * e:(h + 1) * e]
                g_out[:, h, :] = gh
                d_out[:, h, :], m_out[:, h, :], v_out[:, h, :] = _adamw_math(
                    w_ref[:, h, :], gh, m_ref[:, h, :], v_ref[:, h, :])

    hspec = pl.BlockSpec((tr, cc), lambda i, idx: (i % nb, 0))
    spec = pl.BlockSpec((tr,) + w.shape[1:], lambda i, idx: (i,) + (0,) * (w.ndim - 1))
    return pl.pallas_call(
        body, name=name,
        grid_spec=pltpu.PrefetchScalarGridSpec(
            num_scalar_prefetch=1, grid=(rows // tr,),
            in_specs=[hspec, hspec, spec, spec, spec], out_specs=[spec] * 4),
        out_shape=[jax.ShapeDtypeStruct(w.shape, F32)] * 4,
        compiler_params=_params(("parallel",)),
    )(idx, mine, theirs, w, m, v)


SMALL_NAMES = ("norm_mix_g", "lb_logits", "hgrn_norm_g", "w_pool", "pool_scale", "norm_x_g", "norm_mem_g",
               "norm_ffn_g", "final_norm_g")


def _small_update(slab, d_wp, ws, ms, vs):
    n = len(SMALL_NAMES)
    half = D_MODEL // 2

    def body(slab_ref, wp_ref, *refs):
        w_refs, m_refs, v_refs, outs = refs[:n], refs[n:2 * n], refs[2 * n:3 * n], refs[3 * n:]
        row = lambda k: slab_ref[k:k + 1, :]
        lbl = w_refs[SMALL_NAMES.index("lb_logits")][...]
        s0 = _lower_bound(lbl[0:1, :], lbl[1:2, :])
        dl0 = row(ROW_LB_HGN)[:, :half] * s0 * (1.0 - s0)
        grads = dict(norm_mix_g=row(ROW_GMIX), lb_logits=jnp.concatenate([dl0, -dl0], axis=0),
                     hgrn_norm_g=row(ROW_LB_HGN)[:, half:], w_pool=wp_ref[...], pool_scale=row(ROW_PSCALE)[:, :half],
                     norm_x_g=row(ROW_GX), norm_mem_g=row(ROW_GMEM), norm_ffn_g=row(ROW_GFFN),
                     final_norm_g=row(ROW_GFIN))
        outs[0][...] = row(ROW_LOSS)[:, :128]
        for i, name in enumerate(SMALL_NAMES):
            g = grads[name]
            delta, m2, v2 = _adamw_math(w_refs[i][...], g, m_refs[i][...], v_refs[i][...])
            for o, val in zip(outs[1 + 4 * i:5 + 4 * i], (g, delta, m2, v2)):
                o[...] = val

    args = [ws[k] for k in SMALL_NAMES] + [ms[k] for k in SMALL_NAMES] + [vs[k] for k in SMALL_NAMES]
    out_shape = [jax.ShapeDtypeStruct((1, 128), F32)]
    for k in SMALL_NAMES:
        out_shape += [jax.ShapeDtypeStruct(ws[k].shape, F32)] * 4
    res = pl.pallas_call(
        body, name="small_update",
        in_specs=[VMEM] * (2 + 3 * n), out_specs=[VMEM] * len(out_shape), out_shape=out_shape,
    )(slab, d_wp, *args)
    return res[0], {k: res[1 + 4 * i:5 + 4 * i] for i, k in enumerate(SMALL_NAMES)}


ALL_NAMES = ("norm_mix_g", "w_in", "lb_logits", "hgrn_norm_g", "w_pool", "pool_scale", "w_out", "norm_x_g",
             "norm_mem_g", "w_xq", "w_xk", "w_xv", "w_xo", "norm_ffn_g", "w_ff1", "w_ff2", "final_norm_g")


def _shard_2d(name, a):
    a = a[0]
    if name in ("w_xq", "w_xk", "w_xv"):
        return a.reshape(a.shape[0], -1)
    if name == "w_xo":
        return a.reshape(-1, a.shape[-1])
    return a


def _small_2d(name, a):
    if name == "w_pool":
        return a.reshape(-1, HEAD_DIM)
    if name == "lb_logits":
        return a
    return a.reshape(1, -1)


def kernel(x, mem, norm_mix_g, w_in, lb_logits, hgrn_norm_g, w_pool, pool_scale, w_out, norm_x_g, norm_mem_g, w_xq, w_xk, w_xv, w_xo, norm_ffn_g, w_ff1, w_ff2, final_norm_g, loss_target, m_norm_mix_g, m_w_in, m_lb_logits, m_hgrn_norm_g, m_w_pool, m_pool_scale, m_w_out, m_norm_x_g, m_norm_mem_g, m_w_xq, m_w_xk, m_w_xv, m_w_xo, m_norm_ffn_g, m_w_ff1, m_w_ff2, m_final_norm_g, v_norm_mix_g, v_w_in, v_lb_logits, v_hgrn_norm_g, v_w_pool, v_pool_scale, v_w_out, v_norm_x_g, v_norm_mem_g, v_w_xq, v_w_xk, v_w_xv, v_w_xo, v_norm_ffn_g, v_w_ff1, v_w_ff2, v_final_norm_g):
    w = dict(norm_mix_g=norm_mix_g, w_in=w_in, lb_logits=lb_logits, hgrn_norm_g=hgrn_norm_g, w_pool=w_pool, pool_scale=pool_scale, w_out=w_out, norm_x_g=norm_x_g, norm_mem_g=norm_mem_g, w_xq=w_xq, w_xk=w_xk, w_xv=w_xv, w_xo=w_xo, norm_ffn_g=norm_ffn_g, w_ff1=w_ff1, w_ff2=w_ff2, final_norm_g=final_norm_g)
    m = dict(norm_mix_g=m_norm_mix_g, w_in=m_w_in, lb_logits=m_lb_logits, hgrn_norm_g=m_hgrn_norm_g, w_pool=m_w_pool, pool_scale=m_pool_scale, w_out=m_w_out, norm_x_g=m_norm_x_g, norm_mem_g=m_norm_mem_g, w_xq=m_w_xq, w_xk=m_w_xk, w_xv=m_w_xv, w_xo=m_w_xo, norm_ffn_g=m_norm_ffn_g, w_ff1=m_w_ff1, w_ff2=m_w_ff2, final_norm_g=m_final_norm_g)
    v = dict(norm_mix_g=v_norm_mix_g, w_in=v_w_in, lb_logits=v_lb_logits, hgrn_norm_g=v_hgrn_norm_g, w_pool=v_w_pool, pool_scale=v_pool_scale, w_out=v_w_out, norm_x_g=v_norm_x_g, norm_mem_g=v_norm_mem_g, w_xq=v_w_xq, w_xk=v_w_xk, w_xv=v_w_xv, w_xo=v_w_xo, norm_ffn_g=v_norm_ffn_g, w_ff1=v_w_ff1, w_ff2=v_w_ff2, final_norm_g=v_final_norm_g)

    big_w = {k: _shard_2d(k, w[k]) for k in BIG_NAMES}
    slab_attn = jnp.concatenate([big_w[k] for k in ("w_out", "w_xq", "w_xk", "w_xv")], axis=0).astype(BF16)
    shards = dict(slab_attn=slab_attn, **{k: big_w[k].astype(BF16) for k in ("w_in", "w_xo", "w_ff1", "w_ff2")})

    cx, cy, cc = lax.axis_index("x"), lax.axis_index("y"), lax.axis_index("c")
    chip = 2 * cx + cy
    idx = jnp.stack([cc, chip, chip ^ 1, chip ^ 2, chip ^ 3]).astype(jnp.int32)
    small = {k: w[k] for k in SMALL_NAMES}
    grad_x, stats, d_wp, partials = _step(x[0], mem[0], loss_target[0], small, shards, idx)

    halves = [_grad_chip_add("grad_chip_add_" + k, *partials[k], tr=min(256, partials[k][0].shape[0]))
              for k in BIG_NAMES]
    theirs = _grad_half_exchange(halves)

    grads, deltas, new_m, new_v = {}, {}, {}, {}
    for k, mine, other in zip(BIG_NAMES, halves, theirs):
        as_held = (lambda a: a[0]) if k in ("w_xq", "w_xk", "w_xv") else functools.partial(_shard_2d, k)
        res = _adamw("adamw_" + k, mine, other, as_held(w[k]), as_held(m[k]), as_held(v[k]), idx,
                     tr=min(256, mine.shape[0]))
        for store, val in zip((grads, deltas, new_m, new_v), res):
            store[k] = val.reshape(w[k].shape)

    slab_sum, wp_sum = _small_allreduce(stats, d_wp.reshape(-1, HEAD_DIM))
    loss, upd = _small_update(slab_sum, wp_sum, {k: _small_2d(k, w[k]) for k in SMALL_NAMES},
                              {k: _small_2d(k, m[k]) for k in SMALL_NAMES}, {k: _small_2d(k, v[k]) for k in SMALL_NAMES})
    for k in SMALL_NAMES:
        for store, val in zip((grads, deltas, new_m, new_v), upd[k]):
            store[k] = val.reshape(w[k].shape)

    return (loss[0, 0], grad_x[None], *[grads[k] for k in ALL_NAMES], *[deltas[k] for k in ALL_NAMES],
            *[new_m[k] for k in ALL_NAMES], *[new_v[k] for k in ALL_NAMES])
```

```python
import functools

import jax
import jax.numpy as jnp
from jax import lax
from jax.experimental import pallas as pl
from jax.experimental.pallas import tpu as pltpu

F32 = jnp.float32
BF16 = jnp.bfloat16
LOG2E = 1.4426950408889634
NEG_BIG = -1e30
MAX_LOG2_GROWTH = 100.0
MESH = pl.DeviceIdType.MESH
ANY = pl.BlockSpec(memory_space=pl.ANY)
VMEM = pl.BlockSpec(memory_space=pltpu.VMEM)

D_MODEL = 1024
N_CHIPS = 4
HGRN_HEADS = 4
HEAD_DIM = 128
HGRN_WIDTH = HGRN_HEADS * HEAD_DIM
POOL_WINDOWS = (2, 4, 8, 16)
POOL_HALO = 16
SUB = 16
HALF = SUB // 2
CHUNK = 64
XATTN_HEADS = 4
XATTN_HEAD_DIM = 256
EPS = 1e-6
ADAM_LR, ADAM_B1, ADAM_B2, ADAM_EPS, ADAM_WD, ADAM_STEP = 0.001, 0.9, 0.999, 1e-08, 0.01, 10

V7X_VMEM_BYTES = 64 * 1024 * 1024
VMEM_LIMIT = V7X_VMEM_BYTES - 8 * 1024 * 1024

NN = (((1,), (0,)), ((), ()))
NT = (((1,), (1,)), ((), ()))
TN = (((0,), (0,)), ((), ()))

ROW_GMIX, ROW_GX, ROW_GMEM, ROW_GFFN, ROW_GFIN, ROW_LB_HGN, ROW_PSCALE, ROW_LOSS = range(8)


def _dot(a, b, dims=NN):
    return lax.dot_general(a, b, dims, preferred_element_type=F32)


def _sigmoid(x):
    return 1.0 / (1.0 + jnp.exp(-x))


def _rms_fwd(x, g):
    r = lax.rsqrt(jnp.mean(x * x, axis=-1, keepdims=True) + EPS)
    n = x * r
    return n * g, n, r


def _rms_bwd(dh, n, r, g):
    dn = dh * g
    dx = r * (dn - n * jnp.mean(dn * n, axis=-1, keepdims=True))
    return dx, jnp.sum(dh * n, axis=0, keepdims=True)


def _params(sem=None):
    return pltpu.CompilerParams(dimension_semantics=sem, vmem_limit_bytes=VMEM_LIMIT)


def _const(shape):
    nd = len(shape)
    return pl.BlockSpec(shape, lambda *_: (0,) * nd, pipeline_mode=pl.Buffered(1))


def _const_out(shape):
    nd = len(shape)
    return pl.BlockSpec(shape, lambda *_: (0,) * nd)


def _acc_rows(ref, t, rows):
    upd = jnp.concatenate(rows + [jnp.zeros((8 - len(rows), rows[0].shape[1]), F32)], axis=0)

    @pl.when(t == 0)
    def _():
        ref[...] = upd

    @pl.when(t > 0)
    def _():
        ref[...] = ref[...] + upd


def _fuse_exchange(body, n_in, n_out, n_scratch, plan, ndim):
    if plan is None:
        return body
    n = plan.n

    def wrapped(*refs):
        ins, cin = refs[:n_in], refs[n_in:n_in + n]
        outs, cout = refs[n_in + n:n_in + n + n_out], refs[n_in + n + n_out:n_in + 2 * n + n_out]
        rest = refs[n_in + 2 * n + n_out:]
        scr, csem = rest[:n_scratch], rest[n_scratch:]
        first = pl.program_id(0) == 0
        last = pl.program_id(0) == pl.num_programs(0) - 1
        for i in range(1, ndim):
            first = first & (pl.program_id(i) == 0)
            last = last & (pl.program_id(i) == pl.num_programs(i) - 1)

        @pl.when(first)
        def _():
            plan.start(cin, cout, csem)

        body(*ins, *outs, *scr)

        @pl.when(last)
        def _():
            plan.finish(cin, cout, csem)

    return wrapped


def _plan_extras(plan):
    if plan is None:
        return [], [], []
    return [ANY] * plan.n, list(plan.out_shape), list(plan.scratch_shapes)


def _in_proj(x, g, win_g, tm, plan=None, plan_args=()):
    s, d = x.shape
    nsh, _, wc = win_g.shape

    def body(x_ref, g_ref, w_ref, z_ref, h_ref):
        h, _, _ = _rms_fwd(x_ref[...], g_ref[...])
        hb = h.astype(BF16)
        h_ref[...] = hb
        for j in range(nsh):
            z_ref[:, j * wc:(j + 1) * wc] = _dot(hb, w_ref[j])

    x_specs, x_shapes, x_scratch = _plan_extras(plan)
    return pl.pallas_call(
        _fuse_exchange(body, 3, 2, 0, plan, 1), name="in_proj", grid=(s // tm,),
        in_specs=[pl.BlockSpec((tm, d), lambda t: (t, 0)), _const((1, d)), _const((nsh, d, wc))] + x_specs,
        out_specs=[pl.BlockSpec((tm, nsh * wc), lambda t: (t, 0)), pl.BlockSpec((tm, d), lambda t: (t, 0))] + x_specs,
        out_shape=[jax.ShapeDtypeStruct((s, nsh * wc), F32), jax.ShapeDtypeStruct((s, d), BF16)] + x_shapes,
        scratch_shapes=x_scratch,
        compiler_params=_params(("arbitrary",)),
    )(x, g, win_g, *plan_args)


def _lower_bound(l0, l1):
    m = jnp.maximum(l0, l1)
    e0, e1 = jnp.exp(l0 - m), jnp.exp(l1 - m)
    return e0 / (e0 + e1)


def _block_tri(n, group, upper):
    r = lax.broadcasted_iota(jnp.int32, (n, n), 0)
    c = lax.broadcasted_iota(jnp.int32, (n, n), 1)
    keep = (r // group == c // group) & ((c >= r) if upper else (c <= r))
    return keep.astype(BF16)


def _group_cumsum(tri, x):
    hi = x.astype(BF16)
    rest = x - hi.astype(F32)
    mid = rest.astype(BF16)
    lo = (rest - mid.astype(F32)).astype(BF16)
    return (_dot(tri, hi) + _dot(tri, mid)) + _dot(tri, lo)


def _decay(b, bj, rows, first):
    d = b - bj
    if first:
        d = jnp.where(rows >= first, d, NEG_BIG)
    return jnp.exp2(d)


class _RowSums:
    ORDER = (0, 4, 2, 6, 1, 5, 3, 7)

    def __init__(self, rows):
        self.rows = rows
        self.level = {4: {}, 2: {}, 1: {}}

    def _pair(self, p, q, d):
        return jnp.where((self.rows & d) != 0, p + pltpu.roll(p, d, axis=0), q + pltpu.roll(q, HALF - d, axis=0))

    def push(self, j, y, d=4):
        if d == 0:
            self.out = y
            return
        slot = self.level[d]
        key = j % d
        if key not in slot:
            slot[key] = (j, y)
            return
        j0, y0 = slot.pop(key)
        p, q = (y, y0) if j & d else (y0, y)
        self.push(key, self._pair(p, q, d), d // 2)

    def result(self):
        return self.out


def _hgrn_gates(qp, fp, lb):
    sq = _sigmoid(qp)
    sf = _sigmoid(fp)
    f = lb + (1.0 - lb) * sf
    return qp * sq, sq, f, sf


def _hgrn_fwd(z, l0, l1, gn, tc, unroll=1, plan=None, plan_args=()):
    s = z.shape[0]
    nsub = tc // SUB
    hd = HEAD_DIM

    def body(q_ref, f_ref, v_ref, g_ref, l0_ref, l1_ref, gn_ref, tri_ref, tric_ref, o_ref, oa_ref, st_ref,
             state, qs, ks, bs, os_):
        @pl.when(pl.program_id(1) == 0)
        def _():
            state[...] = jnp.zeros_like(state)

        lb = _lower_bound(l0_ref[...], l1_ref[...])
        q, _, f, _ = _hgrn_gates(q_ref[...], f_ref[...], lb)
        k = 1.0 - f
        lf = jnp.log(f) * LOG2E
        bc = _group_cumsum(tric_ref[...], lf)
        bounded = jnp.min(bc) >= -MAX_LOG2_GROWTH

        @pl.when(bounded)
        def _():
            qt = (q * jnp.exp2(bc)).astype(BF16)
            ki = (k * jnp.exp2(-bc)).astype(BF16)
            vb = v_ref[...].astype(BF16)
            a = jnp.where(tric_ref[...] > 0, _dot(qt, ki, NT), 0.0).astype(BF16)
            o_in = _dot(a, vb)
            for c in range(tc // CHUNK):
                rs = slice(c * CHUNK, (c + 1) * CHUNK)
                st = state[...]
                st_ref[c] = st
                os_[rs, :] = o_in[rs] + _dot(qt[rs], st.astype(BF16), NT)
                bl = bc[(c + 1) * CHUNK - 1:(c + 1) * CHUNK, :]
                kt = (k[rs] * jnp.exp2(bl - bc[rs])).astype(BF16)
                state[...] = st * jnp.exp2(bl) + _dot(vb[rs], kt, TN)

        @pl.when(jnp.logical_not(bounded))
        def _():
            qs[...] = q
            ks[...] = k
            bs[...] = _group_cumsum(tri_ref[...], lf)
            rows = lax.broadcasted_iota(jnp.int32, (HALF, 1), 0)

            def step(i, carry):
                r0 = pl.multiple_of(i * SUB, SUB)
                q_ = qs[pl.ds(r0, SUB), :]
                k_ = ks[pl.ds(r0, SUB), :]
                b_ = bs[pl.ds(r0, SUB), :]
                v_ = v_ref[pl.ds(r0, SUB), :]
                st = state[...]

                @pl.when(i % (CHUNK // SUB) == 0)
                def _():
                    st_ref[i // (CHUNK // SUB)] = st

                bl = b_[SUB - 1:SUB, :]
                o = _dot((q_ * jnp.exp2(b_)).astype(BF16), st.astype(BF16), NT)
                (q_lo, q_hi), (b_lo, b_hi), (o_lo, o_hi) = ((x[:HALF], x[HALF:]) for x in (q_, b_, o))
                for j in range(SUB):
                    bj, kj, vj = b_[j:j + 1, :], k_[j:j + 1, :], v_[j:j + 1, :]
                    if j < HALF:
                        e = _decay(b_lo, bj, rows, j)
                        o_lo = o_lo + jnp.sum(q_lo * e * kj, axis=-1, keepdims=True) * vj
                    e = _decay(b_hi, bj, rows, j - HALF if j > HALF else None)
                    o_hi = o_hi + jnp.sum(q_hi * e * kj, axis=-1, keepdims=True) * vj
                os_[pl.ds(r0, HALF), :] = o_lo
                os_[pl.ds(r0 + HALF, HALF), :] = o_hi
                kt = (k_ * jnp.exp2(bl - b_)).astype(BF16)
                state[...] = st * jnp.exp2(bl) + _dot(v_.astype(BF16), kt, TN)
                return carry

            lax.fori_loop(0, nsub, step, 0, unroll=unroll)

        o = os_[...]
        o_ref[...] = o
        r = lax.rsqrt(jnp.mean(o * o, axis=-1, keepdims=True) + EPS)
        gp = g_ref[...]
        oa_ref[...] = (o * r * gn_ref[...] * (gp * _sigmoid(gp))).astype(BF16)

    col = lambda k: pl.BlockSpec((tc, hd), lambda h, t: (t, k * HGRN_HEADS + h))
    vec = pl.BlockSpec((None, 1, hd), lambda h, t: (h, 0, 0))
    x_specs, x_shapes, x_scratch = _plan_extras(plan)
    return pl.pallas_call(
        _fuse_exchange(body, 9, 3, 5, plan, 2), name="hgrn_fwd", grid=(HGRN_HEADS, s // tc),
        in_specs=[col(0), col(1), col(2), col(3), vec, vec, vec, _const((tc, tc)), _const((tc, tc))] + x_specs,
        out_specs=[pl.BlockSpec((tc, hd), lambda h, t: (t, h)), pl.BlockSpec((tc, hd), lambda h, t: (t, h)),
                   pl.BlockSpec((None, tc // CHUNK, hd, hd), lambda h, t: (h, t, 0, 0))] + x_specs,
        out_shape=[jax.ShapeDtypeStruct((s, HGRN_WIDTH), F32), jax.ShapeDtypeStruct((s, HGRN_WIDTH), BF16),
                   jax.ShapeDtypeStruct((HGRN_HEADS, s // CHUNK, hd, hd), F32)] + x_shapes,
        scratch_shapes=[pltpu.VMEM((hd, hd), F32)] + [pltpu.VMEM((tc, hd), F32)] * 4 + x_scratch,
        compiler_params=_params(("arbitrary", "arbitrary")),
    )(z, z, z, z, l0, l1, gn, _block_tri(tc, SUB, False), _block_tri(tc, CHUNK, False), *plan_args)


def _pooled(p, ext, tok0):
    tm = p.shape[0]
    tok = tok0 + lax.broadcasted_iota(jnp.int32, (tm, 1), 0)
    outs = []
    for g, w in enumerate(POOL_WINDOWS):
        acc = ext[:, g * HEAD_DIM:(g + 1) * HEAD_DIM]
        sh = 1
        while sh < w:
            acc = acc + pltpu.roll(acc, sh, axis=0)
            sh *= 2
        cnt = jnp.minimum(tok + 1, w).astype(F32)
        outs.append(acc[POOL_HALO:, :] / cnt - p[:, g * HEAD_DIM:(g + 1) * HEAD_DIM])
    return outs


def _pool_fwd(z, wp, scale, tm):
    s = z.shape[0]
    pw = len(POOL_WINDOWS) * HEAD_DIM
    nb = tm // POOL_HALO

    def body(p_ref, prev_ref, wp_ref, sc_ref, ob_ref):
        t = pl.program_id(0)
        p = p_ref[...]
        prev = jnp.where(t > 0, prev_ref[...], 0.0)
        pooled = _pooled(p, jnp.concatenate([prev, p], axis=0), t * tm)
        ys = [_dot(pooled[g].astype(BF16), wp_ref[g].astype(BF16)) for g in range(len(POOL_WINDOWS))]
        ob_ref[...] = (jnp.concatenate(ys, axis=1) * sc_ref[...]).astype(BF16)

    return pl.pallas_call(
        body, name="pool_fwd", grid=(s // tm,),
        in_specs=[pl.BlockSpec((tm, pw), lambda t: (t, 4)),
                  pl.BlockSpec((POOL_HALO, pw), lambda t: (jnp.maximum(t * nb - 1, 0), 4)),
                  _const(wp.shape), _const((1, pw))],
        out_specs=pl.BlockSpec((tm, pw), lambda t: (t, 0)),
        out_shape=jax.ShapeDtypeStruct((s, pw), BF16),
        compiler_params=_params(("parallel",)),
    )(z, z, wp, scale)


def _kv_proj(mem, g, slab_g):
    m, d = mem.shape
    rows = d // N_CHIPS

    def body(mem_ref, g_ref, wk_ref, wv_ref, xk_ref, xv_ref):
        hm, _, _ = _rms_fwd(mem_ref[...], g_ref[...])
        hb = hm.astype(BF16)
        xk_ref[...] = _dot(hb, wk_ref[...].reshape(d, d)).astype(BF16)
        xv_ref[...] = _dot(hb, wv_ref[...].reshape(d, d)).astype(BF16)

    blk = lambda k: pl.BlockSpec((N_CHIPS, rows, d), lambda i: (0, k, 0))
    return pl.pallas_call(
        body, name="kv_proj", grid=(1,),
        in_specs=[_const((m, d)), _const((1, d)), blk(0), blk(1)],
        out_specs=[_const_out((m, d)), _const_out((m, d))],
        out_shape=[jax.ShapeDtypeStruct((m, d), BF16)] * 2,
        compiler_params=_params(("arbitrary",)),
    )(mem, g, slab_g, slab_g)


def _softmax_rows(sc):
    e = jnp.exp(sc - jnp.max(sc, axis=-1, keepdims=True))
    return e / jnp.sum(e, axis=-1, keepdims=True)


def _mix_xattn_fwd(x, oa, ob, gx, slab_g, wo_g, xk, xv, tm, plan=None, plan_args=()):
    s, d = x.shape
    m = xk.shape[0]
    rows = d // N_CHIPS
    hw = oa.shape[1]
    e = XATTN_HEAD_DIM

    def body(x_ref, oa_ref, ob_ref, gx_ref, wout_ref, wq_ref, wo_ref, xk_ref, xv_ref,
             x1_ref, mixed_ref, hq_ref, xq_ref, att_ref, x2_ref):
        mixed = jnp.concatenate([oa_ref[...], ob_ref[...]], axis=1)
        mixed_ref[...] = mixed
        x1 = x_ref[...] + _dot(mixed, wout_ref[...].reshape(d, d))
        x1_ref[...] = x1
        hq, _, _ = _rms_fwd(x1, gx_ref[...])
        hqb = hq.astype(BF16)
        hq_ref[...] = hqb
        xq = _dot(hqb, wq_ref[...].reshape(d, d)).astype(BF16)
        xq_ref[...] = xq
        atts = []
        for h in range(XATTN_HEADS):
            cs = slice(h * e, (h + 1) * e)
            p = _softmax_rows(_dot(xq[:, cs], xk_ref[:, cs], NT) * (e ** -0.5))
            atts.append(_dot(p.astype(BF16), xv_ref[:, cs]).astype(BF16))
        att = jnp.concatenate(atts, axis=1)
        att_ref[...] = att
        for j in range(N_CHIPS):
            x2_ref[:, j * rows:(j + 1) * rows] = x1[:, j * rows:(j + 1) * rows] + _dot(att, wo_ref[j])

    tile = lambda w: pl.BlockSpec((tm, w), lambda t: (t, 0))
    blk = lambda k: pl.BlockSpec((N_CHIPS, rows, d), lambda t: (0, k, 0), pipeline_mode=pl.Buffered(1))
    x_specs, x_shapes, x_scratch = _plan_extras(plan)
    return pl.pallas_call(
        _fuse_exchange(body, 9, 6, 0, plan, 1), name="mix_xattn_fwd", grid=(s // tm,),
        in_specs=[tile(d), tile(hw), tile(hw), _const((1, d)), blk(0), blk(1), _const(wo_g.shape),
                  _const((m, d)), _const((m, d))] + x_specs,
        out_specs=[tile(d)] * 6 + x_specs,
        out_shape=[jax.ShapeDtypeStruct((s, d), F32)] + [jax.ShapeDtypeStruct((s, d), BF16)] * 4
                  + [jax.ShapeDtypeStruct((s, d), F32)] + x_shapes,
        scratch_shapes=x_scratch,
        compiler_params=_params(("arbitrary",)),
    )(x, oa, ob, gx, slab_g, slab_g, wo_g, xk, xv, *plan_args)


def _mlp_loss_fwd(x2, gffn, gfin, w1_g, w2_g, target, tm):
    s, d = x2.shape
    wr = w1_g.shape[1]

    def body(x2_ref, gffn_ref, gfin_ref, w1_ref, w2_ref, tg_ref, a_ref, hf_ref, dx3_ref, dx3b_ref, st_ref):
        x2v = x2_ref[...]
        hf, _, _ = _rms_fwd(x2v, gffn_ref[...])
        hfb = hf.astype(BF16)
        hf_ref[...] = hfb
        acc = x2v
        for j in range(N_CHIPS):
            a = _dot(hfb, w1_ref[j])
            a_ref[:, j * wr:(j + 1) * wr] = a
            r = jnp.maximum(a, 0.0)
            acc = acc + _dot((r * r).astype(BF16), w2_ref[j])
        gf = gfin_ref[...]
        y, n, r3 = _rms_fwd(acc, gf)
        err = y - tg_ref[...]
        loss = 0.5 * jnp.sum(jnp.sum(err * err, axis=-1, keepdims=True) * (1.0 / d), axis=0, keepdims=True)
        dy = err * (1.0 / d)
        dx3, dgf = _rms_bwd(dy, n, r3, gf)
        dx3_ref[...] = dx3
        dx3b_ref[...] = dx3.astype(BF16)
        _acc_rows(st_ref, pl.program_id(0), [dgf, jnp.broadcast_to(loss, (1, d))])

    tile = lambda w: pl.BlockSpec((tm, w), lambda t: (t, 0))
    blk = lambda k: pl.BlockSpec((N_CHIPS, wr, d), lambda t: (0, k, 0), pipeline_mode=pl.Buffered(1))
    return pl.pallas_call(
        body, name="mlp_loss_fwd", grid=(s // tm,),
        in_specs=[tile(d), _const((1, d)), _const((1, d)), blk(0), blk(0), tile(d)],
        out_specs=[tile(N_CHIPS * wr), tile(d), tile(d), tile(d), _const_out((8, d))],
        out_shape=[jax.ShapeDtypeStruct((s, N_CHIPS * wr), F32), jax.ShapeDtypeStruct((s, d), BF16),
                   jax.ShapeDtypeStruct((s, d), F32), jax.ShapeDtypeStruct((s, d), BF16),
                   jax.ShapeDtypeStruct((8, d), F32)],
        compiler_params=_params(("arbitrary",)),
    )(x2, gffn, gfin, w1_g, w2_g, target)


def _mlp_bwd(dx3, dx3b, a, x2, gffn, w1_g, w2_g, tm):
    s, d = x2.shape
    wr = w1_g.shape[1]

    def body(dx3_ref, dx3b_ref, a_ref, x2_ref, g_ref, w1_ref, w2_ref, da_ref, u_ref, dx2_ref, dx2b_ref, st_ref):
        dyb = dx3b_ref[...]
        dhf = jnp.zeros((tm, d), F32)
        for j in range(N_CHIPS):
            r = jnp.maximum(a_ref[:, j * wr:(j + 1) * wr], 0.0)
            da = (_dot(dyb, w2_ref[j], NT) * (2.0 * r)).astype(BF16)
            da_ref[:, j * wr:(j + 1) * wr] = da
            u_ref[:, j * wr:(j + 1) * wr] = (r * r).astype(BF16)
            dhf = dhf + _dot(da, w1_ref[j], NT)
        g = g_ref[...]
        _, n, r2 = _rms_fwd(x2_ref[...], g)
        dxn, dg = _rms_bwd(dhf, n, r2, g)
        dx2 = dx3_ref[...] + dxn
        dx2_ref[...] = dx2
        dx2b_ref[...] = dx2.astype(BF16)
        _acc_rows(st_ref, pl.program_id(0), [dg])

    tile = lambda w: pl.BlockSpec((tm, w), lambda t: (t, 0))
    blk = lambda k: pl.BlockSpec((N_CHIPS, wr, d), lambda t: (0, k, 0), pipeline_mode=pl.Buffered(1))
    nf = N_CHIPS * wr
    return pl.pallas_call(
        body, name="mlp_bwd", grid=(s // tm,),
        in_specs=[tile(d), tile(d), tile(nf), tile(d), _const((1, d)), blk(0), blk(0)],
        out_specs=[tile(nf), tile(nf), tile(d), tile(d), _const_out((8, d))],
        out_shape=[jax.ShapeDtypeStruct((s, nf), BF16), jax.ShapeDtypeStruct((s, nf), BF16),
                   jax.ShapeDtypeStruct((s, d), F32), jax.ShapeDtypeStruct((s, d), BF16),
                   jax.ShapeDtypeStruct((8, d), F32)],
        compiler_params=_params(("arbitrary",)),
    )(dx3, dx3b, a, x2, gffn, w1_g, w2_g)


def _xattn_mix_bwd(dx2, x1, xq, xk, xv, gx, slab_g, wo_g, tm, plan=None, plan_args=()):
    s, d = x1.shape
    m = xk.shape[0]
    rows = d // N_CHIPS
    e = XATTN_HEAD_DIM

    def body(dx2_ref, x1_ref, xq_ref, xk_ref, xv_ref, gx_ref, wout_ref, wq_ref, wo_ref,
             dx1_ref, dx1b_ref, dxq_ref, dmix_ref, dxk_ref, dxv_ref, st_ref):
        t = pl.program_id(0)
        dx2 = dx2_ref[...]
        dx2b = dx2.astype(BF16)
        datt = jnp.zeros((tm, d), F32)
        for j in range(N_CHIPS):
            datt = datt + _dot(dx2b[:, j * rows:(j + 1) * rows], wo_ref[j], NT)
        dattb = datt.astype(BF16)
        dxqs, dxks, dxvs = [], [], []
        for h in range(XATTN_HEADS):
            cs = slice(h * e, (h + 1) * e)
            xq_h, xk_h, xv_h = xq_ref[:, cs], xk_ref[:, cs], xv_ref[:, cs]
            p = _softmax_rows(_dot(xq_h, xk_h, NT) * (e ** -0.5))
            dp = _dot(dattb[:, cs], xv_h, NT)
            ds = (p * (dp - jnp.sum(dp * p, axis=-1, keepdims=True)) * (e ** -0.5)).astype(BF16)
            dxqs.append(_dot(ds, xk_h).astype(BF16))
            dxks.append(_dot(ds, xq_h, TN))
            dxvs.append(_dot(p.astype(BF16), dattb[:, cs], TN))
        dxq = jnp.concatenate(dxqs, axis=1)
        dxq_ref[...] = dxq
        dxk = jnp.concatenate(dxks, axis=1)
        dxv = jnp.concatenate(dxvs, axis=1)

        @pl.when(t == 0)
        def _():
            dxk_ref[...] = dxk
            dxv_ref[...] = dxv

        @pl.when(t > 0)
        def _():
            dxk_ref[...] = dxk_ref[...] + dxk
            dxv_ref[...] = dxv_ref[...] + dxv

        dhq = jnp.concatenate([_dot(dxq, wq_ref[j], NT) for j in range(N_CHIPS)], axis=1)
        g = gx_ref[...]
        _, n, r1 = _rms_fwd(x1_ref[...], g)
        dxn, dg = _rms_bwd(dhq, n, r1, g)
        dx1 = dx2 + dxn
        dx1_ref[...] = dx1
        dx1b = dx1.astype(BF16)
        dx1b_ref[...] = dx1b
        for j in range(N_CHIPS):
            dmix_ref[:, j * rows:(j + 1) * rows] = _dot(dx1b, wout_ref[j], NT)
        _acc_rows(st_ref, t, [dg])

    tile = lambda: pl.BlockSpec((tm, d), lambda t: (t, 0))
    blk = lambda k: pl.BlockSpec((N_CHIPS, rows, d), lambda t: (0, k, 0), pipeline_mode=pl.Buffered(1))
    x_specs, x_shapes, x_scratch = _plan_extras(plan)
    return pl.pallas_call(
        _fuse_exchange(body, 9, 7, 0, plan, 1), name="xattn_mix_bwd", grid=(s // tm,),
        in_specs=[tile(), tile(), tile(), _const((m, d)), _const((m, d)), _const((1, d)), blk(0), blk(1),
                  _const(wo_g.shape)] + x_specs,
        out_specs=[tile(), tile(), tile(), tile(), _const_out((m, d)), _const_out((m, d)), _const_out((8, d))]
                  + x_specs,
        out_shape=[jax.ShapeDtypeStruct((s, d), F32), jax.ShapeDtypeStruct((s, d), BF16),
                   jax.ShapeDtypeStruct((s, d), BF16), jax.ShapeDtypeStruct((s, d), F32),
                   jax.ShapeDtypeStruct((m, d), F32), jax.ShapeDtypeStruct((m, d), F32),
                   jax.ShapeDtypeStruct((8, d), F32)] + x_shapes,
        scratch_shapes=x_scratch,
        compiler_params=_params(("arbitrary",)),
    )(dx2, x1, xq, xk, xv, gx, slab_g, slab_g, wo_g, *plan_args)


def _kv_bwd(mem, g, dxk, dxv, slab_g):
    m, d = mem.shape
    rows = d // N_CHIPS

    def body(mem_ref, g_ref, dxk_ref, dxv_ref, wk_ref, wv_ref, dwk_ref, dwv_ref, st_ref):
        gv = g_ref[...]
        hm, n, _ = _rms_fwd(mem_ref[...], gv)
        hb = hm.astype(BF16)
        dkb = dxk_ref[...].astype(BF16)
        dvb = dxv_ref[...].astype(BF16)
        dhm = []
        for j in range(N_CHIPS):
            hj = hb[:, j * rows:(j + 1) * rows]
            dwk_ref[j] = _dot(hj, dkb, TN)
            dwv_ref[j] = _dot(hj, dvb, TN)
            dhm.append(_dot(dkb, wk_ref[j], NT) + _dot(dvb, wv_ref[j], NT))
        dg = jnp.sum(jnp.concatenate(dhm, axis=1) * n, axis=0, keepdims=True)
        st_ref[...] = jnp.concatenate([dg, jnp.zeros((7, d), F32)], axis=0)

    blk = lambda k: pl.BlockSpec((N_CHIPS, rows, d), lambda i: (0, k, 0))
    return pl.pallas_call(
        body, name="kv_bwd", grid=(1,),
        in_specs=[_const((m, d)), _const((1, d)), _const((m, d)), _const((m, d)), blk(0), blk(1)],
        out_specs=[_const_out((N_CHIPS, rows, d)), _const_out((N_CHIPS, rows, d)), _const_out((8, d))],
        out_shape=[jax.ShapeDtypeStruct((N_CHIPS, rows, d), F32)] * 2 + [jax.ShapeDtypeStruct((8, d), F32)],
        compiler_params=_params(("arbitrary",)),
    )(mem, g, dxk, dxv, slab_g, slab_g)


def _pool_bwd(z, dmix, wp, scale, tm, plan=None, plan_args=()):
    s = z.shape[0]
    ng = len(POOL_WINDOWS)
    pw = ng * HEAD_DIM
    nb = tm // POOL_HALO
    nt = s // tm
    n_ext = tm + POOL_HALO

    def body(p_ref, prev_ref, dm_ref, dmn_ref, wp_ref, sc_ref, dp_ref, dwp_ref, st_ref):
        t = pl.program_id(0)
        p = p_ref[...]
        prev = jnp.where(t > 0, prev_ref[...], 0.0)
        pooled = _pooled(p, jnp.concatenate([prev, p], axis=0), t * tm)
        dm = dm_ref[...]
        dme = jnp.concatenate([dm, jnp.where(t < nt - 1, dmn_ref[...], 0.0)], axis=0) * sc_ref[...]
        tok = t * tm + lax.broadcasted_iota(jnp.int32, (n_ext, 1), 0)
        dsc, dps, dwps = [], [], []
        for g, w in enumerate(POOL_WINDOWS):
            cs = slice(g * HEAD_DIM, (g + 1) * HEAD_DIM)
            wpb = wp_ref[g].astype(BF16)
            pb = pooled[g].astype(BF16)
            dsc.append(jnp.sum(dm[:, cs] * _dot(pb, wpb), axis=0, keepdims=True))
            dye = dme[:, cs].astype(BF16)
            dwps.append(_dot(pb, dye[:tm], TN))
            dpe = _dot(dye, wpb, NT)
            acc = dpe / jnp.minimum(tok + 1, w).astype(F32)
            sh = 1
            while sh < w:
                acc = acc + pltpu.roll(acc, n_ext - sh, axis=0)
                sh *= 2
            dps.append(acc[:tm] - dpe[:tm])
        dp_ref[...] = jnp.concatenate(dps, axis=1)
        dsc_row = jnp.concatenate(dsc, axis=1)

        @pl.when(t == 0)
        def _():
            for g in range(ng):
                dwp_ref[g] = dwps[g]

        @pl.when(t > 0)
        def _():
            for g in range(ng):
                dwp_ref[g] = dwp_ref[g] + dwps[g]

        _acc_rows(st_ref, t, [dsc_row])

    x_specs, x_shapes, x_scratch = _plan_extras(plan)
    return pl.pallas_call(
        _fuse_exchange(body, 6, 3, 0, plan, 1), name="pool_bwd", grid=(nt,),
        in_specs=[pl.BlockSpec((tm, pw), lambda t: (t, 4)),
                  pl.BlockSpec((POOL_HALO, pw), lambda t: (jnp.maximum(t * nb - 1, 0), 4)),
                  pl.BlockSpec((tm, pw), lambda t: (t, 1)),
                  pl.BlockSpec((POOL_HALO, pw), lambda t: (jnp.minimum((t + 1) * nb, s // POOL_HALO - 1), 1)),
                  _const(wp.shape), _const((1, pw))] + x_specs,
        out_specs=[pl.BlockSpec((tm, pw), lambda t: (t, 0)), _const_out(wp.shape), _const_out((8, pw))] + x_specs,
        out_shape=[jax.ShapeDtypeStruct((s, pw), F32), jax.ShapeDtypeStruct(wp.shape, F32),
                   jax.ShapeDtypeStruct((8, pw), F32)] + x_shapes,
        scratch_shapes=x_scratch,
        compiler_params=_params(("arbitrary",)),
    )(z, z, dmix, dmix, wp, scale, *plan_args)


def _hgrn_bwd(z, o, dmix, st, l0, l1, gn, tc, unroll=1, plan=None, plan_args=()):
    s = z.shape[0]
    nsub = tc // SUB
    nt = s // tc
    hd = HEAD_DIM

    def body(q_ref, f_ref, v_ref, g_ref, l0_ref, l1_ref, gn_ref, o_ref, dm_ref, st_ref,
             tril_ref, triu_ref, trilc_ref, triuc_ref,
             dq_ref, df_ref, di_ref, dg_ref, stat_ref, dstate, qs, ks, bs, dos, dqs, dks, dbs, sts):
        t = pl.program_id(1)

        @pl.when(t == 0)
        def _():
            dstate[...] = jnp.zeros_like(dstate)

        lb = _lower_bound(l0_ref[...], l1_ref[...])
        qp = q_ref[...]
        q, sq, f, sf = _hgrn_gates(qp, f_ref[...], lb)
        k = 1.0 - f
        lf = jnp.log(f) * LOG2E
        bc = _group_cumsum(trilc_ref[...], lf)
        bounded = jnp.min(bc) >= -MAX_LOG2_GROWTH

        o = o_ref[...]
        r = lax.rsqrt(jnp.mean(o * o, axis=-1, keepdims=True) + EPS)
        n = o * r
        gnv = gn_ref[...]
        gp = g_ref[...]
        sg = _sigmoid(gp)
        dm = dm_ref[...]
        dg_ref[...] = dm * (n * gnv) * (sg * (1.0 + gp * (1.0 - sg)))
        don = dm * (gp * sg)
        dgn = jnp.sum(don * n, axis=0, keepdims=True)
        dn = don * gnv
        do_all = r * (dn - n * jnp.mean(dn * n, axis=-1, keepdims=True))

        @pl.when(bounded)
        def _():
            eb = jnp.exp2(bc)
            eib = jnp.exp2(-bc)
            qt = (q * eb).astype(BF16)
            ki = (k * eib).astype(BF16)
            vb = v_ref[...].astype(BF16)
            dob = do_all.astype(BF16)
            mask = trilc_ref[...] > 0
            a = jnp.where(mask, _dot(qt, ki, NT), 0.0).astype(BF16)
            da = jnp.where(mask, _dot(dob, vb, NT), 0.0).astype(BF16)
            dq_in = _dot(da, ki)
            dk_in = _dot(da, qt, TN)
            dv_in = _dot(a, dob, TN)
            last_row = lax.broadcasted_iota(jnp.int32, (CHUNK, 1), 0) == CHUNK - 1
            for c in reversed(range(tc // CHUNK)):
                rs = slice(c * CHUNK, (c + 1) * CHUNK)
                stp = st_ref[c]
                dst = dstate[...]
                dstb = dst.astype(BF16)
                bl = bc[(c + 1) * CHUNK - 1:(c + 1) * CHUNK, :]
                ekl = jnp.exp2(bl - bc[rs])
                ebl = jnp.exp2(bl)
                kt = k[rs] * ekl
                dq_st = _dot(dob[rs], stp.astype(BF16)) * eb[rs]
                dkt = _dot(vb[rs], dstb)
                extra = jnp.sum(kt * dkt, axis=0, keepdims=True) + ebl * jnp.sum(stp * dst, axis=0, keepdims=True)
                dqs[rs, :] = dq_st + dq_in[rs] * eb[rs]
                dks[rs, :] = dkt * ekl + dk_in[rs] * eib[rs]
                di_ref[rs, :] = _dot(kt.astype(BF16), dstb, NT) + dv_in[rs]
                dbs[rs, :] = (q[rs] * dq_st - kt * dkt + jnp.where(last_row, extra, 0.0)
                              + (qt[rs].astype(F32) * dq_in[rs] - ki[rs].astype(F32) * dk_in[rs]))
                dstate[...] = dst * ebl + _dot(dob[rs], qt[rs], TN)
            dbs[...] = _group_cumsum(triuc_ref[...], dbs[...])

        @pl.when(jnp.logical_not(bounded))
        def _():
            qs[...] = q
            ks[...] = k
            bs[...] = _group_cumsum(tril_ref[...], lf)
            dos[...] = do_all
            per = CHUNK // SUB

            def restore(i, carry):
                @pl.when(i % per == 0)
                def _():
                    sts[i] = st_ref[i // per]

                @pl.when(i % per != 0)
                def _():
                    rp = pl.multiple_of((i - 1) * SUB, SUB)
                    b_ = bs[pl.ds(rp, SUB), :]
                    bl = b_[SUB - 1:SUB, :]
                    kt = (ks[pl.ds(rp, SUB), :] * jnp.exp2(bl - b_)).astype(BF16)
                    sts[i] = sts[i - 1] * jnp.exp2(bl) + _dot(v_ref[pl.ds(rp, SUB), :].astype(BF16), kt, TN)

                return carry

            lax.fori_loop(0, nsub, restore, 0)
            rows = lax.broadcasted_iota(jnp.int32, (HALF, 1), 0)
            last_row = lax.broadcasted_iota(jnp.int32, (SUB, 1), 0) == SUB - 1

            def step(i, carry):
                ii = nsub - 1 - i
                r0 = pl.multiple_of(ii * SUB, SUB)
                q_ = qs[pl.ds(r0, SUB), :]
                k_ = ks[pl.ds(r0, SUB), :]
                b_ = bs[pl.ds(r0, SUB), :]
                v_ = v_ref[pl.ds(r0, SUB), :]
                do_ = dos[pl.ds(r0, SUB), :]
                stp = sts[ii]
                dst = dstate[...]
                bl = b_[SUB - 1:SUB, :]
                eb = jnp.exp2(b_)
                ekl = jnp.exp2(bl - b_)
                ebl = jnp.exp2(bl)
                dob = do_.astype(BF16)
                dstb = dst.astype(BF16)
                kt = k_ * ekl
                dq = _dot(dob, stp.astype(BF16)) * eb
                dkt = _dot(v_.astype(BF16), dstb)
                dk = dkt * ekl
                dv = _dot(kt.astype(BF16), dstb, NT)
                extra = jnp.sum(kt * dkt, axis=0, keepdims=True) + ebl * jnp.sum(stp * dst, axis=0, keepdims=True)
                halves = lambda x: [x[:HALF], x[HALF:]]
                q_h, b_h, do_h, dq_h, dk_h, dv_h = (halves(x) for x in (q_, b_, do_, dq, dk, dv))
                for own in range(2):
                    dk_rows, dv_rows = _RowSums(rows), _RowSums(rows)
                    for jj in _RowSums.ORDER:
                        j = own * HALF + jj
                        bj, kj, vj = b_[j:j + 1, :], k_[j:j + 1, :], v_[j:j + 1, :]
                        dk_sum = dv_sum = None
                        for h in range(own, 2):
                            e = _decay(b_h[h], bj, rows, jj if h == own else None)
                            pe = q_h[h] * e
                            acol = jnp.sum(pe * kj, axis=-1, keepdims=True)
                            dacol = jnp.sum(do_h[h] * vj, axis=-1, keepdims=True)
                            dq_h[h] = dq_h[h] + dacol * (e * kj)
                            dk_sum = dacol * pe if dk_sum is None else dk_sum + dacol * pe
                            dv_sum = acol * do_h[h] if dv_sum is None else dv_sum + acol * do_h[h]
                        dk_rows.push(jj, dk_sum)
                        dv_rows.push(jj, dv_sum)
                    dk_h[own] = dk_h[own] + dk_rows.result()
                    dv_h[own] = dv_h[own] + dv_rows.result()
                dq, dk, dv = (jnp.concatenate(x, axis=0) for x in (dq_h, dk_h, dv_h))
                dqs[pl.ds(r0, SUB), :] = dq
                dks[pl.ds(r0, SUB), :] = dk
                di_ref[pl.ds(r0, SUB), :] = dv
                dbs[pl.ds(r0, SUB), :] = q_ * dq - k_ * dk + jnp.where(last_row, extra, 0.0)
                dstate[...] = dst * ebl + _dot(dob, (q_ * eb).astype(BF16), TN)
                return carry

            lax.fori_loop(0, nsub, step, 0, unroll=unroll)
            dbs[...] = _group_cumsum(triu_ref[...], dbs[...])

        dlf = dbs[...]
        dfv = dlf / f - dks[...]
        df_ref[...] = dfv * (1.0 - lb) * sf * (1.0 - sf)
        dlb = jnp.sum(dfv * (1.0 - sf), axis=0, keepdims=True)
        dq_ref[...] = dqs[...] * (sq * (1.0 + qp * (1.0 - sq)))
        _acc_rows(stat_ref, t, [dgn, dlb])

    rev = lambda t: nt - 1 - t
    col = lambda k: pl.BlockSpec((tc, hd), lambda h, t: (rev(t), k * HGRN_HEADS + h))
    vec = pl.BlockSpec((None, 1, hd), lambda h, t: (h, 0, 0))
    head = pl.BlockSpec((tc, hd), lambda h, t: (rev(t), h))
    x_specs, x_shapes, x_scratch = _plan_extras(plan)
    return pl.pallas_call(
        _fuse_exchange(body, 14, 5, 9, plan, 2), name="hgrn_bwd", grid=(HGRN_HEADS, nt),
        in_specs=[col(0), col(1), col(2), col(3), vec, vec, vec, head, head,
                  pl.BlockSpec((None, tc // CHUNK, hd, hd), lambda h, t: (h, rev(t), 0, 0))]
                 + [_const((tc, tc))] * 4 + x_specs,
        out_specs=[head, head, head, head, pl.BlockSpec((None, 8, hd), lambda h, t: (h, 0, 0))] + x_specs,
        out_shape=[jax.ShapeDtypeStruct((s, HGRN_WIDTH), F32)] * 4 + [jax.ShapeDtypeStruct((HGRN_HEADS, 8, hd), F32)]
                  + x_shapes,
        scratch_shapes=[pltpu.VMEM((hd, hd), F32)] + [pltpu.VMEM((tc, hd), F32)] * 7
                       + [pltpu.VMEM((nsub, hd, hd), F32)] + x_scratch,
        compiler_params=_params(("arbitrary", "arbitrary")),
    )(z, z, z, z, l0, l1, gn, o, dmix, st, _block_tri(tc, SUB, False), _block_tri(tc, SUB, True),
      _block_tri(tc, CHUNK, False), _block_tri(tc, CHUNK, True), *plan_args)


def _in_bwd(dparts, dx1, x, g, win_g, tm, plan=None, plan_args=()):
    s, d = x.shape
    nsh, _, wc = win_g.shape
    pw = dparts[0].shape[1]

    def body(dq_ref, df_ref, di_ref, dg_ref, dp_ref, dx1_ref, x_ref, g_ref, w_ref, gx_ref, dz_ref, st_ref):
        dz = jnp.concatenate([dq_ref[...], df_ref[...], di_ref[...], dg_ref[...], dp_ref[...]], axis=1).astype(BF16)
        dz_ref[...] = dz
        dh = jnp.zeros((tm, d), F32)
        for j in range(nsh):
            dh = dh + _dot(dz[:, j * wc:(j + 1) * wc], w_ref[j], NT)
        gv = g_ref[...]
        _, n, r = _rms_fwd(x_ref[...], gv)
        dxn, dg = _rms_bwd(dh, n, r, gv)
        gx_ref[...] = dx1_ref[...] + dxn
        _acc_rows(st_ref, pl.program_id(0), [dg])

    tile = lambda w: pl.BlockSpec((tm, w), lambda t: (t, 0))
    x_specs, x_shapes, x_scratch = _plan_extras(plan)
    return pl.pallas_call(
        _fuse_exchange(body, 9, 3, 0, plan, 1), name="in_bwd", grid=(s // tm,),
        in_specs=[tile(pw)] * 5 + [tile(d), tile(d), _const((1, d)), _const(win_g.shape)] + x_specs,
        out_specs=[tile(d), tile(nsh * wc), _const_out((8, d))] + x_specs,
        out_shape=[jax.ShapeDtypeStruct((s, d), F32), jax.ShapeDtypeStruct((s, nsh * wc), BF16),
                   jax.ShapeDtypeStruct((8, d), F32)] + x_shapes,
        scratch_shapes=x_scratch,
        compiler_params=_params(("arbitrary",)),
    )(*dparts, dx1, x, g, win_g, *plan_args)


def _tn_grad(name, a, b, out_rows, out_cols, a_sharded, tr, tc, plan=None, plan_args=()):
    s = a.shape[0]
    nr, nc = out_rows // tr, out_cols // tc

    def body(a_ref, b_ref, o_ref):
        o_ref[...] = _dot(a_ref[...], b_ref[...], TN)

    a_map = (lambda j, i, k: (0, j * nr + i)) if a_sharded else (lambda j, i, k: (0, i))
    b_map = (lambda j, i, k: (0, k)) if a_sharded else (lambda j, i, k: (0, j * nc + k))
    x_specs, x_shapes, x_scratch = _plan_extras(plan)
    res = pl.pallas_call(
        _fuse_exchange(body, 2, 1, 0, plan, 3), name=name, grid=(N_CHIPS, nr, nc),
        in_specs=[pl.BlockSpec((s, tr), a_map), pl.BlockSpec((s, tc), b_map)] + x_specs,
        out_specs=[pl.BlockSpec((None, tr, tc), lambda j, i, k: (j, i, k))] + x_specs,
        out_shape=[jax.ShapeDtypeStruct((N_CHIPS, out_rows, out_cols), F32)] + x_shapes,
        scratch_shapes=x_scratch,
        compiler_params=_params(("arbitrary", "arbitrary", "arbitrary")),
    )(a, b, *plan_args)
    return res if plan else res[0]


FFN_NAMES = ("w_ff1", "w_ff2")
ATTN_NAMES = ("w_xo", "w_xq", "w_out", "w_xk", "w_xv")
EARLY_NAMES = FFN_NAMES + ATTN_NAMES
BIG_NAMES = EARLY_NAMES + ("w_in",)


def _halved(g):
    return g.reshape(N_CHIPS, 2, g.shape[1] // 2, g.shape[2])


def _pair_adds(names, gs, got, idx):
    pairs = [_grad_pair_add("grad_pair_add_" + k, g, r, idx, tr=min(256, g.shape[2])) for k, g, r in zip(names, gs, got)]
    return [p[0] for p in pairs], [p[1] for p in pairs]


def _step(x, mem, target, small, shards, idx):
    d = x.shape[1]
    l0 = small["lb_logits"][0].reshape(HGRN_HEADS, 1, HEAD_DIM)
    l1 = small["lb_logits"][1].reshape(HGRN_HEADS, 1, HEAD_DIM)
    gn = small["hgrn_norm_g"].reshape(HGRN_HEADS, 1, HEAD_DIM)
    wp = small["w_pool"].reshape(len(POOL_WINDOWS), HEAD_DIM, HEAD_DIM)
    psc = small["pool_scale"].reshape(1, -1)
    gmix, gx, gmem, gffn = (small[k].reshape(1, d) for k in ("norm_mix_g", "norm_x_g", "norm_mem_g", "norm_ffn_g"))
    gfin = small["final_norm_g"].reshape(1, d)

    (win_g,) = _run_exchange("gather_w_in", _WeightGather([shards["w_in"]]), [shards["w_in"]])
    z, h, kv_g = _in_proj(x, gmix, win_g, tm=512, plan=_WeightGather([shards["slab_kv"]]),
                          plan_args=[shards["slab_kv"]])
    mid_w = [shards["slab_oq"], shards["w_xo"], shards["w_ff1"]]
    o, oa, st, oq_g, wo_g, w1_g = _hgrn_fwd(z, l0, l1, gn, tc=256, unroll=8,
                                            plan=_WeightGather(mid_w), plan_args=mid_w)
    ob = _pool_fwd(z, wp, psc, tm=512)
    xk, xv = _kv_proj(mem, gmem, kv_g)
    late_w = [shards["w_ff2"]]
    x1, mixed, hq, xq, att, x2, w2_g = _mix_xattn_fwd(x, oa, ob, gx, oq_g, wo_g, xk, xv, tm=512,
                                                      plan=_WeightGather(late_w), plan_args=late_w)
    a, hf, dx3, dx3b, st_loss = _mlp_loss_fwd(x2, gffn, gfin, w1_g, w2_g, target, tm=512)

    da, u, dx2, dx2b, st_ffn = _mlp_bwd(dx3, dx3b, a, x2, gffn, w1_g, w2_g, tm=256)
    g_ff1 = [_halved(_tn_grad("dw_ff1", hf, da, d, d, False, 1024, 1024))]
    dw_ff2, *got = _tn_grad("dw_ff2", u, dx3b, d, d, True, 1024, 1024, plan=_PairExchange(g_ff1), plan_args=g_ff1)
    keep_ff1, send_ff1 = _pair_adds(("w_ff1",), g_ff1, got, idx)
    g_ff2 = [_halved(dw_ff2)]
    dx1, dx1b, dxq, dmix, dxk, dxv, st_x, *got = _xattn_mix_bwd(
        dx2, x1, xq, xk, xv, gx, oq_g, wo_g, tm=512,
        plan=_Plans([_ChipExchange(send_ff1), _PairExchange(g_ff2)]), plan_args=send_ff1 + g_ff2)
    recv_ff1 = got[:1]
    keep_ff2, send_ff2 = _pair_adds(("w_ff2",), g_ff2, got[1:], idx)
    dw = {}
    dw["w_xo"] = _tn_grad("dw_xo", att, dx2b, d, d // N_CHIPS, False, 1024, 256)
    dw["w_xq"] = _tn_grad("dw_xq", hq, dxq, d // N_CHIPS, d, True, 256, 1024)
    dw["w_out"] = _tn_grad("dw_out", mixed, dx1b, d // N_CHIPS, d, True, 256, 1024)
    dw["w_xk"], dw["w_xv"], st_mem = _kv_bwd(mem, gmem, dxk, dxv, kv_g)
    gs_attn = [_halved(dw[k]) for k in ATTN_NAMES]
    dp, d_wp, st_pool, *got_attn = _pool_bwd(z, dmix, wp, psc, tm=512, plan=_PairExchange(gs_attn), plan_args=gs_attn)
    keep_attn, send_attn = _pair_adds(ATTN_NAMES, gs_attn, got_attn, idx)
    sends = send_ff2 + send_attn
    dq, df, di, dg, st_hgrn, *received = _hgrn_bwd(z, o, dmix, st, l0, l1, gn, tc=256, unroll=4,
                                                    plan=_ChipExchange(sends), plan_args=sends)
    grad_x, dz, st_mix = _in_bwd([dq, df, di, dg, dp], dx1, x, gmix, win_g, tm=512)
    keeps = keep_ff1 + keep_ff2 + keep_attn
    received = recv_ff1 + list(received)
    gs_in = [_halved(_tn_grad("dw_in", h, dz, d, win_g.shape[2], False, 1024, win_g.shape[2]))]
    got_in = _run_exchange("grad_pair_exchange_w_in", _PairExchange(gs_in), gs_in)
    keep_in, send_in = _pair_adds(("w_in",), gs_in, got_in, idx)
    recv_in = _run_exchange("grad_chip_exchange_w_in", _ChipExchange(send_in), send_in)

    partials = dict(zip(BIG_NAMES, zip(keeps + keep_in, list(received) + list(recv_in))))
    stats = dict(mix=st_mix, x=st_x, mem=st_mem, ffn=st_ffn, loss=st_loss, hgrn=st_hgrn, pool=st_pool)
    return grad_x, stats, d_wp, partials


def _place():
    x, y, c = lax.axis_index("x"), lax.axis_index("y"), lax.axis_index("c")
    return x, y, c, [(x, 1 - y), (1 - x, y), (1 - x, 1 - y)]


def _rcopy(src, dst, ssem, rsem, dev):
    return pltpu.make_async_remote_copy(src_ref=src, dst_ref=dst, send_sem=ssem, recv_sem=rsem,
                                        device_id=dev, device_id_type=MESH)


class _WeightGather:
    def __init__(self, shards):
        self.n = len(shards)
        self.rows = [w.shape[0] for w in shards]
        self.out_shape = [jax.ShapeDtypeStruct((N_CHIPS,) + w.shape, w.dtype) for w in shards]
        self.scratch_shapes = [pltpu.SemaphoreType.DMA((self.n,))] * 2 + [pltpu.SemaphoreType.DMA((self.n, 3))] * 4

    def _copies(self, ins, outs, sems, with_pass_on):
        lsem, lrsem, ssem, rsem, fsem, frsem = sems
        x, y, c, peers = _place()
        chip = 2 * x + y
        sib = (x, y, 1 - c)
        own = [_rcopy(ins[a], outs[a].at[chip], lsem.at[a], lrsem.at[a], sib) for a in range(self.n)]
        sends, arrived, passed, passed_in = [], [], [], []
        for a in range(self.n):
            hr = self.rows[a] // 2
            half = lambda who, hc, a=a, hr=hr: outs[a].at[who, pl.ds(hc * hr, hr), :]
            for r, (px, py) in enumerate(peers):
                pc = 2 * px + py
                sends.append(_rcopy(ins[a].at[pl.ds(c * hr, hr), :], half(chip, c), ssem.at[a, r], rsem.at[a, r],
                                    (px, py, c)))
                if with_pass_on:
                    arrived.append(_rcopy(half(pc, c), half(pc, c), ssem.at[a, r], rsem.at[a, r], (px, py, c)))
                    passed.append(_rcopy(half(pc, c), half(pc, c), fsem.at[a, r], frsem.at[a, r], sib))
                    passed_in.append(_rcopy(half(pc, 1 - c), half(pc, 1 - c), fsem.at[a, r], frsem.at[a, r], sib))
        return own, sends, arrived, passed, passed_in

    def start(self, ins, outs, sems):
        own, sends, _, _, _ = self._copies(ins, outs, sems, False)
        for cp in own + sends:
            cp.start()

    def finish(self, ins, outs, sems):
        own, sends, arrived, passed, passed_in = self._copies(ins, outs, sems, True)
        for got, fwd in zip(arrived, passed):
            got.wait_recv()
            fwd.start()
        for cp in passed_in:
            cp.wait_recv()
        for cp in sends + passed:
            cp.wait_send()
        for cp in own:
            cp.wait()


class _ChipExchange:
    def __init__(self, sends):
        self.n = len(sends)
        self.out_shape = [jax.ShapeDtypeStruct(g.shape, g.dtype) for g in sends]
        self.scratch_shapes = [pltpu.SemaphoreType.DMA((self.n, 3))] * 2

    def _copies(self, ins, outs, sems):
        ssem, rsem = sems
        _, _, c, peers = _place()
        return [_rcopy(ins[a].at[r], outs[a].at[r], ssem.at[a, r], rsem.at[a, r], (px, py, c))
                for a in range(self.n) for r, (px, py) in enumerate(peers)]

    def start(self, ins, outs, sems):
        for cp in self._copies(ins, outs, sems):
            cp.start()

    def finish(self, ins, outs, sems):
        for cp in self._copies(ins, outs, sems):
            cp.wait()


class _Plans:
    def __init__(self, plans):
        self.plans = plans
        self.n = sum(p.n for p in plans)
        self.out_shape = [s for p in plans for s in p.out_shape]
        self.scratch_shapes = [s for p in plans for s in p.scratch_shapes]

    def _each(self, ins, outs, sems):
        a = b = 0
        for p in self.plans:
            ns = len(p.scratch_shapes)
            yield p, ins[a:a + p.n], outs[a:a + p.n], sems[b:b + ns]
            a, b = a + p.n, b + ns

    def start(self, ins, outs, sems):
        for p, i, o, s in self._each(ins, outs, sems):
            p.start(i, o, s)

    def finish(self, ins, outs, sems):
        for p, i, o, s in self._each(ins, outs, sems):
            p.finish(i, o, s)


def _run_exchange(name, plan, arrays):
    n = plan.n

    def body(*refs):
        ins, outs, sems = refs[:n], refs[n:2 * n], refs[2 * n:]
        plan.start(ins, outs, sems)
        plan.finish(ins, outs, sems)

    return pl.pallas_call(
        body, name=name, in_specs=[ANY] * n, out_specs=[ANY] * n,
        out_shape=plan.out_shape, scratch_shapes=plan.scratch_shapes,
    )(*arrays)


class _PairExchange:
    def __init__(self, gs):
        self.n = len(gs)
        self.out_shape = [jax.ShapeDtypeStruct((g.shape[0],) + g.shape[2:], g.dtype) for g in gs]
        self.scratch_shapes = [pltpu.SemaphoreType.DMA((self.n,))] * 2

    def _copies(self, ins, outs, sems):
        ssem, rsem = sems
        x, y, c, _ = _place()
        return [_rcopy(ins[a].at[:, 1 - c], outs[a], ssem.at[a], rsem.at[a], (x, y, 1 - c)) for a in range(self.n)]

    def start(self, ins, outs, sems):
        for cp in self._copies(ins, outs, sems):
            cp.start()

    def finish(self, ins, outs, sems):
        for cp in self._copies(ins, outs, sems):
            cp.wait()


def _grad_pair_add(name, g, got, idx, tr):
    _, _, hr, cc = g.shape

    def body(idx_ref, g0, g1, g2, g3, r0, r1, r2, r3, keep_ref, send_ref):
        keep_ref[...] = g0[...] + r0[...]
        for q, (gq, rq) in enumerate(((g1, r1), (g2, r2), (g3, r3))):
            send_ref[q] = (gq[...] + rq[...]).astype(BF16)

    gspec = lambda q: pl.BlockSpec((None, None, tr, cc), lambda i, idx: (idx[1 + q], idx[0], i, 0))
    rspec = lambda q: pl.BlockSpec((None, tr, cc), lambda i, idx: (idx[1 + q], i, 0))
    return pl.pallas_call(
        body, name=name,
        grid_spec=pltpu.PrefetchScalarGridSpec(
            num_scalar_prefetch=1, grid=(hr // tr,),
            in_specs=[gspec(q) for q in range(4)] + [rspec(q) for q in range(4)],
            out_specs=[pl.BlockSpec((tr, cc), lambda i, idx: (i, 0)), pl.BlockSpec((3, tr, cc), lambda i, idx: (0, i, 0))]),
        out_shape=[jax.ShapeDtypeStruct((hr, cc), F32), jax.ShapeDtypeStruct((3, hr, cc), BF16)],
        compiler_params=_params(("parallel",)),
    )(idx, g, g, g, g, got, got, got, got)


def _grad_chip_add(name, keep, got, tr):
    hr, cc = keep.shape

    def body(k_ref, g_ref, o_ref):
        o_ref[...] = ((k_ref[...] + g_ref[0].astype(F32)) + g_ref[1].astype(F32)) + g_ref[2].astype(F32)

    return pl.pallas_call(
        body, name=name, grid=(hr // tr,),
        in_specs=[pl.BlockSpec((tr, cc), lambda i: (i, 0)), pl.BlockSpec((3, tr, cc), lambda i: (0, i, 0))],
        out_specs=pl.BlockSpec((tr, cc), lambda i: (i, 0)),
        out_shape=jax.ShapeDtypeStruct((hr, cc), F32),
        compiler_params=_params(("parallel",)),
    )(keep, got)


def _grad_half_exchange(ts):
    n = len(ts)

    def body(*refs):
        ins, outs, ssem, rsem = refs[:n], refs[n:2 * n], refs[2 * n], refs[2 * n + 1]
        x, y, c, _ = _place()
        cps = [_rcopy(ins[a], outs[a], ssem.at[a], rsem.at[a], (x, y, 1 - c)) for a in range(n)]
        for cp in cps:
            cp.start()
        for cp in cps:
            cp.wait()

    return pl.pallas_call(
        body, name="grad_half_exchange",
        in_specs=[ANY] * n, out_specs=[ANY] * n,
        out_shape=[jax.ShapeDtypeStruct(t.shape, t.dtype) for t in ts],
        scratch_shapes=[pltpu.SemaphoreType.DMA((n,))] * 2,
    )(*ts)


def _small_allreduce(stats, d_wp):
    d = D_MODEL
    half = d // 2
    wps = d_wp.shape

    def body(mix_ref, x_ref, mem_ref, ffn_ref, loss_ref, hg_ref, pool_ref, wp_ref, slab_out, wp_out,
             slab_buf, wp_buf, sib_s, sib_w, ssem, rsem):
        x, y, c, peers = _place()
        chip = 2 * x + y
        sib = (x, y, 1 - c)
        hgn = jnp.concatenate([hg_ref[h, 0:1, :] for h in range(HGRN_HEADS)], axis=1)
        dlb = jnp.concatenate([hg_ref[h, 1:2, :] for h in range(HGRN_HEADS)], axis=1)
        slab_buf[0] = jnp.concatenate([
            mix_ref[0:1, :], x_ref[0:1, :], mem_ref[0:1, :], ffn_ref[0:1, :], loss_ref[0:1, :],
            jnp.concatenate([dlb, hgn], axis=1),
            jnp.concatenate([pool_ref[0:1, :], jnp.zeros((1, half), F32)], axis=1),
            loss_ref[1:2, :]], axis=0)
        wp_buf[0] = wp_ref[...]
        pair = [_rcopy(slab_buf.at[0], sib_s, ssem.at[0], rsem.at[0], sib),
                _rcopy(wp_buf.at[0], sib_w, ssem.at[1], rsem.at[1], sib)]
        for cp in pair:
            cp.start()
        for cp in pair:
            cp.wait()
        slab_buf[0] = slab_buf[0] + sib_s[...]
        wp_buf[0] = wp_buf[0] + sib_w[...]
        cps = []
        for r, (px, py) in enumerate(peers):
            cps.append(_rcopy(slab_buf.at[0], slab_buf.at[r + 1], ssem.at[2 + 2 * r], rsem.at[2 + 2 * r], (px, py, c)))
            cps.append(_rcopy(wp_buf.at[0], wp_buf.at[r + 1], ssem.at[3 + 2 * r], rsem.at[3 + 2 * r], (px, py, c)))
        for cp in cps:
            cp.start()
        for cp in cps:
            cp.wait()
        tot_s, tot_w = slab_buf[chip], wp_buf[chip]
        for j in range(1, N_CHIPS):
            tot_s = tot_s + slab_buf[jnp.bitwise_xor(j, chip)]
            tot_w = tot_w + wp_buf[jnp.bitwise_xor(j, chip)]
        slab_out[...] = tot_s
        wp_out[...] = tot_w

    return pl.pallas_call(
        body, name="small_allreduce",
        in_specs=[VMEM] * 8, out_specs=[VMEM] * 2,
        out_shape=[jax.ShapeDtypeStruct((8, d), F32), jax.ShapeDtypeStruct(wps, F32)],
        scratch_shapes=[pltpu.VMEM((N_CHIPS, 8, d), F32), pltpu.VMEM((N_CHIPS,) + wps, F32),
                        pltpu.VMEM((8, d), F32), pltpu.VMEM(wps, F32),
                        pltpu.SemaphoreType.DMA((8,)), pltpu.SemaphoreType.DMA((8,))],
    )(stats["mix"], stats["x"], stats["mem"], stats["ffn"], stats["loss"], stats["hgrn"], stats["pool"], d_wp)


def _adamw_math(w, g, m, v):
    m = ADAM_B1 * m + (1.0 - ADAM_B1) * g
    v = ADAM_B2 * v + (1.0 - ADAM_B2) * (g * g)
    m_hat = m / (1.0 - ADAM_B1 ** ADAM_STEP)
    v_hat = v / (1.0 - ADAM_B2 ** ADAM_STEP)
    delta = -ADAM_LR * (m_hat / (jnp.sqrt(v_hat) + ADAM_EPS) + ADAM_WD * w)
    return delta, m, v


def _adamw(name, mine, theirs, w, m, v, idx, tr):
    rows = w.shape[0]
    cc = mine.shape[1]
    nb = rows // 2 // tr
    heads = w.shape[1] if w.ndim == 3 else 1
    e = cc // heads

    def body(idx_ref, a_ref, b_ref, w_ref, m_ref, v_ref, g_out, d_out, m_out, v_out):
        g = jnp.where(pl.program_id(0) // nb == idx_ref[0], a_ref[...], b_ref[...])
        if w.ndim == 2:
            g_out[...] = g
            d_out[...], m_out[...], v_out[...] = _adamw_math(w_ref[...], g, m_ref[...], v_ref[...])
        else:
            for h in range(heads):
                gh = g[:, h * e:(h + 1) * e]
                g_out[:, h, :] = gh
                d_out[:, h, :], m_out[:, h, :], v_out[:, h, :] = _adamw_math(
                    w_ref[:, h, :], gh, m_ref[:, h, :], v_ref[:, h, :])

    hspec = pl.BlockSpec((tr, cc), lambda i, idx: (i % nb, 0))
    spec = pl.BlockSpec((tr,) + w.shape[1:], lambda i, idx: (i,) + (0,) * (w.ndim - 1))
    return pl.pallas_call(
        body, name=name,
        grid_spec=pltpu.PrefetchScalarGridSpec(
            num_scalar_prefetch=1, grid=(rows // tr,),
            in_specs=[hspec, hspec, spec, spec, spec], out_specs=[spec] * 4),
        out_shape=[jax.ShapeDtypeStruct(w.shape, F32)] * 4,
        compiler_params=_params(("parallel",)),
    )(idx, mine, theirs, w, m, v)


SMALL_NAMES = ("norm_mix_g", "lb_logits", "hgrn_norm_g", "w_pool", "pool_scale", "norm_x_g", "norm_mem_g",
               "norm_ffn_g", "final_norm_g")


def _small_update(slab, d_wp, ws, ms, vs):
    n = len(SMALL_NAMES)
    half = D_MODEL // 2

    def body(slab_ref, wp_ref, *refs):
        w_refs, m_refs, v_refs, outs = refs[:n], refs[n:2 * n], refs[2 * n:3 * n], refs[3 * n:]
        row = lambda k: slab_ref[k:k + 1, :]
        lbl = w_refs[SMALL_NAMES.index("lb_logits")][...]
        s0 = _lower_bound(lbl[0:1, :], lbl[1:2, :])
        dl0 = row(ROW_LB_HGN)[:, :half] * s0 * (1.0 - s0)
        grads = dict(norm_mix_g=row(ROW_GMIX), lb_logits=jnp.concatenate([dl0, -dl0], axis=0),
                     hgrn_norm_g=row(ROW_LB_HGN)[:, half:], w_pool=wp_ref[...], pool_scale=row(ROW_PSCALE)[:, :half],
                     norm_x_g=row(ROW_GX), norm_mem_g=row(ROW_GMEM), norm_ffn_g=row(ROW_GFFN),
                     final_norm_g=row(ROW_GFIN))
        outs[0][...] = row(ROW_LOSS)[:, :128]
        for i, name in enumerate(SMALL_NAMES):
            g = grads[name]
            delta, m2, v2 = _adamw_math(w_refs[i][...], g, m_refs[i][...], v_refs[i][...])
            for o, val in zip(outs[1 + 4 * i:5 + 4 * i], (g, delta, m2, v2)):
                o[...] = val

    args = [ws[k] for k in SMALL_NAMES] + [ms[k] for k in SMALL_NAMES] + [vs[k] for k in SMALL_NAMES]
    out_shape = [jax.ShapeDtypeStruct((1, 128), F32)]
    for k in SMALL_NAMES:
        out_shape += [jax.ShapeDtypeStruct(ws[k].shape, F32)] * 4
    res = pl.pallas_call(
        body, name="small_update",
        in_specs=[VMEM] * (2 + 3 * n), out_specs=[VMEM] * len(out_shape), out_shape=out_shape,
    )(slab, d_wp, *args)
    return res[0], {k: res[1 + 4 * i:5 + 4 * i] for i, k in enumerate(SMALL_NAMES)}


ALL_NAMES = ("norm_mix_g", "w_in", "lb_logits", "hgrn_norm_g", "w_pool", "pool_scale", "w_out", "norm_x_g",
             "norm_mem_g", "w_xq", "w_xk", "w_xv", "w_xo", "norm_ffn_g", "w_ff1", "w_ff2", "final_norm_g")


def _shard_2d(name, a):
    a = a[0]
    if name in ("w_xq", "w_xk", "w_xv"):
        return a.reshape(a.shape[0], -1)
    if name == "w_xo":
        return a.reshape(-1, a.shape[-1])
    return a


def _small_2d(name, a):
    if name == "w_pool":
        return a.reshape(-1, HEAD_DIM)
    if name == "lb_logits":
        return a
    return a.reshape(1, -1)


def kernel(x, mem, norm_mix_g, w_in, lb_logits, hgrn_norm_g, w_pool, pool_scale, w_out, norm_x_g, norm_mem_g, w_xq, w_xk, w_xv, w_xo, norm_ffn_g, w_ff1, w_ff2, final_norm_g, loss_target, m_norm_mix_g, m_w_in, m_lb_logits, m_hgrn_norm_g, m_w_pool, m_pool_scale, m_w_out, m_norm_x_g, m_norm_mem_g, m_w_xq, m_w_xk, m_w_xv, m_w_xo, m_norm_ffn_g, m_w_ff1, m_w_ff2, m_final_norm_g, v_norm_mix_g, v_w_in, v_lb_logits, v_hgrn_norm_g, v_w_pool, v_pool_scale, v_w_out, v_norm_x_g, v_norm_mem_g, v_w_xq, v_w_xk, v_w_xv, v_w_xo, v_norm_ffn_g, v_w_ff1, v_w_ff2, v_final_norm_g):
    w = dict(norm_mix_g=norm_mix_g, w_in=w_in, lb_logits=lb_logits, hgrn_norm_g=hgrn_norm_g, w_pool=w_pool, pool_scale=pool_scale, w_out=w_out, norm_x_g=norm_x_g, norm_mem_g=norm_mem_g, w_xq=w_xq, w_xk=w_xk, w_xv=w_xv, w_xo=w_xo, norm_ffn_g=norm_ffn_g, w_ff1=w_ff1, w_ff2=w_ff2, final_norm_g=final_norm_g)
    m = dict(norm_mix_g=m_norm_mix_g, w_in=m_w_in, lb_logits=m_lb_logits, hgrn_norm_g=m_hgrn_norm_g, w_pool=m_w_pool, pool_scale=m_pool_scale, w_out=m_w_out, norm_x_g=m_norm_x_g, norm_mem_g=m_norm_mem_g, w_xq=m_w_xq, w_xk=m_w_xk, w_xv=m_w_xv, w_xo=m_w_xo, norm_ffn_g=m_norm_ffn_g, w_ff1=m_w_ff1, w_ff2=m_w_ff2, final_norm_g=m_final_norm_g)
    v = dict(norm_mix_g=v_norm_mix_g, w_in=v_w_in, lb_logits=v_lb_logits, hgrn_norm_g=v_hgrn_norm_g, w_pool=v_w_pool, pool_scale=v_pool_scale, w_out=v_w_out, norm_x_g=v_norm_x_g, norm_mem_g=v_norm_mem_g, w_xq=v_w_xq, w_xk=v_w_xk, w_xv=v_w_xv, w_xo=v_w_xo, norm_ffn_g=v_norm_ffn_g, w_ff1=v_w_ff1, w_ff2=v_w_ff2, final_norm_g=v_final_norm_g)

    big_w = {k: _shard_2d(k, w[k]) for k in BIG_NAMES}
    slab_oq = jnp.concatenate([big_w["w_out"], big_w["w_xq"]], axis=0).astype(BF16)
    slab_kv = jnp.concatenate([big_w["w_xk"], big_w["w_xv"]], axis=0).astype(BF16)
    shards = dict(slab_oq=slab_oq, slab_kv=slab_kv,
                  **{k: big_w[k].astype(BF16) for k in ("w_in", "w_xo", "w_ff1", "w_ff2")})

    cx, cy, cc = lax.axis_index("x"), lax.axis_index("y"), lax.axis_index("c")
    chip = 2 * cx + cy
    idx = jnp.stack([cc, chip, chip ^ 1, chip ^ 2, chip ^ 3]).astype(jnp.int32)
    small = {k: w[k] for k in SMALL_NAMES}
    grad_x, stats, d_wp, partials = _step(x[0], mem[0], loss_target[0], small, shards, idx)

    halves = [_grad_chip_add("grad_chip_add_" + k, *partials[k], tr=min(256, partials[k][0].shape[0]))
              for k in BIG_NAMES]
    theirs = _grad_half_exchange(halves)

    grads, deltas, new_m, new_v = {}, {}, {}, {}
    for k, mine, other in zip(BIG_NAMES, halves, theirs):
        as_held = (lambda a: a[0]) if k in ("w_xq", "w_xk", "w_xv") else functools.partial(_shard_2d, k)
        res = _adamw("adamw_" + k, mine, other, as_held(w[k]), as_held(m[k]), as_held(v[k]), idx,
                     tr=min(256, mine.shape[0]))
        for store, val in zip((grads, deltas, new_m, new_v), res):
            store[k] = val.reshape(w[k].shape)

    slab_sum, wp_sum = _small_allreduce(stats, d_wp.reshape(-1, HEAD_DIM))
    loss, upd = _small_update(slab_sum, wp_sum, {k: _small_2d(k, w[k]) for k in SMALL_NAMES},
                              {k: _small_2d(k, m[k]) for k in SMALL_NAMES}, {k: _small_2d(k, v[k]) for k in SMALL_NAMES})
    for k in SMALL_NAMES:
        for store, val in zip((grads, deltas, new_m, new_v), upd[k]):
            store[k] = val.reshape(w[k].shape)

    return (loss[0, 0], grad_x[None], *[grads[k] for k in ALL_NAMES], *[deltas[k] for k in ALL_NAMES],
            *[new_m[k] for k in ALL_NAMES], *[new_v[k] for k in ALL_NAMES])
```

```python
import functools

import jax
import jax.numpy as jnp
from jax import lax
from jax.experimental import pallas as pl
from jax.experimental.pallas import tpu as pltpu

F32 = jnp.float32
BF16 = jnp.bfloat16
LOG2E = 1.4426950408889634
NEG_BIG = -1e30
MAX_LOG2_GROWTH = 100.0
MESH = pl.DeviceIdType.MESH
ANY = pl.BlockSpec(memory_space=pl.ANY)
VMEM = pl.BlockSpec(memory_space=pltpu.VMEM)

D_MODEL = 1024
N_CHIPS = 4
HGRN_HEADS = 4
HEAD_DIM = 128
HGRN_WIDTH = HGRN_HEADS * HEAD_DIM
POOL_WINDOWS = (2, 4, 8, 16)
POOL_HALO = 16
SUB = 16
HALF = SUB // 2
CHUNK = 64
XATTN_HEADS = 4
XATTN_HEAD_DIM = 256
EPS = 1e-6
ADAM_LR, ADAM_B1, ADAM_B2, ADAM_EPS, ADAM_WD, ADAM_STEP = 0.001, 0.9, 0.999, 1e-08, 0.01, 10

V7X_VMEM_BYTES = 64 * 1024 * 1024
VMEM_LIMIT = V7X_VMEM_BYTES - 8 * 1024 * 1024

NN = (((1,), (0,)), ((), ()))
NT = (((1,), (1,)), ((), ()))
TN = (((0,), (0,)), ((), ()))

ROW_GMIX, ROW_GX, ROW_GMEM, ROW_GFFN, ROW_GFIN, ROW_LB_HGN, ROW_PSCALE, ROW_LOSS = range(8)


def _dot(a, b, dims=NN):
    return lax.dot_general(a, b, dims, preferred_element_type=F32)


def _sigmoid(x):
    return 1.0 / (1.0 + jnp.exp(-x))


def _rms_fwd(x, g):
    r = lax.rsqrt(jnp.mean(x * x, axis=-1, keepdims=True) + EPS)
    n = x * r
    return n * g, n, r


def _rms_bwd(dh, n, r, g):
    dn = dh * g
    dx = r * (dn - n * jnp.mean(dn * n, axis=-1, keepdims=True))
    return dx, jnp.sum(dh * n, axis=0, keepdims=True)


def _params(sem=None):
    return pltpu.CompilerParams(dimension_semantics=sem, vmem_limit_bytes=VMEM_LIMIT)


def _const(shape):
    nd = len(shape)
    return pl.BlockSpec(shape, lambda *_: (0,) * nd, pipeline_mode=pl.Buffered(1))


def _const_out(shape):
    nd = len(shape)
    return pl.BlockSpec(shape, lambda *_: (0,) * nd)


def _acc_rows(ref, t, rows):
    upd = jnp.concatenate(rows + [jnp.zeros((8 - len(rows), rows[0].shape[1]), F32)], axis=0)

    @pl.when(t == 0)
    def _():
        ref[...] = upd

    @pl.when(t > 0)
    def _():
        ref[...] = ref[...] + upd


def _fuse_exchange(body, n_in, n_out, n_scratch, plan, ndim):
    if plan is None:
        return body
    n = plan.n

    def wrapped(*refs):
        ins, cin = refs[:n_in], refs[n_in:n_in + n]
        outs, cout = refs[n_in + n:n_in + n + n_out], refs[n_in + n + n_out:n_in + 2 * n + n_out]
        rest = refs[n_in + 2 * n + n_out:]
        scr, csem = rest[:n_scratch], rest[n_scratch:]
        first = pl.program_id(0) == 0
        last = pl.program_id(0) == pl.num_programs(0) - 1
        for i in range(1, ndim):
            first = first & (pl.program_id(i) == 0)
            last = last & (pl.program_id(i) == pl.num_programs(i) - 1)

        @pl.when(first)
        def _():
            plan.start(cin, cout, csem)

        body(*ins, *outs, *scr)

        @pl.when(last)
        def _():
            plan.finish(cin, cout, csem)

    return wrapped


def _plan_extras(plan):
    if plan is None:
        return [], [], []
    return [ANY] * plan.n, list(plan.out_shape), list(plan.scratch_shapes)


def _in_proj(x, g, win_g, tm, plan=None, plan_args=()):
    s, d = x.shape
    nsh, _, wc = win_g.shape

    def body(x_ref, g_ref, w_ref, z_ref, h_ref):
        h, _, _ = _rms_fwd(x_ref[...], g_ref[...])
        hb = h.astype(BF16)
        h_ref[...] = hb
        for j in range(nsh):
            z_ref[:, j * wc:(j + 1) * wc] = _dot(hb, w_ref[j])

    x_specs, x_shapes, x_scratch = _plan_extras(plan)
    return pl.pallas_call(
        _fuse_exchange(body, 3, 2, 0, plan, 1), name="in_proj", grid=(s // tm,),
        in_specs=[pl.BlockSpec((tm, d), lambda t: (t, 0)), _const((1, d)), _const((nsh, d, wc))] + x_specs,
        out_specs=[pl.BlockSpec((tm, nsh * wc), lambda t: (t, 0)), pl.BlockSpec((tm, d), lambda t: (t, 0))] + x_specs,
        out_shape=[jax.ShapeDtypeStruct((s, nsh * wc), F32), jax.ShapeDtypeStruct((s, d), BF16)] + x_shapes,
        scratch_shapes=x_scratch,
        compiler_params=_params(("arbitrary",)),
    )(x, g, win_g, *plan_args)


def _lower_bound(l0, l1):
    m = jnp.maximum(l0, l1)
    e0, e1 = jnp.exp(l0 - m), jnp.exp(l1 - m)
    return e0 / (e0 + e1)


def _block_tri(n, group, upper):
    r = lax.broadcasted_iota(jnp.int32, (n, n), 0)
    c = lax.broadcasted_iota(jnp.int32, (n, n), 1)
    keep = (r // group == c // group) & ((c >= r) if upper else (c <= r))
    return keep.astype(BF16)


def _group_cumsum(tri, x):
    hi = x.astype(BF16)
    rest = x - hi.astype(F32)
    mid = rest.astype(BF16)
    lo = (rest - mid.astype(F32)).astype(BF16)
    return (_dot(tri, hi) + _dot(tri, mid)) + _dot(tri, lo)


def _decay(b, bj, rows, first):
    d = b - bj
    if first:
        d = jnp.where(rows >= first, d, NEG_BIG)
    return jnp.exp2(d)


class _RowSums:
    ORDER = (0, 4, 2, 6, 1, 5, 3, 7)

    def __init__(self, rows):
        self.rows = rows
        self.level = {4: {}, 2: {}, 1: {}}

    def _pair(self, p, q, d):
        return jnp.where((self.rows & d) != 0, p + pltpu.roll(p, d, axis=0), q + pltpu.roll(q, HALF - d, axis=0))

    def push(self, j, y, d=4):
        if d == 0:
            self.out = y
            return
        slot = self.level[d]
        key = j % d
        if key not in slot:
            slot[key] = (j, y)
            return
        j0, y0 = slot.pop(key)
        p, q = (y, y0) if j & d else (y0, y)
        self.push(key, self._pair(p, q, d), d // 2)

    def result(self):
        return self.out


def _hgrn_gates(qp, fp, lb):
    sq = _sigmoid(qp)
    sf = _sigmoid(fp)
    f = lb + (1.0 - lb) * sf
    return qp * sq, sq, f, sf


def _hgrn_fwd(z, l0, l1, gn, tc, unroll=1, plan=None, plan_args=()):
    s = z.shape[0]
    nsub = tc // SUB
    hd = HEAD_DIM

    def body(q_ref, f_ref, v_ref, g_ref, l0_ref, l1_ref, gn_ref, tri_ref, tric_ref, o_ref, oa_ref, st_ref,
             state, qs, ks, bs, os_):
        @pl.when(pl.program_id(1) == 0)
        def _():
            state[...] = jnp.zeros_like(state)

        lb = _lower_bound(l0_ref[...], l1_ref[...])
        q, _, f, _ = _hgrn_gates(q_ref[...], f_ref[...], lb)
        k = 1.0 - f
        lf = jnp.log(f) * LOG2E
        bc = _group_cumsum(tric_ref[...], lf)
        bounded = jnp.min(bc) >= -MAX_LOG2_GROWTH

        @pl.when(bounded)
        def _():
            qt = (q * jnp.exp2(bc)).astype(BF16)
            ki = (k * jnp.exp2(-bc)).astype(BF16)
            vb = v_ref[...].astype(BF16)
            a = jnp.where(tric_ref[...] > 0, _dot(qt, ki, NT), 0.0).astype(BF16)
            o_in = _dot(a, vb)
            for c in range(tc // CHUNK):
                rs = slice(c * CHUNK, (c + 1) * CHUNK)
                st = state[...]
                st_ref[c] = st
                os_[rs, :] = o_in[rs] + _dot(qt[rs], st.astype(BF16), NT)
                bl = bc[(c + 1) * CHUNK - 1:(c + 1) * CHUNK, :]
                kt = (k[rs] * jnp.exp2(bl - bc[rs])).astype(BF16)
                state[...] = st * jnp.exp2(bl) + _dot(vb[rs], kt, TN)

        @pl.when(jnp.logical_not(bounded))
        def _():
            qs[...] = q
            ks[...] = k
            bs[...] = _group_cumsum(tri_ref[...], lf)
            rows = lax.broadcasted_iota(jnp.int32, (HALF, 1), 0)

            def step(i, carry):
                r0 = pl.multiple_of(i * SUB, SUB)
                q_ = qs[pl.ds(r0, SUB), :]
                k_ = ks[pl.ds(r0, SUB), :]
                b_ = bs[pl.ds(r0, SUB), :]
                v_ = v_ref[pl.ds(r0, SUB), :]
                st = state[...]

                @pl.when(i % (CHUNK // SUB) == 0)
                def _():
                    st_ref[i // (CHUNK // SUB)] = st

                bl = b_[SUB - 1:SUB, :]
                o = _dot((q_ * jnp.exp2(b_)).astype(BF16), st.astype(BF16), NT)
                (q_lo, q_hi), (b_lo, b_hi), (o_lo, o_hi) = ((x[:HALF], x[HALF:]) for x in (q_, b_, o))
                for j in range(SUB):
                    bj, kj, vj = b_[j:j + 1, :], k_[j:j + 1, :], v_[j:j + 1, :]
                    if j < HALF:
                        e = _decay(b_lo, bj, rows, j)
                        o_lo = o_lo + jnp.sum(q_lo * e * kj, axis=-1, keepdims=True) * vj
                    e = _decay(b_hi, bj, rows, j - HALF if j > HALF else None)
                    o_hi = o_hi + jnp.sum(q_hi * e * kj, axis=-1, keepdims=True) * vj
                os_[pl.ds(r0, HALF), :] = o_lo
                os_[pl.ds(r0 + HALF, HALF), :] = o_hi
                kt = (k_ * jnp.exp2(bl - b_)).astype(BF16)
                state[...] = st * jnp.exp2(bl) + _dot(v_.astype(BF16), kt, TN)
                return carry

            lax.fori_loop(0, nsub, step, 0, unroll=unroll)

        o = os_[...]
        o_ref[...] = o
        r = lax.rsqrt(jnp.mean(o * o, axis=-1, keepdims=True) + EPS)
        gp = g_ref[...]
        oa_ref[...] = (o * r * gn_ref[...] * (gp * _sigmoid(gp))).astype(BF16)

    col = lambda k: pl.BlockSpec((tc, hd), lambda h, t: (t, k * HGRN_HEADS + h))
    vec = pl.BlockSpec((None, 1, hd), lambda h, t: (h, 0, 0))
    x_specs, x_shapes, x_scratch = _plan_extras(plan)
    return pl.pallas_call(
        _fuse_exchange(body, 9, 3, 5, plan, 2), name="hgrn_fwd", grid=(HGRN_HEADS, s // tc),
        in_specs=[col(0), col(1), col(2), col(3), vec, vec, vec, _const((tc, tc)), _const((tc, tc))] + x_specs,
        out_specs=[pl.BlockSpec((tc, hd), lambda h, t: (t, h)), pl.BlockSpec((tc, hd), lambda h, t: (t, h)),
                   pl.BlockSpec((None, tc // CHUNK, hd, hd), lambda h, t: (h, t, 0, 0))] + x_specs,
        out_shape=[jax.ShapeDtypeStruct((s, HGRN_WIDTH), F32), jax.ShapeDtypeStruct((s, HGRN_WIDTH), BF16),
                   jax.ShapeDtypeStruct((HGRN_HEADS, s // CHUNK, hd, hd), F32)] + x_shapes,
        scratch_shapes=[pltpu.VMEM((hd, hd), F32)] + [pltpu.VMEM((tc, hd), F32)] * 4 + x_scratch,
        compiler_params=_params(("arbitrary", "arbitrary")),
    )(z, z, z, z, l0, l1, gn, _block_tri(tc, SUB, False), _block_tri(tc, CHUNK, False), *plan_args)


def _pooled(p, ext, tok0):
    tm = p.shape[0]
    tok = tok0 + lax.broadcasted_iota(jnp.int32, (tm, 1), 0)
    outs = []
    for g, w in enumerate(POOL_WINDOWS):
        acc = ext[:, g * HEAD_DIM:(g + 1) * HEAD_DIM]
        sh = 1
        while sh < w:
            acc = acc + pltpu.roll(acc, sh, axis=0)
            sh *= 2
        cnt = jnp.minimum(tok + 1, w).astype(F32)
        outs.append(acc[POOL_HALO:, :] / cnt - p[:, g * HEAD_DIM:(g + 1) * HEAD_DIM])
    return outs


def _pool_fwd(z, wp, scale, tm):
    s = z.shape[0]
    pw = len(POOL_WINDOWS) * HEAD_DIM
    nb = tm // POOL_HALO

    def body(p_ref, prev_ref, wp_ref, sc_ref, ob_ref):
        t = pl.program_id(0)
        p = p_ref[...]
        prev = jnp.where(t > 0, prev_ref[...], 0.0)
        pooled = _pooled(p, jnp.concatenate([prev, p], axis=0), t * tm)
        ys = [_dot(pooled[g].astype(BF16), wp_ref[g].astype(BF16)) for g in range(len(POOL_WINDOWS))]
        ob_ref[...] = (jnp.concatenate(ys, axis=1) * sc_ref[...]).astype(BF16)

    return pl.pallas_call(
        body, name="pool_fwd", grid=(s // tm,),
        in_specs=[pl.BlockSpec((tm, pw), lambda t: (t, 4)),
                  pl.BlockSpec((POOL_HALO, pw), lambda t: (jnp.maximum(t * nb - 1, 0), 4)),
                  _const(wp.shape), _const((1, pw))],
        out_specs=pl.BlockSpec((tm, pw), lambda t: (t, 0)),
        out_shape=jax.ShapeDtypeStruct((s, pw), BF16),
        compiler_params=_params(("parallel",)),
    )(z, z, wp, scale)


def _kv_proj(mem, g, slab_g):
    m, d = mem.shape
    rows = d // N_CHIPS

    def body(mem_ref, g_ref, wk_ref, wv_ref, xk_ref, xv_ref):
        hm, _, _ = _rms_fwd(mem_ref[...], g_ref[...])
        hb = hm.astype(BF16)
        xk_ref[...] = _dot(hb, wk_ref[...].reshape(d, d)).astype(BF16)
        xv_ref[...] = _dot(hb, wv_ref[...].reshape(d, d)).astype(BF16)

    blk = lambda k: pl.BlockSpec((N_CHIPS, rows, d), lambda i: (0, k, 0))
    return pl.pallas_call(
        body, name="kv_proj", grid=(1,),
        in_specs=[_const((m, d)), _const((1, d)), blk(0), blk(1)],
        out_specs=[_const_out((m, d)), _const_out((m, d))],
        out_shape=[jax.ShapeDtypeStruct((m, d), BF16)] * 2,
        compiler_params=_params(("arbitrary",)),
    )(mem, g, slab_g, slab_g)


def _softmax_rows(sc):
    e = jnp.exp(sc - jnp.max(sc, axis=-1, keepdims=True))
    return e / jnp.sum(e, axis=-1, keepdims=True)


def _mix_xattn_fwd(x, oa, ob, gx, slab_g, wo_g, xk, xv, tm, plan=None, plan_args=()):
    s, d = x.shape
    m = xk.shape[0]
    rows = d // N_CHIPS
    hw = oa.shape[1]
    e = XATTN_HEAD_DIM

    def body(x_ref, oa_ref, ob_ref, gx_ref, wout_ref, wq_ref, wo_ref, xk_ref, xv_ref,
             x1_ref, mixed_ref, hq_ref, xq_ref, att_ref, x2_ref):
        mixed = jnp.concatenate([oa_ref[...], ob_ref[...]], axis=1)
        mixed_ref[...] = mixed
        x1 = x_ref[...] + _dot(mixed, wout_ref[...].reshape(d, d))
        x1_ref[...] = x1
        hq, _, _ = _rms_fwd(x1, gx_ref[...])
        hqb = hq.astype(BF16)
        hq_ref[...] = hqb
        xq = _dot(hqb, wq_ref[...].reshape(d, d)).astype(BF16)
        xq_ref[...] = xq
        atts = []
        for h in range(XATTN_HEADS):
            cs = slice(h * e, (h + 1) * e)
            p = _softmax_rows(_dot(xq[:, cs], xk_ref[:, cs], NT) * (e ** -0.5))
            atts.append(_dot(p.astype(BF16), xv_ref[:, cs]).astype(BF16))
        att = jnp.concatenate(atts, axis=1)
        att_ref[...] = att
        for j in range(N_CHIPS):
            x2_ref[:, j * rows:(j + 1) * rows] = x1[:, j * rows:(j + 1) * rows] + _dot(att, wo_ref[j])

    tile = lambda w: pl.BlockSpec((tm, w), lambda t: (t, 0))
    blk = lambda k: pl.BlockSpec((N_CHIPS, rows, d), lambda t: (0, k, 0), pipeline_mode=pl.Buffered(1))
    x_specs, x_shapes, x_scratch = _plan_extras(plan)
    return pl.pallas_call(
        _fuse_exchange(body, 9, 6, 0, plan, 1), name="mix_xattn_fwd", grid=(s // tm,),
        in_specs=[tile(d), tile(hw), tile(hw), _const((1, d)), blk(0), blk(1), _const(wo_g.shape),
                  _const((m, d)), _const((m, d))] + x_specs,
        out_specs=[tile(d)] * 6 + x_specs,
        out_shape=[jax.ShapeDtypeStruct((s, d), F32)] + [jax.ShapeDtypeStruct((s, d), BF16)] * 4
                  + [jax.ShapeDtypeStruct((s, d), F32)] + x_shapes,
        scratch_shapes=x_scratch,
        compiler_params=_params(("arbitrary",)),
    )(x, oa, ob, gx, slab_g, slab_g, wo_g, xk, xv, *plan_args)


def _mlp_loss_fwd(x2, gffn, gfin, w1_g, w2_g, target, tm):
    s, d = x2.shape
    wr = w1_g.shape[1]

    def body(x2_ref, gffn_ref, gfin_ref, w1_ref, w2_ref, tg_ref, a_ref, hf_ref, dx3_ref, dx3b_ref, st_ref):
        x2v = x2_ref[...]
        hf, _, _ = _rms_fwd(x2v, gffn_ref[...])
        hfb = hf.astype(BF16)
        hf_ref[...] = hfb
        acc = x2v
        for j in range(N_CHIPS):
            a = _dot(hfb, w1_ref[j])
            a_ref[:, j * wr:(j + 1) * wr] = a
            r = jnp.maximum(a, 0.0)
            acc = acc + _dot((r * r).astype(BF16), w2_ref[j])
        gf = gfin_ref[...]
        y, n, r3 = _rms_fwd(acc, gf)
        err = y - tg_ref[...]
        loss = 0.5 * jnp.sum(jnp.sum(err * err, axis=-1, keepdims=True) * (1.0 / d), axis=0, keepdims=True)
        dy = err * (1.0 / d)
        dx3, dgf = _rms_bwd(dy, n, r3, gf)
        dx3_ref[...] = dx3
        dx3b_ref[...] = dx3.astype(BF16)
        _acc_rows(st_ref, pl.program_id(0), [dgf, jnp.broadcast_to(loss, (1, d))])

    tile = lambda w: pl.BlockSpec((tm, w), lambda t: (t, 0))
    blk = lambda k: pl.BlockSpec((N_CHIPS, wr, d), lambda t: (0, k, 0), pipeline_mode=pl.Buffered(1))
    return pl.pallas_call(
        body, name="mlp_loss_fwd", grid=(s // tm,),
        in_specs=[tile(d), _const((1, d)), _const((1, d)), blk(0), blk(0), tile(d)],
        out_specs=[tile(N_CHIPS * wr), tile(d), tile(d), tile(d), _const_out((8, d))],
        out_shape=[jax.ShapeDtypeStruct((s, N_CHIPS * wr), F32), jax.ShapeDtypeStruct((s, d), BF16),
                   jax.ShapeDtypeStruct((s, d), F32), jax.ShapeDtypeStruct((s, d), BF16),
                   jax.ShapeDtypeStruct((8, d), F32)],
        compiler_params=_params(("arbitrary",)),
    )(x2, gffn, gfin, w1_g, w2_g, target)


def _mlp_bwd(dx3, dx3b, a, x2, gffn, w1_g, w2_g, tm):
    s, d = x2.shape
    wr = w1_g.shape[1]

    def body(dx3_ref, dx3b_ref, a_ref, x2_ref, g_ref, w1_ref, w2_ref, da_ref, u_ref, dx2_ref, dx2b_ref, st_ref):
        dyb = dx3b_ref[...]
        dhf = jnp.zeros((tm, d), F32)
        for j in range(N_CHIPS):
            r = jnp.maximum(a_ref[:, j * wr:(j + 1) * wr], 0.0)
            da = (_dot(dyb, w2_ref[j], NT) * (2.0 * r)).astype(BF16)
            da_ref[:, j * wr:(j + 1) * wr] = da
            u_ref[:, j * wr:(j + 1) * wr] = (r * r).astype(BF16)
            dhf = dhf + _dot(da, w1_ref[j], NT)
        g = g_ref[...]
        _, n, r2 = _rms_fwd(x2_ref[...], g)
        dxn, dg = _rms_bwd(dhf, n, r2, g)
        dx2 = dx3_ref[...] + dxn
        dx2_ref[...] = dx2
        dx2b_ref[...] = dx2.astype(BF16)
        _acc_rows(st_ref, pl.program_id(0), [dg])

    tile = lambda w: pl.BlockSpec((tm, w), lambda t: (t, 0))
    blk = lambda k: pl.BlockSpec((N_CHIPS, wr, d), lambda t: (0, k, 0), pipeline_mode=pl.Buffered(1))
    nf = N_CHIPS * wr
    return pl.pallas_call(
        body, name="mlp_bwd", grid=(s // tm,),
        in_specs=[tile(d), tile(d), tile(nf), tile(d), _const((1, d)), blk(0), blk(0)],
        out_specs=[tile(nf), tile(nf), tile(d), tile(d), _const_out((8, d))],
        out_shape=[jax.ShapeDtypeStruct((s, nf), BF16), jax.ShapeDtypeStruct((s, nf), BF16),
                   jax.ShapeDtypeStruct((s, d), F32), jax.ShapeDtypeStruct((s, d), BF16),
                   jax.ShapeDtypeStruct((8, d), F32)],
        compiler_params=_params(("arbitrary",)),
    )(dx3, dx3b, a, x2, gffn, w1_g, w2_g)


def _xattn_mix_bwd(dx2, x1, xq, xk, xv, gx, slab_g, wo_g, tm, plan=None, plan_args=()):
    s, d = x1.shape
    m = xk.shape[0]
    rows = d // N_CHIPS
    e = XATTN_HEAD_DIM

    def body(dx2_ref, x1_ref, xq_ref, xk_ref, xv_ref, gx_ref, wout_ref, wq_ref, wo_ref,
             dx1_ref, dx1b_ref, dxq_ref, dmix_ref, dxk_ref, dxv_ref, st_ref):
        t = pl.program_id(0)
        dx2 = dx2_ref[...]
        dx2b = dx2.astype(BF16)
        datt = jnp.zeros((tm, d), F32)
        for j in range(N_CHIPS):
            datt = datt + _dot(dx2b[:, j * rows:(j + 1) * rows], wo_ref[j], NT)
        dattb = datt.astype(BF16)
        dxqs, dxks, dxvs = [], [], []
        for h in range(XATTN_HEADS):
            cs = slice(h * e, (h + 1) * e)
            xq_h, xk_h, xv_h = xq_ref[:, cs], xk_ref[:, cs], xv_ref[:, cs]
            p = _softmax_rows(_dot(xq_h, xk_h, NT) * (e ** -0.5))
            dp = _dot(dattb[:, cs], xv_h, NT)
            ds = (p * (dp - jnp.sum(dp * p, axis=-1, keepdims=True)) * (e ** -0.5)).astype(BF16)
            dxqs.append(_dot(ds, xk_h).astype(BF16))
            dxks.append(_dot(ds, xq_h, TN))
            dxvs.append(_dot(p.astype(BF16), dattb[:, cs], TN))
        dxq = jnp.concatenate(dxqs, axis=1)
        dxq_ref[...] = dxq
        dxk = jnp.concatenate(dxks, axis=1)
        dxv = jnp.concatenate(dxvs, axis=1)

        @pl.when(t == 0)
        def _():
            dxk_ref[...] = dxk
            dxv_ref[...] = dxv

        @pl.when(t > 0)
        def _():
            dxk_ref[...] = dxk_ref[...] + dxk
            dxv_ref[...] = dxv_ref[...] + dxv

        dhq = jnp.concatenate([_dot(dxq, wq_ref[j], NT) for j in range(N_CHIPS)], axis=1)
        g = gx_ref[...]
        _, n, r1 = _rms_fwd(x1_ref[...], g)
        dxn, dg = _rms_bwd(dhq, n, r1, g)
        dx1 = dx2 + dxn
        dx1_ref[...] = dx1
        dx1b = dx1.astype(BF16)
        dx1b_ref[...] = dx1b
        for j in range(N_CHIPS):
            dmix_ref[:, j * rows:(j + 1) * rows] = _dot(dx1b, wout_ref[j], NT)
        _acc_rows(st_ref, t, [dg])

    tile = lambda: pl.BlockSpec((tm, d), lambda t: (t, 0))
    blk = lambda k: pl.BlockSpec((N_CHIPS, rows, d), lambda t: (0, k, 0), pipeline_mode=pl.Buffered(1))
    x_specs, x_shapes, x_scratch = _plan_extras(plan)
    return pl.pallas_call(
        _fuse_exchange(body, 9, 7, 0, plan, 1), name="xattn_mix_bwd", grid=(s // tm,),
        in_specs=[tile(), tile(), tile(), _const((m, d)), _const((m, d)), _const((1, d)), blk(0), blk(1),
                  _const(wo_g.shape)] + x_specs,
        out_specs=[tile(), tile(), tile(), tile(), _const_out((m, d)), _const_out((m, d)), _const_out((8, d))]
                  + x_specs,
        out_shape=[jax.ShapeDtypeStruct((s, d), F32), jax.ShapeDtypeStruct((s, d), BF16),
                   jax.ShapeDtypeStruct((s, d), BF16), jax.ShapeDtypeStruct((s, d), F32),
                   jax.ShapeDtypeStruct((m, d), F32), jax.ShapeDtypeStruct((m, d), F32),
                   jax.ShapeDtypeStruct((8, d), F32)] + x_shapes,
        scratch_shapes=x_scratch,
        compiler_params=_params(("arbitrary",)),
    )(dx2, x1, xq, xk, xv, gx, slab_g, slab_g, wo_g, *plan_args)


def _kv_bwd(mem, g, dxk, dxv, slab_g):
    m, d = mem.shape
    rows = d // N_CHIPS

    def body(mem_ref, g_ref, dxk_ref, dxv_ref, wk_ref, wv_ref, dwk_ref, dwv_ref, st_ref):
        gv = g_ref[...]
        hm, n, _ = _rms_fwd(mem_ref[...], gv)
        hb = hm.astype(BF16)
        dkb = dxk_ref[...].astype(BF16)
        dvb = dxv_ref[...].astype(BF16)
        dhm = []
        for j in range(N_CHIPS):
            hj = hb[:, j * rows:(j + 1) * rows]
            dwk_ref[j] = _dot(hj, dkb, TN)
            dwv_ref[j] = _dot(hj, dvb, TN)
            dhm.append(_dot(dkb, wk_ref[j], NT) + _dot(dvb, wv_ref[j], NT))
        dg = jnp.sum(jnp.concatenate(dhm, axis=1) * n, axis=0, keepdims=True)
        st_ref[...] = jnp.concatenate([dg, jnp.zeros((7, d), F32)], axis=0)

    blk = lambda k: pl.BlockSpec((N_CHIPS, rows, d), lambda i: (0, k, 0))
    return pl.pallas_call(
        body, name="kv_bwd", grid=(1,),
        in_specs=[_const((m, d)), _const((1, d)), _const((m, d)), _const((m, d)), blk(0), blk(1)],
        out_specs=[_const_out((N_CHIPS, rows, d)), _const_out((N_CHIPS, rows, d)), _const_out((8, d))],
        out_shape=[jax.ShapeDtypeStruct((N_CHIPS, rows, d), F32)] * 2 + [jax.ShapeDtypeStruct((8, d), F32)],
        compiler_params=_params(("arbitrary",)),
    )(mem, g, dxk, dxv, slab_g, slab_g)


def _pool_bwd(z, dmix, wp, scale, tm, plan=None, plan_args=()):
    s = z.shape[0]
    ng = len(POOL_WINDOWS)
    pw = ng * HEAD_DIM
    nb = tm // POOL_HALO
    nt = s // tm
    n_ext = tm + POOL_HALO

    def body(p_ref, prev_ref, dm_ref, dmn_ref, wp_ref, sc_ref, dp_ref, dwp_ref, st_ref):
        t = pl.program_id(0)
        p = p_ref[...]
        prev = jnp.where(t > 0, prev_ref[...], 0.0)
        pooled = _pooled(p, jnp.concatenate([prev, p], axis=0), t * tm)
        dm = dm_ref[...]
        dme = jnp.concatenate([dm, jnp.where(t < nt - 1, dmn_ref[...], 0.0)], axis=0) * sc_ref[...]
        tok = t * tm + lax.broadcasted_iota(jnp.int32, (n_ext, 1), 0)
        dsc, dps, dwps = [], [], []
        for g, w in enumerate(POOL_WINDOWS):
            cs = slice(g * HEAD_DIM, (g + 1) * HEAD_DIM)
            wpb = wp_ref[g].astype(BF16)
            pb = pooled[g].astype(BF16)
            dsc.append(jnp.sum(dm[:, cs] * _dot(pb, wpb), axis=0, keepdims=True))
            dye = dme[:, cs].astype(BF16)
            dwps.append(_dot(pb, dye[:tm], TN))
            dpe = _dot(dye, wpb, NT)
            acc = dpe / jnp.minimum(tok + 1, w).astype(F32)
            sh = 1
            while sh < w:
                acc = acc + pltpu.roll(acc, n_ext - sh, axis=0)
                sh *= 2
            dps.append(acc[:tm] - dpe[:tm])
        dp_ref[...] = jnp.concatenate(dps, axis=1)
        dsc_row = jnp.concatenate(dsc, axis=1)

        @pl.when(t == 0)
        def _():
            for g in range(ng):
                dwp_ref[g] = dwps[g]

        @pl.when(t > 0)
        def _():
            for g in range(ng):
                dwp_ref[g] = dwp_ref[g] + dwps[g]

        _acc_rows(st_ref, t, [dsc_row])

    x_specs, x_shapes, x_scratch = _plan_extras(plan)
    return pl.pallas_call(
        _fuse_exchange(body, 6, 3, 0, plan, 1), name="pool_bwd", grid=(nt,),
        in_specs=[pl.BlockSpec((tm, pw), lambda t: (t, 4)),
                  pl.BlockSpec((POOL_HALO, pw), lambda t: (jnp.maximum(t * nb - 1, 0), 4)),
                  pl.BlockSpec((tm, pw), lambda t: (t, 1)),
                  pl.BlockSpec((POOL_HALO, pw), lambda t: (jnp.minimum((t + 1) * nb, s // POOL_HALO - 1), 1)),
                  _const(wp.shape), _const((1, pw))] + x_specs,
        out_specs=[pl.BlockSpec((tm, pw), lambda t: (t, 0)), _const_out(wp.shape), _const_out((8, pw))] + x_specs,
        out_shape=[jax.ShapeDtypeStruct((s, pw), F32), jax.ShapeDtypeStruct(wp.shape, F32),
                   jax.ShapeDtypeStruct((8, pw), F32)] + x_shapes,
        scratch_shapes=x_scratch,
        compiler_params=_params(("arbitrary",)),
    )(z, z, dmix, dmix, wp, scale, *plan_args)


def _hgrn_bwd(z, o, dmix, st, l0, l1, gn, tc, unroll=1, plan=None, plan_args=()):
    s = z.shape[0]
    nsub = tc // SUB
    nt = s // tc
    hd = HEAD_DIM

    def body(q_ref, f_ref, v_ref, g_ref, l0_ref, l1_ref, gn_ref, o_ref, dm_ref, st_ref,
             tril_ref, triu_ref, trilc_ref, triuc_ref,
             dq_ref, df_ref, di_ref, dg_ref, stat_ref, dstate, qs, ks, bs, dos, dqs, dks, dbs, sts):
        t = pl.program_id(1)

        @pl.when(t == 0)
        def _():
            dstate[...] = jnp.zeros_like(dstate)

        lb = _lower_bound(l0_ref[...], l1_ref[...])
        qp = q_ref[...]
        q, sq, f, sf = _hgrn_gates(qp, f_ref[...], lb)
        k = 1.0 - f
        lf = jnp.log(f) * LOG2E
        bc = _group_cumsum(trilc_ref[...], lf)
        bounded = jnp.min(bc) >= -MAX_LOG2_GROWTH

        o = o_ref[...]
        r = lax.rsqrt(jnp.mean(o * o, axis=-1, keepdims=True) + EPS)
        n = o * r
        gnv = gn_ref[...]
        gp = g_ref[...]
        sg = _sigmoid(gp)
        dm = dm_ref[...]
        dg_ref[...] = dm * (n * gnv) * (sg * (1.0 + gp * (1.0 - sg)))
        don = dm * (gp * sg)
        dgn = jnp.sum(don * n, axis=0, keepdims=True)
        dn = don * gnv
        do_all = r * (dn - n * jnp.mean(dn * n, axis=-1, keepdims=True))

        @pl.when(bounded)
        def _():
            eb = jnp.exp2(bc)
            eib = jnp.exp2(-bc)
            qt = (q * eb).astype(BF16)
            ki = (k * eib).astype(BF16)
            vb = v_ref[...].astype(BF16)
            dob = do_all.astype(BF16)
            mask = trilc_ref[...] > 0
            a = jnp.where(mask, _dot(qt, ki, NT), 0.0).astype(BF16)
            da = jnp.where(mask, _dot(dob, vb, NT), 0.0).astype(BF16)
            dq_in = _dot(da, ki)
            dk_in = _dot(da, qt, TN)
            dv_in = _dot(a, dob, TN)
            last_row = lax.broadcasted_iota(jnp.int32, (CHUNK, 1), 0) == CHUNK - 1
            for c in reversed(range(tc // CHUNK)):
                rs = slice(c * CHUNK, (c + 1) * CHUNK)
                stp = st_ref[c]
                dst = dstate[...]
                dstb = dst.astype(BF16)
                bl = bc[(c + 1) * CHUNK - 1:(c + 1) * CHUNK, :]
                ekl = jnp.exp2(bl - bc[rs])
                ebl = jnp.exp2(bl)
                kt = k[rs] * ekl
                dq_st = _dot(dob[rs], stp.astype(BF16)) * eb[rs]
                dkt = _dot(vb[rs], dstb)
                extra = jnp.sum(kt * dkt, axis=0, keepdims=True) + ebl * jnp.sum(stp * dst, axis=0, keepdims=True)
                dqs[rs, :] = dq_st + dq_in[rs] * eb[rs]
                dks[rs, :] = dkt * ekl + dk_in[rs] * eib[rs]
                di_ref[rs, :] = _dot(kt.astype(BF16), dstb, NT) + dv_in[rs]
                dbs[rs, :] = (q[rs] * dq_st - kt * dkt + jnp.where(last_row, extra, 0.0)
                              + (qt[rs].astype(F32) * dq_in[rs] - ki[rs].astype(F32) * dk_in[rs]))
                dstate[...] = dst * ebl + _dot(dob[rs], qt[rs], TN)
            dbs[...] = _group_cumsum(triuc_ref[...], dbs[...])

        @pl.when(jnp.logical_not(bounded))
        def _():
            qs[...] = q
            ks[...] = k
            bs[...] = _group_cumsum(tril_ref[...], lf)
            dos[...] = do_all
            per = CHUNK // SUB

            def restore(i, carry):
                @pl.when(i % per == 0)
                def _():
                    sts[i] = st_ref[i // per]

                @pl.when(i % per != 0)
                def _():
                    rp = pl.multiple_of((i - 1) * SUB, SUB)
                    b_ = bs[pl.ds(rp, SUB), :]
                    bl = b_[SUB - 1:SUB, :]
                    kt = (ks[pl.ds(rp, SUB), :] * jnp.exp2(bl - b_)).astype(BF16)
                    sts[i] = sts[i - 1] * jnp.exp2(bl) + _dot(v_ref[pl.ds(rp, SUB), :].astype(BF16), kt, TN)

                return carry

            lax.fori_loop(0, nsub, restore, 0)
            rows = lax.broadcasted_iota(jnp.int32, (HALF, 1), 0)
            last_row = lax.broadcasted_iota(jnp.int32, (SUB, 1), 0) == SUB - 1

            def step(i, carry):
                ii = nsub - 1 - i
                r0 = pl.multiple_of(ii * SUB, SUB)
                q_ = qs[pl.ds(r0, SUB), :]
                k_ = ks[pl.ds(r0, SUB), :]
                b_ = bs[pl.ds(r0, SUB), :]
                v_ = v_ref[pl.ds(r0, SUB), :]
                do_ = dos[pl.ds(r0, SUB), :]
                stp = sts[ii]
                dst = dstate[...]
                bl = b_[SUB - 1:SUB, :]
                eb = jnp.exp2(b_)
                ekl = jnp.exp2(bl - b_)
                ebl = jnp.exp2(bl)
                dob = do_.astype(BF16)
                dstb = dst.astype(BF16)
                kt = k_ * ekl
                dq = _dot(dob, stp.astype(BF16)) * eb
                dkt = _dot(v_.astype(BF16), dstb)
                dk = dkt * ekl
                dv = _dot(kt.astype(BF16), dstb, NT)
                extra = jnp.sum(kt * dkt, axis=0, keepdims=True) + ebl * jnp.sum(stp * dst, axis=0, keepdims=True)
                halves = lambda x: [x[:HALF], x[HALF:]]
                q_h, b_h, do_h, dq_h, dk_h, dv_h = (halves(x) for x in (q_, b_, do_, dq, dk, dv))
                for own in range(2):
                    dk_rows, dv_rows = _RowSums(rows), _RowSums(rows)
                    for jj in _RowSums.ORDER:
                        j = own * HALF + jj
                        bj, kj, vj = b_[j:j + 1, :], k_[j:j + 1, :], v_[j:j + 1, :]
                        dk_sum = dv_sum = None
                        for h in range(own, 2):
                            e = _decay(b_h[h], bj, rows, jj if h == own else None)
                            pe = q_h[h] * e
                            acol = jnp.sum(pe * kj, axis=-1, keepdims=True)
                            dacol = jnp.sum(do_h[h] * vj, axis=-1, keepdims=True)
                            dq_h[h] = dq_h[h] + dacol * (e * kj)
                            dk_sum = dacol * pe if dk_sum is None else dk_sum + dacol * pe
                            dv_sum = acol * do_h[h] if dv_sum is None else dv_sum + acol * do_h[h]
                        dk_rows.push(jj, dk_sum)
                        dv_rows.push(jj, dv_sum)
                    dk_h[own] = dk_h[own] + dk_rows.result()
                    dv_h[own] = dv_h[own] + dv_rows.result()
                dq, dk, dv = (jnp.concatenate(x, axis=0) for x in (dq_h, dk_h, dv_h))
                dqs[pl.ds(r0, SUB), :] = dq
                dks[pl.ds(r0, SUB), :] = dk
                di_ref[pl.ds(r0, SUB), :] = dv
                dbs[pl.ds(r0, SUB), :] = q_ * dq - k_ * dk + jnp.where(last_row, extra, 0.0)
                dstate[...] = dst * ebl + _dot(dob, (q_ * eb).astype(BF16), TN)
                return carry

            lax.fori_loop(0, nsub, step, 0, unroll=unroll)
            dbs[...] = _group_cumsum(triu_ref[...], dbs[...])

        dlf = dbs[...]
        dfv = dlf / f - dks[...]
        df_ref[...] = dfv * (1.0 - lb) * sf * (1.0 - sf)
        dlb = jnp.sum(dfv * (1.0 - sf), axis=0, keepdims=True)
        dq_ref[...] = dqs[...] * (sq * (1.0 + qp * (1.0 - sq)))
        _acc_rows(stat_ref, t, [dgn, dlb])

    rev = lambda t: nt - 1 - t
    col = lambda k: pl.BlockSpec((tc, hd), lambda h, t: (rev(t), k * HGRN_HEADS + h))
    vec = pl.BlockSpec((None, 1, hd), lambda h, t: (h, 0, 0))
    head = pl.BlockSpec((tc, hd), lambda h, t: (rev(t), h))
    x_specs, x_shapes, x_scratch = _plan_extras(plan)
    return pl.pallas_call(
        _fuse_exchange(body, 14, 5, 9, plan, 2), name="hgrn_bwd", grid=(HGRN_HEADS, nt),
        in_specs=[col(0), col(1), col(2), col(3), vec, vec, vec, head, head,
                  pl.BlockSpec((None, tc // CHUNK, hd, hd), lambda h, t: (h, rev(t), 0, 0))]
                 + [_const((tc, tc))] * 4 + x_specs,
        out_specs=[head, head, head, head, pl.BlockSpec((None, 8, hd), lambda h, t: (h, 0, 0))] + x_specs,
        out_shape=[jax.ShapeDtypeStruct((s, HGRN_WIDTH), F32)] * 4 + [jax.ShapeDtypeStruct((HGRN_HEADS, 8, hd), F32)]
                  + x_shapes,
        scratch_shapes=[pltpu.VMEM((hd, hd), F32)] + [pltpu.VMEM((tc, hd), F32)] * 7
                       + [pltpu.VMEM((nsub, hd, hd), F32)] + x_scratch,
        compiler_params=_params(("arbitrary", "arbitrary")),
    )(z, z, z, z, l0, l1, gn, o, dmix, st, _block_tri(tc, SUB, False), _block_tri(tc, SUB, True),
      _block_tri(tc, CHUNK, False), _block_tri(tc, CHUNK, True), *plan_args)


def _in_bwd(dparts, dx1, x, g, win_g, tm, plan=None, plan_args=()):
    s, d = x.shape
    nsh, _, wc = win_g.shape
    pw = dparts[0].shape[1]

    def body(dq_ref, df_ref, di_ref, dg_ref, dp_ref, dx1_ref, x_ref, g_ref, w_ref, gx_ref, dz_ref, st_ref):
        dz = jnp.concatenate([dq_ref[...], df_ref[...], di_ref[...], dg_ref[...], dp_ref[...]], axis=1).astype(BF16)
        dz_ref[...] = dz
        dh = jnp.zeros((tm, d), F32)
        for j in range(nsh):
            dh = dh + _dot(dz[:, j * wc:(j + 1) * wc], w_ref[j], NT)
        gv = g_ref[...]
        _, n, r = _rms_fwd(x_ref[...], gv)
        dxn, dg = _rms_bwd(dh, n, r, gv)
        gx_ref[...] = dx1_ref[...] + dxn
        _acc_rows(st_ref, pl.program_id(0), [dg])

    tile = lambda w: pl.BlockSpec((tm, w), lambda t: (t, 0))
    x_specs, x_shapes, x_scratch = _plan_extras(plan)
    return pl.pallas_call(
        _fuse_exchange(body, 9, 3, 0, plan, 1), name="in_bwd", grid=(s // tm,),
        in_specs=[tile(pw)] * 5 + [tile(d), tile(d), _const((1, d)), _const(win_g.shape)] + x_specs,
        out_specs=[tile(d), tile(nsh * wc), _const_out((8, d))] + x_specs,
        out_shape=[jax.ShapeDtypeStruct((s, d), F32), jax.ShapeDtypeStruct((s, nsh * wc), BF16),
                   jax.ShapeDtypeStruct((8, d), F32)] + x_shapes,
        scratch_shapes=x_scratch,
        compiler_params=_params(("arbitrary",)),
    )(*dparts, dx1, x, g, win_g, *plan_args)


def _tn_grad(name, a, b, out_rows, out_cols, a_sharded, tr, tc, plan=None, plan_args=()):
    s = a.shape[0]
    nr, nc = out_rows // tr, out_cols // tc

    def body(a_ref, b_ref, o_ref):
        o_ref[...] = _dot(a_ref[...], b_ref[...], TN)

    a_map = (lambda j, i, k: (0, j * nr + i)) if a_sharded else (lambda j, i, k: (0, i))
    b_map = (lambda j, i, k: (0, k)) if a_sharded else (lambda j, i, k: (0, j * nc + k))
    x_specs, x_shapes, x_scratch = _plan_extras(plan)
    res = pl.pallas_call(
        _fuse_exchange(body, 2, 1, 0, plan, 3), name=name, grid=(N_CHIPS, nr, nc),
        in_specs=[pl.BlockSpec((s, tr), a_map), pl.BlockSpec((s, tc), b_map)] + x_specs,
        out_specs=[pl.BlockSpec((None, tr, tc), lambda j, i, k: (j, i, k))] + x_specs,
        out_shape=[jax.ShapeDtypeStruct((N_CHIPS, out_rows, out_cols), F32)] + x_shapes,
        scratch_shapes=x_scratch,
        compiler_params=_params(("arbitrary", "arbitrary", "arbitrary")),
    )(a, b, *plan_args)
    return res if plan else res[0]


FFN_NAMES = ("w_ff1", "w_ff2")
ATTN_NAMES = ("w_xo", "w_xq", "w_out", "w_xk", "w_xv")
EARLY_NAMES = FFN_NAMES + ATTN_NAMES
BIG_NAMES = EARLY_NAMES + ("w_in",)


def _halved(g):
    return g.reshape(N_CHIPS, 2, g.shape[1] // 2, g.shape[2])


def _pair_adds(names, gs, got, idx):
    pairs = [_grad_pair_add("grad_pair_add_" + k, g, r, idx, tr=min(256, g.shape[2])) for k, g, r in zip(names, gs, got)]
    return [p[0] for p in pairs], [p[1] for p in pairs]


def _step(x, mem, target, small, shards, idx):
    d = x.shape[1]
    l0 = small["lb_logits"][0].reshape(HGRN_HEADS, 1, HEAD_DIM)
    l1 = small["lb_logits"][1].reshape(HGRN_HEADS, 1, HEAD_DIM)
    gn = small["hgrn_norm_g"].reshape(HGRN_HEADS, 1, HEAD_DIM)
    wp = small["w_pool"].reshape(len(POOL_WINDOWS), HEAD_DIM, HEAD_DIM)
    psc = small["pool_scale"].reshape(1, -1)
    gmix, gx, gmem, gffn = (small[k].reshape(1, d) for k in ("norm_mix_g", "norm_x_g", "norm_mem_g", "norm_ffn_g"))
    gfin = small["final_norm_g"].reshape(1, d)

    (win_g,) = _run_exchange("gather_w_in", _WeightGather([shards["w_in"]]), [shards["w_in"]])
    z, h, kv_g = _in_proj(x, gmix, win_g, tm=512, plan=_WeightGather([shards["slab_kv"]]),
                          plan_args=[shards["slab_kv"]])
    mid_w = [shards["slab_oq"], shards["w_xo"], shards["w_ff1"]]
    o, oa, st, oq_g, wo_g, w1_g = _hgrn_fwd(z, l0, l1, gn, tc=256, unroll=8,
                                            plan=_WeightGather(mid_w), plan_args=mid_w)
    ob = _pool_fwd(z, wp, psc, tm=512)
    xk, xv = _kv_proj(mem, gmem, kv_g)
    late_w = [shards["w_ff2"]]
    x1, mixed, hq, xq, att, x2, w2_g = _mix_xattn_fwd(x, oa, ob, gx, oq_g, wo_g, xk, xv, tm=512,
                                                      plan=_WeightGather(late_w), plan_args=late_w)
    a, hf, dx3, dx3b, st_loss = _mlp_loss_fwd(x2, gffn, gfin, w1_g, w2_g, target, tm=512)

    da, u, dx2, dx2b, st_ffn = _mlp_bwd(dx3, dx3b, a, x2, gffn, w1_g, w2_g, tm=256)
    g_ff1 = [_halved(_tn_grad("dw_ff1", hf, da, d, d, False, 1024, 1024))]
    dw_ff2, *got = _tn_grad("dw_ff2", u, dx3b, d, d, True, 1024, 1024, plan=_PairExchange(g_ff1), plan_args=g_ff1)
    keep_ff1, send_ff1 = _pair_adds(("w_ff1",), g_ff1, got, idx)
    g_ff2 = [_halved(dw_ff2)]
    dx1, dx1b, dxq, dmix, dxk, dxv, st_x, *got = _xattn_mix_bwd(
        dx2, x1, xq, xk, xv, gx, oq_g, wo_g, tm=512,
        plan=_Plans([_ChipExchange(send_ff1), _PairExchange(g_ff2)]), plan_args=send_ff1 + g_ff2)
    recv_ff1 = got[:1]
    keep_ff2, send_ff2 = _pair_adds(("w_ff2",), g_ff2, got[1:], idx)
    dw = {}
    dw["w_xo"] = _tn_grad("dw_xo", att, dx2b, d, d // N_CHIPS, False, 1024, 256)
    dw["w_xq"] = _tn_grad("dw_xq", hq, dxq, d // N_CHIPS, d, True, 256, 1024)
    dw["w_out"] = _tn_grad("dw_out", mixed, dx1b, d // N_CHIPS, d, True, 256, 1024)
    dw["w_xk"], dw["w_xv"], st_mem = _kv_bwd(mem, gmem, dxk, dxv, kv_g)
    gs_attn = [_halved(dw[k]) for k in ATTN_NAMES]
    dp, d_wp, st_pool, *got_attn = _pool_bwd(z, dmix, wp, psc, tm=512, plan=_PairExchange(gs_attn), plan_args=gs_attn)
    keep_attn, send_attn = _pair_adds(ATTN_NAMES, gs_attn, got_attn, idx)
    sends = send_ff2 + send_attn
    dq, df, di, dg, st_hgrn, *received = _hgrn_bwd(z, o, dmix, st, l0, l1, gn, tc=256, unroll=4,
                                                    plan=_ChipExchange(sends), plan_args=sends)
    grad_x, dz, st_mix = _in_bwd([dq, df, di, dg, dp], dx1, x, gmix, win_g, tm=512)
    keeps = keep_ff1 + keep_ff2 + keep_attn
    received = recv_ff1 + list(received)
    gs_in = [_halved(_tn_grad("dw_in", h, dz, d, win_g.shape[2], False, 1024, win_g.shape[2]))]
    got_in = _run_exchange("grad_pair_exchange_w_in", _PairExchange(gs_in), gs_in)
    keep_in, send_in = _pair_adds(("w_in",), gs_in, got_in, idx)

    partials = dict(zip(EARLY_NAMES, zip(keeps, received)))
    stats = dict(mix=st_mix, x=st_x, mem=st_mem, ffn=st_ffn, loss=st_loss, hgrn=st_hgrn, pool=st_pool)
    return grad_x, stats, d_wp, partials, keep_in, send_in


def _place():
    x, y, c = lax.axis_index("x"), lax.axis_index("y"), lax.axis_index("c")
    return x, y, c, [(x, 1 - y), (1 - x, y), (1 - x, 1 - y)]


def _rcopy(src, dst, ssem, rsem, dev):
    return pltpu.make_async_remote_copy(src_ref=src, dst_ref=dst, send_sem=ssem, recv_sem=rsem,
                                        device_id=dev, device_id_type=MESH)


class _WeightGather:
    def __init__(self, shards):
        self.n = len(shards)
        self.rows = [w.shape[0] for w in shards]
        self.out_shape = [jax.ShapeDtypeStruct((N_CHIPS,) + w.shape, w.dtype) for w in shards]
        self.scratch_shapes = [pltpu.SemaphoreType.DMA((self.n,))] * 2 + [pltpu.SemaphoreType.DMA((self.n, 3))] * 4

    def _copies(self, ins, outs, sems, with_pass_on):
        lsem, lrsem, ssem, rsem, fsem, frsem = sems
        x, y, c, peers = _place()
        chip = 2 * x + y
        sib = (x, y, 1 - c)
        own = [_rcopy(ins[a], outs[a].at[chip], lsem.at[a], lrsem.at[a], sib) for a in range(self.n)]
        sends, arrived, passed, passed_in = [], [], [], []
        for a in range(self.n):
            hr = self.rows[a] // 2
            half = lambda who, hc, a=a, hr=hr: outs[a].at[who, pl.ds(hc * hr, hr), :]
            for r, (px, py) in enumerate(peers):
                pc = 2 * px + py
                sends.append(_rcopy(ins[a].at[pl.ds(c * hr, hr), :], half(chip, c), ssem.at[a, r], rsem.at[a, r],
                                    (px, py, c)))
                if with_pass_on:
                    arrived.append(_rcopy(half(pc, c), half(pc, c), ssem.at[a, r], rsem.at[a, r], (px, py, c)))
                    passed.append(_rcopy(half(pc, c), half(pc, c), fsem.at[a, r], frsem.at[a, r], sib))
                    passed_in.append(_rcopy(half(pc, 1 - c), half(pc, 1 - c), fsem.at[a, r], frsem.at[a, r], sib))
        return own, sends, arrived, passed, passed_in

    def start(self, ins, outs, sems):
        own, sends, _, _, _ = self._copies(ins, outs, sems, False)
        for cp in own + sends:
            cp.start()

    def finish(self, ins, outs, sems):
        own, sends, arrived, passed, passed_in = self._copies(ins, outs, sems, True)
        for got, fwd in zip(arrived, passed):
            got.wait_recv()
            fwd.start()
        for cp in passed_in:
            cp.wait_recv()
        for cp in sends + passed:
            cp.wait_send()
        for cp in own:
            cp.wait()


class _ChipExchange:
    def __init__(self, sends):
        self.n = len(sends)
        self.out_shape = [jax.ShapeDtypeStruct(g.shape, g.dtype) for g in sends]
        self.scratch_shapes = [pltpu.SemaphoreType.DMA((self.n, 3))] * 2

    def _copies(self, ins, outs, sems):
        ssem, rsem = sems
        _, _, c, peers = _place()
        return [_rcopy(ins[a].at[r], outs[a].at[r], ssem.at[a, r], rsem.at[a, r], (px, py, c))
                for a in range(self.n) for r, (px, py) in enumerate(peers)]

    def start(self, ins, outs, sems):
        for cp in self._copies(ins, outs, sems):
            cp.start()

    def finish(self, ins, outs, sems):
        for cp in self._copies(ins, outs, sems):
            cp.wait()


class _Plans:
    def __init__(self, plans):
        self.plans = plans
        self.n = sum(p.n for p in plans)
        self.out_shape = [s for p in plans for s in p.out_shape]
        self.scratch_shapes = [s for p in plans for s in p.scratch_shapes]

    def _each(self, ins, outs, sems):
        a = b = 0
        for p in self.plans:
            ns = len(p.scratch_shapes)
            yield p, ins[a:a + p.n], outs[a:a + p.n], sems[b:b + ns]
            a, b = a + p.n, b + ns

    def start(self, ins, outs, sems):
        for p, i, o, s in self._each(ins, outs, sems):
            p.start(i, o, s)

    def finish(self, ins, outs, sems):
        for p, i, o, s in self._each(ins, outs, sems):
            p.finish(i, o, s)


def _run_exchange(name, plan, arrays):
    n = plan.n

    def body(*refs):
        ins, outs, sems = refs[:n], refs[n:2 * n], refs[2 * n:]
        plan.start(ins, outs, sems)
        plan.finish(ins, outs, sems)

    return pl.pallas_call(
        body, name=name, in_specs=[ANY] * n, out_specs=[ANY] * n,
        out_shape=plan.out_shape, scratch_shapes=plan.scratch_shapes,
    )(*arrays)


class _PairExchange:
    def __init__(self, gs):
        self.n = len(gs)
        self.out_shape = [jax.ShapeDtypeStruct((g.shape[0],) + g.shape[2:], g.dtype) for g in gs]
        self.scratch_shapes = [pltpu.SemaphoreType.DMA((self.n,))] * 2

    def _copies(self, ins, outs, sems):
        ssem, rsem = sems
        x, y, c, _ = _place()
        return [_rcopy(ins[a].at[:, 1 - c], outs[a], ssem.at[a], rsem.at[a], (x, y, 1 - c)) for a in range(self.n)]

    def start(self, ins, outs, sems):
        for cp in self._copies(ins, outs, sems):
            cp.start()

    def finish(self, ins, outs, sems):
        for cp in self._copies(ins, outs, sems):
            cp.wait()


def _grad_pair_add(name, g, got, idx, tr):
    _, _, hr, cc = g.shape

    def body(idx_ref, g0, g1, g2, g3, r0, r1, r2, r3, keep_ref, send_ref):
        keep_ref[...] = g0[...] + r0[...]
        for q, (gq, rq) in enumerate(((g1, r1), (g2, r2), (g3, r3))):
            send_ref[q] = (gq[...] + rq[...]).astype(BF16)

    gspec = lambda q: pl.BlockSpec((None, None, tr, cc), lambda i, idx: (idx[1 + q], idx[0], i, 0))
    rspec = lambda q: pl.BlockSpec((None, tr, cc), lambda i, idx: (idx[1 + q], i, 0))
    return pl.pallas_call(
        body, name=name,
        grid_spec=pltpu.PrefetchScalarGridSpec(
            num_scalar_prefetch=1, grid=(hr // tr,),
            in_specs=[gspec(q) for q in range(4)] + [rspec(q) for q in range(4)],
            out_specs=[pl.BlockSpec((tr, cc), lambda i, idx: (i, 0)), pl.BlockSpec((3, tr, cc), lambda i, idx: (0, i, 0))]),
        out_shape=[jax.ShapeDtypeStruct((hr, cc), F32), jax.ShapeDtypeStruct((3, hr, cc), BF16)],
        compiler_params=_params(("parallel",)),
    )(idx, g, g, g, g, got, got, got, got)


def _grad_chip_add(name, keep, got, tr):
    hr, cc = keep.shape

    def body(k_ref, g_ref, o_ref):
        o_ref[...] = ((k_ref[...] + g_ref[0].astype(F32)) + g_ref[1].astype(F32)) + g_ref[2].astype(F32)

    return pl.pallas_call(
        body, name=name, grid=(hr // tr,),
        in_specs=[pl.BlockSpec((tr, cc), lambda i: (i, 0)), pl.BlockSpec((3, tr, cc), lambda i: (0, i, 0))],
        out_specs=pl.BlockSpec((tr, cc), lambda i: (i, 0)),
        out_shape=jax.ShapeDtypeStruct((hr, cc), F32),
        compiler_params=_params(("parallel",)),
    )(keep, got)


class _HalfExchange:
    def __init__(self, ts):
        self.n = len(ts)
        self.out_shape = [jax.ShapeDtypeStruct(t.shape, t.dtype) for t in ts]
        self.scratch_shapes = [pltpu.SemaphoreType.DMA((self.n,))] * 2

    def _copies(self, ins, outs, sems):
        ssem, rsem = sems
        x, y, c, _ = _place()
        return [_rcopy(ins[a], outs[a], ssem.at[a], rsem.at[a], (x, y, 1 - c)) for a in range(self.n)]

    def start(self, ins, outs, sems):
        for cp in self._copies(ins, outs, sems):
            cp.start()

    def finish(self, ins, outs, sems):
        for cp in self._copies(ins, outs, sems):
            cp.wait()


def _small_allreduce(stats, d_wp, plan, plan_args):
    d = D_MODEL
    half = d // 2
    wps = d_wp.shape
    n = plan.n

    def body(mix_ref, x_ref, mem_ref, ffn_ref, loss_ref, hg_ref, pool_ref, wp_ref, *refs):
        cin, (slab_out, wp_out), cout = refs[:n], refs[n:n + 2], refs[n + 2:2 * n + 2]
        slab_buf, wp_buf, sib_s, sib_w, ssem, rsem = refs[2 * n + 2:2 * n + 8]
        csem = refs[2 * n + 8:]
        plan.start(cin, cout, csem)
        x, y, c, peers = _place()
        chip = 2 * x + y
        sib = (x, y, 1 - c)
        hgn = jnp.concatenate([hg_ref[h, 0:1, :] for h in range(HGRN_HEADS)], axis=1)
        dlb = jnp.concatenate([hg_ref[h, 1:2, :] for h in range(HGRN_HEADS)], axis=1)
        slab_buf[0] = jnp.concatenate([
            mix_ref[0:1, :], x_ref[0:1, :], mem_ref[0:1, :], ffn_ref[0:1, :], loss_ref[0:1, :],
            jnp.concatenate([dlb, hgn], axis=1),
            jnp.concatenate([pool_ref[0:1, :], jnp.zeros((1, half), F32)], axis=1),
            loss_ref[1:2, :]], axis=0)
        wp_buf[0] = wp_ref[...]
        pair = [_rcopy(slab_buf.at[0], sib_s, ssem.at[0], rsem.at[0], sib),
                _rcopy(wp_buf.at[0], sib_w, ssem.at[1], rsem.at[1], sib)]
        for cp in pair:
            cp.start()
        for cp in pair:
            cp.wait()
        slab_buf[0] = slab_buf[0] + sib_s[...]
        wp_buf[0] = wp_buf[0] + sib_w[...]
        cps = []
        for r, (px, py) in enumerate(peers):
            cps.append(_rcopy(slab_buf.at[0], slab_buf.at[r + 1], ssem.at[2 + 2 * r], rsem.at[2 + 2 * r], (px, py, c)))
            cps.append(_rcopy(wp_buf.at[0], wp_buf.at[r + 1], ssem.at[3 + 2 * r], rsem.at[3 + 2 * r], (px, py, c)))
        for cp in cps:
            cp.start()
        for cp in cps:
            cp.wait()
        tot_s, tot_w = slab_buf[chip], wp_buf[chip]
        for j in range(1, N_CHIPS):
            tot_s = tot_s + slab_buf[jnp.bitwise_xor(j, chip)]
            tot_w = tot_w + wp_buf[jnp.bitwise_xor(j, chip)]
        slab_out[...] = tot_s
        wp_out[...] = tot_w
        plan.finish(cin, cout, csem)

    return pl.pallas_call(
        body, name="small_allreduce",
        in_specs=[VMEM] * 8 + [ANY] * n, out_specs=[VMEM] * 2 + [ANY] * n,
        out_shape=[jax.ShapeDtypeStruct((8, d), F32), jax.ShapeDtypeStruct(wps, F32)] + list(plan.out_shape),
        scratch_shapes=[pltpu.VMEM((N_CHIPS, 8, d), F32), pltpu.VMEM((N_CHIPS,) + wps, F32),
                        pltpu.VMEM((8, d), F32), pltpu.VMEM(wps, F32),
                        pltpu.SemaphoreType.DMA((8,)), pltpu.SemaphoreType.DMA((8,))] + list(plan.scratch_shapes),
    )(stats["mix"], stats["x"], stats["mem"], stats["ffn"], stats["loss"], stats["hgrn"], stats["pool"], d_wp,
      *plan_args)


def _adamw_math(w, g, m, v):
    m = ADAM_B1 * m + (1.0 - ADAM_B1) * g
    v = ADAM_B2 * v + (1.0 - ADAM_B2) * (g * g)
    m_hat = m / (1.0 - ADAM_B1 ** ADAM_STEP)
    v_hat = v / (1.0 - ADAM_B2 ** ADAM_STEP)
    delta = -ADAM_LR * (m_hat / (jnp.sqrt(v_hat) + ADAM_EPS) + ADAM_WD * w)
    return delta, m, v


def _adamw(name, mine, theirs, w, m, v, idx, tr):
    rows = w.shape[0]
    cc = mine.shape[1]
    nb = rows // 2 // tr
    heads = w.shape[1] if w.ndim == 3 else 1
    e = cc // heads

    def body(idx_ref, a_ref, b_ref, w_ref, m_ref, v_ref, g_out, d_out, m_out, v_out):
        g = jnp.where(pl.program_id(0) // nb == idx_ref[0], a_ref[...], b_ref[...])
        if w.ndim == 2:
            g_out[...] = g
            d_out[...], m_out[...], v_out[...] = _adamw_math(w_ref[...], g, m_ref[...], v_ref[...])
        else:
            for h in range(heads):
                gh = g[:, h * e:(h + 1) * e]
                g_out[:, h, :] = gh
                d_out[:, h, :], m_out[:, h, :], v_out[:, h, :] = _adamw_math(
                    w_ref[:, h, :], gh, m_ref[:, h, :], v_ref[:, h, :])

    hspec = pl.BlockSpec((tr, cc), lambda i, idx: (i % nb, 0))
    spec = pl.BlockSpec((tr,) + w.shape[1:], lambda i, idx: (i,) + (0,) * (w.ndim - 1))
    return pl.pallas_call(
        body, name=name,
        grid_spec=pltpu.PrefetchScalarGridSpec(
            num_scalar_prefetch=1, grid=(rows // tr,),
            in_specs=[hspec, hspec, spec, spec, spec], out_specs=[spec] * 4),
        out_shape=[jax.ShapeDtypeStruct(w.shape, F32)] * 4,
        compiler_params=_params(("parallel",)),
    )(idx, mine, theirs, w, m, v)


SMALL_NAMES = ("norm_mix_g", "lb_logits", "hgrn_norm_g", "w_pool", "pool_scale", "norm_x_g", "norm_mem_g",
               "norm_ffn_g", "final_norm_g")


def _small_update(slab, d_wp, ws, ms, vs):
    n = len(SMALL_NAMES)
    half = D_MODEL // 2

    def body(slab_ref, wp_ref, *refs):
        w_refs, m_refs, v_refs, outs = refs[:n], refs[n:2 * n], refs[2 * n:3 * n], refs[3 * n:]
        row = lambda k: slab_ref[k:k + 1, :]
        lbl = w_refs[SMALL_NAMES.index("lb_logits")][...]
        s0 = _lower_bound(lbl[0:1, :], lbl[1:2, :])
        dl0 = row(ROW_LB_HGN)[:, :half] * s0 * (1.0 - s0)
        grads = dict(norm_mix_g=row(ROW_GMIX), lb_logits=jnp.concatenate([dl0, -dl0], axis=0),
                     hgrn_norm_g=row(ROW_LB_HGN)[:, half:], w_pool=wp_ref[...], pool_scale=row(ROW_PSCALE)[:, :half],
                     norm_x_g=row(ROW_GX), norm_mem_g=row(ROW_GMEM), norm_ffn_g=row(ROW_GFFN),
                     final_norm_g=row(ROW_GFIN))
        outs[0][...] = row(ROW_LOSS)[:, :128]
        for i, name in enumerate(SMALL_NAMES):
            g = grads[name]
            delta, m2, v2 = _adamw_math(w_refs[i][...], g, m_refs[i][...], v_refs[i][...])
            for o, val in zip(outs[1 + 4 * i:5 + 4 * i], (g, delta, m2, v2)):
                o[...] = val

    args = [ws[k] for k in SMALL_NAMES] + [ms[k] for k in SMALL_NAMES] + [vs[k] for k in SMALL_NAMES]
    out_shape = [jax.ShapeDtypeStruct((1, 128), F32)]
    for k in SMALL_NAMES:
        out_shape += [jax.ShapeDtypeStruct(ws[k].shape, F32)] * 4
    res = pl.pallas_call(
        body, name="small_update",
        in_specs=[VMEM] * (2 + 3 * n), out_specs=[VMEM] * len(out_shape), out_shape=out_shape,
    )(slab, d_wp, *args)
    return res[0], {k: res[1 + 4 * i:5 + 4 * i] for i, k in enumerate(SMALL_NAMES)}


ALL_NAMES = ("norm_mix_g", "w_in", "lb_logits", "hgrn_norm_g", "w_pool", "pool_scale", "w_out", "norm_x_g",
             "norm_mem_g", "w_xq", "w_xk", "w_xv", "w_xo", "norm_ffn_g", "w_ff1", "w_ff2", "final_norm_g")


def _shard_2d(name, a):
    a = a[0]
    if name in ("w_xq", "w_xk", "w_xv"):
        return a.reshape(a.shape[0], -1)
    if name == "w_xo":
        return a.reshape(-1, a.shape[-1])
    return a


def _small_2d(name, a):
    if name == "w_pool":
        return a.reshape(-1, HEAD_DIM)
    if name == "lb_logits":
        return a
    return a.reshape(1, -1)


def kernel(x, mem, norm_mix_g, w_in, lb_logits, hgrn_norm_g, w_pool, pool_scale, w_out, norm_x_g, norm_mem_g, w_xq, w_xk, w_xv, w_xo, norm_ffn_g, w_ff1, w_ff2, final_norm_g, loss_target, m_norm_mix_g, m_w_in, m_lb_logits, m_hgrn_norm_g, m_w_pool, m_pool_scale, m_w_out, m_norm_x_g, m_norm_mem_g, m_w_xq, m_w_xk, m_w_xv, m_w_xo, m_norm_ffn_g, m_w_ff1, m_w_ff2, m_final_norm_g, v_norm_mix_g, v_w_in, v_lb_logits, v_hgrn_norm_g, v_w_pool, v_pool_scale, v_w_out, v_norm_x_g, v_norm_mem_g, v_w_xq, v_w_xk, v_w_xv, v_w_xo, v_norm_ffn_g, v_w_ff1, v_w_ff2, v_final_norm_g):
    w = dict(norm_mix_g=norm_mix_g, w_in=w_in, lb_logits=lb_logits, hgrn_norm_g=hgrn_norm_g, w_pool=w_pool, pool_scale=pool_scale, w_out=w_out, norm_x_g=norm_x_g, norm_mem_g=norm_mem_g, w_xq=w_xq, w_xk=w_xk, w_xv=w_xv, w_xo=w_xo, norm_ffn_g=norm_ffn_g, w_ff1=w_ff1, w_ff2=w_ff2, final_norm_g=final_norm_g)
    m = dict(norm_mix_g=m_norm_mix_g, w_in=m_w_in, lb_logits=m_lb_logits, hgrn_norm_g=m_hgrn_norm_g, w_pool=m_w_pool, pool_scale=m_pool_scale, w_out=m_w_out, norm_x_g=m_norm_x_g, norm_mem_g=m_norm_mem_g, w_xq=m_w_xq, w_xk=m_w_xk, w_xv=m_w_xv, w_xo=m_w_xo, norm_ffn_g=m_norm_ffn_g, w_ff1=m_w_ff1, w_ff2=m_w_ff2, final_norm_g=m_final_norm_g)
    v = dict(norm_mix_g=v_norm_mix_g, w_in=v_w_in, lb_logits=v_lb_logits, hgrn_norm_g=v_hgrn_norm_g, w_pool=v_w_pool, pool_scale=v_pool_scale, w_out=v_w_out, norm_x_g=v_norm_x_g, norm_mem_g=v_norm_mem_g, w_xq=v_w_xq, w_xk=v_w_xk, w_xv=v_w_xv, w_xo=v_w_xo, norm_ffn_g=v_norm_ffn_g, w_ff1=v_w_ff1, w_ff2=v_w_ff2, final_norm_g=v_final_norm_g)

    big_w = {k: _shard_2d(k, w[k]) for k in BIG_NAMES}
    slab_oq = jnp.concatenate([big_w["w_out"], big_w["w_xq"]], axis=0).astype(BF16)
    slab_kv = jnp.concatenate([big_w["w_xk"], big_w["w_xv"]], axis=0).astype(BF16)
    shards = dict(slab_oq=slab_oq, slab_kv=slab_kv,
                  **{k: big_w[k].astype(BF16) for k in ("w_in", "w_xo", "w_ff1", "w_ff2")})

    cx, cy, cc = lax.axis_index("x"), lax.axis_index("y"), lax.axis_index("c")
    chip = 2 * cx + cy
    idx = jnp.stack([cc, chip, chip ^ 1, chip ^ 2, chip ^ 3]).astype(jnp.int32)
    small = {k: w[k] for k in SMALL_NAMES}
    grad_x, stats, d_wp, partials, keep_in, send_in = _step(x[0], mem[0], loss_target[0], small, shards, idx)

    chip_add = lambda k, keep, got: _grad_chip_add("grad_chip_add_" + k, keep, got, tr=min(256, keep.shape[0]))
    halves = {k: chip_add(k, *partials[k]) for k in EARLY_NAMES}
    early = [halves[k] for k in EARLY_NAMES]
    slab_sum, wp_sum, recv_in, *their_early = _small_allreduce(
        stats, d_wp.reshape(-1, HEAD_DIM), _Plans([_ChipExchange(send_in), _HalfExchange(early)]), send_in + early)
    halves["w_in"] = chip_add("w_in", keep_in[0], recv_in)
    theirs = dict(zip(EARLY_NAMES, their_early))
    (theirs["w_in"],) = _run_exchange("grad_half_exchange_w_in", _HalfExchange([halves["w_in"]]), [halves["w_in"]])

    grads, deltas, new_m, new_v = {}, {}, {}, {}
    for k in BIG_NAMES:
        as_held = (lambda a: a[0]) if k in ("w_xq", "w_xk", "w_xv") else functools.partial(_shard_2d, k)
        res = _adamw("adamw_" + k, halves[k], theirs[k], as_held(w[k]), as_held(m[k]), as_held(v[k]), idx,
                     tr=min(256, halves[k].shape[0]))
        for store, val in zip((grads, deltas, new_m, new_v), res):
            store[k] = val.reshape(w[k].shape)

    loss, upd = _small_update(slab_sum, wp_sum, {k: _small_2d(k, w[k]) for k in SMALL_NAMES},
                              {k: _small_2d(k, m[k]) for k in SMALL_NAMES}, {k: _small_2d(k, v[k]) for k in SMALL_NAMES})
    for k in SMALL_NAMES:
        for store, val in zip((grads, deltas, new_m, new_v), upd[k]):
            store[k] = val.reshape(w[k].shape)

    return (loss[0, 0], grad_x[None], *[grads[k] for k in ALL_NAMES], *[deltas[k] for k in ALL_NAMES],
            *[new_m[k] for k in ALL_NAMES], *[new_v[k] for k in ALL_NAMES])
```

```python
import functools

import jax
import jax.numpy as jnp
from jax import lax
from jax.experimental import pallas as pl
from jax.experimental.pallas import tpu as pltpu

F32 = jnp.float32
BF16 = jnp.bfloat16
LOG2E = 1.4426950408889634
NEG_BIG = -1e30
MAX_LOG2_GROWTH = 100.0
MESH = pl.DeviceIdType.MESH
ANY = pl.BlockSpec(memory_space=pl.ANY)
VMEM = pl.BlockSpec(memory_space=pltpu.VMEM)

D_MODEL = 1024
N_CHIPS = 4
HGRN_HEADS = 4
HEAD_DIM = 128
HGRN_WIDTH = HGRN_HEADS * HEAD_DIM
POOL_WINDOWS = (2, 4, 8, 16)
POOL_HALO = 16
SUB = 16
HALF = SUB // 2
CHUNK = 64
HEADS_PER_STEP = 2
XATTN_HEADS = 4
XATTN_HEAD_DIM = 256
EPS = 1e-6
ADAM_LR, ADAM_B1, ADAM_B2, ADAM_EPS, ADAM_WD, ADAM_STEP = 0.001, 0.9, 0.999, 1e-08, 0.01, 10

V7X_VMEM_BYTES = 64 * 1024 * 1024
VMEM_LIMIT = V7X_VMEM_BYTES - 8 * 1024 * 1024

NN = (((1,), (0,)), ((), ()))
NT = (((1,), (1,)), ((), ()))
TN = (((0,), (0,)), ((), ()))

ROW_GMIX, ROW_GX, ROW_GMEM, ROW_GFFN, ROW_GFIN, ROW_LB_HGN, ROW_PSCALE, ROW_LOSS = range(8)


def _dot(a, b, dims=NN):
    return lax.dot_general(a, b, dims, preferred_element_type=F32)


def _sigmoid(x):
    return 1.0 / (1.0 + jnp.exp(-x))


def _rms_fwd(x, g):
    r = lax.rsqrt(jnp.mean(x * x, axis=-1, keepdims=True) + EPS)
    n = x * r
    return n * g, n, r


def _rms_bwd(dh, n, r, g):
    dn = dh * g
    dx = r * (dn - n * jnp.mean(dn * n, axis=-1, keepdims=True))
    return dx, jnp.sum(dh * n, axis=0, keepdims=True)


def _params(sem=None):
    return pltpu.CompilerParams(dimension_semantics=sem, vmem_limit_bytes=VMEM_LIMIT)


def _const(shape):
    nd = len(shape)
    return pl.BlockSpec(shape, lambda *_: (0,) * nd, pipeline_mode=pl.Buffered(1))


def _const_out(shape):
    nd = len(shape)
    return pl.BlockSpec(shape, lambda *_: (0,) * nd)


def _acc_rows(ref, t, rows):
    upd = jnp.concatenate(rows + [jnp.zeros((8 - len(rows), rows[0].shape[1]), F32)], axis=0)

    @pl.when(t == 0)
    def _():
        ref[...] = upd

    @pl.when(t > 0)
    def _():
        ref[...] = ref[...] + upd


def _fuse_exchange(body, n_in, n_out, n_scratch, plan, ndim):
    if plan is None:
        return body
    n = plan.n

    def wrapped(*refs):
        ins, cin = refs[:n_in], refs[n_in:n_in + n]
        outs, cout = refs[n_in + n:n_in + n + n_out], refs[n_in + n + n_out:n_in + 2 * n + n_out]
        rest = refs[n_in + 2 * n + n_out:]
        scr, csem = rest[:n_scratch], rest[n_scratch:]
        first = pl.program_id(0) == 0
        last = pl.program_id(0) == pl.num_programs(0) - 1
        for i in range(1, ndim):
            first = first & (pl.program_id(i) == 0)
            last = last & (pl.program_id(i) == pl.num_programs(i) - 1)

        @pl.when(first)
        def _():
            plan.start(cin, cout, csem)

        body(*ins, *outs, *scr)

        @pl.when(last)
        def _():
            plan.finish(cin, cout, csem)

    return wrapped


def _plan_extras(plan):
    if plan is None:
        return [], [], []
    return [ANY] * plan.n, list(plan.out_shape), list(plan.scratch_shapes)


def _in_proj(x, g, win_g, tm, plan=None, plan_args=()):
    s, d = x.shape
    nsh, _, wc = win_g.shape

    def body(x_ref, g_ref, w_ref, z_ref, h_ref):
        h, _, _ = _rms_fwd(x_ref[...], g_ref[...])
        hb = h.astype(BF16)
        h_ref[...] = hb
        for j in range(nsh):
            z_ref[:, j * wc:(j + 1) * wc] = _dot(hb, w_ref[j])

    x_specs, x_shapes, x_scratch = _plan_extras(plan)
    return pl.pallas_call(
        _fuse_exchange(body, 3, 2, 0, plan, 1), name="in_proj", grid=(s // tm,),
        in_specs=[pl.BlockSpec((tm, d), lambda t: (t, 0)), _const((1, d)), _const((nsh, d, wc))] + x_specs,
        out_specs=[pl.BlockSpec((tm, nsh * wc), lambda t: (t, 0)), pl.BlockSpec((tm, d), lambda t: (t, 0))] + x_specs,
        out_shape=[jax.ShapeDtypeStruct((s, nsh * wc), F32), jax.ShapeDtypeStruct((s, d), BF16)] + x_shapes,
        scratch_shapes=x_scratch,
        compiler_params=_params(("arbitrary",)),
    )(x, g, win_g, *plan_args)


def _lower_bound(l0, l1):
    m = jnp.maximum(l0, l1)
    e0, e1 = jnp.exp(l0 - m), jnp.exp(l1 - m)
    return e0 / (e0 + e1)


def _block_tri(n, group, upper):
    r = lax.broadcasted_iota(jnp.int32, (n, n), 0)
    c = lax.broadcasted_iota(jnp.int32, (n, n), 1)
    keep = (r // group == c // group) & ((c >= r) if upper else (c <= r))
    return keep.astype(BF16)


def _group_cumsum(tri, x):
    hi = x.astype(BF16)
    rest = x - hi.astype(F32)
    mid = rest.astype(BF16)
    lo = (rest - mid.astype(F32)).astype(BF16)
    return (_dot(tri, hi) + _dot(tri, mid)) + _dot(tri, lo)


def _decay(b, bj, rows, first):
    d = b - bj
    if first:
        d = jnp.where(rows >= first, d, NEG_BIG)
    return jnp.exp2(d)


class _RowSums:
    ORDER = (0, 4, 2, 6, 1, 5, 3, 7)

    def __init__(self, rows):
        self.rows = rows
        self.level = {4: {}, 2: {}, 1: {}}

    def _pair(self, p, q, d):
        return jnp.where((self.rows & d) != 0, p + pltpu.roll(p, d, axis=0), q + pltpu.roll(q, HALF - d, axis=0))

    def push(self, j, y, d=4):
        if d == 0:
            self.out = y
            return
        slot = self.level[d]
        key = j % d
        if key not in slot:
            slot[key] = (j, y)
            return
        j0, y0 = slot.pop(key)
        p, q = (y, y0) if j & d else (y0, y)
        self.push(key, self._pair(p, q, d), d // 2)

    def result(self):
        return self.out


def _hgrn_gates(qp, fp, lb):
    sq = _sigmoid(qp)
    sf = _sigmoid(fp)
    f = lb + (1.0 - lb) * sf
    return qp * sq, sq, f, sf


def _hgrn_fwd(z, l0, l1, gn, tc, unroll=1, plan=None, plan_args=()):
    s = z.shape[0]
    nsub = tc // SUB
    hd = HEAD_DIM
    hp = HEADS_PER_STEP

    def body(q_ref, f_ref, v_ref, g_ref, l0_ref, l1_ref, gn_ref, tri_ref, tric_ref, o_ref, oa_ref, st_ref,
             state, qs, ks, bs, os_):
        @pl.when(pl.program_id(1) == 0)
        def _():
            state[...] = jnp.zeros_like(state)

        cols = [slice(hh * hd, (hh + 1) * hd) for hh in range(hp)]
        q, k, lf, bc = [], [], [], []
        for hh, cs in enumerate(cols):
            qh, _, fh, _ = _hgrn_gates(q_ref[:, cs], f_ref[:, cs], _lower_bound(l0_ref[hh], l1_ref[hh]))
            q.append(qh)
            k.append(1.0 - fh)
            lf.append(jnp.log(fh) * LOG2E)
            bc.append(_group_cumsum(tric_ref[...], lf[hh]))
        bounded = functools.reduce(jnp.minimum, [jnp.min(b) for b in bc]) >= -MAX_LOG2_GROWTH

        @pl.when(bounded)
        def _():
            mask = tric_ref[...] > 0
            for hh, cs in enumerate(cols):
                qt = (q[hh] * jnp.exp2(bc[hh])).astype(BF16)
                ki = (k[hh] * jnp.exp2(-bc[hh])).astype(BF16)
                vb = v_ref[:, cs].astype(BF16)
                a = jnp.where(mask, _dot(qt, ki, NT), 0.0).astype(BF16)
                o_in = _dot(a, vb)
                for c in range(tc // CHUNK):
                    rs = slice(c * CHUNK, (c + 1) * CHUNK)
                    st = state[hh]
                    st_ref[hh, c] = st
                    os_[rs, cs] = o_in[rs] + _dot(qt[rs], st.astype(BF16), NT)
                    bl = bc[hh][(c + 1) * CHUNK - 1:(c + 1) * CHUNK, :]
                    kt = (k[hh][rs] * jnp.exp2(bl - bc[hh][rs])).astype(BF16)
                    state[hh] = st * jnp.exp2(bl) + _dot(vb[rs], kt, TN)

        @pl.when(jnp.logical_not(bounded))
        def _():
            rows = lax.broadcasted_iota(jnp.int32, (HALF, 1), 0)
            for hh, cs in enumerate(cols):
                qs[:, cs] = q[hh]
                ks[:, cs] = k[hh]
                bs[:, cs] = _group_cumsum(tri_ref[...], lf[hh])

                def step(i, carry, hh=hh, cs=cs):
                    r0 = pl.multiple_of(i * SUB, SUB)
                    q_ = qs[pl.ds(r0, SUB), cs]
                    k_ = ks[pl.ds(r0, SUB), cs]
                    b_ = bs[pl.ds(r0, SUB), cs]
                    v_ = v_ref[pl.ds(r0, SUB), cs]
                    st = state[hh]

                    @pl.when(i % (CHUNK // SUB) == 0)
                    def _():
                        st_ref[hh, i // (CHUNK // SUB)] = st

                    bl = b_[SUB - 1:SUB, :]
                    o = _dot((q_ * jnp.exp2(b_)).astype(BF16), st.astype(BF16), NT)
                    (q_lo, q_hi), (b_lo, b_hi), (o_lo, o_hi) = ((x[:HALF], x[HALF:]) for x in (q_, b_, o))
                    for j in range(SUB):
                        bj, kj, vj = b_[j:j + 1, :], k_[j:j + 1, :], v_[j:j + 1, :]
                        if j < HALF:
                            e = _decay(b_lo, bj, rows, j)
                            o_lo = o_lo + jnp.sum(q_lo * e * kj, axis=-1, keepdims=True) * vj
                        e = _decay(b_hi, bj, rows, j - HALF if j > HALF else None)
                        o_hi = o_hi + jnp.sum(q_hi * e * kj, axis=-1, keepdims=True) * vj
                    os_[pl.ds(r0, HALF), cs] = o_lo
                    os_[pl.ds(r0 + HALF, HALF), cs] = o_hi
                    kt = (k_ * jnp.exp2(bl - b_)).astype(BF16)
                    state[hh] = st * jnp.exp2(bl) + _dot(v_.astype(BF16), kt, TN)
                    return carry

                lax.fori_loop(0, nsub, step, 0, unroll=unroll)

        for hh, cs in enumerate(cols):
            o = os_[:, cs]
            o_ref[:, cs] = o
            r = lax.rsqrt(jnp.mean(o * o, axis=-1, keepdims=True) + EPS)
            gp = g_ref[:, cs]
            oa_ref[:, cs] = (o * r * gn_ref[hh] * (gp * _sigmoid(gp))).astype(BF16)

    ng = HGRN_HEADS // hp
    col = lambda k: pl.BlockSpec((tc, hp * hd), lambda h, t: (t, k * ng + h))
    vec = pl.BlockSpec((hp, 1, hd), lambda h, t: (h, 0, 0))
    out = pl.BlockSpec((tc, hp * hd), lambda h, t: (t, h))
    x_specs, x_shapes, x_scratch = _plan_extras(plan)
    return pl.pallas_call(
        _fuse_exchange(body, 9, 3, 5, plan, 2), name="hgrn_fwd", grid=(ng, s // tc),
        in_specs=[col(0), col(1), col(2), col(3), vec, vec, vec, _const((tc, tc)), _const((tc, tc))] + x_specs,
        out_specs=[out, out, pl.BlockSpec((hp, tc // CHUNK, hd, hd), lambda h, t: (h, t, 0, 0))] + x_specs,
        out_shape=[jax.ShapeDtypeStruct((s, HGRN_WIDTH), F32), jax.ShapeDtypeStruct((s, HGRN_WIDTH), BF16),
                   jax.ShapeDtypeStruct((HGRN_HEADS, s // CHUNK, hd, hd), F32)] + x_shapes,
        scratch_shapes=[pltpu.VMEM((hp, hd, hd), F32)] + [pltpu.VMEM((tc, hp * hd), F32)] * 4 + x_scratch,
        compiler_params=_params(("arbitrary", "arbitrary")),
    )(z, z, z, z, l0, l1, gn, _block_tri(tc, SUB, False), _block_tri(tc, CHUNK, False), *plan_args)


def _pooled(p, ext, tok0):
    tm = p.shape[0]
    tok = tok0 + lax.broadcasted_iota(jnp.int32, (tm, 1), 0)
    outs = []
    for g, w in enumerate(POOL_WINDOWS):
        acc = ext[:, g * HEAD_DIM:(g + 1) * HEAD_DIM]
        sh = 1
        while sh < w:
            acc = acc + pltpu.roll(acc, sh, axis=0)
            sh *= 2
        cnt = jnp.minimum(tok + 1, w).astype(F32)
        outs.append(acc[POOL_HALO:, :] / cnt - p[:, g * HEAD_DIM:(g + 1) * HEAD_DIM])
    return outs


def _pool_fwd(z, wp, scale, tm):
    s = z.shape[0]
    pw = len(POOL_WINDOWS) * HEAD_DIM
    nb = tm // POOL_HALO

    def body(p_ref, prev_ref, wp_ref, sc_ref, ob_ref):
        t = pl.program_id(0)
        p = p_ref[...]
        prev = jnp.where(t > 0, prev_ref[...], 0.0)
        pooled = _pooled(p, jnp.concatenate([prev, p], axis=0), t * tm)
        ys = [_dot(pooled[g].astype(BF16), wp_ref[g].astype(BF16)) for g in range(len(POOL_WINDOWS))]
        ob_ref[...] = (jnp.concatenate(ys, axis=1) * sc_ref[...]).astype(BF16)

    return pl.pallas_call(
        body, name="pool_fwd", grid=(s // tm,),
        in_specs=[pl.BlockSpec((tm, pw), lambda t: (t, 4)),
                  pl.BlockSpec((POOL_HALO, pw), lambda t: (jnp.maximum(t * nb - 1, 0), 4)),
                  _const(wp.shape), _const((1, pw))],
        out_specs=pl.BlockSpec((tm, pw), lambda t: (t, 0)),
        out_shape=jax.ShapeDtypeStruct((s, pw), BF16),
        compiler_params=_params(("parallel",)),
    )(z, z, wp, scale)


def _kv_proj(mem, g, slab_g):
    m, d = mem.shape
    rows = d // N_CHIPS

    def body(mem_ref, g_ref, wk_ref, wv_ref, xk_ref, xv_ref):
        hm, _, _ = _rms_fwd(mem_ref[...], g_ref[...])
        hb = hm.astype(BF16)
        xk_ref[...] = _dot(hb, wk_ref[...].reshape(d, d)).astype(BF16)
        xv_ref[...] = _dot(hb, wv_ref[...].reshape(d, d)).astype(BF16)

    blk = lambda k: pl.BlockSpec((N_CHIPS, rows, d), lambda i: (0, k, 0))
    return pl.pallas_call(
        body, name="kv_proj", grid=(1,),
        in_specs=[_const((m, d)), _const((1, d)), blk(0), blk(1)],
        out_specs=[_const_out((m, d)), _const_out((m, d))],
        out_shape=[jax.ShapeDtypeStruct((m, d), BF16)] * 2,
        compiler_params=_params(("arbitrary",)),
    )(mem, g, slab_g, slab_g)


def _softmax_rows(sc):
    e = jnp.exp(sc - jnp.max(sc, axis=-1, keepdims=True))
    return e / jnp.sum(e, axis=-1, keepdims=True)


def _mix_xattn_fwd(x, oa, ob, gx, slab_g, wo_g, xk, xv, tm, plan=None, plan_args=()):
    s, d = x.shape
    m = xk.shape[0]
    rows = d // N_CHIPS
    hw = oa.shape[1]
    e = XATTN_HEAD_DIM

    def body(x_ref, oa_ref, ob_ref, gx_ref, wout_ref, wq_ref, wo_ref, xk_ref, xv_ref,
             x1_ref, mixed_ref, hq_ref, xq_ref, att_ref, x2_ref):
        mixed = jnp.concatenate([oa_ref[...], ob_ref[...]], axis=1)
        mixed_ref[...] = mixed
        x1 = x_ref[...] + _dot(mixed, wout_ref[...].reshape(d, d))
        x1_ref[...] = x1
        hq, _, _ = _rms_fwd(x1, gx_ref[...])
        hqb = hq.astype(BF16)
        hq_ref[...] = hqb
        xq = _dot(hqb, wq_ref[...].reshape(d, d)).astype(BF16)
        xq_ref[...] = xq
        atts = []
        for h in range(XATTN_HEADS):
            cs = slice(h * e, (h + 1) * e)
            p = _softmax_rows(_dot(xq[:, cs], xk_ref[:, cs], NT) * (e ** -0.5))
            atts.append(_dot(p.astype(BF16), xv_ref[:, cs]).astype(BF16))
        att = jnp.concatenate(atts, axis=1)
        att_ref[...] = att
        for j in range(N_CHIPS):
            x2_ref[:, j * rows:(j + 1) * rows] = x1[:, j * rows:(j + 1) * rows] + _dot(att, wo_ref[j])

    tile = lambda w: pl.BlockSpec((tm, w), lambda t: (t, 0))
    blk = lambda k: pl.BlockSpec((N_CHIPS, rows, d), lambda t: (0, k, 0), pipeline_mode=pl.Buffered(1))
    x_specs, x_shapes, x_scratch = _plan_extras(plan)
    return pl.pallas_call(
        _fuse_exchange(body, 9, 6, 0, plan, 1), name="mix_xattn_fwd", grid=(s // tm,),
        in_specs=[tile(d), tile(hw), tile(hw), _const((1, d)), blk(0), blk(1), _const(wo_g.shape),
                  _const((m, d)), _const((m, d))] + x_specs,
        out_specs=[tile(d)] * 6 + x_specs,
        out_shape=[jax.ShapeDtypeStruct((s, d), F32)] + [jax.ShapeDtypeStruct((s, d), BF16)] * 4
                  + [jax.ShapeDtypeStruct((s, d), F32)] + x_shapes,
        scratch_shapes=x_scratch,
        compiler_params=_params(("arbitrary",)),
    )(x, oa, ob, gx, slab_g, slab_g, wo_g, xk, xv, *plan_args)


def _mlp_loss_fwd(x2, gffn, gfin, w1_g, w2_g, target, tm):
    s, d = x2.shape
    wr = w1_g.shape[1]

    def body(x2_ref, gffn_ref, gfin_ref, w1_ref, w2_ref, tg_ref, a_ref, hf_ref, dx3_ref, dx3b_ref, st_ref):
        x2v = x2_ref[...]
        hf, _, _ = _rms_fwd(x2v, gffn_ref[...])
        hfb = hf.astype(BF16)
        hf_ref[...] = hfb
        acc = x2v
        for j in range(N_CHIPS):
            a = _dot(hfb, w1_ref[j])
            a_ref[:, j * wr:(j + 1) * wr] = a
            r = jnp.maximum(a, 0.0)
            acc = acc + _dot((r * r).astype(BF16), w2_ref[j])
        gf = gfin_ref[...]
        y, n, r3 = _rms_fwd(acc, gf)
        err = y - tg_ref[...]
        loss = 0.5 * jnp.sum(jnp.sum(err * err, axis=-1, keepdims=True) * (1.0 / d), axis=0, keepdims=True)
        dy = err * (1.0 / d)
        dx3, dgf = _rms_bwd(dy, n, r3, gf)
        dx3_ref[...] = dx3
        dx3b_ref[...] = dx3.astype(BF16)
        _acc_rows(st_ref, pl.program_id(0), [dgf, jnp.broadcast_to(loss, (1, d))])

    tile = lambda w: pl.BlockSpec((tm, w), lambda t: (t, 0))
    blk = lambda k: pl.BlockSpec((N_CHIPS, wr, d), lambda t: (0, k, 0), pipeline_mode=pl.Buffered(1))
    return pl.pallas_call(
        body, name="mlp_loss_fwd", grid=(s // tm,),
        in_specs=[tile(d), _const((1, d)), _const((1, d)), blk(0), blk(0), tile(d)],
        out_specs=[tile(N_CHIPS * wr), tile(d), tile(d), tile(d), _const_out((8, d))],
        out_shape=[jax.ShapeDtypeStruct((s, N_CHIPS * wr), F32), jax.ShapeDtypeStruct((s, d), BF16),
                   jax.ShapeDtypeStruct((s, d), F32), jax.ShapeDtypeStruct((s, d), BF16),
                   jax.ShapeDtypeStruct((8, d), F32)],
        compiler_params=_params(("arbitrary",)),
    )(x2, gffn, gfin, w1_g, w2_g, target)


def _mlp_bwd(dx3, dx3b, a, x2, gffn, w1_g, w2_g, tm):
    s, d = x2.shape
    wr = w1_g.shape[1]

    def body(dx3_ref, dx3b_ref, a_ref, x2_ref, g_ref, w1_ref, w2_ref, da_ref, u_ref, dx2_ref, dx2b_ref, st_ref):
        dyb = dx3b_ref[...]
        dhf = jnp.zeros((tm, d), F32)
        for j in range(N_CHIPS):
            r = jnp.maximum(a_ref[:, j * wr:(j + 1) * wr], 0.0)
            da = (_dot(dyb, w2_ref[j], NT) * (2.0 * r)).astype(BF16)
            da_ref[:, j * wr:(j + 1) * wr] = da
            u_ref[:, j * wr:(j + 1) * wr] = (r * r).astype(BF16)
            dhf = dhf + _dot(da, w1_ref[j], NT)
        g = g_ref[...]
        _, n, r2 = _rms_fwd(x2_ref[...], g)
        dxn, dg = _rms_bwd(dhf, n, r2, g)
        dx2 = dx3_ref[...] + dxn
        dx2_ref[...] = dx2
        dx2b_ref[...] = dx2.astype(BF16)
        _acc_rows(st_ref, pl.program_id(0), [dg])

    tile = lambda w: pl.BlockSpec((tm, w), lambda t: (t, 0))
    blk = lambda k: pl.BlockSpec((N_CHIPS, wr, d), lambda t: (0, k, 0), pipeline_mode=pl.Buffered(1))
    nf = N_CHIPS * wr
    return pl.pallas_call(
        body, name="mlp_bwd", grid=(s // tm,),
        in_specs=[tile(d), tile(d), tile(nf), tile(d), _const((1, d)), blk(0), blk(0)],
        out_specs=[tile(nf), tile(nf), tile(d), tile(d), _const_out((8, d))],
        out_shape=[jax.ShapeDtypeStruct((s, nf), BF16), jax.ShapeDtypeStruct((s, nf), BF16),
                   jax.ShapeDtypeStruct((s, d), F32), jax.ShapeDtypeStruct((s, d), BF16),
                   jax.ShapeDtypeStruct((8, d), F32)],
        compiler_params=_params(("arbitrary",)),
    )(dx3, dx3b, a, x2, gffn, w1_g, w2_g)


def _xattn_mix_bwd(dx2, x1, xq, xk, xv, gx, slab_g, wo_g, tm, plan=None, plan_args=()):
    s, d = x1.shape
    m = xk.shape[0]
    rows = d // N_CHIPS
    e = XATTN_HEAD_DIM

    def body(dx2_ref, x1_ref, xq_ref, xk_ref, xv_ref, gx_ref, wout_ref, wq_ref, wo_ref,
             dx1_ref, dx1b_ref, dxq_ref, dmix_ref, dxk_ref, dxv_ref, st_ref):
        t = pl.program_id(0)
        dx2 = dx2_ref[...]
        dx2b = dx2.astype(BF16)
        datt = jnp.zeros((tm, d), F32)
        for j in range(N_CHIPS):
            datt = datt + _dot(dx2b[:, j * rows:(j + 1) * rows], wo_ref[j], NT)
        dattb = datt.astype(BF16)
        dxqs, dxks, dxvs = [], [], []
        for h in range(XATTN_HEADS):
            cs = slice(h * e, (h + 1) * e)
            xq_h, xk_h, xv_h = xq_ref[:, cs], xk_ref[:, cs], xv_ref[:, cs]
            p = _softmax_rows(_dot(xq_h, xk_h, NT) * (e ** -0.5))
            dp = _dot(dattb[:, cs], xv_h, NT)
            ds = (p * (dp - jnp.sum(dp * p, axis=-1, keepdims=True)) * (e ** -0.5)).astype(BF16)
            dxqs.append(_dot(ds, xk_h).astype(BF16))
            dxks.append(_dot(ds, xq_h, TN))
            dxvs.append(_dot(p.astype(BF16), dattb[:, cs], TN))
        dxq = jnp.concatenate(dxqs, axis=1)
        dxq_ref[...] = dxq
        dxk = jnp.concatenate(dxks, axis=1)
        dxv = jnp.concatenate(dxvs, axis=1)

        @pl.when(t == 0)
        def _():
            dxk_ref[...] = dxk
            dxv_ref[...] = dxv

        @pl.when(t > 0)
        def _():
            dxk_ref[...] = dxk_ref[...] + dxk
            dxv_ref[...] = dxv_ref[...] + dxv

        dhq = jnp.concatenate([_dot(dxq, wq_ref[j], NT) for j in range(N_CHIPS)], axis=1)
        g = gx_ref[...]
        _, n, r1 = _rms_fwd(x1_ref[...], g)
        dxn, dg = _rms_bwd(dhq, n, r1, g)
        dx1 = dx2 + dxn
        dx1_ref[...] = dx1
        dx1b = dx1.astype(BF16)
        dx1b_ref[...] = dx1b
        for j in range(N_CHIPS):
            dmix_ref[:, j * rows:(j + 1) * rows] = _dot(dx1b, wout_ref[j], NT)
        _acc_rows(st_ref, t, [dg])

    tile = lambda: pl.BlockSpec((tm, d), lambda t: (t, 0))
    blk = lambda k: pl.BlockSpec((N_CHIPS, rows, d), lambda t: (0, k, 0), pipeline_mode=pl.Buffered(1))
    x_specs, x_shapes, x_scratch = _plan_extras(plan)
    return pl.pallas_call(
        _fuse_exchange(body, 9, 7, 0, plan, 1), name="xattn_mix_bwd", grid=(s // tm,),
        in_specs=[tile(), tile(), tile(), _const((m, d)), _const((m, d)), _const((1, d)), blk(0), blk(1),
                  _const(wo_g.shape)] + x_specs,
        out_specs=[tile(), tile(), tile(), tile(), _const_out((m, d)), _const_out((m, d)), _const_out((8, d))]
                  + x_specs,
        out_shape=[jax.ShapeDtypeStruct((s, d), F32), jax.ShapeDtypeStruct((s, d), BF16),
                   jax.ShapeDtypeStruct((s, d), BF16), jax.ShapeDtypeStruct((s, d), F32),
                   jax.ShapeDtypeStruct((m, d), F32), jax.ShapeDtypeStruct((m, d), F32),
                   jax.ShapeDtypeStruct((8, d), F32)] + x_shapes,
        scratch_shapes=x_scratch,
        compiler_params=_params(("arbitrary",)),
    )(dx2, x1, xq, xk, xv, gx, slab_g, slab_g, wo_g, *plan_args)


def _kv_bwd(mem, g, dxk, dxv, slab_g):
    m, d = mem.shape
    rows = d // N_CHIPS

    def body(mem_ref, g_ref, dxk_ref, dxv_ref, wk_ref, wv_ref, dwk_ref, dwv_ref, st_ref):
        gv = g_ref[...]
        hm, n, _ = _rms_fwd(mem_ref[...], gv)
        hb = hm.astype(BF16)
        dkb = dxk_ref[...].astype(BF16)
        dvb = dxv_ref[...].astype(BF16)
        dhm = []
        for j in range(N_CHIPS):
            hj = hb[:, j * rows:(j + 1) * rows]
            dwk_ref[j] = _dot(hj, dkb, TN)
            dwv_ref[j] = _dot(hj, dvb, TN)
            dhm.append(_dot(dkb, wk_ref[j], NT) + _dot(dvb, wv_ref[j], NT))
        dg = jnp.sum(jnp.concatenate(dhm, axis=1) * n, axis=0, keepdims=True)
        st_ref[...] = jnp.concatenate([dg, jnp.zeros((7, d), F32)], axis=0)

    blk = lambda k: pl.BlockSpec((N_CHIPS, rows, d), lambda i: (0, k, 0))
    return pl.pallas_call(
        body, name="kv_bwd", grid=(1,),
        in_specs=[_const((m, d)), _const((1, d)), _const((m, d)), _const((m, d)), blk(0), blk(1)],
        out_specs=[_const_out((N_CHIPS, rows, d)), _const_out((N_CHIPS, rows, d)), _const_out((8, d))],
        out_shape=[jax.ShapeDtypeStruct((N_CHIPS, rows, d), F32)] * 2 + [jax.ShapeDtypeStruct((8, d), F32)],
        compiler_params=_params(("arbitrary",)),
    )(mem, g, dxk, dxv, slab_g, slab_g)


def _pool_bwd(z, dmix, wp, scale, tm, plan=None, plan_args=()):
    s = z.shape[0]
    ng = len(POOL_WINDOWS)
    pw = ng * HEAD_DIM
    nb = tm // POOL_HALO
    nt = s // tm
    n_ext = tm + POOL_HALO

    def body(p_ref, prev_ref, dm_ref, dmn_ref, wp_ref, sc_ref, dp_ref, dwp_ref, st_ref):
        t = pl.program_id(0)
        p = p_ref[...]
        prev = jnp.where(t > 0, prev_ref[...], 0.0)
        pooled = _pooled(p, jnp.concatenate([prev, p], axis=0), t * tm)
        dm = dm_ref[...]
        dme = jnp.concatenate([dm, jnp.where(t < nt - 1, dmn_ref[...], 0.0)], axis=0) * sc_ref[...]
        tok = t * tm + lax.broadcasted_iota(jnp.int32, (n_ext, 1), 0)
        dsc, dps, dwps = [], [], []
        for g, w in enumerate(POOL_WINDOWS):
            cs = slice(g * HEAD_DIM, (g + 1) * HEAD_DIM)
            wpb = wp_ref[g].astype(BF16)
            pb = pooled[g].astype(BF16)
            dsc.append(jnp.sum(dm[:, cs] * _dot(pb, wpb), axis=0, keepdims=True))
            dye = dme[:, cs].astype(BF16)
            dwps.append(_dot(pb, dye[:tm], TN))
            dpe = _dot(dye, wpb, NT)
            acc = dpe / jnp.minimum(tok + 1, w).astype(F32)
            sh = 1
            while sh < w:
                acc = acc + pltpu.roll(acc, n_ext - sh, axis=0)
                sh *= 2
            dps.append(acc[:tm] - dpe[:tm])
        dp_ref[...] = jnp.concatenate(dps, axis=1)
        dsc_row = jnp.concatenate(dsc, axis=1)

        @pl.when(t == 0)
        def _():
            for g in range(ng):
                dwp_ref[g] = dwps[g]

        @pl.when(t > 0)
        def _():
            for g in range(ng):
                dwp_ref[g] = dwp_ref[g] + dwps[g]

        _acc_rows(st_ref, t, [dsc_row])

    x_specs, x_shapes, x_scratch = _plan_extras(plan)
    return pl.pallas_call(
        _fuse_exchange(body, 6, 3, 0, plan, 1), name="pool_bwd", grid=(nt,),
        in_specs=[pl.BlockSpec((tm, pw), lambda t: (t, 4)),
                  pl.BlockSpec((POOL_HALO, pw), lambda t: (jnp.maximum(t * nb - 1, 0), 4)),
                  pl.BlockSpec((tm, pw), lambda t: (t, 1)),
                  pl.BlockSpec((POOL_HALO, pw), lambda t: (jnp.minimum((t + 1) * nb, s // POOL_HALO - 1), 1)),
                  _const(wp.shape), _const((1, pw))] + x_specs,
        out_specs=[pl.BlockSpec((tm, pw), lambda t: (t, 0)), _const_out(wp.shape), _const_out((8, pw))] + x_specs,
        out_shape=[jax.ShapeDtypeStruct((s, pw), F32), jax.ShapeDtypeStruct(wp.shape, F32),
                   jax.ShapeDtypeStruct((8, pw), F32)] + x_shapes,
        scratch_shapes=x_scratch,
        compiler_params=_params(("arbitrary",)),
    )(z, z, dmix, dmix, wp, scale, *plan_args)


def _hgrn_bwd(z, o, dmix, st, l0, l1, gn, tc, unroll=1, plan=None, plan_args=()):
    s = z.shape[0]
    nsub = tc // SUB
    nt = s // tc
    hd = HEAD_DIM

    def body(q_ref, f_ref, v_ref, g_ref, l0_ref, l1_ref, gn_ref, o_ref, dm_ref, st_ref,
             tril_ref, triu_ref, trilc_ref, triuc_ref,
             dq_ref, df_ref, di_ref, dg_ref, stat_ref, dstate, qs, ks, bs, dos, dqs, dks, dbs, sts):
        t = pl.program_id(1)

        @pl.when(t == 0)
        def _():
            dstate[...] = jnp.zeros_like(dstate)

        lb = _lower_bound(l0_ref[...], l1_ref[...])
        qp = q_ref[...]
        q, sq, f, sf = _hgrn_gates(qp, f_ref[...], lb)
        k = 1.0 - f
        lf = jnp.log(f) * LOG2E
        bc = _group_cumsum(trilc_ref[...], lf)
        bounded = jnp.min(bc) >= -MAX_LOG2_GROWTH

        o = o_ref[...]
        r = lax.rsqrt(jnp.mean(o * o, axis=-1, keepdims=True) + EPS)
        n = o * r
        gnv = gn_ref[...]
        gp = g_ref[...]
        sg = _sigmoid(gp)
        dm = dm_ref[...]
        dg_ref[...] = dm * (n * gnv) * (sg * (1.0 + gp * (1.0 - sg)))
        don = dm * (gp * sg)
        dgn = jnp.sum(don * n, axis=0, keepdims=True)
        dn = don * gnv
        do_all = r * (dn - n * jnp.mean(dn * n, axis=-1, keepdims=True))

        @pl.when(bounded)
        def _():
            eb = jnp.exp2(bc)
            eib = jnp.exp2(-bc)
            qt = (q * eb).astype(BF16)
            ki = (k * eib).astype(BF16)
            vb = v_ref[...].astype(BF16)
            dob = do_all.astype(BF16)
            mask = trilc_ref[...] > 0
            a = jnp.where(mask, _dot(qt, ki, NT), 0.0).astype(BF16)
            da = jnp.where(mask, _dot(dob, vb, NT), 0.0).astype(BF16)
            dq_in = _dot(da, ki)
            dk_in = _dot(da, qt, TN)
            dv_in = _dot(a, dob, TN)
            last_row = lax.broadcasted_iota(jnp.int32, (CHUNK, 1), 0) == CHUNK - 1
            for c in reversed(range(tc // CHUNK)):
                rs = slice(c * CHUNK, (c + 1) * CHUNK)
                stp = st_ref[c]
                dst = dstate[...]
                dstb = dst.astype(BF16)
                bl = bc[(c + 1) * CHUNK - 1:(c + 1) * CHUNK, :]
                ekl = jnp.exp2(bl - bc[rs])
                ebl = jnp.exp2(bl)
                kt = k[rs] * ekl
                dq_st = _dot(dob[rs], stp.astype(BF16)) * eb[rs]
                dkt = _dot(vb[rs], dstb)
                extra = jnp.sum(kt * dkt, axis=0, keepdims=True) + ebl * jnp.sum(stp * dst, axis=0, keepdims=True)
                dqs[rs, :] = dq_st + dq_in[rs] * eb[rs]
                dks[rs, :] = dkt * ekl + dk_in[rs] * eib[rs]
                di_ref[rs, :] = _dot(kt.astype(BF16), dstb, NT) + dv_in[rs]
                dbs[rs, :] = (q[rs] * dq_st - kt * dkt + jnp.where(last_row, extra, 0.0)
                              + (qt[rs].astype(F32) * dq_in[rs] - ki[rs].astype(F32) * dk_in[rs]))
                dstate[...] = dst * ebl + _dot(dob[rs], qt[rs], TN)
            dbs[...] = _group_cumsum(triuc_ref[...], dbs[...])

        @pl.when(jnp.logical_not(bounded))
        def _():
            qs[...] = q
            ks[...] = k
            bs[...] = _group_cumsum(tril_ref[...], lf)
            dos[...] = do_all
            per = CHUNK // SUB

            def restore(i, carry):
                @pl.when(i % per == 0)
                def _():
                    sts[i] = st_ref[i // per]

                @pl.when(i % per != 0)
                def _():
                    rp = pl.multiple_of((i - 1) * SUB, SUB)
                    b_ = bs[pl.ds(rp, SUB), :]
                    bl = b_[SUB - 1:SUB, :]
                    kt = (ks[pl.ds(rp, SUB), :] * jnp.exp2(bl - b_)).astype(BF16)
                    sts[i] = sts[i - 1] * jnp.exp2(bl) + _dot(v_ref[pl.ds(rp, SUB), :].astype(BF16), kt, TN)

                return carry

            lax.fori_loop(0, nsub, restore, 0)
            rows = lax.broadcasted_iota(jnp.int32, (HALF, 1), 0)
            last_row = lax.broadcasted_iota(jnp.int32, (SUB, 1), 0) == SUB - 1

            def step(i, carry):
                ii = nsub - 1 - i
                r0 = pl.multiple_of(ii * SUB, SUB)
                q_ = qs[pl.ds(r0, SUB), :]
                k_ = ks[pl.ds(r0, SUB), :]
                b_ = bs[pl.ds(r0, SUB), :]
                v_ = v_ref[pl.ds(r0, SUB), :]
                do_ = dos[pl.ds(r0, SUB), :]
                stp = sts[ii]
                dst = dstate[...]
                bl = b_[SUB - 1:SUB, :]
                eb = jnp.exp2(b_)
                ekl = jnp.exp2(bl - b_)
                ebl = jnp.exp2(bl)
                dob = do_.astype(BF16)
                dstb = dst.astype(BF16)
                kt = k_ * ekl
                dq = _dot(dob, stp.astype(BF16)) * eb
                dkt = _dot(v_.astype(BF16), dstb)
                dk = dkt * ekl
                dv = _dot(kt.astype(BF16), dstb, NT)
                extra = jnp.sum(kt * dkt, axis=0, keepdims=True) + ebl * jnp.sum(stp * dst, axis=0, keepdims=True)
                halves = lambda x: [x[:HALF], x[HALF:]]
                q_h, b_h, do_h, dq_h, dk_h, dv_h = (halves(x) for x in (q_, b_, do_, dq, dk, dv))
                for own in range(2):
                    dk_rows, dv_rows = _RowSums(rows), _RowSums(rows)
                    for jj in _RowSums.ORDER:
                        j = own * HALF + jj
                        bj, kj, vj = b_[j:j + 1, :], k_[j:j + 1, :], v_[j:j + 1, :]
                        dk_sum = dv_sum = None
                        for h in range(own, 2):
                            e = _decay(b_h[h], bj, rows, jj if h == own else None)
                            pe = q_h[h] * e
                            acol = jnp.sum(pe * kj, axis=-1, keepdims=True)
                            dacol = jnp.sum(do_h[h] * vj, axis=-1, keepdims=True)
                            dq_h[h] = dq_h[h] + dacol * (e * kj)
                            dk_sum = dacol * pe if dk_sum is None else dk_sum + dacol * pe
                            dv_sum = acol * do_h[h] if dv_sum is None else dv_sum + acol * do_h[h]
                        dk_rows.push(jj, dk_sum)
                        dv_rows.push(jj, dv_sum)
                    dk_h[own] = dk_h[own] + dk_rows.result()
                    dv_h[own] = dv_h[own] + dv_rows.result()
                dq, dk, dv = (jnp.concatenate(x, axis=0) for x in (dq_h, dk_h, dv_h))
                dqs[pl.ds(r0, SUB), :] = dq
                dks[pl.ds(r0, SUB), :] = dk
                di_ref[pl.ds(r0, SUB), :] = dv
                dbs[pl.ds(r0, SUB), :] = q_ * dq - k_ * dk + jnp.where(last_row, extra, 0.0)
                dstate[...] = dst * ebl + _dot(dob, (q_ * eb).astype(BF16), TN)
                return carry

            lax.fori_loop(0, nsub, step, 0, unroll=unroll)
            dbs[...] = _group_cumsum(triu_ref[...], dbs[...])

        dlf = dbs[...]
        dfv = dlf / f - dks[...]
        df_ref[...] = dfv * (1.0 - lb) * sf * (1.0 - sf)
        dlb = jnp.sum(dfv * (1.0 - sf), axis=0, keepdims=True)
        dq_ref[...] = dqs[...] * (sq * (1.0 + qp * (1.0 - sq)))
        _acc_rows(stat_ref, t, [dgn, dlb])

    rev = lambda t: nt - 1 - t
    col = lambda k: pl.BlockSpec((tc, hd), lambda h, t: (rev(t), k * HGRN_HEADS + h))
    vec = pl.BlockSpec((None, 1, hd), lambda h, t: (h, 0, 0))
    head = pl.BlockSpec((tc, hd), lambda h, t: (rev(t), h))
    x_specs, x_shapes, x_scratch = _plan_extras(plan)
    return pl.pallas_call(
        _fuse_exchange(body, 14, 5, 9, plan, 2), name="hgrn_bwd", grid=(HGRN_HEADS, nt),
        in_specs=[col(0), col(1), col(2), col(3), vec, vec, vec, head, head,
                  pl.BlockSpec((None, tc // CHUNK, hd, hd), lambda h, t: (h, rev(t), 0, 0))]
                 + [_const((tc, tc))] * 4 + x_specs,
        out_specs=[head, head, head, head, pl.BlockSpec((None, 8, hd), lambda h, t: (h, 0, 0))] + x_specs,
        out_shape=[jax.ShapeDtypeStruct((s, HGRN_WIDTH), F32)] * 4 + [jax.ShapeDtypeStruct((HGRN_HEADS, 8, hd), F32)]
                  + x_shapes,
        scratch_shapes=[pltpu.VMEM((hd, hd), F32)] + [pltpu.VMEM((tc, hd), F32)] * 7
                       + [pltpu.VMEM((nsub, hd, hd), F32)] + x_scratch,
        compiler_params=_params(("arbitrary", "arbitrary")),
    )(z, z, z, z, l0, l1, gn, o, dmix, st, _block_tri(tc, SUB, False), _block_tri(tc, SUB, True),
      _block_tri(tc, CHUNK, False), _block_tri(tc, CHUNK, True), *plan_args)


def _in_bwd(dparts, dx1, x, g, win_g, tm, plan=None, plan_args=()):
    s, d = x.shape
    nsh, _, wc = win_g.shape
    pw = dparts[0].shape[1]

    def body(dq_ref, df_ref, di_ref, dg_ref, dp_ref, dx1_ref, x_ref, g_ref, w_ref, gx_ref, dz_ref, st_ref):
        dz = jnp.concatenate([dq_ref[...], df_ref[...], di_ref[...], dg_ref[...], dp_ref[...]], axis=1).astype(BF16)
        dz_ref[...] = dz
        dh = jnp.zeros((tm, d), F32)
        for j in range(nsh):
            dh = dh + _dot(dz[:, j * wc:(j + 1) * wc], w_ref[j], NT)
        gv = g_ref[...]
        _, n, r = _rms_fwd(x_ref[...], gv)
        dxn, dg = _rms_bwd(dh, n, r, gv)
        gx_ref[...] = dx1_ref[...] + dxn
        _acc_rows(st_ref, pl.program_id(0), [dg])

    tile = lambda w: pl.BlockSpec((tm, w), lambda t: (t, 0))
    x_specs, x_shapes, x_scratch = _plan_extras(plan)
    return pl.pallas_call(
        _fuse_exchange(body, 9, 3, 0, plan, 1), name="in_bwd", grid=(s // tm,),
        in_specs=[tile(pw)] * 5 + [tile(d), tile(d), _const((1, d)), _const(win_g.shape)] + x_specs,
        out_specs=[tile(d), tile(nsh * wc), _const_out((8, d))] + x_specs,
        out_shape=[jax.ShapeDtypeStruct((s, d), F32), jax.ShapeDtypeStruct((s, nsh * wc), BF16),
                   jax.ShapeDtypeStruct((8, d), F32)] + x_shapes,
        scratch_shapes=x_scratch,
        compiler_params=_params(("arbitrary",)),
    )(*dparts, dx1, x, g, win_g, *plan_args)


def _tn_grad(name, a, b, out_rows, out_cols, a_sharded, tr, tc, plan=None, plan_args=()):
    s = a.shape[0]
    nr, nc = out_rows // tr, out_cols // tc

    def body(a_ref, b_ref, o_ref):
        o_ref[...] = _dot(a_ref[...], b_ref[...], TN)

    a_map = (lambda j, i, k: (0, j * nr + i)) if a_sharded else (lambda j, i, k: (0, i))
    b_map = (lambda j, i, k: (0, k)) if a_sharded else (lambda j, i, k: (0, j * nc + k))
    x_specs, x_shapes, x_scratch = _plan_extras(plan)
    res = pl.pallas_call(
        _fuse_exchange(body, 2, 1, 0, plan, 3), name=name, grid=(N_CHIPS, nr, nc),
        in_specs=[pl.BlockSpec((s, tr), a_map), pl.BlockSpec((s, tc), b_map)] + x_specs,
        out_specs=[pl.BlockSpec((None, tr, tc), lambda j, i, k: (j, i, k))] + x_specs,
        out_shape=[jax.ShapeDtypeStruct((N_CHIPS, out_rows, out_cols), F32)] + x_shapes,
        scratch_shapes=x_scratch,
        compiler_params=_params(("arbitrary", "arbitrary", "arbitrary")),
    )(a, b, *plan_args)
    return res if plan else res[0]


FFN_NAMES = ("w_ff1", "w_ff2")
ATTN_NAMES = ("w_xo", "w_xq", "w_out", "w_xk", "w_xv")
EARLY_NAMES = FFN_NAMES + ATTN_NAMES
BIG_NAMES = EARLY_NAMES + ("w_in",)


def _halved(g):
    return g.reshape(N_CHIPS, 2, g.shape[1] // 2, g.shape[2])


def _pair_adds(names, gs, got, idx):
    pairs = [_grad_pair_add("grad_pair_add_" + k, g, r, idx, tr=min(256, g.shape[2])) for k, g, r in zip(names, gs, got)]
    return [p[0] for p in pairs], [p[1] for p in pairs]


def _step(x, mem, target, small, shards, idx):
    d = x.shape[1]
    l0 = small["lb_logits"][0].reshape(HGRN_HEADS, 1, HEAD_DIM)
    l1 = small["lb_logits"][1].reshape(HGRN_HEADS, 1, HEAD_DIM)
    gn = small["hgrn_norm_g"].reshape(HGRN_HEADS, 1, HEAD_DIM)
    wp = small["w_pool"].reshape(len(POOL_WINDOWS), HEAD_DIM, HEAD_DIM)
    psc = small["pool_scale"].reshape(1, -1)
    gmix, gx, gmem, gffn = (small[k].reshape(1, d) for k in ("norm_mix_g", "norm_x_g", "norm_mem_g", "norm_ffn_g"))
    gfin = small["final_norm_g"].reshape(1, d)

    (win_g,) = _run_exchange("gather_w_in", _WeightGather([shards["w_in"]]), [shards["w_in"]])
    z, h, kv_g = _in_proj(x, gmix, win_g, tm=512, plan=_WeightGather([shards["slab_kv"]]),
                          plan_args=[shards["slab_kv"]])
    mid_w = [shards["slab_oq"], shards["w_xo"], shards["w_ff1"]]
    o, oa, st, oq_g, wo_g, w1_g = _hgrn_fwd(z, l0, l1, gn, tc=256, unroll=8,
                                            plan=_WeightGather(mid_w), plan_args=mid_w)
    ob = _pool_fwd(z, wp, psc, tm=512)
    xk, xv = _kv_proj(mem, gmem, kv_g)
    late_w = [shards["w_ff2"]]
    x1, mixed, hq, xq, att, x2, w2_g = _mix_xattn_fwd(x, oa, ob, gx, oq_g, wo_g, xk, xv, tm=512,
                                                      plan=_WeightGather(late_w), plan_args=late_w)
    a, hf, dx3, dx3b, st_loss = _mlp_loss_fwd(x2, gffn, gfin, w1_g, w2_g, target, tm=512)

    da, u, dx2, dx2b, st_ffn = _mlp_bwd(dx3, dx3b, a, x2, gffn, w1_g, w2_g, tm=256)
    g_ff1 = [_halved(_tn_grad("dw_ff1", hf, da, d, d, False, 1024, 1024))]
    dw_ff2, *got = _tn_grad("dw_ff2", u, dx3b, d, d, True, 1024, 1024, plan=_PairExchange(g_ff1), plan_args=g_ff1)
    keep_ff1, send_ff1 = _pair_adds(("w_ff1",), g_ff1, got, idx)
    g_ff2 = [_halved(dw_ff2)]
    dx1, dx1b, dxq, dmix, dxk, dxv, st_x, *got = _xattn_mix_bwd(
        dx2, x1, xq, xk, xv, gx, oq_g, wo_g, tm=512,
        plan=_Plans([_ChipExchange(send_ff1), _PairExchange(g_ff2)]), plan_args=send_ff1 + g_ff2)
    recv_ff1 = got[:1]
    keep_ff2, send_ff2 = _pair_adds(("w_ff2",), g_ff2, got[1:], idx)
    dw = {}
    dw["w_xo"] = _tn_grad("dw_xo", att, dx2b, d, d // N_CHIPS, False, 1024, 256)
    dw["w_xq"] = _tn_grad("dw_xq", hq, dxq, d // N_CHIPS, d, True, 256, 1024)
    dw["w_out"] = _tn_grad("dw_out", mixed, dx1b, d // N_CHIPS, d, True, 256, 1024)
    dw["w_xk"], dw["w_xv"], st_mem = _kv_bwd(mem, gmem, dxk, dxv, kv_g)
    gs_attn = [_halved(dw[k]) for k in ATTN_NAMES]
    dp, d_wp, st_pool, *got_attn = _pool_bwd(z, dmix, wp, psc, tm=512, plan=_PairExchange(gs_attn), plan_args=gs_attn)
    keep_attn, send_attn = _pair_adds(ATTN_NAMES, gs_attn, got_attn, idx)
    sends = send_ff2 + send_attn
    dq, df, di, dg, st_hgrn, *received = _hgrn_bwd(z, o, dmix, st, l0, l1, gn, tc=256, unroll=4,
                                                    plan=_ChipExchange(sends), plan_args=sends)
    grad_x, dz, st_mix = _in_bwd([dq, df, di, dg, dp], dx1, x, gmix, win_g, tm=512)
    keeps = keep_ff1 + keep_ff2 + keep_attn
    received = recv_ff1 + list(received)
    gs_in = [_halved(_tn_grad("dw_in", h, dz, d, win_g.shape[2], False, 1024, win_g.shape[2]))]
    got_in = _run_exchange("grad_pair_exchange_w_in", _PairExchange(gs_in), gs_in)
    keep_in, send_in = _pair_adds(("w_in",), gs_in, got_in, idx)

    partials = dict(zip(EARLY_NAMES, zip(keeps, received)))
    stats = dict(mix=st_mix, x=st_x, mem=st_mem, ffn=st_ffn, loss=st_loss, hgrn=st_hgrn, pool=st_pool)
    return grad_x, stats, d_wp, partials, keep_in, send_in


def _place():
    x, y, c = lax.axis_index("x"), lax.axis_index("y"), lax.axis_index("c")
    return x, y, c, [(x, 1 - y), (1 - x, y), (1 - x, 1 - y)]


def _rcopy(src, dst, ssem, rsem, dev):
    return pltpu.make_async_remote_copy(src_ref=src, dst_ref=dst, send_sem=ssem, recv_sem=rsem,
                                        device_id=dev, device_id_type=MESH)


class _WeightGather:
    def __init__(self, shards):
        self.n = len(shards)
        self.rows = [w.shape[0] for w in shards]
        self.out_shape = [jax.ShapeDtypeStruct((N_CHIPS,) + w.shape, w.dtype) for w in shards]
        self.scratch_shapes = [pltpu.SemaphoreType.DMA((self.n,))] * 2 + [pltpu.SemaphoreType.DMA((self.n, 3))] * 4

    def _copies(self, ins, outs, sems, with_pass_on):
        lsem, lrsem, ssem, rsem, fsem, frsem = sems
        x, y, c, peers = _place()
        chip = 2 * x + y
        sib = (x, y, 1 - c)
        own = [_rcopy(ins[a], outs[a].at[chip], lsem.at[a], lrsem.at[a], sib) for a in range(self.n)]
        sends, arrived, passed, passed_in = [], [], [], []
        for a in range(self.n):
            hr = self.rows[a] // 2
            half = lambda who, hc, a=a, hr=hr: outs[a].at[who, pl.ds(hc * hr, hr), :]
            for r, (px, py) in enumerate(peers):
                pc = 2 * px + py
                sends.append(_rcopy(ins[a].at[pl.ds(c * hr, hr), :], half(chip, c), ssem.at[a, r], rsem.at[a, r],
                                    (px, py, c)))
                if with_pass_on:
                    arrived.append(_rcopy(half(pc, c), half(pc, c), ssem.at[a, r], rsem.at[a, r], (px, py, c)))
                    passed.append(_rcopy(half(pc, c), half(pc, c), fsem.at[a, r], frsem.at[a, r], sib))
                    passed_in.append(_rcopy(half(pc, 1 - c), half(pc, 1 - c), fsem.at[a, r], frsem.at[a, r], sib))
        return own, sends, arrived, passed, passed_in

    def start(self, ins, outs, sems):
        own, sends, _, _, _ = self._copies(ins, outs, sems, False)
        for cp in own + sends:
            cp.start()

    def finish(self, ins, outs, sems):
        own, sends, arrived, passed, passed_in = self._copies(ins, outs, sems, True)
        for got, fwd in zip(arrived, passed):
            got.wait_recv()
            fwd.start()
        for cp in passed_in:
            cp.wait_recv()
        for cp in sends + passed:
            cp.wait_send()
        for cp in own:
            cp.wait()


class _ChipExchange:
    def __init__(self, sends):
        self.n = len(sends)
        self.out_shape = [jax.ShapeDtypeStruct(g.shape, g.dtype) for g in sends]
        self.scratch_shapes = [pltpu.SemaphoreType.DMA((self.n, 3))] * 2

    def _copies(self, ins, outs, sems):
        ssem, rsem = sems
        _, _, c, peers = _place()
        return [_rcopy(ins[a].at[r], outs[a].at[r], ssem.at[a, r], rsem.at[a, r], (px, py, c))
                for a in range(self.n) for r, (px, py) in enumerate(peers)]

    def start(self, ins, outs, sems):
        for cp in self._copies(ins, outs, sems):
            cp.start()

    def finish(self, ins, outs, sems):
        for cp in self._copies(ins, outs, sems):
            cp.wait()


class _Plans:
    def __init__(self, plans):
        self.plans = plans
        self.n = sum(p.n for p in plans)
        self.out_shape = [s for p in plans for s in p.out_shape]
        self.scratch_shapes = [s for p in plans for s in p.scratch_shapes]

    def _each(self, ins, outs, sems):
        a = b = 0
        for p in self.plans:
            ns = len(p.scratch_shapes)
            yield p, ins[a:a + p.n], outs[a:a + p.n], sems[b:b + ns]
            a, b = a + p.n, b + ns

    def start(self, ins, outs, sems):
        for p, i, o, s in self._each(ins, outs, sems):
            p.start(i, o, s)

    def finish(self, ins, outs, sems):
        for p, i, o, s in self._each(ins, outs, sems):
            p.finish(i, o, s)


def _run_exchange(name, plan, arrays):
    n = plan.n

    def body(*refs):
        ins, outs, sems = refs[:n], refs[n:2 * n], refs[2 * n:]
        plan.start(ins, outs, sems)
        plan.finish(ins, outs, sems)

    return pl.pallas_call(
        body, name=name, in_specs=[ANY] * n, out_specs=[ANY] * n,
        out_shape=plan.out_shape, scratch_shapes=plan.scratch_shapes,
    )(*arrays)


class _PairExchange:
    def __init__(self, gs):
        self.n = len(gs)
        self.out_shape = [jax.ShapeDtypeStruct((g.shape[0],) + g.shape[2:], g.dtype) for g in gs]
        self.scratch_shapes = [pltpu.SemaphoreType.DMA((self.n,))] * 2

    def _copies(self, ins, outs, sems):
        ssem, rsem = sems
        x, y, c, _ = _place()
        return [_rcopy(ins[a].at[:, 1 - c], outs[a], ssem.at[a], rsem.at[a], (x, y, 1 - c)) for a in range(self.n)]

    def start(self, ins, outs, sems):
        for cp in self._copies(ins, outs, sems):
            cp.start()

    def finish(self, ins, outs, sems):
        for cp in self._copies(ins, outs, sems):
            cp.wait()


def _grad_pair_add(name, g, got, idx, tr):
    _, _, hr, cc = g.shape

    def body(idx_ref, g0, g1, g2, g3, r0, r1, r2, r3, keep_ref, send_ref):
        keep_ref[...] = g0[...] + r0[...]
        for q, (gq, rq) in enumerate(((g1, r1), (g2, r2), (g3, r3))):
            send_ref[q] = (gq[...] + rq[...]).astype(BF16)

    gspec = lambda q: pl.BlockSpec((None, None, tr, cc), lambda i, idx: (idx[1 + q], idx[0], i, 0))
    rspec = lambda q: pl.BlockSpec((None, tr, cc), lambda i, idx: (idx[1 + q], i, 0))
    return pl.pallas_call(
        body, name=name,
        grid_spec=pltpu.PrefetchScalarGridSpec(
            num_scalar_prefetch=1, grid=(hr // tr,),
            in_specs=[gspec(q) for q in range(4)] + [rspec(q) for q in range(4)],
            out_specs=[pl.BlockSpec((tr, cc), lambda i, idx: (i, 0)), pl.BlockSpec((3, tr, cc), lambda i, idx: (0, i, 0))]),
        out_shape=[jax.ShapeDtypeStruct((hr, cc), F32), jax.ShapeDtypeStruct((3, hr, cc), BF16)],
        compiler_params=_params(("parallel",)),
    )(idx, g, g, g, g, got, got, got, got)


def _grad_chip_add(name, keep, got, tr):
    hr, cc = keep.shape

    def body(k_ref, g_ref, o_ref):
        o_ref[...] = ((k_ref[...] + g_ref[0].astype(F32)) + g_ref[1].astype(F32)) + g_ref[2].astype(F32)

    return pl.pallas_call(
        body, name=name, grid=(hr // tr,),
        in_specs=[pl.BlockSpec((tr, cc), lambda i: (i, 0)), pl.BlockSpec((3, tr, cc), lambda i: (0, i, 0))],
        out_specs=pl.BlockSpec((tr, cc), lambda i: (i, 0)),
        out_shape=jax.ShapeDtypeStruct((hr, cc), F32),
        compiler_params=_params(("parallel",)),
    )(keep, got)


class _HalfExchange:
    def __init__(self, ts):
        self.n = len(ts)
        self.out_shape = [jax.ShapeDtypeStruct(t.shape, t.dtype) for t in ts]
        self.scratch_shapes = [pltpu.SemaphoreType.DMA((self.n,))] * 2

    def _copies(self, ins, outs, sems):
        ssem, rsem = sems
        x, y, c, _ = _place()
        return [_rcopy(ins[a], outs[a], ssem.at[a], rsem.at[a], (x, y, 1 - c)) for a in range(self.n)]

    def start(self, ins, outs, sems):
        for cp in self._copies(ins, outs, sems):
            cp.start()

    def finish(self, ins, outs, sems):
        for cp in self._copies(ins, outs, sems):
            cp.wait()


def _small_allreduce(stats, d_wp, plan, plan_args):
    d = D_MODEL
    half = d // 2
    wps = d_wp.shape
    n = plan.n

    def body(mix_ref, x_ref, mem_ref, ffn_ref, loss_ref, hg_ref, pool_ref, wp_ref, *refs):
        cin, (slab_out, wp_out), cout = refs[:n], refs[n:n + 2], refs[n + 2:2 * n + 2]
        slab_buf, wp_buf, sib_s, sib_w, ssem, rsem = refs[2 * n + 2:2 * n + 8]
        csem = refs[2 * n + 8:]
        plan.start(cin, cout, csem)
        x, y, c, peers = _place()
        chip = 2 * x + y
        sib = (x, y, 1 - c)
        hgn = jnp.concatenate([hg_ref[h, 0:1, :] for h in range(HGRN_HEADS)], axis=1)
        dlb = jnp.concatenate([hg_ref[h, 1:2, :] for h in range(HGRN_HEADS)], axis=1)
        slab_buf[0] = jnp.concatenate([
            mix_ref[0:1, :], x_ref[0:1, :], mem_ref[0:1, :], ffn_ref[0:1, :], loss_ref[0:1, :],
            jnp.concatenate([dlb, hgn], axis=1),
            jnp.concatenate([pool_ref[0:1, :], jnp.zeros((1, half), F32)], axis=1),
            loss_ref[1:2, :]], axis=0)
        wp_buf[0] = wp_ref[...]
        pair = [_rcopy(slab_buf.at[0], sib_s, ssem.at[0], rsem.at[0], sib),
                _rcopy(wp_buf.at[0], sib_w, ssem.at[1], rsem.at[1], sib)]
        for cp in pair:
            cp.start()
        for cp in pair:
            cp.wait()
        slab_buf[0] = slab_buf[0] + sib_s[...]
        wp_buf[0] = wp_buf[0] + sib_w[...]
        cps = []
        for r, (px, py) in enumerate(peers):
            cps.append(_rcopy(slab_buf.at[0], slab_buf.at[r + 1], ssem.at[2 + 2 * r], rsem.at[2 + 2 * r], (px, py, c)))
            cps.append(_rcopy(wp_buf.at[0], wp_buf.at[r + 1], ssem.at[3 + 2 * r], rsem.at[3 + 2 * r], (px, py, c)))
        for cp in cps:
            cp.start()
        for cp in cps:
            cp.wait()
        tot_s, tot_w = slab_buf[chip], wp_buf[chip]
        for j in range(1, N_CHIPS):
            tot_s = tot_s + slab_buf[jnp.bitwise_xor(j, chip)]
            tot_w = tot_w + wp_buf[jnp.bitwise_xor(j, chip)]
        slab_out[...] = tot_s
        wp_out[...] = tot_w
        plan.finish(cin, cout, csem)

    return pl.pallas_call(
        body, name="small_allreduce",
        in_specs=[VMEM] * 8 + [ANY] * n, out_specs=[VMEM] * 2 + [ANY] * n,
        out_shape=[jax.ShapeDtypeStruct((8, d), F32), jax.ShapeDtypeStruct(wps, F32)] + list(plan.out_shape),
        scratch_shapes=[pltpu.VMEM((N_CHIPS, 8, d), F32), pltpu.VMEM((N_CHIPS,) + wps, F32),
                        pltpu.VMEM((8, d), F32), pltpu.VMEM(wps, F32),
                        pltpu.SemaphoreType.DMA((8,)), pltpu.SemaphoreType.DMA((8,))] + list(plan.scratch_shapes),
    )(stats["mix"], stats["x"], stats["mem"], stats["ffn"], stats["loss"], stats["hgrn"], stats["pool"], d_wp,
      *plan_args)


def _adamw_math(w, g, m, v):
    m = ADAM_B1 * m + (1.0 - ADAM_B1) * g
    v = ADAM_B2 * v + (1.0 - ADAM_B2) * (g * g)
    m_hat = m / (1.0 - ADAM_B1 ** ADAM_STEP)
    v_hat = v / (1.0 - ADAM_B2 ** ADAM_STEP)
    delta = -ADAM_LR * (m_hat / (jnp.sqrt(v_hat) + ADAM_EPS) + ADAM_WD * w)
    return delta, m, v


def _adamw(name, mine, theirs, w, m, v, idx, tr):
    rows = w.shape[0]
    cc = mine.shape[1]
    nb = rows // 2 // tr
    heads = w.shape[1] if w.ndim == 3 else 1
    e = cc // heads

    def body(idx_ref, a_ref, b_ref, w_ref, m_ref, v_ref, g_out, d_out, m_out, v_out):
        g = jnp.where(pl.program_id(0) // nb == idx_ref[0], a_ref[...], b_ref[...])
        if w.ndim == 2:
            g_out[...] = g
            d_out[...], m_out[...], v_out[...] = _adamw_math(w_ref[...], g, m_ref[...], v_ref[...])
        else:
            for h in range(heads):
                gh = g[:, h * e:(h + 1) * e]
                g_out[:, h, :] = gh
                d_out[:, h, :], m_out[:, h, :], v_out[:, h, :] = _adamw_math(
                    w_ref[:, h, :], gh, m_ref[:, h, :], v_ref[:, h, :])

    hspec = pl.BlockSpec((tr, cc), lambda i, idx: (i % nb, 0))
    spec = pl.BlockSpec((tr,) + w.shape[1:], lambda i, idx: (i,) + (0,) * (w.ndim - 1))
    return pl.pallas_call(
        body, name=name,
        grid_spec=pltpu.PrefetchScalarGridSpec(
            num_scalar_prefetch=1, grid=(rows // tr,),
            in_specs=[hspec, hspec, spec, spec, spec], out_specs=[spec] * 4),
        out_shape=[jax.ShapeDtypeStruct(w.shape, F32)] * 4,
        compiler_params=_params(("parallel",)),
    )(idx, mine, theirs, w, m, v)


SMALL_NAMES = ("norm_mix_g", "lb_logits", "hgrn_norm_g", "w_pool", "pool_scale", "norm_x_g", "norm_mem_g",
               "norm_ffn_g", "final_norm_g")


def _small_update(slab, d_wp, ws, ms, vs):
    n = len(SMALL_NAMES)
    half = D_MODEL // 2

    def body(slab_ref, wp_ref, *refs):
        w_refs, m_refs, v_refs, outs = refs[:n], refs[n:2 * n], refs[2 * n:3 * n], refs[3 * n:]
        row = lambda k: slab_ref[k:k + 1, :]
        lbl = w_refs[SMALL_NAMES.index("lb_logits")][...]
        s0 = _lower_bound(lbl[0:1, :], lbl[1:2, :])
        dl0 = row(ROW_LB_HGN)[:, :half] * s0 * (1.0 - s0)
        grads = dict(norm_mix_g=row(ROW_GMIX), lb_logits=jnp.concatenate([dl0, -dl0], axis=0),
                     hgrn_norm_g=row(ROW_LB_HGN)[:, half:], w_pool=wp_ref[...], pool_scale=row(ROW_PSCALE)[:, :half],
                     norm_x_g=row(ROW_GX), norm_mem_g=row(ROW_GMEM), norm_ffn_g=row(ROW_GFFN),
                     final_norm_g=row(ROW_GFIN))
        outs[0][...] = row(ROW_LOSS)[:, :128]
        for i, name in enumerate(SMALL_NAMES):
            g = grads[name]
            delta, m2, v2 = _adamw_math(w_refs[i][...], g, m_refs[i][...], v_refs[i][...])
            for o, val in zip(outs[1 + 4 * i:5 + 4 * i], (g, delta, m2, v2)):
                o[...] = val

    args = [ws[k] for k in SMALL_NAMES] + [ms[k] for k in SMALL_NAMES] + [vs[k] for k in SMALL_NAMES]
    out_shape = [jax.ShapeDtypeStruct((1, 128), F32)]
    for k in SMALL_NAMES:
        out_shape += [jax.ShapeDtypeStruct(ws[k].shape, F32)] * 4
    res = pl.pallas_call(
        body, name="small_update",
        in_specs=[VMEM] * (2 + 3 * n), out_specs=[VMEM] * len(out_shape), out_shape=out_shape,
    )(slab, d_wp, *args)
    return res[0], {k: res[1 + 4 * i:5 + 4 * i] for i, k in enumerate(SMALL_NAMES)}


ALL_NAMES = ("norm_mix_g", "w_in", "lb_logits", "hgrn_norm_g", "w_pool", "pool_scale", "w_out", "norm_x_g",
             "norm_mem_g", "w_xq", "w_xk", "w_xv", "w_xo", "norm_ffn_g", "w_ff1", "w_ff2", "final_norm_g")


def _shard_2d(name, a):
    a = a[0]
    if name in ("w_xq", "w_xk", "w_xv"):
        return a.reshape(a.shape[0], -1)
    if name == "w_xo":
        return a.reshape(-1, a.shape[-1])
    return a


def _small_2d(name, a):
    if name == "w_pool":
        return a.reshape(-1, HEAD_DIM)
    if name == "lb_logits":
        return a
    return a.reshape(1, -1)


def kernel(x, mem, norm_mix_g, w_in, lb_logits, hgrn_norm_g, w_pool, pool_scale, w_out, norm_x_g, norm_mem_g, w_xq, w_xk, w_xv, w_xo, norm_ffn_g, w_ff1, w_ff2, final_norm_g, loss_target, m_norm_mix_g, m_w_in, m_lb_logits, m_hgrn_norm_g, m_w_pool, m_pool_scale, m_w_out, m_norm_x_g, m_norm_mem_g, m_w_xq, m_w_xk, m_w_xv, m_w_xo, m_norm_ffn_g, m_w_ff1, m_w_ff2, m_final_norm_g, v_norm_mix_g, v_w_in, v_lb_logits, v_hgrn_norm_g, v_w_pool, v_pool_scale, v_w_out, v_norm_x_g, v_norm_mem_g, v_w_xq, v_w_xk, v_w_xv, v_w_xo, v_norm_ffn_g, v_w_ff1, v_w_ff2, v_final_norm_g):
    w = dict(norm_mix_g=norm_mix_g, w_in=w_in, lb_logits=lb_logits, hgrn_norm_g=hgrn_norm_g, w_pool=w_pool, pool_scale=pool_scale, w_out=w_out, norm_x_g=norm_x_g, norm_mem_g=norm_mem_g, w_xq=w_xq, w_xk=w_xk, w_xv=w_xv, w_xo=w_xo, norm_ffn_g=norm_ffn_g, w_ff1=w_ff1, w_ff2=w_ff2, final_norm_g=final_norm_g)
    m = dict(norm_mix_g=m_norm_mix_g, w_in=m_w_in, lb_logits=m_lb_logits, hgrn_norm_g=m_hgrn_norm_g, w_pool=m_w_pool, pool_scale=m_pool_scale, w_out=m_w_out, norm_x_g=m_norm_x_g, norm_mem_g=m_norm_mem_g, w_xq=m_w_xq, w_xk=m_w_xk, w_xv=m_w_xv, w_xo=m_w_xo, norm_ffn_g=m_norm_ffn_g, w_ff1=m_w_ff1, w_ff2=m_w_ff2, final_norm_g=m_final_norm_g)
    v = dict(norm_mix_g=v_norm_mix_g, w_in=v_w_in, lb_logits=v_lb_logits, hgrn_norm_g=v_hgrn_norm_g, w_pool=v_w_pool, pool_scale=v_pool_scale, w_out=v_w_out, norm_x_g=v_norm_x_g, norm_mem_g=v_norm_mem_g, w_xq=v_w_xq, w_xk=v_w_xk, w_xv=v_w_xv, w_xo=v_w_xo, norm_ffn_g=v_norm_ffn_g, w_ff1=v_w_ff1, w_ff2=v_w_ff2, final_norm_g=v_final_norm_g)

    big_w = {k: _shard_2d(k, w[k]) for k in BIG_NAMES}
    slab_oq = jnp.concatenate([big_w["w_out"], big_w["w_xq"]], axis=0).astype(BF16)
    slab_kv = jnp.concatenate([big_w["w_xk"], big_w["w_xv"]], axis=0).astype(BF16)
    shards = dict(slab_oq=slab_oq, slab_kv=slab_kv,
                  **{k: big_w[k].astype(BF16) for k in ("w_in", "w_xo", "w_ff1", "w_ff2")})

    cx, cy, cc = lax.axis_index("x"), lax.axis_index("y"), lax.axis_index("c")
    chip = 2 * cx + cy
    idx = jnp.stack([cc, chip, chip ^ 1, chip ^ 2, chip ^ 3]).astype(jnp.int32)
    small = {k: w[k] for k in SMALL_NAMES}
    grad_x, stats, d_wp, partials, keep_in, send_in = _step(x[0], mem[0], loss_target[0], small, shards, idx)

    chip_add = lambda k, keep, got: _grad_chip_add("grad_chip_add_" + k, keep, got, tr=min(256, keep.shape[0]))
    halves = {k: chip_add(k, *partials[k]) for k in EARLY_NAMES}
    early = [halves[k] for k in EARLY_NAMES]
    slab_sum, wp_sum, recv_in, *their_early = _small_allreduce(
        stats, d_wp.reshape(-1, HEAD_DIM), _Plans([_ChipExchange(send_in), _HalfExchange(early)]), send_in + early)
    halves["w_in"] = chip_add("w_in", keep_in[0], recv_in)
    theirs = dict(zip(EARLY_NAMES, their_early))
    (theirs["w_in"],) = _run_exchange("grad_half_exchange_w_in", _HalfExchange([halves["w_in"]]), [halves["w_in"]])

    grads, deltas, new_m, new_v = {}, {}, {}, {}
    for k in BIG_NAMES:
        as_held = (lambda a: a[0]) if k in ("w_xq", "w_xk", "w_xv") else functools.partial(_shard_2d, k)
        res = _adamw("adamw_" + k, halves[k], theirs[k], as_held(w[k]), as_held(m[k]), as_held(v[k]), idx,
                     tr=min(256, halves[k].shape[0]))
        for store, val in zip((grads, deltas, new_m, new_v), res):
            store[k] = val.reshape(w[k].shape)

    loss, upd = _small_update(slab_sum, wp_sum, {k: _small_2d(k, w[k]) for k in SMALL_NAMES},
                              {k: _small_2d(k, m[k]) for k in SMALL_NAMES}, {k: _small_2d(k, v[k]) for k in SMALL_NAMES})
    for k in SMALL_NAMES:
        for store, val in zip((grads, deltas, new_m, new_v), upd[k]):
            store[k] = val.reshape(w[k].shape)

    return (loss[0, 0], grad_x[None], *[grads[k] for k in ALL_NAMES], *[deltas[k] for k in ALL_NAMES],
            *[new_m[k] for k in ALL_NAMES], *[new_v[k] for k in ALL_NAMES])
```

```python
import functools

import jax
import jax.numpy as jnp
from jax import lax
from jax.experimental import pallas as pl
from jax.experimental.pallas import tpu as pltpu

F32 = jnp.float32
BF16 = jnp.bfloat16
LOG2E = 1.4426950408889634
NEG_BIG = -1e30
MAX_LOG2_GROWTH = 100.0
MESH = pl.DeviceIdType.MESH
ANY = pl.BlockSpec(memory_space=pl.ANY)
VMEM = pl.BlockSpec(memory_space=pltpu.VMEM)

D_MODEL = 1024
N_CHIPS = 4
HGRN_HEADS = 4
HEAD_DIM = 128
HGRN_WIDTH = HGRN_HEADS * HEAD_DIM
POOL_WINDOWS = (2, 4, 8, 16)
POOL_HALO = 16
SUB = 16
HALF = SUB // 2
CHUNK = 64
HEADS_PER_STEP = 2
XATTN_HEADS = 4
XATTN_HEAD_DIM = 256
EPS = 1e-6
ADAM_LR, ADAM_B1, ADAM_B2, ADAM_EPS, ADAM_WD, ADAM_STEP = 0.001, 0.9, 0.999, 1e-08, 0.01, 10

V7X_VMEM_BYTES = 64 * 1024 * 1024
VMEM_LIMIT = V7X_VMEM_BYTES - 8 * 1024 * 1024

NN = (((1,), (0,)), ((), ()))
NT = (((1,), (1,)), ((), ()))
TN = (((0,), (0,)), ((), ()))

ROW_GMIX, ROW_GX, ROW_GMEM, ROW_GFFN, ROW_GFIN, ROW_LB_HGN, ROW_PSCALE, ROW_LOSS = range(8)


def _dot(a, b, dims=NN):
    return lax.dot_general(a, b, dims, preferred_element_type=F32)


def _sigmoid(x):
    return 1.0 / (1.0 + jnp.exp(-x))


def _rms_fwd(x, g):
    r = lax.rsqrt(jnp.mean(x * x, axis=-1, keepdims=True) + EPS)
    n = x * r
    return n * g, n, r


def _rms_bwd(dh, n, r, g):
    dn = dh * g
    dx = r * (dn - n * jnp.mean(dn * n, axis=-1, keepdims=True))
    return dx, jnp.sum(dh * n, axis=0, keepdims=True)


def _params(sem=None):
    return pltpu.CompilerParams(dimension_semantics=sem, vmem_limit_bytes=VMEM_LIMIT)


def _const(shape):
    nd = len(shape)
    return pl.BlockSpec(shape, lambda *_: (0,) * nd, pipeline_mode=pl.Buffered(1))


def _const_out(shape):
    nd = len(shape)
    return pl.BlockSpec(shape, lambda *_: (0,) * nd)


def _acc_rows(ref, t, rows):
    upd = jnp.concatenate(rows + [jnp.zeros((8 - len(rows), rows[0].shape[1]), F32)], axis=0)

    @pl.when(t == 0)
    def _():
        ref[...] = upd

    @pl.when(t > 0)
    def _():
        ref[...] = ref[...] + upd


def _fuse_exchange(body, n_in, n_out, n_scratch, plan, ndim):
    if plan is None:
        return body
    n = plan.n

    def wrapped(*refs):
        ins, cin = refs[:n_in], refs[n_in:n_in + n]
        outs, cout = refs[n_in + n:n_in + n + n_out], refs[n_in + n + n_out:n_in + 2 * n + n_out]
        rest = refs[n_in + 2 * n + n_out:]
        scr, csem = rest[:n_scratch], rest[n_scratch:]
        first = pl.program_id(0) == 0
        last = pl.program_id(0) == pl.num_programs(0) - 1
        for i in range(1, ndim):
            first = first & (pl.program_id(i) == 0)
            last = last & (pl.program_id(i) == pl.num_programs(i) - 1)

        @pl.when(first)
        def _():
            plan.start(cin, cout, csem)

        body(*ins, *outs, *scr)

        @pl.when(last)
        def _():
            plan.finish(cin, cout, csem)

    return wrapped


def _plan_extras(plan):
    if plan is None:
        return [], [], []
    return [ANY] * plan.n, list(plan.out_shape), list(plan.scratch_shapes)


def _in_proj(x, g, win_g, tm, plan=None, plan_args=()):
    s, d = x.shape
    nsh, _, wc = win_g.shape

    def body(x_ref, g_ref, w_ref, z_ref, h_ref):
        h, _, _ = _rms_fwd(x_ref[...], g_ref[...])
        hb = h.astype(BF16)
        h_ref[...] = hb
        for j in range(nsh):
            z_ref[:, j * wc:(j + 1) * wc] = _dot(hb, w_ref[j])

    x_specs, x_shapes, x_scratch = _plan_extras(plan)
    return pl.pallas_call(
        _fuse_exchange(body, 3, 2, 0, plan, 1), name="in_proj", grid=(s // tm,),
        in_specs=[pl.BlockSpec((tm, d), lambda t: (t, 0)), _const((1, d)), _const((nsh, d, wc))] + x_specs,
        out_specs=[pl.BlockSpec((tm, nsh * wc), lambda t: (t, 0)), pl.BlockSpec((tm, d), lambda t: (t, 0))] + x_specs,
        out_shape=[jax.ShapeDtypeStruct((s, nsh * wc), F32), jax.ShapeDtypeStruct((s, d), BF16)] + x_shapes,
        scratch_shapes=x_scratch,
        compiler_params=_params(("arbitrary",)),
    )(x, g, win_g, *plan_args)


def _lower_bound(l0, l1):
    m = jnp.maximum(l0, l1)
    e0, e1 = jnp.exp(l0 - m), jnp.exp(l1 - m)
    return e0 / (e0 + e1)


def _block_tri(n, group, upper):
    r = lax.broadcasted_iota(jnp.int32, (n, n), 0)
    c = lax.broadcasted_iota(jnp.int32, (n, n), 1)
    keep = (r // group == c // group) & ((c >= r) if upper else (c <= r))
    return keep.astype(BF16)


def _group_cumsum(tri, x):
    hi = x.astype(BF16)
    rest = x - hi.astype(F32)
    mid = rest.astype(BF16)
    lo = (rest - mid.astype(F32)).astype(BF16)
    return (_dot(tri, hi) + _dot(tri, mid)) + _dot(tri, lo)


def _decay(b, bj, rows, first):
    d = b - bj
    if first:
        d = jnp.where(rows >= first, d, NEG_BIG)
    return jnp.exp2(d)


class _RowSums:
    ORDER = (0, 4, 2, 6, 1, 5, 3, 7)

    def __init__(self, rows):
        self.rows = rows
        self.level = {4: {}, 2: {}, 1: {}}

    def _pair(self, p, q, d):
        return jnp.where((self.rows & d) != 0, p + pltpu.roll(p, d, axis=0), q + pltpu.roll(q, HALF - d, axis=0))

    def push(self, j, y, d=4):
        if d == 0:
            self.out = y
            return
        slot = self.level[d]
        key = j % d
        if key not in slot:
            slot[key] = (j, y)
            return
        j0, y0 = slot.pop(key)
        p, q = (y, y0) if j & d else (y0, y)
        self.push(key, self._pair(p, q, d), d // 2)

    def result(self):
        return self.out


def _hgrn_gates(qp, fp, lb):
    sq = _sigmoid(qp)
    sf = _sigmoid(fp)
    f = lb + (1.0 - lb) * sf
    return qp * sq, sq, f, sf


def _hgrn_fwd(z, l0, l1, gn, tc, unroll=1, plan=None, plan_args=()):
    s = z.shape[0]
    nsub = tc // SUB
    hd = HEAD_DIM
    hp = HEADS_PER_STEP

    def body(q_ref, f_ref, v_ref, g_ref, l0_ref, l1_ref, gn_ref, tri_ref, tric_ref, o_ref, oa_ref, st_ref,
             state, qs, ks, bs, os_):
        @pl.when(pl.program_id(1) == 0)
        def _():
            state[...] = jnp.zeros_like(state)

        cols = [slice(hh * hd, (hh + 1) * hd) for hh in range(hp)]
        q, k, lf, bc = [], [], [], []
        for hh, cs in enumerate(cols):
            qh, _, fh, _ = _hgrn_gates(q_ref[:, cs], f_ref[:, cs], _lower_bound(l0_ref[hh], l1_ref[hh]))
            q.append(qh)
            k.append(1.0 - fh)
            lf.append(jnp.log(fh) * LOG2E)
            bc.append(_group_cumsum(tric_ref[...], lf[hh]))
        bounded = functools.reduce(jnp.minimum, [jnp.min(b) for b in bc]) >= -MAX_LOG2_GROWTH

        @pl.when(bounded)
        def _():
            mask = tric_ref[...] > 0
            for hh, cs in enumerate(cols):
                qt = (q[hh] * jnp.exp2(bc[hh])).astype(BF16)
                ki = (k[hh] * jnp.exp2(-bc[hh])).astype(BF16)
                vb = v_ref[:, cs].astype(BF16)
                a = jnp.where(mask, _dot(qt, ki, NT), 0.0).astype(BF16)
                o_in = _dot(a, vb)
                for c in range(tc // CHUNK):
                    rs = slice(c * CHUNK, (c + 1) * CHUNK)
                    st = state[hh]
                    st_ref[hh, c] = st
                    os_[rs, cs] = o_in[rs] + _dot(qt[rs], st.astype(BF16), NT)
                    bl = bc[hh][(c + 1) * CHUNK - 1:(c + 1) * CHUNK, :]
                    kt = (k[hh][rs] * jnp.exp2(bl - bc[hh][rs])).astype(BF16)
                    state[hh] = st * jnp.exp2(bl) + _dot(vb[rs], kt, TN)

        @pl.when(jnp.logical_not(bounded))
        def _():
            rows = lax.broadcasted_iota(jnp.int32, (HALF, 1), 0)
            for hh, cs in enumerate(cols):
                qs[:, cs] = q[hh]
                ks[:, cs] = k[hh]
                bs[:, cs] = _group_cumsum(tri_ref[...], lf[hh])

                def step(i, carry, hh=hh, cs=cs):
                    r0 = pl.multiple_of(i * SUB, SUB)
                    q_ = qs[pl.ds(r0, SUB), cs]
                    k_ = ks[pl.ds(r0, SUB), cs]
                    b_ = bs[pl.ds(r0, SUB), cs]
                    v_ = v_ref[pl.ds(r0, SUB), cs]
                    st = state[hh]

                    @pl.when(i % (CHUNK // SUB) == 0)
                    def _():
                        st_ref[hh, i // (CHUNK // SUB)] = st

                    bl = b_[SUB - 1:SUB, :]
                    o = _dot((q_ * jnp.exp2(b_)).astype(BF16), st.astype(BF16), NT)
                    (q_lo, q_hi), (b_lo, b_hi), (o_lo, o_hi) = ((x[:HALF], x[HALF:]) for x in (q_, b_, o))
                    for j in range(SUB):
                        bj, kj, vj = b_[j:j + 1, :], k_[j:j + 1, :], v_[j:j + 1, :]
                        if j < HALF:
                            e = _decay(b_lo, bj, rows, j)
                            o_lo = o_lo + jnp.sum(q_lo * e * kj, axis=-1, keepdims=True) * vj
                        e = _decay(b_hi, bj, rows, j - HALF if j > HALF else None)
                        o_hi = o_hi + jnp.sum(q_hi * e * kj, axis=-1, keepdims=True) * vj
                    os_[pl.ds(r0, HALF), cs] = o_lo
                    os_[pl.ds(r0 + HALF, HALF), cs] = o_hi
                    kt = (k_ * jnp.exp2(bl - b_)).astype(BF16)
                    state[hh] = st * jnp.exp2(bl) + _dot(v_.astype(BF16), kt, TN)
                    return carry

                lax.fori_loop(0, nsub, step, 0, unroll=unroll)

        for hh, cs in enumerate(cols):
            o = os_[:, cs]
            o_ref[:, cs] = o
            r = lax.rsqrt(jnp.mean(o * o, axis=-1, keepdims=True) + EPS)
            gp = g_ref[:, cs]
            oa_ref[:, cs] = (o * r * gn_ref[hh] * (gp * _sigmoid(gp))).astype(BF16)

    ng = HGRN_HEADS // hp
    col = lambda k: pl.BlockSpec((tc, hp * hd), lambda h, t: (t, k * ng + h))
    vec = pl.BlockSpec((hp, 1, hd), lambda h, t: (h, 0, 0))
    out = pl.BlockSpec((tc, hp * hd), lambda h, t: (t, h))
    x_specs, x_shapes, x_scratch = _plan_extras(plan)
    return pl.pallas_call(
        _fuse_exchange(body, 9, 3, 5, plan, 2), name="hgrn_fwd", grid=(ng, s // tc),
        in_specs=[col(0), col(1), col(2), col(3), vec, vec, vec, _const((tc, tc)), _const((tc, tc))] + x_specs,
        out_specs=[out, out, pl.BlockSpec((hp, tc // CHUNK, hd, hd), lambda h, t: (h, t, 0, 0))] + x_specs,
        out_shape=[jax.ShapeDtypeStruct((s, HGRN_WIDTH), F32), jax.ShapeDtypeStruct((s, HGRN_WIDTH), BF16),
                   jax.ShapeDtypeStruct((HGRN_HEADS, s // CHUNK, hd, hd), F32)] + x_shapes,
        scratch_shapes=[pltpu.VMEM((hp, hd, hd), F32)] + [pltpu.VMEM((tc, hp * hd), F32)] * 4 + x_scratch,
        compiler_params=_params(("arbitrary", "arbitrary")),
    )(z, z, z, z, l0, l1, gn, _block_tri(tc, SUB, False), _block_tri(tc, CHUNK, False), *plan_args)


def _pooled(p, ext, tok0):
    tm = p.shape[0]
    tok = tok0 + lax.broadcasted_iota(jnp.int32, (tm, 1), 0)
    outs = []
    for g, w in enumerate(POOL_WINDOWS):
        acc = ext[:, g * HEAD_DIM:(g + 1) * HEAD_DIM]
        sh = 1
        while sh < w:
            acc = acc + pltpu.roll(acc, sh, axis=0)
            sh *= 2
        cnt = jnp.minimum(tok + 1, w).astype(F32)
        outs.append(acc[POOL_HALO:, :] / cnt - p[:, g * HEAD_DIM:(g + 1) * HEAD_DIM])
    return outs


def _pool_fwd(z, wp, scale, tm):
    s = z.shape[0]
    pw = len(POOL_WINDOWS) * HEAD_DIM
    nb = tm // POOL_HALO

    def body(p_ref, prev_ref, wp_ref, sc_ref, ob_ref):
        t = pl.program_id(0)
        p = p_ref[...]
        prev = jnp.where(t > 0, prev_ref[...], 0.0)
        pooled = _pooled(p, jnp.concatenate([prev, p], axis=0), t * tm)
        ys = [_dot(pooled[g].astype(BF16), wp_ref[g].astype(BF16)) for g in range(len(POOL_WINDOWS))]
        ob_ref[...] = (jnp.concatenate(ys, axis=1) * sc_ref[...]).astype(BF16)

    return pl.pallas_call(
        body, name="pool_fwd", grid=(s // tm,),
        in_specs=[pl.BlockSpec((tm, pw), lambda t: (t, 4)),
                  pl.BlockSpec((POOL_HALO, pw), lambda t: (jnp.maximum(t * nb - 1, 0), 4)),
                  _const(wp.shape), _const((1, pw))],
        out_specs=pl.BlockSpec((tm, pw), lambda t: (t, 0)),
        out_shape=jax.ShapeDtypeStruct((s, pw), BF16),
        compiler_params=_params(("parallel",)),
    )(z, z, wp, scale)


def _kv_proj(mem, g, slab_g):
    m, d = mem.shape
    rows = d // N_CHIPS

    def body(mem_ref, g_ref, wk_ref, wv_ref, xk_ref, xv_ref):
        hm, _, _ = _rms_fwd(mem_ref[...], g_ref[...])
        hb = hm.astype(BF16)
        xk_ref[...] = _dot(hb, wk_ref[...].reshape(d, d)).astype(BF16)
        xv_ref[...] = _dot(hb, wv_ref[...].reshape(d, d)).astype(BF16)

    blk = lambda k: pl.BlockSpec((N_CHIPS, rows, d), lambda i: (0, k, 0))
    return pl.pallas_call(
        body, name="kv_proj", grid=(1,),
        in_specs=[_const((m, d)), _const((1, d)), blk(0), blk(1)],
        out_specs=[_const_out((m, d)), _const_out((m, d))],
        out_shape=[jax.ShapeDtypeStruct((m, d), BF16)] * 2,
        compiler_params=_params(("arbitrary",)),
    )(mem, g, slab_g, slab_g)


def _softmax_rows(sc):
    e = jnp.exp(sc - jnp.max(sc, axis=-1, keepdims=True))
    return e / jnp.sum(e, axis=-1, keepdims=True)


def _mix_xattn_fwd(x, oa, ob, gx, slab_g, wo_g, xk, xv, tm, plan=None, plan_args=()):
    s, d = x.shape
    m = xk.shape[0]
    rows = d // N_CHIPS
    hw = oa.shape[1]
    e = XATTN_HEAD_DIM

    def body(x_ref, oa_ref, ob_ref, gx_ref, wout_ref, wq_ref, wo_ref, xk_ref, xv_ref,
             x1_ref, mixed_ref, hq_ref, xq_ref, att_ref, x2_ref):
        mixed = jnp.concatenate([oa_ref[...], ob_ref[...]], axis=1)
        mixed_ref[...] = mixed
        x1 = x_ref[...] + _dot(mixed, wout_ref[...].reshape(d, d))
        x1_ref[...] = x1
        hq, _, _ = _rms_fwd(x1, gx_ref[...])
        hqb = hq.astype(BF16)
        hq_ref[...] = hqb
        xq = _dot(hqb, wq_ref[...].reshape(d, d)).astype(BF16)
        xq_ref[...] = xq
        atts = []
        for h in range(XATTN_HEADS):
            cs = slice(h * e, (h + 1) * e)
            p = _softmax_rows(_dot(xq[:, cs], xk_ref[:, cs], NT) * (e ** -0.5))
            atts.append(_dot(p.astype(BF16), xv_ref[:, cs]).astype(BF16))
        att = jnp.concatenate(atts, axis=1)
        att_ref[...] = att
        for j in range(N_CHIPS):
            x2_ref[:, j * rows:(j + 1) * rows] = x1[:, j * rows:(j + 1) * rows] + _dot(att, wo_ref[j])

    tile = lambda w: pl.BlockSpec((tm, w), lambda t: (t, 0))
    blk = lambda k: pl.BlockSpec((N_CHIPS, rows, d), lambda t: (0, k, 0), pipeline_mode=pl.Buffered(1))
    x_specs, x_shapes, x_scratch = _plan_extras(plan)
    return pl.pallas_call(
        _fuse_exchange(body, 9, 6, 0, plan, 1), name="mix_xattn_fwd", grid=(s // tm,),
        in_specs=[tile(d), tile(hw), tile(hw), _const((1, d)), blk(0), blk(1), _const(wo_g.shape),
                  _const((m, d)), _const((m, d))] + x_specs,
        out_specs=[tile(d)] * 6 + x_specs,
        out_shape=[jax.ShapeDtypeStruct((s, d), F32)] + [jax.ShapeDtypeStruct((s, d), BF16)] * 4
                  + [jax.ShapeDtypeStruct((s, d), F32)] + x_shapes,
        scratch_shapes=x_scratch,
        compiler_params=_params(("arbitrary",)),
    )(x, oa, ob, gx, slab_g, slab_g, wo_g, xk, xv, *plan_args)


def _mlp_loss_fwd(x2, gffn, gfin, w1_g, w2_g, target, tm):
    s, d = x2.shape
    wr = w1_g.shape[1]

    def body(x2_ref, gffn_ref, gfin_ref, w1_ref, w2_ref, tg_ref, a_ref, hf_ref, dx3_ref, dx3b_ref, st_ref):
        x2v = x2_ref[...]
        hf, _, _ = _rms_fwd(x2v, gffn_ref[...])
        hfb = hf.astype(BF16)
        hf_ref[...] = hfb
        acc = x2v
        for j in range(N_CHIPS):
            a = _dot(hfb, w1_ref[j])
            a_ref[:, j * wr:(j + 1) * wr] = a
            r = jnp.maximum(a, 0.0)
            acc = acc + _dot((r * r).astype(BF16), w2_ref[j])
        gf = gfin_ref[...]
        y, n, r3 = _rms_fwd(acc, gf)
        err = y - tg_ref[...]
        loss = 0.5 * jnp.sum(jnp.sum(err * err, axis=-1, keepdims=True) * (1.0 / d), axis=0, keepdims=True)
        dy = err * (1.0 / d)
        dx3, dgf = _rms_bwd(dy, n, r3, gf)
        dx3_ref[...] = dx3
        dx3b_ref[...] = dx3.astype(BF16)
        _acc_rows(st_ref, pl.program_id(0), [dgf, jnp.broadcast_to(loss, (1, d))])

    tile = lambda w: pl.BlockSpec((tm, w), lambda t: (t, 0))
    blk = lambda k: pl.BlockSpec((N_CHIPS, wr, d), lambda t: (0, k, 0), pipeline_mode=pl.Buffered(1))
    return pl.pallas_call(
        body, name="mlp_loss_fwd", grid=(s // tm,),
        in_specs=[tile(d), _const((1, d)), _const((1, d)), blk(0), blk(0), tile(d)],
        out_specs=[tile(N_CHIPS * wr), tile(d), tile(d), tile(d), _const_out((8, d))],
        out_shape=[jax.ShapeDtypeStruct((s, N_CHIPS * wr), F32), jax.ShapeDtypeStruct((s, d), BF16),
                   jax.ShapeDtypeStruct((s, d), F32), jax.ShapeDtypeStruct((s, d), BF16),
                   jax.ShapeDtypeStruct((8, d), F32)],
        compiler_params=_params(("arbitrary",)),
    )(x2, gffn, gfin, w1_g, w2_g, target)


def _mlp_bwd(dx3, dx3b, a, x2, gffn, w1_g, w2_g, tm):
    s, d = x2.shape
    wr = w1_g.shape[1]

    def body(dx3_ref, dx3b_ref, a_ref, x2_ref, g_ref, w1_ref, w2_ref, da_ref, u_ref, dx2_ref, dx2b_ref, st_ref):
        dyb = dx3b_ref[...]
        dhf = jnp.zeros((tm, d), F32)
        for j in range(N_CHIPS):
            r = jnp.maximum(a_ref[:, j * wr:(j + 1) * wr], 0.0)
            da = (_dot(dyb, w2_ref[j], NT) * (2.0 * r)).astype(BF16)
            da_ref[:, j * wr:(j + 1) * wr] = da
            u_ref[:, j * wr:(j + 1) * wr] = (r * r).astype(BF16)
            dhf = dhf + _dot(da, w1_ref[j], NT)
        g = g_ref[...]
        _, n, r2 = _rms_fwd(x2_ref[...], g)
        dxn, dg = _rms_bwd(dhf, n, r2, g)
        dx2 = dx3_ref[...] + dxn
        dx2_ref[...] = dx2
        dx2b_ref[...] = dx2.astype(BF16)
        _acc_rows(st_ref, pl.program_id(0), [dg])

    tile = lambda w: pl.BlockSpec((tm, w), lambda t: (t, 0))
    blk = lambda k: pl.BlockSpec((N_CHIPS, wr, d), lambda t: (0, k, 0), pipeline_mode=pl.Buffered(1))
    nf = N_CHIPS * wr
    return pl.pallas_call(
        body, name="mlp_bwd", grid=(s // tm,),
        in_specs=[tile(d), tile(d), tile(nf), tile(d), _const((1, d)), blk(0), blk(0)],
        out_specs=[tile(nf), tile(nf), tile(d), tile(d), _const_out((8, d))],
        out_shape=[jax.ShapeDtypeStruct((s, nf), BF16), jax.ShapeDtypeStruct((s, nf), BF16),
                   jax.ShapeDtypeStruct((s, d), F32), jax.ShapeDtypeStruct((s, d), BF16),
                   jax.ShapeDtypeStruct((8, d), F32)],
        compiler_params=_params(("arbitrary",)),
    )(dx3, dx3b, a, x2, gffn, w1_g, w2_g)


def _xattn_mix_bwd(dx2, x1, xq, xk, xv, gx, slab_g, wo_g, tm, plan=None, plan_args=()):
    s, d = x1.shape
    m = xk.shape[0]
    rows = d // N_CHIPS
    e = XATTN_HEAD_DIM

    def body(dx2_ref, x1_ref, xq_ref, xk_ref, xv_ref, gx_ref, wout_ref, wq_ref, wo_ref,
             dx1_ref, dx1b_ref, dxq_ref, dmix_ref, dxk_ref, dxv_ref, st_ref):
        t = pl.program_id(0)
        dx2 = dx2_ref[...]
        dx2b = dx2.astype(BF16)
        datt = jnp.zeros((tm, d), F32)
        for j in range(N_CHIPS):
            datt = datt + _dot(dx2b[:, j * rows:(j + 1) * rows], wo_ref[j], NT)
        dattb = datt.astype(BF16)
        dxqs, dxks, dxvs = [], [], []
        for h in range(XATTN_HEADS):
            cs = slice(h * e, (h + 1) * e)
            xq_h, xk_h, xv_h = xq_ref[:, cs], xk_ref[:, cs], xv_ref[:, cs]
            p = _softmax_rows(_dot(xq_h, xk_h, NT) * (e ** -0.5))
            dp = _dot(dattb[:, cs], xv_h, NT)
            ds = (p * (dp - jnp.sum(dp * p, axis=-1, keepdims=True)) * (e ** -0.5)).astype(BF16)
            dxqs.append(_dot(ds, xk_h).astype(BF16))
            dxks.append(_dot(ds, xq_h, TN))
            dxvs.append(_dot(p.astype(BF16), dattb[:, cs], TN))
        dxq = jnp.concatenate(dxqs, axis=1)
        dxq_ref[...] = dxq
        dxk = jnp.concatenate(dxks, axis=1)
        dxv = jnp.concatenate(dxvs, axis=1)

        @pl.when(t == 0)
        def _():
            dxk_ref[...] = dxk
            dxv_ref[...] = dxv

        @pl.when(t > 0)
        def _():
            dxk_ref[...] = dxk_ref[...] + dxk
            dxv_ref[...] = dxv_ref[...] + dxv

        dhq = jnp.concatenate([_dot(dxq, wq_ref[j], NT) for j in range(N_CHIPS)], axis=1)
        g = gx_ref[...]
        _, n, r1 = _rms_fwd(x1_ref[...], g)
        dxn, dg = _rms_bwd(dhq, n, r1, g)
        dx1 = dx2 + dxn
        dx1_ref[...] = dx1
        dx1b = dx1.astype(BF16)
        dx1b_ref[...] = dx1b
        for j in range(N_CHIPS):
            dmix_ref[:, j * rows:(j + 1) * rows] = _dot(dx1b, wout_ref[j], NT)
        _acc_rows(st_ref, t, [dg])

    tile = lambda: pl.BlockSpec((tm, d), lambda t: (t, 0))
    blk = lambda k: pl.BlockSpec((N_CHIPS, rows, d), lambda t: (0, k, 0), pipeline_mode=pl.Buffered(1))
    x_specs, x_shapes, x_scratch = _plan_extras(plan)
    return pl.pallas_call(
        _fuse_exchange(body, 9, 7, 0, plan, 1), name="xattn_mix_bwd", grid=(s // tm,),
        in_specs=[tile(), tile(), tile(), _const((m, d)), _const((m, d)), _const((1, d)), blk(0), blk(1),
                  _const(wo_g.shape)] + x_specs,
        out_specs=[tile(), tile(), tile(), tile(), _const_out((m, d)), _const_out((m, d)), _const_out((8, d))]
                  + x_specs,
        out_shape=[jax.ShapeDtypeStruct((s, d), F32), jax.ShapeDtypeStruct((s, d), BF16),
                   jax.ShapeDtypeStruct((s, d), BF16), jax.ShapeDtypeStruct((s, d), F32),
                   jax.ShapeDtypeStruct((m, d), F32), jax.ShapeDtypeStruct((m, d), F32),
                   jax.ShapeDtypeStruct((8, d), F32)] + x_shapes,
        scratch_shapes=x_scratch,
        compiler_params=_params(("arbitrary",)),
    )(dx2, x1, xq, xk, xv, gx, slab_g, slab_g, wo_g, *plan_args)


def _kv_bwd(mem, g, dxk, dxv, slab_g):
    m, d = mem.shape
    rows = d // N_CHIPS

    def body(mem_ref, g_ref, dxk_ref, dxv_ref, wk_ref, wv_ref, dwk_ref, dwv_ref, st_ref):
        gv = g_ref[...]
        hm, n, _ = _rms_fwd(mem_ref[...], gv)
        hb = hm.astype(BF16)
        dkb = dxk_ref[...].astype(BF16)
        dvb = dxv_ref[...].astype(BF16)
        dhm = []
        for j in range(N_CHIPS):
            hj = hb[:, j * rows:(j + 1) * rows]
            dwk_ref[j] = _dot(hj, dkb, TN)
            dwv_ref[j] = _dot(hj, dvb, TN)
            dhm.append(_dot(dkb, wk_ref[j], NT) + _dot(dvb, wv_ref[j], NT))
        dg = jnp.sum(jnp.concatenate(dhm, axis=1) * n, axis=0, keepdims=True)
        st_ref[...] = jnp.concatenate([dg, jnp.zeros((7, d), F32)], axis=0)

    blk = lambda k: pl.BlockSpec((N_CHIPS, rows, d), lambda i: (0, k, 0))
    return pl.pallas_call(
        body, name="kv_bwd", grid=(1,),
        in_specs=[_const((m, d)), _const((1, d)), _const((m, d)), _const((m, d)), blk(0), blk(1)],
        out_specs=[_const_out((N_CHIPS, rows, d)), _const_out((N_CHIPS, rows, d)), _const_out((8, d))],
        out_shape=[jax.ShapeDtypeStruct((N_CHIPS, rows, d), F32)] * 2 + [jax.ShapeDtypeStruct((8, d), F32)],
        compiler_params=_params(("arbitrary",)),
    )(mem, g, dxk, dxv, slab_g, slab_g)


def _pool_bwd(z, dmix, wp, scale, tm, plan=None, plan_args=()):
    s = z.shape[0]
    ng = len(POOL_WINDOWS)
    pw = ng * HEAD_DIM
    nb = tm // POOL_HALO
    nt = s // tm
    n_ext = tm + POOL_HALO

    def body(p_ref, prev_ref, dm_ref, dmn_ref, wp_ref, sc_ref, dp_ref, dwp_ref, st_ref):
        t = pl.program_id(0)
        p = p_ref[...]
        prev = jnp.where(t > 0, prev_ref[...], 0.0)
        pooled = _pooled(p, jnp.concatenate([prev, p], axis=0), t * tm)
        dm = dm_ref[...]
        dme = jnp.concatenate([dm, jnp.where(t < nt - 1, dmn_ref[...], 0.0)], axis=0) * sc_ref[...]
        tok = t * tm + lax.broadcasted_iota(jnp.int32, (n_ext, 1), 0)
        dsc, dps, dwps = [], [], []
        for g, w in enumerate(POOL_WINDOWS):
            cs = slice(g * HEAD_DIM, (g + 1) * HEAD_DIM)
            wpb = wp_ref[g].astype(BF16)
            pb = pooled[g].astype(BF16)
            dsc.append(jnp.sum(dm[:, cs] * _dot(pb, wpb), axis=0, keepdims=True))
            dye = dme[:, cs].astype(BF16)
            dwps.append(_dot(pb, dye[:tm], TN))
            dpe = _dot(dye, wpb, NT)
            acc = dpe / jnp.minimum(tok + 1, w).astype(F32)
            sh = 1
            while sh < w:
                acc = acc + pltpu.roll(acc, n_ext - sh, axis=0)
                sh *= 2
            dps.append(acc[:tm] - dpe[:tm])
        dp_ref[...] = jnp.concatenate(dps, axis=1)
        dsc_row = jnp.concatenate(dsc, axis=1)

        @pl.when(t == 0)
        def _():
            for g in range(ng):
                dwp_ref[g] = dwps[g]

        @pl.when(t > 0)
        def _():
            for g in range(ng):
                dwp_ref[g] = dwp_ref[g] + dwps[g]

        _acc_rows(st_ref, t, [dsc_row])

    x_specs, x_shapes, x_scratch = _plan_extras(plan)
    return pl.pallas_call(
        _fuse_exchange(body, 6, 3, 0, plan, 1), name="pool_bwd", grid=(nt,),
        in_specs=[pl.BlockSpec((tm, pw), lambda t: (t, 4)),
                  pl.BlockSpec((POOL_HALO, pw), lambda t: (jnp.maximum(t * nb - 1, 0), 4)),
                  pl.BlockSpec((tm, pw), lambda t: (t, 1)),
                  pl.BlockSpec((POOL_HALO, pw), lambda t: (jnp.minimum((t + 1) * nb, s // POOL_HALO - 1), 1)),
                  _const(wp.shape), _const((1, pw))] + x_specs,
        out_specs=[pl.BlockSpec((tm, pw), lambda t: (t, 0)), _const_out(wp.shape), _const_out((8, pw))] + x_specs,
        out_shape=[jax.ShapeDtypeStruct((s, pw), F32), jax.ShapeDtypeStruct(wp.shape, F32),
                   jax.ShapeDtypeStruct((8, pw), F32)] + x_shapes,
        scratch_shapes=x_scratch,
        compiler_params=_params(("arbitrary",)),
    )(z, z, dmix, dmix, wp, scale, *plan_args)


def _hgrn_bwd(z, o, dmix, st, l0, l1, gn, tc, unroll=1, plan=None, plan_args=()):
    s = z.shape[0]
    nsub = tc // SUB
    nt = s // tc
    hd = HEAD_DIM
    hp = HEADS_PER_STEP

    def body(q_ref, f_ref, v_ref, g_ref, l0_ref, l1_ref, gn_ref, o_ref, dm_ref, st_ref,
             tril_ref, triu_ref, trilc_ref, triuc_ref,
             dq_ref, df_ref, di_ref, dg_ref, stat_ref, dstate, qs, ks, bs, dos, dqs, dks, dbs, sts):
        t = pl.program_id(1)

        @pl.when(t == 0)
        def _():
            dstate[...] = jnp.zeros_like(dstate)

        cols = [slice(hh * hd, (hh + 1) * hd) for hh in range(hp)]
        heads = []
        for hh, cs in enumerate(cols):
            lb = _lower_bound(l0_ref[hh], l1_ref[hh])
            qp = q_ref[:, cs]
            q, sq, f, sf = _hgrn_gates(qp, f_ref[:, cs], lb)
            lf = jnp.log(f) * LOG2E
            o = o_ref[:, cs]
            r = lax.rsqrt(jnp.mean(o * o, axis=-1, keepdims=True) + EPS)
            n = o * r
            gnv = gn_ref[hh]
            gp = g_ref[:, cs]
            sg = _sigmoid(gp)
            dm = dm_ref[:, cs]
            dg_ref[:, cs] = dm * (n * gnv) * (sg * (1.0 + gp * (1.0 - sg)))
            don = dm * (gp * sg)
            dn = don * gnv
            heads.append(dict(lb=lb, qp=qp, q=q, sq=sq, f=f, sf=sf, k=1.0 - f, lf=lf,
                              bc=_group_cumsum(trilc_ref[...], lf), dgn=jnp.sum(don * n, axis=0, keepdims=True),
                              do=r * (dn - n * jnp.mean(dn * n, axis=-1, keepdims=True))))
        bounded = functools.reduce(jnp.minimum, [jnp.min(h["bc"]) for h in heads]) >= -MAX_LOG2_GROWTH

        def factored(hh, cs, q, k, bc, do_all):
            eb = jnp.exp2(bc)
            eib = jnp.exp2(-bc)
            qt = (q * eb).astype(BF16)
            ki = (k * eib).astype(BF16)
            vb = v_ref[:, cs].astype(BF16)
            dob = do_all.astype(BF16)
            mask = trilc_ref[...] > 0
            a = jnp.where(mask, _dot(qt, ki, NT), 0.0).astype(BF16)
            da = jnp.where(mask, _dot(dob, vb, NT), 0.0).astype(BF16)
            dq_in = _dot(da, ki)
            dk_in = _dot(da, qt, TN)
            dv_in = _dot(a, dob, TN)
            last_row = lax.broadcasted_iota(jnp.int32, (CHUNK, 1), 0) == CHUNK - 1
            for c in reversed(range(tc // CHUNK)):
                rs = slice(c * CHUNK, (c + 1) * CHUNK)
                stp = st_ref[hh, c]
                dst = dstate[hh]
                dstb = dst.astype(BF16)
                bl = bc[(c + 1) * CHUNK - 1:(c + 1) * CHUNK, :]
                ekl = jnp.exp2(bl - bc[rs])
                ebl = jnp.exp2(bl)
                kt = k[rs] * ekl
                dq_st = _dot(dob[rs], stp.astype(BF16)) * eb[rs]
                dkt = _dot(vb[rs], dstb)
                extra = jnp.sum(kt * dkt, axis=0, keepdims=True) + ebl * jnp.sum(stp * dst, axis=0, keepdims=True)
                dqs[rs, cs] = dq_st + dq_in[rs] * eb[rs]
                dks[rs, cs] = dkt * ekl + dk_in[rs] * eib[rs]
                di_ref[rs, cs] = _dot(kt.astype(BF16), dstb, NT) + dv_in[rs]
                dbs[rs, cs] = (q[rs] * dq_st - kt * dkt + jnp.where(last_row, extra, 0.0)
                               + (qt[rs].astype(F32) * dq_in[rs] - ki[rs].astype(F32) * dk_in[rs]))
                dstate[hh] = dst * ebl + _dot(dob[rs], qt[rs], TN)
            dbs[:, cs] = _group_cumsum(triuc_ref[...], dbs[:, cs])

        def exact(hh, cs, q, k, lf, do_all):
            qs[:, cs] = q
            ks[:, cs] = k
            bs[:, cs] = _group_cumsum(tril_ref[...], lf)
            dos[:, cs] = do_all
            per = CHUNK // SUB

            def restore(i, carry):
                @pl.when(i % per == 0)
                def _():
                    sts[i] = st_ref[hh, i // per]

                @pl.when(i % per != 0)
                def _():
                    rp = pl.multiple_of((i - 1) * SUB, SUB)
                    b_ = bs[pl.ds(rp, SUB), cs]
                    bl = b_[SUB - 1:SUB, :]
                    kt = (ks[pl.ds(rp, SUB), cs] * jnp.exp2(bl - b_)).astype(BF16)
                    sts[i] = sts[i - 1] * jnp.exp2(bl) + _dot(v_ref[pl.ds(rp, SUB), cs].astype(BF16), kt, TN)

                return carry

            lax.fori_loop(0, nsub, restore, 0)
            rows = lax.broadcasted_iota(jnp.int32, (HALF, 1), 0)
            last_row = lax.broadcasted_iota(jnp.int32, (SUB, 1), 0) == SUB - 1

            def step(i, carry):
                ii = nsub - 1 - i
                r0 = pl.multiple_of(ii * SUB, SUB)
                q_ = qs[pl.ds(r0, SUB), cs]
                k_ = ks[pl.ds(r0, SUB), cs]
                b_ = bs[pl.ds(r0, SUB), cs]
                v_ = v_ref[pl.ds(r0, SUB), cs]
                do_ = dos[pl.ds(r0, SUB), cs]
                stp = sts[ii]
                dst = dstate[hh]
                bl = b_[SUB - 1:SUB, :]
                eb = jnp.exp2(b_)
                ekl = jnp.exp2(bl - b_)
                ebl = jnp.exp2(bl)
                dob = do_.astype(BF16)
                dstb = dst.astype(BF16)
                kt = k_ * ekl
                dq = _dot(dob, stp.astype(BF16)) * eb
                dkt = _dot(v_.astype(BF16), dstb)
                dk = dkt * ekl
                dv = _dot(kt.astype(BF16), dstb, NT)
                extra = jnp.sum(kt * dkt, axis=0, keepdims=True) + ebl * jnp.sum(stp * dst, axis=0, keepdims=True)
                halves = lambda x: [x[:HALF], x[HALF:]]
                q_h, b_h, do_h, dq_h, dk_h, dv_h = (halves(x) for x in (q_, b_, do_, dq, dk, dv))
                for own in range(2):
                    dk_rows, dv_rows = _RowSums(rows), _RowSums(rows)
                    for jj in _RowSums.ORDER:
                        j = own * HALF + jj
                        bj, kj, vj = b_[j:j + 1, :], k_[j:j + 1, :], v_[j:j + 1, :]
                        dk_sum = dv_sum = None
                        for h in range(own, 2):
                            e = _decay(b_h[h], bj, rows, jj if h == own else None)
                            pe = q_h[h] * e
                            acol = jnp.sum(pe * kj, axis=-1, keepdims=True)
                            dacol = jnp.sum(do_h[h] * vj, axis=-1, keepdims=True)
                            dq_h[h] = dq_h[h] + dacol * (e * kj)
                            dk_sum = dacol * pe if dk_sum is None else dk_sum + dacol * pe
                            dv_sum = acol * do_h[h] if dv_sum is None else dv_sum + acol * do_h[h]
                        dk_rows.push(jj, dk_sum)
                        dv_rows.push(jj, dv_sum)
                    dk_h[own] = dk_h[own] + dk_rows.result()
                    dv_h[own] = dv_h[own] + dv_rows.result()
                dq, dk, dv = (jnp.concatenate(x, axis=0) for x in (dq_h, dk_h, dv_h))
                dqs[pl.ds(r0, SUB), cs] = dq
                dks[pl.ds(r0, SUB), cs] = dk
                di_ref[pl.ds(r0, SUB), cs] = dv
                dbs[pl.ds(r0, SUB), cs] = q_ * dq - k_ * dk + jnp.where(last_row, extra, 0.0)
                dstate[hh] = dst * ebl + _dot(dob, (q_ * eb).astype(BF16), TN)
                return carry

            lax.fori_loop(0, nsub, step, 0, unroll=unroll)
            dbs[:, cs] = _group_cumsum(triu_ref[...], dbs[:, cs])

        @pl.when(bounded)
        def _():
            for hh, cs in enumerate(cols):
                factored(hh, cs, heads[hh]["q"], heads[hh]["k"], heads[hh]["bc"], heads[hh]["do"])

        @pl.when(jnp.logical_not(bounded))
        def _():
            for hh, cs in enumerate(cols):
                exact(hh, cs, heads[hh]["q"], heads[hh]["k"], heads[hh]["lf"], heads[hh]["do"])

        for hh, cs in enumerate(cols):
            h = heads[hh]
            dfv = dbs[:, cs] / h["f"] - dks[:, cs]
            df_ref[:, cs] = dfv * (1.0 - h["lb"]) * h["sf"] * (1.0 - h["sf"])
            dlb = jnp.sum(dfv * (1.0 - h["sf"]), axis=0, keepdims=True)
            dq_ref[:, cs] = dqs[:, cs] * (h["sq"] * (1.0 + h["qp"] * (1.0 - h["sq"])))
            _acc_rows(stat_ref.at[hh], t, [h["dgn"], dlb])

    rev = lambda t: nt - 1 - t
    ng = HGRN_HEADS // hp
    col = lambda k: pl.BlockSpec((tc, hp * hd), lambda h, t: (rev(t), k * ng + h))
    vec = pl.BlockSpec((hp, 1, hd), lambda h, t: (h, 0, 0))
    head = pl.BlockSpec((tc, hp * hd), lambda h, t: (rev(t), h))
    x_specs, x_shapes, x_scratch = _plan_extras(plan)
    return pl.pallas_call(
        _fuse_exchange(body, 14, 5, 9, plan, 2), name="hgrn_bwd", grid=(ng, nt),
        in_specs=[col(0), col(1), col(2), col(3), vec, vec, vec, head, head,
                  pl.BlockSpec((hp, tc // CHUNK, hd, hd), lambda h, t: (h, rev(t), 0, 0))]
                 + [_const((tc, tc))] * 4 + x_specs,
        out_specs=[head, head, head, head, pl.BlockSpec((hp, 8, hd), lambda h, t: (h, 0, 0))] + x_specs,
        out_shape=[jax.ShapeDtypeStruct((s, HGRN_WIDTH), F32)] * 4 + [jax.ShapeDtypeStruct((HGRN_HEADS, 8, hd), F32)]
                  + x_shapes,
        scratch_shapes=[pltpu.VMEM((hp, hd, hd), F32)] + [pltpu.VMEM((tc, hp * hd), F32)] * 7
                       + [pltpu.VMEM((nsub, hd, hd), F32)] + x_scratch,
        compiler_params=_params(("arbitrary", "arbitrary")),
    )(z, z, z, z, l0, l1, gn, o, dmix, st, _block_tri(tc, SUB, False), _block_tri(tc, SUB, True),
      _block_tri(tc, CHUNK, False), _block_tri(tc, CHUNK, True), *plan_args)


def _in_bwd(dparts, dx1, x, g, win_g, tm, plan=None, plan_args=()):
    s, d = x.shape
    nsh, _, wc = win_g.shape
    pw = dparts[0].shape[1]

    def body(dq_ref, df_ref, di_ref, dg_ref, dp_ref, dx1_ref, x_ref, g_ref, w_ref, gx_ref, dz_ref, st_ref):
        dz = jnp.concatenate([dq_ref[...], df_ref[...], di_ref[...], dg_ref[...], dp_ref[...]], axis=1).astype(BF16)
        dz_ref[...] = dz
        dh = jnp.zeros((tm, d), F32)
        for j in range(nsh):
            dh = dh + _dot(dz[:, j * wc:(j + 1) * wc], w_ref[j], NT)
        gv = g_ref[...]
        _, n, r = _rms_fwd(x_ref[...], gv)
        dxn, dg = _rms_bwd(dh, n, r, gv)
        gx_ref[...] = dx1_ref[...] + dxn
        _acc_rows(st_ref, pl.program_id(0), [dg])

    tile = lambda w: pl.BlockSpec((tm, w), lambda t: (t, 0))
    x_specs, x_shapes, x_scratch = _plan_extras(plan)
    return pl.pallas_call(
        _fuse_exchange(body, 9, 3, 0, plan, 1), name="in_bwd", grid=(s // tm,),
        in_specs=[tile(pw)] * 5 + [tile(d), tile(d), _const((1, d)), _const(win_g.shape)] + x_specs,
        out_specs=[tile(d), tile(nsh * wc), _const_out((8, d))] + x_specs,
        out_shape=[jax.ShapeDtypeStruct((s, d), F32), jax.ShapeDtypeStruct((s, nsh * wc), BF16),
                   jax.ShapeDtypeStruct((8, d), F32)] + x_shapes,
        scratch_shapes=x_scratch,
        compiler_params=_params(("arbitrary",)),
    )(*dparts, dx1, x, g, win_g, *plan_args)


def _tn_grad(name, a, b, out_rows, out_cols, a_sharded, tr, tc, plan=None, plan_args=()):
    s = a.shape[0]
    nr, nc = out_rows // tr, out_cols // tc

    def body(a_ref, b_ref, o_ref):
        o_ref[...] = _dot(a_ref[...], b_ref[...], TN)

    a_map = (lambda j, i, k: (0, j * nr + i)) if a_sharded else (lambda j, i, k: (0, i))
    b_map = (lambda j, i, k: (0, k)) if a_sharded else (lambda j, i, k: (0, j * nc + k))
    x_specs, x_shapes, x_scratch = _plan_extras(plan)
    res = pl.pallas_call(
        _fuse_exchange(body, 2, 1, 0, plan, 3), name=name, grid=(N_CHIPS, nr, nc),
        in_specs=[pl.BlockSpec((s, tr), a_map), pl.BlockSpec((s, tc), b_map)] + x_specs,
        out_specs=[pl.BlockSpec((None, tr, tc), lambda j, i, k: (j, i, k))] + x_specs,
        out_shape=[jax.ShapeDtypeStruct((N_CHIPS, out_rows, out_cols), F32)] + x_shapes,
        scratch_shapes=x_scratch,
        compiler_params=_params(("arbitrary", "arbitrary", "arbitrary")),
    )(a, b, *plan_args)
    return res if plan else res[0]


FFN_NAMES = ("w_ff1", "w_ff2")
ATTN_NAMES = ("w_xo", "w_xq", "w_out", "w_xk", "w_xv")
EARLY_NAMES = FFN_NAMES + ATTN_NAMES
BIG_NAMES = EARLY_NAMES + ("w_in",)


def _halved(g):
    return g.reshape(N_CHIPS, 2, g.shape[1] // 2, g.shape[2])


def _pair_adds(names, gs, got, idx):
    pairs = [_grad_pair_add("grad_pair_add_" + k, g, r, idx, tr=min(256, g.shape[2])) for k, g, r in zip(names, gs, got)]
    return [p[0] for p in pairs], [p[1] for p in pairs]


def _step(x, mem, target, small, shards, idx):
    d = x.shape[1]
    l0 = small["lb_logits"][0].reshape(HGRN_HEADS, 1, HEAD_DIM)
    l1 = small["lb_logits"][1].reshape(HGRN_HEADS, 1, HEAD_DIM)
    gn = small["hgrn_norm_g"].reshape(HGRN_HEADS, 1, HEAD_DIM)
    wp = small["w_pool"].reshape(len(POOL_WINDOWS), HEAD_DIM, HEAD_DIM)
    psc = small["pool_scale"].reshape(1, -1)
    gmix, gx, gmem, gffn = (small[k].reshape(1, d) for k in ("norm_mix_g", "norm_x_g", "norm_mem_g", "norm_ffn_g"))
    gfin = small["final_norm_g"].reshape(1, d)

    (win_g,) = _run_exchange("gather_w_in", _WeightGather([shards["w_in"]]), [shards["w_in"]])
    z, h, kv_g = _in_proj(x, gmix, win_g, tm=512, plan=_WeightGather([shards["slab_kv"]]),
                          plan_args=[shards["slab_kv"]])
    mid_w = [shards["slab_oq"], shards["w_xo"], shards["w_ff1"]]
    o, oa, st, oq_g, wo_g, w1_g = _hgrn_fwd(z, l0, l1, gn, tc=256, unroll=8,
                                            plan=_WeightGather(mid_w), plan_args=mid_w)
    ob = _pool_fwd(z, wp, psc, tm=512)
    xk, xv = _kv_proj(mem, gmem, kv_g)
    late_w = [shards["w_ff2"]]
    x1, mixed, hq, xq, att, x2, w2_g = _mix_xattn_fwd(x, oa, ob, gx, oq_g, wo_g, xk, xv, tm=512,
                                                      plan=_WeightGather(late_w), plan_args=late_w)
    a, hf, dx3, dx3b, st_loss = _mlp_loss_fwd(x2, gffn, gfin, w1_g, w2_g, target, tm=512)

    da, u, dx2, dx2b, st_ffn = _mlp_bwd(dx3, dx3b, a, x2, gffn, w1_g, w2_g, tm=256)
    g_ff1 = [_halved(_tn_grad("dw_ff1", hf, da, d, d, False, 1024, 1024))]
    dw_ff2, *got = _tn_grad("dw_ff2", u, dx3b, d, d, True, 1024, 1024, plan=_PairExchange(g_ff1), plan_args=g_ff1)
    keep_ff1, send_ff1 = _pair_adds(("w_ff1",), g_ff1, got, idx)
    g_ff2 = [_halved(dw_ff2)]
    dx1, dx1b, dxq, dmix, dxk, dxv, st_x, *got = _xattn_mix_bwd(
        dx2, x1, xq, xk, xv, gx, oq_g, wo_g, tm=512,
        plan=_Plans([_ChipExchange(send_ff1), _PairExchange(g_ff2)]), plan_args=send_ff1 + g_ff2)
    recv_ff1 = got[:1]
    keep_ff2, send_ff2 = _pair_adds(("w_ff2",), g_ff2, got[1:], idx)
    dw = {}
    dw["w_xo"] = _tn_grad("dw_xo", att, dx2b, d, d // N_CHIPS, False, 1024, 256)
    dw["w_xq"] = _tn_grad("dw_xq", hq, dxq, d // N_CHIPS, d, True, 256, 1024)
    dw["w_out"] = _tn_grad("dw_out", mixed, dx1b, d // N_CHIPS, d, True, 256, 1024)
    dw["w_xk"], dw["w_xv"], st_mem = _kv_bwd(mem, gmem, dxk, dxv, kv_g)
    gs_attn = [_halved(dw[k]) for k in ATTN_NAMES]
    dp, d_wp, st_pool, *got_attn = _pool_bwd(z, dmix, wp, psc, tm=512, plan=_PairExchange(gs_attn), plan_args=gs_attn)
    keep_attn, send_attn = _pair_adds(ATTN_NAMES, gs_attn, got_attn, idx)
    sends = send_ff2 + send_attn
    dq, df, di, dg, st_hgrn, *received = _hgrn_bwd(z, o, dmix, st, l0, l1, gn, tc=256, unroll=4,
                                                    plan=_ChipExchange(sends), plan_args=sends)
    grad_x, dz, st_mix = _in_bwd([dq, df, di, dg, dp], dx1, x, gmix, win_g, tm=512)
    keeps = keep_ff1 + keep_ff2 + keep_attn
    received = recv_ff1 + list(received)
    gs_in = [_halved(_tn_grad("dw_in", h, dz, d, win_g.shape[2], False, 1024, win_g.shape[2]))]
    got_in = _run_exchange("grad_pair_exchange_w_in", _PairExchange(gs_in), gs_in)
    keep_in, send_in = _pair_adds(("w_in",), gs_in, got_in, idx)

    partials = dict(zip(EARLY_NAMES, zip(keeps, received)))
    stats = dict(mix=st_mix, x=st_x, mem=st_mem, ffn=st_ffn, loss=st_loss, hgrn=st_hgrn, pool=st_pool)
    return grad_x, stats, d_wp, partials, keep_in, send_in


def _place():
    x, y, c = lax.axis_index("x"), lax.axis_index("y"), lax.axis_index("c")
    return x, y, c, [(x, 1 - y), (1 - x, y), (1 - x, 1 - y)]


def _rcopy(src, dst, ssem, rsem, dev):
    return pltpu.make_async_remote_copy(src_ref=src, dst_ref=dst, send_sem=ssem, recv_sem=rsem,
                                        device_id=dev, device_id_type=MESH)


class _WeightGather:
    def __init__(self, shards):
        self.n = len(shards)
        self.rows = [w.shape[0] for w in shards]
        self.out_shape = [jax.ShapeDtypeStruct((N_CHIPS,) + w.shape, w.dtype) for w in shards]
        self.scratch_shapes = [pltpu.SemaphoreType.DMA((self.n,))] * 2 + [pltpu.SemaphoreType.DMA((self.n, 3))] * 4

    def _copies(self, ins, outs, sems, with_pass_on):
        lsem, lrsem, ssem, rsem, fsem, frsem = sems
        x, y, c, peers = _place()
        chip = 2 * x + y
        sib = (x, y, 1 - c)
        own = [_rcopy(ins[a], outs[a].at[chip], lsem.at[a], lrsem.at[a], sib) for a in range(self.n)]
        sends, arrived, passed, passed_in = [], [], [], []
        for a in range(self.n):
            hr = self.rows[a] // 2
            half = lambda who, hc, a=a, hr=hr: outs[a].at[who, pl.ds(hc * hr, hr), :]
            for r, (px, py) in enumerate(peers):
                pc = 2 * px + py
                sends.append(_rcopy(ins[a].at[pl.ds(c * hr, hr), :], half(chip, c), ssem.at[a, r], rsem.at[a, r],
                                    (px, py, c)))
                if with_pass_on:
                    arrived.append(_rcopy(half(pc, c), half(pc, c), ssem.at[a, r], rsem.at[a, r], (px, py, c)))
                    passed.append(_rcopy(half(pc, c), half(pc, c), fsem.at[a, r], frsem.at[a, r], sib))
                    passed_in.append(_rcopy(half(pc, 1 - c), half(pc, 1 - c), fsem.at[a, r], frsem.at[a, r], sib))
        return own, sends, arrived, passed, passed_in

    def start(self, ins, outs, sems):
        own, sends, _, _, _ = self._copies(ins, outs, sems, False)
        for cp in own + sends:
            cp.start()

    def finish(self, ins, outs, sems):
        own, sends, arrived, passed, passed_in = self._copies(ins, outs, sems, True)
        for got, fwd in zip(arrived, passed):
            got.wait_recv()
            fwd.start()
        for cp in passed_in:
            cp.wait_recv()
        for cp in sends + passed:
            cp.wait_send()
        for cp in own:
            cp.wait()


class _ChipExchange:
    def __init__(self, sends):
        self.n = len(sends)
        self.out_shape = [jax.ShapeDtypeStruct(g.shape, g.dtype) for g in sends]
        self.scratch_shapes = [pltpu.SemaphoreType.DMA((self.n, 3))] * 2

    def _copies(self, ins, outs, sems):
        ssem, rsem = sems
        _, _, c, peers = _place()
        return [_rcopy(ins[a].at[r], outs[a].at[r], ssem.at[a, r], rsem.at[a, r], (px, py, c))
                for a in range(self.n) for r, (px, py) in enumerate(peers)]

    def start(self, ins, outs, sems):
        for cp in self._copies(ins, outs, sems):
            cp.start()

    def finish(self, ins, outs, sems):
        for cp in self._copies(ins, outs, sems):
            cp.wait()


class _Plans:
    def __init__(self, plans):
        self.plans = plans
        self.n = sum(p.n for p in plans)
        self.out_shape = [s for p in plans for s in p.out_shape]
        self.scratch_shapes = [s for p in plans for s in p.scratch_shapes]

    def _each(self, ins, outs, sems):
        a = b = 0
        for p in self.plans:
            ns = len(p.scratch_shapes)
            yield p, ins[a:a + p.n], outs[a:a + p.n], sems[b:b + ns]
            a, b = a + p.n, b + ns

    def start(self, ins, outs, sems):
        for p, i, o, s in self._each(ins, outs, sems):
            p.start(i, o, s)

    def finish(self, ins, outs, sems):
        for p, i, o, s in self._each(ins, outs, sems):
            p.finish(i, o, s)


def _run_exchange(name, plan, arrays):
    n = plan.n

    def body(*refs):
        ins, outs, sems = refs[:n], refs[n:2 * n], refs[2 * n:]
        plan.start(ins, outs, sems)
        plan.finish(ins, outs, sems)

    return pl.pallas_call(
        body, name=name, in_specs=[ANY] * n, out_specs=[ANY] * n,
        out_shape=plan.out_shape, scratch_shapes=plan.scratch_shapes,
    )(*arrays)


class _PairExchange:
    def __init__(self, gs):
        self.n = len(gs)
        self.out_shape = [jax.ShapeDtypeStruct((g.shape[0],) + g.shape[2:], g.dtype) for g in gs]
        self.scratch_shapes = [pltpu.SemaphoreType.DMA((self.n,))] * 2

    def _copies(self, ins, outs, sems):
        ssem, rsem = sems
        x, y, c, _ = _place()
        return [_rcopy(ins[a].at[:, 1 - c], outs[a], ssem.at[a], rsem.at[a], (x, y, 1 - c)) for a in range(self.n)]

    def start(self, ins, outs, sems):
        for cp in self._copies(ins, outs, sems):
            cp.start()

    def finish(self, ins, outs, sems):
        for cp in self._copies(ins, outs, sems):
            cp.wait()


def _grad_pair_add(name, g, got, idx, tr):
    _, _, hr, cc = g.shape

    def body(idx_ref, g0, g1, g2, g3, r0, r1, r2, r3, keep_ref, send_ref):
        keep_ref[...] = g0[...] + r0[...]
        for q, (gq, rq) in enumerate(((g1, r1), (g2, r2), (g3, r3))):
            send_ref[q] = (gq[...] + rq[...]).astype(BF16)

    gspec = lambda q: pl.BlockSpec((None, None, tr, cc), lambda i, idx: (idx[1 + q], idx[0], i, 0))
    rspec = lambda q: pl.BlockSpec((None, tr, cc), lambda i, idx: (idx[1 + q], i, 0))
    return pl.pallas_call(
        body, name=name,
        grid_spec=pltpu.PrefetchScalarGridSpec(
            num_scalar_prefetch=1, grid=(hr // tr,),
            in_specs=[gspec(q) for q in range(4)] + [rspec(q) for q in range(4)],
            out_specs=[pl.BlockSpec((tr, cc), lambda i, idx: (i, 0)), pl.BlockSpec((3, tr, cc), lambda i, idx: (0, i, 0))]),
        out_shape=[jax.ShapeDtypeStruct((hr, cc), F32), jax.ShapeDtypeStruct((3, hr, cc), BF16)],
        compiler_params=_params(("parallel",)),
    )(idx, g, g, g, g, got, got, got, got)


def _grad_chip_add(name, keep, got, tr):
    hr, cc = keep.shape

    def body(k_ref, g_ref, o_ref):
        o_ref[...] = ((k_ref[...] + g_ref[0].astype(F32)) + g_ref[1].astype(F32)) + g_ref[2].astype(F32)

    return pl.pallas_call(
        body, name=name, grid=(hr // tr,),
        in_specs=[pl.BlockSpec((tr, cc), lambda i: (i, 0)), pl.BlockSpec((3, tr, cc), lambda i: (0, i, 0))],
        out_specs=pl.BlockSpec((tr, cc), lambda i: (i, 0)),
        out_shape=jax.ShapeDtypeStruct((hr, cc), F32),
        compiler_params=_params(("parallel",)),
    )(keep, got)


class _HalfExchange:
    def __init__(self, ts):
        self.n = len(ts)
        self.out_shape = [jax.ShapeDtypeStruct(t.shape, t.dtype) for t in ts]
        self.scratch_shapes = [pltpu.SemaphoreType.DMA((self.n,))] * 2

    def _copies(self, ins, outs, sems):
        ssem, rsem = sems
        x, y, c, _ = _place()
        return [_rcopy(ins[a], outs[a], ssem.at[a], rsem.at[a], (x, y, 1 - c)) for a in range(self.n)]

    def start(self, ins, outs, sems):
        for cp in self._copies(ins, outs, sems):
            cp.start()

    def finish(self, ins, outs, sems):
        for cp in self._copies(ins, outs, sems):
            cp.wait()


def _small_allreduce(stats, d_wp, plan, plan_args):
    d = D_MODEL
    half = d // 2
    wps = d_wp.shape
    n = plan.n

    def body(mix_ref, x_ref, mem_ref, ffn_ref, loss_ref, hg_ref, pool_ref, wp_ref, *refs):
        cin, (slab_out, wp_out), cout = refs[:n], refs[n:n + 2], refs[n + 2:2 * n + 2]
        slab_buf, wp_buf, sib_s, sib_w, ssem, rsem = refs[2 * n + 2:2 * n + 8]
        csem = refs[2 * n + 8:]
        plan.start(cin, cout, csem)
        x, y, c, peers = _place()
        chip = 2 * x + y
        sib = (x, y, 1 - c)
        hgn = jnp.concatenate([hg_ref[h, 0:1, :] for h in range(HGRN_HEADS)], axis=1)
        dlb = jnp.concatenate([hg_ref[h, 1:2, :] for h in range(HGRN_HEADS)], axis=1)
        slab_buf[0] = jnp.concatenate([
            mix_ref[0:1, :], x_ref[0:1, :], mem_ref[0:1, :], ffn_ref[0:1, :], loss_ref[0:1, :],
            jnp.concatenate([dlb, hgn], axis=1),
            jnp.concatenate([pool_ref[0:1, :], jnp.zeros((1, half), F32)], axis=1),
            loss_ref[1:2, :]], axis=0)
        wp_buf[0] = wp_ref[...]
        pair = [_rcopy(slab_buf.at[0], sib_s, ssem.at[0], rsem.at[0], sib),
                _rcopy(wp_buf.at[0], sib_w, ssem.at[1], rsem.at[1], sib)]
        for cp in pair:
            cp.start()
        for cp in pair:
            cp.wait()
        slab_buf[0] = slab_buf[0] + sib_s[...]
        wp_buf[0] = wp_buf[0] + sib_w[...]
        cps = []
        for r, (px, py) in enumerate(peers):
            cps.append(_rcopy(slab_buf.at[0], slab_buf.at[r + 1], ssem.at[2 + 2 * r], rsem.at[2 + 2 * r], (px, py, c)))
            cps.append(_rcopy(wp_buf.at[0], wp_buf.at[r + 1], ssem.at[3 + 2 * r], rsem.at[3 + 2 * r], (px, py, c)))
        for cp in cps:
            cp.start()
        for cp in cps:
            cp.wait()
        tot_s, tot_w = slab_buf[chip], wp_buf[chip]
        for j in range(1, N_CHIPS):
            tot_s = tot_s + slab_buf[jnp.bitwise_xor(j, chip)]
            tot_w = tot_w + wp_buf[jnp.bitwise_xor(j, chip)]
        slab_out[...] = tot_s
        wp_out[...] = tot_w
        plan.finish(cin, cout, csem)

    return pl.pallas_call(
        body, name="small_allreduce",
        in_specs=[VMEM] * 8 + [ANY] * n, out_specs=[VMEM] * 2 + [ANY] * n,
        out_shape=[jax.ShapeDtypeStruct((8, d), F32), jax.ShapeDtypeStruct(wps, F32)] + list(plan.out_shape),
        scratch_shapes=[pltpu.VMEM((N_CHIPS, 8, d), F32), pltpu.VMEM((N_CHIPS,) + wps, F32),
                        pltpu.VMEM((8, d), F32), pltpu.VMEM(wps, F32),
                        pltpu.SemaphoreType.DMA((8,)), pltpu.SemaphoreType.DMA((8,))] + list(plan.scratch_shapes),
    )(stats["mix"], stats["x"], stats["mem"], stats["ffn"], stats["loss"], stats["hgrn"], stats["pool"], d_wp,
      *plan_args)


def _adamw_math(w, g, m, v):
    m = ADAM_B1 * m + (1.0 - ADAM_B1) * g
    v = ADAM_B2 * v + (1.0 - ADAM_B2) * (g * g)
    m_hat = m / (1.0 - ADAM_B1 ** ADAM_STEP)
    v_hat = v / (1.0 - ADAM_B2 ** ADAM_STEP)
    delta = -ADAM_LR * (m_hat / (jnp.sqrt(v_hat) + ADAM_EPS) + ADAM_WD * w)
    return delta, m, v


def _adamw(name, mine, theirs, w, m, v, idx, tr):
    rows = w.shape[0]
    cc = mine.shape[1]
    nb = rows // 2 // tr
    heads = w.shape[1] if w.ndim == 3 else 1
    e = cc // heads

    def body(idx_ref, a_ref, b_ref, w_ref, m_ref, v_ref, g_out, d_out, m_out, v_out):
        g = jnp.where(pl.program_id(0) // nb == idx_ref[0], a_ref[...], b_ref[...])
        if w.ndim == 2:
            g_out[...] = g
            d_out[...], m_out[...], v_out[...] = _adamw_math(w_ref[...], g, m_ref[...], v_ref[...])
        else:
            for h in range(heads):
                gh = g[:, h * e:(h + 1) * e]
                g_out[:, h, :] = gh
                d_out[:, h, :], m_out[:, h, :], v_out[:, h, :] = _adamw_math(
                    w_ref[:, h, :], gh, m_ref[:, h, :], v_ref[:, h, :])

    hspec = pl.BlockSpec((tr, cc), lambda i, idx: (i % nb, 0))
    spec = pl.BlockSpec((tr,) + w.shape[1:], lambda i, idx: (i,) + (0,) * (w.ndim - 1))
    return pl.pallas_call(
        body, name=name,
        grid_spec=pltpu.PrefetchScalarGridSpec(
            num_scalar_prefetch=1, grid=(rows // tr,),
            in_specs=[hspec, hspec, spec, spec, spec], out_specs=[spec] * 4),
        out_shape=[jax.ShapeDtypeStruct(w.shape, F32)] * 4,
        compiler_params=_params(("parallel",)),
    )(idx, mine, theirs, w, m, v)


SMALL_NAMES = ("norm_mix_g", "lb_logits", "hgrn_norm_g", "w_pool", "pool_scale", "norm_x_g", "norm_mem_g",
               "norm_ffn_g", "final_norm_g")


def _small_update(slab, d_wp, ws, ms, vs):
    n = len(SMALL_NAMES)
    half = D_MODEL // 2

    def body(slab_ref, wp_ref, *refs):
        w_refs, m_refs, v_refs, outs = refs[:n], refs[n:2 * n], refs[2 * n:3 * n], refs[3 * n:]
        row = lambda k: slab_ref[k:k + 1, :]
        lbl = w_refs[SMALL_NAMES.index("lb_logits")][...]
        s0 = _lower_bound(lbl[0:1, :], lbl[1:2, :])
        dl0 = row(ROW_LB_HGN)[:, :half] * s0 * (1.0 - s0)
        grads = dict(norm_mix_g=row(ROW_GMIX), lb_logits=jnp.concatenate([dl0, -dl0], axis=0),
                     hgrn_norm_g=row(ROW_LB_HGN)[:, half:], w_pool=wp_ref[...], pool_scale=row(ROW_PSCALE)[:, :half],
                     norm_x_g=row(ROW_GX), norm_mem_g=row(ROW_GMEM), norm_ffn_g=row(ROW_GFFN),
                     final_norm_g=row(ROW_GFIN))
        outs[0][...] = row(ROW_LOSS)[:, :128]
        for i, name in enumerate(SMALL_NAMES):
            g = grads[name]
            delta, m2, v2 = _adamw_math(w_refs[i][...], g, m_refs[i][...], v_refs[i][...])
            for o, val in zip(outs[1 + 4 * i:5 + 4 * i], (g, delta, m2, v2)):
                o[...] = val

    args = [ws[k] for k in SMALL_NAMES] + [ms[k] for k in SMALL_NAMES] + [vs[k] for k in SMALL_NAMES]
    out_shape = [jax.ShapeDtypeStruct((1, 128), F32)]
    for k in SMALL_NAMES:
        out_shape += [jax.ShapeDtypeStruct(ws[k].shape, F32)] * 4
    res = pl.pallas_call(
        body, name="small_update",
        in_specs=[VMEM] * (2 + 3 * n), out_specs=[VMEM] * len(out_shape), out_shape=out_shape,
    )(slab, d_wp, *args)
    return res[0], {k: res[1 + 4 * i:5 + 4 * i] for i, k in enumerate(SMALL_NAMES)}


ALL_NAMES = ("norm_mix_g", "w_in", "lb_logits", "hgrn_norm_g", "w_pool", "pool_scale", "w_out", "norm_x_g",
             "norm_mem_g", "w_xq", "w_xk", "w_xv", "w_xo", "norm_ffn_g", "w_ff1", "w_ff2", "final_norm_g")


def _shard_2d(name, a):
    a = a[0]
    if name in ("w_xq", "w_xk", "w_xv"):
        return a.reshape(a.shape[0], -1)
    if name == "w_xo":
        return a.reshape(-1, a.shape[-1])
    return a


def _small_2d(name, a):
    if name == "w_pool":
        return a.reshape(-1, HEAD_DIM)
    if name == "lb_logits":
        return a
    return a.reshape(1, -1)


def kernel(x, mem, norm_mix_g, w_in, lb_logits, hgrn_norm_g, w_pool, pool_scale, w_out, norm_x_g, norm_mem_g, w_xq, w_xk, w_xv, w_xo, norm_ffn_g, w_ff1, w_ff2, final_norm_g, loss_target, m_norm_mix_g, m_w_in, m_lb_logits, m_hgrn_norm_g, m_w_pool, m_pool_scale, m_w_out, m_norm_x_g, m_norm_mem_g, m_w_xq, m_w_xk, m_w_xv, m_w_xo, m_norm_ffn_g, m_w_ff1, m_w_ff2, m_final_norm_g, v_norm_mix_g, v_w_in, v_lb_logits, v_hgrn_norm_g, v_w_pool, v_pool_scale, v_w_out, v_norm_x_g, v_norm_mem_g, v_w_xq, v_w_xk, v_w_xv, v_w_xo, v_norm_ffn_g, v_w_ff1, v_w_ff2, v_final_norm_g):
    w = dict(norm_mix_g=norm_mix_g, w_in=w_in, lb_logits=lb_logits, hgrn_norm_g=hgrn_norm_g, w_pool=w_pool, pool_scale=pool_scale, w_out=w_out, norm_x_g=norm_x_g, norm_mem_g=norm_mem_g, w_xq=w_xq, w_xk=w_xk, w_xv=w_xv, w_xo=w_xo, norm_ffn_g=norm_ffn_g, w_ff1=w_ff1, w_ff2=w_ff2, final_norm_g=final_norm_g)
    m = dict(norm_mix_g=m_norm_mix_g, w_in=m_w_in, lb_logits=m_lb_logits, hgrn_norm_g=m_hgrn_norm_g, w_pool=m_w_pool, pool_scale=m_pool_scale, w_out=m_w_out, norm_x_g=m_norm_x_g, norm_mem_g=m_norm_mem_g, w_xq=m_w_xq, w_xk=m_w_xk, w_xv=m_w_xv, w_xo=m_w_xo, norm_ffn_g=m_norm_ffn_g, w_ff1=m_w_ff1, w_ff2=m_w_ff2, final_norm_g=m_final_norm_g)
    v = dict(norm_mix_g=v_norm_mix_g, w_in=v_w_in, lb_logits=v_lb_logits, hgrn_norm_g=v_hgrn_norm_g, w_pool=v_w_pool, pool_scale=v_pool_scale, w_out=v_w_out, norm_x_g=v_norm_x_g, norm_mem_g=v_norm_mem_g, w_xq=v_w_xq, w_xk=v_w_xk, w_xv=v_w_xv, w_xo=v_w_xo, norm_ffn_g=v_norm_ffn_g, w_ff1=v_w_ff1, w_ff2=v_w_ff2, final_norm_g=v_final_norm_g)

    big_w = {k: _shard_2d(k, w[k]) for k in BIG_NAMES}
    slab_oq = jnp.concatenate([big_w["w_out"], big_w["w_xq"]], axis=0).astype(BF16)
    slab_kv = jnp.concatenate([big_w["w_xk"], big_w["w_xv"]], axis=0).astype(BF16)
    shards = dict(slab_oq=slab_oq, slab_kv=slab_kv,
                  **{k: big_w[k].astype(BF16) for k in ("w_in", "w_xo", "w_ff1", "w_ff2")})

    cx, cy, cc = lax.axis_index("x"), lax.axis_index("y"), lax.axis_index("c")
    chip = 2 * cx + cy
    idx = jnp.stack([cc, chip, chip ^ 1, chip ^ 2, chip ^ 3]).astype(jnp.int32)
    small = {k: w[k] for k in SMALL_NAMES}
    grad_x, stats, d_wp, partials, keep_in, send_in = _step(x[0], mem[0], loss_target[0], small, shards, idx)

    chip_add = lambda k, keep, got: _grad_chip_add("grad_chip_add_" + k, keep, got, tr=min(256, keep.shape[0]))
    halves = {k: chip_add(k, *partials[k]) for k in EARLY_NAMES}
    early = [halves[k] for k in EARLY_NAMES]
    slab_sum, wp_sum, recv_in, *their_early = _small_allreduce(
        stats, d_wp.reshape(-1, HEAD_DIM), _Plans([_ChipExchange(send_in), _HalfExchange(early)]), send_in + early)
    halves["w_in"] = chip_add("w_in", keep_in[0], recv_in)
    theirs = dict(zip(EARLY_NAMES, their_early))
    (theirs["w_in"],) = _run_exchange("grad_half_exchange_w_in", _HalfExchange([halves["w_in"]]), [halves["w_in"]])

    grads, deltas, new_m, new_v = {}, {}, {}, {}
    for k in BIG_NAMES:
        as_held = (lambda a: a[0]) if k in ("w_xq", "w_xk", "w_xv") else functools.partial(_shard_2d, k)
        res = _adamw("adamw_" + k, halves[k], theirs[k], as_held(w[k]), as_held(m[k]), as_held(v[k]), idx,
                     tr=min(256, halves[k].shape[0]))
        for store, val in zip((grads, deltas, new_m, new_v), res):
            store[k] = val.reshape(w[k].shape)

    loss, upd = _small_update(slab_sum, wp_sum, {k: _small_2d(k, w[k]) for k in SMALL_NAMES},
                              {k: _small_2d(k, m[k]) for k in SMALL_NAMES}, {k: _small_2d(k, v[k]) for k in SMALL_NAMES})
    for k in SMALL_NAMES:
        for store, val in zip((grads, deltas, new_m, new_v), upd[k]):
            store[k] = val.reshape(w[k].shape)

    return (loss[0, 0], grad_x[None], *[grads[k] for k in ALL_NAMES], *[deltas[k] for k in ALL_NAMES],
            *[new_m[k] for k in ALL_NAMES], *[new_v[k] for k in ALL_NAMES])
```

```python
import functools

import jax
import jax.numpy as jnp
from jax import lax
from jax.experimental import pallas as pl
from jax.experimental.pallas import tpu as pltpu

F32 = jnp.float32
BF16 = jnp.bfloat16
LOG2E = 1.4426950408889634
NEG_BIG = -1e30
MAX_LOG2_GROWTH = 100.0
MESH = pl.DeviceIdType.MESH
ANY = pl.BlockSpec(memory_space=pl.ANY)
VMEM = pl.BlockSpec(memory_space=pltpu.VMEM)

D_MODEL = 1024
N_CHIPS = 4
HGRN_HEADS = 4
HEAD_DIM = 128
HGRN_WIDTH = HGRN_HEADS * HEAD_DIM
POOL_WINDOWS = (2, 4, 8, 16)
POOL_HALO = 16
SUB = 16
HALF = SUB // 2
CHUNK = 64
HEADS_PER_STEP = 4
XATTN_HEADS = 4
XATTN_HEAD_DIM = 256
EPS = 1e-6
ADAM_LR, ADAM_B1, ADAM_B2, ADAM_EPS, ADAM_WD, ADAM_STEP = 0.001, 0.9, 0.999, 1e-08, 0.01, 10

V7X_VMEM_BYTES = 64 * 1024 * 1024
VMEM_LIMIT = V7X_VMEM_BYTES - 8 * 1024 * 1024

NN = (((1,), (0,)), ((), ()))
NT = (((1,), (1,)), ((), ()))
TN = (((0,), (0,)), ((), ()))

ROW_GMIX, ROW_GX, ROW_GMEM, ROW_GFFN, ROW_GFIN, ROW_LB_HGN, ROW_PSCALE, ROW_LOSS = range(8)


def _dot(a, b, dims=NN):
    return lax.dot_general(a, b, dims, preferred_element_type=F32)


def _sigmoid(x):
    return 1.0 / (1.0 + jnp.exp(-x))


def _rms_fwd(x, g):
    r = lax.rsqrt(jnp.mean(x * x, axis=-1, keepdims=True) + EPS)
    n = x * r
    return n * g, n, r


def _rms_bwd(dh, n, r, g):
    dn = dh * g
    dx = r * (dn - n * jnp.mean(dn * n, axis=-1, keepdims=True))
    return dx, jnp.sum(dh * n, axis=0, keepdims=True)


def _params(sem=None):
    return pltpu.CompilerParams(dimension_semantics=sem, vmem_limit_bytes=VMEM_LIMIT)


def _const(shape):
    nd = len(shape)
    return pl.BlockSpec(shape, lambda *_: (0,) * nd, pipeline_mode=pl.Buffered(1))


def _const_out(shape):
    nd = len(shape)
    return pl.BlockSpec(shape, lambda *_: (0,) * nd)


def _acc_rows(ref, t, rows):
    upd = jnp.concatenate(rows + [jnp.zeros((8 - len(rows), rows[0].shape[1]), F32)], axis=0)

    @pl.when(t == 0)
    def _():
        ref[...] = upd

    @pl.when(t > 0)
    def _():
        ref[...] = ref[...] + upd


def _fuse_exchange(body, n_in, n_out, n_scratch, plan, ndim):
    if plan is None:
        return body
    n = plan.n

    def wrapped(*refs):
        ins, cin = refs[:n_in], refs[n_in:n_in + n]
        outs, cout = refs[n_in + n:n_in + n + n_out], refs[n_in + n + n_out:n_in + 2 * n + n_out]
        rest = refs[n_in + 2 * n + n_out:]
        scr, csem = rest[:n_scratch], rest[n_scratch:]
        first = pl.program_id(0) == 0
        last = pl.program_id(0) == pl.num_programs(0) - 1
        for i in range(1, ndim):
            first = first & (pl.program_id(i) == 0)
            last = last & (pl.program_id(i) == pl.num_programs(i) - 1)

        @pl.when(first)
        def _():
            plan.start(cin, cout, csem)

        body(*ins, *outs, *scr)

        @pl.when(last)
        def _():
            plan.finish(cin, cout, csem)

    return wrapped


def _plan_extras(plan):
    if plan is None:
        return [], [], []
    return [ANY] * plan.n, list(plan.out_shape), list(plan.scratch_shapes)


def _in_proj(x, g, win_g, tm, plan=None, plan_args=()):
    s, d = x.shape
    nsh, _, wc = win_g.shape

    def body(x_ref, g_ref, w_ref, z_ref, h_ref):
        h, _, _ = _rms_fwd(x_ref[...], g_ref[...])
        hb = h.astype(BF16)
        h_ref[...] = hb
        for j in range(nsh):
            z_ref[:, j * wc:(j + 1) * wc] = _dot(hb, w_ref[j])

    x_specs, x_shapes, x_scratch = _plan_extras(plan)
    return pl.pallas_call(
        _fuse_exchange(body, 3, 2, 0, plan, 1), name="in_proj", grid=(s // tm,),
        in_specs=[pl.BlockSpec((tm, d), lambda t: (t, 0)), _const((1, d)), _const((nsh, d, wc))] + x_specs,
        out_specs=[pl.BlockSpec((tm, nsh * wc), lambda t: (t, 0)), pl.BlockSpec((tm, d), lambda t: (t, 0))] + x_specs,
        out_shape=[jax.ShapeDtypeStruct((s, nsh * wc), F32), jax.ShapeDtypeStruct((s, d), BF16)] + x_shapes,
        scratch_shapes=x_scratch,
        compiler_params=_params(("arbitrary",)),
    )(x, g, win_g, *plan_args)


def _lower_bound(l0, l1):
    m = jnp.maximum(l0, l1)
    e0, e1 = jnp.exp(l0 - m), jnp.exp(l1 - m)
    return e0 / (e0 + e1)


def _block_tri(n, group, upper):
    r = lax.broadcasted_iota(jnp.int32, (n, n), 0)
    c = lax.broadcasted_iota(jnp.int32, (n, n), 1)
    keep = (r // group == c // group) & ((c >= r) if upper else (c <= r))
    return keep.astype(BF16)


def _group_cumsum(tri, x):
    hi = x.astype(BF16)
    rest = x - hi.astype(F32)
    mid = rest.astype(BF16)
    lo = (rest - mid.astype(F32)).astype(BF16)
    return (_dot(tri, hi) + _dot(tri, mid)) + _dot(tri, lo)


def _decay(b, bj, rows, first):
    d = b - bj
    if first:
        d = jnp.where(rows >= first, d, NEG_BIG)
    return jnp.exp2(d)


class _RowSums:
    ORDER = (0, 4, 2, 6, 1, 5, 3, 7)

    def __init__(self, rows):
        self.rows = rows
        self.level = {4: {}, 2: {}, 1: {}}

    def _pair(self, p, q, d):
        return jnp.where((self.rows & d) != 0, p + pltpu.roll(p, d, axis=0), q + pltpu.roll(q, HALF - d, axis=0))

    def push(self, j, y, d=4):
        if d == 0:
            self.out = y
            return
        slot = self.level[d]
        key = j % d
        if key not in slot:
            slot[key] = (j, y)
            return
        j0, y0 = slot.pop(key)
        p, q = (y, y0) if j & d else (y0, y)
        self.push(key, self._pair(p, q, d), d // 2)

    def result(self):
        return self.out


def _hgrn_gates(qp, fp, lb):
    sq = _sigmoid(qp)
    sf = _sigmoid(fp)
    f = lb + (1.0 - lb) * sf
    return qp * sq, sq, f, sf


def _hgrn_fwd(z, l0, l1, gn, tc, unroll=1, plan=None, plan_args=()):
    s = z.shape[0]
    nsub = tc // SUB
    hd = HEAD_DIM
    hp = HEADS_PER_STEP

    def body(q_ref, f_ref, v_ref, g_ref, l0_ref, l1_ref, gn_ref, tri_ref, tric_ref, o_ref, oa_ref, st_ref,
             state, qs, ks, bs, os_):
        @pl.when(pl.program_id(1) == 0)
        def _():
            state[...] = jnp.zeros_like(state)

        cols = [slice(hh * hd, (hh + 1) * hd) for hh in range(hp)]
        q, k, lf, bc = [], [], [], []
        for hh, cs in enumerate(cols):
            qh, _, fh, _ = _hgrn_gates(q_ref[:, cs], f_ref[:, cs], _lower_bound(l0_ref[hh], l1_ref[hh]))
            q.append(qh)
            k.append(1.0 - fh)
            lf.append(jnp.log(fh) * LOG2E)
            bc.append(_group_cumsum(tric_ref[...], lf[hh]))
        bounded = functools.reduce(jnp.minimum, [jnp.min(b) for b in bc]) >= -MAX_LOG2_GROWTH

        @pl.when(bounded)
        def _():
            mask = tric_ref[...] > 0
            for hh, cs in enumerate(cols):
                qt = (q[hh] * jnp.exp2(bc[hh])).astype(BF16)
                ki = (k[hh] * jnp.exp2(-bc[hh])).astype(BF16)
                vb = v_ref[:, cs].astype(BF16)
                a = jnp.where(mask, _dot(qt, ki, NT), 0.0).astype(BF16)
                o_in = _dot(a, vb)
                for c in range(tc // CHUNK):
                    rs = slice(c * CHUNK, (c + 1) * CHUNK)
                    st = state[hh]
                    st_ref[hh, c] = st
                    os_[rs, cs] = o_in[rs] + _dot(qt[rs], st.astype(BF16), NT)
                    bl = bc[hh][(c + 1) * CHUNK - 1:(c + 1) * CHUNK, :]
                    kt = (k[hh][rs] * jnp.exp2(bl - bc[hh][rs])).astype(BF16)
                    state[hh] = st * jnp.exp2(bl) + _dot(vb[rs], kt, TN)

        @pl.when(jnp.logical_not(bounded))
        def _():
            rows = lax.broadcasted_iota(jnp.int32, (HALF, 1), 0)
            for hh, cs in enumerate(cols):
                qs[:, cs] = q[hh]
                ks[:, cs] = k[hh]
                bs[:, cs] = _group_cumsum(tri_ref[...], lf[hh])

                def step(i, carry, hh=hh, cs=cs):
                    r0 = pl.multiple_of(i * SUB, SUB)
                    q_ = qs[pl.ds(r0, SUB), cs]
                    k_ = ks[pl.ds(r0, SUB), cs]
                    b_ = bs[pl.ds(r0, SUB), cs]
                    v_ = v_ref[pl.ds(r0, SUB), cs]
                    st = state[hh]

                    @pl.when(i % (CHUNK // SUB) == 0)
                    def _():
                        st_ref[hh, i // (CHUNK // SUB)] = st

                    bl = b_[SUB - 1:SUB, :]
                    o = _dot((q_ * jnp.exp2(b_)).astype(BF16), st.astype(BF16), NT)
                    (q_lo, q_hi), (b_lo, b_hi), (o_lo, o_hi) = ((x[:HALF], x[HALF:]) for x in (q_, b_, o))
                    for j in range(SUB):
                        bj, kj, vj = b_[j:j + 1, :], k_[j:j + 1, :], v_[j:j + 1, :]
                        if j < HALF:
                            e = _decay(b_lo, bj, rows, j)
                            o_lo = o_lo + jnp.sum(q_lo * e * kj, axis=-1, keepdims=True) * vj
                        e = _decay(b_hi, bj, rows, j - HALF if j > HALF else None)
                        o_hi = o_hi + jnp.sum(q_hi * e * kj, axis=-1, keepdims=True) * vj
                    os_[pl.ds(r0, HALF), cs] = o_lo
                    os_[pl.ds(r0 + HALF, HALF), cs] = o_hi
                    kt = (k_ * jnp.exp2(bl - b_)).astype(BF16)
                    state[hh] = st * jnp.exp2(bl) + _dot(v_.astype(BF16), kt, TN)
                    return carry

                lax.fori_loop(0, nsub, step, 0, unroll=unroll)

        for hh, cs in enumerate(cols):
            o = os_[:, cs]
            o_ref[:, cs] = o
            r = lax.rsqrt(jnp.mean(o * o, axis=-1, keepdims=True) + EPS)
            gp = g_ref[:, cs]
            oa_ref[:, cs] = (o * r * gn_ref[hh] * (gp * _sigmoid(gp))).astype(BF16)

    ng = HGRN_HEADS // hp
    col = lambda k: pl.BlockSpec((tc, hp * hd), lambda h, t: (t, k * ng + h))
    vec = pl.BlockSpec((hp, 1, hd), lambda h, t: (h, 0, 0))
    out = pl.BlockSpec((tc, hp * hd), lambda h, t: (t, h))
    x_specs, x_shapes, x_scratch = _plan_extras(plan)
    return pl.pallas_call(
        _fuse_exchange(body, 9, 3, 5, plan, 2), name="hgrn_fwd", grid=(ng, s // tc),
        in_specs=[col(0), col(1), col(2), col(3), vec, vec, vec, _const((tc, tc)), _const((tc, tc))] + x_specs,
        out_specs=[out, out, pl.BlockSpec((hp, tc // CHUNK, hd, hd), lambda h, t: (h, t, 0, 0))] + x_specs,
        out_shape=[jax.ShapeDtypeStruct((s, HGRN_WIDTH), F32), jax.ShapeDtypeStruct((s, HGRN_WIDTH), BF16),
                   jax.ShapeDtypeStruct((HGRN_HEADS, s // CHUNK, hd, hd), F32)] + x_shapes,
        scratch_shapes=[pltpu.VMEM((hp, hd, hd), F32)] + [pltpu.VMEM((tc, hp * hd), F32)] * 4 + x_scratch,
        compiler_params=_params(("arbitrary", "arbitrary")),
    )(z, z, z, z, l0, l1, gn, _block_tri(tc, SUB, False), _block_tri(tc, CHUNK, False), *plan_args)


def _pooled(p, ext, tok0):
    tm = p.shape[0]
    tok = tok0 + lax.broadcasted_iota(jnp.int32, (tm, 1), 0)
    outs = []
    for g, w in enumerate(POOL_WINDOWS):
        acc = ext[:, g * HEAD_DIM:(g + 1) * HEAD_DIM]
        sh = 1
        while sh < w:
            acc = acc + pltpu.roll(acc, sh, axis=0)
            sh *= 2
        cnt = jnp.minimum(tok + 1, w).astype(F32)
        outs.append(acc[POOL_HALO:, :] / cnt - p[:, g * HEAD_DIM:(g + 1) * HEAD_DIM])
    return outs


def _pool_fwd(z, wp, scale, tm):
    s = z.shape[0]
    pw = len(POOL_WINDOWS) * HEAD_DIM
    nb = tm // POOL_HALO

    def body(p_ref, prev_ref, wp_ref, sc_ref, ob_ref):
        t = pl.program_id(0)
        p = p_ref[...]
        prev = jnp.where(t > 0, prev_ref[...], 0.0)
        pooled = _pooled(p, jnp.concatenate([prev, p], axis=0), t * tm)
        ys = [_dot(pooled[g].astype(BF16), wp_ref[g].astype(BF16)) for g in range(len(POOL_WINDOWS))]
        ob_ref[...] = (jnp.concatenate(ys, axis=1) * sc_ref[...]).astype(BF16)

    return pl.pallas_call(
        body, name="pool_fwd", grid=(s // tm,),
        in_specs=[pl.BlockSpec((tm, pw), lambda t: (t, 4)),
                  pl.BlockSpec((POOL_HALO, pw), lambda t: (jnp.maximum(t * nb - 1, 0), 4)),
                  _const(wp.shape), _const((1, pw))],
        out_specs=pl.BlockSpec((tm, pw), lambda t: (t, 0)),
        out_shape=jax.ShapeDtypeStruct((s, pw), BF16),
        compiler_params=_params(("parallel",)),
    )(z, z, wp, scale)


def _kv_proj(mem, g, slab_g):
    m, d = mem.shape
    rows = d // N_CHIPS

    def body(mem_ref, g_ref, wk_ref, wv_ref, xk_ref, xv_ref):
        hm, _, _ = _rms_fwd(mem_ref[...], g_ref[...])
        hb = hm.astype(BF16)
        xk_ref[...] = _dot(hb, wk_ref[...].reshape(d, d)).astype(BF16)
        xv_ref[...] = _dot(hb, wv_ref[...].reshape(d, d)).astype(BF16)

    blk = lambda k: pl.BlockSpec((N_CHIPS, rows, d), lambda i: (0, k, 0))
    return pl.pallas_call(
        body, name="kv_proj", grid=(1,),
        in_specs=[_const((m, d)), _const((1, d)), blk(0), blk(1)],
        out_specs=[_const_out((m, d)), _const_out((m, d))],
        out_shape=[jax.ShapeDtypeStruct((m, d), BF16)] * 2,
        compiler_params=_params(("arbitrary",)),
    )(mem, g, slab_g, slab_g)


def _softmax_rows(sc):
    e = jnp.exp(sc - jnp.max(sc, axis=-1, keepdims=True))
    return e / jnp.sum(e, axis=-1, keepdims=True)


def _mix_xattn_fwd(x, oa, ob, gx, slab_g, wo_g, xk, xv, tm, plan=None, plan_args=()):
    s, d = x.shape
    m = xk.shape[0]
    rows = d // N_CHIPS
    hw = oa.shape[1]
    e = XATTN_HEAD_DIM

    def body(x_ref, oa_ref, ob_ref, gx_ref, wout_ref, wq_ref, wo_ref, xk_ref, xv_ref,
             x1_ref, mixed_ref, hq_ref, xq_ref, att_ref, x2_ref):
        mixed = jnp.concatenate([oa_ref[...], ob_ref[...]], axis=1)
        mixed_ref[...] = mixed
        x1 = x_ref[...] + _dot(mixed, wout_ref[...].reshape(d, d))
        x1_ref[...] = x1
        hq, _, _ = _rms_fwd(x1, gx_ref[...])
        hqb = hq.astype(BF16)
        hq_ref[...] = hqb
        xq = _dot(hqb, wq_ref[...].reshape(d, d)).astype(BF16)
        xq_ref[...] = xq
        atts = []
        for h in range(XATTN_HEADS):
            cs = slice(h * e, (h + 1) * e)
            p = _softmax_rows(_dot(xq[:, cs], xk_ref[:, cs], NT) * (e ** -0.5))
            atts.append(_dot(p.astype(BF16), xv_ref[:, cs]).astype(BF16))
        att = jnp.concatenate(atts, axis=1)
        att_ref[...] = att
        for j in range(N_CHIPS):
            x2_ref[:, j * rows:(j + 1) * rows] = x1[:, j * rows:(j + 1) * rows] + _dot(att, wo_ref[j])

    tile = lambda w: pl.BlockSpec((tm, w), lambda t: (t, 0))
    blk = lambda k: pl.BlockSpec((N_CHIPS, rows, d), lambda t: (0, k, 0), pipeline_mode=pl.Buffered(1))
    x_specs, x_shapes, x_scratch = _plan_extras(plan)
    return pl.pallas_call(
        _fuse_exchange(body, 9, 6, 0, plan, 1), name="mix_xattn_fwd", grid=(s // tm,),
        in_specs=[tile(d), tile(hw), tile(hw), _const((1, d)), blk(0), blk(1), _const(wo_g.shape),
                  _const((m, d)), _const((m, d))] + x_specs,
        out_specs=[tile(d)] * 6 + x_specs,
        out_shape=[jax.ShapeDtypeStruct((s, d), F32)] + [jax.ShapeDtypeStruct((s, d), BF16)] * 4
                  + [jax.ShapeDtypeStruct((s, d), F32)] + x_shapes,
        scratch_shapes=x_scratch,
        compiler_params=_params(("arbitrary",)),
    )(x, oa, ob, gx, slab_g, slab_g, wo_g, xk, xv, *plan_args)


def _mlp_loss_fwd(x2, gffn, gfin, w1_g, w2_g, target, tm):
    s, d = x2.shape
    wr = w1_g.shape[1]

    def body(x2_ref, gffn_ref, gfin_ref, w1_ref, w2_ref, tg_ref, a_ref, hf_ref, dx3_ref, dx3b_ref, st_ref):
        x2v = x2_ref[...]
        hf, _, _ = _rms_fwd(x2v, gffn_ref[...])
        hfb = hf.astype(BF16)
        hf_ref[...] = hfb
        acc = x2v
        for j in range(N_CHIPS):
            a = _dot(hfb, w1_ref[j])
            a_ref[:, j * wr:(j + 1) * wr] = a
            r = jnp.maximum(a, 0.0)
            acc = acc + _dot((r * r).astype(BF16), w2_ref[j])
        gf = gfin_ref[...]
        y, n, r3 = _rms_fwd(acc, gf)
        err = y - tg_ref[...]
        loss = 0.5 * jnp.sum(jnp.sum(err * err, axis=-1, keepdims=True) * (1.0 / d), axis=0, keepdims=True)
        dy = err * (1.0 / d)
        dx3, dgf = _rms_bwd(dy, n, r3, gf)
        dx3_ref[...] = dx3
        dx3b_ref[...] = dx3.astype(BF16)
        _acc_rows(st_ref, pl.program_id(0), [dgf, jnp.broadcast_to(loss, (1, d))])

    tile = lambda w: pl.BlockSpec((tm, w), lambda t: (t, 0))
    blk = lambda k: pl.BlockSpec((N_CHIPS, wr, d), lambda t: (0, k, 0), pipeline_mode=pl.Buffered(1))
    return pl.pallas_call(
        body, name="mlp_loss_fwd", grid=(s // tm,),
        in_specs=[tile(d), _const((1, d)), _const((1, d)), blk(0), blk(0), tile(d)],
        out_specs=[tile(N_CHIPS * wr), tile(d), tile(d), tile(d), _const_out((8, d))],
        out_shape=[jax.ShapeDtypeStruct((s, N_CHIPS * wr), F32), jax.ShapeDtypeStruct((s, d), BF16),
                   jax.ShapeDtypeStruct((s, d), F32), jax.ShapeDtypeStruct((s, d), BF16),
                   jax.ShapeDtypeStruct((8, d), F32)],
        compiler_params=_params(("arbitrary",)),
    )(x2, gffn, gfin, w1_g, w2_g, target)


def _mlp_bwd(dx3, dx3b, a, x2, gffn, w1_g, w2_g, tm):
    s, d = x2.shape
    wr = w1_g.shape[1]

    def body(dx3_ref, dx3b_ref, a_ref, x2_ref, g_ref, w1_ref, w2_ref, da_ref, u_ref, dx2_ref, dx2b_ref, st_ref):
        dyb = dx3b_ref[...]
        dhf = jnp.zeros((tm, d), F32)
        for j in range(N_CHIPS):
            r = jnp.maximum(a_ref[:, j * wr:(j + 1) * wr], 0.0)
            da = (_dot(dyb, w2_ref[j], NT) * (2.0 * r)).astype(BF16)
            da_ref[:, j * wr:(j + 1) * wr] = da
            u_ref[:, j * wr:(j + 1) * wr] = (r * r).astype(BF16)
            dhf = dhf + _dot(da, w1_ref[j], NT)
        g = g_ref[...]
        _, n, r2 = _rms_fwd(x2_ref[...], g)
        dxn, dg = _rms_bwd(dhf, n, r2, g)
        dx2 = dx3_ref[...] + dxn
        dx2_ref[...] = dx2
        dx2b_ref[...] = dx2.astype(BF16)
        _acc_rows(st_ref, pl.program_id(0), [dg])

    tile = lambda w: pl.BlockSpec((tm, w), lambda t: (t, 0))
    blk = lambda k: pl.BlockSpec((N_CHIPS, wr, d), lambda t: (0, k, 0), pipeline_mode=pl.Buffered(1))
    nf = N_CHIPS * wr
    return pl.pallas_call(
        body, name="mlp_bwd", grid=(s // tm,),
        in_specs=[tile(d), tile(d), tile(nf), tile(d), _const((1, d)), blk(0), blk(0)],
        out_specs=[tile(nf), tile(nf), tile(d), tile(d), _const_out((8, d))],
        out_shape=[jax.ShapeDtypeStruct((s, nf), BF16), jax.ShapeDtypeStruct((s, nf), BF16),
                   jax.ShapeDtypeStruct((s, d), F32), jax.ShapeDtypeStruct((s, d), BF16),
                   jax.ShapeDtypeStruct((8, d), F32)],
        compiler_params=_params(("arbitrary",)),
    )(dx3, dx3b, a, x2, gffn, w1_g, w2_g)


def _xattn_mix_bwd(dx2, x1, xq, xk, xv, gx, slab_g, wo_g, tm, plan=None, plan_args=()):
    s, d = x1.shape
    m = xk.shape[0]
    rows = d // N_CHIPS
    e = XATTN_HEAD_DIM

    def body(dx2_ref, x1_ref, xq_ref, xk_ref, xv_ref, gx_ref, wout_ref, wq_ref, wo_ref,
             dx1_ref, dx1b_ref, dxq_ref, dmix_ref, dxk_ref, dxv_ref, st_ref):
        t = pl.program_id(0)
        dx2 = dx2_ref[...]
        dx2b = dx2.astype(BF16)
        datt = jnp.zeros((tm, d), F32)
        for j in range(N_CHIPS):
            datt = datt + _dot(dx2b[:, j * rows:(j + 1) * rows], wo_ref[j], NT)
        dattb = datt.astype(BF16)
        dxqs, dxks, dxvs = [], [], []
        for h in range(XATTN_HEADS):
            cs = slice(h * e, (h + 1) * e)
            xq_h, xk_h, xv_h = xq_ref[:, cs], xk_ref[:, cs], xv_ref[:, cs]
            p = _softmax_rows(_dot(xq_h, xk_h, NT) * (e ** -0.5))
            dp = _dot(dattb[:, cs], xv_h, NT)
            ds = (p * (dp - jnp.sum(dp * p, axis=-1, keepdims=True)) * (e ** -0.5)).astype(BF16)
            dxqs.append(_dot(ds, xk_h).astype(BF16))
            dxks.append(_dot(ds, xq_h, TN))
            dxvs.append(_dot(p.astype(BF16), dattb[:, cs], TN))
        dxq = jnp.concatenate(dxqs, axis=1)
        dxq_ref[...] = dxq
        dxk = jnp.concatenate(dxks, axis=1)
        dxv = jnp.concatenate(dxvs, axis=1)

        @pl.when(t == 0)
        def _():
            dxk_ref[...] = dxk
            dxv_ref[...] = dxv

        @pl.when(t > 0)
        def _():
            dxk_ref[...] = dxk_ref[...] + dxk
            dxv_ref[...] = dxv_ref[...] + dxv

        dhq = jnp.concatenate([_dot(dxq, wq_ref[j], NT) for j in range(N_CHIPS)], axis=1)
        g = gx_ref[...]
        _, n, r1 = _rms_fwd(x1_ref[...], g)
        dxn, dg = _rms_bwd(dhq, n, r1, g)
        dx1 = dx2 + dxn
        dx1_ref[...] = dx1
        dx1b = dx1.astype(BF16)
        dx1b_ref[...] = dx1b
        for j in range(N_CHIPS):
            dmix_ref[:, j * rows:(j + 1) * rows] = _dot(dx1b, wout_ref[j], NT)
        _acc_rows(st_ref, t, [dg])

    tile = lambda: pl.BlockSpec((tm, d), lambda t: (t, 0))
    blk = lambda k: pl.BlockSpec((N_CHIPS, rows, d), lambda t: (0, k, 0), pipeline_mode=pl.Buffered(1))
    x_specs, x_shapes, x_scratch = _plan_extras(plan)
    return pl.pallas_call(
        _fuse_exchange(body, 9, 7, 0, plan, 1), name="xattn_mix_bwd", grid=(s // tm,),
        in_specs=[tile(), tile(), tile(), _const((m, d)), _const((m, d)), _const((1, d)), blk(0), blk(1),
                  _const(wo_g.shape)] + x_specs,
        out_specs=[tile(), tile(), tile(), tile(), _const_out((m, d)), _const_out((m, d)), _const_out((8, d))]
                  + x_specs,
        out_shape=[jax.ShapeDtypeStruct((s, d), F32), jax.ShapeDtypeStruct((s, d), BF16),
                   jax.ShapeDtypeStruct((s, d), BF16), jax.ShapeDtypeStruct((s, d), F32),
                   jax.ShapeDtypeStruct((m, d), F32), jax.ShapeDtypeStruct((m, d), F32),
                   jax.ShapeDtypeStruct((8, d), F32)] + x_shapes,
        scratch_shapes=x_scratch,
        compiler_params=_params(("arbitrary",)),
    )(dx2, x1, xq, xk, xv, gx, slab_g, slab_g, wo_g, *plan_args)


def _kv_bwd(mem, g, dxk, dxv, slab_g):
    m, d = mem.shape
    rows = d // N_CHIPS

    def body(mem_ref, g_ref, dxk_ref, dxv_ref, wk_ref, wv_ref, dwk_ref, dwv_ref, st_ref):
        gv = g_ref[...]
        hm, n, _ = _rms_fwd(mem_ref[...], gv)
        hb = hm.astype(BF16)
        dkb = dxk_ref[...].astype(BF16)
        dvb = dxv_ref[...].astype(BF16)
        dhm = []
        for j in range(N_CHIPS):
            hj = hb[:, j * rows:(j + 1) * rows]
            dwk_ref[j] = _dot(hj, dkb, TN)
            dwv_ref[j] = _dot(hj, dvb, TN)
            dhm.append(_dot(dkb, wk_ref[j], NT) + _dot(dvb, wv_ref[j], NT))
        dg = jnp.sum(jnp.concatenate(dhm, axis=1) * n, axis=0, keepdims=True)
        st_ref[...] = jnp.concatenate([dg, jnp.zeros((7, d), F32)], axis=0)

    blk = lambda k: pl.BlockSpec((N_CHIPS, rows, d), lambda i: (0, k, 0))
    return pl.pallas_call(
        body, name="kv_bwd", grid=(1,),
        in_specs=[_const((m, d)), _const((1, d)), _const((m, d)), _const((m, d)), blk(0), blk(1)],
        out_specs=[_const_out((N_CHIPS, rows, d)), _const_out((N_CHIPS, rows, d)), _const_out((8, d))],
        out_shape=[jax.ShapeDtypeStruct((N_CHIPS, rows, d), F32)] * 2 + [jax.ShapeDtypeStruct((8, d), F32)],
        compiler_params=_params(("arbitrary",)),
    )(mem, g, dxk, dxv, slab_g, slab_g)


def _pool_bwd(z, dmix, wp, scale, tm, plan=None, plan_args=()):
    s = z.shape[0]
    ng = len(POOL_WINDOWS)
    pw = ng * HEAD_DIM
    nb = tm // POOL_HALO
    nt = s // tm
    n_ext = tm + POOL_HALO

    def body(p_ref, prev_ref, dm_ref, dmn_ref, wp_ref, sc_ref, dp_ref, dwp_ref, st_ref):
        t = pl.program_id(0)
        p = p_ref[...]
        prev = jnp.where(t > 0, prev_ref[...], 0.0)
        pooled = _pooled(p, jnp.concatenate([prev, p], axis=0), t * tm)
        dm = dm_ref[...]
        dme = jnp.concatenate([dm, jnp.where(t < nt - 1, dmn_ref[...], 0.0)], axis=0) * sc_ref[...]
        tok = t * tm + lax.broadcasted_iota(jnp.int32, (n_ext, 1), 0)
        dsc, dps, dwps = [], [], []
        for g, w in enumerate(POOL_WINDOWS):
            cs = slice(g * HEAD_DIM, (g + 1) * HEAD_DIM)
            wpb = wp_ref[g].astype(BF16)
            pb = pooled[g].astype(BF16)
            dsc.append(jnp.sum(dm[:, cs] * _dot(pb, wpb), axis=0, keepdims=True))
            dye = dme[:, cs].astype(BF16)
            dwps.append(_dot(pb, dye[:tm], TN))
            dpe = _dot(dye, wpb, NT)
            acc = dpe / jnp.minimum(tok + 1, w).astype(F32)
            sh = 1
            while sh < w:
                acc = acc + pltpu.roll(acc, n_ext - sh, axis=0)
                sh *= 2
            dps.append(acc[:tm] - dpe[:tm])
        dp_ref[...] = jnp.concatenate(dps, axis=1)
        dsc_row = jnp.concatenate(dsc, axis=1)

        @pl.when(t == 0)
        def _():
            for g in range(ng):
                dwp_ref[g] = dwps[g]

        @pl.when(t > 0)
        def _():
            for g in range(ng):
                dwp_ref[g] = dwp_ref[g] + dwps[g]

        _acc_rows(st_ref, t, [dsc_row])

    x_specs, x_shapes, x_scratch = _plan_extras(plan)
    return pl.pallas_call(
        _fuse_exchange(body, 6, 3, 0, plan, 1), name="pool_bwd", grid=(nt,),
        in_specs=[pl.BlockSpec((tm, pw), lambda t: (t, 4)),
                  pl.BlockSpec((POOL_HALO, pw), lambda t: (jnp.maximum(t * nb - 1, 0), 4)),
                  pl.BlockSpec((tm, pw), lambda t: (t, 1)),
                  pl.BlockSpec((POOL_HALO, pw), lambda t: (jnp.minimum((t + 1) * nb, s // POOL_HALO - 1), 1)),
                  _const(wp.shape), _const((1, pw))] + x_specs,
        out_specs=[pl.BlockSpec((tm, pw), lambda t: (t, 0)), _const_out(wp.shape), _const_out((8, pw))] + x_specs,
        out_shape=[jax.ShapeDtypeStruct((s, pw), F32), jax.ShapeDtypeStruct(wp.shape, F32),
                   jax.ShapeDtypeStruct((8, pw), F32)] + x_shapes,
        scratch_shapes=x_scratch,
        compiler_params=_params(("arbitrary",)),
    )(z, z, dmix, dmix, wp, scale, *plan_args)


def _hgrn_bwd(z, o, dmix, st, l0, l1, gn, tc, unroll=1, plan=None, plan_args=()):
    s = z.shape[0]
    nsub = tc // SUB
    nt = s // tc
    hd = HEAD_DIM
    hp = HEADS_PER_STEP

    def body(q_ref, f_ref, v_ref, g_ref, l0_ref, l1_ref, gn_ref, o_ref, dm_ref, st_ref,
             tril_ref, triu_ref, trilc_ref, triuc_ref,
             dq_ref, df_ref, di_ref, dg_ref, stat_ref, dstate, qs, ks, bs, dos, dqs, dks, dbs, sts):
        t = pl.program_id(1)

        @pl.when(t == 0)
        def _():
            dstate[...] = jnp.zeros_like(dstate)

        cols = [slice(hh * hd, (hh + 1) * hd) for hh in range(hp)]
        heads = []
        for hh, cs in enumerate(cols):
            lb = _lower_bound(l0_ref[hh], l1_ref[hh])
            qp = q_ref[:, cs]
            q, sq, f, sf = _hgrn_gates(qp, f_ref[:, cs], lb)
            lf = jnp.log(f) * LOG2E
            o = o_ref[:, cs]
            r = lax.rsqrt(jnp.mean(o * o, axis=-1, keepdims=True) + EPS)
            n = o * r
            gnv = gn_ref[hh]
            gp = g_ref[:, cs]
            sg = _sigmoid(gp)
            dm = dm_ref[:, cs]
            dg_ref[:, cs] = dm * (n * gnv) * (sg * (1.0 + gp * (1.0 - sg)))
            don = dm * (gp * sg)
            dn = don * gnv
            heads.append(dict(lb=lb, qp=qp, q=q, sq=sq, f=f, sf=sf, k=1.0 - f, lf=lf,
                              bc=_group_cumsum(trilc_ref[...], lf), dgn=jnp.sum(don * n, axis=0, keepdims=True),
                              do=r * (dn - n * jnp.mean(dn * n, axis=-1, keepdims=True))))
        bounded = functools.reduce(jnp.minimum, [jnp.min(h["bc"]) for h in heads]) >= -MAX_LOG2_GROWTH

        def factored(hh, cs, q, k, bc, do_all):
            eb = jnp.exp2(bc)
            eib = jnp.exp2(-bc)
            qt = (q * eb).astype(BF16)
            ki = (k * eib).astype(BF16)
            vb = v_ref[:, cs].astype(BF16)
            dob = do_all.astype(BF16)
            mask = trilc_ref[...] > 0
            a = jnp.where(mask, _dot(qt, ki, NT), 0.0).astype(BF16)
            da = jnp.where(mask, _dot(dob, vb, NT), 0.0).astype(BF16)
            dq_in = _dot(da, ki)
            dk_in = _dot(da, qt, TN)
            dv_in = _dot(a, dob, TN)
            last_row = lax.broadcasted_iota(jnp.int32, (CHUNK, 1), 0) == CHUNK - 1
            for c in reversed(range(tc // CHUNK)):
                rs = slice(c * CHUNK, (c + 1) * CHUNK)
                stp = st_ref[hh, c]
                dst = dstate[hh]
                dstb = dst.astype(BF16)
                bl = bc[(c + 1) * CHUNK - 1:(c + 1) * CHUNK, :]
                ekl = jnp.exp2(bl - bc[rs])
                ebl = jnp.exp2(bl)
                kt = k[rs] * ekl
                dq_st = _dot(dob[rs], stp.astype(BF16)) * eb[rs]
                dkt = _dot(vb[rs], dstb)
                extra = jnp.sum(kt * dkt, axis=0, keepdims=True) + ebl * jnp.sum(stp * dst, axis=0, keepdims=True)
                dqs[rs, cs] = dq_st + dq_in[rs] * eb[rs]
                dks[rs, cs] = dkt * ekl + dk_in[rs] * eib[rs]
                di_ref[rs, cs] = _dot(kt.astype(BF16), dstb, NT) + dv_in[rs]
                dbs[rs, cs] = (q[rs] * dq_st - kt * dkt + jnp.where(last_row, extra, 0.0)
                               + (qt[rs].astype(F32) * dq_in[rs] - ki[rs].astype(F32) * dk_in[rs]))
                dstate[hh] = dst * ebl + _dot(dob[rs], qt[rs], TN)
            dbs[:, cs] = _group_cumsum(triuc_ref[...], dbs[:, cs])

        def exact(hh, cs, q, k, lf, do_all):
            qs[:, cs] = q
            ks[:, cs] = k
            bs[:, cs] = _group_cumsum(tril_ref[...], lf)
            dos[:, cs] = do_all
            per = CHUNK // SUB

            def restore(i, carry):
                @pl.when(i % per == 0)
                def _():
                    sts[i] = st_ref[hh, i // per]

                @pl.when(i % per != 0)
                def _():
                    rp = pl.multiple_of((i - 1) * SUB, SUB)
                    b_ = bs[pl.ds(rp, SUB), cs]
                    bl = b_[SUB - 1:SUB, :]
                    kt = (ks[pl.ds(rp, SUB), cs] * jnp.exp2(bl - b_)).astype(BF16)
                    sts[i] = sts[i - 1] * jnp.exp2(bl) + _dot(v_ref[pl.ds(rp, SUB), cs].astype(BF16), kt, TN)

                return carry

            lax.fori_loop(0, nsub, restore, 0)
            rows = lax.broadcasted_iota(jnp.int32, (HALF, 1), 0)
            last_row = lax.broadcasted_iota(jnp.int32, (SUB, 1), 0) == SUB - 1

            def step(i, carry):
                ii = nsub - 1 - i
                r0 = pl.multiple_of(ii * SUB, SUB)
                q_ = qs[pl.ds(r0, SUB), cs]
                k_ = ks[pl.ds(r0, SUB), cs]
                b_ = bs[pl.ds(r0, SUB), cs]
                v_ = v_ref[pl.ds(r0, SUB), cs]
                do_ = dos[pl.ds(r0, SUB), cs]
                stp = sts[ii]
                dst = dstate[hh]
                bl = b_[SUB - 1:SUB, :]
                eb = jnp.exp2(b_)
                ekl = jnp.exp2(bl - b_)
                ebl = jnp.exp2(bl)
                dob = do_.astype(BF16)
                dstb = dst.astype(BF16)
                kt = k_ * ekl
                dq = _dot(dob, stp.astype(BF16)) * eb
                dkt = _dot(v_.astype(BF16), dstb)
                dk = dkt * ekl
                dv = _dot(kt.astype(BF16), dstb, NT)
                extra = jnp.sum(kt * dkt, axis=0, keepdims=True) + ebl * jnp.sum(stp * dst, axis=0, keepdims=True)
                halves = lambda x: [x[:HALF], x[HALF:]]
                q_h, b_h, do_h, dq_h, dk_h, dv_h = (halves(x) for x in (q_, b_, do_, dq, dk, dv))
                for own in range(2):
                    dk_rows, dv_rows = _RowSums(rows), _RowSums(rows)
                    for jj in _RowSums.ORDER:
                        j = own * HALF + jj
                        bj, kj, vj = b_[j:j + 1, :], k_[j:j + 1, :], v_[j:j + 1, :]
                        dk_sum = dv_sum = None
                        for h in range(own, 2):
                            e = _decay(b_h[h], bj, rows, jj if h == own else None)
                            pe = q_h[h] * e
                            acol = jnp.sum(pe * kj, axis=-1, keepdims=True)
                            dacol = jnp.sum(do_h[h] * vj, axis=-1, keepdims=True)
                            dq_h[h] = dq_h[h] + dacol * (e * kj)
                            dk_sum = dacol * pe if dk_sum is None else dk_sum + dacol * pe
                            dv_sum = acol * do_h[h] if dv_sum is None else dv_sum + acol * do_h[h]
                        dk_rows.push(jj, dk_sum)
                        dv_rows.push(jj, dv_sum)
                    dk_h[own] = dk_h[own] + dk_rows.result()
                    dv_h[own] = dv_h[own] + dv_rows.result()
                dq, dk, dv = (jnp.concatenate(x, axis=0) for x in (dq_h, dk_h, dv_h))
                dqs[pl.ds(r0, SUB), cs] = dq
                dks[pl.ds(r0, SUB), cs] = dk
                di_ref[pl.ds(r0, SUB), cs] = dv
                dbs[pl.ds(r0, SUB), cs] = q_ * dq - k_ * dk + jnp.where(last_row, extra, 0.0)
                dstate[hh] = dst * ebl + _dot(dob, (q_ * eb).astype(BF16), TN)
                return carry

            lax.fori_loop(0, nsub, step, 0, unroll=unroll)
            dbs[:, cs] = _group_cumsum(triu_ref[...], dbs[:, cs])

        @pl.when(bounded)
        def _():
            for hh, cs in enumerate(cols):
                factored(hh, cs, heads[hh]["q"], heads[hh]["k"], heads[hh]["bc"], heads[hh]["do"])

        @pl.when(jnp.logical_not(bounded))
        def _():
            for hh, cs in enumerate(cols):
                exact(hh, cs, heads[hh]["q"], heads[hh]["k"], heads[hh]["lf"], heads[hh]["do"])

        for hh, cs in enumerate(cols):
            h = heads[hh]
            dfv = dbs[:, cs] / h["f"] - dks[:, cs]
            df_ref[:, cs] = dfv * (1.0 - h["lb"]) * h["sf"] * (1.0 - h["sf"])
            dlb = jnp.sum(dfv * (1.0 - h["sf"]), axis=0, keepdims=True)
            dq_ref[:, cs] = dqs[:, cs] * (h["sq"] * (1.0 + h["qp"] * (1.0 - h["sq"])))
            _acc_rows(stat_ref.at[hh], t, [h["dgn"], dlb])

    rev = lambda t: nt - 1 - t
    ng = HGRN_HEADS // hp
    col = lambda k: pl.BlockSpec((tc, hp * hd), lambda h, t: (rev(t), k * ng + h))
    vec = pl.BlockSpec((hp, 1, hd), lambda h, t: (h, 0, 0))
    head = pl.BlockSpec((tc, hp * hd), lambda h, t: (rev(t), h))
    x_specs, x_shapes, x_scratch = _plan_extras(plan)
    return pl.pallas_call(
        _fuse_exchange(body, 14, 5, 9, plan, 2), name="hgrn_bwd", grid=(ng, nt),
        in_specs=[col(0), col(1), col(2), col(3), vec, vec, vec, head, head,
                  pl.BlockSpec((hp, tc // CHUNK, hd, hd), lambda h, t: (h, rev(t), 0, 0))]
                 + [_const((tc, tc))] * 4 + x_specs,
        out_specs=[head, head, head, head, pl.BlockSpec((hp, 8, hd), lambda h, t: (h, 0, 0))] + x_specs,
        out_shape=[jax.ShapeDtypeStruct((s, HGRN_WIDTH), F32)] * 4 + [jax.ShapeDtypeStruct((HGRN_HEADS, 8, hd), F32)]
                  + x_shapes,
        scratch_shapes=[pltpu.VMEM((hp, hd, hd), F32)] + [pltpu.VMEM((tc, hp * hd), F32)] * 7
                       + [pltpu.VMEM((nsub, hd, hd), F32)] + x_scratch,
        compiler_params=_params(("arbitrary", "arbitrary")),
    )(z, z, z, z, l0, l1, gn, o, dmix, st, _block_tri(tc, SUB, False), _block_tri(tc, SUB, True),
      _block_tri(tc, CHUNK, False), _block_tri(tc, CHUNK, True), *plan_args)


def _in_bwd(dparts, dx1, x, g, win_g, tm, plan=None, plan_args=()):
    s, d = x.shape
    nsh, _, wc = win_g.shape
    pw = dparts[0].shape[1]

    def body(dq_ref, df_ref, di_ref, dg_ref, dp_ref, dx1_ref, x_ref, g_ref, w_ref, gx_ref, dz_ref, st_ref):
        dz = jnp.concatenate([dq_ref[...], df_ref[...], di_ref[...], dg_ref[...], dp_ref[...]], axis=1).astype(BF16)
        dz_ref[...] = dz
        dh = jnp.zeros((tm, d), F32)
        for j in range(nsh):
            dh = dh + _dot(dz[:, j * wc:(j + 1) * wc], w_ref[j], NT)
        gv = g_ref[...]
        _, n, r = _rms_fwd(x_ref[...], gv)
        dxn, dg = _rms_bwd(dh, n, r, gv)
        gx_ref[...] = dx1_ref[...] + dxn
        _acc_rows(st_ref, pl.program_id(0), [dg])

    tile = lambda w: pl.BlockSpec((tm, w), lambda t: (t, 0))
    x_specs, x_shapes, x_scratch = _plan_extras(plan)
    return pl.pallas_call(
        _fuse_exchange(body, 9, 3, 0, plan, 1), name="in_bwd", grid=(s // tm,),
        in_specs=[tile(pw)] * 5 + [tile(d), tile(d), _const((1, d)), _const(win_g.shape)] + x_specs,
        out_specs=[tile(d), tile(nsh * wc), _const_out((8, d))] + x_specs,
        out_shape=[jax.ShapeDtypeStruct((s, d), F32), jax.ShapeDtypeStruct((s, nsh * wc), BF16),
                   jax.ShapeDtypeStruct((8, d), F32)] + x_shapes,
        scratch_shapes=x_scratch,
        compiler_params=_params(("arbitrary",)),
    )(*dparts, dx1, x, g, win_g, *plan_args)


def _tn_grad(name, a, b, out_rows, out_cols, a_sharded, tr, tc, plan=None, plan_args=()):
    s = a.shape[0]
    nr, nc = out_rows // tr, out_cols // tc

    def body(a_ref, b_ref, o_ref):
        o_ref[...] = _dot(a_ref[...], b_ref[...], TN)

    a_map = (lambda j, i, k: (0, j * nr + i)) if a_sharded else (lambda j, i, k: (0, i))
    b_map = (lambda j, i, k: (0, k)) if a_sharded else (lambda j, i, k: (0, j * nc + k))
    x_specs, x_shapes, x_scratch = _plan_extras(plan)
    res = pl.pallas_call(
        _fuse_exchange(body, 2, 1, 0, plan, 3), name=name, grid=(N_CHIPS, nr, nc),
        in_specs=[pl.BlockSpec((s, tr), a_map), pl.BlockSpec((s, tc), b_map)] + x_specs,
        out_specs=[pl.BlockSpec((None, tr, tc), lambda j, i, k: (j, i, k))] + x_specs,
        out_shape=[jax.ShapeDtypeStruct((N_CHIPS, out_rows, out_cols), F32)] + x_shapes,
        scratch_shapes=x_scratch,
        compiler_params=_params(("arbitrary", "arbitrary", "arbitrary")),
    )(a, b, *plan_args)
    return res if plan else res[0]


FFN_NAMES = ("w_ff1", "w_ff2")
ATTN_NAMES = ("w_xo", "w_xq", "w_out", "w_xk", "w_xv")
EARLY_NAMES = FFN_NAMES + ATTN_NAMES
BIG_NAMES = EARLY_NAMES + ("w_in",)


def _halved(g):
    return g.reshape(N_CHIPS, 2, g.shape[1] // 2, g.shape[2])


def _pair_adds(names, gs, got, idx):
    pairs = [_grad_pair_add("grad_pair_add_" + k, g, r, idx, tr=min(256, g.shape[2])) for k, g, r in zip(names, gs, got)]
    return [p[0] for p in pairs], [p[1] for p in pairs]


def _step(x, mem, target, small, shards, idx):
    d = x.shape[1]
    l0 = small["lb_logits"][0].reshape(HGRN_HEADS, 1, HEAD_DIM)
    l1 = small["lb_logits"][1].reshape(HGRN_HEADS, 1, HEAD_DIM)
    gn = small["hgrn_norm_g"].reshape(HGRN_HEADS, 1, HEAD_DIM)
    wp = small["w_pool"].reshape(len(POOL_WINDOWS), HEAD_DIM, HEAD_DIM)
    psc = small["pool_scale"].reshape(1, -1)
    gmix, gx, gmem, gffn = (small[k].reshape(1, d) for k in ("norm_mix_g", "norm_x_g", "norm_mem_g", "norm_ffn_g"))
    gfin = small["final_norm_g"].reshape(1, d)

    (win_g,) = _run_exchange("gather_w_in", _WeightGather([shards["w_in"]]), [shards["w_in"]])
    z, h, kv_g = _in_proj(x, gmix, win_g, tm=512, plan=_WeightGather([shards["slab_kv"]]),
                          plan_args=[shards["slab_kv"]])
    mid_w = [shards["slab_oq"], shards["w_xo"], shards["w_ff1"]]
    o, oa, st, oq_g, wo_g, w1_g = _hgrn_fwd(z, l0, l1, gn, tc=256, unroll=8,
                                            plan=_WeightGather(mid_w), plan_args=mid_w)
    ob = _pool_fwd(z, wp, psc, tm=512)
    xk, xv = _kv_proj(mem, gmem, kv_g)
    late_w = [shards["w_ff2"]]
    x1, mixed, hq, xq, att, x2, w2_g = _mix_xattn_fwd(x, oa, ob, gx, oq_g, wo_g, xk, xv, tm=512,
                                                      plan=_WeightGather(late_w), plan_args=late_w)
    a, hf, dx3, dx3b, st_loss = _mlp_loss_fwd(x2, gffn, gfin, w1_g, w2_g, target, tm=512)

    da, u, dx2, dx2b, st_ffn = _mlp_bwd(dx3, dx3b, a, x2, gffn, w1_g, w2_g, tm=256)
    g_ff1 = [_halved(_tn_grad("dw_ff1", hf, da, d, d, False, 1024, 1024))]
    dw_ff2, *got = _tn_grad("dw_ff2", u, dx3b, d, d, True, 1024, 1024, plan=_PairExchange(g_ff1), plan_args=g_ff1)
    keep_ff1, send_ff1 = _pair_adds(("w_ff1",), g_ff1, got, idx)
    g_ff2 = [_halved(dw_ff2)]
    dx1, dx1b, dxq, dmix, dxk, dxv, st_x, *got = _xattn_mix_bwd(
        dx2, x1, xq, xk, xv, gx, oq_g, wo_g, tm=512,
        plan=_Plans([_ChipExchange(send_ff1), _PairExchange(g_ff2)]), plan_args=send_ff1 + g_ff2)
    recv_ff1 = got[:1]
    keep_ff2, send_ff2 = _pair_adds(("w_ff2",), g_ff2, got[1:], idx)
    dw = {}
    dw["w_xo"] = _tn_grad("dw_xo", att, dx2b, d, d // N_CHIPS, False, 1024, 256)
    dw["w_xq"] = _tn_grad("dw_xq", hq, dxq, d // N_CHIPS, d, True, 256, 1024)
    dw["w_out"] = _tn_grad("dw_out", mixed, dx1b, d // N_CHIPS, d, True, 256, 1024)
    dw["w_xk"], dw["w_xv"], st_mem = _kv_bwd(mem, gmem, dxk, dxv, kv_g)
    gs_attn = [_halved(dw[k]) for k in ATTN_NAMES]
    dp, d_wp, st_pool, *got_attn = _pool_bwd(z, dmix, wp, psc, tm=512, plan=_PairExchange(gs_attn), plan_args=gs_attn)
    keep_attn, send_attn = _pair_adds(ATTN_NAMES, gs_attn, got_attn, idx)
    sends = send_ff2 + send_attn
    dq, df, di, dg, st_hgrn, *received = _hgrn_bwd(z, o, dmix, st, l0, l1, gn, tc=256, unroll=4,
                                                    plan=_ChipExchange(sends), plan_args=sends)
    grad_x, dz, st_mix = _in_bwd([dq, df, di, dg, dp], dx1, x, gmix, win_g, tm=512)
    keeps = keep_ff1 + keep_ff2 + keep_attn
    received = recv_ff1 + list(received)
    gs_in = [_halved(_tn_grad("dw_in", h, dz, d, win_g.shape[2], False, 1024, win_g.shape[2]))]
    got_in = _run_exchange("grad_pair_exchange_w_in", _PairExchange(gs_in), gs_in)
    keep_in, send_in = _pair_adds(("w_in",), gs_in, got_in, idx)

    partials = dict(zip(EARLY_NAMES, zip(keeps, received)))
    stats = dict(mix=st_mix, x=st_x, mem=st_mem, ffn=st_ffn, loss=st_loss, hgrn=st_hgrn, pool=st_pool)
    return grad_x, stats, d_wp, partials, keep_in, send_in


def _place():
    x, y, c = lax.axis_index("x"), lax.axis_index("y"), lax.axis_index("c")
    return x, y, c, [(x, 1 - y), (1 - x, y), (1 - x, 1 - y)]


def _rcopy(src, dst, ssem, rsem, dev):
    return pltpu.make_async_remote_copy(src_ref=src, dst_ref=dst, send_sem=ssem, recv_sem=rsem,
                                        device_id=dev, device_id_type=MESH)


class _WeightGather:
    def __init__(self, shards):
        self.n = len(shards)
        self.rows = [w.shape[0] for w in shards]
        self.out_shape = [jax.ShapeDtypeStruct((N_CHIPS,) + w.shape, w.dtype) for w in shards]
        self.scratch_shapes = [pltpu.SemaphoreType.DMA((self.n,))] * 2 + [pltpu.SemaphoreType.DMA((self.n, 3))] * 4

    def _copies(self, ins, outs, sems, with_pass_on):
        lsem, lrsem, ssem, rsem, fsem, frsem = sems
        x, y, c, peers = _place()
        chip = 2 * x + y
        sib = (x, y, 1 - c)
        own = [_rcopy(ins[a], outs[a].at[chip], lsem.at[a], lrsem.at[a], sib) for a in range(self.n)]
        sends, arrived, passed, passed_in = [], [], [], []
        for a in range(self.n):
            hr = self.rows[a] // 2
            half = lambda who, hc, a=a, hr=hr: outs[a].at[who, pl.ds(hc * hr, hr), :]
            for r, (px, py) in enumerate(peers):
                pc = 2 * px + py
                sends.append(_rcopy(ins[a].at[pl.ds(c * hr, hr), :], half(chip, c), ssem.at[a, r], rsem.at[a, r],
                                    (px, py, c)))
                if with_pass_on:
                    arrived.append(_rcopy(half(pc, c), half(pc, c), ssem.at[a, r], rsem.at[a, r], (px, py, c)))
                    passed.append(_rcopy(half(pc, c), half(pc, c), fsem.at[a, r], frsem.at[a, r], sib))
                    passed_in.append(_rcopy(half(pc, 1 - c), half(pc, 1 - c), fsem.at[a, r], frsem.at[a, r], sib))
        return own, sends, arrived, passed, passed_in

    def start(self, ins, outs, sems):
        own, sends, _, _, _ = self._copies(ins, outs, sems, False)
        for cp in own + sends:
            cp.start()

    def finish(self, ins, outs, sems):
        own, sends, arrived, passed, passed_in = self._copies(ins, outs, sems, True)
        for got, fwd in zip(arrived, passed):
            got.wait_recv()
            fwd.start()
        for cp in passed_in:
            cp.wait_recv()
        for cp in sends + passed:
            cp.wait_send()
        for cp in own:
            cp.wait()


class _ChipExchange:
    def __init__(self, sends, only=(0, 1, 2)):
        self.n = len(sends)
        self.only = only
        self.out_shape = [jax.ShapeDtypeStruct(g.shape, g.dtype) for g in sends]
        self.scratch_shapes = [pltpu.SemaphoreType.DMA((self.n, 3))] * 2

    def _copies(self, ins, outs, sems):
        ssem, rsem = sems
        _, _, c, peers = _place()
        return [_rcopy(ins[a].at[r], outs[a].at[r], ssem.at[a, r], rsem.at[a, r], (px, py, c))
                for a in range(self.n) for r, (px, py) in enumerate(peers) if r in self.only]

    def start(self, ins, outs, sems):
        for cp in self._copies(ins, outs, sems):
            cp.start()

    def finish(self, ins, outs, sems):
        for cp in self._copies(ins, outs, sems):
            cp.wait()


class _Plans:
    def __init__(self, plans):
        self.plans = plans
        self.n = sum(p.n for p in plans)
        self.out_shape = [s for p in plans for s in p.out_shape]
        self.scratch_shapes = [s for p in plans for s in p.scratch_shapes]

    def _each(self, ins, outs, sems):
        a = b = 0
        for p in self.plans:
            ns = len(p.scratch_shapes)
            yield p, ins[a:a + p.n], outs[a:a + p.n], sems[b:b + ns]
            a, b = a + p.n, b + ns

    def start(self, ins, outs, sems):
        for p, i, o, s in self._each(ins, outs, sems):
            p.start(i, o, s)

    def finish(self, ins, outs, sems):
        for p, i, o, s in self._each(ins, outs, sems):
            p.finish(i, o, s)


def _run_exchange(name, plan, arrays):
    n = plan.n

    def body(*refs):
        ins, outs, sems = refs[:n], refs[n:2 * n], refs[2 * n:]
        plan.start(ins, outs, sems)
        plan.finish(ins, outs, sems)

    return pl.pallas_call(
        body, name=name, in_specs=[ANY] * n, out_specs=[ANY] * n,
        out_shape=plan.out_shape, scratch_shapes=plan.scratch_shapes,
    )(*arrays)


class _PairExchange:
    def __init__(self, gs):
        self.n = len(gs)
        self.out_shape = [jax.ShapeDtypeStruct((g.shape[0],) + g.shape[2:], g.dtype) for g in gs]
        self.scratch_shapes = [pltpu.SemaphoreType.DMA((self.n,))] * 2

    def _copies(self, ins, outs, sems):
        ssem, rsem = sems
        x, y, c, _ = _place()
        return [_rcopy(ins[a].at[:, 1 - c], outs[a], ssem.at[a], rsem.at[a], (x, y, 1 - c)) for a in range(self.n)]

    def start(self, ins, outs, sems):
        for cp in self._copies(ins, outs, sems):
            cp.start()

    def finish(self, ins, outs, sems):
        for cp in self._copies(ins, outs, sems):
            cp.wait()


def _grad_pair_add(name, g, got, idx, tr):
    _, _, hr, cc = g.shape

    def body(idx_ref, g0, g1, g2, g3, r0, r1, r2, r3, keep_ref, send_ref):
        keep_ref[...] = g0[...] + r0[...]
        for q, (gq, rq) in enumerate(((g1, r1), (g2, r2), (g3, r3))):
            send_ref[q] = (gq[...] + rq[...]).astype(BF16)

    gspec = lambda q: pl.BlockSpec((None, None, tr, cc), lambda i, idx: (idx[1 + q], idx[0], i, 0))
    rspec = lambda q: pl.BlockSpec((None, tr, cc), lambda i, idx: (idx[1 + q], i, 0))
    return pl.pallas_call(
        body, name=name,
        grid_spec=pltpu.PrefetchScalarGridSpec(
            num_scalar_prefetch=1, grid=(hr // tr,),
            in_specs=[gspec(q) for q in range(4)] + [rspec(q) for q in range(4)],
            out_specs=[pl.BlockSpec((tr, cc), lambda i, idx: (i, 0)), pl.BlockSpec((3, tr, cc), lambda i, idx: (0, i, 0))]),
        out_shape=[jax.ShapeDtypeStruct((hr, cc), F32), jax.ShapeDtypeStruct((3, hr, cc), BF16)],
        compiler_params=_params(("parallel",)),
    )(idx, g, g, g, g, got, got, got, got)


def _grad_chip_add(name, keep, got, tr):
    hr, cc = keep.shape
    gots = list(got) if isinstance(got, (list, tuple)) else [got] * 3

    def body(k_ref, g0_ref, g1_ref, g2_ref, o_ref):
        o_ref[...] = ((k_ref[...] + g0_ref[...].astype(F32)) + g1_ref[...].astype(F32)) + g2_ref[...].astype(F32)

    slot = lambda r: pl.BlockSpec((None, tr, cc), lambda i: (r, i, 0))
    return pl.pallas_call(
        body, name=name, grid=(hr // tr,),
        in_specs=[pl.BlockSpec((tr, cc), lambda i: (i, 0)), slot(0), slot(1), slot(2)],
        out_specs=pl.BlockSpec((tr, cc), lambda i: (i, 0)),
        out_shape=jax.ShapeDtypeStruct((hr, cc), F32),
        compiler_params=_params(("parallel",)),
    )(keep, *gots)


class _HalfExchange:
    def __init__(self, ts):
        self.n = len(ts)
        self.out_shape = [jax.ShapeDtypeStruct(t.shape, t.dtype) for t in ts]
        self.scratch_shapes = [pltpu.SemaphoreType.DMA((self.n,))] * 2

    def _copies(self, ins, outs, sems):
        ssem, rsem = sems
        x, y, c, _ = _place()
        return [_rcopy(ins[a], outs[a], ssem.at[a], rsem.at[a], (x, y, 1 - c)) for a in range(self.n)]

    def start(self, ins, outs, sems):
        for cp in self._copies(ins, outs, sems):
            cp.start()

    def finish(self, ins, outs, sems):
        for cp in self._copies(ins, outs, sems):
            cp.wait()


def _small_allreduce(stats, d_wp, plan, plan_args):
    d = D_MODEL
    half = d // 2
    wps = d_wp.shape
    n = plan.n

    def body(mix_ref, x_ref, mem_ref, ffn_ref, loss_ref, hg_ref, pool_ref, wp_ref, *refs):
        cin, (slab_out, wp_out), cout = refs[:n], refs[n:n + 2], refs[n + 2:2 * n + 2]
        slab_buf, wp_buf, sib_s, sib_w, ssem, rsem = refs[2 * n + 2:2 * n + 8]
        csem = refs[2 * n + 8:]
        plan.start(cin, cout, csem)
        x, y, c, peers = _place()
        chip = 2 * x + y
        sib = (x, y, 1 - c)
        hgn = jnp.concatenate([hg_ref[h, 0:1, :] for h in range(HGRN_HEADS)], axis=1)
        dlb = jnp.concatenate([hg_ref[h, 1:2, :] for h in range(HGRN_HEADS)], axis=1)
        slab_buf[0] = jnp.concatenate([
            mix_ref[0:1, :], x_ref[0:1, :], mem_ref[0:1, :], ffn_ref[0:1, :], loss_ref[0:1, :],
            jnp.concatenate([dlb, hgn], axis=1),
            jnp.concatenate([pool_ref[0:1, :], jnp.zeros((1, half), F32)], axis=1),
            loss_ref[1:2, :]], axis=0)
        wp_buf[0] = wp_ref[...]
        pair = [_rcopy(slab_buf.at[0], sib_s, ssem.at[0], rsem.at[0], sib),
                _rcopy(wp_buf.at[0], sib_w, ssem.at[1], rsem.at[1], sib)]
        for cp in pair:
            cp.start()
        for cp in pair:
            cp.wait()
        slab_buf[0] = slab_buf[0] + sib_s[...]
        wp_buf[0] = wp_buf[0] + sib_w[...]
        cps = []
        for r, (px, py) in enumerate(peers):
            cps.append(_rcopy(slab_buf.at[0], slab_buf.at[r + 1], ssem.at[2 + 2 * r], rsem.at[2 + 2 * r], (px, py, c)))
            cps.append(_rcopy(wp_buf.at[0], wp_buf.at[r + 1], ssem.at[3 + 2 * r], rsem.at[3 + 2 * r], (px, py, c)))
        for cp in cps:
            cp.start()
        for cp in cps:
            cp.wait()
        tot_s, tot_w = slab_buf[chip], wp_buf[chip]
        for j in range(1, N_CHIPS):
            tot_s = tot_s + slab_buf[jnp.bitwise_xor(j, chip)]
            tot_w = tot_w + wp_buf[jnp.bitwise_xor(j, chip)]
        slab_out[...] = tot_s
        wp_out[...] = tot_w
        plan.finish(cin, cout, csem)

    return pl.pallas_call(
        body, name="small_allreduce",
        in_specs=[VMEM] * 8 + [ANY] * n, out_specs=[VMEM] * 2 + [ANY] * n,
        out_shape=[jax.ShapeDtypeStruct((8, d), F32), jax.ShapeDtypeStruct(wps, F32)] + list(plan.out_shape),
        scratch_shapes=[pltpu.VMEM((N_CHIPS, 8, d), F32), pltpu.VMEM((N_CHIPS,) + wps, F32),
                        pltpu.VMEM((8, d), F32), pltpu.VMEM(wps, F32),
                        pltpu.SemaphoreType.DMA((8,)), pltpu.SemaphoreType.DMA((8,))] + list(plan.scratch_shapes),
    )(stats["mix"], stats["x"], stats["mem"], stats["ffn"], stats["loss"], stats["hgrn"], stats["pool"], d_wp,
      *plan_args)


def _adamw_math(w, g, m, v):
    m = ADAM_B1 * m + (1.0 - ADAM_B1) * g
    v = ADAM_B2 * v + (1.0 - ADAM_B2) * (g * g)
    m_hat = m / (1.0 - ADAM_B1 ** ADAM_STEP)
    v_hat = v / (1.0 - ADAM_B2 ** ADAM_STEP)
    delta = -ADAM_LR * (m_hat / (jnp.sqrt(v_hat) + ADAM_EPS) + ADAM_WD * w)
    return delta, m, v


def _adamw(name, mine, theirs, w, m, v, idx, tr, plan=None, plan_args=()):
    rows = w.shape[0]
    cc = mine.shape[1]
    nb = rows // 2 // tr
    heads = w.shape[1] if w.ndim == 3 else 1
    e = cc // heads

    def body(idx_ref, a_ref, b_ref, w_ref, m_ref, v_ref, g_out, d_out, m_out, v_out):
        g = jnp.where(pl.program_id(0) // nb == idx_ref[0], a_ref[...], b_ref[...])
        if w.ndim == 2:
            g_out[...] = g
            d_out[...], m_out[...], v_out[...] = _adamw_math(w_ref[...], g, m_ref[...], v_ref[...])
        else:
            for h in range(heads):
                gh = g[:, h * e:(h + 1) * e]
                g_out[:, h, :] = gh
                d_out[:, h, :], m_out[:, h, :], v_out[:, h, :] = _adamw_math(
                    w_ref[:, h, :], gh, m_ref[:, h, :], v_ref[:, h, :])

    hspec = pl.BlockSpec((tr, cc), lambda i, idx: (i % nb, 0))
    spec = pl.BlockSpec((tr,) + w.shape[1:], lambda i, idx: (i,) + (0,) * (w.ndim - 1))
    x_specs, x_shapes, x_scratch = _plan_extras(plan)
    return pl.pallas_call(
        _fuse_exchange(body, 6, 4, 0, plan, 1), name=name,
        grid_spec=pltpu.PrefetchScalarGridSpec(
            num_scalar_prefetch=1, grid=(rows // tr,),
            in_specs=[hspec, hspec, spec, spec, spec] + x_specs, out_specs=[spec] * 4 + x_specs,
            scratch_shapes=x_scratch),
        out_shape=[jax.ShapeDtypeStruct(w.shape, F32)] * 4 + x_shapes,
        compiler_params=_params(("arbitrary",)),
    )(idx, mine, theirs, w, m, v, *plan_args)


SMALL_NAMES = ("norm_mix_g", "lb_logits", "hgrn_norm_g", "w_pool", "pool_scale", "norm_x_g", "norm_mem_g",
               "norm_ffn_g", "final_norm_g")


def _small_update(slab, d_wp, ws, ms, vs):
    n = len(SMALL_NAMES)
    half = D_MODEL // 2

    def body(slab_ref, wp_ref, *refs):
        w_refs, m_refs, v_refs, outs = refs[:n], refs[n:2 * n], refs[2 * n:3 * n], refs[3 * n:]
        row = lambda k: slab_ref[k:k + 1, :]
        lbl = w_refs[SMALL_NAMES.index("lb_logits")][...]
        s0 = _lower_bound(lbl[0:1, :], lbl[1:2, :])
        dl0 = row(ROW_LB_HGN)[:, :half] * s0 * (1.0 - s0)
        grads = dict(norm_mix_g=row(ROW_GMIX), lb_logits=jnp.concatenate([dl0, -dl0], axis=0),
                     hgrn_norm_g=row(ROW_LB_HGN)[:, half:], w_pool=wp_ref[...], pool_scale=row(ROW_PSCALE)[:, :half],
                     norm_x_g=row(ROW_GX), norm_mem_g=row(ROW_GMEM), norm_ffn_g=row(ROW_GFFN),
                     final_norm_g=row(ROW_GFIN))
        outs[0][...] = row(ROW_LOSS)[:, :128]
        for i, name in enumerate(SMALL_NAMES):
            g = grads[name]
            delta, m2, v2 = _adamw_math(w_refs[i][...], g, m_refs[i][...], v_refs[i][...])
            for o, val in zip(outs[1 + 4 * i:5 + 4 * i], (g, delta, m2, v2)):
                o[...] = val

    args = [ws[k] for k in SMALL_NAMES] + [ms[k] for k in SMALL_NAMES] + [vs[k] for k in SMALL_NAMES]
    out_shape = [jax.ShapeDtypeStruct((1, 128), F32)]
    for k in SMALL_NAMES:
        out_shape += [jax.ShapeDtypeStruct(ws[k].shape, F32)] * 4
    res = pl.pallas_call(
        body, name="small_update",
        in_specs=[VMEM] * (2 + 3 * n), out_specs=[VMEM] * len(out_shape), out_shape=out_shape,
    )(slab, d_wp, *args)
    return res[0], {k: res[1 + 4 * i:5 + 4 * i] for i, k in enumerate(SMALL_NAMES)}


ALL_NAMES = ("norm_mix_g", "w_in", "lb_logits", "hgrn_norm_g", "w_pool", "pool_scale", "w_out", "norm_x_g",
             "norm_mem_g", "w_xq", "w_xk", "w_xv", "w_xo", "norm_ffn_g", "w_ff1", "w_ff2", "final_norm_g")


def _shard_2d(name, a):
    a = a[0]
    if name in ("w_xq", "w_xk", "w_xv"):
        return a.reshape(a.shape[0], -1)
    if name == "w_xo":
        return a.reshape(-1, a.shape[-1])
    return a


def _small_2d(name, a):
    if name == "w_pool":
        return a.reshape(-1, HEAD_DIM)
    if name == "lb_logits":
        return a
    return a.reshape(1, -1)


def kernel(x, mem, norm_mix_g, w_in, lb_logits, hgrn_norm_g, w_pool, pool_scale, w_out, norm_x_g, norm_mem_g, w_xq, w_xk, w_xv, w_xo, norm_ffn_g, w_ff1, w_ff2, final_norm_g, loss_target, m_norm_mix_g, m_w_in, m_lb_logits, m_hgrn_norm_g, m_w_pool, m_pool_scale, m_w_out, m_norm_x_g, m_norm_mem_g, m_w_xq, m_w_xk, m_w_xv, m_w_xo, m_norm_ffn_g, m_w_ff1, m_w_ff2, m_final_norm_g, v_norm_mix_g, v_w_in, v_lb_logits, v_hgrn_norm_g, v_w_pool, v_pool_scale, v_w_out, v_norm_x_g, v_norm_mem_g, v_w_xq, v_w_xk, v_w_xv, v_w_xo, v_norm_ffn_g, v_w_ff1, v_w_ff2, v_final_norm_g):
    w = dict(norm_mix_g=norm_mix_g, w_in=w_in, lb_logits=lb_logits, hgrn_norm_g=hgrn_norm_g, w_pool=w_pool, pool_scale=pool_scale, w_out=w_out, norm_x_g=norm_x_g, norm_mem_g=norm_mem_g, w_xq=w_xq, w_xk=w_xk, w_xv=w_xv, w_xo=w_xo, norm_ffn_g=norm_ffn_g, w_ff1=w_ff1, w_ff2=w_ff2, final_norm_g=final_norm_g)
    m = dict(norm_mix_g=m_norm_mix_g, w_in=m_w_in, lb_logits=m_lb_logits, hgrn_norm_g=m_hgrn_norm_g, w_pool=m_w_pool, pool_scale=m_pool_scale, w_out=m_w_out, norm_x_g=m_norm_x_g, norm_mem_g=m_norm_mem_g, w_xq=m_w_xq, w_xk=m_w_xk, w_xv=m_w_xv, w_xo=m_w_xo, norm_ffn_g=m_norm_ffn_g, w_ff1=m_w_ff1, w_ff2=m_w_ff2, final_norm_g=m_final_norm_g)
    v = dict(norm_mix_g=v_norm_mix_g, w_in=v_w_in, lb_logits=v_lb_logits, hgrn_norm_g=v_hgrn_norm_g, w_pool=v_w_pool, pool_scale=v_pool_scale, w_out=v_w_out, norm_x_g=v_norm_x_g, norm_mem_g=v_norm_mem_g, w_xq=v_w_xq, w_xk=v_w_xk, w_xv=v_w_xv, w_xo=v_w_xo, norm_ffn_g=v_norm_ffn_g, w_ff1=v_w_ff1, w_ff2=v_w_ff2, final_norm_g=v_final_norm_g)

    big_w = {k: _shard_2d(k, w[k]) for k in BIG_NAMES}
    slab_oq = jnp.concatenate([big_w["w_out"], big_w["w_xq"]], axis=0).astype(BF16)
    slab_kv = jnp.concatenate([big_w["w_xk"], big_w["w_xv"]], axis=0).astype(BF16)
    shards = dict(slab_oq=slab_oq, slab_kv=slab_kv,
                  **{k: big_w[k].astype(BF16) for k in ("w_in", "w_xo", "w_ff1", "w_ff2")})

    cx, cy, cc = lax.axis_index("x"), lax.axis_index("y"), lax.axis_index("c")
    chip = 2 * cx + cy
    idx = jnp.stack([cc, chip, chip ^ 1, chip ^ 2, chip ^ 3]).astype(jnp.int32)
    small = {k: w[k] for k in SMALL_NAMES}
    grad_x, stats, d_wp, partials, keep_in, send_in = _step(x[0], mem[0], loss_target[0], small, shards, idx)

    chip_add = lambda k, keep, got: _grad_chip_add("grad_chip_add_" + k, keep, got, tr=min(256, keep.shape[0]))
    halves = {k: chip_add(k, *partials[k]) for k in EARLY_NAMES}
    early = [halves[k] for k in EARLY_NAMES]
    slab_sum, wp_sum, recv0, *their_early = _small_allreduce(
        stats, d_wp.reshape(-1, HEAD_DIM), _Plans([_ChipExchange(send_in, only=(0,)), _HalfExchange(early)]),
        send_in + early)
    theirs = dict(zip(EARLY_NAMES, their_early))

    grads, deltas, new_m, new_v = {}, {}, {}, {}

    def adamw(k, plan=None, plan_args=()):
        as_held = (lambda a: a[0]) if k in ("w_xq", "w_xk", "w_xv") else functools.partial(_shard_2d, k)
        res = _adamw("adamw_" + k, halves[k], theirs[k], as_held(w[k]), as_held(m[k]), as_held(v[k]), idx,
                     tr=min(256, halves[k].shape[0]), plan=plan, plan_args=plan_args)
        for store, val in zip((grads, deltas, new_m, new_v), res[:4]):
            store[k] = val.reshape(w[k].shape)
        return res[4:]

    (recv1,) = adamw("w_ff1", _ChipExchange(send_in, only=(1,)), send_in)
    (recv2,) = adamw("w_ff2", _ChipExchange(send_in, only=(2,)), send_in)
    halves["w_in"] = chip_add("w_in", keep_in[0], [recv0, recv1, recv2])
    (theirs["w_in"],) = _run_exchange("grad_half_exchange_w_in", _HalfExchange([halves["w_in"]]), [halves["w_in"]])
    for k in BIG_NAMES:
        if k not in FFN_NAMES:
            adamw(k)

    loss, upd = _small_update(slab_sum, wp_sum, {k: _small_2d(k, w[k]) for k in SMALL_NAMES},
                              {k: _small_2d(k, m[k]) for k in SMALL_NAMES}, {k: _small_2d(k, v[k]) for k in SMALL_NAMES})
    for k in SMALL_NAMES:
        for store, val in zip((grads, deltas, new_m, new_v), upd[k]):
            store[k] = val.reshape(w[k].shape)

    return (loss[0, 0], grad_x[None], *[grads[k] for k in ALL_NAMES], *[deltas[k] for k in ALL_NAMES],
            *[new_m[k] for k in ALL_NAMES], *[new_v[k] for k in ALL_NAMES])
```

```python
import functools

import jax
import jax.numpy as jnp
from jax import lax
from jax.experimental import pallas as pl
from jax.experimental.pallas import tpu as pltpu

F32 = jnp.float32
BF16 = jnp.bfloat16
LOG2E = 1.4426950408889634
NEG_BIG = -1e30
MAX_LOG2_GROWTH = 100.0
MESH = pl.DeviceIdType.MESH
ANY = pl.BlockSpec(memory_space=pl.ANY)
VMEM = pl.BlockSpec(memory_space=pltpu.VMEM)

D_MODEL = 1024
N_CHIPS = 4
HGRN_HEADS = 4
HEAD_DIM = 128
HGRN_WIDTH = HGRN_HEADS * HEAD_DIM
POOL_WINDOWS = (2, 4, 8, 16)
POOL_HALO = 16
SUB = 16
HALF = SUB // 2
CHUNK = 64
HEADS_PER_STEP = 4
XATTN_HEADS = 4
XATTN_HEAD_DIM = 256
EPS = 1e-6
ADAM_LR, ADAM_B1, ADAM_B2, ADAM_EPS, ADAM_WD, ADAM_STEP = 0.001, 0.9, 0.999, 1e-08, 0.01, 10

V7X_VMEM_BYTES = 64 * 1024 * 1024
VMEM_LIMIT = V7X_VMEM_BYTES - 8 * 1024 * 1024

NN = (((1,), (0,)), ((), ()))
NT = (((1,), (1,)), ((), ()))
TN = (((0,), (0,)), ((), ()))

ROW_GMIX, ROW_GX, ROW_GMEM, ROW_GFFN, ROW_GFIN, ROW_LB_HGN, ROW_PSCALE, ROW_LOSS = range(8)


def _dot(a, b, dims=NN):
    return lax.dot_general(a, b, dims, preferred_element_type=F32)


def _sigmoid(x):
    return 1.0 / (1.0 + jnp.exp(-x))


def _rms_fwd(x, g):
    r = lax.rsqrt(jnp.mean(x * x, axis=-1, keepdims=True) + EPS)
    n = x * r
    return n * g, n, r


def _rms_bwd(dh, n, r, g):
    dn = dh * g
    dx = r * (dn - n * jnp.mean(dn * n, axis=-1, keepdims=True))
    return dx, jnp.sum(dh * n, axis=0, keepdims=True)


def _params(sem=None):
    return pltpu.CompilerParams(dimension_semantics=sem, vmem_limit_bytes=VMEM_LIMIT)


def _const(shape):
    nd = len(shape)
    return pl.BlockSpec(shape, lambda *_: (0,) * nd, pipeline_mode=pl.Buffered(1))


def _const_out(shape):
    nd = len(shape)
    return pl.BlockSpec(shape, lambda *_: (0,) * nd)


def _acc_rows(ref, t, rows):
    upd = jnp.concatenate(rows + [jnp.zeros((8 - len(rows), rows[0].shape[1]), F32)], axis=0)

    @pl.when(t == 0)
    def _():
        ref[...] = upd

    @pl.when(t > 0)
    def _():
        ref[...] = ref[...] + upd


def _fuse_exchange(body, n_in, n_out, n_scratch, plan, ndim):
    if plan is None:
        return body
    n = plan.n

    def wrapped(*refs):
        ins, cin = refs[:n_in], refs[n_in:n_in + n]
        outs, cout = refs[n_in + n:n_in + n + n_out], refs[n_in + n + n_out:n_in + 2 * n + n_out]
        rest = refs[n_in + 2 * n + n_out:]
        scr, csem = rest[:n_scratch], rest[n_scratch:]
        first = pl.program_id(0) == 0
        last = pl.program_id(0) == pl.num_programs(0) - 1
        for i in range(1, ndim):
            first = first & (pl.program_id(i) == 0)
            last = last & (pl.program_id(i) == pl.num_programs(i) - 1)

        @pl.when(first)
        def _():
            plan.start(cin, cout, csem)

        body(*ins, *outs, *scr)

        @pl.when(last)
        def _():
            plan.finish(cin, cout, csem)

    return wrapped


def _plan_extras(plan):
    if plan is None:
        return [], [], []
    return [ANY] * plan.n, list(plan.out_shape), list(plan.scratch_shapes)


def _in_proj(x, g, win_g, tm, plan=None, plan_args=()):
    s, d = x.shape
    nsh, _, wc = win_g.shape

    def body(x_ref, g_ref, w_ref, z_ref, h_ref):
        h, _, _ = _rms_fwd(x_ref[...], g_ref[...])
        hb = h.astype(BF16)
        h_ref[...] = hb
        for j in range(nsh):
            z_ref[:, j * wc:(j + 1) * wc] = _dot(hb, w_ref[j])

    x_specs, x_shapes, x_scratch = _plan_extras(plan)
    return pl.pallas_call(
        _fuse_exchange(body, 3, 2, 0, plan, 1), name="in_proj", grid=(s // tm,),
        in_specs=[pl.BlockSpec((tm, d), lambda t: (t, 0)), _const((1, d)), _const((nsh, d, wc))] + x_specs,
        out_specs=[pl.BlockSpec((tm, nsh * wc), lambda t: (t, 0)), pl.BlockSpec((tm, d), lambda t: (t, 0))] + x_specs,
        out_shape=[jax.ShapeDtypeStruct((s, nsh * wc), F32), jax.ShapeDtypeStruct((s, d), BF16)] + x_shapes,
        scratch_shapes=x_scratch,
        compiler_params=_params(("arbitrary",)),
    )(x, g, win_g, *plan_args)


def _lower_bound(l0, l1):
    m = jnp.maximum(l0, l1)
    e0, e1 = jnp.exp(l0 - m), jnp.exp(l1 - m)
    return e0 / (e0 + e1)


def _block_tri(n, group, upper):
    r = lax.broadcasted_iota(jnp.int32, (n, n), 0)
    c = lax.broadcasted_iota(jnp.int32, (n, n), 1)
    keep = (r // group == c // group) & ((c >= r) if upper else (c <= r))
    return keep.astype(BF16)


def _group_cumsum(tri, x):
    hi = x.astype(BF16)
    rest = x - hi.astype(F32)
    mid = rest.astype(BF16)
    lo = (rest - mid.astype(F32)).astype(BF16)
    return (_dot(tri, hi) + _dot(tri, mid)) + _dot(tri, lo)


def _decay(b, bj, rows, first):
    d = b - bj
    if first:
        d = jnp.where(rows >= first, d, NEG_BIG)
    return jnp.exp2(d)


class _RowSums:
    ORDER = (0, 4, 2, 6, 1, 5, 3, 7)

    def __init__(self, rows):
        self.rows = rows
        self.level = {4: {}, 2: {}, 1: {}}

    def _pair(self, p, q, d):
        return jnp.where((self.rows & d) != 0, p + pltpu.roll(p, d, axis=0), q + pltpu.roll(q, HALF - d, axis=0))

    def push(self, j, y, d=4):
        if d == 0:
            self.out = y
            return
        slot = self.level[d]
        key = j % d
        if key not in slot:
            slot[key] = (j, y)
            return
        j0, y0 = slot.pop(key)
        p, q = (y, y0) if j & d else (y0, y)
        self.push(key, self._pair(p, q, d), d // 2)

    def result(self):
        return self.out


def _hgrn_gates(qp, fp, lb):
    sq = _sigmoid(qp)
    sf = _sigmoid(fp)
    f = lb + (1.0 - lb) * sf
    return qp * sq, sq, f, sf


def _hgrn_fwd(z, l0, l1, gn, tc, unroll=1, plan=None, plan_args=()):
    s = z.shape[0]
    nsub = tc // SUB
    hd = HEAD_DIM
    hp = HEADS_PER_STEP

    def body(q_ref, f_ref, v_ref, g_ref, l0_ref, l1_ref, gn_ref, tri_ref, tric_ref, o_ref, oa_ref, st_ref,
             state, qs, ks, bs, os_):
        @pl.when(pl.program_id(1) == 0)
        def _():
            state[...] = jnp.zeros_like(state)

        cols = [slice(hh * hd, (hh + 1) * hd) for hh in range(hp)]
        q, k, lf, bc = [], [], [], []
        for hh, cs in enumerate(cols):
            qh, _, fh, _ = _hgrn_gates(q_ref[:, cs], f_ref[:, cs], _lower_bound(l0_ref[hh], l1_ref[hh]))
            q.append(qh)
            k.append(1.0 - fh)
            lf.append(jnp.log(fh) * LOG2E)
            bc.append(_group_cumsum(tric_ref[...], lf[hh]))
        bounded = functools.reduce(jnp.minimum, [jnp.min(b) for b in bc]) >= -MAX_LOG2_GROWTH

        @pl.when(bounded)
        def _():
            mask = tric_ref[...] > 0
            for hh, cs in enumerate(cols):
                qt = (q[hh] * jnp.exp2(bc[hh])).astype(BF16)
                ki = (k[hh] * jnp.exp2(-bc[hh])).astype(BF16)
                vb = v_ref[:, cs].astype(BF16)
                a = jnp.where(mask, _dot(qt, ki, NT), 0.0).astype(BF16)
                o_in = _dot(a, vb)
                for c in range(tc // CHUNK):
                    rs = slice(c * CHUNK, (c + 1) * CHUNK)
                    st = state[hh]
                    st_ref[hh, c] = st
                    os_[rs, cs] = o_in[rs] + _dot(qt[rs], st.astype(BF16), NT)
                    bl = bc[hh][(c + 1) * CHUNK - 1:(c + 1) * CHUNK, :]
                    kt = (k[hh][rs] * jnp.exp2(bl - bc[hh][rs])).astype(BF16)
                    state[hh] = st * jnp.exp2(bl) + _dot(vb[rs], kt, TN)

        @pl.when(jnp.logical_not(bounded))
        def _():
            rows = lax.broadcasted_iota(jnp.int32, (HALF, 1), 0)
            for hh, cs in enumerate(cols):
                qs[:, cs] = q[hh]
                ks[:, cs] = k[hh]
                bs[:, cs] = _group_cumsum(tri_ref[...], lf[hh])

                def step(i, carry, hh=hh, cs=cs):
                    r0 = pl.multiple_of(i * SUB, SUB)
                    q_ = qs[pl.ds(r0, SUB), cs]
                    k_ = ks[pl.ds(r0, SUB), cs]
                    b_ = bs[pl.ds(r0, SUB), cs]
                    v_ = v_ref[pl.ds(r0, SUB), cs]
                    st = state[hh]

                    @pl.when(i % (CHUNK // SUB) == 0)
                    def _():
                        st_ref[hh, i // (CHUNK // SUB)] = st

                    bl = b_[SUB - 1:SUB, :]
                    o = _dot((q_ * jnp.exp2(b_)).astype(BF16), st.astype(BF16), NT)
                    (q_lo, q_hi), (b_lo, b_hi), (o_lo, o_hi) = ((x[:HALF], x[HALF:]) for x in (q_, b_, o))
                    for j in range(SUB):
                        bj, kj, vj = b_[j:j + 1, :], k_[j:j + 1, :], v_[j:j + 1, :]
                        if j < HALF:
                            e = _decay(b_lo, bj, rows, j)
                            o_lo = o_lo + jnp.sum(q_lo * e * kj, axis=-1, keepdims=True) * vj
                        e = _decay(b_hi, bj, rows, j - HALF if j > HALF else None)
                        o_hi = o_hi + jnp.sum(q_hi * e * kj, axis=-1, keepdims=True) * vj
                    os_[pl.ds(r0, HALF), cs] = o_lo
                    os_[pl.ds(r0 + HALF, HALF), cs] = o_hi
                    kt = (k_ * jnp.exp2(bl - b_)).astype(BF16)
                    state[hh] = st * jnp.exp2(bl) + _dot(v_.astype(BF16), kt, TN)
                    return carry

                lax.fori_loop(0, nsub, step, 0, unroll=unroll)

        for hh, cs in enumerate(cols):
            o = os_[:, cs]
            o_ref[:, cs] = o
            r = lax.rsqrt(jnp.mean(o * o, axis=-1, keepdims=True) + EPS)
            gp = g_ref[:, cs]
            oa_ref[:, cs] = (o * r * gn_ref[hh] * (gp * _sigmoid(gp))).astype(BF16)

    ng = HGRN_HEADS // hp
    col = lambda k: pl.BlockSpec((tc, hp * hd), lambda h, t: (t, k * ng + h))
    vec = pl.BlockSpec((hp, 1, hd), lambda h, t: (h, 0, 0))
    out = pl.BlockSpec((tc, hp * hd), lambda h, t: (t, h))
    x_specs, x_shapes, x_scratch = _plan_extras(plan)
    return pl.pallas_call(
        _fuse_exchange(body, 9, 3, 5, plan, 2), name="hgrn_fwd", grid=(ng, s // tc),
        in_specs=[col(0), col(1), col(2), col(3), vec, vec, vec, _const((tc, tc)), _const((tc, tc))] + x_specs,
        out_specs=[out, out, pl.BlockSpec((hp, tc // CHUNK, hd, hd), lambda h, t: (h, t, 0, 0))] + x_specs,
        out_shape=[jax.ShapeDtypeStruct((s, HGRN_WIDTH), F32), jax.ShapeDtypeStruct((s, HGRN_WIDTH), BF16),
                   jax.ShapeDtypeStruct((HGRN_HEADS, s // CHUNK, hd, hd), F32)] + x_shapes,
        scratch_shapes=[pltpu.VMEM((hp, hd, hd), F32)] + [pltpu.VMEM((tc, hp * hd), F32)] * 4 + x_scratch,
        compiler_params=_params(("arbitrary", "arbitrary")),
    )(z, z, z, z, l0, l1, gn, _block_tri(tc, SUB, False), _block_tri(tc, CHUNK, False), *plan_args)


def _pooled(p, ext, tok0):
    tm = p.shape[0]
    tok = tok0 + lax.broadcasted_iota(jnp.int32, (tm, 1), 0)
    outs = []
    for g, w in enumerate(POOL_WINDOWS):
        acc = ext[:, g * HEAD_DIM:(g + 1) * HEAD_DIM]
        sh = 1
        while sh < w:
            acc = acc + pltpu.roll(acc, sh, axis=0)
            sh *= 2
        cnt = jnp.minimum(tok + 1, w).astype(F32)
        outs.append(acc[POOL_HALO:, :] / cnt - p[:, g * HEAD_DIM:(g + 1) * HEAD_DIM])
    return outs


def _pool_fwd(z, wp, scale, tm):
    s = z.shape[0]
    pw = len(POOL_WINDOWS) * HEAD_DIM
    nb = tm // POOL_HALO

    def body(p_ref, prev_ref, wp_ref, sc_ref, ob_ref):
        t = pl.program_id(0)
        p = p_ref[...]
        prev = jnp.where(t > 0, prev_ref[...], 0.0)
        pooled = _pooled(p, jnp.concatenate([prev, p], axis=0), t * tm)
        ys = [_dot(pooled[g].astype(BF16), wp_ref[g].astype(BF16)) for g in range(len(POOL_WINDOWS))]
        ob_ref[...] = (jnp.concatenate(ys, axis=1) * sc_ref[...]).astype(BF16)

    return pl.pallas_call(
        body, name="pool_fwd", grid=(s // tm,),
        in_specs=[pl.BlockSpec((tm, pw), lambda t: (t, 4)),
                  pl.BlockSpec((POOL_HALO, pw), lambda t: (jnp.maximum(t * nb - 1, 0), 4)),
                  _const(wp.shape), _const((1, pw))],
        out_specs=pl.BlockSpec((tm, pw), lambda t: (t, 0)),
        out_shape=jax.ShapeDtypeStruct((s, pw), BF16),
        compiler_params=_params(("parallel",)),
    )(z, z, wp, scale)


def _kv_proj(mem, g, slab_g):
    m, d = mem.shape
    rows = d // N_CHIPS

    def body(mem_ref, g_ref, wk_ref, wv_ref, xk_ref, xv_ref):
        hm, _, _ = _rms_fwd(mem_ref[...], g_ref[...])
        hb = hm.astype(BF16)
        xk_ref[...] = _dot(hb, wk_ref[...].reshape(d, d)).astype(BF16)
        xv_ref[...] = _dot(hb, wv_ref[...].reshape(d, d)).astype(BF16)

    blk = lambda k: pl.BlockSpec((N_CHIPS, rows, d), lambda i: (0, k, 0))
    return pl.pallas_call(
        body, name="kv_proj", grid=(1,),
        in_specs=[_const((m, d)), _const((1, d)), blk(0), blk(1)],
        out_specs=[_const_out((m, d)), _const_out((m, d))],
        out_shape=[jax.ShapeDtypeStruct((m, d), BF16)] * 2,
        compiler_params=_params(("arbitrary",)),
    )(mem, g, slab_g, slab_g)


def _softmax_rows(sc):
    e = jnp.exp(sc - jnp.max(sc, axis=-1, keepdims=True))
    return e / jnp.sum(e, axis=-1, keepdims=True)


def _mix_xattn_fwd(x, oa, ob, gx, slab_g, wo_g, xk, xv, tm, plan=None, plan_args=()):
    s, d = x.shape
    m = xk.shape[0]
    rows = d // N_CHIPS
    hw = oa.shape[1]
    e = XATTN_HEAD_DIM

    def body(x_ref, oa_ref, ob_ref, gx_ref, wout_ref, wq_ref, wo_ref, xk_ref, xv_ref,
             x1_ref, mixed_ref, hq_ref, xq_ref, att_ref, x2_ref):
        mixed = jnp.concatenate([oa_ref[...], ob_ref[...]], axis=1)
        mixed_ref[...] = mixed
        x1 = x_ref[...] + _dot(mixed, wout_ref[...].reshape(d, d))
        x1_ref[...] = x1
        hq, _, _ = _rms_fwd(x1, gx_ref[...])
        hqb = hq.astype(BF16)
        hq_ref[...] = hqb
        xq = _dot(hqb, wq_ref[...].reshape(d, d)).astype(BF16)
        xq_ref[...] = xq
        atts = []
        for h in range(XATTN_HEADS):
            cs = slice(h * e, (h + 1) * e)
            p = _softmax_rows(_dot(xq[:, cs], xk_ref[:, cs], NT) * (e ** -0.5))
            atts.append(_dot(p.astype(BF16), xv_ref[:, cs]).astype(BF16))
        att = jnp.concatenate(atts, axis=1)
        att_ref[...] = att
        for j in range(N_CHIPS):
            x2_ref[:, j * rows:(j + 1) * rows] = x1[:, j * rows:(j + 1) * rows] + _dot(att, wo_ref[j])

    tile = lambda w: pl.BlockSpec((tm, w), lambda t: (t, 0))
    blk = lambda k: pl.BlockSpec((N_CHIPS, rows, d), lambda t: (0, k, 0), pipeline_mode=pl.Buffered(1))
    x_specs, x_shapes, x_scratch = _plan_extras(plan)
    return pl.pallas_call(
        _fuse_exchange(body, 9, 6, 0, plan, 1), name="mix_xattn_fwd", grid=(s // tm,),
        in_specs=[tile(d), tile(hw), tile(hw), _const((1, d)), blk(0), blk(1), _const(wo_g.shape),
                  _const((m, d)), _const((m, d))] + x_specs,
        out_specs=[tile(d)] * 6 + x_specs,
        out_shape=[jax.ShapeDtypeStruct((s, d), F32)] + [jax.ShapeDtypeStruct((s, d), BF16)] * 4
                  + [jax.ShapeDtypeStruct((s, d), F32)] + x_shapes,
        scratch_shapes=x_scratch,
        compiler_params=_params(("arbitrary",)),
    )(x, oa, ob, gx, slab_g, slab_g, wo_g, xk, xv, *plan_args)


def _mlp_loss_fwd(x2, gffn, gfin, w1_g, w2_g, target, tm):
    s, d = x2.shape
    wr = w1_g.shape[1]

    def body(x2_ref, gffn_ref, gfin_ref, w1_ref, w2_ref, tg_ref, a_ref, hf_ref, dx3_ref, dx3b_ref, st_ref):
        x2v = x2_ref[...]
        hf, _, _ = _rms_fwd(x2v, gffn_ref[...])
        hfb = hf.astype(BF16)
        hf_ref[...] = hfb
        acc = x2v
        for j in range(N_CHIPS):
            a = _dot(hfb, w1_ref[j])
            a_ref[:, j * wr:(j + 1) * wr] = a
            r = jnp.maximum(a, 0.0)
            acc = acc + _dot((r * r).astype(BF16), w2_ref[j])
        gf = gfin_ref[...]
        y, n, r3 = _rms_fwd(acc, gf)
        err = y - tg_ref[...]
        loss = 0.5 * jnp.sum(jnp.sum(err * err, axis=-1, keepdims=True) * (1.0 / d), axis=0, keepdims=True)
        dy = err * (1.0 / d)
        dx3, dgf = _rms_bwd(dy, n, r3, gf)
        dx3_ref[...] = dx3
        dx3b_ref[...] = dx3.astype(BF16)
        _acc_rows(st_ref, pl.program_id(0), [dgf, jnp.broadcast_to(loss, (1, d))])

    tile = lambda w: pl.BlockSpec((tm, w), lambda t: (t, 0))
    blk = lambda k: pl.BlockSpec((N_CHIPS, wr, d), lambda t: (0, k, 0), pipeline_mode=pl.Buffered(1))
    return pl.pallas_call(
        body, name="mlp_loss_fwd", grid=(s // tm,),
        in_specs=[tile(d), _const((1, d)), _const((1, d)), blk(0), blk(0), tile(d)],
        out_specs=[tile(N_CHIPS * wr), tile(d), tile(d), tile(d), _const_out((8, d))],
        out_shape=[jax.ShapeDtypeStruct((s, N_CHIPS * wr), F32), jax.ShapeDtypeStruct((s, d), BF16),
                   jax.ShapeDtypeStruct((s, d), F32), jax.ShapeDtypeStruct((s, d), BF16),
                   jax.ShapeDtypeStruct((8, d), F32)],
        compiler_params=_params(("arbitrary",)),
    )(x2, gffn, gfin, w1_g, w2_g, target)


def _mlp_bwd(dx3, dx3b, a, x2, gffn, w1_g, w2_g, tm):
    s, d = x2.shape
    wr = w1_g.shape[1]

    def body(dx3_ref, dx3b_ref, a_ref, x2_ref, g_ref, w1_ref, w2_ref, da_ref, u_ref, dx2_ref, dx2b_ref, st_ref):
        dyb = dx3b_ref[...]
        dhf = jnp.zeros((tm, d), F32)
        for j in range(N_CHIPS):
            r = jnp.maximum(a_ref[:, j * wr:(j + 1) * wr], 0.0)
            da = (_dot(dyb, w2_ref[j], NT) * (2.0 * r)).astype(BF16)
            da_ref[:, j * wr:(j + 1) * wr] = da
            u_ref[:, j * wr:(j + 1) * wr] = (r * r).astype(BF16)
            dhf = dhf + _dot(da, w1_ref[j], NT)
        g = g_ref[...]
        _, n, r2 = _rms_fwd(x2_ref[...], g)
        dxn, dg = _rms_bwd(dhf, n, r2, g)
        dx2 = dx3_ref[...] + dxn
        dx2_ref[...] = dx2
        dx2b_ref[...] = dx2.astype(BF16)
        _acc_rows(st_ref, pl.program_id(0), [dg])

    tile = lambda w: pl.BlockSpec((tm, w), lambda t: (t, 0))
    blk = lambda k: pl.BlockSpec((N_CHIPS, wr, d), lambda t: (0, k, 0), pipeline_mode=pl.Buffered(1))
    nf = N_CHIPS * wr
    return pl.pallas_call(
        body, name="mlp_bwd", grid=(s // tm,),
        in_specs=[tile(d), tile(d), tile(nf), tile(d), _const((1, d)), blk(0), blk(0)],
        out_specs=[tile(nf), tile(nf), tile(d), tile(d), _const_out((8, d))],
        out_shape=[jax.ShapeDtypeStruct((s, nf), BF16), jax.ShapeDtypeStruct((s, nf), BF16),
                   jax.ShapeDtypeStruct((s, d), F32), jax.ShapeDtypeStruct((s, d), BF16),
                   jax.ShapeDtypeStruct((8, d), F32)],
        compiler_params=_params(("arbitrary",)),
    )(dx3, dx3b, a, x2, gffn, w1_g, w2_g)


def _xattn_mix_bwd(dx2, x1, xq, xk, xv, gx, slab_g, wo_g, tm, plan=None, plan_args=()):
    s, d = x1.shape
    m = xk.shape[0]
    rows = d // N_CHIPS
    e = XATTN_HEAD_DIM

    def body(dx2_ref, x1_ref, xq_ref, xk_ref, xv_ref, gx_ref, wout_ref, wq_ref, wo_ref,
             dx1_ref, dx1b_ref, dxq_ref, dmix_ref, dxk_ref, dxv_ref, st_ref):
        t = pl.program_id(0)
        dx2 = dx2_ref[...]
        dx2b = dx2.astype(BF16)
        datt = jnp.zeros((tm, d), F32)
        for j in range(N_CHIPS):
            datt = datt + _dot(dx2b[:, j * rows:(j + 1) * rows], wo_ref[j], NT)
        dattb = datt.astype(BF16)
        dxqs, dxks, dxvs = [], [], []
        for h in range(XATTN_HEADS):
            cs = slice(h * e, (h + 1) * e)
            xq_h, xk_h, xv_h = xq_ref[:, cs], xk_ref[:, cs], xv_ref[:, cs]
            p = _softmax_rows(_dot(xq_h, xk_h, NT) * (e ** -0.5))
            dp = _dot(dattb[:, cs], xv_h, NT)
            ds = (p * (dp - jnp.sum(dp * p, axis=-1, keepdims=True)) * (e ** -0.5)).astype(BF16)
            dxqs.append(_dot(ds, xk_h).astype(BF16))
            dxks.append(_dot(ds, xq_h, TN))
            dxvs.append(_dot(p.astype(BF16), dattb[:, cs], TN))
        dxq = jnp.concatenate(dxqs, axis=1)
        dxq_ref[...] = dxq
        dxk = jnp.concatenate(dxks, axis=1)
        dxv = jnp.concatenate(dxvs, axis=1)

        @pl.when(t == 0)
        def _():
            dxk_ref[...] = dxk
            dxv_ref[...] = dxv

        @pl.when(t > 0)
        def _():
            dxk_ref[...] = dxk_ref[...] + dxk
            dxv_ref[...] = dxv_ref[...] + dxv

        dhq = jnp.concatenate([_dot(dxq, wq_ref[j], NT) for j in range(N_CHIPS)], axis=1)
        g = gx_ref[...]
        _, n, r1 = _rms_fwd(x1_ref[...], g)
        dxn, dg = _rms_bwd(dhq, n, r1, g)
        dx1 = dx2 + dxn
        dx1_ref[...] = dx1
        dx1b = dx1.astype(BF16)
        dx1b_ref[...] = dx1b
        for j in range(N_CHIPS):
            dmix_ref[:, j * rows:(j + 1) * rows] = _dot(dx1b, wout_ref[j], NT)
        _acc_rows(st_ref, t, [dg])

    tile = lambda: pl.BlockSpec((tm, d), lambda t: (t, 0))
    blk = lambda k: pl.BlockSpec((N_CHIPS, rows, d), lambda t: (0, k, 0), pipeline_mode=pl.Buffered(1))
    x_specs, x_shapes, x_scratch = _plan_extras(plan)
    return pl.pallas_call(
        _fuse_exchange(body, 9, 7, 0, plan, 1), name="xattn_mix_bwd", grid=(s // tm,),
        in_specs=[tile(), tile(), tile(), _const((m, d)), _const((m, d)), _const((1, d)), blk(0), blk(1),
                  _const(wo_g.shape)] + x_specs,
        out_specs=[tile(), tile(), tile(), tile(), _const_out((m, d)), _const_out((m, d)), _const_out((8, d))]
                  + x_specs,
        out_shape=[jax.ShapeDtypeStruct((s, d), F32), jax.ShapeDtypeStruct((s, d), BF16),
                   jax.ShapeDtypeStruct((s, d), BF16), jax.ShapeDtypeStruct((s, d), F32),
                   jax.ShapeDtypeStruct((m, d), F32), jax.ShapeDtypeStruct((m, d), F32),
                   jax.ShapeDtypeStruct((8, d), F32)] + x_shapes,
        scratch_shapes=x_scratch,
        compiler_params=_params(("arbitrary",)),
    )(dx2, x1, xq, xk, xv, gx, slab_g, slab_g, wo_g, *plan_args)


def _kv_bwd(mem, g, dxk, dxv, slab_g):
    m, d = mem.shape
    rows = d // N_CHIPS

    def body(mem_ref, g_ref, dxk_ref, dxv_ref, wk_ref, wv_ref, dwk_ref, dwv_ref, st_ref):
        gv = g_ref[...]
        hm, n, _ = _rms_fwd(mem_ref[...], gv)
        hb = hm.astype(BF16)
        dkb = dxk_ref[...].astype(BF16)
        dvb = dxv_ref[...].astype(BF16)
        dhm = []
        for j in range(N_CHIPS):
            hj = hb[:, j * rows:(j + 1) * rows]
            dwk_ref[j] = _dot(hj, dkb, TN)
            dwv_ref[j] = _dot(hj, dvb, TN)
            dhm.append(_dot(dkb, wk_ref[j], NT) + _dot(dvb, wv_ref[j], NT))
        dg = jnp.sum(jnp.concatenate(dhm, axis=1) * n, axis=0, keepdims=True)
        st_ref[...] = jnp.concatenate([dg, jnp.zeros((7, d), F32)], axis=0)

    blk = lambda k: pl.BlockSpec((N_CHIPS, rows, d), lambda i: (0, k, 0))
    return pl.pallas_call(
        body, name="kv_bwd", grid=(1,),
        in_specs=[_const((m, d)), _const((1, d)), _const((m, d)), _const((m, d)), blk(0), blk(1)],
        out_specs=[_const_out((N_CHIPS, rows, d)), _const_out((N_CHIPS, rows, d)), _const_out((8, d))],
        out_shape=[jax.ShapeDtypeStruct((N_CHIPS, rows, d), F32)] * 2 + [jax.ShapeDtypeStruct((8, d), F32)],
        compiler_params=_params(("arbitrary",)),
    )(mem, g, dxk, dxv, slab_g, slab_g)


def _pool_bwd(z, dmix, wp, scale, tm, plan=None, plan_args=()):
    s = z.shape[0]
    ng = len(POOL_WINDOWS)
    pw = ng * HEAD_DIM
    nb = tm // POOL_HALO
    nt = s // tm
    n_ext = tm + POOL_HALO

    def body(p_ref, prev_ref, dm_ref, dmn_ref, wp_ref, sc_ref, dp_ref, dwp_ref, st_ref):
        t = pl.program_id(0)
        p = p_ref[...]
        prev = jnp.where(t > 0, prev_ref[...], 0.0)
        pooled = _pooled(p, jnp.concatenate([prev, p], axis=0), t * tm)
        dm = dm_ref[...]
        dme = jnp.concatenate([dm, jnp.where(t < nt - 1, dmn_ref[...], 0.0)], axis=0) * sc_ref[...]
        tok = t * tm + lax.broadcasted_iota(jnp.int32, (n_ext, 1), 0)
        dsc, dps, dwps = [], [], []
        for g, w in enumerate(POOL_WINDOWS):
            cs = slice(g * HEAD_DIM, (g + 1) * HEAD_DIM)
            wpb = wp_ref[g].astype(BF16)
            pb = pooled[g].astype(BF16)
            dsc.append(jnp.sum(dm[:, cs] * _dot(pb, wpb), axis=0, keepdims=True))
            dye = dme[:, cs].astype(BF16)
            dwps.append(_dot(pb, dye[:tm], TN))
            dpe = _dot(dye, wpb, NT)
            acc = dpe / jnp.minimum(tok + 1, w).astype(F32)
            sh = 1
            while sh < w:
                acc = acc + pltpu.roll(acc, n_ext - sh, axis=0)
                sh *= 2
            dps.append(acc[:tm] - dpe[:tm])
        dp_ref[...] = jnp.concatenate(dps, axis=1)
        dsc_row = jnp.concatenate(dsc, axis=1)

        @pl.when(t == 0)
        def _():
            for g in range(ng):
                dwp_ref[g] = dwps[g]

        @pl.when(t > 0)
        def _():
            for g in range(ng):
                dwp_ref[g] = dwp_ref[g] + dwps[g]

        _acc_rows(st_ref, t, [dsc_row])

    x_specs, x_shapes, x_scratch = _plan_extras(plan)
    return pl.pallas_call(
        _fuse_exchange(body, 6, 3, 0, plan, 1), name="pool_bwd", grid=(nt,),
        in_specs=[pl.BlockSpec((tm, pw), lambda t: (t, 4)),
                  pl.BlockSpec((POOL_HALO, pw), lambda t: (jnp.maximum(t * nb - 1, 0), 4)),
                  pl.BlockSpec((tm, pw), lambda t: (t, 1)),
                  pl.BlockSpec((POOL_HALO, pw), lambda t: (jnp.minimum((t + 1) * nb, s // POOL_HALO - 1), 1)),
                  _const(wp.shape), _const((1, pw))] + x_specs,
        out_specs=[pl.BlockSpec((tm, pw), lambda t: (t, 0)), _const_out(wp.shape), _const_out((8, pw))] + x_specs,
        out_shape=[jax.ShapeDtypeStruct((s, pw), F32), jax.ShapeDtypeStruct(wp.shape, F32),
                   jax.ShapeDtypeStruct((8, pw), F32)] + x_shapes,
        scratch_shapes=x_scratch,
        compiler_params=_params(("arbitrary",)),
    )(z, z, dmix, dmix, wp, scale, *plan_args)


def _hgrn_bwd(z, o, dmix, st, l0, l1, gn, tc, unroll=1, plan=None, plan_args=()):
    s = z.shape[0]
    nsub = tc // SUB
    nt = s // tc
    hd = HEAD_DIM
    hp = HEADS_PER_STEP

    def body(q_ref, f_ref, v_ref, g_ref, l0_ref, l1_ref, gn_ref, o_ref, dm_ref, st_ref,
             tril_ref, triu_ref, trilc_ref, triuc_ref,
             dq_ref, df_ref, di_ref, dg_ref, stat_ref, dstate, qs, ks, bs, dos, dqs, dks, dbs, sts):
        t = pl.program_id(1)

        @pl.when(t == 0)
        def _():
            dstate[...] = jnp.zeros_like(dstate)

        cols = [slice(hh * hd, (hh + 1) * hd) for hh in range(hp)]
        heads = []
        for hh, cs in enumerate(cols):
            lb = _lower_bound(l0_ref[hh], l1_ref[hh])
            qp = q_ref[:, cs]
            q, sq, f, sf = _hgrn_gates(qp, f_ref[:, cs], lb)
            lf = jnp.log(f) * LOG2E
            o = o_ref[:, cs]
            r = lax.rsqrt(jnp.mean(o * o, axis=-1, keepdims=True) + EPS)
            n = o * r
            gnv = gn_ref[hh]
            gp = g_ref[:, cs]
            sg = _sigmoid(gp)
            dm = dm_ref[:, cs]
            dg_ref[:, cs] = dm * (n * gnv) * (sg * (1.0 + gp * (1.0 - sg)))
            don = dm * (gp * sg)
            dn = don * gnv
            heads.append(dict(lb=lb, qp=qp, q=q, sq=sq, f=f, sf=sf, k=1.0 - f, lf=lf,
                              bc=_group_cumsum(trilc_ref[...], lf), dgn=jnp.sum(don * n, axis=0, keepdims=True),
                              do=r * (dn - n * jnp.mean(dn * n, axis=-1, keepdims=True))))
        bounded = functools.reduce(jnp.minimum, [jnp.min(h["bc"]) for h in heads]) >= -MAX_LOG2_GROWTH

        def factored(hh, cs, q, k, bc, do_all):
            eb = jnp.exp2(bc)
            eib = jnp.exp2(-bc)
            qt = (q * eb).astype(BF16)
            ki = (k * eib).astype(BF16)
            vb = v_ref[:, cs].astype(BF16)
            dob = do_all.astype(BF16)
            mask = trilc_ref[...] > 0
            a = jnp.where(mask, _dot(qt, ki, NT), 0.0).astype(BF16)
            da = jnp.where(mask, _dot(dob, vb, NT), 0.0).astype(BF16)
            dq_in = _dot(da, ki)
            dk_in = _dot(da, qt, TN)
            dv_in = _dot(a, dob, TN)
            last_row = lax.broadcasted_iota(jnp.int32, (CHUNK, 1), 0) == CHUNK - 1
            for c in reversed(range(tc // CHUNK)):
                rs = slice(c * CHUNK, (c + 1) * CHUNK)
                stp = st_ref[hh, c]
                dst = dstate[hh]
                dstb = dst.astype(BF16)
                bl = bc[(c + 1) * CHUNK - 1:(c + 1) * CHUNK, :]
                ekl = jnp.exp2(bl - bc[rs])
                ebl = jnp.exp2(bl)
                kt = k[rs] * ekl
                dq_st = _dot(dob[rs], stp.astype(BF16)) * eb[rs]
                dkt = _dot(vb[rs], dstb)
                extra = jnp.sum(kt * dkt, axis=0, keepdims=True) + ebl * jnp.sum(stp * dst, axis=0, keepdims=True)
                dqs[rs, cs] = dq_st + dq_in[rs] * eb[rs]
                dks[rs, cs] = dkt * ekl + dk_in[rs] * eib[rs]
                di_ref[rs, cs] = _dot(kt.astype(BF16), dstb, NT) + dv_in[rs]
                dbs[rs, cs] = (q[rs] * dq_st - kt * dkt + jnp.where(last_row, extra, 0.0)
                               + (qt[rs].astype(F32) * dq_in[rs] - ki[rs].astype(F32) * dk_in[rs]))
                dstate[hh] = dst * ebl + _dot(dob[rs], qt[rs], TN)
            dbs[:, cs] = _group_cumsum(triuc_ref[...], dbs[:, cs])

        def exact(hh, cs, q, k, lf, do_all):
            qs[:, cs] = q
            ks[:, cs] = k
            bs[:, cs] = _group_cumsum(tril_ref[...], lf)
            dos[:, cs] = do_all
            per = CHUNK // SUB

            def restore(i, carry):
                @pl.when(i % per == 0)
                def _():
                    sts[i] = st_ref[hh, i // per]

                @pl.when(i % per != 0)
                def _():
                    rp = pl.multiple_of((i - 1) * SUB, SUB)
                    b_ = bs[pl.ds(rp, SUB), cs]
                    bl = b_[SUB - 1:SUB, :]
                    kt = (ks[pl.ds(rp, SUB), cs] * jnp.exp2(bl - b_)).astype(BF16)
                    sts[i] = sts[i - 1] * jnp.exp2(bl) + _dot(v_ref[pl.ds(rp, SUB), cs].astype(BF16), kt, TN)

                return carry

            lax.fori_loop(0, nsub, restore, 0)
            rows = lax.broadcasted_iota(jnp.int32, (HALF, 1), 0)
            last_row = lax.broadcasted_iota(jnp.int32, (SUB, 1), 0) == SUB - 1

            def step(i, carry):
                ii = nsub - 1 - i
                r0 = pl.multiple_of(ii * SUB, SUB)
                q_ = qs[pl.ds(r0, SUB), cs]
                k_ = ks[pl.ds(r0, SUB), cs]
                b_ = bs[pl.ds(r0, SUB), cs]
                v_ = v_ref[pl.ds(r0, SUB), cs]
                do_ = dos[pl.ds(r0, SUB), cs]
                stp = sts[ii]
                dst = dstate[hh]
                bl = b_[SUB - 1:SUB, :]
                eb = jnp.exp2(b_)
                ekl = jnp.exp2(bl - b_)
                ebl = jnp.exp2(bl)
                dob = do_.astype(BF16)
                dstb = dst.astype(BF16)
                kt = k_ * ekl
                dq = _dot(dob, stp.astype(BF16)) * eb
                dkt = _dot(v_.astype(BF16), dstb)
                dk = dkt * ekl
                dv = _dot(kt.astype(BF16), dstb, NT)
                extra = jnp.sum(kt * dkt, axis=0, keepdims=True) + ebl * jnp.sum(stp * dst, axis=0, keepdims=True)
                halves = lambda x: [x[:HALF], x[HALF:]]
                q_h, b_h, do_h, dq_h, dk_h, dv_h = (halves(x) for x in (q_, b_, do_, dq, dk, dv))
                for own in range(2):
                    dk_rows, dv_rows = _RowSums(rows), _RowSums(rows)
                    for jj in _RowSums.ORDER:
                        j = own * HALF + jj
                        bj, kj, vj = b_[j:j + 1, :], k_[j:j + 1, :], v_[j:j + 1, :]
                        dk_sum = dv_sum = None
                        for h in range(own, 2):
                            e = _decay(b_h[h], bj, rows, jj if h == own else None)
                            pe = q_h[h] * e
                            acol = jnp.sum(pe * kj, axis=-1, keepdims=True)
                            dacol = jnp.sum(do_h[h] * vj, axis=-1, keepdims=True)
                            dq_h[h] = dq_h[h] + dacol * (e * kj)
                            dk_sum = dacol * pe if dk_sum is None else dk_sum + dacol * pe
                            dv_sum = acol * do_h[h] if dv_sum is None else dv_sum + acol * do_h[h]
                        dk_rows.push(jj, dk_sum)
                        dv_rows.push(jj, dv_sum)
                    dk_h[own] = dk_h[own] + dk_rows.result()
                    dv_h[own] = dv_h[own] + dv_rows.result()
                dq, dk, dv = (jnp.concatenate(x, axis=0) for x in (dq_h, dk_h, dv_h))
                dqs[pl.ds(r0, SUB), cs] = dq
                dks[pl.ds(r0, SUB), cs] = dk
                di_ref[pl.ds(r0, SUB), cs] = dv
                dbs[pl.ds(r0, SUB), cs] = q_ * dq - k_ * dk + jnp.where(last_row, extra, 0.0)
                dstate[hh] = dst * ebl + _dot(dob, (q_ * eb).astype(BF16), TN)
                return carry

            lax.fori_loop(0, nsub, step, 0, unroll=unroll)
            dbs[:, cs] = _group_cumsum(triu_ref[...], dbs[:, cs])

        @pl.when(bounded)
        def _():
            for hh, cs in enumerate(cols):
                factored(hh, cs, heads[hh]["q"], heads[hh]["k"], heads[hh]["bc"], heads[hh]["do"])

        @pl.when(jnp.logical_not(bounded))
        def _():
            for hh, cs in enumerate(cols):
                exact(hh, cs, heads[hh]["q"], heads[hh]["k"], heads[hh]["lf"], heads[hh]["do"])

        for hh, cs in enumerate(cols):
            h = heads[hh]
            dfv = dbs[:, cs] / h["f"] - dks[:, cs]
            df_ref[:, cs] = dfv * (1.0 - h["lb"]) * h["sf"] * (1.0 - h["sf"])
            dlb = jnp.sum(dfv * (1.0 - h["sf"]), axis=0, keepdims=True)
            dq_ref[:, cs] = dqs[:, cs] * (h["sq"] * (1.0 + h["qp"] * (1.0 - h["sq"])))
            _acc_rows(stat_ref.at[hh], t, [h["dgn"], dlb])

    rev = lambda t: nt - 1 - t
    ng = HGRN_HEADS // hp
    col = lambda k: pl.BlockSpec((tc, hp * hd), lambda h, t: (rev(t), k * ng + h))
    vec = pl.BlockSpec((hp, 1, hd), lambda h, t: (h, 0, 0))
    head = pl.BlockSpec((tc, hp * hd), lambda h, t: (rev(t), h))
    x_specs, x_shapes, x_scratch = _plan_extras(plan)
    return pl.pallas_call(
        _fuse_exchange(body, 14, 5, 9, plan, 2), name="hgrn_bwd", grid=(ng, nt),
        in_specs=[col(0), col(1), col(2), col(3), vec, vec, vec, head, head,
                  pl.BlockSpec((hp, tc // CHUNK, hd, hd), lambda h, t: (h, rev(t), 0, 0))]
                 + [_const((tc, tc))] * 4 + x_specs,
        out_specs=[head, head, head, head, pl.BlockSpec((hp, 8, hd), lambda h, t: (h, 0, 0))] + x_specs,
        out_shape=[jax.ShapeDtypeStruct((s, HGRN_WIDTH), F32)] * 4 + [jax.ShapeDtypeStruct((HGRN_HEADS, 8, hd), F32)]
                  + x_shapes,
        scratch_shapes=[pltpu.VMEM((hp, hd, hd), F32)] + [pltpu.VMEM((tc, hp * hd), F32)] * 7
                       + [pltpu.VMEM((nsub, hd, hd), F32)] + x_scratch,
        compiler_params=_params(("arbitrary", "arbitrary")),
    )(z, z, z, z, l0, l1, gn, o, dmix, st, _block_tri(tc, SUB, False), _block_tri(tc, SUB, True),
      _block_tri(tc, CHUNK, False), _block_tri(tc, CHUNK, True), *plan_args)


def _in_bwd(dparts, dx1, x, g, win_g, tm, plan=None, plan_args=()):
    s, d = x.shape
    nsh, _, wc = win_g.shape
    pw = dparts[0].shape[1]

    def body(dq_ref, df_ref, di_ref, dg_ref, dp_ref, dx1_ref, x_ref, g_ref, w_ref, gx_ref, dz_ref, st_ref):
        dz = jnp.concatenate([dq_ref[...], df_ref[...], di_ref[...], dg_ref[...], dp_ref[...]], axis=1).astype(BF16)
        dz_ref[...] = dz
        dh = jnp.zeros((tm, d), F32)
        for j in range(nsh):
            dh = dh + _dot(dz[:, j * wc:(j + 1) * wc], w_ref[j], NT)
        gv = g_ref[...]
        _, n, r = _rms_fwd(x_ref[...], gv)
        dxn, dg = _rms_bwd(dh, n, r, gv)
        gx_ref[...] = dx1_ref[...] + dxn
        _acc_rows(st_ref, pl.program_id(0), [dg])

    tile = lambda w: pl.BlockSpec((tm, w), lambda t: (t, 0))
    x_specs, x_shapes, x_scratch = _plan_extras(plan)
    return pl.pallas_call(
        _fuse_exchange(body, 9, 3, 0, plan, 1), name="in_bwd", grid=(s // tm,),
        in_specs=[tile(pw)] * 5 + [tile(d), tile(d), _const((1, d)), _const(win_g.shape)] + x_specs,
        out_specs=[tile(d), tile(nsh * wc), _const_out((8, d))] + x_specs,
        out_shape=[jax.ShapeDtypeStruct((s, d), F32), jax.ShapeDtypeStruct((s, nsh * wc), BF16),
                   jax.ShapeDtypeStruct((8, d), F32)] + x_shapes,
        scratch_shapes=x_scratch,
        compiler_params=_params(("arbitrary",)),
    )(*dparts, dx1, x, g, win_g, *plan_args)


def _tn_grad(name, a, b, out_rows, out_cols, a_sharded, tr, tc, plan=None, plan_args=()):
    s = a.shape[0]
    nr, nc = out_rows // tr, out_cols // tc

    def body(a_ref, b_ref, o_ref):
        o_ref[...] = _dot(a_ref[...], b_ref[...], TN)

    a_map = (lambda j, i, k: (0, j * nr + i)) if a_sharded else (lambda j, i, k: (0, i))
    b_map = (lambda j, i, k: (0, k)) if a_sharded else (lambda j, i, k: (0, j * nc + k))
    x_specs, x_shapes, x_scratch = _plan_extras(plan)
    res = pl.pallas_call(
        _fuse_exchange(body, 2, 1, 0, plan, 3), name=name, grid=(N_CHIPS, nr, nc),
        in_specs=[pl.BlockSpec((s, tr), a_map), pl.BlockSpec((s, tc), b_map)] + x_specs,
        out_specs=[pl.BlockSpec((None, tr, tc), lambda j, i, k: (j, i, k))] + x_specs,
        out_shape=[jax.ShapeDtypeStruct((N_CHIPS, out_rows, out_cols), F32)] + x_shapes,
        scratch_shapes=x_scratch,
        compiler_params=_params(("arbitrary", "arbitrary", "arbitrary")),
    )(a, b, *plan_args)
    return res if plan else res[0]


FFN_NAMES = ("w_ff1", "w_ff2")
ATTN_NAMES = ("w_xo", "w_xq", "w_out", "w_xk", "w_xv")
EARLY_NAMES = FFN_NAMES + ATTN_NAMES
BIG_NAMES = EARLY_NAMES + ("w_in",)


def _halved(g):
    return g.reshape(N_CHIPS, 2, g.shape[1] // 2, g.shape[2])


def _pair_adds(names, gs, got, idx):
    pairs = [_grad_pair_add("grad_pair_add_" + k, g, r, idx, tr=min(256, g.shape[2])) for k, g, r in zip(names, gs, got)]
    return [p[0] for p in pairs], [p[1] for p in pairs]


def _step(x, mem, target, small, shards, idx):
    d = x.shape[1]
    l0 = small["lb_logits"][0].reshape(HGRN_HEADS, 1, HEAD_DIM)
    l1 = small["lb_logits"][1].reshape(HGRN_HEADS, 1, HEAD_DIM)
    gn = small["hgrn_norm_g"].reshape(HGRN_HEADS, 1, HEAD_DIM)
    wp = small["w_pool"].reshape(len(POOL_WINDOWS), HEAD_DIM, HEAD_DIM)
    psc = small["pool_scale"].reshape(1, -1)
    gmix, gx, gmem, gffn = (small[k].reshape(1, d) for k in ("norm_mix_g", "norm_x_g", "norm_mem_g", "norm_ffn_g"))
    gfin = small["final_norm_g"].reshape(1, d)

    (win_g,) = _run_exchange("gather_w_in", _WeightGather([shards["w_in"]]), [shards["w_in"]])
    z, h, kv_g = _in_proj(x, gmix, win_g, tm=512, plan=_WeightGather([shards["slab_kv"]]),
                          plan_args=[shards["slab_kv"]])
    mid_w = [shards["slab_oq"], shards["w_xo"], shards["w_ff1"]]
    o, oa, st, oq_g, wo_g, w1_g = _hgrn_fwd(z, l0, l1, gn, tc=256, unroll=8,
                                            plan=_WeightGather(mid_w), plan_args=mid_w)
    ob = _pool_fwd(z, wp, psc, tm=512)
    xk, xv = _kv_proj(mem, gmem, kv_g)
    late_w = [shards["w_ff2"]]
    x1, mixed, hq, xq, att, x2, w2_g = _mix_xattn_fwd(x, oa, ob, gx, oq_g, wo_g, xk, xv, tm=512,
                                                      plan=_WeightGather(late_w), plan_args=late_w)
    a, hf, dx3, dx3b, st_loss = _mlp_loss_fwd(x2, gffn, gfin, w1_g, w2_g, target, tm=512)

    da, u, dx2, dx2b, st_ffn = _mlp_bwd(dx3, dx3b, a, x2, gffn, w1_g, w2_g, tm=256)
    g_ff1 = [_halved(_tn_grad("dw_ff1", hf, da, d, d, False, 1024, 1024))]
    dw_ff2, *got = _tn_grad("dw_ff2", u, dx3b, d, d, True, 1024, 1024, plan=_PairExchange(g_ff1), plan_args=g_ff1)
    keep_ff1, send_ff1 = _pair_adds(("w_ff1",), g_ff1, got, idx)
    g_ff2 = [_halved(dw_ff2)]
    dx1, dx1b, dxq, dmix, dxk, dxv, st_x, *got = _xattn_mix_bwd(
        dx2, x1, xq, xk, xv, gx, oq_g, wo_g, tm=512,
        plan=_Plans([_ChipExchange(send_ff1), _PairExchange(g_ff2)]), plan_args=send_ff1 + g_ff2)
    recv_ff1 = got[:1]
    keep_ff2, send_ff2 = _pair_adds(("w_ff2",), g_ff2, got[1:], idx)
    dw = {}
    dw["w_xo"] = _tn_grad("dw_xo", att, dx2b, d, d // N_CHIPS, False, 1024, 256)
    dw["w_xq"] = _tn_grad("dw_xq", hq, dxq, d // N_CHIPS, d, True, 256, 1024)
    dw["w_out"] = _tn_grad("dw_out", mixed, dx1b, d // N_CHIPS, d, True, 256, 1024)
    dw["w_xk"], dw["w_xv"], st_mem = _kv_bwd(mem, gmem, dxk, dxv, kv_g)
    gs_attn = [_halved(dw[k]) for k in ATTN_NAMES]
    dp, d_wp, st_pool, *got_attn = _pool_bwd(z, dmix, wp, psc, tm=512, plan=_PairExchange(gs_attn), plan_args=gs_attn)
    keep_attn, send_attn = _pair_adds(ATTN_NAMES, gs_attn, got_attn, idx)
    sends = send_ff2 + send_attn
    dq, df, di, dg, st_hgrn, *received = _hgrn_bwd(z, o, dmix, st, l0, l1, gn, tc=256, unroll=4,
                                                    plan=_ChipExchange(sends), plan_args=sends)
    grad_x, dz, st_mix = _in_bwd([dq, df, di, dg, dp], dx1, x, gmix, win_g, tm=512)
    keeps = keep_ff1 + keep_ff2 + keep_attn
    received = recv_ff1 + list(received)
    gs_in = [_halved(_tn_grad("dw_in", h, dz, d, win_g.shape[2], False, 1024, win_g.shape[2]))]
    got_in = _run_exchange("grad_pair_exchange_w_in", _PairExchange(gs_in), gs_in)
    keep_in, send_in = _pair_adds(("w_in",), gs_in, got_in, idx)

    partials = dict(zip(EARLY_NAMES, zip(keeps, received)))
    stats = dict(mix=st_mix, x=st_x, mem=st_mem, ffn=st_ffn, loss=st_loss, hgrn=st_hgrn, pool=st_pool)
    return grad_x, stats, d_wp, partials, keep_in, send_in


def _place():
    x, y, c = lax.axis_index("x"), lax.axis_index("y"), lax.axis_index("c")
    return x, y, c, [(x, 1 - y), (1 - x, y), (1 - x, 1 - y)]


def _rcopy(src, dst, ssem, rsem, dev):
    return pltpu.make_async_remote_copy(src_ref=src, dst_ref=dst, send_sem=ssem, recv_sem=rsem,
                                        device_id=dev, device_id_type=MESH)


class _WeightGather:
    def __init__(self, shards):
        self.n = len(shards)
        self.rows = [w.shape[0] for w in shards]
        self.out_shape = [jax.ShapeDtypeStruct((N_CHIPS,) + w.shape, w.dtype) for w in shards]
        self.scratch_shapes = [pltpu.SemaphoreType.DMA((self.n,))] * 2 + [pltpu.SemaphoreType.DMA((self.n, 3))] * 4

    def _copies(self, ins, outs, sems, with_pass_on):
        lsem, lrsem, ssem, rsem, fsem, frsem = sems
        x, y, c, peers = _place()
        chip = 2 * x + y
        sib = (x, y, 1 - c)
        own = [_rcopy(ins[a], outs[a].at[chip], lsem.at[a], lrsem.at[a], sib) for a in range(self.n)]
        sends, arrived, passed, passed_in = [], [], [], []
        for a in range(self.n):
            hr = self.rows[a] // 2
            half = lambda who, hc, a=a, hr=hr: outs[a].at[who, pl.ds(hc * hr, hr), :]
            for r, (px, py) in enumerate(peers):
                pc = 2 * px + py
                sends.append(_rcopy(ins[a].at[pl.ds(c * hr, hr), :], half(chip, c), ssem.at[a, r], rsem.at[a, r],
                                    (px, py, c)))
                if with_pass_on:
                    arrived.append(_rcopy(half(pc, c), half(pc, c), ssem.at[a, r], rsem.at[a, r], (px, py, c)))
                    passed.append(_rcopy(half(pc, c), half(pc, c), fsem.at[a, r], frsem.at[a, r], sib))
                    passed_in.append(_rcopy(half(pc, 1 - c), half(pc, 1 - c), fsem.at[a, r], frsem.at[a, r], sib))
        return own, sends, arrived, passed, passed_in

    def start(self, ins, outs, sems):
        own, sends, _, _, _ = self._copies(ins, outs, sems, False)
        for cp in own + sends:
            cp.start()

    def finish(self, ins, outs, sems):
        own, sends, arrived, passed, passed_in = self._copies(ins, outs, sems, True)
        for got, fwd in zip(arrived, passed):
            got.wait_recv()
            fwd.start()
        for cp in passed_in:
            cp.wait_recv()
        for cp in sends + passed:
            cp.wait_send()
        for cp in own:
            cp.wait()


class _ChipExchange:
    def __init__(self, sends):
        self.n = len(sends)
        self.out_shape = [jax.ShapeDtypeStruct(g.shape, g.dtype) for g in sends]
        self.scratch_shapes = [pltpu.SemaphoreType.DMA((self.n, 3))] * 2

    def _copies(self, ins, outs, sems):
        ssem, rsem = sems
        _, _, c, peers = _place()
        return [_rcopy(ins[a].at[r], outs[a].at[r], ssem.at[a, r], rsem.at[a, r], (px, py, c))
                for a in range(self.n) for r, (px, py) in enumerate(peers)]

    def start(self, ins, outs, sems):
        for cp in self._copies(ins, outs, sems):
            cp.start()

    def finish(self, ins, outs, sems):
        for cp in self._copies(ins, outs, sems):
            cp.wait()


class _Plans:
    def __init__(self, plans):
        self.plans = plans
        self.n = sum(p.n for p in plans)
        self.out_shape = [s for p in plans for s in p.out_shape]
        self.scratch_shapes = [s for p in plans for s in p.scratch_shapes]

    def _each(self, ins, outs, sems):
        a = b = 0
        for p in self.plans:
            ns = len(p.scratch_shapes)
            yield p, ins[a:a + p.n], outs[a:a + p.n], sems[b:b + ns]
            a, b = a + p.n, b + ns

    def start(self, ins, outs, sems):
        for p, i, o, s in self._each(ins, outs, sems):
            p.start(i, o, s)

    def finish(self, ins, outs, sems):
        for p, i, o, s in self._each(ins, outs, sems):
            p.finish(i, o, s)


def _run_exchange(name, plan, arrays):
    n = plan.n

    def body(*refs):
        ins, outs, sems = refs[:n], refs[n:2 * n], refs[2 * n:]
        plan.start(ins, outs, sems)
        plan.finish(ins, outs, sems)

    return pl.pallas_call(
        body, name=name, in_specs=[ANY] * n, out_specs=[ANY] * n,
        out_shape=plan.out_shape, scratch_shapes=plan.scratch_shapes,
    )(*arrays)


class _PairExchange:
    def __init__(self, gs):
        self.n = len(gs)
        self.out_shape = [jax.ShapeDtypeStruct((g.shape[0],) + g.shape[2:], g.dtype) for g in gs]
        self.scratch_shapes = [pltpu.SemaphoreType.DMA((self.n,))] * 2

    def _copies(self, ins, outs, sems):
        ssem, rsem = sems
        x, y, c, _ = _place()
        return [_rcopy(ins[a].at[:, 1 - c], outs[a], ssem.at[a], rsem.at[a], (x, y, 1 - c)) for a in range(self.n)]

    def start(self, ins, outs, sems):
        for cp in self._copies(ins, outs, sems):
            cp.start()

    def finish(self, ins, outs, sems):
        for cp in self._copies(ins, outs, sems):
            cp.wait()


def _grad_pair_add(name, g, got, idx, tr):
    _, _, hr, cc = g.shape

    def body(idx_ref, g0, g1, g2, g3, r0, r1, r2, r3, keep_ref, send_ref):
        keep_ref[...] = g0[...] + r0[...]
        for q, (gq, rq) in enumerate(((g1, r1), (g2, r2), (g3, r3))):
            send_ref[q] = (gq[...] + rq[...]).astype(BF16)

    gspec = lambda q: pl.BlockSpec((None, None, tr, cc), lambda i, idx: (idx[1 + q], idx[0], i, 0))
    rspec = lambda q: pl.BlockSpec((None, tr, cc), lambda i, idx: (idx[1 + q], i, 0))
    return pl.pallas_call(
        body, name=name,
        grid_spec=pltpu.PrefetchScalarGridSpec(
            num_scalar_prefetch=1, grid=(hr // tr,),
            in_specs=[gspec(q) for q in range(4)] + [rspec(q) for q in range(4)],
            out_specs=[pl.BlockSpec((tr, cc), lambda i, idx: (i, 0)), pl.BlockSpec((3, tr, cc), lambda i, idx: (0, i, 0))]),
        out_shape=[jax.ShapeDtypeStruct((hr, cc), F32), jax.ShapeDtypeStruct((3, hr, cc), BF16)],
        compiler_params=_params(("parallel",)),
    )(idx, g, g, g, g, got, got, got, got)


def _grad_chip_add(name, keep, got, tr):
    hr, cc = keep.shape

    def body(k_ref, g_ref, o_ref):
        o_ref[...] = ((k_ref[...] + g_ref[0].astype(F32)) + g_ref[1].astype(F32)) + g_ref[2].astype(F32)

    return pl.pallas_call(
        body, name=name, grid=(hr // tr,),
        in_specs=[pl.BlockSpec((tr, cc), lambda i: (i, 0)), pl.BlockSpec((3, tr, cc), lambda i: (0, i, 0))],
        out_specs=pl.BlockSpec((tr, cc), lambda i: (i, 0)),
        out_shape=jax.ShapeDtypeStruct((hr, cc), F32),
        compiler_params=_params(("parallel",)),
    )(keep, got)


class _HalfExchange:
    def __init__(self, ts):
        self.n = len(ts)
        self.out_shape = [jax.ShapeDtypeStruct(t.shape, t.dtype) for t in ts]
        self.scratch_shapes = [pltpu.SemaphoreType.DMA((self.n,))] * 2

    def _copies(self, ins, outs, sems):
        ssem, rsem = sems
        x, y, c, _ = _place()
        return [_rcopy(ins[a], outs[a], ssem.at[a], rsem.at[a], (x, y, 1 - c)) for a in range(self.n)]

    def start(self, ins, outs, sems):
        for cp in self._copies(ins, outs, sems):
            cp.start()

    def finish(self, ins, outs, sems):
        for cp in self._copies(ins, outs, sems):
            cp.wait()


def _small_allreduce(stats, d_wp, plan, plan_args):
    d = D_MODEL
    half = d // 2
    wps = d_wp.shape
    n = plan.n

    def body(mix_ref, x_ref, mem_ref, ffn_ref, loss_ref, hg_ref, pool_ref, wp_ref, *refs):
        cin, (slab_out, wp_out), cout = refs[:n], refs[n:n + 2], refs[n + 2:2 * n + 2]
        slab_buf, wp_buf, sib_s, sib_w, ssem, rsem = refs[2 * n + 2:2 * n + 8]
        csem = refs[2 * n + 8:]
        plan.start(cin, cout, csem)
        x, y, c, peers = _place()
        chip = 2 * x + y
        sib = (x, y, 1 - c)
        hgn = jnp.concatenate([hg_ref[h, 0:1, :] for h in range(HGRN_HEADS)], axis=1)
        dlb = jnp.concatenate([hg_ref[h, 1:2, :] for h in range(HGRN_HEADS)], axis=1)
        slab_buf[0] = jnp.concatenate([
            mix_ref[0:1, :], x_ref[0:1, :], mem_ref[0:1, :], ffn_ref[0:1, :], loss_ref[0:1, :],
            jnp.concatenate([dlb, hgn], axis=1),
            jnp.concatenate([pool_ref[0:1, :], jnp.zeros((1, half), F32)], axis=1),
            loss_ref[1:2, :]], axis=0)
        wp_buf[0] = wp_ref[...]
        pair = [_rcopy(slab_buf.at[0], sib_s, ssem.at[0], rsem.at[0], sib),
                _rcopy(wp_buf.at[0], sib_w, ssem.at[1], rsem.at[1], sib)]
        for cp in pair:
            cp.start()
        for cp in pair:
            cp.wait()
        slab_buf[0] = slab_buf[0] + sib_s[...]
        wp_buf[0] = wp_buf[0] + sib_w[...]
        cps = []
        for r, (px, py) in enumerate(peers):
            cps.append(_rcopy(slab_buf.at[0], slab_buf.at[r + 1], ssem.at[2 + 2 * r], rsem.at[2 + 2 * r], (px, py, c)))
            cps.append(_rcopy(wp_buf.at[0], wp_buf.at[r + 1], ssem.at[3 + 2 * r], rsem.at[3 + 2 * r], (px, py, c)))
        for cp in cps:
            cp.start()
        for cp in cps:
            cp.wait()
        tot_s, tot_w = slab_buf[chip], wp_buf[chip]
        for j in range(1, N_CHIPS):
            tot_s = tot_s + slab_buf[jnp.bitwise_xor(j, chip)]
            tot_w = tot_w + wp_buf[jnp.bitwise_xor(j, chip)]
        slab_out[...] = tot_s
        wp_out[...] = tot_w
        plan.finish(cin, cout, csem)

    return pl.pallas_call(
        body, name="small_allreduce",
        in_specs=[VMEM] * 8 + [ANY] * n, out_specs=[VMEM] * 2 + [ANY] * n,
        out_shape=[jax.ShapeDtypeStruct((8, d), F32), jax.ShapeDtypeStruct(wps, F32)] + list(plan.out_shape),
        scratch_shapes=[pltpu.VMEM((N_CHIPS, 8, d), F32), pltpu.VMEM((N_CHIPS,) + wps, F32),
                        pltpu.VMEM((8, d), F32), pltpu.VMEM(wps, F32),
                        pltpu.SemaphoreType.DMA((8,)), pltpu.SemaphoreType.DMA((8,))] + list(plan.scratch_shapes),
    )(stats["mix"], stats["x"], stats["mem"], stats["ffn"], stats["loss"], stats["hgrn"], stats["pool"], d_wp,
      *plan_args)


def _adamw_math(w, g, m, v):
    m = ADAM_B1 * m + (1.0 - ADAM_B1) * g
    v = ADAM_B2 * v + (1.0 - ADAM_B2) * (g * g)
    m_hat = m / (1.0 - ADAM_B1 ** ADAM_STEP)
    v_hat = v / (1.0 - ADAM_B2 ** ADAM_STEP)
    delta = -ADAM_LR * (m_hat / (jnp.sqrt(v_hat) + ADAM_EPS) + ADAM_WD * w)
    return delta, m, v


def _adamw(name, mine, theirs, w, m, v, idx, tr):
    rows = w.shape[0]
    cc = mine.shape[1]
    nb = rows // 2 // tr
    heads = w.shape[1] if w.ndim == 3 else 1
    e = cc // heads

    def body(idx_ref, a_ref, b_ref, w_ref, m_ref, v_ref, g_out, d_out, m_out, v_out):
        g = jnp.where(pl.program_id(0) // nb == idx_ref[0], a_ref[...], b_ref[...])
        if w.ndim == 2:
            g_out[...] = g
            d_out[...], m_out[...], v_out[...] = _adamw_math(w_ref[...], g, m_ref[...], v_ref[...])
        else:
            for h in range(heads):
                gh = g[:, h * e:(h + 1) * e]
                g_out[:, h, :] = gh
                d_out[:, h, :], m_out[:, h, :], v_out[:, h, :] = _adamw_math(
                    w_ref[:, h, :], gh, m_ref[:, h, :], v_ref[:, h, :])

    hspec = pl.BlockSpec((tr, cc), lambda i, idx: (i % nb, 0))
    spec = pl.BlockSpec((tr,) + w.shape[1:], lambda i, idx: (i,) + (0,) * (w.ndim - 1))
    return pl.pallas_call(
        body, name=name,
        grid_spec=pltpu.PrefetchScalarGridSpec(
            num_scalar_prefetch=1, grid=(rows // tr,),
            in_specs=[hspec, hspec, spec, spec, spec], out_specs=[spec] * 4),
        out_shape=[jax.ShapeDtypeStruct(w.shape, F32)] * 4,
        compiler_params=_params(("parallel",)),
    )(idx, mine, theirs, w, m, v)


SMALL_NAMES = ("norm_mix_g", "lb_logits", "hgrn_norm_g", "w_pool", "pool_scale", "norm_x_g", "norm_mem_g",
               "norm_ffn_g", "final_norm_g")


def _small_update(slab, d_wp, ws, ms, vs):
    n = len(SMALL_NAMES)
    half = D_MODEL // 2

    def body(slab_ref, wp_ref, *refs):
        w_refs, m_refs, v_refs, outs = refs[:n], refs[n:2 * n], refs[2 * n:3 * n], refs[3 * n:]
        row = lambda k: slab_ref[k:k + 1, :]
        lbl = w_refs[SMALL_NAMES.index("lb_logits")][...]
        s0 = _lower_bound(lbl[0:1, :], lbl[1:2, :])
        dl0 = row(ROW_LB_HGN)[:, :half] * s0 * (1.0 - s0)
        grads = dict(norm_mix_g=row(ROW_GMIX), lb_logits=jnp.concatenate([dl0, -dl0], axis=0),
                     hgrn_norm_g=row(ROW_LB_HGN)[:, half:], w_pool=wp_ref[...], pool_scale=row(ROW_PSCALE)[:, :half],
                     norm_x_g=row(ROW_GX), norm_mem_g=row(ROW_GMEM), norm_ffn_g=row(ROW_GFFN),
                     final_norm_g=row(ROW_GFIN))
        outs[0][...] = row(ROW_LOSS)[:, :128]
        for i, name in enumerate(SMALL_NAMES):
            g = grads[name]
            delta, m2, v2 = _adamw_math(w_refs[i][...], g, m_refs[i][...], v_refs[i][...])
            for o, val in zip(outs[1 + 4 * i:5 + 4 * i], (g, delta, m2, v2)):
                o[...] = val

    args = [ws[k] for k in SMALL_NAMES] + [ms[k] for k in SMALL_NAMES] + [vs[k] for k in SMALL_NAMES]
    out_shape = [jax.ShapeDtypeStruct((1, 128), F32)]
    for k in SMALL_NAMES:
        out_shape += [jax.ShapeDtypeStruct(ws[k].shape, F32)] * 4
    res = pl.pallas_call(
        body, name="small_update",
        in_specs=[VMEM] * (2 + 3 * n), out_specs=[VMEM] * len(out_shape), out_shape=out_shape,
    )(slab, d_wp, *args)
    return res[0], {k: res[1 + 4 * i:5 + 4 * i] for i, k in enumerate(SMALL_NAMES)}


ALL_NAMES = ("norm_mix_g", "w_in", "lb_logits", "hgrn_norm_g", "w_pool", "pool_scale", "w_out", "norm_x_g",
             "norm_mem_g", "w_xq", "w_xk", "w_xv", "w_xo", "norm_ffn_g", "w_ff1", "w_ff2", "final_norm_g")


def _shard_2d(name, a):
    a = a[0]
    if name in ("w_xq", "w_xk", "w_xv"):
        return a.reshape(a.shape[0], -1)
    if name == "w_xo":
        return a.reshape(-1, a.shape[-1])
    return a


def _small_2d(name, a):
    if name == "w_pool":
        return a.reshape(-1, HEAD_DIM)
    if name == "lb_logits":
        return a
    return a.reshape(1, -1)


def kernel(x, mem, norm_mix_g, w_in, lb_logits, hgrn_norm_g, w_pool, pool_scale, w_out, norm_x_g, norm_mem_g, w_xq, w_xk, w_xv, w_xo, norm_ffn_g, w_ff1, w_ff2, final_norm_g, loss_target, m_norm_mix_g, m_w_in, m_lb_logits, m_hgrn_norm_g, m_w_pool, m_pool_scale, m_w_out, m_norm_x_g, m_norm_mem_g, m_w_xq, m_w_xk, m_w_xv, m_w_xo, m_norm_ffn_g, m_w_ff1, m_w_ff2, m_final_norm_g, v_norm_mix_g, v_w_in, v_lb_logits, v_hgrn_norm_g, v_w_pool, v_pool_scale, v_w_out, v_norm_x_g, v_norm_mem_g, v_w_xq, v_w_xk, v_w_xv, v_w_xo, v_norm_ffn_g, v_w_ff1, v_w_ff2, v_final_norm_g):
    w = dict(norm_mix_g=norm_mix_g, w_in=w_in, lb_logits=lb_logits, hgrn_norm_g=hgrn_norm_g, w_pool=w_pool, pool_scale=pool_scale, w_out=w_out, norm_x_g=norm_x_g, norm_mem_g=norm_mem_g, w_xq=w_xq, w_xk=w_xk, w_xv=w_xv, w_xo=w_xo, norm_ffn_g=norm_ffn_g, w_ff1=w_ff1, w_ff2=w_ff2, final_norm_g=final_norm_g)
    m = dict(norm_mix_g=m_norm_mix_g, w_in=m_w_in, lb_logits=m_lb_logits, hgrn_norm_g=m_hgrn_norm_g, w_pool=m_w_pool, pool_scale=m_pool_scale, w_out=m_w_out, norm_x_g=m_norm_x_g, norm_mem_g=m_norm_mem_g, w_xq=m_w_xq, w_xk=m_w_xk, w_xv=m_w_xv, w_xo=m_w_xo, norm_ffn_g=m_norm_ffn_g, w_ff1=m_w_ff1, w_ff2=m_w_ff2, final_norm_g=m_final_norm_g)
    v = dict(norm_mix_g=v_norm_mix_g, w_in=v_w_in, lb_logits=v_lb_logits, hgrn_norm_g=v_hgrn_norm_g, w_pool=v_w_pool, pool_scale=v_pool_scale, w_out=v_w_out, norm_x_g=v_norm_x_g, norm_mem_g=v_norm_mem_g, w_xq=v_w_xq, w_xk=v_w_xk, w_xv=v_w_xv, w_xo=v_w_xo, norm_ffn_g=v_norm_ffn_g, w_ff1=v_w_ff1, w_ff2=v_w_ff2, final_norm_g=v_final_norm_g)

    big_w = {k: _shard_2d(k, w[k]) for k in BIG_NAMES}
    slab_oq = jnp.concatenate([big_w["w_out"], big_w["w_xq"]], axis=0).astype(BF16)
    slab_kv = jnp.concatenate([big_w["w_xk"], big_w["w_xv"]], axis=0).astype(BF16)
    shards = dict(slab_oq=slab_oq, slab_kv=slab_kv,
                  **{k: big_w[k].astype(BF16) for k in ("w_in", "w_xo", "w_ff1", "w_ff2")})

    cx, cy, cc = lax.axis_index("x"), lax.axis_index("y"), lax.axis_index("c")
    chip = 2 * cx + cy
    idx = jnp.stack([cc, chip, chip ^ 1, chip ^ 2, chip ^ 3]).astype(jnp.int32)
    small = {k: w[k] for k in SMALL_NAMES}
    grad_x, stats, d_wp, partials, keep_in, send_in = _step(x[0], mem[0], loss_target[0], small, shards, idx)

    chip_add = lambda k, keep, got: _grad_chip_add("grad_chip_add_" + k, keep, got, tr=min(256, keep.shape[0]))
    halves = {k: chip_add(k, *partials[k]) for k in EARLY_NAMES}
    early = [halves[k] for k in EARLY_NAMES]
    slab_sum, wp_sum, recv_in, *their_early = _small_allreduce(
        stats, d_wp.reshape(-1, HEAD_DIM), _Plans([_ChipExchange(send_in), _HalfExchange(early)]), send_in + early)
    halves["w_in"] = chip_add("w_in", keep_in[0], recv_in)
    theirs = dict(zip(EARLY_NAMES, their_early))
    (theirs["w_in"],) = _run_exchange("grad_half_exchange_w_in", _HalfExchange([halves["w_in"]]), [halves["w_in"]])

    grads, deltas, new_m, new_v = {}, {}, {}, {}
    for k in BIG_NAMES:
        as_held = (lambda a: a[0]) if k in ("w_xq", "w_xk", "w_xv") else functools.partial(_shard_2d, k)
        res = _adamw("adamw_" + k, halves[k], theirs[k], as_held(w[k]), as_held(m[k]), as_held(v[k]), idx,
                     tr=min(256, halves[k].shape[0]))
        for store, val in zip((grads, deltas, new_m, new_v), res):
            store[k] = val.reshape(w[k].shape)

    loss, upd = _small_update(slab_sum, wp_sum, {k: _small_2d(k, w[k]) for k in SMALL_NAMES},
                              {k: _small_2d(k, m[k]) for k in SMALL_NAMES}, {k: _small_2d(k, v[k]) for k in SMALL_NAMES})
    for k in SMALL_NAMES:
        for store, val in zip((grads, deltas, new_m, new_v), upd[k]):
            store[k] = val.reshape(w[k].shape)

    return (loss[0, 0], grad_x[None], *[grads[k] for k in ALL_NAMES], *[deltas[k] for k in ALL_NAMES],
            *[new_m[k] for k in ALL_NAMES], *[new_v[k] for k in ALL_NAMES])
```

```python
import functools

import jax
import jax.numpy as jnp
from jax import lax
from jax.experimental import pallas as pl
from jax.experimental.pallas import tpu as pltpu

F32 = jnp.float32
BF16 = jnp.bfloat16
LOG2E = 1.4426950408889634
NEG_BIG = -1e30
MAX_LOG2_GROWTH = 100.0
MESH = pl.DeviceIdType.MESH
ANY = pl.BlockSpec(memory_space=pl.ANY)
VMEM = pl.BlockSpec(memory_space=pltpu.VMEM)

D_MODEL = 1024
N_CHIPS = 4
HGRN_HEADS = 4
HEAD_DIM = 128
HGRN_WIDTH = HGRN_HEADS * HEAD_DIM
POOL_WINDOWS = (2, 4, 8, 16)
POOL_HALO = 16
SUB = 16
HALF = SUB // 2
CHUNK = 64
HEADS_PER_STEP = 4
XATTN_HEADS = 4
XATTN_HEAD_DIM = 256
EPS = 1e-6
ADAM_LR, ADAM_B1, ADAM_B2, ADAM_EPS, ADAM_WD, ADAM_STEP = 0.001, 0.9, 0.999, 1e-08, 0.01, 10

TOKEN_TILE = 512
MLP_BWD_TOKEN_TILE = 256
HGRN_BLOCK = 256
GRAD_TILE = 1024

V7X_VMEM_BYTES = 64 * 1024 * 1024
VMEM_LIMIT = V7X_VMEM_BYTES - 8 * 1024 * 1024

NN = (((1,), (0,)), ((), ()))
NT = (((1,), (1,)), ((), ()))
TN = (((0,), (0,)), ((), ()))

ROW_GMIX, ROW_GX, ROW_GMEM, ROW_GFFN, ROW_GFIN, ROW_LB_HGN, ROW_PSCALE, ROW_LOSS = range(8)


def _dot(a, b, dims=NN):
    return lax.dot_general(a, b, dims, preferred_element_type=F32)


def _sigmoid(x):
    return 1.0 / (1.0 + jnp.exp(-x))


def _rms_fwd(x, g):
    r = lax.rsqrt(jnp.mean(x * x, axis=-1, keepdims=True) + EPS)
    n = x * r
    return n * g, n, r


def _rms_bwd(dh, n, r, g):
    dn = dh * g
    dx = r * (dn - n * jnp.mean(dn * n, axis=-1, keepdims=True))
    return dx, jnp.sum(dh * n, axis=0, keepdims=True)


def _params(sem=None):
    return pltpu.CompilerParams(dimension_semantics=sem, vmem_limit_bytes=VMEM_LIMIT)


def _const(shape):
    nd = len(shape)
    return pl.BlockSpec(shape, lambda *_: (0,) * nd, pipeline_mode=pl.Buffered(1))


def _const_out(shape):
    nd = len(shape)
    return pl.BlockSpec(shape, lambda *_: (0,) * nd)


def _acc_rows(ref, t, rows):
    upd = jnp.concatenate(rows + [jnp.zeros((8 - len(rows), rows[0].shape[1]), F32)], axis=0)

    @pl.when(t == 0)
    def _():
        ref[...] = upd

    @pl.when(t > 0)
    def _():
        ref[...] = ref[...] + upd


def _fuse_exchange(body, n_in, n_out, n_scratch, plan, ndim):
    if plan is None:
        return body
    n = plan.n

    def wrapped(*refs):
        ins, cin = refs[:n_in], refs[n_in:n_in + n]
        outs, cout = refs[n_in + n:n_in + n + n_out], refs[n_in + n + n_out:n_in + 2 * n + n_out]
        rest = refs[n_in + 2 * n + n_out:]
        scr, csem = rest[:n_scratch], rest[n_scratch:]
        first = pl.program_id(0) == 0
        last = pl.program_id(0) == pl.num_programs(0) - 1
        for i in range(1, ndim):
            first = first & (pl.program_id(i) == 0)
            last = last & (pl.program_id(i) == pl.num_programs(i) - 1)

        @pl.when(first)
        def _():
            plan.start(cin, cout, csem)

        body(*ins, *outs, *scr)

        @pl.when(last)
        def _():
            plan.finish(cin, cout, csem)

    return wrapped


def _plan_extras(plan):
    if plan is None:
        return [], [], []
    return [ANY] * plan.n, list(plan.out_shape), list(plan.scratch_shapes)


def _in_proj(x, g, win_g, tm, plan=None, plan_args=()):
    s, d = x.shape
    nsh, _, wc = win_g.shape

    def body(x_ref, g_ref, w_ref, z_ref, h_ref):
        h, _, _ = _rms_fwd(x_ref[...], g_ref[...])
        hb = h.astype(BF16)
        h_ref[...] = hb
        for j in range(nsh):
            z_ref[:, j * wc:(j + 1) * wc] = _dot(hb, w_ref[j])

    x_specs, x_shapes, x_scratch = _plan_extras(plan)
    return pl.pallas_call(
        _fuse_exchange(body, 3, 2, 0, plan, 1), name="in_proj", grid=(s // tm,),
        in_specs=[pl.BlockSpec((tm, d), lambda t: (t, 0)), _const((1, d)), _const((nsh, d, wc))] + x_specs,
        out_specs=[pl.BlockSpec((tm, nsh * wc), lambda t: (t, 0)), pl.BlockSpec((tm, d), lambda t: (t, 0))] + x_specs,
        out_shape=[jax.ShapeDtypeStruct((s, nsh * wc), F32), jax.ShapeDtypeStruct((s, d), BF16)] + x_shapes,
        scratch_shapes=x_scratch,
        compiler_params=_params(("arbitrary",)),
    )(x, g, win_g, *plan_args)


def _lower_bound(l0, l1):
    m = jnp.maximum(l0, l1)
    e0, e1 = jnp.exp(l0 - m), jnp.exp(l1 - m)
    return e0 / (e0 + e1)


def _block_tri(n, group, upper):
    r = lax.broadcasted_iota(jnp.int32, (n, n), 0)
    c = lax.broadcasted_iota(jnp.int32, (n, n), 1)
    keep = (r // group == c // group) & ((c >= r) if upper else (c <= r))
    return keep.astype(BF16)


def _group_cumsum(tri, x):
    hi = x.astype(BF16)
    rest = x - hi.astype(F32)
    mid = rest.astype(BF16)
    lo = (rest - mid.astype(F32)).astype(BF16)
    return (_dot(tri, hi) + _dot(tri, mid)) + _dot(tri, lo)


def _decay(b, bj, rows, first):
    d = b - bj
    if first:
        d = jnp.where(rows >= first, d, NEG_BIG)
    return jnp.exp2(d)


class _RowSums:
    ORDER = (0, 4, 2, 6, 1, 5, 3, 7)

    def __init__(self, rows):
        self.rows = rows
        self.level = {4: {}, 2: {}, 1: {}}

    def _pair(self, p, q, d):
        return jnp.where((self.rows & d) != 0, p + pltpu.roll(p, d, axis=0), q + pltpu.roll(q, HALF - d, axis=0))

    def push(self, j, y, d=4):
        if d == 0:
            self.out = y
            return
        slot = self.level[d]
        key = j % d
        if key not in slot:
            slot[key] = (j, y)
            return
        j0, y0 = slot.pop(key)
        p, q = (y, y0) if j & d else (y0, y)
        self.push(key, self._pair(p, q, d), d // 2)

    def result(self):
        return self.out


def _hgrn_gates(qp, fp, lb):
    sq = _sigmoid(qp)
    sf = _sigmoid(fp)
    f = lb + (1.0 - lb) * sf
    return qp * sq, sq, f, sf


def _hgrn_fwd(z, l0, l1, gn, tc, unroll=1, plan=None, plan_args=()):
    s = z.shape[0]
    nsub = tc // SUB
    hd = HEAD_DIM
    hp = HEADS_PER_STEP

    def body(q_ref, f_ref, v_ref, g_ref, l0_ref, l1_ref, gn_ref, tri_ref, tric_ref, o_ref, oa_ref, st_ref,
             state, qs, ks, bs, os_):
        @pl.when(pl.program_id(1) == 0)
        def _():
            state[...] = jnp.zeros_like(state)

        cols = [slice(hh * hd, (hh + 1) * hd) for hh in range(hp)]
        q, k, lf, bc = [], [], [], []
        for hh, cs in enumerate(cols):
            qh, _, fh, _ = _hgrn_gates(q_ref[:, cs], f_ref[:, cs], _lower_bound(l0_ref[hh], l1_ref[hh]))
            q.append(qh)
            k.append(1.0 - fh)
            lf.append(jnp.log(fh) * LOG2E)
            bc.append(_group_cumsum(tric_ref[...], lf[hh]))
        bounded = functools.reduce(jnp.minimum, [jnp.min(b) for b in bc]) >= -MAX_LOG2_GROWTH

        @pl.when(bounded)
        def _():
            mask = tric_ref[...] > 0
            for hh, cs in enumerate(cols):
                qt = (q[hh] * jnp.exp2(bc[hh])).astype(BF16)
                ki = (k[hh] * jnp.exp2(-bc[hh])).astype(BF16)
                vb = v_ref[:, cs].astype(BF16)
                a = jnp.where(mask, _dot(qt, ki, NT), 0.0).astype(BF16)
                o_in = _dot(a, vb)
                for c in range(tc // CHUNK):
                    rs = slice(c * CHUNK, (c + 1) * CHUNK)
                    st = state[hh]
                    st_ref[hh, c] = st
                    os_[rs, cs] = o_in[rs] + _dot(qt[rs], st.astype(BF16), NT)
                    bl = bc[hh][(c + 1) * CHUNK - 1:(c + 1) * CHUNK, :]
                    kt = (k[hh][rs] * jnp.exp2(bl - bc[hh][rs])).astype(BF16)
                    state[hh] = st * jnp.exp2(bl) + _dot(vb[rs], kt, TN)

        @pl.when(jnp.logical_not(bounded))
        def _():
            rows = lax.broadcasted_iota(jnp.int32, (HALF, 1), 0)
            for hh, cs in enumerate(cols):
                qs[:, cs] = q[hh]
                ks[:, cs] = k[hh]
                bs[:, cs] = _group_cumsum(tri_ref[...], lf[hh])

                def step(i, carry, hh=hh, cs=cs):
                    r0 = pl.multiple_of(i * SUB, SUB)
                    q_ = qs[pl.ds(r0, SUB), cs]
                    k_ = ks[pl.ds(r0, SUB), cs]
                    b_ = bs[pl.ds(r0, SUB), cs]
                    v_ = v_ref[pl.ds(r0, SUB), cs]
                    st = state[hh]

                    @pl.when(i % (CHUNK // SUB) == 0)
                    def _():
                        st_ref[hh, i // (CHUNK // SUB)] = st

                    bl = b_[SUB - 1:SUB, :]
                    o = _dot((q_ * jnp.exp2(b_)).astype(BF16), st.astype(BF16), NT)
                    (q_lo, q_hi), (b_lo, b_hi), (o_lo, o_hi) = ((x[:HALF], x[HALF:]) for x in (q_, b_, o))
                    for j in range(SUB):
                        bj, kj, vj = b_[j:j + 1, :], k_[j:j + 1, :], v_[j:j + 1, :]
                        if j < HALF:
                            e = _decay(b_lo, bj, rows, j)
                            o_lo = o_lo + jnp.sum(q_lo * e * kj, axis=-1, keepdims=True) * vj
                        e = _decay(b_hi, bj, rows, j - HALF if j > HALF else None)
                        o_hi = o_hi + jnp.sum(q_hi * e * kj, axis=-1, keepdims=True) * vj
                    os_[pl.ds(r0, HALF), cs] = o_lo
                    os_[pl.ds(r0 + HALF, HALF), cs] = o_hi
                    kt = (k_ * jnp.exp2(bl - b_)).astype(BF16)
                    state[hh] = st * jnp.exp2(bl) + _dot(v_.astype(BF16), kt, TN)
                    return carry

                lax.fori_loop(0, nsub, step, 0, unroll=unroll)

        for hh, cs in enumerate(cols):
            o = os_[:, cs]
            o_ref[:, cs] = o
            r = lax.rsqrt(jnp.mean(o * o, axis=-1, keepdims=True) + EPS)
            gp = g_ref[:, cs]
            oa_ref[:, cs] = (o * r * gn_ref[hh] * (gp * _sigmoid(gp))).astype(BF16)

    ng = HGRN_HEADS // hp
    col = lambda k: pl.BlockSpec((tc, hp * hd), lambda h, t: (t, k * ng + h))
    vec = pl.BlockSpec((hp, 1, hd), lambda h, t: (h, 0, 0))
    out = pl.BlockSpec((tc, hp * hd), lambda h, t: (t, h))
    x_specs, x_shapes, x_scratch = _plan_extras(plan)
    return pl.pallas_call(
        _fuse_exchange(body, 9, 3, 5, plan, 2), name="hgrn_fwd", grid=(ng, s // tc),
        in_specs=[col(0), col(1), col(2), col(3), vec, vec, vec, _const((tc, tc)), _const((tc, tc))] + x_specs,
        out_specs=[out, out, pl.BlockSpec((hp, tc // CHUNK, hd, hd), lambda h, t: (h, t, 0, 0))] + x_specs,
        out_shape=[jax.ShapeDtypeStruct((s, HGRN_WIDTH), F32), jax.ShapeDtypeStruct((s, HGRN_WIDTH), BF16),
                   jax.ShapeDtypeStruct((HGRN_HEADS, s // CHUNK, hd, hd), F32)] + x_shapes,
        scratch_shapes=[pltpu.VMEM((hp, hd, hd), F32)] + [pltpu.VMEM((tc, hp * hd), F32)] * 4 + x_scratch,
        compiler_params=_params(("arbitrary", "arbitrary")),
    )(z, z, z, z, l0, l1, gn, _block_tri(tc, SUB, False), _block_tri(tc, CHUNK, False), *plan_args)


def _pooled(p, ext, tok0):
    tm = p.shape[0]
    tok = tok0 + lax.broadcasted_iota(jnp.int32, (tm, 1), 0)
    outs = []
    for g, w in enumerate(POOL_WINDOWS):
        acc = ext[:, g * HEAD_DIM:(g + 1) * HEAD_DIM]
        sh = 1
        while sh < w:
            acc = acc + pltpu.roll(acc, sh, axis=0)
            sh *= 2
        cnt = jnp.minimum(tok + 1, w).astype(F32)
        outs.append(acc[POOL_HALO:, :] / cnt - p[:, g * HEAD_DIM:(g + 1) * HEAD_DIM])
    return outs


def _pool_fwd(z, wp, scale, tm, plan, plan_args):
    s = z.shape[0]
    pw = len(POOL_WINDOWS) * HEAD_DIM
    nb = tm // POOL_HALO

    def body(p_ref, prev_ref, wp_ref, sc_ref, ob_ref):
        t = pl.program_id(0)
        p = p_ref[...]
        prev = jnp.where(t > 0, prev_ref[...], 0.0)
        pooled = _pooled(p, jnp.concatenate([prev, p], axis=0), t * tm)
        ys = [_dot(pooled[g].astype(BF16), wp_ref[g].astype(BF16)) for g in range(len(POOL_WINDOWS))]
        ob_ref[...] = (jnp.concatenate(ys, axis=1) * sc_ref[...]).astype(BF16)

    x_specs, x_shapes, x_scratch = _plan_extras(plan)
    return pl.pallas_call(
        _fuse_exchange(body, 4, 1, 0, plan, 1), name="pool_fwd", grid=(s // tm,),
        in_specs=[pl.BlockSpec((tm, pw), lambda t: (t, 4)),
                  pl.BlockSpec((POOL_HALO, pw), lambda t: (jnp.maximum(t * nb - 1, 0), 4)),
                  _const(wp.shape), _const((1, pw))] + x_specs,
        out_specs=[pl.BlockSpec((tm, pw), lambda t: (t, 0))] + x_specs,
        out_shape=[jax.ShapeDtypeStruct((s, pw), BF16)] + x_shapes,
        scratch_shapes=x_scratch,
        compiler_params=_params(("arbitrary",)),
    )(z, z, wp, scale, *plan_args)


def _kv_proj(mem, g, slab_g):
    m, d = mem.shape
    rows = d // N_CHIPS

    def body(mem_ref, g_ref, wk_ref, wv_ref, xk_ref, xv_ref):
        hm, _, _ = _rms_fwd(mem_ref[...], g_ref[...])
        hb = hm.astype(BF16)
        xk_ref[...] = _dot(hb, wk_ref[...].reshape(d, d)).astype(BF16)
        xv_ref[...] = _dot(hb, wv_ref[...].reshape(d, d)).astype(BF16)

    blk = lambda k: pl.BlockSpec((N_CHIPS, rows, d), lambda i: (0, k, 0))
    return pl.pallas_call(
        body, name="kv_proj", grid=(1,),
        in_specs=[_const((m, d)), _const((1, d)), blk(0), blk(1)],
        out_specs=[_const_out((m, d)), _const_out((m, d))],
        out_shape=[jax.ShapeDtypeStruct((m, d), BF16)] * 2,
        compiler_params=_params(("arbitrary",)),
    )(mem, g, slab_g, slab_g)


def _softmax_rows(sc):
    e = jnp.exp(sc - jnp.max(sc, axis=-1, keepdims=True))
    return e / jnp.sum(e, axis=-1, keepdims=True)


def _mix_xattn_fwd(x, oa, ob, gx, slab_g, wo_g, xk, xv, tm, plan=None, plan_args=()):
    s, d = x.shape
    m = xk.shape[0]
    rows = d // N_CHIPS
    hw = oa.shape[1]
    e = XATTN_HEAD_DIM

    def body(x_ref, oa_ref, ob_ref, gx_ref, wout_ref, wq_ref, wo_ref, xk_ref, xv_ref,
             x1_ref, mixed_ref, hq_ref, xq_ref, att_ref, x2_ref):
        mixed = jnp.concatenate([oa_ref[...], ob_ref[...]], axis=1)
        mixed_ref[...] = mixed
        x1 = x_ref[...] + _dot(mixed, wout_ref[...].reshape(d, d))
        x1_ref[...] = x1
        hq, _, _ = _rms_fwd(x1, gx_ref[...])
        hqb = hq.astype(BF16)
        hq_ref[...] = hqb
        xq = _dot(hqb, wq_ref[...].reshape(d, d)).astype(BF16)
        xq_ref[...] = xq
        atts = []
        for h in range(XATTN_HEADS):
            cs = slice(h * e, (h + 1) * e)
            p = _softmax_rows(_dot(xq[:, cs], xk_ref[:, cs], NT) * (e ** -0.5))
            atts.append(_dot(p.astype(BF16), xv_ref[:, cs]).astype(BF16))
        att = jnp.concatenate(atts, axis=1)
        att_ref[...] = att
        for j in range(N_CHIPS):
            x2_ref[:, j * rows:(j + 1) * rows] = x1[:, j * rows:(j + 1) * rows] + _dot(att, wo_ref[j])

    tile = lambda w: pl.BlockSpec((tm, w), lambda t: (t, 0))
    blk = lambda k: pl.BlockSpec((N_CHIPS, rows, d), lambda t: (0, k, 0), pipeline_mode=pl.Buffered(1))
    x_specs, x_shapes, x_scratch = _plan_extras(plan)
    return pl.pallas_call(
        _fuse_exchange(body, 9, 6, 0, plan, 1), name="mix_xattn_fwd", grid=(s // tm,),
        in_specs=[tile(d), tile(hw), tile(hw), _const((1, d)), blk(0), blk(1), _const(wo_g.shape),
                  _const((m, d)), _const((m, d))] + x_specs,
        out_specs=[tile(d)] * 6 + x_specs,
        out_shape=[jax.ShapeDtypeStruct((s, d), F32)] + [jax.ShapeDtypeStruct((s, d), BF16)] * 4
                  + [jax.ShapeDtypeStruct((s, d), F32)] + x_shapes,
        scratch_shapes=x_scratch,
        compiler_params=_params(("arbitrary",)),
    )(x, oa, ob, gx, slab_g, slab_g, wo_g, xk, xv, *plan_args)


def _mlp_loss_fwd(x2, gffn, gfin, w1_g, w2_g, target, tm):
    s, d = x2.shape
    wr = w1_g.shape[1]

    def body(x2_ref, gffn_ref, gfin_ref, w1_ref, w2_ref, tg_ref, a_ref, hf_ref, dx3_ref, dx3b_ref, st_ref):
        x2v = x2_ref[...]
        hf, _, _ = _rms_fwd(x2v, gffn_ref[...])
        hfb = hf.astype(BF16)
        hf_ref[...] = hfb
        acc = x2v
        for j in range(N_CHIPS):
            a = _dot(hfb, w1_ref[j])
            a_ref[:, j * wr:(j + 1) * wr] = a
            r = jnp.maximum(a, 0.0)
            acc = acc + _dot((r * r).astype(BF16), w2_ref[j])
        gf = gfin_ref[...]
        y, n, r3 = _rms_fwd(acc, gf)
        err = y - tg_ref[...]
        loss = 0.5 * jnp.sum(jnp.sum(err * err, axis=-1, keepdims=True) * (1.0 / d), axis=0, keepdims=True)
        dy = err * (1.0 / d)
        dx3, dgf = _rms_bwd(dy, n, r3, gf)
        dx3_ref[...] = dx3
        dx3b_ref[...] = dx3.astype(BF16)
        _acc_rows(st_ref, pl.program_id(0), [dgf, jnp.broadcast_to(loss, (1, d))])

    tile = lambda w: pl.BlockSpec((tm, w), lambda t: (t, 0))
    blk = lambda k: pl.BlockSpec((N_CHIPS, wr, d), lambda t: (0, k, 0), pipeline_mode=pl.Buffered(1))
    return pl.pallas_call(
        body, name="mlp_loss_fwd", grid=(s // tm,),
        in_specs=[tile(d), _const((1, d)), _const((1, d)), blk(0), blk(0), tile(d)],
        out_specs=[tile(N_CHIPS * wr), tile(d), tile(d), tile(d), _const_out((8, d))],
        out_shape=[jax.ShapeDtypeStruct((s, N_CHIPS * wr), F32), jax.ShapeDtypeStruct((s, d), BF16),
                   jax.ShapeDtypeStruct((s, d), F32), jax.ShapeDtypeStruct((s, d), BF16),
                   jax.ShapeDtypeStruct((8, d), F32)],
        compiler_params=_params(("arbitrary",)),
    )(x2, gffn, gfin, w1_g, w2_g, target)


def _mlp_bwd(dx3, dx3b, a, x2, gffn, w1_g, w2_g, tm):
    s, d = x2.shape
    wr = w1_g.shape[1]

    def body(dx3_ref, dx3b_ref, a_ref, x2_ref, g_ref, w1_ref, w2_ref, da_ref, u_ref, dx2_ref, dx2b_ref, st_ref):
        dyb = dx3b_ref[...]
        dhf = jnp.zeros((tm, d), F32)
        for j in range(N_CHIPS):
            r = jnp.maximum(a_ref[:, j * wr:(j + 1) * wr], 0.0)
            da = (_dot(dyb, w2_ref[j], NT) * (2.0 * r)).astype(BF16)
            da_ref[:, j * wr:(j + 1) * wr] = da
            u_ref[:, j * wr:(j + 1) * wr] = (r * r).astype(BF16)
            dhf = dhf + _dot(da, w1_ref[j], NT)
        g = g_ref[...]
        _, n, r2 = _rms_fwd(x2_ref[...], g)
        dxn, dg = _rms_bwd(dhf, n, r2, g)
        dx2 = dx3_ref[...] + dxn
        dx2_ref[...] = dx2
        dx2b_ref[...] = dx2.astype(BF16)
        _acc_rows(st_ref, pl.program_id(0), [dg])

    tile = lambda w: pl.BlockSpec((tm, w), lambda t: (t, 0))
    blk = lambda k: pl.BlockSpec((N_CHIPS, wr, d), lambda t: (0, k, 0), pipeline_mode=pl.Buffered(1))
    nf = N_CHIPS * wr
    return pl.pallas_call(
        body, name="mlp_bwd", grid=(s // tm,),
        in_specs=[tile(d), tile(d), tile(nf), tile(d), _const((1, d)), blk(0), blk(0)],
        out_specs=[tile(nf), tile(nf), tile(d), tile(d), _const_out((8, d))],
        out_shape=[jax.ShapeDtypeStruct((s, nf), BF16), jax.ShapeDtypeStruct((s, nf), BF16),
                   jax.ShapeDtypeStruct((s, d), F32), jax.ShapeDtypeStruct((s, d), BF16),
                   jax.ShapeDtypeStruct((8, d), F32)],
        compiler_params=_params(("arbitrary",)),
    )(dx3, dx3b, a, x2, gffn, w1_g, w2_g)


def _xattn_mix_bwd(dx2, x1, xq, xk, xv, gx, slab_g, wo_g, tm, plan=None, plan_args=()):
    s, d = x1.shape
    m = xk.shape[0]
    rows = d // N_CHIPS
    e = XATTN_HEAD_DIM

    def body(dx2_ref, x1_ref, xq_ref, xk_ref, xv_ref, gx_ref, wout_ref, wq_ref, wo_ref,
             dx1_ref, dx1b_ref, dxq_ref, dmix_ref, dxk_ref, dxv_ref, st_ref):
        t = pl.program_id(0)
        dx2 = dx2_ref[...]
        dx2b = dx2.astype(BF16)
        datt = jnp.zeros((tm, d), F32)
        for j in range(N_CHIPS):
            datt = datt + _dot(dx2b[:, j * rows:(j + 1) * rows], wo_ref[j], NT)
        dattb = datt.astype(BF16)
        dxqs, dxks, dxvs = [], [], []
        for h in range(XATTN_HEADS):
            cs = slice(h * e, (h + 1) * e)
            xq_h, xk_h, xv_h = xq_ref[:, cs], xk_ref[:, cs], xv_ref[:, cs]
            p = _softmax_rows(_dot(xq_h, xk_h, NT) * (e ** -0.5))
            dp = _dot(dattb[:, cs], xv_h, NT)
            ds = (p * (dp - jnp.sum(dp * p, axis=-1, keepdims=True)) * (e ** -0.5)).astype(BF16)
            dxqs.append(_dot(ds, xk_h).astype(BF16))
            dxks.append(_dot(ds, xq_h, TN))
            dxvs.append(_dot(p.astype(BF16), dattb[:, cs], TN))
        dxq = jnp.concatenate(dxqs, axis=1)
        dxq_ref[...] = dxq
        dxk = jnp.concatenate(dxks, axis=1)
        dxv = jnp.concatenate(dxvs, axis=1)

        @pl.when(t == 0)
        def _():
            dxk_ref[...] = dxk
            dxv_ref[...] = dxv

        @pl.when(t > 0)
        def _():
            dxk_ref[...] = dxk_ref[...] + dxk
            dxv_ref[...] = dxv_ref[...] + dxv

        dhq = jnp.concatenate([_dot(dxq, wq_ref[j], NT) for j in range(N_CHIPS)], axis=1)
        g = gx_ref[...]
        _, n, r1 = _rms_fwd(x1_ref[...], g)
        dxn, dg = _rms_bwd(dhq, n, r1, g)
        dx1 = dx2 + dxn
        dx1_ref[...] = dx1
        dx1b = dx1.astype(BF16)
        dx1b_ref[...] = dx1b
        for j in range(N_CHIPS):
            dmix_ref[:, j * rows:(j + 1) * rows] = _dot(dx1b, wout_ref[j], NT)
        _acc_rows(st_ref, t, [dg])

    tile = lambda: pl.BlockSpec((tm, d), lambda t: (t, 0))
    blk = lambda k: pl.BlockSpec((N_CHIPS, rows, d), lambda t: (0, k, 0), pipeline_mode=pl.Buffered(1))
    x_specs, x_shapes, x_scratch = _plan_extras(plan)
    return pl.pallas_call(
        _fuse_exchange(body, 9, 7, 0, plan, 1), name="xattn_mix_bwd", grid=(s // tm,),
        in_specs=[tile(), tile(), tile(), _const((m, d)), _const((m, d)), _const((1, d)), blk(0), blk(1),
                  _const(wo_g.shape)] + x_specs,
        out_specs=[tile(), tile(), tile(), tile(), _const_out((m, d)), _const_out((m, d)), _const_out((8, d))]
                  + x_specs,
        out_shape=[jax.ShapeDtypeStruct((s, d), F32), jax.ShapeDtypeStruct((s, d), BF16),
                   jax.ShapeDtypeStruct((s, d), BF16), jax.ShapeDtypeStruct((s, d), F32),
                   jax.ShapeDtypeStruct((m, d), F32), jax.ShapeDtypeStruct((m, d), F32),
                   jax.ShapeDtypeStruct((8, d), F32)] + x_shapes,
        scratch_shapes=x_scratch,
        compiler_params=_params(("arbitrary",)),
    )(dx2, x1, xq, xk, xv, gx, slab_g, slab_g, wo_g, *plan_args)


def _kv_bwd(mem, g, dxk, dxv, slab_g):
    m, d = mem.shape
    rows = d // N_CHIPS

    def body(mem_ref, g_ref, dxk_ref, dxv_ref, wk_ref, wv_ref, dwk_ref, dwv_ref, st_ref):
        gv = g_ref[...]
        hm, n, _ = _rms_fwd(mem_ref[...], gv)
        hb = hm.astype(BF16)
        dkb = dxk_ref[...].astype(BF16)
        dvb = dxv_ref[...].astype(BF16)
        dhm = []
        for j in range(N_CHIPS):
            hj = hb[:, j * rows:(j + 1) * rows]
            dwk_ref[j] = _dot(hj, dkb, TN)
            dwv_ref[j] = _dot(hj, dvb, TN)
            dhm.append(_dot(dkb, wk_ref[j], NT) + _dot(dvb, wv_ref[j], NT))
        dg = jnp.sum(jnp.concatenate(dhm, axis=1) * n, axis=0, keepdims=True)
        st_ref[...] = jnp.concatenate([dg, jnp.zeros((7, d), F32)], axis=0)

    blk = lambda k: pl.BlockSpec((N_CHIPS, rows, d), lambda i: (0, k, 0))
    return pl.pallas_call(
        body, name="kv_bwd", grid=(1,),
        in_specs=[_const((m, d)), _const((1, d)), _const((m, d)), _const((m, d)), blk(0), blk(1)],
        out_specs=[_const_out((N_CHIPS, rows, d)), _const_out((N_CHIPS, rows, d)), _const_out((8, d))],
        out_shape=[jax.ShapeDtypeStruct((N_CHIPS, rows, d), F32)] * 2 + [jax.ShapeDtypeStruct((8, d), F32)],
        compiler_params=_params(("arbitrary",)),
    )(mem, g, dxk, dxv, slab_g, slab_g)


def _pool_bwd(z, dmix, wp, scale, tm, plan=None, plan_args=()):
    s = z.shape[0]
    ng = len(POOL_WINDOWS)
    pw = ng * HEAD_DIM
    nb = tm // POOL_HALO
    nt = s // tm
    n_ext = tm + POOL_HALO

    def body(p_ref, prev_ref, dm_ref, dmn_ref, wp_ref, sc_ref, dp_ref, dwp_ref, st_ref):
        t = pl.program_id(0)
        p = p_ref[...]
        prev = jnp.where(t > 0, prev_ref[...], 0.0)
        pooled = _pooled(p, jnp.concatenate([prev, p], axis=0), t * tm)
        dm = dm_ref[...]
        dme = jnp.concatenate([dm, jnp.where(t < nt - 1, dmn_ref[...], 0.0)], axis=0) * sc_ref[...]
        tok = t * tm + lax.broadcasted_iota(jnp.int32, (n_ext, 1), 0)
        dsc, dps, dwps = [], [], []
        for g, w in enumerate(POOL_WINDOWS):
            cs = slice(g * HEAD_DIM, (g + 1) * HEAD_DIM)
            wpb = wp_ref[g].astype(BF16)
            pb = pooled[g].astype(BF16)
            dsc.append(jnp.sum(dm[:, cs] * _dot(pb, wpb), axis=0, keepdims=True))
            dye = dme[:, cs].astype(BF16)
            dwps.append(_dot(pb, dye[:tm], TN))
            dpe = _dot(dye, wpb, NT)
            acc = dpe / jnp.minimum(tok + 1, w).astype(F32)
            sh = 1
            while sh < w:
                acc = acc + pltpu.roll(acc, n_ext - sh, axis=0)
                sh *= 2
            dps.append(acc[:tm] - dpe[:tm])
        dp_ref[...] = jnp.concatenate(dps, axis=1)
        dsc_row = jnp.concatenate(dsc, axis=1)

        @pl.when(t == 0)
        def _():
            for g in range(ng):
                dwp_ref[g] = dwps[g]

        @pl.when(t > 0)
        def _():
            for g in range(ng):
                dwp_ref[g] = dwp_ref[g] + dwps[g]

        _acc_rows(st_ref, t, [dsc_row])

    x_specs, x_shapes, x_scratch = _plan_extras(plan)
    return pl.pallas_call(
        _fuse_exchange(body, 6, 3, 0, plan, 1), name="pool_bwd", grid=(nt,),
        in_specs=[pl.BlockSpec((tm, pw), lambda t: (t, 4)),
                  pl.BlockSpec((POOL_HALO, pw), lambda t: (jnp.maximum(t * nb - 1, 0), 4)),
                  pl.BlockSpec((tm, pw), lambda t: (t, 1)),
                  pl.BlockSpec((POOL_HALO, pw), lambda t: (jnp.minimum((t + 1) * nb, s // POOL_HALO - 1), 1)),
                  _const(wp.shape), _const((1, pw))] + x_specs,
        out_specs=[pl.BlockSpec((tm, pw), lambda t: (t, 0)), _const_out(wp.shape), _const_out((8, pw))] + x_specs,
        out_shape=[jax.ShapeDtypeStruct((s, pw), F32), jax.ShapeDtypeStruct(wp.shape, F32),
                   jax.ShapeDtypeStruct((8, pw), F32)] + x_shapes,
        scratch_shapes=x_scratch,
        compiler_params=_params(("arbitrary",)),
    )(z, z, dmix, dmix, wp, scale, *plan_args)


def _hgrn_bwd(z, o, dmix, st, l0, l1, gn, tc, unroll=1, plan=None, plan_args=()):
    s = z.shape[0]
    nsub = tc // SUB
    nt = s // tc
    hd = HEAD_DIM
    hp = HEADS_PER_STEP

    def body(q_ref, f_ref, v_ref, g_ref, l0_ref, l1_ref, gn_ref, o_ref, dm_ref, st_ref,
             tril_ref, triu_ref, trilc_ref, triuc_ref,
             dq_ref, df_ref, di_ref, dg_ref, stat_ref, dstate, qs, ks, bs, dos, dqs, dks, dbs, sts):
        t = pl.program_id(1)

        @pl.when(t == 0)
        def _():
            dstate[...] = jnp.zeros_like(dstate)

        cols = [slice(hh * hd, (hh + 1) * hd) for hh in range(hp)]
        heads = []
        for hh, cs in enumerate(cols):
            lb = _lower_bound(l0_ref[hh], l1_ref[hh])
            qp = q_ref[:, cs]
            q, sq, f, sf = _hgrn_gates(qp, f_ref[:, cs], lb)
            lf = jnp.log(f) * LOG2E
            o = o_ref[:, cs]
            r = lax.rsqrt(jnp.mean(o * o, axis=-1, keepdims=True) + EPS)
            n = o * r
            gnv = gn_ref[hh]
            gp = g_ref[:, cs]
            sg = _sigmoid(gp)
            dm = dm_ref[:, cs]
            dg_ref[:, cs] = dm * (n * gnv) * (sg * (1.0 + gp * (1.0 - sg)))
            don = dm * (gp * sg)
            dn = don * gnv
            heads.append(dict(lb=lb, qp=qp, q=q, sq=sq, f=f, sf=sf, k=1.0 - f, lf=lf,
                              bc=_group_cumsum(trilc_ref[...], lf), dgn=jnp.sum(don * n, axis=0, keepdims=True),
                              do=r * (dn - n * jnp.mean(dn * n, axis=-1, keepdims=True))))
        bounded = functools.reduce(jnp.minimum, [jnp.min(h["bc"]) for h in heads]) >= -MAX_LOG2_GROWTH

        def factored(hh, cs, q, k, bc, do_all):
            eb = jnp.exp2(bc)
            eib = jnp.exp2(-bc)
            qt = (q * eb).astype(BF16)
            ki = (k * eib).astype(BF16)
            vb = v_ref[:, cs].astype(BF16)
            dob = do_all.astype(BF16)
            mask = trilc_ref[...] > 0
            a = jnp.where(mask, _dot(qt, ki, NT), 0.0).astype(BF16)
            da = jnp.where(mask, _dot(dob, vb, NT), 0.0).astype(BF16)
            dq_in = _dot(da, ki)
            dk_in = _dot(da, qt, TN)
            dv_in = _dot(a, dob, TN)
            last_row = lax.broadcasted_iota(jnp.int32, (CHUNK, 1), 0) == CHUNK - 1
            for c in reversed(range(tc // CHUNK)):
                rs = slice(c * CHUNK, (c + 1) * CHUNK)
                stp = st_ref[hh, c]
                dst = dstate[hh]
                dstb = dst.astype(BF16)
                bl = bc[(c + 1) * CHUNK - 1:(c + 1) * CHUNK, :]
                ekl = jnp.exp2(bl - bc[rs])
                ebl = jnp.exp2(bl)
                kt = k[rs] * ekl
                dq_st = _dot(dob[rs], stp.astype(BF16)) * eb[rs]
                dkt = _dot(vb[rs], dstb)
                extra = jnp.sum(kt * dkt, axis=0, keepdims=True) + ebl * jnp.sum(stp * dst, axis=0, keepdims=True)
                dqs[rs, cs] = dq_st + dq_in[rs] * eb[rs]
                dks[rs, cs] = dkt * ekl + dk_in[rs] * eib[rs]
                di_ref[rs, cs] = _dot(kt.astype(BF16), dstb, NT) + dv_in[rs]
                dbs[rs, cs] = (q[rs] * dq_st - kt * dkt + jnp.where(last_row, extra, 0.0)
                               + (qt[rs].astype(F32) * dq_in[rs] - ki[rs].astype(F32) * dk_in[rs]))
                dstate[hh] = dst * ebl + _dot(dob[rs], qt[rs], TN)
            dbs[:, cs] = _group_cumsum(triuc_ref[...], dbs[:, cs])

        def exact(hh, cs, q, k, lf, do_all):
            qs[:, cs] = q
            ks[:, cs] = k
            bs[:, cs] = _group_cumsum(tril_ref[...], lf)
            dos[:, cs] = do_all
            per = CHUNK // SUB

            def restore(i, carry):
                @pl.when(i % per == 0)
                def _():
                    sts[i] = st_ref[hh, i // per]

                @pl.when(i % per != 0)
                def _():
                    rp = pl.multiple_of((i - 1) * SUB, SUB)
                    b_ = bs[pl.ds(rp, SUB), cs]
                    bl = b_[SUB - 1:SUB, :]
                    kt = (ks[pl.ds(rp, SUB), cs] * jnp.exp2(bl - b_)).astype(BF16)
                    sts[i] = sts[i - 1] * jnp.exp2(bl) + _dot(v_ref[pl.ds(rp, SUB), cs].astype(BF16), kt, TN)

                return carry

            lax.fori_loop(0, nsub, restore, 0)
            rows = lax.broadcasted_iota(jnp.int32, (HALF, 1), 0)
            last_row = lax.broadcasted_iota(jnp.int32, (SUB, 1), 0) == SUB - 1

            def step(i, carry):
                ii = nsub - 1 - i
                r0 = pl.multiple_of(ii * SUB, SUB)
                q_ = qs[pl.ds(r0, SUB), cs]
                k_ = ks[pl.ds(r0, SUB), cs]
                b_ = bs[pl.ds(r0, SUB), cs]
                v_ = v_ref[pl.ds(r0, SUB), cs]
                do_ = dos[pl.ds(r0, SUB), cs]
                stp = sts[ii]
                dst = dstate[hh]
                bl = b_[SUB - 1:SUB, :]
                eb = jnp.exp2(b_)
                ekl = jnp.exp2(bl - b_)
                ebl = jnp.exp2(bl)
                dob = do_.astype(BF16)
                dstb = dst.astype(BF16)
                kt = k_ * ekl
                dq = _dot(dob, stp.astype(BF16)) * eb
                dkt = _dot(v_.astype(BF16), dstb)
                dk = dkt * ekl
                dv = _dot(kt.astype(BF16), dstb, NT)
                extra = jnp.sum(kt * dkt, axis=0, keepdims=True) + ebl * jnp.sum(stp * dst, axis=0, keepdims=True)
                halves = lambda x: [x[:HALF], x[HALF:]]
                q_h, b_h, do_h, dq_h, dk_h, dv_h = (halves(x) for x in (q_, b_, do_, dq, dk, dv))
                for own in range(2):
                    dk_rows, dv_rows = _RowSums(rows), _RowSums(rows)
                    for jj in _RowSums.ORDER:
                        j = own * HALF + jj
                        bj, kj, vj = b_[j:j + 1, :], k_[j:j + 1, :], v_[j:j + 1, :]
                        dk_sum = dv_sum = None
                        for h in range(own, 2):
                            e = _decay(b_h[h], bj, rows, jj if h == own else None)
                            pe = q_h[h] * e
                            acol = jnp.sum(pe * kj, axis=-1, keepdims=True)
                            dacol = jnp.sum(do_h[h] * vj, axis=-1, keepdims=True)
                            dq_h[h] = dq_h[h] + dacol * (e * kj)
                            dk_sum = dacol * pe if dk_sum is None else dk_sum + dacol * pe
                            dv_sum = acol * do_h[h] if dv_sum is None else dv_sum + acol * do_h[h]
                        dk_rows.push(jj, dk_sum)
                        dv_rows.push(jj, dv_sum)
                    dk_h[own] = dk_h[own] + dk_rows.result()
                    dv_h[own] = dv_h[own] + dv_rows.result()
                dq, dk, dv = (jnp.concatenate(x, axis=0) for x in (dq_h, dk_h, dv_h))
                dqs[pl.ds(r0, SUB), cs] = dq
                dks[pl.ds(r0, SUB), cs] = dk
                di_ref[pl.ds(r0, SUB), cs] = dv
                dbs[pl.ds(r0, SUB), cs] = q_ * dq - k_ * dk + jnp.where(last_row, extra, 0.0)
                dstate[hh] = dst * ebl + _dot(dob, (q_ * eb).astype(BF16), TN)
                return carry

            lax.fori_loop(0, nsub, step, 0, unroll=unroll)
            dbs[:, cs] = _group_cumsum(triu_ref[...], dbs[:, cs])

        @pl.when(bounded)
        def _():
            for hh, cs in enumerate(cols):
                factored(hh, cs, heads[hh]["q"], heads[hh]["k"], heads[hh]["bc"], heads[hh]["do"])

        @pl.when(jnp.logical_not(bounded))
        def _():
            for hh, cs in enumerate(cols):
                exact(hh, cs, heads[hh]["q"], heads[hh]["k"], heads[hh]["lf"], heads[hh]["do"])

        for hh, cs in enumerate(cols):
            h = heads[hh]
            dfv = dbs[:, cs] / h["f"] - dks[:, cs]
            df_ref[:, cs] = dfv * (1.0 - h["lb"]) * h["sf"] * (1.0 - h["sf"])
            dlb = jnp.sum(dfv * (1.0 - h["sf"]), axis=0, keepdims=True)
            dq_ref[:, cs] = dqs[:, cs] * (h["sq"] * (1.0 + h["qp"] * (1.0 - h["sq"])))
            _acc_rows(stat_ref.at[hh], t, [h["dgn"], dlb])

    rev = lambda t: nt - 1 - t
    ng = HGRN_HEADS // hp
    col = lambda k: pl.BlockSpec((tc, hp * hd), lambda h, t: (rev(t), k * ng + h))
    vec = pl.BlockSpec((hp, 1, hd), lambda h, t: (h, 0, 0))
    head = pl.BlockSpec((tc, hp * hd), lambda h, t: (rev(t), h))
    x_specs, x_shapes, x_scratch = _plan_extras(plan)
    return pl.pallas_call(
        _fuse_exchange(body, 14, 5, 9, plan, 2), name="hgrn_bwd", grid=(ng, nt),
        in_specs=[col(0), col(1), col(2), col(3), vec, vec, vec, head, head,
                  pl.BlockSpec((hp, tc // CHUNK, hd, hd), lambda h, t: (h, rev(t), 0, 0))]
                 + [_const((tc, tc))] * 4 + x_specs,
        out_specs=[head, head, head, head, pl.BlockSpec((hp, 8, hd), lambda h, t: (h, 0, 0))] + x_specs,
        out_shape=[jax.ShapeDtypeStruct((s, HGRN_WIDTH), F32)] * 4 + [jax.ShapeDtypeStruct((HGRN_HEADS, 8, hd), F32)]
                  + x_shapes,
        scratch_shapes=[pltpu.VMEM((hp, hd, hd), F32)] + [pltpu.VMEM((tc, hp * hd), F32)] * 7
                       + [pltpu.VMEM((nsub, hd, hd), F32)] + x_scratch,
        compiler_params=_params(("arbitrary", "arbitrary")),
    )(z, z, z, z, l0, l1, gn, o, dmix, st, _block_tri(tc, SUB, False), _block_tri(tc, SUB, True),
      _block_tri(tc, CHUNK, False), _block_tri(tc, CHUNK, True), *plan_args)


def _in_bwd(dparts, dx1, x, g, win_g, tm, plan=None, plan_args=()):
    s, d = x.shape
    nsh, _, wc = win_g.shape
    pw = dparts[0].shape[1]

    def body(dq_ref, df_ref, di_ref, dg_ref, dp_ref, dx1_ref, x_ref, g_ref, w_ref, gx_ref, dz_ref, st_ref):
        dz = jnp.concatenate([dq_ref[...], df_ref[...], di_ref[...], dg_ref[...], dp_ref[...]], axis=1).astype(BF16)
        dz_ref[...] = dz
        dh = jnp.zeros((tm, d), F32)
        for j in range(nsh):
            dh = dh + _dot(dz[:, j * wc:(j + 1) * wc], w_ref[j], NT)
        gv = g_ref[...]
        _, n, r = _rms_fwd(x_ref[...], gv)
        dxn, dg = _rms_bwd(dh, n, r, gv)
        gx_ref[...] = dx1_ref[...] + dxn
        _acc_rows(st_ref, pl.program_id(0), [dg])

    tile = lambda w: pl.BlockSpec((tm, w), lambda t: (t, 0))
    x_specs, x_shapes, x_scratch = _plan_extras(plan)
    return pl.pallas_call(
        _fuse_exchange(body, 9, 3, 0, plan, 1), name="in_bwd", grid=(s // tm,),
        in_specs=[tile(pw)] * 5 + [tile(d), tile(d), _const((1, d)), _const(win_g.shape)] + x_specs,
        out_specs=[tile(d), tile(nsh * wc), _const_out((8, d))] + x_specs,
        out_shape=[jax.ShapeDtypeStruct((s, d), F32), jax.ShapeDtypeStruct((s, nsh * wc), BF16),
                   jax.ShapeDtypeStruct((8, d), F32)] + x_shapes,
        scratch_shapes=x_scratch,
        compiler_params=_params(("arbitrary",)),
    )(*dparts, dx1, x, g, win_g, *plan_args)


def _tn_grad(name, a, b, out_rows, out_cols, a_sharded, plan=None, plan_args=()):
    s = a.shape[0]
    tr, tc = min(out_rows, GRAD_TILE), min(out_cols, GRAD_TILE)
    nr, nc = out_rows // tr, out_cols // tc

    def body(a_ref, b_ref, o_ref):
        o_ref[...] = _dot(a_ref[...], b_ref[...], TN)

    a_map = (lambda j, i, k: (0, j * nr + i)) if a_sharded else (lambda j, i, k: (0, i))
    b_map = (lambda j, i, k: (0, k)) if a_sharded else (lambda j, i, k: (0, j * nc + k))
    x_specs, x_shapes, x_scratch = _plan_extras(plan)
    res = pl.pallas_call(
        _fuse_exchange(body, 2, 1, 0, plan, 3), name=name, grid=(N_CHIPS, nr, nc),
        in_specs=[pl.BlockSpec((s, tr), a_map), pl.BlockSpec((s, tc), b_map)] + x_specs,
        out_specs=[pl.BlockSpec((None, tr, tc), lambda j, i, k: (j, i, k))] + x_specs,
        out_shape=[jax.ShapeDtypeStruct((N_CHIPS, out_rows, out_cols), F32)] + x_shapes,
        scratch_shapes=x_scratch,
        compiler_params=_params(("arbitrary", "arbitrary", "arbitrary")),
    )(a, b, *plan_args)
    return res if plan else res[0]


FFN_NAMES = ("w_ff1", "w_ff2")
ATTN_NAMES = ("w_xo", "w_xq", "w_out", "w_xk", "w_xv")
EARLY_NAMES = FFN_NAMES + ATTN_NAMES
BIG_NAMES = EARLY_NAMES + ("w_in",)


def _halved(g):
    return g.reshape(N_CHIPS, 2, g.shape[1] // 2, g.shape[2])


def _pair_adds(names, gs, got, idx):
    pairs = [_grad_pair_add("grad_pair_add_" + k, g, r, idx, tr=min(256, g.shape[2])) for k, g, r in zip(names, gs, got)]
    return [p[0] for p in pairs], [p[1] for p in pairs]


def _step(x, mem, target, small, shards, idx):
    d = x.shape[1]
    l0 = small["lb_logits"][0].reshape(HGRN_HEADS, 1, HEAD_DIM)
    l1 = small["lb_logits"][1].reshape(HGRN_HEADS, 1, HEAD_DIM)
    gn = small["hgrn_norm_g"].reshape(HGRN_HEADS, 1, HEAD_DIM)
    wp = small["w_pool"].reshape(len(POOL_WINDOWS), HEAD_DIM, HEAD_DIM)
    psc = small["pool_scale"].reshape(1, -1)
    gmix, gx, gmem, gffn = (small[k].reshape(1, d) for k in ("norm_mix_g", "norm_x_g", "norm_mem_g", "norm_ffn_g"))
    gfin = small["final_norm_g"].reshape(1, d)

    (win_g,) = _run_exchange("gather_w_in", _WeightGather([shards["w_in"]]), [shards["w_in"]])
    z, h, kv_g = _in_proj(x, gmix, win_g, tm=TOKEN_TILE, plan=_WeightGather([shards["slab_kv"]]),
                          plan_args=[shards["slab_kv"]])
    mid_w = [shards["slab_oq"], shards["w_ff1"]]
    o, oa, st, oq_g, w1_g = _hgrn_fwd(z, l0, l1, gn, tc=HGRN_BLOCK, unroll=8,
                                      plan=_WeightGather(mid_w), plan_args=mid_w)
    ob, wo_g = _pool_fwd(z, wp, psc, tm=TOKEN_TILE, plan=_WeightGather([shards["w_xo"]]), plan_args=[shards["w_xo"]])
    xk, xv = _kv_proj(mem, gmem, kv_g)
    late_w = [shards["w_ff2"]]
    x1, mixed, hq, xq, att, x2, w2_g = _mix_xattn_fwd(x, oa, ob, gx, oq_g, wo_g, xk, xv, tm=TOKEN_TILE,
                                                      plan=_WeightGather(late_w), plan_args=late_w)
    a, hf, dx3, dx3b, st_loss = _mlp_loss_fwd(x2, gffn, gfin, w1_g, w2_g, target, tm=TOKEN_TILE)

    da, u, dx2, dx2b, st_ffn = _mlp_bwd(dx3, dx3b, a, x2, gffn, w1_g, w2_g, tm=MLP_BWD_TOKEN_TILE)
    g_ff1 = [_halved(_tn_grad("dw_ff1", hf, da, d, d, False))]
    dw_ff2, *got = _tn_grad("dw_ff2", u, dx3b, d, d, True, plan=_PairExchange(g_ff1), plan_args=g_ff1)
    keep_ff1, send_ff1 = _pair_adds(("w_ff1",), g_ff1, got, idx)
    g_ff2 = [_halved(dw_ff2)]
    dx1, dx1b, dxq, dmix, dxk, dxv, st_x, *got = _xattn_mix_bwd(
        dx2, x1, xq, xk, xv, gx, oq_g, wo_g, tm=TOKEN_TILE,
        plan=_Plans([_ChipExchange(send_ff1), _PairExchange(g_ff2)]), plan_args=send_ff1 + g_ff2)
    recv_ff1 = got[:1]
    keep_ff2, send_ff2 = _pair_adds(("w_ff2",), g_ff2, got[1:], idx)
    dw = {}
    dw["w_xo"] = _tn_grad("dw_xo", att, dx2b, d, d // N_CHIPS, False)
    dw["w_xq"] = _tn_grad("dw_xq", hq, dxq, d // N_CHIPS, d, True)
    dw["w_out"] = _tn_grad("dw_out", mixed, dx1b, d // N_CHIPS, d, True)
    dw["w_xk"], dw["w_xv"], st_mem = _kv_bwd(mem, gmem, dxk, dxv, kv_g)
    gs_attn = [_halved(dw[k]) for k in ATTN_NAMES]
    dp, d_wp, st_pool, *got_attn = _pool_bwd(z, dmix, wp, psc, tm=TOKEN_TILE, plan=_PairExchange(gs_attn), plan_args=gs_attn)
    keep_attn, send_attn = _pair_adds(ATTN_NAMES, gs_attn, got_attn, idx)
    sends = send_ff2 + send_attn
    dq, df, di, dg, st_hgrn, *received = _hgrn_bwd(z, o, dmix, st, l0, l1, gn, tc=HGRN_BLOCK, unroll=4,
                                                    plan=_ChipExchange(sends), plan_args=sends)
    grad_x, dz, st_mix = _in_bwd([dq, df, di, dg, dp], dx1, x, gmix, win_g, tm=TOKEN_TILE)
    keeps = keep_ff1 + keep_ff2 + keep_attn
    received = recv_ff1 + list(received)
    gs_in = [_halved(_tn_grad("dw_in", h, dz, d, win_g.shape[2], False))]
    got_in = _run_exchange("grad_pair_exchange_w_in", _PairExchange(gs_in), gs_in)
    keep_in, send_in = _pair_adds(("w_in",), gs_in, got_in, idx)

    partials = dict(zip(EARLY_NAMES, zip(keeps, received)))
    stats = dict(mix=st_mix, x=st_x, mem=st_mem, ffn=st_ffn, loss=st_loss, hgrn=st_hgrn, pool=st_pool)
    return grad_x, stats, d_wp, partials, keep_in, send_in


def _place():
    x, y, c = lax.axis_index("x"), lax.axis_index("y"), lax.axis_index("c")
    return x, y, c, [(x, 1 - y), (1 - x, y), (1 - x, 1 - y)]


def _rcopy(src, dst, ssem, rsem, dev):
    return pltpu.make_async_remote_copy(src_ref=src, dst_ref=dst, send_sem=ssem, recv_sem=rsem,
                                        device_id=dev, device_id_type=MESH)


class _WeightGather:
    def __init__(self, shards):
        self.n = len(shards)
        self.rows = [w.shape[0] for w in shards]
        self.out_shape = [jax.ShapeDtypeStruct((N_CHIPS,) + w.shape, w.dtype) for w in shards]
        self.scratch_shapes = [pltpu.SemaphoreType.DMA((self.n,))] * 2 + [pltpu.SemaphoreType.DMA((self.n, 3))] * 4

    def _copies(self, ins, outs, sems, with_pass_on):
        lsem, lrsem, ssem, rsem, fsem, frsem = sems
        x, y, c, peers = _place()
        chip = 2 * x + y
        sib = (x, y, 1 - c)
        own = [_rcopy(ins[a], outs[a].at[chip], lsem.at[a], lrsem.at[a], sib) for a in range(self.n)]
        sends, arrived, passed, passed_in = [], [], [], []
        for a in range(self.n):
            hr = self.rows[a] // 2
            half = lambda who, hc, a=a, hr=hr: outs[a].at[who, pl.ds(hc * hr, hr), :]
            for r, (px, py) in enumerate(peers):
                pc = 2 * px + py
                sends.append(_rcopy(ins[a].at[pl.ds(c * hr, hr), :], half(chip, c), ssem.at[a, r], rsem.at[a, r],
                                    (px, py, c)))
                if with_pass_on:
                    arrived.append(_rcopy(half(pc, c), half(pc, c), ssem.at[a, r], rsem.at[a, r], (px, py, c)))
                    passed.append(_rcopy(half(pc, c), half(pc, c), fsem.at[a, r], frsem.at[a, r], sib))
                    passed_in.append(_rcopy(half(pc, 1 - c), half(pc, 1 - c), fsem.at[a, r], frsem.at[a, r], sib))
        return own, sends, arrived, passed, passed_in

    def start(self, ins, outs, sems):
        own, sends, _, _, _ = self._copies(ins, outs, sems, False)
        for cp in own + sends:
            cp.start()

    def finish(self, ins, outs, sems):
        own, sends, arrived, passed, passed_in = self._copies(ins, outs, sems, True)
        for got, fwd in zip(arrived, passed):
            got.wait_recv()
            fwd.start()
        for cp in passed_in:
            cp.wait_recv()
        for cp in sends + passed:
            cp.wait_send()
        for cp in own:
            cp.wait()


class _ChipExchange:
    def __init__(self, sends):
        self.n = len(sends)
        self.out_shape = [jax.ShapeDtypeStruct(g.shape, g.dtype) for g in sends]
        self.scratch_shapes = [pltpu.SemaphoreType.DMA((self.n, 3))] * 2

    def _copies(self, ins, outs, sems):
        ssem, rsem = sems
        _, _, c, peers = _place()
        return [_rcopy(ins[a].at[r], outs[a].at[r], ssem.at[a, r], rsem.at[a, r], (px, py, c))
                for a in range(self.n) for r, (px, py) in enumerate(peers)]

    def start(self, ins, outs, sems):
        for cp in self._copies(ins, outs, sems):
            cp.start()

    def finish(self, ins, outs, sems):
        for cp in self._copies(ins, outs, sems):
            cp.wait()


class _Plans:
    def __init__(self, plans):
        self.plans = plans
        self.n = sum(p.n for p in plans)
        self.out_shape = [s for p in plans for s in p.out_shape]
        self.scratch_shapes = [s for p in plans for s in p.scratch_shapes]

    def _each(self, ins, outs, sems):
        a = b = 0
        for p in self.plans:
            ns = len(p.scratch_shapes)
            yield p, ins[a:a + p.n], outs[a:a + p.n], sems[b:b + ns]
            a, b = a + p.n, b + ns

    def start(self, ins, outs, sems):
        for p, i, o, s in self._each(ins, outs, sems):
            p.start(i, o, s)

    def finish(self, ins, outs, sems):
        for p, i, o, s in self._each(ins, outs, sems):
            p.finish(i, o, s)


def _run_exchange(name, plan, arrays):
    n = plan.n

    def body(*refs):
        ins, outs, sems = refs[:n], refs[n:2 * n], refs[2 * n:]
        plan.start(ins, outs, sems)
        plan.finish(ins, outs, sems)

    return pl.pallas_call(
        body, name=name, in_specs=[ANY] * n, out_specs=[ANY] * n,
        out_shape=plan.out_shape, scratch_shapes=plan.scratch_shapes,
    )(*arrays)


class _PairExchange:
    def __init__(self, gs):
        self.n = len(gs)
        self.out_shape = [jax.ShapeDtypeStruct((g.shape[0],) + g.shape[2:], g.dtype) for g in gs]
        self.scratch_shapes = [pltpu.SemaphoreType.DMA((self.n,))] * 2

    def _copies(self, ins, outs, sems):
        ssem, rsem = sems
        x, y, c, _ = _place()
        return [_rcopy(ins[a].at[:, 1 - c], outs[a], ssem.at[a], rsem.at[a], (x, y, 1 - c)) for a in range(self.n)]

    def start(self, ins, outs, sems):
        for cp in self._copies(ins, outs, sems):
            cp.start()

    def finish(self, ins, outs, sems):
        for cp in self._copies(ins, outs, sems):
            cp.wait()


def _grad_pair_add(name, g, got, idx, tr):
    _, _, hr, cc = g.shape

    def body(idx_ref, g0, g1, g2, g3, r0, r1, r2, r3, keep_ref, send_ref):
        keep_ref[...] = g0[...] + r0[...]
        for q, (gq, rq) in enumerate(((g1, r1), (g2, r2), (g3, r3))):
            send_ref[q] = (gq[...] + rq[...]).astype(BF16)

    gspec = lambda q: pl.BlockSpec((None, None, tr, cc), lambda i, idx: (idx[1 + q], idx[0], i, 0))
    rspec = lambda q: pl.BlockSpec((None, tr, cc), lambda i, idx: (idx[1 + q], i, 0))
    return pl.pallas_call(
        body, name=name,
        grid_spec=pltpu.PrefetchScalarGridSpec(
            num_scalar_prefetch=1, grid=(hr // tr,),
            in_specs=[gspec(q) for q in range(4)] + [rspec(q) for q in range(4)],
            out_specs=[pl.BlockSpec((tr, cc), lambda i, idx: (i, 0)), pl.BlockSpec((3, tr, cc), lambda i, idx: (0, i, 0))]),
        out_shape=[jax.ShapeDtypeStruct((hr, cc), F32), jax.ShapeDtypeStruct((3, hr, cc), BF16)],
        compiler_params=_params(("parallel",)),
    )(idx, g, g, g, g, got, got, got, got)


def _grad_chip_add(name, keep, got, tr):
    hr, cc = keep.shape

    def body(k_ref, g_ref, o_ref):
        o_ref[...] = ((k_ref[...] + g_ref[0].astype(F32)) + g_ref[1].astype(F32)) + g_ref[2].astype(F32)

    return pl.pallas_call(
        body, name=name, grid=(hr // tr,),
        in_specs=[pl.BlockSpec((tr, cc), lambda i: (i, 0)), pl.BlockSpec((3, tr, cc), lambda i: (0, i, 0))],
        out_specs=pl.BlockSpec((tr, cc), lambda i: (i, 0)),
        out_shape=jax.ShapeDtypeStruct((hr, cc), F32),
        compiler_params=_params(("parallel",)),
    )(keep, got)


class _HalfExchange:
    def __init__(self, ts):
        self.n = len(ts)
        self.out_shape = [jax.ShapeDtypeStruct(t.shape, t.dtype) for t in ts]
        self.scratch_shapes = [pltpu.SemaphoreType.DMA((self.n,))] * 2

    def _copies(self, ins, outs, sems):
        ssem, rsem = sems
        x, y, c, _ = _place()
        return [_rcopy(ins[a], outs[a], ssem.at[a], rsem.at[a], (x, y, 1 - c)) for a in range(self.n)]

    def start(self, ins, outs, sems):
        for cp in self._copies(ins, outs, sems):
            cp.start()

    def finish(self, ins, outs, sems):
        for cp in self._copies(ins, outs, sems):
            cp.wait()


def _small_allreduce(stats, d_wp, plan, plan_args):
    d = D_MODEL
    half = d // 2
    wps = d_wp.shape
    n = plan.n

    def body(mix_ref, x_ref, mem_ref, ffn_ref, loss_ref, hg_ref, pool_ref, wp_ref, *refs):
        cin, (slab_out, wp_out), cout = refs[:n], refs[n:n + 2], refs[n + 2:2 * n + 2]
        slab_buf, wp_buf, sib_s, sib_w, ssem, rsem = refs[2 * n + 2:2 * n + 8]
        csem = refs[2 * n + 8:]
        plan.start(cin, cout, csem)
        x, y, c, peers = _place()
        chip = 2 * x + y
        sib = (x, y, 1 - c)
        hgn = jnp.concatenate([hg_ref[h, 0:1, :] for h in range(HGRN_HEADS)], axis=1)
        dlb = jnp.concatenate([hg_ref[h, 1:2, :] for h in range(HGRN_HEADS)], axis=1)
        slab_buf[0] = jnp.concatenate([
            mix_ref[0:1, :], x_ref[0:1, :], mem_ref[0:1, :], ffn_ref[0:1, :], loss_ref[0:1, :],
            jnp.concatenate([dlb, hgn], axis=1),
            jnp.concatenate([pool_ref[0:1, :], jnp.zeros((1, half), F32)], axis=1),
            loss_ref[1:2, :]], axis=0)
        wp_buf[0] = wp_ref[...]
        pair = [_rcopy(slab_buf.at[0], sib_s, ssem.at[0], rsem.at[0], sib),
                _rcopy(wp_buf.at[0], sib_w, ssem.at[1], rsem.at[1], sib)]
        for cp in pair:
            cp.start()
        for cp in pair:
            cp.wait()
        slab_buf[0] = slab_buf[0] + sib_s[...]
        wp_buf[0] = wp_buf[0] + sib_w[...]
        cps = []
        for r, (px, py) in enumerate(peers):
            cps.append(_rcopy(slab_buf.at[0], slab_buf.at[r + 1], ssem.at[2 + 2 * r], rsem.at[2 + 2 * r], (px, py, c)))
            cps.append(_rcopy(wp_buf.at[0], wp_buf.at[r + 1], ssem.at[3 + 2 * r], rsem.at[3 + 2 * r], (px, py, c)))
        for cp in cps:
            cp.start()
        for cp in cps:
            cp.wait()
        tot_s, tot_w = slab_buf[chip], wp_buf[chip]
        for j in range(1, N_CHIPS):
            tot_s = tot_s + slab_buf[jnp.bitwise_xor(j, chip)]
            tot_w = tot_w + wp_buf[jnp.bitwise_xor(j, chip)]
        slab_out[...] = tot_s
        wp_out[...] = tot_w
        plan.finish(cin, cout, csem)

    return pl.pallas_call(
        body, name="small_allreduce",
        in_specs=[VMEM] * 8 + [ANY] * n, out_specs=[VMEM] * 2 + [ANY] * n,
        out_shape=[jax.ShapeDtypeStruct((8, d), F32), jax.ShapeDtypeStruct(wps, F32)] + list(plan.out_shape),
        scratch_shapes=[pltpu.VMEM((N_CHIPS, 8, d), F32), pltpu.VMEM((N_CHIPS,) + wps, F32),
                        pltpu.VMEM((8, d), F32), pltpu.VMEM(wps, F32),
                        pltpu.SemaphoreType.DMA((8,)), pltpu.SemaphoreType.DMA((8,))] + list(plan.scratch_shapes),
    )(stats["mix"], stats["x"], stats["mem"], stats["ffn"], stats["loss"], stats["hgrn"], stats["pool"], d_wp,
      *plan_args)


def _adamw_math(w, g, m, v):
    m = ADAM_B1 * m + (1.0 - ADAM_B1) * g
    v = ADAM_B2 * v + (1.0 - ADAM_B2) * (g * g)
    m_hat = m / (1.0 - ADAM_B1 ** ADAM_STEP)
    v_hat = v / (1.0 - ADAM_B2 ** ADAM_STEP)
    delta = -ADAM_LR * (m_hat / (jnp.sqrt(v_hat) + ADAM_EPS) + ADAM_WD * w)
    return delta, m, v


def _adamw(name, mine, theirs, w, m, v, idx, tr):
    rows = w.shape[0]
    cc = mine.shape[1]
    nb = rows // 2 // tr
    heads = w.shape[1] if w.ndim == 3 else 1
    e = cc // heads

    def body(idx_ref, a_ref, b_ref, w_ref, m_ref, v_ref, g_out, d_out, m_out, v_out):
        g = jnp.where(pl.program_id(0) // nb == idx_ref[0], a_ref[...], b_ref[...])
        if w.ndim == 2:
            g_out[...] = g
            d_out[...], m_out[...], v_out[...] = _adamw_math(w_ref[...], g, m_ref[...], v_ref[...])
        else:
            for h in range(heads):
                gh = g[:, h * e:(h + 1) * e]
                g_out[:, h, :] = gh
                d_out[:, h, :], m_out[:, h, :], v_out[:, h, :] = _adamw_math(
                    w_ref[:, h, :], gh, m_ref[:, h, :], v_ref[:, h, :])

    hspec = pl.BlockSpec((tr, cc), lambda i, idx: (i % nb, 0))
    spec = pl.BlockSpec((tr,) + w.shape[1:], lambda i, idx: (i,) + (0,) * (w.ndim - 1))
    return pl.pallas_call(
        body, name=name,
        grid_spec=pltpu.PrefetchScalarGridSpec(
            num_scalar_prefetch=1, grid=(rows // tr,),
            in_specs=[hspec, hspec, spec, spec, spec], out_specs=[spec] * 4),
        out_shape=[jax.ShapeDtypeStruct(w.shape, F32)] * 4,
        compiler_params=_params(("parallel",)),
    )(idx, mine, theirs, w, m, v)


SMALL_NAMES = ("norm_mix_g", "lb_logits", "hgrn_norm_g", "w_pool", "pool_scale", "norm_x_g", "norm_mem_g",
               "norm_ffn_g", "final_norm_g")


def _small_update(slab, d_wp, ws, ms, vs):
    n = len(SMALL_NAMES)
    half = D_MODEL // 2

    def body(slab_ref, wp_ref, *refs):
        w_refs, m_refs, v_refs, outs = refs[:n], refs[n:2 * n], refs[2 * n:3 * n], refs[3 * n:]
        row = lambda k: slab_ref[k:k + 1, :]
        lbl = w_refs[SMALL_NAMES.index("lb_logits")][...]
        s0 = _lower_bound(lbl[0:1, :], lbl[1:2, :])
        dl0 = row(ROW_LB_HGN)[:, :half] * s0 * (1.0 - s0)
        grads = dict(norm_mix_g=row(ROW_GMIX), lb_logits=jnp.concatenate([dl0, -dl0], axis=0),
                     hgrn_norm_g=row(ROW_LB_HGN)[:, half:], w_pool=wp_ref[...], pool_scale=row(ROW_PSCALE)[:, :half],
                     norm_x_g=row(ROW_GX), norm_mem_g=row(ROW_GMEM), norm_ffn_g=row(ROW_GFFN),
                     final_norm_g=row(ROW_GFIN))
        outs[0][...] = row(ROW_LOSS)[:, :128]
        for i, name in enumerate(SMALL_NAMES):
            g = grads[name]
            delta, m2, v2 = _adamw_math(w_refs[i][...], g, m_refs[i][...], v_refs[i][...])
            for o, val in zip(outs[1 + 4 * i:5 + 4 * i], (g, delta, m2, v2)):
                o[...] = val

    args = [ws[k] for k in SMALL_NAMES] + [ms[k] for k in SMALL_NAMES] + [vs[k] for k in SMALL_NAMES]
    out_shape = [jax.ShapeDtypeStruct((1, 128), F32)]
    for k in SMALL_NAMES:
        out_shape += [jax.ShapeDtypeStruct(ws[k].shape, F32)] * 4
    res = pl.pallas_call(
        body, name="small_update",
        in_specs=[VMEM] * (2 + 3 * n), out_specs=[VMEM] * len(out_shape), out_shape=out_shape,
    )(slab, d_wp, *args)
    return res[0], {k: res[1 + 4 * i:5 + 4 * i] for i, k in enumerate(SMALL_NAMES)}


ALL_NAMES = ("norm_mix_g", "w_in", "lb_logits", "hgrn_norm_g", "w_pool", "pool_scale", "w_out", "norm_x_g",
             "norm_mem_g", "w_xq", "w_xk", "w_xv", "w_xo", "norm_ffn_g", "w_ff1", "w_ff2", "final_norm_g")


def _shard_2d(name, a):
    a = a[0]
    if name in ("w_xq", "w_xk", "w_xv"):
        return a.reshape(a.shape[0], -1)
    if name == "w_xo":
        return a.reshape(-1, a.shape[-1])
    return a


def _small_2d(name, a):
    if name == "w_pool":
        return a.reshape(-1, HEAD_DIM)
    if name == "lb_logits":
        return a
    return a.reshape(1, -1)


def kernel(x, mem, norm_mix_g, w_in, lb_logits, hgrn_norm_g, w_pool, pool_scale, w_out, norm_x_g, norm_mem_g, w_xq, w_xk, w_xv, w_xo, norm_ffn_g, w_ff1, w_ff2, final_norm_g, loss_target, m_norm_mix_g, m_w_in, m_lb_logits, m_hgrn_norm_g, m_w_pool, m_pool_scale, m_w_out, m_norm_x_g, m_norm_mem_g, m_w_xq, m_w_xk, m_w_xv, m_w_xo, m_norm_ffn_g, m_w_ff1, m_w_ff2, m_final_norm_g, v_norm_mix_g, v_w_in, v_lb_logits, v_hgrn_norm_g, v_w_pool, v_pool_scale, v_w_out, v_norm_x_g, v_norm_mem_g, v_w_xq, v_w_xk, v_w_xv, v_w_xo, v_norm_ffn_g, v_w_ff1, v_w_ff2, v_final_norm_g):
    w = dict(norm_mix_g=norm_mix_g, w_in=w_in, lb_logits=lb_logits, hgrn_norm_g=hgrn_norm_g, w_pool=w_pool, pool_scale=pool_scale, w_out=w_out, norm_x_g=norm_x_g, norm_mem_g=norm_mem_g, w_xq=w_xq, w_xk=w_xk, w_xv=w_xv, w_xo=w_xo, norm_ffn_g=norm_ffn_g, w_ff1=w_ff1, w_ff2=w_ff2, final_norm_g=final_norm_g)
    m = dict(norm_mix_g=m_norm_mix_g, w_in=m_w_in, lb_logits=m_lb_logits, hgrn_norm_g=m_hgrn_norm_g, w_pool=m_w_pool, pool_scale=m_pool_scale, w_out=m_w_out, norm_x_g=m_norm_x_g, norm_mem_g=m_norm_mem_g, w_xq=m_w_xq, w_xk=m_w_xk, w_xv=m_w_xv, w_xo=m_w_xo, norm_ffn_g=m_norm_ffn_g, w_ff1=m_w_ff1, w_ff2=m_w_ff2, final_norm_g=m_final_norm_g)
    v = dict(norm_mix_g=v_norm_mix_g, w_in=v_w_in, lb_logits=v_lb_logits, hgrn_norm_g=v_hgrn_norm_g, w_pool=v_w_pool, pool_scale=v_pool_scale, w_out=v_w_out, norm_x_g=v_norm_x_g, norm_mem_g=v_norm_mem_g, w_xq=v_w_xq, w_xk=v_w_xk, w_xv=v_w_xv, w_xo=v_w_xo, norm_ffn_g=v_norm_ffn_g, w_ff1=v_w_ff1, w_ff2=v_w_ff2, final_norm_g=v_final_norm_g)

    big_w = {k: _shard_2d(k, w[k]) for k in BIG_NAMES}
    slab_oq = jnp.concatenate([big_w["w_out"], big_w["w_xq"]], axis=0).astype(BF16)
    slab_kv = jnp.concatenate([big_w["w_xk"], big_w["w_xv"]], axis=0).astype(BF16)
    shards = dict(slab_oq=slab_oq, slab_kv=slab_kv,
                  **{k: big_w[k].astype(BF16) for k in ("w_in", "w_xo", "w_ff1", "w_ff2")})

    cx, cy, cc = lax.axis_index("x"), lax.axis_index("y"), lax.axis_index("c")
    chip = 2 * cx + cy
    idx = jnp.stack([cc, chip, chip ^ 1, chip ^ 2, chip ^ 3]).astype(jnp.int32)
    small = {k: w[k] for k in SMALL_NAMES}
    grad_x, stats, d_wp, partials, keep_in, send_in = _step(x[0], mem[0], loss_target[0], small, shards, idx)

    chip_add = lambda k, keep, got: _grad_chip_add("grad_chip_add_" + k, keep, got, tr=min(256, keep.shape[0]))
    halves = {k: chip_add(k, *partials[k]) for k in EARLY_NAMES}
    early = [halves[k] for k in EARLY_NAMES]
    slab_sum, wp_sum, recv_in, *their_early = _small_allreduce(
        stats, d_wp.reshape(-1, HEAD_DIM), _Plans([_ChipExchange(send_in), _HalfExchange(early)]), send_in + early)
    halves["w_in"] = chip_add("w_in", keep_in[0], recv_in)
    theirs = dict(zip(EARLY_NAMES, their_early))
    (theirs["w_in"],) = _run_exchange("grad_half_exchange_w_in", _HalfExchange([halves["w_in"]]), [halves["w_in"]])

    grads, deltas, new_m, new_v = {}, {}, {}, {}
    for k in BIG_NAMES:
        as_held = (lambda a: a[0]) if k in ("w_xq", "w_xk", "w_xv") else functools.partial(_shard_2d, k)
        res = _adamw("adamw_" + k, halves[k], theirs[k], as_held(w[k]), as_held(m[k]), as_held(v[k]), idx,
                     tr=min(256, halves[k].shape[0]))
        for store, val in zip((grads, deltas, new_m, new_v), res):
            store[k] = val.reshape(w[k].shape)

    loss, upd = _small_update(slab_sum, wp_sum, {k: _small_2d(k, w[k]) for k in SMALL_NAMES},
                              {k: _small_2d(k, m[k]) for k in SMALL_NAMES}, {k: _small_2d(k, v[k]) for k in SMALL_NAMES})
    for k in SMALL_NAMES:
        for store, val in zip((grads, deltas, new_m, new_v), upd[k]):
            store[k] = val.reshape(w[k].shape)

    return (loss[0, 0], grad_x[None], *[grads[k] for k in ALL_NAMES], *[deltas[k] for k in ALL_NAMES],
            *[new_m[k] for k in ALL_NAMES], *[new_v[k] for k in ALL_NAMES])
```

```python
import functools

import jax
import jax.numpy as jnp
from jax import lax
from jax.experimental import pallas as pl
from jax.experimental.pallas import tpu as pltpu

F32 = jnp.float32
BF16 = jnp.bfloat16
LOG2E = 1.4426950408889634
NEG_BIG = -1e30
MAX_LOG2_GROWTH = 100.0
MESH = pl.DeviceIdType.MESH
ANY = pl.BlockSpec(memory_space=pl.ANY)
VMEM = pl.BlockSpec(memory_space=pltpu.VMEM)

D_MODEL = 1024
N_CHIPS = 4
HGRN_HEADS = 4
HEAD_DIM = 128
HGRN_WIDTH = HGRN_HEADS * HEAD_DIM
POOL_WINDOWS = (2, 4, 8, 16)
POOL_HALO = 16
SUB = 16
HALF = SUB // 2
CHUNK = 64
HEADS_PER_STEP = 4
XATTN_HEADS = 4
XATTN_HEAD_DIM = 256
EPS = 1e-6
ADAM_LR, ADAM_B1, ADAM_B2, ADAM_EPS, ADAM_WD, ADAM_STEP = 0.001, 0.9, 0.999, 1e-08, 0.01, 10

TOKEN_TILE = 512
MLP_BWD_TOKEN_TILE = 256
HGRN_BLOCK = 256
GRAD_TILE = 1024

V7X_VMEM_BYTES = 64 * 1024 * 1024
VMEM_LIMIT = V7X_VMEM_BYTES - 8 * 1024 * 1024

NN = (((1,), (0,)), ((), ()))
NT = (((1,), (1,)), ((), ()))
TN = (((0,), (0,)), ((), ()))

ROW_GMIX, ROW_GX, ROW_GMEM, ROW_GFFN, ROW_GFIN, ROW_LB_HGN, ROW_PSCALE, ROW_LOSS = range(8)


def _dot(a, b, dims=NN):
    return lax.dot_general(a, b, dims, preferred_element_type=F32)


def _sigmoid(x):
    return 1.0 / (1.0 + jnp.exp(-x))


def _rms_fwd(x, g):
    r = lax.rsqrt(jnp.mean(x * x, axis=-1, keepdims=True) + EPS)
    n = x * r
    return n * g, n, r


def _rms_bwd(dh, n, r, g):
    dn = dh * g
    dx = r * (dn - n * jnp.mean(dn * n, axis=-1, keepdims=True))
    return dx, jnp.sum(dh * n, axis=0, keepdims=True)


def _params(sem=None):
    return pltpu.CompilerParams(dimension_semantics=sem, vmem_limit_bytes=VMEM_LIMIT)


def _const(shape):
    nd = len(shape)
    return pl.BlockSpec(shape, lambda *_: (0,) * nd, pipeline_mode=pl.Buffered(1))


def _const_out(shape):
    nd = len(shape)
    return pl.BlockSpec(shape, lambda *_: (0,) * nd)


def _acc_rows(ref, t, rows):
    upd = jnp.concatenate(rows + [jnp.zeros((8 - len(rows), rows[0].shape[1]), F32)], axis=0)

    @pl.when(t == 0)
    def _():
        ref[...] = upd

    @pl.when(t > 0)
    def _():
        ref[...] = ref[...] + upd


def _fuse_exchange(body, n_in, n_out, n_scratch, plan, ndim):
    if plan is None:
        return body
    n = plan.n

    def wrapped(*refs):
        ins, cin = refs[:n_in], refs[n_in:n_in + n]
        outs, cout = refs[n_in + n:n_in + n + n_out], refs[n_in + n + n_out:n_in + 2 * n + n_out]
        rest = refs[n_in + 2 * n + n_out:]
        scr, csem = rest[:n_scratch], rest[n_scratch:]
        first = pl.program_id(0) == 0
        last = pl.program_id(0) == pl.num_programs(0) - 1
        for i in range(1, ndim):
            first = first & (pl.program_id(i) == 0)
            last = last & (pl.program_id(i) == pl.num_programs(i) - 1)

        @pl.when(first)
        def _():
            plan.start(cin, cout, csem)

        if hasattr(plan, "middle"):
            step, total = pl.program_id(0), pl.num_programs(0)
            for i in range(1, ndim):
                step, total = step * pl.num_programs(i) + pl.program_id(i), total * pl.num_programs(i)

            @pl.when(step == total // 2)
            def _():
                plan.middle(cin, cout, csem)

        body(*ins, *outs, *scr)

        @pl.when(last)
        def _():
            plan.finish(cin, cout, csem)

    return wrapped


def _plan_extras(plan):
    if plan is None:
        return [], [], []
    return [ANY] * plan.n, list(plan.out_shape), list(plan.scratch_shapes)


def _in_proj(x, g, win_g, tm, plan=None, plan_args=()):
    s, d = x.shape
    nsh, _, wc = win_g.shape

    def body(x_ref, g_ref, w_ref, z_ref, h_ref):
        h, _, _ = _rms_fwd(x_ref[...], g_ref[...])
        hb = h.astype(BF16)
        h_ref[...] = hb
        for j in range(nsh):
            z_ref[:, j * wc:(j + 1) * wc] = _dot(hb, w_ref[j])

    x_specs, x_shapes, x_scratch = _plan_extras(plan)
    return pl.pallas_call(
        _fuse_exchange(body, 3, 2, 0, plan, 1), name="in_proj", grid=(s // tm,),
        in_specs=[pl.BlockSpec((tm, d), lambda t: (t, 0)), _const((1, d)), _const((nsh, d, wc))] + x_specs,
        out_specs=[pl.BlockSpec((tm, nsh * wc), lambda t: (t, 0)), pl.BlockSpec((tm, d), lambda t: (t, 0))] + x_specs,
        out_shape=[jax.ShapeDtypeStruct((s, nsh * wc), F32), jax.ShapeDtypeStruct((s, d), BF16)] + x_shapes,
        scratch_shapes=x_scratch,
        compiler_params=_params(("arbitrary",)),
    )(x, g, win_g, *plan_args)


def _lower_bound(l0, l1):
    m = jnp.maximum(l0, l1)
    e0, e1 = jnp.exp(l0 - m), jnp.exp(l1 - m)
    return e0 / (e0 + e1)


def _block_tri(n, group, upper):
    r = lax.broadcasted_iota(jnp.int32, (n, n), 0)
    c = lax.broadcasted_iota(jnp.int32, (n, n), 1)
    keep = (r // group == c // group) & ((c >= r) if upper else (c <= r))
    return keep.astype(BF16)


def _group_cumsum(tri, x):
    hi = x.astype(BF16)
    rest = x - hi.astype(F32)
    mid = rest.astype(BF16)
    lo = (rest - mid.astype(F32)).astype(BF16)
    return (_dot(tri, hi) + _dot(tri, mid)) + _dot(tri, lo)


def _decay(b, bj, rows, first):
    d = b - bj
    if first:
        d = jnp.where(rows >= first, d, NEG_BIG)
    return jnp.exp2(d)


class _RowSums:
    ORDER = (0, 4, 2, 6, 1, 5, 3, 7)

    def __init__(self, rows):
        self.rows = rows
        self.level = {4: {}, 2: {}, 1: {}}

    def _pair(self, p, q, d):
        return jnp.where((self.rows & d) != 0, p + pltpu.roll(p, d, axis=0), q + pltpu.roll(q, HALF - d, axis=0))

    def push(self, j, y, d=4):
        if d == 0:
            self.out = y
            return
        slot = self.level[d]
        key = j % d
        if key not in slot:
            slot[key] = (j, y)
            return
        j0, y0 = slot.pop(key)
        p, q = (y, y0) if j & d else (y0, y)
        self.push(key, self._pair(p, q, d), d // 2)

    def result(self):
        return self.out


def _hgrn_gates(qp, fp, lb):
    sq = _sigmoid(qp)
    sf = _sigmoid(fp)
    f = lb + (1.0 - lb) * sf
    return qp * sq, sq, f, sf


def _hgrn_fwd(z, l0, l1, gn, tc, unroll=1, plan=None, plan_args=()):
    s = z.shape[0]
    nsub = tc // SUB
    hd = HEAD_DIM
    hp = HEADS_PER_STEP

    def body(q_ref, f_ref, v_ref, g_ref, l0_ref, l1_ref, gn_ref, tri_ref, tric_ref, o_ref, oa_ref, st_ref,
             state, qs, ks, bs, os_):
        @pl.when(pl.program_id(1) == 0)
        def _():
            state[...] = jnp.zeros_like(state)

        cols = [slice(hh * hd, (hh + 1) * hd) for hh in range(hp)]
        q, k, lf, bc = [], [], [], []
        for hh, cs in enumerate(cols):
            qh, _, fh, _ = _hgrn_gates(q_ref[:, cs], f_ref[:, cs], _lower_bound(l0_ref[hh], l1_ref[hh]))
            q.append(qh)
            k.append(1.0 - fh)
            lf.append(jnp.log(fh) * LOG2E)
            bc.append(_group_cumsum(tric_ref[...], lf[hh]))
        bounded = functools.reduce(jnp.minimum, [jnp.min(b) for b in bc]) >= -MAX_LOG2_GROWTH

        @pl.when(bounded)
        def _():
            mask = tric_ref[...] > 0
            for hh, cs in enumerate(cols):
                qt = (q[hh] * jnp.exp2(bc[hh])).astype(BF16)
                ki = (k[hh] * jnp.exp2(-bc[hh])).astype(BF16)
                vb = v_ref[:, cs].astype(BF16)
                a = jnp.where(mask, _dot(qt, ki, NT), 0.0).astype(BF16)
                o_in = _dot(a, vb)
                for c in range(tc // CHUNK):
                    rs = slice(c * CHUNK, (c + 1) * CHUNK)
                    st = state[hh]
                    st_ref[hh, c] = st
                    os_[rs, cs] = o_in[rs] + _dot(qt[rs], st.astype(BF16), NT)
                    bl = bc[hh][(c + 1) * CHUNK - 1:(c + 1) * CHUNK, :]
                    kt = (k[hh][rs] * jnp.exp2(bl - bc[hh][rs])).astype(BF16)
                    state[hh] = st * jnp.exp2(bl) + _dot(vb[rs], kt, TN)

        @pl.when(jnp.logical_not(bounded))
        def _():
            rows = lax.broadcasted_iota(jnp.int32, (HALF, 1), 0)
            for hh, cs in enumerate(cols):
                qs[:, cs] = q[hh]
                ks[:, cs] = k[hh]
                bs[:, cs] = _group_cumsum(tri_ref[...], lf[hh])

                def step(i, carry, hh=hh, cs=cs):
                    r0 = pl.multiple_of(i * SUB, SUB)
                    q_ = qs[pl.ds(r0, SUB), cs]
                    k_ = ks[pl.ds(r0, SUB), cs]
                    b_ = bs[pl.ds(r0, SUB), cs]
                    v_ = v_ref[pl.ds(r0, SUB), cs]
                    st = state[hh]

                    @pl.when(i % (CHUNK // SUB) == 0)
                    def _():
                        st_ref[hh, i // (CHUNK // SUB)] = st

                    bl = b_[SUB - 1:SUB, :]
                    o = _dot((q_ * jnp.exp2(b_)).astype(BF16), st.astype(BF16), NT)
                    (q_lo, q_hi), (b_lo, b_hi), (o_lo, o_hi) = ((x[:HALF], x[HALF:]) for x in (q_, b_, o))
                    for j in range(SUB):
                        bj, kj, vj = b_[j:j + 1, :], k_[j:j + 1, :], v_[j:j + 1, :]
                        if j < HALF:
                            e = _decay(b_lo, bj, rows, j)
                            o_lo = o_lo + jnp.sum(q_lo * e * kj, axis=-1, keepdims=True) * vj
                        e = _decay(b_hi, bj, rows, j - HALF if j > HALF else None)
                        o_hi = o_hi + jnp.sum(q_hi * e * kj, axis=-1, keepdims=True) * vj
                    os_[pl.ds(r0, HALF), cs] = o_lo
                    os_[pl.ds(r0 + HALF, HALF), cs] = o_hi
                    kt = (k_ * jnp.exp2(bl - b_)).astype(BF16)
                    state[hh] = st * jnp.exp2(bl) + _dot(v_.astype(BF16), kt, TN)
                    return carry

                lax.fori_loop(0, nsub, step, 0, unroll=unroll)

        for hh, cs in enumerate(cols):
            o = os_[:, cs]
            o_ref[:, cs] = o
            r = lax.rsqrt(jnp.mean(o * o, axis=-1, keepdims=True) + EPS)
            gp = g_ref[:, cs]
            oa_ref[:, cs] = (o * r * gn_ref[hh] * (gp * _sigmoid(gp))).astype(BF16)

    ng = HGRN_HEADS // hp
    col = lambda k: pl.BlockSpec((tc, hp * hd), lambda h, t: (t, k * ng + h))
    vec = pl.BlockSpec((hp, 1, hd), lambda h, t: (h, 0, 0))
    out = pl.BlockSpec((tc, hp * hd), lambda h, t: (t, h))
    x_specs, x_shapes, x_scratch = _plan_extras(plan)
    return pl.pallas_call(
        _fuse_exchange(body, 9, 3, 5, plan, 2), name="hgrn_fwd", grid=(ng, s // tc),
        in_specs=[col(0), col(1), col(2), col(3), vec, vec, vec, _const((tc, tc)), _const((tc, tc))] + x_specs,
        out_specs=[out, out, pl.BlockSpec((hp, tc // CHUNK, hd, hd), lambda h, t: (h, t, 0, 0))] + x_specs,
        out_shape=[jax.ShapeDtypeStruct((s, HGRN_WIDTH), F32), jax.ShapeDtypeStruct((s, HGRN_WIDTH), BF16),
                   jax.ShapeDtypeStruct((HGRN_HEADS, s // CHUNK, hd, hd), F32)] + x_shapes,
        scratch_shapes=[pltpu.VMEM((hp, hd, hd), F32)] + [pltpu.VMEM((tc, hp * hd), F32)] * 4 + x_scratch,
        compiler_params=_params(("arbitrary", "arbitrary")),
    )(z, z, z, z, l0, l1, gn, _block_tri(tc, SUB, False), _block_tri(tc, CHUNK, False), *plan_args)


def _pooled(p, ext, tok0):
    tm = p.shape[0]
    tok = tok0 + lax.broadcasted_iota(jnp.int32, (tm, 1), 0)
    outs = []
    for g, w in enumerate(POOL_WINDOWS):
        acc = ext[:, g * HEAD_DIM:(g + 1) * HEAD_DIM]
        sh = 1
        while sh < w:
            acc = acc + pltpu.roll(acc, sh, axis=0)
            sh *= 2
        cnt = jnp.minimum(tok + 1, w).astype(F32)
        outs.append(acc[POOL_HALO:, :] / cnt - p[:, g * HEAD_DIM:(g + 1) * HEAD_DIM])
    return outs


def _pool_fwd(z, wp, scale, tm):
    s = z.shape[0]
    pw = len(POOL_WINDOWS) * HEAD_DIM
    nb = tm // POOL_HALO

    def body(p_ref, prev_ref, wp_ref, sc_ref, ob_ref):
        t = pl.program_id(0)
        p = p_ref[...]
        prev = jnp.where(t > 0, prev_ref[...], 0.0)
        pooled = _pooled(p, jnp.concatenate([prev, p], axis=0), t * tm)
        ys = [_dot(pooled[g].astype(BF16), wp_ref[g].astype(BF16)) for g in range(len(POOL_WINDOWS))]
        ob_ref[...] = (jnp.concatenate(ys, axis=1) * sc_ref[...]).astype(BF16)

    return pl.pallas_call(
        body, name="pool_fwd", grid=(s // tm,),
        in_specs=[pl.BlockSpec((tm, pw), lambda t: (t, 4)),
                  pl.BlockSpec((POOL_HALO, pw), lambda t: (jnp.maximum(t * nb - 1, 0), 4)),
                  _const(wp.shape), _const((1, pw))],
        out_specs=pl.BlockSpec((tm, pw), lambda t: (t, 0)),
        out_shape=jax.ShapeDtypeStruct((s, pw), BF16),
        compiler_params=_params(("parallel",)),
    )(z, z, wp, scale)


def _kv_proj(mem, g, slab_g):
    m, d = mem.shape
    rows = d // N_CHIPS

    def body(mem_ref, g_ref, wk_ref, wv_ref, xk_ref, xv_ref):
        hm, _, _ = _rms_fwd(mem_ref[...], g_ref[...])
        hb = hm.astype(BF16)
        xk_ref[...] = _dot(hb, wk_ref[...].reshape(d, d)).astype(BF16)
        xv_ref[...] = _dot(hb, wv_ref[...].reshape(d, d)).astype(BF16)

    blk = lambda k: pl.BlockSpec((N_CHIPS, rows, d), lambda i: (0, k, 0))
    return pl.pallas_call(
        body, name="kv_proj", grid=(1,),
        in_specs=[_const((m, d)), _const((1, d)), blk(0), blk(1)],
        out_specs=[_const_out((m, d)), _const_out((m, d))],
        out_shape=[jax.ShapeDtypeStruct((m, d), BF16)] * 2,
        compiler_params=_params(("arbitrary",)),
    )(mem, g, slab_g, slab_g)


def _softmax_rows(sc):
    e = jnp.exp(sc - jnp.max(sc, axis=-1, keepdims=True))
    return e / jnp.sum(e, axis=-1, keepdims=True)


def _mix_xattn_fwd(x, oa, ob, gx, slab_g, wo_g, xk, xv, tm, plan=None, plan_args=()):
    s, d = x.shape
    m = xk.shape[0]
    rows = d // N_CHIPS
    hw = oa.shape[1]
    e = XATTN_HEAD_DIM

    def body(x_ref, oa_ref, ob_ref, gx_ref, wout_ref, wq_ref, wo_ref, xk_ref, xv_ref,
             x1_ref, mixed_ref, hq_ref, xq_ref, att_ref, x2_ref):
        mixed = jnp.concatenate([oa_ref[...], ob_ref[...]], axis=1)
        mixed_ref[...] = mixed
        x1 = x_ref[...] + _dot(mixed, wout_ref[...].reshape(d, d))
        x1_ref[...] = x1
        hq, _, _ = _rms_fwd(x1, gx_ref[...])
        hqb = hq.astype(BF16)
        hq_ref[...] = hqb
        xq = _dot(hqb, wq_ref[...].reshape(d, d)).astype(BF16)
        xq_ref[...] = xq
        atts = []
        for h in range(XATTN_HEADS):
            cs = slice(h * e, (h + 1) * e)
            p = _softmax_rows(_dot(xq[:, cs], xk_ref[:, cs], NT) * (e ** -0.5))
            atts.append(_dot(p.astype(BF16), xv_ref[:, cs]).astype(BF16))
        att = jnp.concatenate(atts, axis=1)
        att_ref[...] = att
        for j in range(N_CHIPS):
            x2_ref[:, j * rows:(j + 1) * rows] = x1[:, j * rows:(j + 1) * rows] + _dot(att, wo_ref[j])

    tile = lambda w: pl.BlockSpec((tm, w), lambda t: (t, 0))
    blk = lambda k: pl.BlockSpec((N_CHIPS, rows, d), lambda t: (0, k, 0), pipeline_mode=pl.Buffered(1))
    x_specs, x_shapes, x_scratch = _plan_extras(plan)
    return pl.pallas_call(
        _fuse_exchange(body, 9, 6, 0, plan, 1), name="mix_xattn_fwd", grid=(s // tm,),
        in_specs=[tile(d), tile(hw), tile(hw), _const((1, d)), blk(0), blk(1), _const(wo_g.shape),
                  _const((m, d)), _const((m, d))] + x_specs,
        out_specs=[tile(d)] * 6 + x_specs,
        out_shape=[jax.ShapeDtypeStruct((s, d), F32)] + [jax.ShapeDtypeStruct((s, d), BF16)] * 4
                  + [jax.ShapeDtypeStruct((s, d), F32)] + x_shapes,
        scratch_shapes=x_scratch,
        compiler_params=_params(("arbitrary",)),
    )(x, oa, ob, gx, slab_g, slab_g, wo_g, xk, xv, *plan_args)


def _mlp_loss_fwd(x2, gffn, gfin, w1_g, w2_g, target, tm):
    s, d = x2.shape
    wr = w1_g.shape[1]

    def body(x2_ref, gffn_ref, gfin_ref, w1_ref, w2_ref, tg_ref, a_ref, hf_ref, dx3_ref, dx3b_ref, st_ref):
        x2v = x2_ref[...]
        hf, _, _ = _rms_fwd(x2v, gffn_ref[...])
        hfb = hf.astype(BF16)
        hf_ref[...] = hfb
        acc = x2v
        for j in range(N_CHIPS):
            a = _dot(hfb, w1_ref[j])
            a_ref[:, j * wr:(j + 1) * wr] = a
            r = jnp.maximum(a, 0.0)
            acc = acc + _dot((r * r).astype(BF16), w2_ref[j])
        gf = gfin_ref[...]
        y, n, r3 = _rms_fwd(acc, gf)
        err = y - tg_ref[...]
        loss = 0.5 * jnp.sum(jnp.sum(err * err, axis=-1, keepdims=True) * (1.0 / d), axis=0, keepdims=True)
        dy = err * (1.0 / d)
        dx3, dgf = _rms_bwd(dy, n, r3, gf)
        dx3_ref[...] = dx3
        dx3b_ref[...] = dx3.astype(BF16)
        _acc_rows(st_ref, pl.program_id(0), [dgf, jnp.broadcast_to(loss, (1, d))])

    tile = lambda w: pl.BlockSpec((tm, w), lambda t: (t, 0))
    blk = lambda k: pl.BlockSpec((N_CHIPS, wr, d), lambda t: (0, k, 0), pipeline_mode=pl.Buffered(1))
    return pl.pallas_call(
        body, name="mlp_loss_fwd", grid=(s // tm,),
        in_specs=[tile(d), _const((1, d)), _const((1, d)), blk(0), blk(0), tile(d)],
        out_specs=[tile(N_CHIPS * wr), tile(d), tile(d), tile(d), _const_out((8, d))],
        out_shape=[jax.ShapeDtypeStruct((s, N_CHIPS * wr), F32), jax.ShapeDtypeStruct((s, d), BF16),
                   jax.ShapeDtypeStruct((s, d), F32), jax.ShapeDtypeStruct((s, d), BF16),
                   jax.ShapeDtypeStruct((8, d), F32)],
        compiler_params=_params(("arbitrary",)),
    )(x2, gffn, gfin, w1_g, w2_g, target)


def _mlp_bwd(dx3, dx3b, a, x2, gffn, w1_g, w2_g, tm):
    s, d = x2.shape
    wr = w1_g.shape[1]

    def body(dx3_ref, dx3b_ref, a_ref, x2_ref, g_ref, w1_ref, w2_ref, da_ref, u_ref, dx2_ref, dx2b_ref, st_ref):
        dyb = dx3b_ref[...]
        dhf = jnp.zeros((tm, d), F32)
        for j in range(N_CHIPS):
            r = jnp.maximum(a_ref[:, j * wr:(j + 1) * wr], 0.0)
            da = (_dot(dyb, w2_ref[j], NT) * (2.0 * r)).astype(BF16)
            da_ref[:, j * wr:(j + 1) * wr] = da
            u_ref[:, j * wr:(j + 1) * wr] = (r * r).astype(BF16)
            dhf = dhf + _dot(da, w1_ref[j], NT)
        g = g_ref[...]
        _, n, r2 = _rms_fwd(x2_ref[...], g)
        dxn, dg = _rms_bwd(dhf, n, r2, g)
        dx2 = dx3_ref[...] + dxn
        dx2_ref[...] = dx2
        dx2b_ref[...] = dx2.astype(BF16)
        _acc_rows(st_ref, pl.program_id(0), [dg])

    tile = lambda w: pl.BlockSpec((tm, w), lambda t: (t, 0))
    blk = lambda k: pl.BlockSpec((N_CHIPS, wr, d), lambda t: (0, k, 0), pipeline_mode=pl.Buffered(1))
    nf = N_CHIPS * wr
    return pl.pallas_call(
        body, name="mlp_bwd", grid=(s // tm,),
        in_specs=[tile(d), tile(d), tile(nf), tile(d), _const((1, d)), blk(0), blk(0)],
        out_specs=[tile(nf), tile(nf), tile(d), tile(d), _const_out((8, d))],
        out_shape=[jax.ShapeDtypeStruct((s, nf), BF16), jax.ShapeDtypeStruct((s, nf), BF16),
                   jax.ShapeDtypeStruct((s, d), F32), jax.ShapeDtypeStruct((s, d), BF16),
                   jax.ShapeDtypeStruct((8, d), F32)],
        compiler_params=_params(("arbitrary",)),
    )(dx3, dx3b, a, x2, gffn, w1_g, w2_g)


def _xattn_mix_bwd(dx2, x1, xq, xk, xv, gx, slab_g, wo_g, tm, plan=None, plan_args=()):
    s, d = x1.shape
    m = xk.shape[0]
    rows = d // N_CHIPS
    e = XATTN_HEAD_DIM

    def body(dx2_ref, x1_ref, xq_ref, xk_ref, xv_ref, gx_ref, wout_ref, wq_ref, wo_ref,
             dx1_ref, dx1b_ref, dxq_ref, dmix_ref, dxk_ref, dxv_ref, st_ref):
        t = pl.program_id(0)
        dx2 = dx2_ref[...]
        dx2b = dx2.astype(BF16)
        datt = jnp.zeros((tm, d), F32)
        for j in range(N_CHIPS):
            datt = datt + _dot(dx2b[:, j * rows:(j + 1) * rows], wo_ref[j], NT)
        dattb = datt.astype(BF16)
        dxqs, dxks, dxvs = [], [], []
        for h in range(XATTN_HEADS):
            cs = slice(h * e, (h + 1) * e)
            xq_h, xk_h, xv_h = xq_ref[:, cs], xk_ref[:, cs], xv_ref[:, cs]
            p = _softmax_rows(_dot(xq_h, xk_h, NT) * (e ** -0.5))
            dp = _dot(dattb[:, cs], xv_h, NT)
            ds = (p * (dp - jnp.sum(dp * p, axis=-1, keepdims=True)) * (e ** -0.5)).astype(BF16)
            dxqs.append(_dot(ds, xk_h).astype(BF16))
            dxks.append(_dot(ds, xq_h, TN))
            dxvs.append(_dot(p.astype(BF16), dattb[:, cs], TN))
        dxq = jnp.concatenate(dxqs, axis=1)
        dxq_ref[...] = dxq
        dxk = jnp.concatenate(dxks, axis=1)
        dxv = jnp.concatenate(dxvs, axis=1)

        @pl.when(t == 0)
        def _():
            dxk_ref[...] = dxk
            dxv_ref[...] = dxv

        @pl.when(t > 0)
        def _():
            dxk_ref[...] = dxk_ref[...] + dxk
            dxv_ref[...] = dxv_ref[...] + dxv

        dhq = jnp.concatenate([_dot(dxq, wq_ref[j], NT) for j in range(N_CHIPS)], axis=1)
        g = gx_ref[...]
        _, n, r1 = _rms_fwd(x1_ref[...], g)
        dxn, dg = _rms_bwd(dhq, n, r1, g)
        dx1 = dx2 + dxn
        dx1_ref[...] = dx1
        dx1b = dx1.astype(BF16)
        dx1b_ref[...] = dx1b
        for j in range(N_CHIPS):
            dmix_ref[:, j * rows:(j + 1) * rows] = _dot(dx1b, wout_ref[j], NT)
        _acc_rows(st_ref, t, [dg])

    tile = lambda: pl.BlockSpec((tm, d), lambda t: (t, 0))
    blk = lambda k: pl.BlockSpec((N_CHIPS, rows, d), lambda t: (0, k, 0), pipeline_mode=pl.Buffered(1))
    x_specs, x_shapes, x_scratch = _plan_extras(plan)
    return pl.pallas_call(
        _fuse_exchange(body, 9, 7, 0, plan, 1), name="xattn_mix_bwd", grid=(s // tm,),
        in_specs=[tile(), tile(), tile(), _const((m, d)), _const((m, d)), _const((1, d)), blk(0), blk(1),
                  _const(wo_g.shape)] + x_specs,
        out_specs=[tile(), tile(), tile(), tile(), _const_out((m, d)), _const_out((m, d)), _const_out((8, d))]
                  + x_specs,
        out_shape=[jax.ShapeDtypeStruct((s, d), F32), jax.ShapeDtypeStruct((s, d), BF16),
                   jax.ShapeDtypeStruct((s, d), BF16), jax.ShapeDtypeStruct((s, d), F32),
                   jax.ShapeDtypeStruct((m, d), F32), jax.ShapeDtypeStruct((m, d), F32),
                   jax.ShapeDtypeStruct((8, d), F32)] + x_shapes,
        scratch_shapes=x_scratch,
        compiler_params=_params(("arbitrary",)),
    )(dx2, x1, xq, xk, xv, gx, slab_g, slab_g, wo_g, *plan_args)


def _kv_bwd(mem, g, dxk, dxv, slab_g):
    m, d = mem.shape
    rows = d // N_CHIPS

    def body(mem_ref, g_ref, dxk_ref, dxv_ref, wk_ref, wv_ref, dwk_ref, dwv_ref, st_ref):
        gv = g_ref[...]
        hm, n, _ = _rms_fwd(mem_ref[...], gv)
        hb = hm.astype(BF16)
        dkb = dxk_ref[...].astype(BF16)
        dvb = dxv_ref[...].astype(BF16)
        dhm = []
        for j in range(N_CHIPS):
            hj = hb[:, j * rows:(j + 1) * rows]
            dwk_ref[j] = _dot(hj, dkb, TN)
            dwv_ref[j] = _dot(hj, dvb, TN)
            dhm.append(_dot(dkb, wk_ref[j], NT) + _dot(dvb, wv_ref[j], NT))
        dg = jnp.sum(jnp.concatenate(dhm, axis=1) * n, axis=0, keepdims=True)
        st_ref[...] = jnp.concatenate([dg, jnp.zeros((7, d), F32)], axis=0)

    blk = lambda k: pl.BlockSpec((N_CHIPS, rows, d), lambda i: (0, k, 0))
    return pl.pallas_call(
        body, name="kv_bwd", grid=(1,),
        in_specs=[_const((m, d)), _const((1, d)), _const((m, d)), _const((m, d)), blk(0), blk(1)],
        out_specs=[_const_out((N_CHIPS, rows, d)), _const_out((N_CHIPS, rows, d)), _const_out((8, d))],
        out_shape=[jax.ShapeDtypeStruct((N_CHIPS, rows, d), F32)] * 2 + [jax.ShapeDtypeStruct((8, d), F32)],
        compiler_params=_params(("arbitrary",)),
    )(mem, g, dxk, dxv, slab_g, slab_g)


def _pool_bwd(z, dmix, wp, scale, tm, plan=None, plan_args=()):
    s = z.shape[0]
    ng = len(POOL_WINDOWS)
    pw = ng * HEAD_DIM
    nb = tm // POOL_HALO
    nt = s // tm
    n_ext = tm + POOL_HALO

    def body(p_ref, prev_ref, dm_ref, dmn_ref, wp_ref, sc_ref, dp_ref, dwp_ref, st_ref):
        t = pl.program_id(0)
        p = p_ref[...]
        prev = jnp.where(t > 0, prev_ref[...], 0.0)
        pooled = _pooled(p, jnp.concatenate([prev, p], axis=0), t * tm)
        dm = dm_ref[...]
        dme = jnp.concatenate([dm, jnp.where(t < nt - 1, dmn_ref[...], 0.0)], axis=0) * sc_ref[...]
        tok = t * tm + lax.broadcasted_iota(jnp.int32, (n_ext, 1), 0)
        dsc, dps, dwps = [], [], []
        for g, w in enumerate(POOL_WINDOWS):
            cs = slice(g * HEAD_DIM, (g + 1) * HEAD_DIM)
            wpb = wp_ref[g].astype(BF16)
            pb = pooled[g].astype(BF16)
            dsc.append(jnp.sum(dm[:, cs] * _dot(pb, wpb), axis=0, keepdims=True))
            dye = dme[:, cs].astype(BF16)
            dwps.append(_dot(pb, dye[:tm], TN))
            dpe = _dot(dye, wpb, NT)
            acc = dpe / jnp.minimum(tok + 1, w).astype(F32)
            sh = 1
            while sh < w:
                acc = acc + pltpu.roll(acc, n_ext - sh, axis=0)
                sh *= 2
            dps.append(acc[:tm] - dpe[:tm])
        dp_ref[...] = jnp.concatenate(dps, axis=1)
        dsc_row = jnp.concatenate(dsc, axis=1)

        @pl.when(t == 0)
        def _():
            for g in range(ng):
                dwp_ref[g] = dwps[g]

        @pl.when(t > 0)
        def _():
            for g in range(ng):
                dwp_ref[g] = dwp_ref[g] + dwps[g]

        _acc_rows(st_ref, t, [dsc_row])

    x_specs, x_shapes, x_scratch = _plan_extras(plan)
    return pl.pallas_call(
        _fuse_exchange(body, 6, 3, 0, plan, 1), name="pool_bwd", grid=(nt,),
        in_specs=[pl.BlockSpec((tm, pw), lambda t: (t, 4)),
                  pl.BlockSpec((POOL_HALO, pw), lambda t: (jnp.maximum(t * nb - 1, 0), 4)),
                  pl.BlockSpec((tm, pw), lambda t: (t, 1)),
                  pl.BlockSpec((POOL_HALO, pw), lambda t: (jnp.minimum((t + 1) * nb, s // POOL_HALO - 1), 1)),
                  _const(wp.shape), _const((1, pw))] + x_specs,
        out_specs=[pl.BlockSpec((tm, pw), lambda t: (t, 0)), _const_out(wp.shape), _const_out((8, pw))] + x_specs,
        out_shape=[jax.ShapeDtypeStruct((s, pw), F32), jax.ShapeDtypeStruct(wp.shape, F32),
                   jax.ShapeDtypeStruct((8, pw), F32)] + x_shapes,
        scratch_shapes=x_scratch,
        compiler_params=_params(("arbitrary",)),
    )(z, z, dmix, dmix, wp, scale, *plan_args)


def _hgrn_bwd(z, o, dmix, st, l0, l1, gn, tc, unroll=1, plan=None, plan_args=()):
    s = z.shape[0]
    nsub = tc // SUB
    nt = s // tc
    hd = HEAD_DIM
    hp = HEADS_PER_STEP

    def body(q_ref, f_ref, v_ref, g_ref, l0_ref, l1_ref, gn_ref, o_ref, dm_ref, st_ref,
             tril_ref, triu_ref, trilc_ref, triuc_ref,
             dq_ref, df_ref, di_ref, dg_ref, stat_ref, dstate, qs, ks, bs, dos, dqs, dks, dbs, sts):
        t = pl.program_id(1)

        @pl.when(t == 0)
        def _():
            dstate[...] = jnp.zeros_like(dstate)

        cols = [slice(hh * hd, (hh + 1) * hd) for hh in range(hp)]
        heads = []
        for hh, cs in enumerate(cols):
            lb = _lower_bound(l0_ref[hh], l1_ref[hh])
            qp = q_ref[:, cs]
            q, sq, f, sf = _hgrn_gates(qp, f_ref[:, cs], lb)
            lf = jnp.log(f) * LOG2E
            o = o_ref[:, cs]
            r = lax.rsqrt(jnp.mean(o * o, axis=-1, keepdims=True) + EPS)
            n = o * r
            gnv = gn_ref[hh]
            gp = g_ref[:, cs]
            sg = _sigmoid(gp)
            dm = dm_ref[:, cs]
            dg_ref[:, cs] = dm * (n * gnv) * (sg * (1.0 + gp * (1.0 - sg)))
            don = dm * (gp * sg)
            dn = don * gnv
            heads.append(dict(lb=lb, qp=qp, q=q, sq=sq, f=f, sf=sf, k=1.0 - f, lf=lf,
                              bc=_group_cumsum(trilc_ref[...], lf), dgn=jnp.sum(don * n, axis=0, keepdims=True),
                              do=r * (dn - n * jnp.mean(dn * n, axis=-1, keepdims=True))))
        bounded = functools.reduce(jnp.minimum, [jnp.min(h["bc"]) for h in heads]) >= -MAX_LOG2_GROWTH

        def factored(hh, cs, q, k, bc, do_all):
            eb = jnp.exp2(bc)
            eib = jnp.exp2(-bc)
            qt = (q * eb).astype(BF16)
            ki = (k * eib).astype(BF16)
            vb = v_ref[:, cs].astype(BF16)
            dob = do_all.astype(BF16)
            mask = trilc_ref[...] > 0
            a = jnp.where(mask, _dot(qt, ki, NT), 0.0).astype(BF16)
            da = jnp.where(mask, _dot(dob, vb, NT), 0.0).astype(BF16)
            dq_in = _dot(da, ki)
            dk_in = _dot(da, qt, TN)
            dv_in = _dot(a, dob, TN)
            last_row = lax.broadcasted_iota(jnp.int32, (CHUNK, 1), 0) == CHUNK - 1
            for c in reversed(range(tc // CHUNK)):
                rs = slice(c * CHUNK, (c + 1) * CHUNK)
                stp = st_ref[hh, c]
                dst = dstate[hh]
                dstb = dst.astype(BF16)
                bl = bc[(c + 1) * CHUNK - 1:(c + 1) * CHUNK, :]
                ekl = jnp.exp2(bl - bc[rs])
                ebl = jnp.exp2(bl)
                kt = k[rs] * ekl
                dq_st = _dot(dob[rs], stp.astype(BF16)) * eb[rs]
                dkt = _dot(vb[rs], dstb)
                extra = jnp.sum(kt * dkt, axis=0, keepdims=True) + ebl * jnp.sum(stp * dst, axis=0, keepdims=True)
                dqs[rs, cs] = dq_st + dq_in[rs] * eb[rs]
                dks[rs, cs] = dkt * ekl + dk_in[rs] * eib[rs]
                di_ref[rs, cs] = _dot(kt.astype(BF16), dstb, NT) + dv_in[rs]
                dbs[rs, cs] = (q[rs] * dq_st - kt * dkt + jnp.where(last_row, extra, 0.0)
                               + (qt[rs].astype(F32) * dq_in[rs] - ki[rs].astype(F32) * dk_in[rs]))
                dstate[hh] = dst * ebl + _dot(dob[rs], qt[rs], TN)
            dbs[:, cs] = _group_cumsum(triuc_ref[...], dbs[:, cs])

        def exact(hh, cs, q, k, lf, do_all):
            qs[:, cs] = q
            ks[:, cs] = k
            bs[:, cs] = _group_cumsum(tril_ref[...], lf)
            dos[:, cs] = do_all
            per = CHUNK // SUB

            def restore(i, carry):
                @pl.when(i % per == 0)
                def _():
                    sts[i] = st_ref[hh, i // per]

                @pl.when(i % per != 0)
                def _():
                    rp = pl.multiple_of((i - 1) * SUB, SUB)
                    b_ = bs[pl.ds(rp, SUB), cs]
                    bl = b_[SUB - 1:SUB, :]
                    kt = (ks[pl.ds(rp, SUB), cs] * jnp.exp2(bl - b_)).astype(BF16)
                    sts[i] = sts[i - 1] * jnp.exp2(bl) + _dot(v_ref[pl.ds(rp, SUB), cs].astype(BF16), kt, TN)

                return carry

            lax.fori_loop(0, nsub, restore, 0)
            rows = lax.broadcasted_iota(jnp.int32, (HALF, 1), 0)
            last_row = lax.broadcasted_iota(jnp.int32, (SUB, 1), 0) == SUB - 1

            def step(i, carry):
                ii = nsub - 1 - i
                r0 = pl.multiple_of(ii * SUB, SUB)
                q_ = qs[pl.ds(r0, SUB), cs]
                k_ = ks[pl.ds(r0, SUB), cs]
                b_ = bs[pl.ds(r0, SUB), cs]
                v_ = v_ref[pl.ds(r0, SUB), cs]
                do_ = dos[pl.ds(r0, SUB), cs]
                stp = sts[ii]
                dst = dstate[hh]
                bl = b_[SUB - 1:SUB, :]
                eb = jnp.exp2(b_)
                ekl = jnp.exp2(bl - b_)
                ebl = jnp.exp2(bl)
                dob = do_.astype(BF16)
                dstb = dst.astype(BF16)
                kt = k_ * ekl
                dq = _dot(dob, stp.astype(BF16)) * eb
                dkt = _dot(v_.astype(BF16), dstb)
                dk = dkt * ekl
                dv = _dot(kt.astype(BF16), dstb, NT)
                extra = jnp.sum(kt * dkt, axis=0, keepdims=True) + ebl * jnp.sum(stp * dst, axis=0, keepdims=True)
                halves = lambda x: [x[:HALF], x[HALF:]]
                q_h, b_h, do_h, dq_h, dk_h, dv_h = (halves(x) for x in (q_, b_, do_, dq, dk, dv))
                for own in range(2):
                    dk_rows, dv_rows = _RowSums(rows), _RowSums(rows)
                    for jj in _RowSums.ORDER:
                        j = own * HALF + jj
                        bj, kj, vj = b_[j:j + 1, :], k_[j:j + 1, :], v_[j:j + 1, :]
                        dk_sum = dv_sum = None
                        for h in range(own, 2):
                            e = _decay(b_h[h], bj, rows, jj if h == own else None)
                            pe = q_h[h] * e
                            acol = jnp.sum(pe * kj, axis=-1, keepdims=True)
                            dacol = jnp.sum(do_h[h] * vj, axis=-1, keepdims=True)
                            dq_h[h] = dq_h[h] + dacol * (e * kj)
                            dk_sum = dacol * pe if dk_sum is None else dk_sum + dacol * pe
                            dv_sum = acol * do_h[h] if dv_sum is None else dv_sum + acol * do_h[h]
                        dk_rows.push(jj, dk_sum)
                        dv_rows.push(jj, dv_sum)
                    dk_h[own] = dk_h[own] + dk_rows.result()
                    dv_h[own] = dv_h[own] + dv_rows.result()
                dq, dk, dv = (jnp.concatenate(x, axis=0) for x in (dq_h, dk_h, dv_h))
                dqs[pl.ds(r0, SUB), cs] = dq
                dks[pl.ds(r0, SUB), cs] = dk
                di_ref[pl.ds(r0, SUB), cs] = dv
                dbs[pl.ds(r0, SUB), cs] = q_ * dq - k_ * dk + jnp.where(last_row, extra, 0.0)
                dstate[hh] = dst * ebl + _dot(dob, (q_ * eb).astype(BF16), TN)
                return carry

            lax.fori_loop(0, nsub, step, 0, unroll=unroll)
            dbs[:, cs] = _group_cumsum(triu_ref[...], dbs[:, cs])

        @pl.when(bounded)
        def _():
            for hh, cs in enumerate(cols):
                factored(hh, cs, heads[hh]["q"], heads[hh]["k"], heads[hh]["bc"], heads[hh]["do"])

        @pl.when(jnp.logical_not(bounded))
        def _():
            for hh, cs in enumerate(cols):
                exact(hh, cs, heads[hh]["q"], heads[hh]["k"], heads[hh]["lf"], heads[hh]["do"])

        for hh, cs in enumerate(cols):
            h = heads[hh]
            dfv = dbs[:, cs] / h["f"] - dks[:, cs]
            df_ref[:, cs] = dfv * (1.0 - h["lb"]) * h["sf"] * (1.0 - h["sf"])
            dlb = jnp.sum(dfv * (1.0 - h["sf"]), axis=0, keepdims=True)
            dq_ref[:, cs] = dqs[:, cs] * (h["sq"] * (1.0 + h["qp"] * (1.0 - h["sq"])))
            _acc_rows(stat_ref.at[hh], t, [h["dgn"], dlb])

    rev = lambda t: nt - 1 - t
    ng = HGRN_HEADS // hp
    col = lambda k: pl.BlockSpec((tc, hp * hd), lambda h, t: (rev(t), k * ng + h))
    vec = pl.BlockSpec((hp, 1, hd), lambda h, t: (h, 0, 0))
    head = pl.BlockSpec((tc, hp * hd), lambda h, t: (rev(t), h))
    x_specs, x_shapes, x_scratch = _plan_extras(plan)
    return pl.pallas_call(
        _fuse_exchange(body, 14, 5, 9, plan, 2), name="hgrn_bwd", grid=(ng, nt),
        in_specs=[col(0), col(1), col(2), col(3), vec, vec, vec, head, head,
                  pl.BlockSpec((hp, tc // CHUNK, hd, hd), lambda h, t: (h, rev(t), 0, 0))]
                 + [_const((tc, tc))] * 4 + x_specs,
        out_specs=[head, head, head, head, pl.BlockSpec((hp, 8, hd), lambda h, t: (h, 0, 0))] + x_specs,
        out_shape=[jax.ShapeDtypeStruct((s, HGRN_WIDTH), F32)] * 4 + [jax.ShapeDtypeStruct((HGRN_HEADS, 8, hd), F32)]
                  + x_shapes,
        scratch_shapes=[pltpu.VMEM((hp, hd, hd), F32)] + [pltpu.VMEM((tc, hp * hd), F32)] * 7
                       + [pltpu.VMEM((nsub, hd, hd), F32)] + x_scratch,
        compiler_params=_params(("arbitrary", "arbitrary")),
    )(z, z, z, z, l0, l1, gn, o, dmix, st, _block_tri(tc, SUB, False), _block_tri(tc, SUB, True),
      _block_tri(tc, CHUNK, False), _block_tri(tc, CHUNK, True), *plan_args)


def _in_bwd(dparts, dx1, x, g, win_g, tm, plan=None, plan_args=()):
    s, d = x.shape
    nsh, _, wc = win_g.shape
    pw = dparts[0].shape[1]

    def body(dq_ref, df_ref, di_ref, dg_ref, dp_ref, dx1_ref, x_ref, g_ref, w_ref, gx_ref, dz_ref, st_ref):
        dz = jnp.concatenate([dq_ref[...], df_ref[...], di_ref[...], dg_ref[...], dp_ref[...]], axis=1).astype(BF16)
        dz_ref[...] = dz
        dh = jnp.zeros((tm, d), F32)
        for j in range(nsh):
            dh = dh + _dot(dz[:, j * wc:(j + 1) * wc], w_ref[j], NT)
        gv = g_ref[...]
        _, n, r = _rms_fwd(x_ref[...], gv)
        dxn, dg = _rms_bwd(dh, n, r, gv)
        gx_ref[...] = dx1_ref[...] + dxn
        _acc_rows(st_ref, pl.program_id(0), [dg])

    tile = lambda w: pl.BlockSpec((tm, w), lambda t: (t, 0))
    x_specs, x_shapes, x_scratch = _plan_extras(plan)
    return pl.pallas_call(
        _fuse_exchange(body, 9, 3, 0, plan, 1), name="in_bwd", grid=(s // tm,),
        in_specs=[tile(pw)] * 5 + [tile(d), tile(d), _const((1, d)), _const(win_g.shape)] + x_specs,
        out_specs=[tile(d), tile(nsh * wc), _const_out((8, d))] + x_specs,
        out_shape=[jax.ShapeDtypeStruct((s, d), F32), jax.ShapeDtypeStruct((s, nsh * wc), BF16),
                   jax.ShapeDtypeStruct((8, d), F32)] + x_shapes,
        scratch_shapes=x_scratch,
        compiler_params=_params(("arbitrary",)),
    )(*dparts, dx1, x, g, win_g, *plan_args)


def _tn_grad(name, a, b, out_rows, out_cols, a_sharded, plan=None, plan_args=()):
    s = a.shape[0]
    tr, tc = min(out_rows, GRAD_TILE), min(out_cols, GRAD_TILE)
    nr, nc = out_rows // tr, out_cols // tc

    def body(a_ref, b_ref, o_ref):
        o_ref[...] = _dot(a_ref[...], b_ref[...], TN)

    a_map = (lambda j, i, k: (0, j * nr + i)) if a_sharded else (lambda j, i, k: (0, i))
    b_map = (lambda j, i, k: (0, k)) if a_sharded else (lambda j, i, k: (0, j * nc + k))
    x_specs, x_shapes, x_scratch = _plan_extras(plan)
    res = pl.pallas_call(
        _fuse_exchange(body, 2, 1, 0, plan, 3), name=name, grid=(N_CHIPS, nr, nc),
        in_specs=[pl.BlockSpec((s, tr), a_map), pl.BlockSpec((s, tc), b_map)] + x_specs,
        out_specs=[pl.BlockSpec((None, tr, tc), lambda j, i, k: (j, i, k))] + x_specs,
        out_shape=[jax.ShapeDtypeStruct((N_CHIPS, out_rows, out_cols), F32)] + x_shapes,
        scratch_shapes=x_scratch,
        compiler_params=_params(("arbitrary", "arbitrary", "arbitrary")),
    )(a, b, *plan_args)
    return res if plan else res[0]


FFN_NAMES = ("w_ff1", "w_ff2")
ATTN_NAMES = ("w_xo", "w_xq", "w_out", "w_xk", "w_xv")
EARLY_NAMES = FFN_NAMES + ATTN_NAMES
BIG_NAMES = EARLY_NAMES + ("w_in",)


def _halved(g):
    return g.reshape(N_CHIPS, 2, g.shape[1] // 2, g.shape[2])


def _pair_adds(names, gs, got, idx):
    pairs = [_grad_pair_add("grad_pair_add_" + k, g, r, idx, tr=min(256, g.shape[2])) for k, g, r in zip(names, gs, got)]
    return [p[0] for p in pairs], [p[1] for p in pairs]


def _step(x, mem, target, small, shards, idx):
    d = x.shape[1]
    l0 = small["lb_logits"][0].reshape(HGRN_HEADS, 1, HEAD_DIM)
    l1 = small["lb_logits"][1].reshape(HGRN_HEADS, 1, HEAD_DIM)
    gn = small["hgrn_norm_g"].reshape(HGRN_HEADS, 1, HEAD_DIM)
    wp = small["w_pool"].reshape(len(POOL_WINDOWS), HEAD_DIM, HEAD_DIM)
    psc = small["pool_scale"].reshape(1, -1)
    gmix, gx, gmem, gffn = (small[k].reshape(1, d) for k in ("norm_mix_g", "norm_x_g", "norm_mem_g", "norm_ffn_g"))
    gfin = small["final_norm_g"].reshape(1, d)

    (win_g,) = _run_exchange("gather_w_in", _WeightGather([shards["w_in"]]), [shards["w_in"]])
    z, h, kv_g = _in_proj(x, gmix, win_g, tm=TOKEN_TILE, plan=_WeightGather([shards["slab_kv"]]),
                          plan_args=[shards["slab_kv"]])
    mid_w = [shards["slab_oq"], shards["w_xo"], shards["w_ff1"]]
    o, oa, st, oq_g, wo_g, w1_g = _hgrn_fwd(z, l0, l1, gn, tc=HGRN_BLOCK, unroll=8,
                                            plan=_WeightGather(mid_w), plan_args=mid_w)
    ob = _pool_fwd(z, wp, psc, tm=TOKEN_TILE)
    xk, xv = _kv_proj(mem, gmem, kv_g)
    late_w = [shards["w_ff2"]]
    x1, mixed, hq, xq, att, x2, w2_g = _mix_xattn_fwd(x, oa, ob, gx, oq_g, wo_g, xk, xv, tm=TOKEN_TILE,
                                                      plan=_WeightGather(late_w), plan_args=late_w)
    a, hf, dx3, dx3b, st_loss = _mlp_loss_fwd(x2, gffn, gfin, w1_g, w2_g, target, tm=TOKEN_TILE)

    da, u, dx2, dx2b, st_ffn = _mlp_bwd(dx3, dx3b, a, x2, gffn, w1_g, w2_g, tm=MLP_BWD_TOKEN_TILE)
    g_ff1 = [_halved(_tn_grad("dw_ff1", hf, da, d, d, False))]
    dw_ff2, *got = _tn_grad("dw_ff2", u, dx3b, d, d, True, plan=_PairExchange(g_ff1), plan_args=g_ff1)
    keep_ff1, send_ff1 = _pair_adds(("w_ff1",), g_ff1, got, idx)
    g_ff2 = [_halved(dw_ff2)]
    dx1, dx1b, dxq, dmix, dxk, dxv, st_x, *got = _xattn_mix_bwd(
        dx2, x1, xq, xk, xv, gx, oq_g, wo_g, tm=TOKEN_TILE,
        plan=_Plans([_ChipExchange(send_ff1), _PairExchange(g_ff2)]), plan_args=send_ff1 + g_ff2)
    recv_ff1 = got[:1]
    keep_ff2, send_ff2 = _pair_adds(("w_ff2",), g_ff2, got[1:], idx)
    dw = {}
    dw["w_xo"] = _tn_grad("dw_xo", att, dx2b, d, d // N_CHIPS, False)
    dw["w_xq"] = _tn_grad("dw_xq", hq, dxq, d // N_CHIPS, d, True)
    dw["w_out"] = _tn_grad("dw_out", mixed, dx1b, d // N_CHIPS, d, True)
    dw["w_xk"], dw["w_xv"], st_mem = _kv_bwd(mem, gmem, dxk, dxv, kv_g)
    gs_attn = [_halved(dw[k]) for k in ATTN_NAMES]
    dp, d_wp, st_pool, *got_attn = _pool_bwd(z, dmix, wp, psc, tm=TOKEN_TILE, plan=_PairExchange(gs_attn), plan_args=gs_attn)
    keep_attn, send_attn = _pair_adds(ATTN_NAMES, gs_attn, got_attn, idx)
    sends = send_ff2 + send_attn
    dq, df, di, dg, st_hgrn, *received = _hgrn_bwd(z, o, dmix, st, l0, l1, gn, tc=HGRN_BLOCK, unroll=4,
                                                    plan=_ChipExchange(sends), plan_args=sends)
    grad_x, dz, st_mix = _in_bwd([dq, df, di, dg, dp], dx1, x, gmix, win_g, tm=TOKEN_TILE)
    keeps = keep_ff1 + keep_ff2 + keep_attn
    received = recv_ff1 + list(received)
    gs_in = [_halved(_tn_grad("dw_in", h, dz, d, win_g.shape[2], False))]
    got_in = _run_exchange("grad_pair_exchange_w_in", _PairExchange(gs_in), gs_in)
    keep_in, send_in = _pair_adds(("w_in",), gs_in, got_in, idx)

    partials = dict(zip(EARLY_NAMES, zip(keeps, received)))
    stats = dict(mix=st_mix, x=st_x, mem=st_mem, ffn=st_ffn, loss=st_loss, hgrn=st_hgrn, pool=st_pool)
    return grad_x, stats, d_wp, partials, keep_in, send_in


def _place():
    x, y, c = lax.axis_index("x"), lax.axis_index("y"), lax.axis_index("c")
    return x, y, c, [(x, 1 - y), (1 - x, y), (1 - x, 1 - y)]


def _rcopy(src, dst, ssem, rsem, dev):
    return pltpu.make_async_remote_copy(src_ref=src, dst_ref=dst, send_sem=ssem, recv_sem=rsem,
                                        device_id=dev, device_id_type=MESH)


class _WeightGather:
    def __init__(self, shards):
        self.n = len(shards)
        self.rows = [w.shape[0] for w in shards]
        self.out_shape = [jax.ShapeDtypeStruct((N_CHIPS,) + w.shape, w.dtype) for w in shards]
        dma = pltpu.SemaphoreType.DMA
        self.scratch_shapes = [dma((self.n,))] * 2 + [dma((self.n, 2))] * 4 + [dma((self.n, 3))] * 2

    def _copies(self, ins, outs, sems, phase):
        osem, orsem, ssem, rsem, qsem, qrsem, fsem, frsem = sems
        x, y, c, _ = _place()
        chip = 2 * x + y
        sib = (x, y, 1 - c)
        nbrs = [(x, 1 - y), (1 - x, y)]
        diag = 2 * (1 - x) + (1 - y)
        cps = []
        if phase in ("start", "finish"):
            cps += [("own", _rcopy(ins[a], outs[a].at[chip], osem.at[a], orsem.at[a], sib)) for a in range(self.n)]
        for a in range(self.n):
            hr = self.rows[a] // 2
            qr = hr // 2
            half = lambda who, hc, a=a, hr=hr: outs[a].at[who, pl.ds(hc * hr, hr), :]
            quarter = lambda who, k, a=a, hr=hr, qr=qr: outs[a].at[who, pl.ds(c * hr + k * qr, qr), :]
            for r, (px, py) in enumerate(nbrs):
                pc = 2 * px + py
                ox, oy = nbrs[1 - r]
                if phase in ("start", "finish"):
                    cps.append(("direct", _rcopy(ins[a].at[pl.ds(c * hr, hr), :], half(chip, c), ssem.at[a, r],
                                                 rsem.at[a, r], (px, py, c))))
                if phase == "middle":
                    cps.append(("arrived", _rcopy(half(pc, c), half(pc, c), ssem.at[a, r], rsem.at[a, r], (px, py, c))))
                if phase in ("middle", "finish"):
                    cps.append(("relay", _rcopy(quarter(pc, r), quarter(pc, r), qsem.at[a, r], qrsem.at[a, r],
                                                (ox, oy, c))))
                    cps.append(("pass", _rcopy(half(pc, c), half(pc, c), fsem.at[a, r], frsem.at[a, r], sib)))
                if phase == "finish":
                    cps.append(("relayed", _rcopy(quarter(diag, r), quarter(diag, r), qsem.at[a, r], qrsem.at[a, r],
                                                  (ox, oy, c))))
                    cps.append(("pass_in", _rcopy(half(pc, 1 - c), half(pc, 1 - c), fsem.at[a, r], frsem.at[a, r], sib)))
            if phase == "finish":
                cps.append(("pass_diag", _rcopy(half(diag, c), half(diag, c), fsem.at[a, 2], frsem.at[a, 2], sib)))
                cps.append(("pass_in", _rcopy(half(diag, 1 - c), half(diag, 1 - c), fsem.at[a, 2], frsem.at[a, 2], sib)))
        return cps

    def start(self, ins, outs, sems):
        for _, cp in self._copies(ins, outs, sems, "start"):
            cp.start()

    def middle(self, ins, outs, sems):
        for kind, cp in self._copies(ins, outs, sems, "middle"):
            if kind == "arrived":
                cp.wait_recv()
            else:
                cp.start()

    def finish(self, ins, outs, sems):
        cps = self._copies(ins, outs, sems, "finish")
        for kind, cp in cps:
            if kind == "relayed":
                cp.wait_recv()
        for kind, cp in cps:
            if kind == "pass_diag":
                cp.start()
        for kind, cp in cps:
            if kind == "pass_in":
                cp.wait_recv()
        for kind, cp in cps:
            if kind in ("direct", "relay", "pass", "pass_diag"):
                cp.wait_send()
        for kind, cp in cps:
            if kind == "own":
                cp.wait()


class _ChipExchange:
    def __init__(self, sends):
        self.n = len(sends)
        self.out_shape = [jax.ShapeDtypeStruct(g.shape, g.dtype) for g in sends]
        self.scratch_shapes = [pltpu.SemaphoreType.DMA((self.n, 3))] * 2

    def _copies(self, ins, outs, sems):
        ssem, rsem = sems
        _, _, c, peers = _place()
        return [_rcopy(ins[a].at[r], outs[a].at[r], ssem.at[a, r], rsem.at[a, r], (px, py, c))
                for a in range(self.n) for r, (px, py) in enumerate(peers)]

    def start(self, ins, outs, sems):
        for cp in self._copies(ins, outs, sems):
            cp.start()

    def finish(self, ins, outs, sems):
        for cp in self._copies(ins, outs, sems):
            cp.wait()


class _Plans:
    def __init__(self, plans):
        self.plans = plans
        self.n = sum(p.n for p in plans)
        self.out_shape = [s for p in plans for s in p.out_shape]
        self.scratch_shapes = [s for p in plans for s in p.scratch_shapes]

    def _each(self, ins, outs, sems):
        a = b = 0
        for p in self.plans:
            ns = len(p.scratch_shapes)
            yield p, ins[a:a + p.n], outs[a:a + p.n], sems[b:b + ns]
            a, b = a + p.n, b + ns

    def start(self, ins, outs, sems):
        for p, i, o, s in self._each(ins, outs, sems):
            p.start(i, o, s)

    def finish(self, ins, outs, sems):
        for p, i, o, s in self._each(ins, outs, sems):
            p.finish(i, o, s)


def _run_exchange(name, plan, arrays):
    n = plan.n

    def body(*refs):
        ins, outs, sems = refs[:n], refs[n:2 * n], refs[2 * n:]
        plan.start(ins, outs, sems)
        if hasattr(plan, "middle"):
            plan.middle(ins, outs, sems)
        plan.finish(ins, outs, sems)

    return pl.pallas_call(
        body, name=name, in_specs=[ANY] * n, out_specs=[ANY] * n,
        out_shape=plan.out_shape, scratch_shapes=plan.scratch_shapes,
    )(*arrays)


class _PairExchange:
    def __init__(self, gs):
        self.n = len(gs)
        self.out_shape = [jax.ShapeDtypeStruct((g.shape[0],) + g.shape[2:], g.dtype) for g in gs]
        self.scratch_shapes = [pltpu.SemaphoreType.DMA((self.n,))] * 2

    def _copies(self, ins, outs, sems):
        ssem, rsem = sems
        x, y, c, _ = _place()
        return [_rcopy(ins[a].at[:, 1 - c], outs[a], ssem.at[a], rsem.at[a], (x, y, 1 - c)) for a in range(self.n)]

    def start(self, ins, outs, sems):
        for cp in self._copies(ins, outs, sems):
            cp.start()

    def finish(self, ins, outs, sems):
        for cp in self._copies(ins, outs, sems):
            cp.wait()


def _grad_pair_add(name, g, got, idx, tr):
    _, _, hr, cc = g.shape

    def body(idx_ref, g0, g1, g2, g3, r0, r1, r2, r3, keep_ref, send_ref):
        keep_ref[...] = g0[...] + r0[...]
        for q, (gq, rq) in enumerate(((g1, r1), (g2, r2), (g3, r3))):
            send_ref[q] = (gq[...] + rq[...]).astype(BF16)

    gspec = lambda q: pl.BlockSpec((None, None, tr, cc), lambda i, idx: (idx[1 + q], idx[0], i, 0))
    rspec = lambda q: pl.BlockSpec((None, tr, cc), lambda i, idx: (idx[1 + q], i, 0))
    return pl.pallas_call(
        body, name=name,
        grid_spec=pltpu.PrefetchScalarGridSpec(
            num_scalar_prefetch=1, grid=(hr // tr,),
            in_specs=[gspec(q) for q in range(4)] + [rspec(q) for q in range(4)],
            out_specs=[pl.BlockSpec((tr, cc), lambda i, idx: (i, 0)), pl.BlockSpec((3, tr, cc), lambda i, idx: (0, i, 0))]),
        out_shape=[jax.ShapeDtypeStruct((hr, cc), F32), jax.ShapeDtypeStruct((3, hr, cc), BF16)],
        compiler_params=_params(("parallel",)),
    )(idx, g, g, g, g, got, got, got, got)


def _grad_chip_add(name, keep, got, tr):
    hr, cc = keep.shape

    def body(k_ref, g_ref, o_ref):
        o_ref[...] = ((k_ref[...] + g_ref[0].astype(F32)) + g_ref[1].astype(F32)) + g_ref[2].astype(F32)

    return pl.pallas_call(
        body, name=name, grid=(hr // tr,),
        in_specs=[pl.BlockSpec((tr, cc), lambda i: (i, 0)), pl.BlockSpec((3, tr, cc), lambda i: (0, i, 0))],
        out_specs=pl.BlockSpec((tr, cc), lambda i: (i, 0)),
        out_shape=jax.ShapeDtypeStruct((hr, cc), F32),
        compiler_params=_params(("parallel",)),
    )(keep, got)


class _HalfExchange:
    def __init__(self, ts):
        self.n = len(ts)
        self.out_shape = [jax.ShapeDtypeStruct(t.shape, t.dtype) for t in ts]
        self.scratch_shapes = [pltpu.SemaphoreType.DMA((self.n,))] * 2

    def _copies(self, ins, outs, sems):
        ssem, rsem = sems
        x, y, c, _ = _place()
        return [_rcopy(ins[a], outs[a], ssem.at[a], rsem.at[a], (x, y, 1 - c)) for a in range(self.n)]

    def start(self, ins, outs, sems):
        for cp in self._copies(ins, outs, sems):
            cp.start()

    def finish(self, ins, outs, sems):
        for cp in self._copies(ins, outs, sems):
            cp.wait()


def _small_allreduce(stats, d_wp, plan, plan_args):
    d = D_MODEL
    half = d // 2
    wps = d_wp.shape
    n = plan.n

    def body(mix_ref, x_ref, mem_ref, ffn_ref, loss_ref, hg_ref, pool_ref, wp_ref, *refs):
        cin, (slab_out, wp_out), cout = refs[:n], refs[n:n + 2], refs[n + 2:2 * n + 2]
        slab_buf, wp_buf, sib_s, sib_w, ssem, rsem = refs[2 * n + 2:2 * n + 8]
        csem = refs[2 * n + 8:]
        plan.start(cin, cout, csem)
        x, y, c, peers = _place()
        chip = 2 * x + y
        sib = (x, y, 1 - c)
        hgn = jnp.concatenate([hg_ref[h, 0:1, :] for h in range(HGRN_HEADS)], axis=1)
        dlb = jnp.concatenate([hg_ref[h, 1:2, :] for h in range(HGRN_HEADS)], axis=1)
        slab_buf[0] = jnp.concatenate([
            mix_ref[0:1, :], x_ref[0:1, :], mem_ref[0:1, :], ffn_ref[0:1, :], loss_ref[0:1, :],
            jnp.concatenate([dlb, hgn], axis=1),
            jnp.concatenate([pool_ref[0:1, :], jnp.zeros((1, half), F32)], axis=1),
            loss_ref[1:2, :]], axis=0)
        wp_buf[0] = wp_ref[...]
        pair = [_rcopy(slab_buf.at[0], sib_s, ssem.at[0], rsem.at[0], sib),
                _rcopy(wp_buf.at[0], sib_w, ssem.at[1], rsem.at[1], sib)]
        for cp in pair:
            cp.start()
        for cp in pair:
            cp.wait()
        slab_buf[0] = slab_buf[0] + sib_s[...]
        wp_buf[0] = wp_buf[0] + sib_w[...]
        cps = []
        for r, (px, py) in enumerate(peers):
            cps.append(_rcopy(slab_buf.at[0], slab_buf.at[r + 1], ssem.at[2 + 2 * r], rsem.at[2 + 2 * r], (px, py, c)))
            cps.append(_rcopy(wp_buf.at[0], wp_buf.at[r + 1], ssem.at[3 + 2 * r], rsem.at[3 + 2 * r], (px, py, c)))
        for cp in cps:
            cp.start()
        for cp in cps:
            cp.wait()
        tot_s, tot_w = slab_buf[chip], wp_buf[chip]
        for j in range(1, N_CHIPS):
            tot_s = tot_s + slab_buf[jnp.bitwise_xor(j, chip)]
            tot_w = tot_w + wp_buf[jnp.bitwise_xor(j, chip)]
        slab_out[...] = tot_s
        wp_out[...] = tot_w
        plan.finish(cin, cout, csem)

    return pl.pallas_call(
        body, name="small_allreduce",
        in_specs=[VMEM] * 8 + [ANY] * n, out_specs=[VMEM] * 2 + [ANY] * n,
        out_shape=[jax.ShapeDtypeStruct((8, d), F32), jax.ShapeDtypeStruct(wps, F32)] + list(plan.out_shape),
        scratch_shapes=[pltpu.VMEM((N_CHIPS, 8, d), F32), pltpu.VMEM((N_CHIPS,) + wps, F32),
                        pltpu.VMEM((8, d), F32), pltpu.VMEM(wps, F32),
                        pltpu.SemaphoreType.DMA((8,)), pltpu.SemaphoreType.DMA((8,))] + list(plan.scratch_shapes),
    )(stats["mix"], stats["x"], stats["mem"], stats["ffn"], stats["loss"], stats["hgrn"], stats["pool"], d_wp,
      *plan_args)


def _adamw_math(w, g, m, v):
    m = ADAM_B1 * m + (1.0 - ADAM_B1) * g
    v = ADAM_B2 * v + (1.0 - ADAM_B2) * (g * g)
    m_hat = m / (1.0 - ADAM_B1 ** ADAM_STEP)
    v_hat = v / (1.0 - ADAM_B2 ** ADAM_STEP)
    delta = -ADAM_LR * (m_hat / (jnp.sqrt(v_hat) + ADAM_EPS) + ADAM_WD * w)
    return delta, m, v


def _adamw(name, mine, theirs, w, m, v, idx, tr):
    rows = w.shape[0]
    cc = mine.shape[1]
    nb = rows // 2 // tr
    heads = w.shape[1] if w.ndim == 3 else 1
    e = cc // heads

    def body(idx_ref, a_ref, b_ref, w_ref, m_ref, v_ref, g_out, d_out, m_out, v_out):
        g = jnp.where(pl.program_id(0) // nb == idx_ref[0], a_ref[...], b_ref[...])
        if w.ndim == 2:
            g_out[...] = g
            d_out[...], m_out[...], v_out[...] = _adamw_math(w_ref[...], g, m_ref[...], v_ref[...])
        else:
            for h in range(heads):
                gh = g[:, h * e:(h + 1) * e]
                g_out[:, h, :] = gh
                d_out[:, h, :], m_out[:, h, :], v_out[:, h, :] = _adamw_math(
                    w_ref[:, h, :], gh, m_ref[:, h, :], v_ref[:, h, :])

    hspec = pl.BlockSpec((tr, cc), lambda i, idx: (i % nb, 0))
    spec = pl.BlockSpec((tr,) + w.shape[1:], lambda i, idx: (i,) + (0,) * (w.ndim - 1))
    return pl.pallas_call(
        body, name=name,
        grid_spec=pltpu.PrefetchScalarGridSpec(
            num_scalar_prefetch=1, grid=(rows // tr,),
            in_specs=[hspec, hspec, spec, spec, spec], out_specs=[spec] * 4),
        out_shape=[jax.ShapeDtypeStruct(w.shape, F32)] * 4,
        compiler_params=_params(("parallel",)),
    )(idx, mine, theirs, w, m, v)


SMALL_NAMES = ("norm_mix_g", "lb_logits", "hgrn_norm_g", "w_pool", "pool_scale", "norm_x_g", "norm_mem_g",
               "norm_ffn_g", "final_norm_g")


def _small_update(slab, d_wp, ws, ms, vs):
    n = len(SMALL_NAMES)
    half = D_MODEL // 2

    def body(slab_ref, wp_ref, *refs):
        w_refs, m_refs, v_refs, outs = refs[:n], refs[n:2 * n], refs[2 * n:3 * n], refs[3 * n:]
        row = lambda k: slab_ref[k:k + 1, :]
        lbl = w_refs[SMALL_NAMES.index("lb_logits")][...]
        s0 = _lower_bound(lbl[0:1, :], lbl[1:2, :])
        dl0 = row(ROW_LB_HGN)[:, :half] * s0 * (1.0 - s0)
        grads = dict(norm_mix_g=row(ROW_GMIX), lb_logits=jnp.concatenate([dl0, -dl0], axis=0),
                     hgrn_norm_g=row(ROW_LB_HGN)[:, half:], w_pool=wp_ref[...], pool_scale=row(ROW_PSCALE)[:, :half],
                     norm_x_g=row(ROW_GX), norm_mem_g=row(ROW_GMEM), norm_ffn_g=row(ROW_GFFN),
                     final_norm_g=row(ROW_GFIN))
        outs[0][...] = row(ROW_LOSS)[:, :128]
        for i, name in enumerate(SMALL_NAMES):
            g = grads[name]
            delta, m2, v2 = _adamw_math(w_refs[i][...], g, m_refs[i][...], v_refs[i][...])
            for o, val in zip(outs[1 + 4 * i:5 + 4 * i], (g, delta, m2, v2)):
                o[...] = val

    args = [ws[k] for k in SMALL_NAMES] + [ms[k] for k in SMALL_NAMES] + [vs[k] for k in SMALL_NAMES]
    out_shape = [jax.ShapeDtypeStruct((1, 128), F32)]
    for k in SMALL_NAMES:
        out_shape += [jax.ShapeDtypeStruct(ws[k].shape, F32)] * 4
    res = pl.pallas_call(
        body, name="small_update",
        in_specs=[VMEM] * (2 + 3 * n), out_specs=[VMEM] * len(out_shape), out_shape=out_shape,
    )(slab, d_wp, *args)
    return res[0], {k: res[1 + 4 * i:5 + 4 * i] for i, k in enumerate(SMALL_NAMES)}


ALL_NAMES = ("norm_mix_g", "w_in", "lb_logits", "hgrn_norm_g", "w_pool", "pool_scale", "w_out", "norm_x_g",
             "norm_mem_g", "w_xq", "w_xk", "w_xv", "w_xo", "norm_ffn_g", "w_ff1", "w_ff2", "final_norm_g")


def _shard_2d(name, a):
    a = a[0]
    if name in ("w_xq", "w_xk", "w_xv"):
        return a.reshape(a.shape[0], -1)
    if name == "w_xo":
        return a.reshape(-1, a.shape[-1])
    return a


def _small_2d(name, a):
    if name == "w_pool":
        return a.reshape(-1, HEAD_DIM)
    if name == "lb_logits":
        return a
    return a.reshape(1, -1)


def kernel(x, mem, norm_mix_g, w_in, lb_logits, hgrn_norm_g, w_pool, pool_scale, w_out, norm_x_g, norm_mem_g, w_xq, w_xk, w_xv, w_xo, norm_ffn_g, w_ff1, w_ff2, final_norm_g, loss_target, m_norm_mix_g, m_w_in, m_lb_logits, m_hgrn_norm_g, m_w_pool, m_pool_scale, m_w_out, m_norm_x_g, m_norm_mem_g, m_w_xq, m_w_xk, m_w_xv, m_w_xo, m_norm_ffn_g, m_w_ff1, m_w_ff2, m_final_norm_g, v_norm_mix_g, v_w_in, v_lb_logits, v_hgrn_norm_g, v_w_pool, v_pool_scale, v_w_out, v_norm_x_g, v_norm_mem_g, v_w_xq, v_w_xk, v_w_xv, v_w_xo, v_norm_ffn_g, v_w_ff1, v_w_ff2, v_final_norm_g):
    w = dict(norm_mix_g=norm_mix_g, w_in=w_in, lb_logits=lb_logits, hgrn_norm_g=hgrn_norm_g, w_pool=w_pool, pool_scale=pool_scale, w_out=w_out, norm_x_g=norm_x_g, norm_mem_g=norm_mem_g, w_xq=w_xq, w_xk=w_xk, w_xv=w_xv, w_xo=w_xo, norm_ffn_g=norm_ffn_g, w_ff1=w_ff1, w_ff2=w_ff2, final_norm_g=final_norm_g)
    m = dict(norm_mix_g=m_norm_mix_g, w_in=m_w_in, lb_logits=m_lb_logits, hgrn_norm_g=m_hgrn_norm_g, w_pool=m_w_pool, pool_scale=m_pool_scale, w_out=m_w_out, norm_x_g=m_norm_x_g, norm_mem_g=m_norm_mem_g, w_xq=m_w_xq, w_xk=m_w_xk, w_xv=m_w_xv, w_xo=m_w_xo, norm_ffn_g=m_norm_ffn_g, w_ff1=m_w_ff1, w_ff2=m_w_ff2, final_norm_g=m_final_norm_g)
    v = dict(norm_mix_g=v_norm_mix_g, w_in=v_w_in, lb_logits=v_lb_logits, hgrn_norm_g=v_hgrn_norm_g, w_pool=v_w_pool, pool_scale=v_pool_scale, w_out=v_w_out, norm_x_g=v_norm_x_g, norm_mem_g=v_norm_mem_g, w_xq=v_w_xq, w_xk=v_w_xk, w_xv=v_w_xv, w_xo=v_w_xo, norm_ffn_g=v_norm_ffn_g, w_ff1=v_w_ff1, w_ff2=v_w_ff2, final_norm_g=v_final_norm_g)

    big_w = {k: _shard_2d(k, w[k]) for k in BIG_NAMES}
    slab_oq = jnp.concatenate([big_w["w_out"], big_w["w_xq"]], axis=0).astype(BF16)
    slab_kv = jnp.concatenate([big_w["w_xk"], big_w["w_xv"]], axis=0).astype(BF16)
    shards = dict(slab_oq=slab_oq, slab_kv=slab_kv,
                  **{k: big_w[k].astype(BF16) for k in ("w_in", "w_xo", "w_ff1", "w_ff2")})

    cx, cy, cc = lax.axis_index("x"), lax.axis_index("y"), lax.axis_index("c")
    chip = 2 * cx + cy
    idx = jnp.stack([cc, chip, chip ^ 1, chip ^ 2, chip ^ 3]).astype(jnp.int32)
    small = {k: w[k] for k in SMALL_NAMES}
    grad_x, stats, d_wp, partials, keep_in, send_in = _step(x[0], mem[0], loss_target[0], small, shards, idx)

    chip_add = lambda k, keep, got: _grad_chip_add("grad_chip_add_" + k, keep, got, tr=min(256, keep.shape[0]))
    halves = {k: chip_add(k, *partials[k]) for k in EARLY_NAMES}
    early = [halves[k] for k in EARLY_NAMES]
    slab_sum, wp_sum, recv_in, *their_early = _small_allreduce(
        stats, d_wp.reshape(-1, HEAD_DIM), _Plans([_ChipExchange(send_in), _HalfExchange(early)]), send_in + early)
    halves["w_in"] = chip_add("w_in", keep_in[0], recv_in)
    theirs = dict(zip(EARLY_NAMES, their_early))
    (theirs["w_in"],) = _run_exchange("grad_half_exchange_w_in", _HalfExchange([halves["w_in"]]), [halves["w_in"]])

    grads, deltas, new_m, new_v = {}, {}, {}, {}
    for k in BIG_NAMES:
        as_held = (lambda a: a[0]) if k in ("w_xq", "w_xk", "w_xv") else functools.partial(_shard_2d, k)
        res = _adamw("adamw_" + k, halves[k], theirs[k], as_held(w[k]), as_held(m[k]), as_held(v[k]), idx,
                     tr=min(256, halves[k].shape[0]))
        for store, val in zip((grads, deltas, new_m, new_v), res):
            store[k] = val.reshape(w[k].shape)

    loss, upd = _small_update(slab_sum, wp_sum, {k: _small_2d(k, w[k]) for k in SMALL_NAMES},
                              {k: _small_2d(k, m[k]) for k in SMALL_NAMES}, {k: _small_2d(k, v[k]) for k in SMALL_NAMES})
    for k in SMALL_NAMES:
        for store, val in zip((grads, deltas, new_m, new_v), upd[k]):
            store[k] = val.reshape(w[k].shape)

    return (loss[0, 0], grad_x[None], *[grads[k] for k in ALL_NAMES], *[deltas[k] for k in ALL_NAMES],
            *[new_m[k] for k in ALL_NAMES], *[new_v[k] for k in ALL_NAMES])
```

```python
import functools

import jax
import jax.numpy as jnp
from jax import lax
from jax.experimental import pallas as pl
from jax.experimental.pallas import tpu as pltpu

F32 = jnp.float32
BF16 = jnp.bfloat16
LOG2E = 1.4426950408889634
NEG_BIG = -1e30
MAX_LOG2_GROWTH = 100.0
MESH = pl.DeviceIdType.MESH
ANY = pl.BlockSpec(memory_space=pl.ANY)
VMEM = pl.BlockSpec(memory_space=pltpu.VMEM)

D_MODEL = 1024
N_CHIPS = 4
HGRN_HEADS = 4
HEAD_DIM = 128
HGRN_WIDTH = HGRN_HEADS * HEAD_DIM
POOL_WINDOWS = (2, 4, 8, 16)
POOL_HALO = 16
SUB = 16
HALF = SUB // 2
CHUNK = 64
HEADS_PER_STEP = 4
XATTN_HEADS = 4
XATTN_HEAD_DIM = 256
EPS = 1e-6
ADAM_LR, ADAM_B1, ADAM_B2, ADAM_EPS, ADAM_WD, ADAM_STEP = 0.001, 0.9, 0.999, 1e-08, 0.01, 10

TOKEN_TILE = 512
MLP_BWD_TOKEN_TILE = 256
HGRN_BLOCK = 256
GRAD_TILE = 1024

V7X_VMEM_BYTES = 64 * 1024 * 1024
VMEM_LIMIT = V7X_VMEM_BYTES - 8 * 1024 * 1024

NN = (((1,), (0,)), ((), ()))
NT = (((1,), (1,)), ((), ()))
TN = (((0,), (0,)), ((), ()))

ROW_GMIX, ROW_GX, ROW_GMEM, ROW_GFFN, ROW_GFIN, ROW_LB_HGN, ROW_PSCALE, ROW_LOSS = range(8)


def _dot(a, b, dims=NN):
    return lax.dot_general(a, b, dims, preferred_element_type=F32)


def _sigmoid(x):
    return 1.0 / (1.0 + jnp.exp(-x))


def _rms_fwd(x, g):
    r = lax.rsqrt(jnp.mean(x * x, axis=-1, keepdims=True) + EPS)
    n = x * r
    return n * g, n, r


def _rms_bwd(dh, n, r, g):
    dn = dh * g
    dx = r * (dn - n * jnp.mean(dn * n, axis=-1, keepdims=True))
    return dx, jnp.sum(dh * n, axis=0, keepdims=True)


def _params(sem=None):
    return pltpu.CompilerParams(dimension_semantics=sem, vmem_limit_bytes=VMEM_LIMIT)


def _const(shape):
    nd = len(shape)
    return pl.BlockSpec(shape, lambda *_: (0,) * nd, pipeline_mode=pl.Buffered(1))


def _const_out(shape):
    nd = len(shape)
    return pl.BlockSpec(shape, lambda *_: (0,) * nd)


def _acc_rows(ref, t, rows):
    upd = jnp.concatenate(rows + [jnp.zeros((8 - len(rows), rows[0].shape[1]), F32)], axis=0)

    @pl.when(t == 0)
    def _():
        ref[...] = upd

    @pl.when(t > 0)
    def _():
        ref[...] = ref[...] + upd


def _fuse_exchange(body, n_in, n_out, n_scratch, plan, ndim):
    if plan is None:
        return body
    n = plan.n

    def wrapped(*refs):
        ins, cin = refs[:n_in], refs[n_in:n_in + n]
        outs, cout = refs[n_in + n:n_in + n + n_out], refs[n_in + n + n_out:n_in + 2 * n + n_out]
        rest = refs[n_in + 2 * n + n_out:]
        scr, csem = rest[:n_scratch], rest[n_scratch:]
        first = pl.program_id(0) == 0
        last = pl.program_id(0) == pl.num_programs(0) - 1
        for i in range(1, ndim):
            first = first & (pl.program_id(i) == 0)
            last = last & (pl.program_id(i) == pl.num_programs(i) - 1)

        @pl.when(first)
        def _():
            plan.start(cin, cout, csem)

        if hasattr(plan, "middle"):
            step, total = pl.program_id(0), pl.num_programs(0)
            for i in range(1, ndim):
                step, total = step * pl.num_programs(i) + pl.program_id(i), total * pl.num_programs(i)

            @pl.when(step == total // 2)
            def _():
                plan.middle(cin, cout, csem)

        body(*ins, *outs, *scr)

        @pl.when(last)
        def _():
            plan.finish(cin, cout, csem)

    return wrapped


def _plan_extras(plan):
    if plan is None:
        return [], [], []
    return [ANY] * plan.n, list(plan.out_shape), list(plan.scratch_shapes)


def _in_proj(x, g, win_g, tm, plan=None, plan_args=()):
    s, d = x.shape
    nsh, _, wc = win_g.shape

    def body(x_ref, g_ref, w_ref, z_ref, h_ref):
        h, _, _ = _rms_fwd(x_ref[...], g_ref[...])
        hb = h.astype(BF16)
        h_ref[...] = hb
        for j in range(nsh):
            z_ref[:, j * wc:(j + 1) * wc] = _dot(hb, w_ref[j])

    x_specs, x_shapes, x_scratch = _plan_extras(plan)
    return pl.pallas_call(
        _fuse_exchange(body, 3, 2, 0, plan, 1), name="in_proj", grid=(s // tm,),
        in_specs=[pl.BlockSpec((tm, d), lambda t: (t, 0)), _const((1, d)), _const((nsh, d, wc))] + x_specs,
        out_specs=[pl.BlockSpec((tm, nsh * wc), lambda t: (t, 0)), pl.BlockSpec((tm, d), lambda t: (t, 0))] + x_specs,
        out_shape=[jax.ShapeDtypeStruct((s, nsh * wc), F32), jax.ShapeDtypeStruct((s, d), BF16)] + x_shapes,
        scratch_shapes=x_scratch,
        compiler_params=_params(("arbitrary",)),
    )(x, g, win_g, *plan_args)


def _lower_bound(l0, l1):
    m = jnp.maximum(l0, l1)
    e0, e1 = jnp.exp(l0 - m), jnp.exp(l1 - m)
    return e0 / (e0 + e1)


def _block_tri(n, group, upper):
    r = lax.broadcasted_iota(jnp.int32, (n, n), 0)
    c = lax.broadcasted_iota(jnp.int32, (n, n), 1)
    keep = (r // group == c // group) & ((c >= r) if upper else (c <= r))
    return keep.astype(BF16)


def _group_cumsum(tri, x):
    hi = x.astype(BF16)
    rest = x - hi.astype(F32)
    mid = rest.astype(BF16)
    lo = (rest - mid.astype(F32)).astype(BF16)
    return (_dot(tri, hi) + _dot(tri, mid)) + _dot(tri, lo)


def _decay(b, bj, rows, first):
    d = b - bj
    if first:
        d = jnp.where(rows >= first, d, NEG_BIG)
    return jnp.exp2(d)


class _RowSums:
    ORDER = (0, 4, 2, 6, 1, 5, 3, 7)

    def __init__(self, rows):
        self.rows = rows
        self.level = {4: {}, 2: {}, 1: {}}

    def _pair(self, p, q, d):
        return jnp.where((self.rows & d) != 0, p + pltpu.roll(p, d, axis=0), q + pltpu.roll(q, HALF - d, axis=0))

    def push(self, j, y, d=4):
        if d == 0:
            self.out = y
            return
        slot = self.level[d]
        key = j % d
        if key not in slot:
            slot[key] = (j, y)
            return
        j0, y0 = slot.pop(key)
        p, q = (y, y0) if j & d else (y0, y)
        self.push(key, self._pair(p, q, d), d // 2)

    def result(self):
        return self.out


def _hgrn_gates(qp, fp, lb):
    sq = _sigmoid(qp)
    sf = _sigmoid(fp)
    f = lb + (1.0 - lb) * sf
    return qp * sq, sq, f, sf


def _hgrn_fwd(z, l0, l1, gn, tc, unroll=1, plan=None, plan_args=()):
    s = z.shape[0]
    nsub = tc // SUB
    hd = HEAD_DIM
    hp = HEADS_PER_STEP

    def body(q_ref, f_ref, v_ref, g_ref, l0_ref, l1_ref, gn_ref, tri_ref, tric_ref, o_ref, oa_ref, st_ref,
             state, qs, ks, bs, os_):
        @pl.when(pl.program_id(1) == 0)
        def _():
            state[...] = jnp.zeros_like(state)

        cols = [slice(hh * hd, (hh + 1) * hd) for hh in range(hp)]
        q, k, lf, bc = [], [], [], []
        for hh, cs in enumerate(cols):
            qh, _, fh, _ = _hgrn_gates(q_ref[:, cs], f_ref[:, cs], _lower_bound(l0_ref[hh], l1_ref[hh]))
            q.append(qh)
            k.append(1.0 - fh)
            lf.append(jnp.log(fh) * LOG2E)
            bc.append(_group_cumsum(tric_ref[...], lf[hh]))
        bounded = functools.reduce(jnp.minimum, [jnp.min(b) for b in bc]) >= -MAX_LOG2_GROWTH

        @pl.when(bounded)
        def _():
            mask = tric_ref[...] > 0
            for hh, cs in enumerate(cols):
                qt = (q[hh] * jnp.exp2(bc[hh])).astype(BF16)
                ki = (k[hh] * jnp.exp2(-bc[hh])).astype(BF16)
                vb = v_ref[:, cs].astype(BF16)
                a = jnp.where(mask, _dot(qt, ki, NT), 0.0).astype(BF16)
                o_in = _dot(a, vb)
                for c in range(tc // CHUNK):
                    rs = slice(c * CHUNK, (c + 1) * CHUNK)
                    st = state[hh]
                    st_ref[hh, c] = st
                    os_[rs, cs] = o_in[rs] + _dot(qt[rs], st.astype(BF16), NT)
                    bl = bc[hh][(c + 1) * CHUNK - 1:(c + 1) * CHUNK, :]
                    kt = (k[hh][rs] * jnp.exp2(bl - bc[hh][rs])).astype(BF16)
                    state[hh] = st * jnp.exp2(bl) + _dot(vb[rs], kt, TN)

        @pl.when(jnp.logical_not(bounded))
        def _():
            rows = lax.broadcasted_iota(jnp.int32, (HALF, 1), 0)
            for hh, cs in enumerate(cols):
                qs[:, cs] = q[hh]
                ks[:, cs] = k[hh]
                bs[:, cs] = _group_cumsum(tri_ref[...], lf[hh])

                def step(i, carry, hh=hh, cs=cs):
                    r0 = pl.multiple_of(i * SUB, SUB)
                    q_ = qs[pl.ds(r0, SUB), cs]
                    k_ = ks[pl.ds(r0, SUB), cs]
                    b_ = bs[pl.ds(r0, SUB), cs]
                    v_ = v_ref[pl.ds(r0, SUB), cs]
                    st = state[hh]

                    @pl.when(i % (CHUNK // SUB) == 0)
                    def _():
                        st_ref[hh, i // (CHUNK // SUB)] = st

                    bl = b_[SUB - 1:SUB, :]
                    o = _dot((q_ * jnp.exp2(b_)).astype(BF16), st.astype(BF16), NT)
                    (q_lo, q_hi), (b_lo, b_hi), (o_lo, o_hi) = ((x[:HALF], x[HALF:]) for x in (q_, b_, o))
                    for j in range(SUB):
                        bj, kj, vj = b_[j:j + 1, :], k_[j:j + 1, :], v_[j:j + 1, :]
                        if j < HALF:
                            e = _decay(b_lo, bj, rows, j)
                            o_lo = o_lo + jnp.sum(q_lo * e * kj, axis=-1, keepdims=True) * vj
                        e = _decay(b_hi, bj, rows, j - HALF if j > HALF else None)
                        o_hi = o_hi + jnp.sum(q_hi * e * kj, axis=-1, keepdims=True) * vj
                    os_[pl.ds(r0, HALF), cs] = o_lo
                    os_[pl.ds(r0 + HALF, HALF), cs] = o_hi
                    kt = (k_ * jnp.exp2(bl - b_)).astype(BF16)
                    state[hh] = st * jnp.exp2(bl) + _dot(v_.astype(BF16), kt, TN)
                    return carry

                lax.fori_loop(0, nsub, step, 0, unroll=unroll)

        for hh, cs in enumerate(cols):
            o = os_[:, cs]
            o_ref[:, cs] = o
            r = lax.rsqrt(jnp.mean(o * o, axis=-1, keepdims=True) + EPS)
            gp = g_ref[:, cs]
            oa_ref[:, cs] = (o * r * gn_ref[hh] * (gp * _sigmoid(gp))).astype(BF16)

    ng = HGRN_HEADS // hp
    col = lambda k: pl.BlockSpec((tc, hp * hd), lambda h, t: (t, k * ng + h))
    vec = pl.BlockSpec((hp, 1, hd), lambda h, t: (h, 0, 0))
    out = pl.BlockSpec((tc, hp * hd), lambda h, t: (t, h))
    x_specs, x_shapes, x_scratch = _plan_extras(plan)
    return pl.pallas_call(
        _fuse_exchange(body, 9, 3, 5, plan, 2), name="hgrn_fwd", grid=(ng, s // tc),
        in_specs=[col(0), col(1), col(2), col(3), vec, vec, vec, _const((tc, tc)), _const((tc, tc))] + x_specs,
        out_specs=[out, out, pl.BlockSpec((hp, tc // CHUNK, hd, hd), lambda h, t: (h, t, 0, 0))] + x_specs,
        out_shape=[jax.ShapeDtypeStruct((s, HGRN_WIDTH), F32), jax.ShapeDtypeStruct((s, HGRN_WIDTH), BF16),
                   jax.ShapeDtypeStruct((HGRN_HEADS, s // CHUNK, hd, hd), F32)] + x_shapes,
        scratch_shapes=[pltpu.VMEM((hp, hd, hd), F32)] + [pltpu.VMEM((tc, hp * hd), F32)] * 4 + x_scratch,
        compiler_params=_params(("arbitrary", "arbitrary")),
    )(z, z, z, z, l0, l1, gn, _block_tri(tc, SUB, False), _block_tri(tc, CHUNK, False), *plan_args)


def _pooled(p, ext, tok0):
    tm = p.shape[0]
    tok = tok0 + lax.broadcasted_iota(jnp.int32, (tm, 1), 0)
    outs = []
    for g, w in enumerate(POOL_WINDOWS):
        acc = ext[:, g * HEAD_DIM:(g + 1) * HEAD_DIM]
        sh = 1
        while sh < w:
            acc = acc + pltpu.roll(acc, sh, axis=0)
            sh *= 2
        cnt = jnp.minimum(tok + 1, w).astype(F32)
        outs.append(acc[POOL_HALO:, :] / cnt - p[:, g * HEAD_DIM:(g + 1) * HEAD_DIM])
    return outs


def _pool_fwd(z, wp, scale, tm):
    s = z.shape[0]
    pw = len(POOL_WINDOWS) * HEAD_DIM
    nb = tm // POOL_HALO

    def body(p_ref, prev_ref, wp_ref, sc_ref, ob_ref):
        t = pl.program_id(0)
        p = p_ref[...]
        prev = jnp.where(t > 0, prev_ref[...], 0.0)
        pooled = _pooled(p, jnp.concatenate([prev, p], axis=0), t * tm)
        ys = [_dot(pooled[g].astype(BF16), wp_ref[g].astype(BF16)) for g in range(len(POOL_WINDOWS))]
        ob_ref[...] = (jnp.concatenate(ys, axis=1) * sc_ref[...]).astype(BF16)

    return pl.pallas_call(
        body, name="pool_fwd", grid=(s // tm,),
        in_specs=[pl.BlockSpec((tm, pw), lambda t: (t, 4)),
                  pl.BlockSpec((POOL_HALO, pw), lambda t: (jnp.maximum(t * nb - 1, 0), 4)),
                  _const(wp.shape), _const((1, pw))],
        out_specs=pl.BlockSpec((tm, pw), lambda t: (t, 0)),
        out_shape=jax.ShapeDtypeStruct((s, pw), BF16),
        compiler_params=_params(("parallel",)),
    )(z, z, wp, scale)


def _kv_proj(mem, g, slab_g):
    m, d = mem.shape
    rows = d // N_CHIPS

    def body(mem_ref, g_ref, wk_ref, wv_ref, xk_ref, xv_ref):
        hm, _, _ = _rms_fwd(mem_ref[...], g_ref[...])
        hb = hm.astype(BF16)
        xk_ref[...] = _dot(hb, wk_ref[...].reshape(d, d)).astype(BF16)
        xv_ref[...] = _dot(hb, wv_ref[...].reshape(d, d)).astype(BF16)

    blk = lambda k: pl.BlockSpec((N_CHIPS, rows, d), lambda i: (0, k, 0))
    return pl.pallas_call(
        body, name="kv_proj", grid=(1,),
        in_specs=[_const((m, d)), _const((1, d)), blk(0), blk(1)],
        out_specs=[_const_out((m, d)), _const_out((m, d))],
        out_shape=[jax.ShapeDtypeStruct((m, d), BF16)] * 2,
        compiler_params=_params(("arbitrary",)),
    )(mem, g, slab_g, slab_g)


def _softmax_rows(sc):
    e = jnp.exp(sc - jnp.max(sc, axis=-1, keepdims=True))
    return e / jnp.sum(e, axis=-1, keepdims=True)


def _mix_xattn_fwd(x, oa, ob, gx, slab_g, wo_g, xk, xv, tm, plan=None, plan_args=()):
    s, d = x.shape
    m = xk.shape[0]
    rows = d // N_CHIPS
    hw = oa.shape[1]
    e = XATTN_HEAD_DIM

    def body(x_ref, oa_ref, ob_ref, gx_ref, wout_ref, wq_ref, wo_ref, xk_ref, xv_ref,
             x1_ref, mixed_ref, hq_ref, xq_ref, att_ref, x2_ref):
        mixed = jnp.concatenate([oa_ref[...], ob_ref[...]], axis=1)
        mixed_ref[...] = mixed
        x1 = x_ref[...] + _dot(mixed, wout_ref[...].reshape(d, d))
        x1_ref[...] = x1
        hq, _, _ = _rms_fwd(x1, gx_ref[...])
        hqb = hq.astype(BF16)
        hq_ref[...] = hqb
        xq = _dot(hqb, wq_ref[...].reshape(d, d)).astype(BF16)
        xq_ref[...] = xq
        atts = []
        for h in range(XATTN_HEADS):
            cs = slice(h * e, (h + 1) * e)
            p = _softmax_rows(_dot(xq[:, cs], xk_ref[:, cs], NT) * (e ** -0.5))
            atts.append(_dot(p.astype(BF16), xv_ref[:, cs]).astype(BF16))
        att = jnp.concatenate(atts, axis=1)
        att_ref[...] = att
        for j in range(N_CHIPS):
            x2_ref[:, j * rows:(j + 1) * rows] = x1[:, j * rows:(j + 1) * rows] + _dot(att, wo_ref[j])

    tile = lambda w: pl.BlockSpec((tm, w), lambda t: (t, 0))
    blk = lambda k: pl.BlockSpec((N_CHIPS, rows, d), lambda t: (0, k, 0), pipeline_mode=pl.Buffered(1))
    x_specs, x_shapes, x_scratch = _plan_extras(plan)
    return pl.pallas_call(
        _fuse_exchange(body, 9, 6, 0, plan, 1), name="mix_xattn_fwd", grid=(s // tm,),
        in_specs=[tile(d), tile(hw), tile(hw), _const((1, d)), blk(0), blk(1), _const(wo_g.shape),
                  _const((m, d)), _const((m, d))] + x_specs,
        out_specs=[tile(d)] * 6 + x_specs,
        out_shape=[jax.ShapeDtypeStruct((s, d), F32)] + [jax.ShapeDtypeStruct((s, d), BF16)] * 4
                  + [jax.ShapeDtypeStruct((s, d), F32)] + x_shapes,
        scratch_shapes=x_scratch,
        compiler_params=_params(("arbitrary",)),
    )(x, oa, ob, gx, slab_g, slab_g, wo_g, xk, xv, *plan_args)


def _mlp_loss_fwd(x2, gffn, gfin, w1_g, w2_g, target, tm):
    s, d = x2.shape
    wr = w1_g.shape[1]

    def body(x2_ref, gffn_ref, gfin_ref, w1_ref, w2_ref, tg_ref, a_ref, hf_ref, dx3_ref, dx3b_ref, st_ref):
        x2v = x2_ref[...]
        hf, _, _ = _rms_fwd(x2v, gffn_ref[...])
        hfb = hf.astype(BF16)
        hf_ref[...] = hfb
        acc = x2v
        for j in range(N_CHIPS):
            a = _dot(hfb, w1_ref[j])
            a_ref[:, j * wr:(j + 1) * wr] = a
            r = jnp.maximum(a, 0.0)
            acc = acc + _dot((r * r).astype(BF16), w2_ref[j])
        gf = gfin_ref[...]
        y, n, r3 = _rms_fwd(acc, gf)
        err = y - tg_ref[...]
        loss = 0.5 * jnp.sum(jnp.sum(err * err, axis=-1, keepdims=True) * (1.0 / d), axis=0, keepdims=True)
        dy = err * (1.0 / d)
        dx3, dgf = _rms_bwd(dy, n, r3, gf)
        dx3_ref[...] = dx3
        dx3b_ref[...] = dx3.astype(BF16)
        _acc_rows(st_ref, pl.program_id(0), [dgf, jnp.broadcast_to(loss, (1, d))])

    tile = lambda w: pl.BlockSpec((tm, w), lambda t: (t, 0))
    blk = lambda k: pl.BlockSpec((N_CHIPS, wr, d), lambda t: (0, k, 0), pipeline_mode=pl.Buffered(1))
    return pl.pallas_call(
        body, name="mlp_loss_fwd", grid=(s // tm,),
        in_specs=[tile(d), _const((1, d)), _const((1, d)), blk(0), blk(0), tile(d)],
        out_specs=[tile(N_CHIPS * wr), tile(d), tile(d), tile(d), _const_out((8, d))],
        out_shape=[jax.ShapeDtypeStruct((s, N_CHIPS * wr), F32), jax.ShapeDtypeStruct((s, d), BF16),
                   jax.ShapeDtypeStruct((s, d), F32), jax.ShapeDtypeStruct((s, d), BF16),
                   jax.ShapeDtypeStruct((8, d), F32)],
        compiler_params=_params(("arbitrary",)),
    )(x2, gffn, gfin, w1_g, w2_g, target)


def _mlp_bwd(dx3, dx3b, a, x2, gffn, w1_g, w2_g, tm):
    s, d = x2.shape
    wr = w1_g.shape[1]

    def body(dx3_ref, dx3b_ref, a_ref, x2_ref, g_ref, w1_ref, w2_ref, da_ref, u_ref, dx2_ref, dx2b_ref, st_ref):
        dyb = dx3b_ref[...]
        dhf = jnp.zeros((tm, d), F32)
        for j in range(N_CHIPS):
            r = jnp.maximum(a_ref[:, j * wr:(j + 1) * wr], 0.0)
            da = (_dot(dyb, w2_ref[j], NT) * (2.0 * r)).astype(BF16)
            da_ref[:, j * wr:(j + 1) * wr] = da
            u_ref[:, j * wr:(j + 1) * wr] = (r * r).astype(BF16)
            dhf = dhf + _dot(da, w1_ref[j], NT)
        g = g_ref[...]
        _, n, r2 = _rms_fwd(x2_ref[...], g)
        dxn, dg = _rms_bwd(dhf, n, r2, g)
        dx2 = dx3_ref[...] + dxn
        dx2_ref[...] = dx2
        dx2b_ref[...] = dx2.astype(BF16)
        _acc_rows(st_ref, pl.program_id(0), [dg])

    tile = lambda w: pl.BlockSpec((tm, w), lambda t: (t, 0))
    blk = lambda k: pl.BlockSpec((N_CHIPS, wr, d), lambda t: (0, k, 0), pipeline_mode=pl.Buffered(1))
    nf = N_CHIPS * wr
    return pl.pallas_call(
        body, name="mlp_bwd", grid=(s // tm,),
        in_specs=[tile(d), tile(d), tile(nf), tile(d), _const((1, d)), blk(0), blk(0)],
        out_specs=[tile(nf), tile(nf), tile(d), tile(d), _const_out((8, d))],
        out_shape=[jax.ShapeDtypeStruct((s, nf), BF16), jax.ShapeDtypeStruct((s, nf), BF16),
                   jax.ShapeDtypeStruct((s, d), F32), jax.ShapeDtypeStruct((s, d), BF16),
                   jax.ShapeDtypeStruct((8, d), F32)],
        compiler_params=_params(("arbitrary",)),
    )(dx3, dx3b, a, x2, gffn, w1_g, w2_g)


def _xattn_mix_bwd(dx2, x1, xq, xk, xv, gx, slab_g, wo_g, tm, plan=None, plan_args=()):
    s, d = x1.shape
    m = xk.shape[0]
    rows = d // N_CHIPS
    e = XATTN_HEAD_DIM

    def body(dx2_ref, x1_ref, xq_ref, xk_ref, xv_ref, gx_ref, wout_ref, wq_ref, wo_ref,
             dx1_ref, dx1b_ref, dxq_ref, dmix_ref, dxk_ref, dxv_ref, st_ref):
        t = pl.program_id(0)
        dx2 = dx2_ref[...]
        dx2b = dx2.astype(BF16)
        datt = jnp.zeros((tm, d), F32)
        for j in range(N_CHIPS):
            datt = datt + _dot(dx2b[:, j * rows:(j + 1) * rows], wo_ref[j], NT)
        dattb = datt.astype(BF16)
        dxqs, dxks, dxvs = [], [], []
        for h in range(XATTN_HEADS):
            cs = slice(h * e, (h + 1) * e)
            xq_h, xk_h, xv_h = xq_ref[:, cs], xk_ref[:, cs], xv_ref[:, cs]
            p = _softmax_rows(_dot(xq_h, xk_h, NT) * (e ** -0.5))
            dp = _dot(dattb[:, cs], xv_h, NT)
            ds = (p * (dp - jnp.sum(dp * p, axis=-1, keepdims=True)) * (e ** -0.5)).astype(BF16)
            dxqs.append(_dot(ds, xk_h).astype(BF16))
            dxks.append(_dot(ds, xq_h, TN))
            dxvs.append(_dot(p.astype(BF16), dattb[:, cs], TN))
        dxq = jnp.concatenate(dxqs, axis=1)
        dxq_ref[...] = dxq
        dxk = jnp.concatenate(dxks, axis=1)
        dxv = jnp.concatenate(dxvs, axis=1)

        @pl.when(t == 0)
        def _():
            dxk_ref[...] = dxk
            dxv_ref[...] = dxv

        @pl.when(t > 0)
        def _():
            dxk_ref[...] = dxk_ref[...] + dxk
            dxv_ref[...] = dxv_ref[...] + dxv

        dhq = jnp.concatenate([_dot(dxq, wq_ref[j], NT) for j in range(N_CHIPS)], axis=1)
        g = gx_ref[...]
        _, n, r1 = _rms_fwd(x1_ref[...], g)
        dxn, dg = _rms_bwd(dhq, n, r1, g)
        dx1 = dx2 + dxn
        dx1_ref[...] = dx1
        dx1b = dx1.astype(BF16)
        dx1b_ref[...] = dx1b
        for j in range(N_CHIPS):
            dmix_ref[:, j * rows:(j + 1) * rows] = _dot(dx1b, wout_ref[j], NT)
        _acc_rows(st_ref, t, [dg])

    tile = lambda: pl.BlockSpec((tm, d), lambda t: (t, 0))
    blk = lambda k: pl.BlockSpec((N_CHIPS, rows, d), lambda t: (0, k, 0), pipeline_mode=pl.Buffered(1))
    x_specs, x_shapes, x_scratch = _plan_extras(plan)
    return pl.pallas_call(
        _fuse_exchange(body, 9, 7, 0, plan, 1), name="xattn_mix_bwd", grid=(s // tm,),
        in_specs=[tile(), tile(), tile(), _const((m, d)), _const((m, d)), _const((1, d)), blk(0), blk(1),
                  _const(wo_g.shape)] + x_specs,
        out_specs=[tile(), tile(), tile(), tile(), _const_out((m, d)), _const_out((m, d)), _const_out((8, d))]
                  + x_specs,
        out_shape=[jax.ShapeDtypeStruct((s, d), F32), jax.ShapeDtypeStruct((s, d), BF16),
                   jax.ShapeDtypeStruct((s, d), BF16), jax.ShapeDtypeStruct((s, d), F32),
                   jax.ShapeDtypeStruct((m, d), F32), jax.ShapeDtypeStruct((m, d), F32),
                   jax.ShapeDtypeStruct((8, d), F32)] + x_shapes,
        scratch_shapes=x_scratch,
        compiler_params=_params(("arbitrary",)),
    )(dx2, x1, xq, xk, xv, gx, slab_g, slab_g, wo_g, *plan_args)


def _kv_bwd(mem, g, dxk, dxv, slab_g):
    m, d = mem.shape
    rows = d // N_CHIPS

    def body(mem_ref, g_ref, dxk_ref, dxv_ref, wk_ref, wv_ref, dwk_ref, dwv_ref, st_ref):
        gv = g_ref[...]
        hm, n, _ = _rms_fwd(mem_ref[...], gv)
        hb = hm.astype(BF16)
        dkb = dxk_ref[...].astype(BF16)
        dvb = dxv_ref[...].astype(BF16)
        dhm = []
        for j in range(N_CHIPS):
            hj = hb[:, j * rows:(j + 1) * rows]
            dwk_ref[j] = _dot(hj, dkb, TN)
            dwv_ref[j] = _dot(hj, dvb, TN)
            dhm.append(_dot(dkb, wk_ref[j], NT) + _dot(dvb, wv_ref[j], NT))
        dg = jnp.sum(jnp.concatenate(dhm, axis=1) * n, axis=0, keepdims=True)
        st_ref[...] = jnp.concatenate([dg, jnp.zeros((7, d), F32)], axis=0)

    blk = lambda k: pl.BlockSpec((N_CHIPS, rows, d), lambda i: (0, k, 0))
    return pl.pallas_call(
        body, name="kv_bwd", grid=(1,),
        in_specs=[_const((m, d)), _const((1, d)), _const((m, d)), _const((m, d)), blk(0), blk(1)],
        out_specs=[_const_out((N_CHIPS, rows, d)), _const_out((N_CHIPS, rows, d)), _const_out((8, d))],
        out_shape=[jax.ShapeDtypeStruct((N_CHIPS, rows, d), F32)] * 2 + [jax.ShapeDtypeStruct((8, d), F32)],
        compiler_params=_params(("arbitrary",)),
    )(mem, g, dxk, dxv, slab_g, slab_g)


def _pool_bwd(z, dmix, wp, scale, tm, plan=None, plan_args=()):
    s = z.shape[0]
    ng = len(POOL_WINDOWS)
    pw = ng * HEAD_DIM
    nb = tm // POOL_HALO
    nt = s // tm
    n_ext = tm + POOL_HALO

    def body(p_ref, prev_ref, dm_ref, dmn_ref, wp_ref, sc_ref, dp_ref, dwp_ref, st_ref):
        t = pl.program_id(0)
        p = p_ref[...]
        prev = jnp.where(t > 0, prev_ref[...], 0.0)
        pooled = _pooled(p, jnp.concatenate([prev, p], axis=0), t * tm)
        dm = dm_ref[...]
        dme = jnp.concatenate([dm, jnp.where(t < nt - 1, dmn_ref[...], 0.0)], axis=0) * sc_ref[...]
        tok = t * tm + lax.broadcasted_iota(jnp.int32, (n_ext, 1), 0)
        dsc, dps, dwps = [], [], []
        for g, w in enumerate(POOL_WINDOWS):
            cs = slice(g * HEAD_DIM, (g + 1) * HEAD_DIM)
            wpb = wp_ref[g].astype(BF16)
            pb = pooled[g].astype(BF16)
            dsc.append(jnp.sum(dm[:, cs] * _dot(pb, wpb), axis=0, keepdims=True))
            dye = dme[:, cs].astype(BF16)
            dwps.append(_dot(pb, dye[:tm], TN))
            dpe = _dot(dye, wpb, NT)
            acc = dpe / jnp.minimum(tok + 1, w).astype(F32)
            sh = 1
            while sh < w:
                acc = acc + pltpu.roll(acc, n_ext - sh, axis=0)
                sh *= 2
            dps.append(acc[:tm] - dpe[:tm])
        dp_ref[...] = jnp.concatenate(dps, axis=1)
        dsc_row = jnp.concatenate(dsc, axis=1)

        @pl.when(t == 0)
        def _():
            for g in range(ng):
                dwp_ref[g] = dwps[g]

        @pl.when(t > 0)
        def _():
            for g in range(ng):
                dwp_ref[g] = dwp_ref[g] + dwps[g]

        _acc_rows(st_ref, t, [dsc_row])

    x_specs, x_shapes, x_scratch = _plan_extras(plan)
    return pl.pallas_call(
        _fuse_exchange(body, 6, 3, 0, plan, 1), name="pool_bwd", grid=(nt,),
        in_specs=[pl.BlockSpec((tm, pw), lambda t: (t, 4)),
                  pl.BlockSpec((POOL_HALO, pw), lambda t: (jnp.maximum(t * nb - 1, 0), 4)),
                  pl.BlockSpec((tm, pw), lambda t: (t, 1)),
                  pl.BlockSpec((POOL_HALO, pw), lambda t: (jnp.minimum((t + 1) * nb, s // POOL_HALO - 1), 1)),
                  _const(wp.shape), _const((1, pw))] + x_specs,
        out_specs=[pl.BlockSpec((tm, pw), lambda t: (t, 0)), _const_out(wp.shape), _const_out((8, pw))] + x_specs,
        out_shape=[jax.ShapeDtypeStruct((s, pw), F32), jax.ShapeDtypeStruct(wp.shape, F32),
                   jax.ShapeDtypeStruct((8, pw), F32)] + x_shapes,
        scratch_shapes=x_scratch,
        compiler_params=_params(("arbitrary",)),
    )(z, z, dmix, dmix, wp, scale, *plan_args)


def _hgrn_bwd(z, o, dmix, st, l0, l1, gn, tc, unroll=1, plan=None, plan_args=()):
    s = z.shape[0]
    nsub = tc // SUB
    nt = s // tc
    hd = HEAD_DIM
    hp = HEADS_PER_STEP

    def body(q_ref, f_ref, v_ref, g_ref, l0_ref, l1_ref, gn_ref, o_ref, dm_ref, st_ref,
             tril_ref, triu_ref, trilc_ref, triuc_ref,
             dq_ref, df_ref, di_ref, dg_ref, stat_ref, dstate, qs, ks, bs, dos, dqs, dks, dbs, sts):
        t = pl.program_id(1)

        @pl.when(t == 0)
        def _():
            dstate[...] = jnp.zeros_like(dstate)

        cols = [slice(hh * hd, (hh + 1) * hd) for hh in range(hp)]
        heads = []
        for hh, cs in enumerate(cols):
            lb = _lower_bound(l0_ref[hh], l1_ref[hh])
            qp = q_ref[:, cs]
            q, sq, f, sf = _hgrn_gates(qp, f_ref[:, cs], lb)
            lf = jnp.log(f) * LOG2E
            o = o_ref[:, cs]
            r = lax.rsqrt(jnp.mean(o * o, axis=-1, keepdims=True) + EPS)
            n = o * r
            gnv = gn_ref[hh]
            gp = g_ref[:, cs]
            sg = _sigmoid(gp)
            dm = dm_ref[:, cs]
            dg_ref[:, cs] = dm * (n * gnv) * (sg * (1.0 + gp * (1.0 - sg)))
            don = dm * (gp * sg)
            dn = don * gnv
            heads.append(dict(lb=lb, qp=qp, q=q, sq=sq, f=f, sf=sf, k=1.0 - f, lf=lf,
                              bc=_group_cumsum(trilc_ref[...], lf), dgn=jnp.sum(don * n, axis=0, keepdims=True),
                              do=r * (dn - n * jnp.mean(dn * n, axis=-1, keepdims=True))))
        bounded = functools.reduce(jnp.minimum, [jnp.min(h["bc"]) for h in heads]) >= -MAX_LOG2_GROWTH

        def factored(hh, cs, q, k, bc, do_all):
            eb = jnp.exp2(bc)
            eib = jnp.exp2(-bc)
            qt = (q * eb).astype(BF16)
            ki = (k * eib).astype(BF16)
            vb = v_ref[:, cs].astype(BF16)
            dob = do_all.astype(BF16)
            mask = trilc_ref[...] > 0
            a = jnp.where(mask, _dot(qt, ki, NT), 0.0).astype(BF16)
            da = jnp.where(mask, _dot(dob, vb, NT), 0.0).astype(BF16)
            dq_in = _dot(da, ki)
            dk_in = _dot(da, qt, TN)
            dv_in = _dot(a, dob, TN)
            last_row = lax.broadcasted_iota(jnp.int32, (CHUNK, 1), 0) == CHUNK - 1
            for c in reversed(range(tc // CHUNK)):
                rs = slice(c * CHUNK, (c + 1) * CHUNK)
                stp = st_ref[hh, c]
                dst = dstate[hh]
                dstb = dst.astype(BF16)
                bl = bc[(c + 1) * CHUNK - 1:(c + 1) * CHUNK, :]
                ekl = jnp.exp2(bl - bc[rs])
                ebl = jnp.exp2(bl)
                kt = k[rs] * ekl
                dq_st = _dot(dob[rs], stp.astype(BF16)) * eb[rs]
                dkt = _dot(vb[rs], dstb)
                extra = jnp.sum(kt * dkt, axis=0, keepdims=True) + ebl * jnp.sum(stp * dst, axis=0, keepdims=True)
                dqs[rs, cs] = dq_st + dq_in[rs] * eb[rs]
                dks[rs, cs] = dkt * ekl + dk_in[rs] * eib[rs]
                di_ref[rs, cs] = _dot(kt.astype(BF16), dstb, NT) + dv_in[rs]
                dbs[rs, cs] = (q[rs] * dq_st - kt * dkt + jnp.where(last_row, extra, 0.0)
                               + (qt[rs].astype(F32) * dq_in[rs] - ki[rs].astype(F32) * dk_in[rs]))
                dstate[hh] = dst * ebl + _dot(dob[rs], qt[rs], TN)
            dbs[:, cs] = _group_cumsum(triuc_ref[...], dbs[:, cs])

        def exact(hh, cs, q, k, lf, do_all):
            qs[:, cs] = q
            ks[:, cs] = k
            bs[:, cs] = _group_cumsum(tril_ref[...], lf)
            dos[:, cs] = do_all
            per = CHUNK // SUB

            def restore(i, carry):
                @pl.when(i % per == 0)
                def _():
                    sts[i] = st_ref[hh, i // per]

                @pl.when(i % per != 0)
                def _():
                    rp = pl.multiple_of((i - 1) * SUB, SUB)
                    b_ = bs[pl.ds(rp, SUB), cs]
                    bl = b_[SUB - 1:SUB, :]
                    kt = (ks[pl.ds(rp, SUB), cs] * jnp.exp2(bl - b_)).astype(BF16)
                    sts[i] = sts[i - 1] * jnp.exp2(bl) + _dot(v_ref[pl.ds(rp, SUB), cs].astype(BF16), kt, TN)

                return carry

            lax.fori_loop(0, nsub, restore, 0)
            rows = lax.broadcasted_iota(jnp.int32, (HALF, 1), 0)
            last_row = lax.broadcasted_iota(jnp.int32, (SUB, 1), 0) == SUB - 1

            def step(i, carry):
                ii = nsub - 1 - i
                r0 = pl.multiple_of(ii * SUB, SUB)
                q_ = qs[pl.ds(r0, SUB), cs]
                k_ = ks[pl.ds(r0, SUB), cs]
                b_ = bs[pl.ds(r0, SUB), cs]
                v_ = v_ref[pl.ds(r0, SUB), cs]
                do_ = dos[pl.ds(r0, SUB), cs]
                stp = sts[ii]
                dst = dstate[hh]
                bl = b_[SUB - 1:SUB, :]
                eb = jnp.exp2(b_)
                ekl = jnp.exp2(bl - b_)
                ebl = jnp.exp2(bl)
                dob = do_.astype(BF16)
                dstb = dst.astype(BF16)
                kt = k_ * ekl
                dq = _dot(dob, stp.astype(BF16)) * eb
                dkt = _dot(v_.astype(BF16), dstb)
                dk = dkt * ekl
                dv = _dot(kt.astype(BF16), dstb, NT)
                extra = jnp.sum(kt * dkt, axis=0, keepdims=True) + ebl * jnp.sum(stp * dst, axis=0, keepdims=True)
                halves = lambda x: [x[:HALF], x[HALF:]]
                q_h, b_h, do_h, dq_h, dk_h, dv_h = (halves(x) for x in (q_, b_, do_, dq, dk, dv))
                for own in range(2):
                    dk_rows, dv_rows = _RowSums(rows), _RowSums(rows)
                    for jj in _RowSums.ORDER:
                        j = own * HALF + jj
                        bj, kj, vj = b_[j:j + 1, :], k_[j:j + 1, :], v_[j:j + 1, :]
                        dk_sum = dv_sum = None
                        for h in range(own, 2):
                            e = _decay(b_h[h], bj, rows, jj if h == own else None)
                            pe = q_h[h] * e
                            acol = jnp.sum(pe * kj, axis=-1, keepdims=True)
                            dacol = jnp.sum(do_h[h] * vj, axis=-1, keepdims=True)
                            dq_h[h] = dq_h[h] + dacol * (e * kj)
                            dk_sum = dacol * pe if dk_sum is None else dk_sum + dacol * pe
                            dv_sum = acol * do_h[h] if dv_sum is None else dv_sum + acol * do_h[h]
                        dk_rows.push(jj, dk_sum)
                        dv_rows.push(jj, dv_sum)
                    dk_h[own] = dk_h[own] + dk_rows.result()
                    dv_h[own] = dv_h[own] + dv_rows.result()
                dq, dk, dv = (jnp.concatenate(x, axis=0) for x in (dq_h, dk_h, dv_h))
                dqs[pl.ds(r0, SUB), cs] = dq
                dks[pl.ds(r0, SUB), cs] = dk
                di_ref[pl.ds(r0, SUB), cs] = dv
                dbs[pl.ds(r0, SUB), cs] = q_ * dq - k_ * dk + jnp.where(last_row, extra, 0.0)
                dstate[hh] = dst * ebl + _dot(dob, (q_ * eb).astype(BF16), TN)
                return carry

            lax.fori_loop(0, nsub, step, 0, unroll=unroll)
            dbs[:, cs] = _group_cumsum(triu_ref[...], dbs[:, cs])

        @pl.when(bounded)
        def _():
            for hh, cs in enumerate(cols):
                factored(hh, cs, heads[hh]["q"], heads[hh]["k"], heads[hh]["bc"], heads[hh]["do"])

        @pl.when(jnp.logical_not(bounded))
        def _():
            for hh, cs in enumerate(cols):
                exact(hh, cs, heads[hh]["q"], heads[hh]["k"], heads[hh]["lf"], heads[hh]["do"])

        for hh, cs in enumerate(cols):
            h = heads[hh]
            dfv = dbs[:, cs] / h["f"] - dks[:, cs]
            df_ref[:, cs] = dfv * (1.0 - h["lb"]) * h["sf"] * (1.0 - h["sf"])
            dlb = jnp.sum(dfv * (1.0 - h["sf"]), axis=0, keepdims=True)
            dq_ref[:, cs] = dqs[:, cs] * (h["sq"] * (1.0 + h["qp"] * (1.0 - h["sq"])))
            _acc_rows(stat_ref.at[hh], t, [h["dgn"], dlb])

    rev = lambda t: nt - 1 - t
    ng = HGRN_HEADS // hp
    col = lambda k: pl.BlockSpec((tc, hp * hd), lambda h, t: (rev(t), k * ng + h))
    vec = pl.BlockSpec((hp, 1, hd), lambda h, t: (h, 0, 0))
    head = pl.BlockSpec((tc, hp * hd), lambda h, t: (rev(t), h))
    x_specs, x_shapes, x_scratch = _plan_extras(plan)
    return pl.pallas_call(
        _fuse_exchange(body, 14, 5, 9, plan, 2), name="hgrn_bwd", grid=(ng, nt),
        in_specs=[col(0), col(1), col(2), col(3), vec, vec, vec, head, head,
                  pl.BlockSpec((hp, tc // CHUNK, hd, hd), lambda h, t: (h, rev(t), 0, 0))]
                 + [_const((tc, tc))] * 4 + x_specs,
        out_specs=[head, head, head, head, pl.BlockSpec((hp, 8, hd), lambda h, t: (h, 0, 0))] + x_specs,
        out_shape=[jax.ShapeDtypeStruct((s, HGRN_WIDTH), F32)] * 4 + [jax.ShapeDtypeStruct((HGRN_HEADS, 8, hd), F32)]
                  + x_shapes,
        scratch_shapes=[pltpu.VMEM((hp, hd, hd), F32)] + [pltpu.VMEM((tc, hp * hd), F32)] * 7
                       + [pltpu.VMEM((nsub, hd, hd), F32)] + x_scratch,
        compiler_params=_params(("arbitrary", "arbitrary")),
    )(z, z, z, z, l0, l1, gn, o, dmix, st, _block_tri(tc, SUB, False), _block_tri(tc, SUB, True),
      _block_tri(tc, CHUNK, False), _block_tri(tc, CHUNK, True), *plan_args)


def _in_bwd(dparts, dx1, x, g, win_g, tm, plan=None, plan_args=()):
    s, d = x.shape
    nsh, _, wc = win_g.shape
    pw = dparts[0].shape[1]

    def body(dq_ref, df_ref, di_ref, dg_ref, dp_ref, dx1_ref, x_ref, g_ref, w_ref, gx_ref, dz_ref, st_ref):
        dz = jnp.concatenate([dq_ref[...], df_ref[...], di_ref[...], dg_ref[...], dp_ref[...]], axis=1).astype(BF16)
        dz_ref[...] = dz
        dh = jnp.zeros((tm, d), F32)
        for j in range(nsh):
            dh = dh + _dot(dz[:, j * wc:(j + 1) * wc], w_ref[j], NT)
        gv = g_ref[...]
        _, n, r = _rms_fwd(x_ref[...], gv)
        dxn, dg = _rms_bwd(dh, n, r, gv)
        gx_ref[...] = dx1_ref[...] + dxn
        _acc_rows(st_ref, pl.program_id(0), [dg])

    tile = lambda w: pl.BlockSpec((tm, w), lambda t: (t, 0))
    x_specs, x_shapes, x_scratch = _plan_extras(plan)
    return pl.pallas_call(
        _fuse_exchange(body, 9, 3, 0, plan, 1), name="in_bwd", grid=(s // tm,),
        in_specs=[tile(pw)] * 5 + [tile(d), tile(d), _const((1, d)), _const(win_g.shape)] + x_specs,
        out_specs=[tile(d), tile(nsh * wc), _const_out((8, d))] + x_specs,
        out_shape=[jax.ShapeDtypeStruct((s, d), F32), jax.ShapeDtypeStruct((s, nsh * wc), BF16),
                   jax.ShapeDtypeStruct((8, d), F32)] + x_shapes,
        scratch_shapes=x_scratch,
        compiler_params=_params(("arbitrary",)),
    )(*dparts, dx1, x, g, win_g, *plan_args)


def _tn_grad(name, a, b, out_rows, out_cols, a_sharded, plan=None, plan_args=()):
    s = a.shape[0]
    tr, tc = min(out_rows, GRAD_TILE), min(out_cols, GRAD_TILE)
    nr, nc = out_rows // tr, out_cols // tc

    def body(a_ref, b_ref, o_ref):
        o_ref[...] = _dot(a_ref[...], b_ref[...], TN)

    a_map = (lambda j, i, k: (0, j * nr + i)) if a_sharded else (lambda j, i, k: (0, i))
    b_map = (lambda j, i, k: (0, k)) if a_sharded else (lambda j, i, k: (0, j * nc + k))
    x_specs, x_shapes, x_scratch = _plan_extras(plan)
    res = pl.pallas_call(
        _fuse_exchange(body, 2, 1, 0, plan, 3), name=name, grid=(N_CHIPS, nr, nc),
        in_specs=[pl.BlockSpec((s, tr), a_map), pl.BlockSpec((s, tc), b_map)] + x_specs,
        out_specs=[pl.BlockSpec((None, tr, tc), lambda j, i, k: (j, i, k))] + x_specs,
        out_shape=[jax.ShapeDtypeStruct((N_CHIPS, out_rows, out_cols), F32)] + x_shapes,
        scratch_shapes=x_scratch,
        compiler_params=_params(("arbitrary", "arbitrary", "arbitrary")),
    )(a, b, *plan_args)
    return res if plan else res[0]


FFN_NAMES = ("w_ff1", "w_ff2")
ATTN_NAMES = ("w_xo", "w_xq", "w_out", "w_xk", "w_xv")
EARLY_NAMES = FFN_NAMES + ATTN_NAMES
BIG_NAMES = EARLY_NAMES + ("w_in",)


def _halved(g):
    return g.reshape(N_CHIPS, 2, g.shape[1] // 2, g.shape[2])


def _pair_adds(names, gs, got, idx):
    pairs = [_grad_pair_add("grad_pair_add_" + k, g, r, idx, tr=min(256, g.shape[2])) for k, g, r in zip(names, gs, got)]
    return [p[0] for p in pairs], [p[1] for p in pairs]


def _step(x, mem, target, small, shards, idx):
    d = x.shape[1]
    l0 = small["lb_logits"][0].reshape(HGRN_HEADS, 1, HEAD_DIM)
    l1 = small["lb_logits"][1].reshape(HGRN_HEADS, 1, HEAD_DIM)
    gn = small["hgrn_norm_g"].reshape(HGRN_HEADS, 1, HEAD_DIM)
    wp = small["w_pool"].reshape(len(POOL_WINDOWS), HEAD_DIM, HEAD_DIM)
    psc = small["pool_scale"].reshape(1, -1)
    gmix, gx, gmem, gffn = (small[k].reshape(1, d) for k in ("norm_mix_g", "norm_x_g", "norm_mem_g", "norm_ffn_g"))
    gfin = small["final_norm_g"].reshape(1, d)

    (win_g,) = _run_exchange("gather_w_in", _WeightGather([shards["w_in"]]), [shards["w_in"]])
    first_w = [shards["slab_kv"], shards["w_xo"]]
    z, h, kv_g, wo_g = _in_proj(x, gmix, win_g, tm=TOKEN_TILE, plan=_WeightGather(first_w), plan_args=first_w)
    mid_w = [shards["slab_oq"], shards["w_ff1"]]
    o, oa, st, oq_g, w1_g = _hgrn_fwd(z, l0, l1, gn, tc=HGRN_BLOCK, unroll=8,
                                      plan=_WeightGather(mid_w), plan_args=mid_w)
    ob = _pool_fwd(z, wp, psc, tm=TOKEN_TILE)
    xk, xv = _kv_proj(mem, gmem, kv_g)
    late_w = [shards["w_ff2"]]
    x1, mixed, hq, xq, att, x2, w2_g = _mix_xattn_fwd(x, oa, ob, gx, oq_g, wo_g, xk, xv, tm=TOKEN_TILE,
                                                      plan=_WeightGather(late_w), plan_args=late_w)
    a, hf, dx3, dx3b, st_loss = _mlp_loss_fwd(x2, gffn, gfin, w1_g, w2_g, target, tm=TOKEN_TILE)

    da, u, dx2, dx2b, st_ffn = _mlp_bwd(dx3, dx3b, a, x2, gffn, w1_g, w2_g, tm=MLP_BWD_TOKEN_TILE)
    g_ff1 = [_halved(_tn_grad("dw_ff1", hf, da, d, d, False))]
    dw_ff2, *got = _tn_grad("dw_ff2", u, dx3b, d, d, True, plan=_PairExchange(g_ff1), plan_args=g_ff1)
    keep_ff1, send_ff1 = _pair_adds(("w_ff1",), g_ff1, got, idx)
    g_ff2 = [_halved(dw_ff2)]
    dx1, dx1b, dxq, dmix, dxk, dxv, st_x, *got = _xattn_mix_bwd(
        dx2, x1, xq, xk, xv, gx, oq_g, wo_g, tm=TOKEN_TILE,
        plan=_Plans([_ChipExchange(send_ff1), _PairExchange(g_ff2)]), plan_args=send_ff1 + g_ff2)
    recv_ff1 = got[:1]
    keep_ff2, send_ff2 = _pair_adds(("w_ff2",), g_ff2, got[1:], idx)
    dw = {}
    dw["w_xo"] = _tn_grad("dw_xo", att, dx2b, d, d // N_CHIPS, False)
    dw["w_xq"] = _tn_grad("dw_xq", hq, dxq, d // N_CHIPS, d, True)
    dw["w_out"] = _tn_grad("dw_out", mixed, dx1b, d // N_CHIPS, d, True)
    dw["w_xk"], dw["w_xv"], st_mem = _kv_bwd(mem, gmem, dxk, dxv, kv_g)
    gs_attn = [_halved(dw[k]) for k in ATTN_NAMES]
    dp, d_wp, st_pool, *got_attn = _pool_bwd(z, dmix, wp, psc, tm=TOKEN_TILE, plan=_PairExchange(gs_attn), plan_args=gs_attn)
    keep_attn, send_attn = _pair_adds(ATTN_NAMES, gs_attn, got_attn, idx)
    sends = send_ff2 + send_attn
    dq, df, di, dg, st_hgrn, *received = _hgrn_bwd(z, o, dmix, st, l0, l1, gn, tc=HGRN_BLOCK, unroll=4,
                                                    plan=_ChipExchange(sends), plan_args=sends)
    grad_x, dz, st_mix = _in_bwd([dq, df, di, dg, dp], dx1, x, gmix, win_g, tm=TOKEN_TILE)
    keeps = keep_ff1 + keep_ff2 + keep_attn
    received = recv_ff1 + list(received)
    gs_in = [_halved(_tn_grad("dw_in", h, dz, d, win_g.shape[2], False))]
    got_in = _run_exchange("grad_pair_exchange_w_in", _PairExchange(gs_in), gs_in)
    keep_in, send_in = _pair_adds(("w_in",), gs_in, got_in, idx)

    partials = dict(zip(EARLY_NAMES, zip(keeps, received)))
    stats = dict(mix=st_mix, x=st_x, mem=st_mem, ffn=st_ffn, loss=st_loss, hgrn=st_hgrn, pool=st_pool)
    return grad_x, stats, d_wp, partials, keep_in, send_in


def _place():
    x, y, c = lax.axis_index("x"), lax.axis_index("y"), lax.axis_index("c")
    return x, y, c, [(x, 1 - y), (1 - x, y), (1 - x, 1 - y)]


def _rcopy(src, dst, ssem, rsem, dev):
    return pltpu.make_async_remote_copy(src_ref=src, dst_ref=dst, send_sem=ssem, recv_sem=rsem,
                                        device_id=dev, device_id_type=MESH)


class _WeightGather:
    def __init__(self, shards):
        self.n = len(shards)
        self.rows = [w.shape[0] for w in shards]
        self.out_shape = [jax.ShapeDtypeStruct((N_CHIPS,) + w.shape, w.dtype) for w in shards]
        dma = pltpu.SemaphoreType.DMA
        self.scratch_shapes = [dma((self.n,))] * 2 + [dma((self.n, 2))] * 4 + [dma((self.n, 3))] * 2

    def _copies(self, ins, outs, sems, phase):
        osem, orsem, ssem, rsem, qsem, qrsem, fsem, frsem = sems
        x, y, c, _ = _place()
        chip = 2 * x + y
        sib = (x, y, 1 - c)
        nbrs = [(x, 1 - y), (1 - x, y)]
        diag = 2 * (1 - x) + (1 - y)
        cps = []
        if phase in ("start", "finish"):
            cps += [("own", _rcopy(ins[a], outs[a].at[chip], osem.at[a], orsem.at[a], sib)) for a in range(self.n)]
        for a in range(self.n):
            hr = self.rows[a] // 2
            qr = hr // 2
            half = lambda who, hc, a=a, hr=hr: outs[a].at[who, pl.ds(hc * hr, hr), :]
            quarter = lambda who, k, a=a, hr=hr, qr=qr: outs[a].at[who, pl.ds(c * hr + k * qr, qr), :]
            for r, (px, py) in enumerate(nbrs):
                pc = 2 * px + py
                ox, oy = nbrs[1 - r]
                if phase in ("start", "finish"):
                    cps.append(("direct", _rcopy(ins[a].at[pl.ds(c * hr, hr), :], half(chip, c), ssem.at[a, r],
                                                 rsem.at[a, r], (px, py, c))))
                if phase == "middle":
                    cps.append(("arrived", _rcopy(half(pc, c), half(pc, c), ssem.at[a, r], rsem.at[a, r], (px, py, c))))
                if phase in ("middle", "finish"):
                    cps.append(("relay", _rcopy(quarter(pc, r), quarter(pc, r), qsem.at[a, r], qrsem.at[a, r],
                                                (ox, oy, c))))
                    cps.append(("pass", _rcopy(half(pc, c), half(pc, c), fsem.at[a, r], frsem.at[a, r], sib)))
                if phase == "finish":
                    cps.append(("relayed", _rcopy(quarter(diag, r), quarter(diag, r), qsem.at[a, r], qrsem.at[a, r],
                                                  (ox, oy, c))))
                    cps.append(("pass_in", _rcopy(half(pc, 1 - c), half(pc, 1 - c), fsem.at[a, r], frsem.at[a, r], sib)))
            if phase == "finish":
                cps.append(("pass_diag", _rcopy(half(diag, c), half(diag, c), fsem.at[a, 2], frsem.at[a, 2], sib)))
                cps.append(("pass_in", _rcopy(half(diag, 1 - c), half(diag, 1 - c), fsem.at[a, 2], frsem.at[a, 2], sib)))
        return cps

    def start(self, ins, outs, sems):
        for _, cp in self._copies(ins, outs, sems, "start"):
            cp.start()

    def middle(self, ins, outs, sems):
        for kind, cp in self._copies(ins, outs, sems, "middle"):
            if kind == "arrived":
                cp.wait_recv()
            else:
                cp.start()

    def finish(self, ins, outs, sems):
        cps = self._copies(ins, outs, sems, "finish")
        for kind, cp in cps:
            if kind == "relayed":
                cp.wait_recv()
        for kind, cp in cps:
            if kind == "pass_diag":
                cp.start()
        for kind, cp in cps:
            if kind == "pass_in":
                cp.wait_recv()
        for kind, cp in cps:
            if kind in ("direct", "relay", "pass", "pass_diag"):
                cp.wait_send()
        for kind, cp in cps:
            if kind == "own":
                cp.wait()


class _ChipExchange:
    def __init__(self, sends):
        self.n = len(sends)
        self.out_shape = [jax.ShapeDtypeStruct(g.shape, g.dtype) for g in sends]
        self.scratch_shapes = [pltpu.SemaphoreType.DMA((self.n, 3))] * 2

    def _copies(self, ins, outs, sems):
        ssem, rsem = sems
        _, _, c, peers = _place()
        return [_rcopy(ins[a].at[r], outs[a].at[r], ssem.at[a, r], rsem.at[a, r], (px, py, c))
                for a in range(self.n) for r, (px, py) in enumerate(peers)]

    def start(self, ins, outs, sems):
        for cp in self._copies(ins, outs, sems):
            cp.start()

    def finish(self, ins, outs, sems):
        for cp in self._copies(ins, outs, sems):
            cp.wait()


class _Plans:
    def __init__(self, plans):
        self.plans = plans
        self.n = sum(p.n for p in plans)
        self.out_shape = [s for p in plans for s in p.out_shape]
        self.scratch_shapes = [s for p in plans for s in p.scratch_shapes]

    def _each(self, ins, outs, sems):
        a = b = 0
        for p in self.plans:
            ns = len(p.scratch_shapes)
            yield p, ins[a:a + p.n], outs[a:a + p.n], sems[b:b + ns]
            a, b = a + p.n, b + ns

    def start(self, ins, outs, sems):
        for p, i, o, s in self._each(ins, outs, sems):
            p.start(i, o, s)

    def finish(self, ins, outs, sems):
        for p, i, o, s in self._each(ins, outs, sems):
            p.finish(i, o, s)


def _run_exchange(name, plan, arrays):
    n = plan.n

    def body(*refs):
        ins, outs, sems = refs[:n], refs[n:2 * n], refs[2 * n:]
        plan.start(ins, outs, sems)
        if hasattr(plan, "middle"):
            plan.middle(ins, outs, sems)
        plan.finish(ins, outs, sems)

    return pl.pallas_call(
        body, name=name, in_specs=[ANY] * n, out_specs=[ANY] * n,
        out_shape=plan.out_shape, scratch_shapes=plan.scratch_shapes,
    )(*arrays)


class _PairExchange:
    def __init__(self, gs):
        self.n = len(gs)
        self.out_shape = [jax.ShapeDtypeStruct((g.shape[0],) + g.shape[2:], g.dtype) for g in gs]
        self.scratch_shapes = [pltpu.SemaphoreType.DMA((self.n,))] * 2

    def _copies(self, ins, outs, sems):
        ssem, rsem = sems
        x, y, c, _ = _place()
        return [_rcopy(ins[a].at[:, 1 - c], outs[a], ssem.at[a], rsem.at[a], (x, y, 1 - c)) for a in range(self.n)]

    def start(self, ins, outs, sems):
        for cp in self._copies(ins, outs, sems):
            cp.start()

    def finish(self, ins, outs, sems):
        for cp in self._copies(ins, outs, sems):
            cp.wait()


def _grad_pair_add(name, g, got, idx, tr):
    _, _, hr, cc = g.shape

    def body(idx_ref, g0, g1, g2, g3, r0, r1, r2, r3, keep_ref, send_ref):
        keep_ref[...] = g0[...] + r0[...]
        for q, (gq, rq) in enumerate(((g1, r1), (g2, r2), (g3, r3))):
            send_ref[q] = (gq[...] + rq[...]).astype(BF16)

    gspec = lambda q: pl.BlockSpec((None, None, tr, cc), lambda i, idx: (idx[1 + q], idx[0], i, 0))
    rspec = lambda q: pl.BlockSpec((None, tr, cc), lambda i, idx: (idx[1 + q], i, 0))
    return pl.pallas_call(
        body, name=name,
        grid_spec=pltpu.PrefetchScalarGridSpec(
            num_scalar_prefetch=1, grid=(hr // tr,),
            in_specs=[gspec(q) for q in range(4)] + [rspec(q) for q in range(4)],
            out_specs=[pl.BlockSpec((tr, cc), lambda i, idx: (i, 0)), pl.BlockSpec((3, tr, cc), lambda i, idx: (0, i, 0))]),
        out_shape=[jax.ShapeDtypeStruct((hr, cc), F32), jax.ShapeDtypeStruct((3, hr, cc), BF16)],
        compiler_params=_params(("parallel",)),
    )(idx, g, g, g, g, got, got, got, got)


def _grad_chip_add(name, keep, got, tr):
    hr, cc = keep.shape

    def body(k_ref, g_ref, o_ref):
        o_ref[...] = ((k_ref[...] + g_ref[0].astype(F32)) + g_ref[1].astype(F32)) + g_ref[2].astype(F32)

    return pl.pallas_call(
        body, name=name, grid=(hr // tr,),
        in_specs=[pl.BlockSpec((tr, cc), lambda i: (i, 0)), pl.BlockSpec((3, tr, cc), lambda i: (0, i, 0))],
        out_specs=pl.BlockSpec((tr, cc), lambda i: (i, 0)),
        out_shape=jax.ShapeDtypeStruct((hr, cc), F32),
        compiler_params=_params(("parallel",)),
    )(keep, got)


class _HalfExchange:
    def __init__(self, ts):
        self.n = len(ts)
        self.out_shape = [jax.ShapeDtypeStruct(t.shape, t.dtype) for t in ts]
        self.scratch_shapes = [pltpu.SemaphoreType.DMA((self.n,))] * 2

    def _copies(self, ins, outs, sems):
        ssem, rsem = sems
        x, y, c, _ = _place()
        return [_rcopy(ins[a], outs[a], ssem.at[a], rsem.at[a], (x, y, 1 - c)) for a in range(self.n)]

    def start(self, ins, outs, sems):
        for cp in self._copies(ins, outs, sems):
            cp.start()

    def finish(self, ins, outs, sems):
        for cp in self._copies(ins, outs, sems):
            cp.wait()


def _small_allreduce(stats, d_wp, plan, plan_args):
    d = D_MODEL
    half = d // 2
    wps = d_wp.shape
    n = plan.n

    def body(mix_ref, x_ref, mem_ref, ffn_ref, loss_ref, hg_ref, pool_ref, wp_ref, *refs):
        cin, (slab_out, wp_out), cout = refs[:n], refs[n:n + 2], refs[n + 2:2 * n + 2]
        slab_buf, wp_buf, sib_s, sib_w, ssem, rsem = refs[2 * n + 2:2 * n + 8]
        csem = refs[2 * n + 8:]
        plan.start(cin, cout, csem)
        x, y, c, peers = _place()
        chip = 2 * x + y
        sib = (x, y, 1 - c)
        hgn = jnp.concatenate([hg_ref[h, 0:1, :] for h in range(HGRN_HEADS)], axis=1)
        dlb = jnp.concatenate([hg_ref[h, 1:2, :] for h in range(HGRN_HEADS)], axis=1)
        slab_buf[0] = jnp.concatenate([
            mix_ref[0:1, :], x_ref[0:1, :], mem_ref[0:1, :], ffn_ref[0:1, :], loss_ref[0:1, :],
            jnp.concatenate([dlb, hgn], axis=1),
            jnp.concatenate([pool_ref[0:1, :], jnp.zeros((1, half), F32)], axis=1),
            loss_ref[1:2, :]], axis=0)
        wp_buf[0] = wp_ref[...]
        pair = [_rcopy(slab_buf.at[0], sib_s, ssem.at[0], rsem.at[0], sib),
                _rcopy(wp_buf.at[0], sib_w, ssem.at[1], rsem.at[1], sib)]
        for cp in pair:
            cp.start()
        for cp in pair:
            cp.wait()
        slab_buf[0] = slab_buf[0] + sib_s[...]
        wp_buf[0] = wp_buf[0] + sib_w[...]
        cps = []
        for r, (px, py) in enumerate(peers):
            cps.append(_rcopy(slab_buf.at[0], slab_buf.at[r + 1], ssem.at[2 + 2 * r], rsem.at[2 + 2 * r], (px, py, c)))
            cps.append(_rcopy(wp_buf.at[0], wp_buf.at[r + 1], ssem.at[3 + 2 * r], rsem.at[3 + 2 * r], (px, py, c)))
        for cp in cps:
            cp.start()
        for cp in cps:
            cp.wait()
        tot_s, tot_w = slab_buf[chip], wp_buf[chip]
        for j in range(1, N_CHIPS):
            tot_s = tot_s + slab_buf[jnp.bitwise_xor(j, chip)]
            tot_w = tot_w + wp_buf[jnp.bitwise_xor(j, chip)]
        slab_out[...] = tot_s
        wp_out[...] = tot_w
        plan.finish(cin, cout, csem)

    return pl.pallas_call(
        body, name="small_allreduce",
        in_specs=[VMEM] * 8 + [ANY] * n, out_specs=[VMEM] * 2 + [ANY] * n,
        out_shape=[jax.ShapeDtypeStruct((8, d), F32), jax.ShapeDtypeStruct(wps, F32)] + list(plan.out_shape),
        scratch_shapes=[pltpu.VMEM((N_CHIPS, 8, d), F32), pltpu.VMEM((N_CHIPS,) + wps, F32),
                        pltpu.VMEM((8, d), F32), pltpu.VMEM(wps, F32),
                        pltpu.SemaphoreType.DMA((8,)), pltpu.SemaphoreType.DMA((8,))] + list(plan.scratch_shapes),
    )(stats["mix"], stats["x"], stats["mem"], stats["ffn"], stats["loss"], stats["hgrn"], stats["pool"], d_wp,
      *plan_args)


def _adamw_math(w, g, m, v):
    m = ADAM_B1 * m + (1.0 - ADAM_B1) * g
    v = ADAM_B2 * v + (1.0 - ADAM_B2) * (g * g)
    m_hat = m / (1.0 - ADAM_B1 ** ADAM_STEP)
    v_hat = v / (1.0 - ADAM_B2 ** ADAM_STEP)
    delta = -ADAM_LR * (m_hat / (jnp.sqrt(v_hat) + ADAM_EPS) + ADAM_WD * w)
    return delta, m, v


def _adamw(name, mine, theirs, w, m, v, idx, tr):
    rows = w.shape[0]
    cc = mine.shape[1]
    nb = rows // 2 // tr
    heads = w.shape[1] if w.ndim == 3 else 1
    e = cc // heads

    def body(idx_ref, a_ref, b_ref, w_ref, m_ref, v_ref, g_out, d_out, m_out, v_out):
        g = jnp.where(pl.program_id(0) // nb == idx_ref[0], a_ref[...], b_ref[...])
        if w.ndim == 2:
            g_out[...] = g
            d_out[...], m_out[...], v_out[...] = _adamw_math(w_ref[...], g, m_ref[...], v_ref[...])
        else:
            for h in range(heads):
                gh = g[:, h * e:(h + 1) * e]
                g_out[:, h, :] = gh
                d_out[:, h, :], m_out[:, h, :], v_out[:, h, :] = _adamw_math(
                    w_ref[:, h, :], gh, m_ref[:, h, :], v_ref[:, h, :])

    hspec = pl.BlockSpec((tr, cc), lambda i, idx: (i % nb, 0))
    spec = pl.BlockSpec((tr,) + w.shape[1:], lambda i, idx: (i,) + (0,) * (w.ndim - 1))
    return pl.pallas_call(
        body, name=name,
        grid_spec=pltpu.PrefetchScalarGridSpec(
            num_scalar_prefetch=1, grid=(rows // tr,),
            in_specs=[hspec, hspec, spec, spec, spec], out_specs=[spec] * 4),
        out_shape=[jax.ShapeDtypeStruct(w.shape, F32)] * 4,
        compiler_params=_params(("parallel",)),
    )(idx, mine, theirs, w, m, v)


SMALL_NAMES = ("norm_mix_g", "lb_logits", "hgrn_norm_g", "w_pool", "pool_scale", "norm_x_g", "norm_mem_g",
               "norm_ffn_g", "final_norm_g")


def _small_update(slab, d_wp, ws, ms, vs):
    n = len(SMALL_NAMES)
    half = D_MODEL // 2

    def body(slab_ref, wp_ref, *refs):
        w_refs, m_refs, v_refs, outs = refs[:n], refs[n:2 * n], refs[2 * n:3 * n], refs[3 * n:]
        row = lambda k: slab_ref[k:k + 1, :]
        lbl = w_refs[SMALL_NAMES.index("lb_logits")][...]
        s0 = _lower_bound(lbl[0:1, :], lbl[1:2, :])
        dl0 = row(ROW_LB_HGN)[:, :half] * s0 * (1.0 - s0)
        grads = dict(norm_mix_g=row(ROW_GMIX), lb_logits=jnp.concatenate([dl0, -dl0], axis=0),
                     hgrn_norm_g=row(ROW_LB_HGN)[:, half:], w_pool=wp_ref[...], pool_scale=row(ROW_PSCALE)[:, :half],
                     norm_x_g=row(ROW_GX), norm_mem_g=row(ROW_GMEM), norm_ffn_g=row(ROW_GFFN),
                     final_norm_g=row(ROW_GFIN))
        outs[0][...] = row(ROW_LOSS)[:, :128]
        for i, name in enumerate(SMALL_NAMES):
            g = grads[name]
            delta, m2, v2 = _adamw_math(w_refs[i][...], g, m_refs[i][...], v_refs[i][...])
            for o, val in zip(outs[1 + 4 * i:5 + 4 * i], (g, delta, m2, v2)):
                o[...] = val

    args = [ws[k] for k in SMALL_NAMES] + [ms[k] for k in SMALL_NAMES] + [vs[k] for k in SMALL_NAMES]
    out_shape = [jax.ShapeDtypeStruct((1, 128), F32)]
    for k in SMALL_NAMES:
        out_shape += [jax.ShapeDtypeStruct(ws[k].shape, F32)] * 4
    res = pl.pallas_call(
        body, name="small_update",
        in_specs=[VMEM] * (2 + 3 * n), out_specs=[VMEM] * len(out_shape), out_shape=out_shape,
    )(slab, d_wp, *args)
    return res[0], {k: res[1 + 4 * i:5 + 4 * i] for i, k in enumerate(SMALL_NAMES)}


ALL_NAMES = ("norm_mix_g", "w_in", "lb_logits", "hgrn_norm_g", "w_pool", "pool_scale", "w_out", "norm_x_g",
             "norm_mem_g", "w_xq", "w_xk", "w_xv", "w_xo", "norm_ffn_g", "w_ff1", "w_ff2", "final_norm_g")


def _shard_2d(name, a):
    a = a[0]
    if name in ("w_xq", "w_xk", "w_xv"):
        return a.reshape(a.shape[0], -1)
    if name == "w_xo":
        return a.reshape(-1, a.shape[-1])
    return a


def _small_2d(name, a):
    if name == "w_pool":
        return a.reshape(-1, HEAD_DIM)
    if name == "lb_logits":
        return a
    return a.reshape(1, -1)


def kernel(x, mem, norm_mix_g, w_in, lb_logits, hgrn_norm_g, w_pool, pool_scale, w_out, norm_x_g, norm_mem_g, w_xq, w_xk, w_xv, w_xo, norm_ffn_g, w_ff1, w_ff2, final_norm_g, loss_target, m_norm_mix_g, m_w_in, m_lb_logits, m_hgrn_norm_g, m_w_pool, m_pool_scale, m_w_out, m_norm_x_g, m_norm_mem_g, m_w_xq, m_w_xk, m_w_xv, m_w_xo, m_norm_ffn_g, m_w_ff1, m_w_ff2, m_final_norm_g, v_norm_mix_g, v_w_in, v_lb_logits, v_hgrn_norm_g, v_w_pool, v_pool_scale, v_w_out, v_norm_x_g, v_norm_mem_g, v_w_xq, v_w_xk, v_w_xv, v_w_xo, v_norm_ffn_g, v_w_ff1, v_w_ff2, v_final_norm_g):
    w = dict(norm_mix_g=norm_mix_g, w_in=w_in, lb_logits=lb_logits, hgrn_norm_g=hgrn_norm_g, w_pool=w_pool, pool_scale=pool_scale, w_out=w_out, norm_x_g=norm_x_g, norm_mem_g=norm_mem_g, w_xq=w_xq, w_xk=w_xk, w_xv=w_xv, w_xo=w_xo, norm_ffn_g=norm_ffn_g, w_ff1=w_ff1, w_ff2=w_ff2, final_norm_g=final_norm_g)
    m = dict(norm_mix_g=m_norm_mix_g, w_in=m_w_in, lb_logits=m_lb_logits, hgrn_norm_g=m_hgrn_norm_g, w_pool=m_w_pool, pool_scale=m_pool_scale, w_out=m_w_out, norm_x_g=m_norm_x_g, norm_mem_g=m_norm_mem_g, w_xq=m_w_xq, w_xk=m_w_xk, w_xv=m_w_xv, w_xo=m_w_xo, norm_ffn_g=m_norm_ffn_g, w_ff1=m_w_ff1, w_ff2=m_w_ff2, final_norm_g=m_final_norm_g)
    v = dict(norm_mix_g=v_norm_mix_g, w_in=v_w_in, lb_logits=v_lb_logits, hgrn_norm_g=v_hgrn_norm_g, w_pool=v_w_pool, pool_scale=v_pool_scale, w_out=v_w_out, norm_x_g=v_norm_x_g, norm_mem_g=v_norm_mem_g, w_xq=v_w_xq, w_xk=v_w_xk, w_xv=v_w_xv, w_xo=v_w_xo, norm_ffn_g=v_norm_ffn_g, w_ff1=v_w_ff1, w_ff2=v_w_ff2, final_norm_g=v_final_norm_g)

    big_w = {k: _shard_2d(k, w[k]) for k in BIG_NAMES}
    slab_oq = jnp.concatenate([big_w["w_out"], big_w["w_xq"]], axis=0).astype(BF16)
    slab_kv = jnp.concatenate([big_w["w_xk"], big_w["w_xv"]], axis=0).astype(BF16)
    shards = dict(slab_oq=slab_oq, slab_kv=slab_kv,
                  **{k: big_w[k].astype(BF16) for k in ("w_in", "w_xo", "w_ff1", "w_ff2")})

    cx, cy, cc = lax.axis_index("x"), lax.axis_index("y"), lax.axis_index("c")
    chip = 2 * cx + cy
    idx = jnp.stack([cc, chip, chip ^ 1, chip ^ 2, chip ^ 3]).astype(jnp.int32)
    small = {k: w[k] for k in SMALL_NAMES}
    grad_x, stats, d_wp, partials, keep_in, send_in = _step(x[0], mem[0], loss_target[0], small, shards, idx)

    chip_add = lambda k, keep, got: _grad_chip_add("grad_chip_add_" + k, keep, got, tr=min(256, keep.shape[0]))
    halves = {k: chip_add(k, *partials[k]) for k in EARLY_NAMES}
    early = [halves[k] for k in EARLY_NAMES]
    slab_sum, wp_sum, recv_in, *their_early = _small_allreduce(
        stats, d_wp.reshape(-1, HEAD_DIM), _Plans([_ChipExchange(send_in), _HalfExchange(early)]), send_in + early)
    halves["w_in"] = chip_add("w_in", keep_in[0], recv_in)
    theirs = dict(zip(EARLY_NAMES, their_early))
    (theirs["w_in"],) = _run_exchange("grad_half_exchange_w_in", _HalfExchange([halves["w_in"]]), [halves["w_in"]])

    grads, deltas, new_m, new_v = {}, {}, {}, {}
    for k in BIG_NAMES:
        as_held = (lambda a: a[0]) if k in ("w_xq", "w_xk", "w_xv") else functools.partial(_shard_2d, k)
        res = _adamw("adamw_" + k, halves[k], theirs[k], as_held(w[k]), as_held(m[k]), as_held(v[k]), idx,
                     tr=min(256, halves[k].shape[0]))
        for store, val in zip((grads, deltas, new_m, new_v), res):
            store[k] = val.reshape(w[k].shape)

    loss, upd = _small_update(slab_sum, wp_sum, {k: _small_2d(k, w[k]) for k in SMALL_NAMES},
                              {k: _small_2d(k, m[k]) for k in SMALL_NAMES}, {k: _small_2d(k, v[k]) for k in SMALL_NAMES})
    for k in SMALL_NAMES:
        for store, val in zip((grads, deltas, new_m, new_v), upd[k]):
            store[k] = val.reshape(w[k].shape)

    return (loss[0, 0], grad_x[None], *[grads[k] for k in ALL_NAMES], *[deltas[k] for k in ALL_NAMES],
            *[new_m[k] for k in ALL_NAMES], *[new_v[k] for k in ALL_NAMES])
```

```python
import functools

import jax
import jax.numpy as jnp
from jax import lax
from jax.experimental import pallas as pl
from jax.experimental.pallas import tpu as pltpu

F32 = jnp.float32
BF16 = jnp.bfloat16
LOG2E = 1.4426950408889634
NEG_BIG = -1e30
MAX_LOG2_GROWTH = 100.0
MESH = pl.DeviceIdType.MESH
ANY = pl.BlockSpec(memory_space=pl.ANY)
VMEM = pl.BlockSpec(memory_space=pltpu.VMEM)

D_MODEL = 1024
N_CHIPS = 4
HGRN_HEADS = 4
HEAD_DIM = 128
HGRN_WIDTH = HGRN_HEADS * HEAD_DIM
POOL_WINDOWS = (2, 4, 8, 16)
POOL_HALO = 16
SUB = 16
HALF = SUB // 2
CHUNK = 64
HEADS_PER_STEP = 4
XATTN_HEADS = 4
XATTN_HEAD_DIM = 256
EPS = 1e-6
ADAM_LR, ADAM_B1, ADAM_B2, ADAM_EPS, ADAM_WD, ADAM_STEP = 0.001, 0.9, 0.999, 1e-08, 0.01, 10

TOKEN_TILE = 512
MLP_BWD_TOKEN_TILE = 256
HGRN_BLOCK = 256
GRAD_TILE = 1024

V7X_VMEM_BYTES = 64 * 1024 * 1024
VMEM_LIMIT = V7X_VMEM_BYTES - 8 * 1024 * 1024

NN = (((1,), (0,)), ((), ()))
NT = (((1,), (1,)), ((), ()))
TN = (((0,), (0,)), ((), ()))

ROW_GMIX, ROW_GX, ROW_GMEM, ROW_GFFN, ROW_GFIN, ROW_LB_HGN, ROW_PSCALE, ROW_LOSS = range(8)


def _dot(a, b, dims=NN):
    return lax.dot_general(a, b, dims, preferred_element_type=F32)


def _sigmoid(x):
    return 1.0 / (1.0 + jnp.exp(-x))


def _rms_fwd(x, g):
    r = lax.rsqrt(jnp.mean(x * x, axis=-1, keepdims=True) + EPS)
    n = x * r
    return n * g, n, r


def _rms_bwd(dh, n, r, g):
    dn = dh * g
    dx = r * (dn - n * jnp.mean(dn * n, axis=-1, keepdims=True))
    return dx, jnp.sum(dh * n, axis=0, keepdims=True)


def _params(sem=None):
    return pltpu.CompilerParams(dimension_semantics=sem, vmem_limit_bytes=VMEM_LIMIT)


def _const(shape):
    nd = len(shape)
    return pl.BlockSpec(shape, lambda *_: (0,) * nd, pipeline_mode=pl.Buffered(1))


def _const_out(shape):
    nd = len(shape)
    return pl.BlockSpec(shape, lambda *_: (0,) * nd)


def _acc_rows(ref, t, rows):
    upd = jnp.concatenate(rows + [jnp.zeros((8 - len(rows), rows[0].shape[1]), F32)], axis=0)

    @pl.when(t == 0)
    def _():
        ref[...] = upd

    @pl.when(t > 0)
    def _():
        ref[...] = ref[...] + upd


def _fuse_exchange(body, n_in, n_out, n_scratch, plan, ndim):
    if plan is None:
        return body
    n = plan.n

    def wrapped(*refs):
        ins, cin = refs[:n_in], refs[n_in:n_in + n]
        outs, cout = refs[n_in + n:n_in + n + n_out], refs[n_in + n + n_out:n_in + 2 * n + n_out]
        rest = refs[n_in + 2 * n + n_out:]
        scr, csem = rest[:n_scratch], rest[n_scratch:]
        first = pl.program_id(0) == 0
        last = pl.program_id(0) == pl.num_programs(0) - 1
        for i in range(1, ndim):
            first = first & (pl.program_id(i) == 0)
            last = last & (pl.program_id(i) == pl.num_programs(i) - 1)

        @pl.when(first)
        def _():
            plan.start(cin, cout, csem)

        if hasattr(plan, "middle"):
            step, total = pl.program_id(0), pl.num_programs(0)
            for i in range(1, ndim):
                step, total = step * pl.num_programs(i) + pl.program_id(i), total * pl.num_programs(i)

            @pl.when(step == (5 * total) // 8)
            def _():
                plan.middle(cin, cout, csem)

        body(*ins, *outs, *scr)

        @pl.when(last)
        def _():
            plan.finish(cin, cout, csem)

    return wrapped


def _plan_extras(plan):
    if plan is None:
        return [], [], []
    return [ANY] * plan.n, list(plan.out_shape), list(plan.scratch_shapes)


def _in_proj(x, g, win_g, tm, plan=None, plan_args=()):
    s, d = x.shape
    nsh, _, wc = win_g.shape

    def body(x_ref, g_ref, w_ref, z_ref, h_ref):
        h, _, _ = _rms_fwd(x_ref[...], g_ref[...])
        hb = h.astype(BF16)
        h_ref[...] = hb
        for j in range(nsh):
            z_ref[:, j * wc:(j + 1) * wc] = _dot(hb, w_ref[j])

    x_specs, x_shapes, x_scratch = _plan_extras(plan)
    return pl.pallas_call(
        _fuse_exchange(body, 3, 2, 0, plan, 1), name="in_proj", grid=(s // tm,),
        in_specs=[pl.BlockSpec((tm, d), lambda t: (t, 0)), _const((1, d)), _const((nsh, d, wc))] + x_specs,
        out_specs=[pl.BlockSpec((tm, nsh * wc), lambda t: (t, 0)), pl.BlockSpec((tm, d), lambda t: (t, 0))] + x_specs,
        out_shape=[jax.ShapeDtypeStruct((s, nsh * wc), F32), jax.ShapeDtypeStruct((s, d), BF16)] + x_shapes,
        scratch_shapes=x_scratch,
        compiler_params=_params(("arbitrary",)),
    )(x, g, win_g, *plan_args)


def _lower_bound(l0, l1):
    m = jnp.maximum(l0, l1)
    e0, e1 = jnp.exp(l0 - m), jnp.exp(l1 - m)
    return e0 / (e0 + e1)


def _block_tri(n, group, upper):
    r = lax.broadcasted_iota(jnp.int32, (n, n), 0)
    c = lax.broadcasted_iota(jnp.int32, (n, n), 1)
    keep = (r // group == c // group) & ((c >= r) if upper else (c <= r))
    return keep.astype(BF16)


def _group_cumsum(tri, x):
    hi = x.astype(BF16)
    rest = x - hi.astype(F32)
    mid = rest.astype(BF16)
    lo = (rest - mid.astype(F32)).astype(BF16)
    return (_dot(tri, hi) + _dot(tri, mid)) + _dot(tri, lo)


def _decay(b, bj, rows, first):
    d = b - bj
    if first:
        d = jnp.where(rows >= first, d, NEG_BIG)
    return jnp.exp2(d)


class _RowSums:
    ORDER = (0, 4, 2, 6, 1, 5, 3, 7)

    def __init__(self, rows):
        self.rows = rows
        self.level = {4: {}, 2: {}, 1: {}}

    def _pair(self, p, q, d):
        return jnp.where((self.rows & d) != 0, p + pltpu.roll(p, d, axis=0), q + pltpu.roll(q, HALF - d, axis=0))

    def push(self, j, y, d=4):
        if d == 0:
            self.out = y
            return
        slot = self.level[d]
        key = j % d
        if key not in slot:
            slot[key] = (j, y)
            return
        j0, y0 = slot.pop(key)
        p, q = (y, y0) if j & d else (y0, y)
        self.push(key, self._pair(p, q, d), d // 2)

    def result(self):
        return self.out


def _hgrn_gates(qp, fp, lb):
    sq = _sigmoid(qp)
    sf = _sigmoid(fp)
    f = lb + (1.0 - lb) * sf
    return qp * sq, sq, f, sf


def _hgrn_fwd(z, l0, l1, gn, tc, unroll=1, plan=None, plan_args=()):
    s = z.shape[0]
    nsub = tc // SUB
    hd = HEAD_DIM
    hp = HEADS_PER_STEP

    def body(q_ref, f_ref, v_ref, g_ref, l0_ref, l1_ref, gn_ref, tri_ref, tric_ref, o_ref, oa_ref, st_ref,
             state, qs, ks, bs, os_):
        @pl.when(pl.program_id(1) == 0)
        def _():
            state[...] = jnp.zeros_like(state)

        cols = [slice(hh * hd, (hh + 1) * hd) for hh in range(hp)]
        q, k, lf, bc = [], [], [], []
        for hh, cs in enumerate(cols):
            qh, _, fh, _ = _hgrn_gates(q_ref[:, cs], f_ref[:, cs], _lower_bound(l0_ref[hh], l1_ref[hh]))
            q.append(qh)
            k.append(1.0 - fh)
            lf.append(jnp.log(fh) * LOG2E)
            bc.append(_group_cumsum(tric_ref[...], lf[hh]))
        bounded = functools.reduce(jnp.minimum, [jnp.min(b) for b in bc]) >= -MAX_LOG2_GROWTH

        @pl.when(bounded)
        def _():
            mask = tric_ref[...] > 0
            for hh, cs in enumerate(cols):
                qt = (q[hh] * jnp.exp2(bc[hh])).astype(BF16)
                ki = (k[hh] * jnp.exp2(-bc[hh])).astype(BF16)
                vb = v_ref[:, cs].astype(BF16)
                a = jnp.where(mask, _dot(qt, ki, NT), 0.0).astype(BF16)
                o_in = _dot(a, vb)
                for c in range(tc // CHUNK):
                    rs = slice(c * CHUNK, (c + 1) * CHUNK)
                    st = state[hh]
                    st_ref[hh, c] = st
                    os_[rs, cs] = o_in[rs] + _dot(qt[rs], st.astype(BF16), NT)
                    bl = bc[hh][(c + 1) * CHUNK - 1:(c + 1) * CHUNK, :]
                    kt = (k[hh][rs] * jnp.exp2(bl - bc[hh][rs])).astype(BF16)
                    state[hh] = st * jnp.exp2(bl) + _dot(vb[rs], kt, TN)

        @pl.when(jnp.logical_not(bounded))
        def _():
            rows = lax.broadcasted_iota(jnp.int32, (HALF, 1), 0)
            for hh, cs in enumerate(cols):
                qs[:, cs] = q[hh]
                ks[:, cs] = k[hh]
                bs[:, cs] = _group_cumsum(tri_ref[...], lf[hh])

                def step(i, carry, hh=hh, cs=cs):
                    r0 = pl.multiple_of(i * SUB, SUB)
                    q_ = qs[pl.ds(r0, SUB), cs]
                    k_ = ks[pl.ds(r0, SUB), cs]
                    b_ = bs[pl.ds(r0, SUB), cs]
                    v_ = v_ref[pl.ds(r0, SUB), cs]
                    st = state[hh]

                    @pl.when(i % (CHUNK // SUB) == 0)
                    def _():
                        st_ref[hh, i // (CHUNK // SUB)] = st

                    bl = b_[SUB - 1:SUB, :]
                    o = _dot((q_ * jnp.exp2(b_)).astype(BF16), st.astype(BF16), NT)
                    (q_lo, q_hi), (b_lo, b_hi), (o_lo, o_hi) = ((x[:HALF], x[HALF:]) for x in (q_, b_, o))
                    for j in range(SUB):
                        bj, kj, vj = b_[j:j + 1, :], k_[j:j + 1, :], v_[j:j + 1, :]
                        if j < HALF:
                            e = _decay(b_lo, bj, rows, j)
                            o_lo = o_lo + jnp.sum(q_lo * e * kj, axis=-1, keepdims=True) * vj
                        e = _decay(b_hi, bj, rows, j - HALF if j > HALF else None)
                        o_hi = o_hi + jnp.sum(q_hi * e * kj, axis=-1, keepdims=True) * vj
                    os_[pl.ds(r0, HALF), cs] = o_lo
                    os_[pl.ds(r0 + HALF, HALF), cs] = o_hi
                    kt = (k_ * jnp.exp2(bl - b_)).astype(BF16)
                    state[hh] = st * jnp.exp2(bl) + _dot(v_.astype(BF16), kt, TN)
                    return carry

                lax.fori_loop(0, nsub, step, 0, unroll=unroll)

        for hh, cs in enumerate(cols):
            o = os_[:, cs]
            o_ref[:, cs] = o
            r = lax.rsqrt(jnp.mean(o * o, axis=-1, keepdims=True) + EPS)
            gp = g_ref[:, cs]
            oa_ref[:, cs] = (o * r * gn_ref[hh] * (gp * _sigmoid(gp))).astype(BF16)

    ng = HGRN_HEADS // hp
    col = lambda k: pl.BlockSpec((tc, hp * hd), lambda h, t: (t, k * ng + h))
    vec = pl.BlockSpec((hp, 1, hd), lambda h, t: (h, 0, 0))
    out = pl.BlockSpec((tc, hp * hd), lambda h, t: (t, h))
    x_specs, x_shapes, x_scratch = _plan_extras(plan)
    return pl.pallas_call(
        _fuse_exchange(body, 9, 3, 5, plan, 2), name="hgrn_fwd", grid=(ng, s // tc),
        in_specs=[col(0), col(1), col(2), col(3), vec, vec, vec, _const((tc, tc)), _const((tc, tc))] + x_specs,
        out_specs=[out, out, pl.BlockSpec((hp, tc // CHUNK, hd, hd), lambda h, t: (h, t, 0, 0))] + x_specs,
        out_shape=[jax.ShapeDtypeStruct((s, HGRN_WIDTH), F32), jax.ShapeDtypeStruct((s, HGRN_WIDTH), BF16),
                   jax.ShapeDtypeStruct((HGRN_HEADS, s // CHUNK, hd, hd), F32)] + x_shapes,
        scratch_shapes=[pltpu.VMEM((hp, hd, hd), F32)] + [pltpu.VMEM((tc, hp * hd), F32)] * 4 + x_scratch,
        compiler_params=_params(("arbitrary", "arbitrary")),
    )(z, z, z, z, l0, l1, gn, _block_tri(tc, SUB, False), _block_tri(tc, CHUNK, False), *plan_args)


def _pooled(p, ext, tok0):
    tm = p.shape[0]
    tok = tok0 + lax.broadcasted_iota(jnp.int32, (tm, 1), 0)
    outs = []
    for g, w in enumerate(POOL_WINDOWS):
        acc = ext[:, g * HEAD_DIM:(g + 1) * HEAD_DIM]
        sh = 1
        while sh < w:
            acc = acc + pltpu.roll(acc, sh, axis=0)
            sh *= 2
        cnt = jnp.minimum(tok + 1, w).astype(F32)
        outs.append(acc[POOL_HALO:, :] / cnt - p[:, g * HEAD_DIM:(g + 1) * HEAD_DIM])
    return outs


def _pool_fwd(z, wp, scale, tm):
    s = z.shape[0]
    pw = len(POOL_WINDOWS) * HEAD_DIM
    nb = tm // POOL_HALO

    def body(p_ref, prev_ref, wp_ref, sc_ref, ob_ref):
        t = pl.program_id(0)
        p = p_ref[...]
        prev = jnp.where(t > 0, prev_ref[...], 0.0)
        pooled = _pooled(p, jnp.concatenate([prev, p], axis=0), t * tm)
        ys = [_dot(pooled[g].astype(BF16), wp_ref[g].astype(BF16)) for g in range(len(POOL_WINDOWS))]
        ob_ref[...] = (jnp.concatenate(ys, axis=1) * sc_ref[...]).astype(BF16)

    return pl.pallas_call(
        body, name="pool_fwd", grid=(s // tm,),
        in_specs=[pl.BlockSpec((tm, pw), lambda t: (t, 4)),
                  pl.BlockSpec((POOL_HALO, pw), lambda t: (jnp.maximum(t * nb - 1, 0), 4)),
                  _const(wp.shape), _const((1, pw))],
        out_specs=pl.BlockSpec((tm, pw), lambda t: (t, 0)),
        out_shape=jax.ShapeDtypeStruct((s, pw), BF16),
        compiler_params=_params(("parallel",)),
    )(z, z, wp, scale)


def _kv_proj(mem, g, slab_g):
    m, d = mem.shape
    rows = d // N_CHIPS

    def body(mem_ref, g_ref, wk_ref, wv_ref, xk_ref, xv_ref):
        hm, _, _ = _rms_fwd(mem_ref[...], g_ref[...])
        hb = hm.astype(BF16)
        xk_ref[...] = _dot(hb, wk_ref[...].reshape(d, d)).astype(BF16)
        xv_ref[...] = _dot(hb, wv_ref[...].reshape(d, d)).astype(BF16)

    blk = lambda k: pl.BlockSpec((N_CHIPS, rows, d), lambda i: (0, k, 0))
    return pl.pallas_call(
        body, name="kv_proj", grid=(1,),
        in_specs=[_const((m, d)), _const((1, d)), blk(0), blk(1)],
        out_specs=[_const_out((m, d)), _const_out((m, d))],
        out_shape=[jax.ShapeDtypeStruct((m, d), BF16)] * 2,
        compiler_params=_params(("arbitrary",)),
    )(mem, g, slab_g, slab_g)


def _softmax_rows(sc):
    e = jnp.exp(sc - jnp.max(sc, axis=-1, keepdims=True))
    return e / jnp.sum(e, axis=-1, keepdims=True)


def _mix_xattn_fwd(x, oa, ob, gx, slab_g, wo_g, xk, xv, tm, plan=None, plan_args=()):
    s, d = x.shape
    m = xk.shape[0]
    rows = d // N_CHIPS
    hw = oa.shape[1]
    e = XATTN_HEAD_DIM

    def body(x_ref, oa_ref, ob_ref, gx_ref, wout_ref, wq_ref, wo_ref, xk_ref, xv_ref,
             x1_ref, mixed_ref, hq_ref, xq_ref, att_ref, x2_ref):
        mixed = jnp.concatenate([oa_ref[...], ob_ref[...]], axis=1)
        mixed_ref[...] = mixed
        x1 = x_ref[...] + _dot(mixed, wout_ref[...].reshape(d, d))
        x1_ref[...] = x1
        hq, _, _ = _rms_fwd(x1, gx_ref[...])
        hqb = hq.astype(BF16)
        hq_ref[...] = hqb
        xq = _dot(hqb, wq_ref[...].reshape(d, d)).astype(BF16)
        xq_ref[...] = xq
        atts = []
        for h in range(XATTN_HEADS):
            cs = slice(h * e, (h + 1) * e)
            p = _softmax_rows(_dot(xq[:, cs], xk_ref[:, cs], NT) * (e ** -0.5))
            atts.append(_dot(p.astype(BF16), xv_ref[:, cs]).astype(BF16))
        att = jnp.concatenate(atts, axis=1)
        att_ref[...] = att
        for j in range(N_CHIPS):
            x2_ref[:, j * rows:(j + 1) * rows] = x1[:, j * rows:(j + 1) * rows] + _dot(att, wo_ref[j])

    tile = lambda w: pl.BlockSpec((tm, w), lambda t: (t, 0))
    blk = lambda k: pl.BlockSpec((N_CHIPS, rows, d), lambda t: (0, k, 0), pipeline_mode=pl.Buffered(1))
    x_specs, x_shapes, x_scratch = _plan_extras(plan)
    return pl.pallas_call(
        _fuse_exchange(body, 9, 6, 0, plan, 1), name="mix_xattn_fwd", grid=(s // tm,),
        in_specs=[tile(d), tile(hw), tile(hw), _const((1, d)), blk(0), blk(1), _const(wo_g.shape),
                  _const((m, d)), _const((m, d))] + x_specs,
        out_specs=[tile(d)] * 6 + x_specs,
        out_shape=[jax.ShapeDtypeStruct((s, d), F32)] + [jax.ShapeDtypeStruct((s, d), BF16)] * 4
                  + [jax.ShapeDtypeStruct((s, d), F32)] + x_shapes,
        scratch_shapes=x_scratch,
        compiler_params=_params(("arbitrary",)),
    )(x, oa, ob, gx, slab_g, slab_g, wo_g, xk, xv, *plan_args)


def _mlp_loss_fwd(x2, gffn, gfin, w1_g, w2_g, target, tm):
    s, d = x2.shape
    wr = w1_g.shape[1]

    def body(x2_ref, gffn_ref, gfin_ref, w1_ref, w2_ref, tg_ref, a_ref, hf_ref, dx3_ref, dx3b_ref, st_ref):
        x2v = x2_ref[...]
        hf, _, _ = _rms_fwd(x2v, gffn_ref[...])
        hfb = hf.astype(BF16)
        hf_ref[...] = hfb
        acc = x2v
        for j in range(N_CHIPS):
            a = _dot(hfb, w1_ref[j])
            a_ref[:, j * wr:(j + 1) * wr] = a
            r = jnp.maximum(a, 0.0)
            acc = acc + _dot((r * r).astype(BF16), w2_ref[j])
        gf = gfin_ref[...]
        y, n, r3 = _rms_fwd(acc, gf)
        err = y - tg_ref[...]
        loss = 0.5 * jnp.sum(jnp.sum(err * err, axis=-1, keepdims=True) * (1.0 / d), axis=0, keepdims=True)
        dy = err * (1.0 / d)
        dx3, dgf = _rms_bwd(dy, n, r3, gf)
        dx3_ref[...] = dx3
        dx3b_ref[...] = dx3.astype(BF16)
        _acc_rows(st_ref, pl.program_id(0), [dgf, jnp.broadcast_to(loss, (1, d))])

    tile = lambda w: pl.BlockSpec((tm, w), lambda t: (t, 0))
    blk = lambda k: pl.BlockSpec((N_CHIPS, wr, d), lambda t: (0, k, 0), pipeline_mode=pl.Buffered(1))
    return pl.pallas_call(
        body, name="mlp_loss_fwd", grid=(s // tm,),
        in_specs=[tile(d), _const((1, d)), _const((1, d)), blk(0), blk(0), tile(d)],
        out_specs=[tile(N_CHIPS * wr), tile(d), tile(d), tile(d), _const_out((8, d))],
        out_shape=[jax.ShapeDtypeStruct((s, N_CHIPS * wr), F32), jax.ShapeDtypeStruct((s, d), BF16),
                   jax.ShapeDtypeStruct((s, d), F32), jax.ShapeDtypeStruct((s, d), BF16),
                   jax.ShapeDtypeStruct((8, d), F32)],
        compiler_params=_params(("arbitrary",)),
    )(x2, gffn, gfin, w1_g, w2_g, target)


def _mlp_bwd(dx3, dx3b, a, x2, gffn, w1_g, w2_g, tm):
    s, d = x2.shape
    wr = w1_g.shape[1]

    def body(dx3_ref, dx3b_ref, a_ref, x2_ref, g_ref, w1_ref, w2_ref, da_ref, u_ref, dx2_ref, dx2b_ref, st_ref):
        dyb = dx3b_ref[...]
        dhf = jnp.zeros((tm, d), F32)
        for j in range(N_CHIPS):
            r = jnp.maximum(a_ref[:, j * wr:(j + 1) * wr], 0.0)
            da = (_dot(dyb, w2_ref[j], NT) * (2.0 * r)).astype(BF16)
            da_ref[:, j * wr:(j + 1) * wr] = da
            u_ref[:, j * wr:(j + 1) * wr] = (r * r).astype(BF16)
            dhf = dhf + _dot(da, w1_ref[j], NT)
        g = g_ref[...]
        _, n, r2 = _rms_fwd(x2_ref[...], g)
        dxn, dg = _rms_bwd(dhf, n, r2, g)
        dx2 = dx3_ref[...] + dxn
        dx2_ref[...] = dx2
        dx2b_ref[...] = dx2.astype(BF16)
        _acc_rows(st_ref, pl.program_id(0), [dg])

    tile = lambda w: pl.BlockSpec((tm, w), lambda t: (t, 0))
    blk = lambda k: pl.BlockSpec((N_CHIPS, wr, d), lambda t: (0, k, 0), pipeline_mode=pl.Buffered(1))
    nf = N_CHIPS * wr
    return pl.pallas_call(
        body, name="mlp_bwd", grid=(s // tm,),
        in_specs=[tile(d), tile(d), tile(nf), tile(d), _const((1, d)), blk(0), blk(0)],
        out_specs=[tile(nf), tile(nf), tile(d), tile(d), _const_out((8, d))],
        out_shape=[jax.ShapeDtypeStruct((s, nf), BF16), jax.ShapeDtypeStruct((s, nf), BF16),
                   jax.ShapeDtypeStruct((s, d), F32), jax.ShapeDtypeStruct((s, d), BF16),
                   jax.ShapeDtypeStruct((8, d), F32)],
        compiler_params=_params(("arbitrary",)),
    )(dx3, dx3b, a, x2, gffn, w1_g, w2_g)


def _xattn_mix_bwd(dx2, x1, xq, xk, xv, gx, slab_g, wo_g, tm, plan=None, plan_args=()):
    s, d = x1.shape
    m = xk.shape[0]
    rows = d // N_CHIPS
    e = XATTN_HEAD_DIM

    def body(dx2_ref, x1_ref, xq_ref, xk_ref, xv_ref, gx_ref, wout_ref, wq_ref, wo_ref,
             dx1_ref, dx1b_ref, dxq_ref, dmix_ref, dxk_ref, dxv_ref, st_ref):
        t = pl.program_id(0)
        dx2 = dx2_ref[...]
        dx2b = dx2.astype(BF16)
        datt = jnp.zeros((tm, d), F32)
        for j in range(N_CHIPS):
            datt = datt + _dot(dx2b[:, j * rows:(j + 1) * rows], wo_ref[j], NT)
        dattb = datt.astype(BF16)
        dxqs, dxks, dxvs = [], [], []
        for h in range(XATTN_HEADS):
            cs = slice(h * e, (h + 1) * e)
            xq_h, xk_h, xv_h = xq_ref[:, cs], xk_ref[:, cs], xv_ref[:, cs]
            p = _softmax_rows(_dot(xq_h, xk_h, NT) * (e ** -0.5))
            dp = _dot(dattb[:, cs], xv_h, NT)
            ds = (p * (dp - jnp.sum(dp * p, axis=-1, keepdims=True)) * (e ** -0.5)).astype(BF16)
            dxqs.append(_dot(ds, xk_h).astype(BF16))
            dxks.append(_dot(ds, xq_h, TN))
            dxvs.append(_dot(p.astype(BF16), dattb[:, cs], TN))
        dxq = jnp.concatenate(dxqs, axis=1)
        dxq_ref[...] = dxq
        dxk = jnp.concatenate(dxks, axis=1)
        dxv = jnp.concatenate(dxvs, axis=1)

        @pl.when(t == 0)
        def _():
            dxk_ref[...] = dxk
            dxv_ref[...] = dxv

        @pl.when(t > 0)
        def _():
            dxk_ref[...] = dxk_ref[...] + dxk
            dxv_ref[...] = dxv_ref[...] + dxv

        dhq = jnp.concatenate([_dot(dxq, wq_ref[j], NT) for j in range(N_CHIPS)], axis=1)
        g = gx_ref[...]
        _, n, r1 = _rms_fwd(x1_ref[...], g)
        dxn, dg = _rms_bwd(dhq, n, r1, g)
        dx1 = dx2 + dxn
        dx1_ref[...] = dx1
        dx1b = dx1.astype(BF16)
        dx1b_ref[...] = dx1b
        for j in range(N_CHIPS):
            dmix_ref[:, j * rows:(j + 1) * rows] = _dot(dx1b, wout_ref[j], NT)
        _acc_rows(st_ref, t, [dg])

    tile = lambda: pl.BlockSpec((tm, d), lambda t: (t, 0))
    blk = lambda k: pl.BlockSpec((N_CHIPS, rows, d), lambda t: (0, k, 0), pipeline_mode=pl.Buffered(1))
    x_specs, x_shapes, x_scratch = _plan_extras(plan)
    return pl.pallas_call(
        _fuse_exchange(body, 9, 7, 0, plan, 1), name="xattn_mix_bwd", grid=(s // tm,),
        in_specs=[tile(), tile(), tile(), _const((m, d)), _const((m, d)), _const((1, d)), blk(0), blk(1),
                  _const(wo_g.shape)] + x_specs,
        out_specs=[tile(), tile(), tile(), tile(), _const_out((m, d)), _const_out((m, d)), _const_out((8, d))]
                  + x_specs,
        out_shape=[jax.ShapeDtypeStruct((s, d), F32), jax.ShapeDtypeStruct((s, d), BF16),
                   jax.ShapeDtypeStruct((s, d), BF16), jax.ShapeDtypeStruct((s, d), F32),
                   jax.ShapeDtypeStruct((m, d), F32), jax.ShapeDtypeStruct((m, d), F32),
                   jax.ShapeDtypeStruct((8, d), F32)] + x_shapes,
        scratch_shapes=x_scratch,
        compiler_params=_params(("arbitrary",)),
    )(dx2, x1, xq, xk, xv, gx, slab_g, slab_g, wo_g, *plan_args)


def _kv_bwd(mem, g, dxk, dxv, slab_g):
    m, d = mem.shape
    rows = d // N_CHIPS

    def body(mem_ref, g_ref, dxk_ref, dxv_ref, wk_ref, wv_ref, dwk_ref, dwv_ref, st_ref):
        gv = g_ref[...]
        hm, n, _ = _rms_fwd(mem_ref[...], gv)
        hb = hm.astype(BF16)
        dkb = dxk_ref[...].astype(BF16)
        dvb = dxv_ref[...].astype(BF16)
        dhm = []
        for j in range(N_CHIPS):
            hj = hb[:, j * rows:(j + 1) * rows]
            dwk_ref[j] = _dot(hj, dkb, TN)
            dwv_ref[j] = _dot(hj, dvb, TN)
            dhm.append(_dot(dkb, wk_ref[j], NT) + _dot(dvb, wv_ref[j], NT))
        dg = jnp.sum(jnp.concatenate(dhm, axis=1) * n, axis=0, keepdims=True)
        st_ref[...] = jnp.concatenate([dg, jnp.zeros((7, d), F32)], axis=0)

    blk = lambda k: pl.BlockSpec((N_CHIPS, rows, d), lambda i: (0, k, 0))
    return pl.pallas_call(
        body, name="kv_bwd", grid=(1,),
        in_specs=[_const((m, d)), _const((1, d)), _const((m, d)), _const((m, d)), blk(0), blk(1)],
        out_specs=[_const_out((N_CHIPS, rows, d)), _const_out((N_CHIPS, rows, d)), _const_out((8, d))],
        out_shape=[jax.ShapeDtypeStruct((N_CHIPS, rows, d), F32)] * 2 + [jax.ShapeDtypeStruct((8, d), F32)],
        compiler_params=_params(("arbitrary",)),
    )(mem, g, dxk, dxv, slab_g, slab_g)


def _pool_bwd(z, dmix, wp, scale, tm, plan=None, plan_args=()):
    s = z.shape[0]
    ng = len(POOL_WINDOWS)
    pw = ng * HEAD_DIM
    nb = tm // POOL_HALO
    nt = s // tm
    n_ext = tm + POOL_HALO

    def body(p_ref, prev_ref, dm_ref, dmn_ref, wp_ref, sc_ref, dp_ref, dwp_ref, st_ref):
        t = pl.program_id(0)
        p = p_ref[...]
        prev = jnp.where(t > 0, prev_ref[...], 0.0)
        pooled = _pooled(p, jnp.concatenate([prev, p], axis=0), t * tm)
        dm = dm_ref[...]
        dme = jnp.concatenate([dm, jnp.where(t < nt - 1, dmn_ref[...], 0.0)], axis=0) * sc_ref[...]
        tok = t * tm + lax.broadcasted_iota(jnp.int32, (n_ext, 1), 0)
        dsc, dps, dwps = [], [], []
        for g, w in enumerate(POOL_WINDOWS):
            cs = slice(g * HEAD_DIM, (g + 1) * HEAD_DIM)
            wpb = wp_ref[g].astype(BF16)
            pb = pooled[g].astype(BF16)
            dsc.append(jnp.sum(dm[:, cs] * _dot(pb, wpb), axis=0, keepdims=True))
            dye = dme[:, cs].astype(BF16)
            dwps.append(_dot(pb, dye[:tm], TN))
            dpe = _dot(dye, wpb, NT)
            acc = dpe / jnp.minimum(tok + 1, w).astype(F32)
            sh = 1
            while sh < w:
                acc = acc + pltpu.roll(acc, n_ext - sh, axis=0)
                sh *= 2
            dps.append(acc[:tm] - dpe[:tm])
        dp_ref[...] = jnp.concatenate(dps, axis=1)
        dsc_row = jnp.concatenate(dsc, axis=1)

        @pl.when(t == 0)
        def _():
            for g in range(ng):
                dwp_ref[g] = dwps[g]

        @pl.when(t > 0)
        def _():
            for g in range(ng):
                dwp_ref[g] = dwp_ref[g] + dwps[g]

        _acc_rows(st_ref, t, [dsc_row])

    x_specs, x_shapes, x_scratch = _plan_extras(plan)
    return pl.pallas_call(
        _fuse_exchange(body, 6, 3, 0, plan, 1), name="pool_bwd", grid=(nt,),
        in_specs=[pl.BlockSpec((tm, pw), lambda t: (t, 4)),
                  pl.BlockSpec((POOL_HALO, pw), lambda t: (jnp.maximum(t * nb - 1, 0), 4)),
                  pl.BlockSpec((tm, pw), lambda t: (t, 1)),
                  pl.BlockSpec((POOL_HALO, pw), lambda t: (jnp.minimum((t + 1) * nb, s // POOL_HALO - 1), 1)),
                  _const(wp.shape), _const((1, pw))] + x_specs,
        out_specs=[pl.BlockSpec((tm, pw), lambda t: (t, 0)), _const_out(wp.shape), _const_out((8, pw))] + x_specs,
        out_shape=[jax.ShapeDtypeStruct((s, pw), F32), jax.ShapeDtypeStruct(wp.shape, F32),
                   jax.ShapeDtypeStruct((8, pw), F32)] + x_shapes,
        scratch_shapes=x_scratch,
        compiler_params=_params(("arbitrary",)),
    )(z, z, dmix, dmix, wp, scale, *plan_args)


def _hgrn_bwd(z, o, dmix, st, l0, l1, gn, tc, unroll=1, plan=None, plan_args=()):
    s = z.shape[0]
    nsub = tc // SUB
    nt = s // tc
    hd = HEAD_DIM
    hp = HEADS_PER_STEP

    def body(q_ref, f_ref, v_ref, g_ref, l0_ref, l1_ref, gn_ref, o_ref, dm_ref, st_ref,
             tril_ref, triu_ref, trilc_ref, triuc_ref,
             dq_ref, df_ref, di_ref, dg_ref, stat_ref, dstate, qs, ks, bs, dos, dqs, dks, dbs, sts):
        t = pl.program_id(1)

        @pl.when(t == 0)
        def _():
            dstate[...] = jnp.zeros_like(dstate)

        cols = [slice(hh * hd, (hh + 1) * hd) for hh in range(hp)]
        heads = []
        for hh, cs in enumerate(cols):
            lb = _lower_bound(l0_ref[hh], l1_ref[hh])
            qp = q_ref[:, cs]
            q, sq, f, sf = _hgrn_gates(qp, f_ref[:, cs], lb)
            lf = jnp.log(f) * LOG2E
            o = o_ref[:, cs]
            r = lax.rsqrt(jnp.mean(o * o, axis=-1, keepdims=True) + EPS)
            n = o * r
            gnv = gn_ref[hh]
            gp = g_ref[:, cs]
            sg = _sigmoid(gp)
            dm = dm_ref[:, cs]
            dg_ref[:, cs] = dm * (n * gnv) * (sg * (1.0 + gp * (1.0 - sg)))
            don = dm * (gp * sg)
            dn = don * gnv
            heads.append(dict(lb=lb, qp=qp, q=q, sq=sq, f=f, sf=sf, k=1.0 - f, lf=lf,
                              bc=_group_cumsum(trilc_ref[...], lf), dgn=jnp.sum(don * n, axis=0, keepdims=True),
                              do=r * (dn - n * jnp.mean(dn * n, axis=-1, keepdims=True))))
        bounded = functools.reduce(jnp.minimum, [jnp.min(h["bc"]) for h in heads]) >= -MAX_LOG2_GROWTH

        def factored(hh, cs, q, k, bc, do_all):
            eb = jnp.exp2(bc)
            eib = jnp.exp2(-bc)
            qt = (q * eb).astype(BF16)
            ki = (k * eib).astype(BF16)
            vb = v_ref[:, cs].astype(BF16)
            dob = do_all.astype(BF16)
            mask = trilc_ref[...] > 0
            a = jnp.where(mask, _dot(qt, ki, NT), 0.0).astype(BF16)
            da = jnp.where(mask, _dot(dob, vb, NT), 0.0).astype(BF16)
            dq_in = _dot(da, ki)
            dk_in = _dot(da, qt, TN)
            dv_in = _dot(a, dob, TN)
            last_row = lax.broadcasted_iota(jnp.int32, (CHUNK, 1), 0) == CHUNK - 1
            for c in reversed(range(tc // CHUNK)):
                rs = slice(c * CHUNK, (c + 1) * CHUNK)
                stp = st_ref[hh, c]
                dst = dstate[hh]
                dstb = dst.astype(BF16)
                bl = bc[(c + 1) * CHUNK - 1:(c + 1) * CHUNK, :]
                ekl = jnp.exp2(bl - bc[rs])
                ebl = jnp.exp2(bl)
                kt = k[rs] * ekl
                dq_st = _dot(dob[rs], stp.astype(BF16)) * eb[rs]
                dkt = _dot(vb[rs], dstb)
                extra = jnp.sum(kt * dkt, axis=0, keepdims=True) + ebl * jnp.sum(stp * dst, axis=0, keepdims=True)
                dqs[rs, cs] = dq_st + dq_in[rs] * eb[rs]
                dks[rs, cs] = dkt * ekl + dk_in[rs] * eib[rs]
                di_ref[rs, cs] = _dot(kt.astype(BF16), dstb, NT) + dv_in[rs]
                dbs[rs, cs] = (q[rs] * dq_st - kt * dkt + jnp.where(last_row, extra, 0.0)
                               + (qt[rs].astype(F32) * dq_in[rs] - ki[rs].astype(F32) * dk_in[rs]))
                dstate[hh] = dst * ebl + _dot(dob[rs], qt[rs], TN)
            dbs[:, cs] = _group_cumsum(triuc_ref[...], dbs[:, cs])

        def exact(hh, cs, q, k, lf, do_all):
            qs[:, cs] = q
            ks[:, cs] = k
            bs[:, cs] = _group_cumsum(tril_ref[...], lf)
            dos[:, cs] = do_all
            per = CHUNK // SUB

            def restore(i, carry):
                @pl.when(i % per == 0)
                def _():
                    sts[i] = st_ref[hh, i // per]

                @pl.when(i % per != 0)
                def _():
                    rp = pl.multiple_of((i - 1) * SUB, SUB)
                    b_ = bs[pl.ds(rp, SUB), cs]
                    bl = b_[SUB - 1:SUB, :]
                    kt = (ks[pl.ds(rp, SUB), cs] * jnp.exp2(bl - b_)).astype(BF16)
                    sts[i] = sts[i - 1] * jnp.exp2(bl) + _dot(v_ref[pl.ds(rp, SUB), cs].astype(BF16), kt, TN)

                return carry

            lax.fori_loop(0, nsub, restore, 0)
            rows = lax.broadcasted_iota(jnp.int32, (HALF, 1), 0)
            last_row = lax.broadcasted_iota(jnp.int32, (SUB, 1), 0) == SUB - 1

            def step(i, carry):
                ii = nsub - 1 - i
                r0 = pl.multiple_of(ii * SUB, SUB)
                q_ = qs[pl.ds(r0, SUB), cs]
                k_ = ks[pl.ds(r0, SUB), cs]
                b_ = bs[pl.ds(r0, SUB), cs]
                v_ = v_ref[pl.ds(r0, SUB), cs]
                do_ = dos[pl.ds(r0, SUB), cs]
                stp = sts[ii]
                dst = dstate[hh]
                bl = b_[SUB - 1:SUB, :]
                eb = jnp.exp2(b_)
                ekl = jnp.exp2(bl - b_)
                ebl = jnp.exp2(bl)
                dob = do_.astype(BF16)
                dstb = dst.astype(BF16)
                kt = k_ * ekl
                dq = _dot(dob, stp.astype(BF16)) * eb
                dkt = _dot(v_.astype(BF16), dstb)
                dk = dkt * ekl
                dv = _dot(kt.astype(BF16), dstb, NT)
                extra = jnp.sum(kt * dkt, axis=0, keepdims=True) + ebl * jnp.sum(stp * dst, axis=0, keepdims=True)
                halves = lambda x: [x[:HALF], x[HALF:]]
                q_h, b_h, do_h, dq_h, dk_h, dv_h = (halves(x) for x in (q_, b_, do_, dq, dk, dv))
                for own in range(2):
                    dk_rows, dv_rows = _RowSums(rows), _RowSums(rows)
                    for jj in _RowSums.ORDER:
                        j = own * HALF + jj
                        bj, kj, vj = b_[j:j + 1, :], k_[j:j + 1, :], v_[j:j + 1, :]
                        dk_sum = dv_sum = None
                        for h in range(own, 2):
                            e = _decay(b_h[h], bj, rows, jj if h == own else None)
                            pe = q_h[h] * e
                            acol = jnp.sum(pe * kj, axis=-1, keepdims=True)
                            dacol = jnp.sum(do_h[h] * vj, axis=-1, keepdims=True)
                            dq_h[h] = dq_h[h] + dacol * (e * kj)
                            dk_sum = dacol * pe if dk_sum is None else dk_sum + dacol * pe
                            dv_sum = acol * do_h[h] if dv_sum is None else dv_sum + acol * do_h[h]
                        dk_rows.push(jj, dk_sum)
                        dv_rows.push(jj, dv_sum)
                    dk_h[own] = dk_h[own] + dk_rows.result()
                    dv_h[own] = dv_h[own] + dv_rows.result()
                dq, dk, dv = (jnp.concatenate(x, axis=0) for x in (dq_h, dk_h, dv_h))
                dqs[pl.ds(r0, SUB), cs] = dq
                dks[pl.ds(r0, SUB), cs] = dk
                di_ref[pl.ds(r0, SUB), cs] = dv
                dbs[pl.ds(r0, SUB), cs] = q_ * dq - k_ * dk + jnp.where(last_row, extra, 0.0)
                dstate[hh] = dst * ebl + _dot(dob, (q_ * eb).astype(BF16), TN)
                return carry

            lax.fori_loop(0, nsub, step, 0, unroll=unroll)
            dbs[:, cs] = _group_cumsum(triu_ref[...], dbs[:, cs])

        @pl.when(bounded)
        def _():
            for hh, cs in enumerate(cols):
                factored(hh, cs, heads[hh]["q"], heads[hh]["k"], heads[hh]["bc"], heads[hh]["do"])

        @pl.when(jnp.logical_not(bounded))
        def _():
            for hh, cs in enumerate(cols):
                exact(hh, cs, heads[hh]["q"], heads[hh]["k"], heads[hh]["lf"], heads[hh]["do"])

        for hh, cs in enumerate(cols):
            h = heads[hh]
            dfv = dbs[:, cs] / h["f"] - dks[:, cs]
            df_ref[:, cs] = dfv * (1.0 - h["lb"]) * h["sf"] * (1.0 - h["sf"])
            dlb = jnp.sum(dfv * (1.0 - h["sf"]), axis=0, keepdims=True)
            dq_ref[:, cs] = dqs[:, cs] * (h["sq"] * (1.0 + h["qp"] * (1.0 - h["sq"])))
            _acc_rows(stat_ref.at[hh], t, [h["dgn"], dlb])

    rev = lambda t: nt - 1 - t
    ng = HGRN_HEADS // hp
    col = lambda k: pl.BlockSpec((tc, hp * hd), lambda h, t: (rev(t), k * ng + h))
    vec = pl.BlockSpec((hp, 1, hd), lambda h, t: (h, 0, 0))
    head = pl.BlockSpec((tc, hp * hd), lambda h, t: (rev(t), h))
    x_specs, x_shapes, x_scratch = _plan_extras(plan)
    return pl.pallas_call(
        _fuse_exchange(body, 14, 5, 9, plan, 2), name="hgrn_bwd", grid=(ng, nt),
        in_specs=[col(0), col(1), col(2), col(3), vec, vec, vec, head, head,
                  pl.BlockSpec((hp, tc // CHUNK, hd, hd), lambda h, t: (h, rev(t), 0, 0))]
                 + [_const((tc, tc))] * 4 + x_specs,
        out_specs=[head, head, head, head, pl.BlockSpec((hp, 8, hd), lambda h, t: (h, 0, 0))] + x_specs,
        out_shape=[jax.ShapeDtypeStruct((s, HGRN_WIDTH), F32)] * 4 + [jax.ShapeDtypeStruct((HGRN_HEADS, 8, hd), F32)]
                  + x_shapes,
        scratch_shapes=[pltpu.VMEM((hp, hd, hd), F32)] + [pltpu.VMEM((tc, hp * hd), F32)] * 7
                       + [pltpu.VMEM((nsub, hd, hd), F32)] + x_scratch,
        compiler_params=_params(("arbitrary", "arbitrary")),
    )(z, z, z, z, l0, l1, gn, o, dmix, st, _block_tri(tc, SUB, False), _block_tri(tc, SUB, True),
      _block_tri(tc, CHUNK, False), _block_tri(tc, CHUNK, True), *plan_args)


def _in_bwd(dparts, dx1, x, g, win_g, tm, plan=None, plan_args=()):
    s, d = x.shape
    nsh, _, wc = win_g.shape
    pw = dparts[0].shape[1]

    def body(dq_ref, df_ref, di_ref, dg_ref, dp_ref, dx1_ref, x_ref, g_ref, w_ref, gx_ref, dz_ref, st_ref):
        dz = jnp.concatenate([dq_ref[...], df_ref[...], di_ref[...], dg_ref[...], dp_ref[...]], axis=1).astype(BF16)
        dz_ref[...] = dz
        dh = jnp.zeros((tm, d), F32)
        for j in range(nsh):
            dh = dh + _dot(dz[:, j * wc:(j + 1) * wc], w_ref[j], NT)
        gv = g_ref[...]
        _, n, r = _rms_fwd(x_ref[...], gv)
        dxn, dg = _rms_bwd(dh, n, r, gv)
        gx_ref[...] = dx1_ref[...] + dxn
        _acc_rows(st_ref, pl.program_id(0), [dg])

    tile = lambda w: pl.BlockSpec((tm, w), lambda t: (t, 0))
    x_specs, x_shapes, x_scratch = _plan_extras(plan)
    return pl.pallas_call(
        _fuse_exchange(body, 9, 3, 0, plan, 1), name="in_bwd", grid=(s // tm,),
        in_specs=[tile(pw)] * 5 + [tile(d), tile(d), _const((1, d)), _const(win_g.shape)] + x_specs,
        out_specs=[tile(d), tile(nsh * wc), _const_out((8, d))] + x_specs,
        out_shape=[jax.ShapeDtypeStruct((s, d), F32), jax.ShapeDtypeStruct((s, nsh * wc), BF16),
                   jax.ShapeDtypeStruct((8, d), F32)] + x_shapes,
        scratch_shapes=x_scratch,
        compiler_params=_params(("arbitrary",)),
    )(*dparts, dx1, x, g, win_g, *plan_args)


def _tn_grad(name, a, b, out_rows, out_cols, a_sharded, plan=None, plan_args=()):
    s = a.shape[0]
    tr, tc = min(out_rows, GRAD_TILE), min(out_cols, GRAD_TILE)
    nr, nc = out_rows // tr, out_cols // tc

    def body(a_ref, b_ref, o_ref):
        o_ref[...] = _dot(a_ref[...], b_ref[...], TN)

    a_map = (lambda j, i, k: (0, j * nr + i)) if a_sharded else (lambda j, i, k: (0, i))
    b_map = (lambda j, i, k: (0, k)) if a_sharded else (lambda j, i, k: (0, j * nc + k))
    x_specs, x_shapes, x_scratch = _plan_extras(plan)
    res = pl.pallas_call(
        _fuse_exchange(body, 2, 1, 0, plan, 3), name=name, grid=(N_CHIPS, nr, nc),
        in_specs=[pl.BlockSpec((s, tr), a_map), pl.BlockSpec((s, tc), b_map)] + x_specs,
        out_specs=[pl.BlockSpec((None, tr, tc), lambda j, i, k: (j, i, k))] + x_specs,
        out_shape=[jax.ShapeDtypeStruct((N_CHIPS, out_rows, out_cols), F32)] + x_shapes,
        scratch_shapes=x_scratch,
        compiler_params=_params(("arbitrary", "arbitrary", "arbitrary")),
    )(a, b, *plan_args)
    return res if plan else res[0]


FFN_NAMES = ("w_ff1", "w_ff2")
ATTN_NAMES = ("w_xo", "w_xq", "w_out", "w_xk", "w_xv")
EARLY_NAMES = FFN_NAMES + ATTN_NAMES
BIG_NAMES = EARLY_NAMES + ("w_in",)


def _halved(g):
    return g.reshape(N_CHIPS, 2, g.shape[1] // 2, g.shape[2])


def _pair_adds(names, gs, got, idx):
    pairs = [_grad_pair_add("grad_pair_add_" + k, g, r, idx, tr=min(256, g.shape[2])) for k, g, r in zip(names, gs, got)]
    return [p[0] for p in pairs], [p[1] for p in pairs]


def _step(x, mem, target, small, shards, idx):
    d = x.shape[1]
    l0 = small["lb_logits"][0].reshape(HGRN_HEADS, 1, HEAD_DIM)
    l1 = small["lb_logits"][1].reshape(HGRN_HEADS, 1, HEAD_DIM)
    gn = small["hgrn_norm_g"].reshape(HGRN_HEADS, 1, HEAD_DIM)
    wp = small["w_pool"].reshape(len(POOL_WINDOWS), HEAD_DIM, HEAD_DIM)
    psc = small["pool_scale"].reshape(1, -1)
    gmix, gx, gmem, gffn = (small[k].reshape(1, d) for k in ("norm_mix_g", "norm_x_g", "norm_mem_g", "norm_ffn_g"))
    gfin = small["final_norm_g"].reshape(1, d)

    (win_g,) = _run_exchange("gather_w_in", _WeightGather([shards["w_in"]]), [shards["w_in"]])
    z, h, kv_g = _in_proj(x, gmix, win_g, tm=TOKEN_TILE, plan=_WeightGather([shards["slab_kv"]]),
                          plan_args=[shards["slab_kv"]])
    mid_w = [shards["slab_oq"], shards["w_xo"], shards["w_ff1"]]
    o, oa, st, oq_g, wo_g, w1_g = _hgrn_fwd(z, l0, l1, gn, tc=HGRN_BLOCK, unroll=8,
                                            plan=_WeightGather(mid_w), plan_args=mid_w)
    ob = _pool_fwd(z, wp, psc, tm=TOKEN_TILE)
    xk, xv = _kv_proj(mem, gmem, kv_g)
    late_w = [shards["w_ff2"]]
    x1, mixed, hq, xq, att, x2, w2_g = _mix_xattn_fwd(x, oa, ob, gx, oq_g, wo_g, xk, xv, tm=TOKEN_TILE,
                                                      plan=_WeightGather(late_w), plan_args=late_w)
    a, hf, dx3, dx3b, st_loss = _mlp_loss_fwd(x2, gffn, gfin, w1_g, w2_g, target, tm=TOKEN_TILE)

    da, u, dx2, dx2b, st_ffn = _mlp_bwd(dx3, dx3b, a, x2, gffn, w1_g, w2_g, tm=MLP_BWD_TOKEN_TILE)
    g_ff1 = [_halved(_tn_grad("dw_ff1", hf, da, d, d, False))]
    dw_ff2, *got = _tn_grad("dw_ff2", u, dx3b, d, d, True, plan=_PairExchange(g_ff1), plan_args=g_ff1)
    keep_ff1, send_ff1 = _pair_adds(("w_ff1",), g_ff1, got, idx)
    g_ff2 = [_halved(dw_ff2)]
    dx1, dx1b, dxq, dmix, dxk, dxv, st_x, *got = _xattn_mix_bwd(
        dx2, x1, xq, xk, xv, gx, oq_g, wo_g, tm=TOKEN_TILE,
        plan=_Plans([_ChipExchange(send_ff1), _PairExchange(g_ff2)]), plan_args=send_ff1 + g_ff2)
    recv_ff1 = got[:1]
    keep_ff2, send_ff2 = _pair_adds(("w_ff2",), g_ff2, got[1:], idx)
    dw = {}
    dw["w_xo"] = _tn_grad("dw_xo", att, dx2b, d, d // N_CHIPS, False)
    dw["w_xq"] = _tn_grad("dw_xq", hq, dxq, d // N_CHIPS, d, True)
    dw["w_out"] = _tn_grad("dw_out", mixed, dx1b, d // N_CHIPS, d, True)
    dw["w_xk"], dw["w_xv"], st_mem = _kv_bwd(mem, gmem, dxk, dxv, kv_g)
    gs_attn = [_halved(dw[k]) for k in ATTN_NAMES]
    dp, d_wp, st_pool, *got_attn = _pool_bwd(z, dmix, wp, psc, tm=TOKEN_TILE, plan=_PairExchange(gs_attn), plan_args=gs_attn)
    keep_attn, send_attn = _pair_adds(ATTN_NAMES, gs_attn, got_attn, idx)
    sends = send_ff2 + send_attn
    dq, df, di, dg, st_hgrn, *received = _hgrn_bwd(z, o, dmix, st, l0, l1, gn, tc=HGRN_BLOCK, unroll=4,
                                                    plan=_ChipExchange(sends), plan_args=sends)
    grad_x, dz, st_mix = _in_bwd([dq, df, di, dg, dp], dx1, x, gmix, win_g, tm=TOKEN_TILE)
    keeps = keep_ff1 + keep_ff2 + keep_attn
    received = recv_ff1 + list(received)
    gs_in = [_halved(_tn_grad("dw_in", h, dz, d, win_g.shape[2], False))]
    got_in = _run_exchange("grad_pair_exchange_w_in", _PairExchange(gs_in), gs_in)
    keep_in, send_in = _pair_adds(("w_in",), gs_in, got_in, idx)

    partials = dict(zip(EARLY_NAMES, zip(keeps, received)))
    stats = dict(mix=st_mix, x=st_x, mem=st_mem, ffn=st_ffn, loss=st_loss, hgrn=st_hgrn, pool=st_pool)
    return grad_x, stats, d_wp, partials, keep_in, send_in


def _place():
    x, y, c = lax.axis_index("x"), lax.axis_index("y"), lax.axis_index("c")
    return x, y, c, [(x, 1 - y), (1 - x, y), (1 - x, 1 - y)]


def _rcopy(src, dst, ssem, rsem, dev):
    return pltpu.make_async_remote_copy(src_ref=src, dst_ref=dst, send_sem=ssem, recv_sem=rsem,
                                        device_id=dev, device_id_type=MESH)


class _WeightGather:
    def __init__(self, shards):
        self.n = len(shards)
        self.rows = [w.shape[0] for w in shards]
        self.out_shape = [jax.ShapeDtypeStruct((N_CHIPS,) + w.shape, w.dtype) for w in shards]
        dma = pltpu.SemaphoreType.DMA
        self.scratch_shapes = [dma((self.n,))] * 2 + [dma((self.n, 2))] * 4 + [dma((self.n, 3))] * 2

    def _copies(self, ins, outs, sems, phase):
        osem, orsem, ssem, rsem, qsem, qrsem, fsem, frsem = sems
        x, y, c, _ = _place()
        chip = 2 * x + y
        sib = (x, y, 1 - c)
        nbrs = [(x, 1 - y), (1 - x, y)]
        diag = 2 * (1 - x) + (1 - y)
        cps = []
        if phase in ("start", "finish"):
            cps += [("own", _rcopy(ins[a], outs[a].at[chip], osem.at[a], orsem.at[a], sib)) for a in range(self.n)]
        for a in range(self.n):
            hr = self.rows[a] // 2
            qr = hr // 2
            half = lambda who, hc, a=a, hr=hr: outs[a].at[who, pl.ds(hc * hr, hr), :]
            quarter = lambda who, k, a=a, hr=hr, qr=qr: outs[a].at[who, pl.ds(c * hr + k * qr, qr), :]
            for r, (px, py) in enumerate(nbrs):
                pc = 2 * px + py
                ox, oy = nbrs[1 - r]
                if phase in ("start", "finish"):
                    cps.append(("direct", _rcopy(ins[a].at[pl.ds(c * hr, hr), :], half(chip, c), ssem.at[a, r],
                                                 rsem.at[a, r], (px, py, c))))
                if phase == "middle":
                    cps.append(("arrived", _rcopy(half(pc, c), half(pc, c), ssem.at[a, r], rsem.at[a, r], (px, py, c))))
                if phase in ("middle", "finish"):
                    cps.append(("relay", _rcopy(quarter(pc, r), quarter(pc, r), qsem.at[a, r], qrsem.at[a, r],
                                                (ox, oy, c))))
                    cps.append(("pass", _rcopy(half(pc, c), half(pc, c), fsem.at[a, r], frsem.at[a, r], sib)))
                if phase == "finish":
                    cps.append(("relayed", _rcopy(quarter(diag, r), quarter(diag, r), qsem.at[a, r], qrsem.at[a, r],
                                                  (ox, oy, c))))
                    cps.append(("pass_in", _rcopy(half(pc, 1 - c), half(pc, 1 - c), fsem.at[a, r], frsem.at[a, r], sib)))
            if phase == "finish":
                cps.append(("pass_diag", _rcopy(half(diag, c), half(diag, c), fsem.at[a, 2], frsem.at[a, 2], sib)))
                cps.append(("pass_in", _rcopy(half(diag, 1 - c), half(diag, 1 - c), fsem.at[a, 2], frsem.at[a, 2], sib)))
        return cps

    def start(self, ins, outs, sems):
        for _, cp in self._copies(ins, outs, sems, "start"):
            cp.start()

    def middle(self, ins, outs, sems):
        for kind, cp in self._copies(ins, outs, sems, "middle"):
            if kind == "arrived":
                cp.wait_recv()
            else:
                cp.start()

    def finish(self, ins, outs, sems):
        cps = self._copies(ins, outs, sems, "finish")
        for kind, cp in cps:
            if kind == "relayed":
                cp.wait_recv()
        for kind, cp in cps:
            if kind == "pass_diag":
                cp.start()
        for kind, cp in cps:
            if kind == "pass_in":
                cp.wait_recv()
        for kind, cp in cps:
            if kind in ("direct", "relay", "pass", "pass_diag"):
                cp.wait_send()
        for kind, cp in cps:
            if kind == "own":
                cp.wait()


class _ChipExchange:
    def __init__(self, sends):
        self.n = len(sends)
        self.out_shape = [jax.ShapeDtypeStruct(g.shape, g.dtype) for g in sends]
        self.scratch_shapes = [pltpu.SemaphoreType.DMA((self.n, 3))] * 2

    def _copies(self, ins, outs, sems):
        ssem, rsem = sems
        _, _, c, peers = _place()
        return [_rcopy(ins[a].at[r], outs[a].at[r], ssem.at[a, r], rsem.at[a, r], (px, py, c))
                for a in range(self.n) for r, (px, py) in enumerate(peers)]

    def start(self, ins, outs, sems):
        for cp in self._copies(ins, outs, sems):
            cp.start()

    def finish(self, ins, outs, sems):
        for cp in self._copies(ins, outs, sems):
            cp.wait()


class _Plans:
    def __init__(self, plans):
        self.plans = plans
        self.n = sum(p.n for p in plans)
        self.out_shape = [s for p in plans for s in p.out_shape]
        self.scratch_shapes = [s for p in plans for s in p.scratch_shapes]

    def _each(self, ins, outs, sems):
        a = b = 0
        for p in self.plans:
            ns = len(p.scratch_shapes)
            yield p, ins[a:a + p.n], outs[a:a + p.n], sems[b:b + ns]
            a, b = a + p.n, b + ns

    def start(self, ins, outs, sems):
        for p, i, o, s in self._each(ins, outs, sems):
            p.start(i, o, s)

    def finish(self, ins, outs, sems):
        for p, i, o, s in self._each(ins, outs, sems):
            p.finish(i, o, s)


def _run_exchange(name, plan, arrays):
    n = plan.n

    def body(*refs):
        ins, outs, sems = refs[:n], refs[n:2 * n], refs[2 * n:]
        plan.start(ins, outs, sems)
        if hasattr(plan, "middle"):
            plan.middle(ins, outs, sems)
        plan.finish(ins, outs, sems)

    return pl.pallas_call(
        body, name=name, in_specs=[ANY] * n, out_specs=[ANY] * n,
        out_shape=plan.out_shape, scratch_shapes=plan.scratch_shapes,
    )(*arrays)


class _PairExchange:
    def __init__(self, gs):
        self.n = len(gs)
        self.out_shape = [jax.ShapeDtypeStruct((g.shape[0],) + g.shape[2:], g.dtype) for g in gs]
        self.scratch_shapes = [pltpu.SemaphoreType.DMA((self.n,))] * 2

    def _copies(self, ins, outs, sems):
        ssem, rsem = sems
        x, y, c, _ = _place()
        return [_rcopy(ins[a].at[:, 1 - c], outs[a], ssem.at[a], rsem.at[a], (x, y, 1 - c)) for a in range(self.n)]

    def start(self, ins, outs, sems):
        for cp in self._copies(ins, outs, sems):
            cp.start()

    def finish(self, ins, outs, sems):
        for cp in self._copies(ins, outs, sems):
            cp.wait()


def _grad_pair_add(name, g, got, idx, tr):
    _, _, hr, cc = g.shape

    def body(idx_ref, g0, g1, g2, g3, r0, r1, r2, r3, keep_ref, send_ref):
        keep_ref[...] = g0[...] + r0[...]
        for q, (gq, rq) in enumerate(((g1, r1), (g2, r2), (g3, r3))):
            send_ref[q] = (gq[...] + rq[...]).astype(BF16)

    gspec = lambda q: pl.BlockSpec((None, None, tr, cc), lambda i, idx: (idx[1 + q], idx[0], i, 0))
    rspec = lambda q: pl.BlockSpec((None, tr, cc), lambda i, idx: (idx[1 + q], i, 0))
    return pl.pallas_call(
        body, name=name,
        grid_spec=pltpu.PrefetchScalarGridSpec(
            num_scalar_prefetch=1, grid=(hr // tr,),
            in_specs=[gspec(q) for q in range(4)] + [rspec(q) for q in range(4)],
            out_specs=[pl.BlockSpec((tr, cc), lambda i, idx: (i, 0)), pl.BlockSpec((3, tr, cc), lambda i, idx: (0, i, 0))]),
        out_shape=[jax.ShapeDtypeStruct((hr, cc), F32), jax.ShapeDtypeStruct((3, hr, cc), BF16)],
        compiler_params=_params(("parallel",)),
    )(idx, g, g, g, g, got, got, got, got)


def _grad_chip_add(name, keep, got, tr):
    hr, cc = keep.shape

    def body(k_ref, g_ref, o_ref):
        o_ref[...] = ((k_ref[...] + g_ref[0].astype(F32)) + g_ref[1].astype(F32)) + g_ref[2].astype(F32)

    return pl.pallas_call(
        body, name=name, grid=(hr // tr,),
        in_specs=[pl.BlockSpec((tr, cc), lambda i: (i, 0)), pl.BlockSpec((3, tr, cc), lambda i: (0, i, 0))],
        out_specs=pl.BlockSpec((tr, cc), lambda i: (i, 0)),
        out_shape=jax.ShapeDtypeStruct((hr, cc), F32),
        compiler_params=_params(("parallel",)),
    )(keep, got)


class _HalfExchange:
    def __init__(self, ts):
        self.n = len(ts)
        self.out_shape = [jax.ShapeDtypeStruct(t.shape, t.dtype) for t in ts]
        self.scratch_shapes = [pltpu.SemaphoreType.DMA((self.n,))] * 2

    def _copies(self, ins, outs, sems):
        ssem, rsem = sems
        x, y, c, _ = _place()
        return [_rcopy(ins[a], outs[a], ssem.at[a], rsem.at[a], (x, y, 1 - c)) for a in range(self.n)]

    def start(self, ins, outs, sems):
        for cp in self._copies(ins, outs, sems):
            cp.start()

    def finish(self, ins, outs, sems):
        for cp in self._copies(ins, outs, sems):
            cp.wait()


def _small_allreduce(stats, d_wp, plan, plan_args):
    d = D_MODEL
    half = d // 2
    wps = d_wp.shape
    n = plan.n

    def body(mix_ref, x_ref, mem_ref, ffn_ref, loss_ref, hg_ref, pool_ref, wp_ref, *refs):
        cin, (slab_out, wp_out), cout = refs[:n], refs[n:n + 2], refs[n + 2:2 * n + 2]
        slab_buf, wp_buf, sib_s, sib_w, ssem, rsem = refs[2 * n + 2:2 * n + 8]
        csem = refs[2 * n + 8:]
        plan.start(cin, cout, csem)
        x, y, c, peers = _place()
        chip = 2 * x + y
        sib = (x, y, 1 - c)
        hgn = jnp.concatenate([hg_ref[h, 0:1, :] for h in range(HGRN_HEADS)], axis=1)
        dlb = jnp.concatenate([hg_ref[h, 1:2, :] for h in range(HGRN_HEADS)], axis=1)
        slab_buf[0] = jnp.concatenate([
            mix_ref[0:1, :], x_ref[0:1, :], mem_ref[0:1, :], ffn_ref[0:1, :], loss_ref[0:1, :],
            jnp.concatenate([dlb, hgn], axis=1),
            jnp.concatenate([pool_ref[0:1, :], jnp.zeros((1, half), F32)], axis=1),
            loss_ref[1:2, :]], axis=0)
        wp_buf[0] = wp_ref[...]
        pair = [_rcopy(slab_buf.at[0], sib_s, ssem.at[0], rsem.at[0], sib),
                _rcopy(wp_buf.at[0], sib_w, ssem.at[1], rsem.at[1], sib)]
        for cp in pair:
            cp.start()
        for cp in pair:
            cp.wait()
        slab_buf[0] = slab_buf[0] + sib_s[...]
        wp_buf[0] = wp_buf[0] + sib_w[...]
        cps = []
        for r, (px, py) in enumerate(peers):
            cps.append(_rcopy(slab_buf.at[0], slab_buf.at[r + 1], ssem.at[2 + 2 * r], rsem.at[2 + 2 * r], (px, py, c)))
            cps.append(_rcopy(wp_buf.at[0], wp_buf.at[r + 1], ssem.at[3 + 2 * r], rsem.at[3 + 2 * r], (px, py, c)))
        for cp in cps:
            cp.start()
        for cp in cps:
            cp.wait()
        tot_s, tot_w = slab_buf[chip], wp_buf[chip]
        for j in range(1, N_CHIPS):
            tot_s = tot_s + slab_buf[jnp.bitwise_xor(j, chip)]
            tot_w = tot_w + wp_buf[jnp.bitwise_xor(j, chip)]
        slab_out[...] = tot_s
        wp_out[...] = tot_w
        plan.finish(cin, cout, csem)

    return pl.pallas_call(
        body, name="small_allreduce",
        in_specs=[VMEM] * 8 + [ANY] * n, out_specs=[VMEM] * 2 + [ANY] * n,
        out_shape=[jax.ShapeDtypeStruct((8, d), F32), jax.ShapeDtypeStruct(wps, F32)] + list(plan.out_shape),
        scratch_shapes=[pltpu.VMEM((N_CHIPS, 8, d), F32), pltpu.VMEM((N_CHIPS,) + wps, F32),
                        pltpu.VMEM((8, d), F32), pltpu.VMEM(wps, F32),
                        pltpu.SemaphoreType.DMA((8,)), pltpu.SemaphoreType.DMA((8,))] + list(plan.scratch_shapes),
    )(stats["mix"], stats["x"], stats["mem"], stats["ffn"], stats["loss"], stats["hgrn"], stats["pool"], d_wp,
      *plan_args)


def _adamw_math(w, g, m, v):
    m = ADAM_B1 * m + (1.0 - ADAM_B1) * g
    v = ADAM_B2 * v + (1.0 - ADAM_B2) * (g * g)
    m_hat = m / (1.0 - ADAM_B1 ** ADAM_STEP)
    v_hat = v / (1.0 - ADAM_B2 ** ADAM_STEP)
    delta = -ADAM_LR * (m_hat / (jnp.sqrt(v_hat) + ADAM_EPS) + ADAM_WD * w)
    return delta, m, v


def _adamw(name, mine, theirs, w, m, v, idx, tr):
    rows = w.shape[0]
    cc = mine.shape[1]
    nb = rows // 2 // tr
    heads = w.shape[1] if w.ndim == 3 else 1
    e = cc // heads

    def body(idx_ref, a_ref, b_ref, w_ref, m_ref, v_ref, g_out, d_out, m_out, v_out):
        g = jnp.where(pl.program_id(0) // nb == idx_ref[0], a_ref[...], b_ref[...])
        if w.ndim == 2:
            g_out[...] = g
            d_out[...], m_out[...], v_out[...] = _adamw_math(w_ref[...], g, m_ref[...], v_ref[...])
        else:
            for h in range(heads):
                gh = g[:, h * e:(h + 1) * e]
                g_out[:, h, :] = gh
                d_out[:, h, :], m_out[:, h, :], v_out[:, h, :] = _adamw_math(
                    w_ref[:, h, :], gh, m_ref[:, h, :], v_ref[:, h, :])

    hspec = pl.BlockSpec((tr, cc), lambda i, idx: (i % nb, 0))
    spec = pl.BlockSpec((tr,) + w.shape[1:], lambda i, idx: (i,) + (0,) * (w.ndim - 1))
    return pl.pallas_call(
        body, name=name,
        grid_spec=pltpu.PrefetchScalarGridSpec(
            num_scalar_prefetch=1, grid=(rows // tr,),
            in_specs=[hspec, hspec, spec, spec, spec], out_specs=[spec] * 4),
        out_shape=[jax.ShapeDtypeStruct(w.shape, F32)] * 4,
        compiler_params=_params(("parallel",)),
    )(idx, mine, theirs, w, m, v)


SMALL_NAMES = ("norm_mix_g", "lb_logits", "hgrn_norm_g", "w_pool", "pool_scale", "norm_x_g", "norm_mem_g",
               "norm_ffn_g", "final_norm_g")


def _small_update(slab, d_wp, ws, ms, vs):
    n = len(SMALL_NAMES)
    half = D_MODEL // 2

    def body(slab_ref, wp_ref, *refs):
        w_refs, m_refs, v_refs, outs = refs[:n], refs[n:2 * n], refs[2 * n:3 * n], refs[3 * n:]
        row = lambda k: slab_ref[k:k + 1, :]
        lbl = w_refs[SMALL_NAMES.index("lb_logits")][...]
        s0 = _lower_bound(lbl[0:1, :], lbl[1:2, :])
        dl0 = row(ROW_LB_HGN)[:, :half] * s0 * (1.0 - s0)
        grads = dict(norm_mix_g=row(ROW_GMIX), lb_logits=jnp.concatenate([dl0, -dl0], axis=0),
                     hgrn_norm_g=row(ROW_LB_HGN)[:, half:], w_pool=wp_ref[...], pool_scale=row(ROW_PSCALE)[:, :half],
                     norm_x_g=row(ROW_GX), norm_mem_g=row(ROW_GMEM), norm_ffn_g=row(ROW_GFFN),
                     final_norm_g=row(ROW_GFIN))
        outs[0][...] = row(ROW_LOSS)[:, :128]
        for i, name in enumerate(SMALL_NAMES):
            g = grads[name]
            delta, m2, v2 = _adamw_math(w_refs[i][...], g, m_refs[i][...], v_refs[i][...])
            for o, val in zip(outs[1 + 4 * i:5 + 4 * i], (g, delta, m2, v2)):
                o[...] = val

    args = [ws[k] for k in SMALL_NAMES] + [ms[k] for k in SMALL_NAMES] + [vs[k] for k in SMALL_NAMES]
    out_shape = [jax.ShapeDtypeStruct((1, 128), F32)]
    for k in SMALL_NAMES:
        out_shape += [jax.ShapeDtypeStruct(ws[k].shape, F32)] * 4
    res = pl.pallas_call(
        body, name="small_update",
        in_specs=[VMEM] * (2 + 3 * n), out_specs=[VMEM] * len(out_shape), out_shape=out_shape,
    )(slab, d_wp, *args)
    return res[0], {k: res[1 + 4 * i:5 + 4 * i] for i, k in enumerate(SMALL_NAMES)}


ALL_NAMES = ("norm_mix_g", "w_in", "lb_logits", "hgrn_norm_g", "w_pool", "pool_scale", "w_out", "norm_x_g",
             "norm_mem_g", "w_xq", "w_xk", "w_xv", "w_xo", "norm_ffn_g", "w_ff1", "w_ff2", "final_norm_g")


def _shard_2d(name, a):
    a = a[0]
    if name in ("w_xq", "w_xk", "w_xv"):
        return a.reshape(a.shape[0], -1)
    if name == "w_xo":
        return a.reshape(-1, a.shape[-1])
    return a


def _small_2d(name, a):
    if name == "w_pool":
        return a.reshape(-1, HEAD_DIM)
    if name == "lb_logits":
        return a
    return a.reshape(1, -1)


def kernel(x, mem, norm_mix_g, w_in, lb_logits, hgrn_norm_g, w_pool, pool_scale, w_out, norm_x_g, norm_mem_g, w_xq, w_xk, w_xv, w_xo, norm_ffn_g, w_ff1, w_ff2, final_norm_g, loss_target, m_norm_mix_g, m_w_in, m_lb_logits, m_hgrn_norm_g, m_w_pool, m_pool_scale, m_w_out, m_norm_x_g, m_norm_mem_g, m_w_xq, m_w_xk, m_w_xv, m_w_xo, m_norm_ffn_g, m_w_ff1, m_w_ff2, m_final_norm_g, v_norm_mix_g, v_w_in, v_lb_logits, v_hgrn_norm_g, v_w_pool, v_pool_scale, v_w_out, v_norm_x_g, v_norm_mem_g, v_w_xq, v_w_xk, v_w_xv, v_w_xo, v_norm_ffn_g, v_w_ff1, v_w_ff2, v_final_norm_g):
    w = dict(norm_mix_g=norm_mix_g, w_in=w_in, lb_logits=lb_logits, hgrn_norm_g=hgrn_norm_g, w_pool=w_pool, pool_scale=pool_scale, w_out=w_out, norm_x_g=norm_x_g, norm_mem_g=norm_mem_g, w_xq=w_xq, w_xk=w_xk, w_xv=w_xv, w_xo=w_xo, norm_ffn_g=norm_ffn_g, w_ff1=w_ff1, w_ff2=w_ff2, final_norm_g=final_norm_g)
    m = dict(norm_mix_g=m_norm_mix_g, w_in=m_w_in, lb_logits=m_lb_logits, hgrn_norm_g=m_hgrn_norm_g, w_pool=m_w_pool, pool_scale=m_pool_scale, w_out=m_w_out, norm_x_g=m_norm_x_g, norm_mem_g=m_norm_mem_g, w_xq=m_w_xq, w_xk=m_w_xk, w_xv=m_w_xv, w_xo=m_w_xo, norm_ffn_g=m_norm_ffn_g, w_ff1=m_w_ff1, w_ff2=m_w_ff2, final_norm_g=m_final_norm_g)
    v = dict(norm_mix_g=v_norm_mix_g, w_in=v_w_in, lb_logits=v_lb_logits, hgrn_norm_g=v_hgrn_norm_g, w_pool=v_w_pool, pool_scale=v_pool_scale, w_out=v_w_out, norm_x_g=v_norm_x_g, norm_mem_g=v_norm_mem_g, w_xq=v_w_xq, w_xk=v_w_xk, w_xv=v_w_xv, w_xo=v_w_xo, norm_ffn_g=v_norm_ffn_g, w_ff1=v_w_ff1, w_ff2=v_w_ff2, final_norm_g=v_final_norm_g)

    big_w = {k: _shard_2d(k, w[k]) for k in BIG_NAMES}
    slab_oq = jnp.concatenate([big_w["w_out"], big_w["w_xq"]], axis=0).astype(BF16)
    slab_kv = jnp.concatenate([big_w["w_xk"], big_w["w_xv"]], axis=0).astype(BF16)
    shards = dict(slab_oq=slab_oq, slab_kv=slab_kv,
                  **{k: big_w[k].astype(BF16) for k in ("w_in", "w_xo", "w_ff1", "w_ff2")})

    cx, cy, cc = lax.axis_index("x"), lax.axis_index("y"), lax.axis_index("c")
    chip = 2 * cx + cy
    idx = jnp.stack([cc, chip, chip ^ 1, chip ^ 2, chip ^ 3]).astype(jnp.int32)
    small = {k: w[k] for k in SMALL_NAMES}
    grad_x, stats, d_wp, partials, keep_in, send_in = _step(x[0], mem[0], loss_target[0], small, shards, idx)

    chip_add = lambda k, keep, got: _grad_chip_add("grad_chip_add_" + k, keep, got, tr=min(256, keep.shape[0]))
    halves = {k: chip_add(k, *partials[k]) for k in EARLY_NAMES}
    early = [halves[k] for k in EARLY_NAMES]
    slab_sum, wp_sum, recv_in, *their_early = _small_allreduce(
        stats, d_wp.reshape(-1, HEAD_DIM), _Plans([_ChipExchange(send_in), _HalfExchange(early)]), send_in + early)
    halves["w_in"] = chip_add("w_in", keep_in[0], recv_in)
    theirs = dict(zip(EARLY_NAMES, their_early))
    (theirs["w_in"],) = _run_exchange("grad_half_exchange_w_in", _HalfExchange([halves["w_in"]]), [halves["w_in"]])

    grads, deltas, new_m, new_v = {}, {}, {}, {}
    for k in BIG_NAMES:
        as_held = (lambda a: a[0]) if k in ("w_xq", "w_xk", "w_xv") else functools.partial(_shard_2d, k)
        res = _adamw("adamw_" + k, halves[k], theirs[k], as_held(w[k]), as_held(m[k]), as_held(v[k]), idx,
                     tr=min(256, halves[k].shape[0]))
        for store, val in zip((grads, deltas, new_m, new_v), res):
            store[k] = val.reshape(w[k].shape)

    loss, upd = _small_update(slab_sum, wp_sum, {k: _small_2d(k, w[k]) for k in SMALL_NAMES},
                              {k: _small_2d(k, m[k]) for k in SMALL_NAMES}, {k: _small_2d(k, v[k]) for k in SMALL_NAMES})
    for k in SMALL_NAMES:
        for store, val in zip((grads, deltas, new_m, new_v), upd[k]):
            store[k] = val.reshape(w[k].shape)

    return (loss[0, 0], grad_x[None], *[grads[k] for k in ALL_NAMES], *[deltas[k] for k in ALL_NAMES],
            *[new_m[k] for k in ALL_NAMES], *[new_v[k] for k in ALL_NAMES])
```

```python
import functools

import jax
import jax.numpy as jnp
from jax import lax
from jax.experimental import pallas as pl
from jax.experimental.pallas import tpu as pltpu

F32 = jnp.float32
BF16 = jnp.bfloat16
LOG2E = 1.4426950408889634
NEG_BIG = -1e30
MAX_LOG2_GROWTH = 100.0
MESH = pl.DeviceIdType.MESH
ANY = pl.BlockSpec(memory_space=pl.ANY)
VMEM = pl.BlockSpec(memory_space=pltpu.VMEM)

D_MODEL = 1024
N_CHIPS = 4
HGRN_HEADS = 4
HEAD_DIM = 128
HGRN_WIDTH = HGRN_HEADS * HEAD_DIM
POOL_WINDOWS = (2, 4, 8, 16)
POOL_HALO = 16
SUB = 16
HALF = SUB // 2
CHUNK = 64
HEADS_PER_STEP = 4
XATTN_HEADS = 4
XATTN_HEAD_DIM = 256
EPS = 1e-6
ADAM_LR, ADAM_B1, ADAM_B2, ADAM_EPS, ADAM_WD, ADAM_STEP = 0.001, 0.9, 0.999, 1e-08, 0.01, 10

TOKEN_TILE = 512
MLP_BWD_TOKEN_TILE = 256
HGRN_BLOCK = 256
GRAD_TILE = 1024

V7X_VMEM_BYTES = 64 * 1024 * 1024
VMEM_LIMIT = V7X_VMEM_BYTES - 8 * 1024 * 1024

NN = (((1,), (0,)), ((), ()))
NT = (((1,), (1,)), ((), ()))
TN = (((0,), (0,)), ((), ()))

ROW_GMIX, ROW_GX, ROW_GMEM, ROW_GFFN, ROW_GFIN, ROW_LB_HGN, ROW_PSCALE, ROW_LOSS = range(8)


def _dot(a, b, dims=NN):
    return lax.dot_general(a, b, dims, preferred_element_type=F32)


def _sigmoid(x):
    return 1.0 / (1.0 + jnp.exp(-x))


def _rms_fwd(x, g):
    r = lax.rsqrt(jnp.mean(x * x, axis=-1, keepdims=True) + EPS)
    n = x * r
    return n * g, n, r


def _rms_bwd(dh, n, r, g):
    dn = dh * g
    dx = r * (dn - n * jnp.mean(dn * n, axis=-1, keepdims=True))
    return dx, jnp.sum(dh * n, axis=0, keepdims=True)


def _params(sem=None):
    return pltpu.CompilerParams(dimension_semantics=sem, vmem_limit_bytes=VMEM_LIMIT)


def _const(shape):
    nd = len(shape)
    return pl.BlockSpec(shape, lambda *_: (0,) * nd, pipeline_mode=pl.Buffered(1))


def _const_out(shape):
    nd = len(shape)
    return pl.BlockSpec(shape, lambda *_: (0,) * nd)


def _acc_rows(ref, t, rows):
    upd = jnp.concatenate(rows + [jnp.zeros((8 - len(rows), rows[0].shape[1]), F32)], axis=0)

    @pl.when(t == 0)
    def _():
        ref[...] = upd

    @pl.when(t > 0)
    def _():
        ref[...] = ref[...] + upd


def _fuse_exchange(body, n_in, n_out, n_scratch, plan, ndim):
    if plan is None:
        return body
    n = plan.n

    def wrapped(*refs):
        ins, cin = refs[:n_in], refs[n_in:n_in + n]
        outs, cout = refs[n_in + n:n_in + n + n_out], refs[n_in + n + n_out:n_in + 2 * n + n_out]
        rest = refs[n_in + 2 * n + n_out:]
        scr, csem = rest[:n_scratch], rest[n_scratch:]
        first = pl.program_id(0) == 0
        last = pl.program_id(0) == pl.num_programs(0) - 1
        for i in range(1, ndim):
            first = first & (pl.program_id(i) == 0)
            last = last & (pl.program_id(i) == pl.num_programs(i) - 1)

        @pl.when(first)
        def _():
            plan.start(cin, cout, csem)

        if hasattr(plan, "middle"):
            step, total = pl.program_id(0), pl.num_programs(0)
            for i in range(1, ndim):
                step, total = step * pl.num_programs(i) + pl.program_id(i), total * pl.num_programs(i)

            @pl.when(step == (5 * total) // 8)
            def _():
                plan.middle(cin, cout, csem)

        body(*ins, *outs, *scr)

        @pl.when(last)
        def _():
            plan.finish(cin, cout, csem)

    return wrapped


def _plan_extras(plan):
    if plan is None:
        return [], [], []
    return [ANY] * plan.n, list(plan.out_shape), list(plan.scratch_shapes)


def _in_proj(x, g, win_g, tm, plan=None, plan_args=()):
    s, d = x.shape
    nsh, _, wc = win_g.shape

    def body(x_ref, g_ref, w_ref, z_ref, h_ref):
        h, _, _ = _rms_fwd(x_ref[...], g_ref[...])
        hb = h.astype(BF16)
        h_ref[...] = hb
        for j in range(nsh):
            z_ref[:, j * wc:(j + 1) * wc] = _dot(hb, w_ref[j])

    x_specs, x_shapes, x_scratch = _plan_extras(plan)
    return pl.pallas_call(
        _fuse_exchange(body, 3, 2, 0, plan, 1), name="in_proj", grid=(s // tm,),
        in_specs=[pl.BlockSpec((tm, d), lambda t: (t, 0)), _const((1, d)), _const((nsh, d, wc))] + x_specs,
        out_specs=[pl.BlockSpec((tm, nsh * wc), lambda t: (t, 0)), pl.BlockSpec((tm, d), lambda t: (t, 0))] + x_specs,
        out_shape=[jax.ShapeDtypeStruct((s, nsh * wc), F32), jax.ShapeDtypeStruct((s, d), BF16)] + x_shapes,
        scratch_shapes=x_scratch,
        compiler_params=_params(("arbitrary",)),
    )(x, g, win_g, *plan_args)


def _lower_bound(l0, l1):
    m = jnp.maximum(l0, l1)
    e0, e1 = jnp.exp(l0 - m), jnp.exp(l1 - m)
    return e0 / (e0 + e1)


def _block_tri(n, group, upper):
    r = lax.broadcasted_iota(jnp.int32, (n, n), 0)
    c = lax.broadcasted_iota(jnp.int32, (n, n), 1)
    keep = (r // group == c // group) & ((c >= r) if upper else (c <= r))
    return keep.astype(BF16)


def _group_cumsum(tri, x):
    hi = x.astype(BF16)
    rest = x - hi.astype(F32)
    mid = rest.astype(BF16)
    lo = (rest - mid.astype(F32)).astype(BF16)
    return (_dot(tri, hi) + _dot(tri, mid)) + _dot(tri, lo)


def _decay(b, bj, rows, first):
    d = b - bj
    if first:
        d = jnp.where(rows >= first, d, NEG_BIG)
    return jnp.exp2(d)


class _RowSums:
    ORDER = (0, 4, 2, 6, 1, 5, 3, 7)

    def __init__(self, rows):
        self.rows = rows
        self.level = {4: {}, 2: {}, 1: {}}

    def _pair(self, p, q, d):
        return jnp.where((self.rows & d) != 0, p + pltpu.roll(p, d, axis=0), q + pltpu.roll(q, HALF - d, axis=0))

    def push(self, j, y, d=4):
        if d == 0:
            self.out = y
            return
        slot = self.level[d]
        key = j % d
        if key not in slot:
            slot[key] = (j, y)
            return
        j0, y0 = slot.pop(key)
        p, q = (y, y0) if j & d else (y0, y)
        self.push(key, self._pair(p, q, d), d // 2)

    def result(self):
        return self.out


def _hgrn_gates(qp, fp, lb):
    sq = _sigmoid(qp)
    sf = _sigmoid(fp)
    f = lb + (1.0 - lb) * sf
    return qp * sq, sq, f, sf


def _hgrn_fwd(z, l0, l1, gn, tc, unroll=1, plan=None, plan_args=()):
    s = z.shape[0]
    nsub = tc // SUB
    hd = HEAD_DIM
    hp = HEADS_PER_STEP

    def body(q_ref, f_ref, v_ref, g_ref, l0_ref, l1_ref, gn_ref, tri_ref, tric_ref, o_ref, oa_ref, st_ref,
             state, qs, ks, bs, os_):
        @pl.when(pl.program_id(1) == 0)
        def _():
            state[...] = jnp.zeros_like(state)

        cols = [slice(hh * hd, (hh + 1) * hd) for hh in range(hp)]
        q, k, lf, bc = [], [], [], []
        for hh, cs in enumerate(cols):
            qh, _, fh, _ = _hgrn_gates(q_ref[:, cs], f_ref[:, cs], _lower_bound(l0_ref[hh], l1_ref[hh]))
            q.append(qh)
            k.append(1.0 - fh)
            lf.append(jnp.log(fh) * LOG2E)
            bc.append(_group_cumsum(tric_ref[...], lf[hh]))
        bounded = functools.reduce(jnp.minimum, [jnp.min(b) for b in bc]) >= -MAX_LOG2_GROWTH

        @pl.when(bounded)
        def _():
            mask = tric_ref[...] > 0
            for hh, cs in enumerate(cols):
                qt = (q[hh] * jnp.exp2(bc[hh])).astype(BF16)
                ki = (k[hh] * jnp.exp2(-bc[hh])).astype(BF16)
                vb = v_ref[:, cs].astype(BF16)
                a = jnp.where(mask, _dot(qt, ki, NT), 0.0).astype(BF16)
                o_in = _dot(a, vb)
                for c in range(tc // CHUNK):
                    rs = slice(c * CHUNK, (c + 1) * CHUNK)
                    st = state[hh]
                    st_ref[hh, c] = st
                    os_[rs, cs] = o_in[rs] + _dot(qt[rs], st.astype(BF16), NT)
                    bl = bc[hh][(c + 1) * CHUNK - 1:(c + 1) * CHUNK, :]
                    kt = (k[hh][rs] * jnp.exp2(bl - bc[hh][rs])).astype(BF16)
                    state[hh] = st * jnp.exp2(bl) + _dot(vb[rs], kt, TN)

        @pl.when(jnp.logical_not(bounded))
        def _():
            rows = lax.broadcasted_iota(jnp.int32, (HALF, 1), 0)
            for hh, cs in enumerate(cols):
                qs[:, cs] = q[hh]
                ks[:, cs] = k[hh]
                bs[:, cs] = _group_cumsum(tri_ref[...], lf[hh])

                def step(i, carry, hh=hh, cs=cs):
                    r0 = pl.multiple_of(i * SUB, SUB)
                    q_ = qs[pl.ds(r0, SUB), cs]
                    k_ = ks[pl.ds(r0, SUB), cs]
                    b_ = bs[pl.ds(r0, SUB), cs]
                    v_ = v_ref[pl.ds(r0, SUB), cs]
                    st = state[hh]

                    @pl.when(i % (CHUNK // SUB) == 0)
                    def _():
                        st_ref[hh, i // (CHUNK // SUB)] = st

                    bl = b_[SUB - 1:SUB, :]
                    o = _dot((q_ * jnp.exp2(b_)).astype(BF16), st.astype(BF16), NT)
                    (q_lo, q_hi), (b_lo, b_hi), (o_lo, o_hi) = ((x[:HALF], x[HALF:]) for x in (q_, b_, o))
                    for j in range(SUB):
                        bj, kj, vj = b_[j:j + 1, :], k_[j:j + 1, :], v_[j:j + 1, :]
                        if j < HALF:
                            e = _decay(b_lo, bj, rows, j)
                            o_lo = o_lo + jnp.sum(q_lo * e * kj, axis=-1, keepdims=True) * vj
                        e = _decay(b_hi, bj, rows, j - HALF if j > HALF else None)
                        o_hi = o_hi + jnp.sum(q_hi * e * kj, axis=-1, keepdims=True) * vj
                    os_[pl.ds(r0, HALF), cs] = o_lo
                    os_[pl.ds(r0 + HALF, HALF), cs] = o_hi
                    kt = (k_ * jnp.exp2(bl - b_)).astype(BF16)
                    state[hh] = st * jnp.exp2(bl) + _dot(v_.astype(BF16), kt, TN)
                    return carry

                lax.fori_loop(0, nsub, step, 0, unroll=unroll)

        for hh, cs in enumerate(cols):
            o = os_[:, cs]
            o_ref[:, cs] = o
            r = lax.rsqrt(jnp.mean(o * o, axis=-1, keepdims=True) + EPS)
            gp = g_ref[:, cs]
            oa_ref[:, cs] = (o * r * gn_ref[hh] * (gp * _sigmoid(gp))).astype(BF16)

    ng = HGRN_HEADS // hp
    col = lambda k: pl.BlockSpec((tc, hp * hd), lambda h, t: (t, k * ng + h))
    vec = pl.BlockSpec((hp, 1, hd), lambda h, t: (h, 0, 0))
    out = pl.BlockSpec((tc, hp * hd), lambda h, t: (t, h))
    x_specs, x_shapes, x_scratch = _plan_extras(plan)
    return pl.pallas_call(
        _fuse_exchange(body, 9, 3, 5, plan, 2), name="hgrn_fwd", grid=(ng, s // tc),
        in_specs=[col(0), col(1), col(2), col(3), vec, vec, vec, _const((tc, tc)), _const((tc, tc))] + x_specs,
        out_specs=[out, out, pl.BlockSpec((hp, tc // CHUNK, hd, hd), lambda h, t: (h, t, 0, 0))] + x_specs,
        out_shape=[jax.ShapeDtypeStruct((s, HGRN_WIDTH), F32), jax.ShapeDtypeStruct((s, HGRN_WIDTH), BF16),
                   jax.ShapeDtypeStruct((HGRN_HEADS, s // CHUNK, hd, hd), F32)] + x_shapes,
        scratch_shapes=[pltpu.VMEM((hp, hd, hd), F32)] + [pltpu.VMEM((tc, hp * hd), F32)] * 4 + x_scratch,
        compiler_params=_params(("arbitrary", "arbitrary")),
    )(z, z, z, z, l0, l1, gn, _block_tri(tc, SUB, False), _block_tri(tc, CHUNK, False), *plan_args)


def _pooled(p, ext, tok0):
    tm = p.shape[0]
    tok = tok0 + lax.broadcasted_iota(jnp.int32, (tm, 1), 0)
    outs = []
    for g, w in enumerate(POOL_WINDOWS):
        acc = ext[:, g * HEAD_DIM:(g + 1) * HEAD_DIM]
        sh = 1
        while sh < w:
            acc = acc + pltpu.roll(acc, sh, axis=0)
            sh *= 2
        cnt = jnp.minimum(tok + 1, w).astype(F32)
        outs.append(acc[POOL_HALO:, :] / cnt - p[:, g * HEAD_DIM:(g + 1) * HEAD_DIM])
    return outs


def _pool_fwd(z, wp, scale, tm):
    s = z.shape[0]
    pw = len(POOL_WINDOWS) * HEAD_DIM
    nb = tm // POOL_HALO

    def body(p_ref, prev_ref, wp_ref, sc_ref, ob_ref):
        t = pl.program_id(0)
        p = p_ref[...]
        prev = jnp.where(t > 0, prev_ref[...], 0.0)
        pooled = _pooled(p, jnp.concatenate([prev, p], axis=0), t * tm)
        ys = [_dot(pooled[g].astype(BF16), wp_ref[g].astype(BF16)) for g in range(len(POOL_WINDOWS))]
        ob_ref[...] = (jnp.concatenate(ys, axis=1) * sc_ref[...]).astype(BF16)

    return pl.pallas_call(
        body, name="pool_fwd", grid=(s // tm,),
        in_specs=[pl.BlockSpec((tm, pw), lambda t: (t, 4)),
                  pl.BlockSpec((POOL_HALO, pw), lambda t: (jnp.maximum(t * nb - 1, 0), 4)),
                  _const(wp.shape), _const((1, pw))],
        out_specs=pl.BlockSpec((tm, pw), lambda t: (t, 0)),
        out_shape=jax.ShapeDtypeStruct((s, pw), BF16),
        compiler_params=_params(("parallel",)),
    )(z, z, wp, scale)


def _kv_proj(mem, g, slab_g):
    m, d = mem.shape
    rows = d // N_CHIPS

    def body(mem_ref, g_ref, wk_ref, wv_ref, xk_ref, xv_ref):
        hm, _, _ = _rms_fwd(mem_ref[...], g_ref[...])
        hb = hm.astype(BF16)
        xk_ref[...] = _dot(hb, wk_ref[...].reshape(d, d)).astype(BF16)
        xv_ref[...] = _dot(hb, wv_ref[...].reshape(d, d)).astype(BF16)

    blk = lambda k: pl.BlockSpec((N_CHIPS, rows, d), lambda i: (0, k, 0))
    return pl.pallas_call(
        body, name="kv_proj", grid=(1,),
        in_specs=[_const((m, d)), _const((1, d)), blk(0), blk(1)],
        out_specs=[_const_out((m, d)), _const_out((m, d))],
        out_shape=[jax.ShapeDtypeStruct((m, d), BF16)] * 2,
        compiler_params=_params(("arbitrary",)),
    )(mem, g, slab_g, slab_g)


def _softmax_rows(sc):
    e = jnp.exp(sc - jnp.max(sc, axis=-1, keepdims=True))
    return e / jnp.sum(e, axis=-1, keepdims=True)


def _mix_xattn_fwd(x, oa, ob, gx, slab_g, wo_g, xk, xv, tm, plan=None, plan_args=()):
    s, d = x.shape
    m = xk.shape[0]
    rows = d // N_CHIPS
    hw = oa.shape[1]
    e = XATTN_HEAD_DIM

    def body(x_ref, oa_ref, ob_ref, gx_ref, wout_ref, wq_ref, wo_ref, xk_ref, xv_ref,
             x1_ref, mixed_ref, hq_ref, xq_ref, att_ref, x2_ref):
        mixed = jnp.concatenate([oa_ref[...], ob_ref[...]], axis=1)
        mixed_ref[...] = mixed
        x1 = x_ref[...] + _dot(mixed, wout_ref[...].reshape(d, d))
        x1_ref[...] = x1
        hq, _, _ = _rms_fwd(x1, gx_ref[...])
        hqb = hq.astype(BF16)
        hq_ref[...] = hqb
        xq = _dot(hqb, wq_ref[...].reshape(d, d)).astype(BF16)
        xq_ref[...] = xq
        atts = []
        for h in range(XATTN_HEADS):
            cs = slice(h * e, (h + 1) * e)
            p = _softmax_rows(_dot(xq[:, cs], xk_ref[:, cs], NT) * (e ** -0.5))
            atts.append(_dot(p.astype(BF16), xv_ref[:, cs]).astype(BF16))
        att = jnp.concatenate(atts, axis=1)
        att_ref[...] = att
        for j in range(N_CHIPS):
            x2_ref[:, j * rows:(j + 1) * rows] = x1[:, j * rows:(j + 1) * rows] + _dot(att, wo_ref[j])

    tile = lambda w: pl.BlockSpec((tm, w), lambda t: (t, 0))
    blk = lambda k: pl.BlockSpec((N_CHIPS, rows, d), lambda t: (0, k, 0), pipeline_mode=pl.Buffered(1))
    x_specs, x_shapes, x_scratch = _plan_extras(plan)
    return pl.pallas_call(
        _fuse_exchange(body, 9, 6, 0, plan, 1), name="mix_xattn_fwd", grid=(s // tm,),
        in_specs=[tile(d), tile(hw), tile(hw), _const((1, d)), blk(0), blk(1), _const(wo_g.shape),
                  _const((m, d)), _const((m, d))] + x_specs,
        out_specs=[tile(d)] * 6 + x_specs,
        out_shape=[jax.ShapeDtypeStruct((s, d), F32)] + [jax.ShapeDtypeStruct((s, d), BF16)] * 4
                  + [jax.ShapeDtypeStruct((s, d), F32)] + x_shapes,
        scratch_shapes=x_scratch,
        compiler_params=_params(("arbitrary",)),
    )(x, oa, ob, gx, slab_g, slab_g, wo_g, xk, xv, *plan_args)


def _mlp_loss_fwd(x2, gffn, gfin, w1_g, w2_g, target, tm):
    s, d = x2.shape
    wr = w1_g.shape[1]

    def body(x2_ref, gffn_ref, gfin_ref, w1_ref, w2_ref, tg_ref, a_ref, hf_ref, dx3_ref, dx3b_ref, st_ref):
        x2v = x2_ref[...]
        hf, _, _ = _rms_fwd(x2v, gffn_ref[...])
        hfb = hf.astype(BF16)
        hf_ref[...] = hfb
        acc = x2v
        for j in range(N_CHIPS):
            a = _dot(hfb, w1_ref[j])
            a_ref[:, j * wr:(j + 1) * wr] = a
            r = jnp.maximum(a, 0.0)
            acc = acc + _dot((r * r).astype(BF16), w2_ref[j])
        gf = gfin_ref[...]
        y, n, r3 = _rms_fwd(acc, gf)
        err = y - tg_ref[...]
        loss = 0.5 * jnp.sum(jnp.sum(err * err, axis=-1, keepdims=True) * (1.0 / d), axis=0, keepdims=True)
        dy = err * (1.0 / d)
        dx3, dgf = _rms_bwd(dy, n, r3, gf)
        dx3_ref[...] = dx3
        dx3b_ref[...] = dx3.astype(BF16)
        _acc_rows(st_ref, pl.program_id(0), [dgf, jnp.broadcast_to(loss, (1, d))])

    tile = lambda w: pl.BlockSpec((tm, w), lambda t: (t, 0))
    blk = lambda k: pl.BlockSpec((N_CHIPS, wr, d), lambda t: (0, k, 0), pipeline_mode=pl.Buffered(1))
    return pl.pallas_call(
        body, name="mlp_loss_fwd", grid=(s // tm,),
        in_specs=[tile(d), _const((1, d)), _const((1, d)), blk(0), blk(0), tile(d)],
        out_specs=[tile(N_CHIPS * wr), tile(d), tile(d), tile(d), _const_out((8, d))],
        out_shape=[jax.ShapeDtypeStruct((s, N_CHIPS * wr), F32), jax.ShapeDtypeStruct((s, d), BF16),
                   jax.ShapeDtypeStruct((s, d), F32), jax.ShapeDtypeStruct((s, d), BF16),
                   jax.ShapeDtypeStruct((8, d), F32)],
        compiler_params=_params(("arbitrary",)),
    )(x2, gffn, gfin, w1_g, w2_g, target)


def _mlp_bwd(dx3, dx3b, a, x2, gffn, w1_g, w2_g, tm):
    s, d = x2.shape
    wr = w1_g.shape[1]

    def body(dx3_ref, dx3b_ref, a_ref, x2_ref, g_ref, w1_ref, w2_ref, da_ref, u_ref, dx2_ref, dx2b_ref, st_ref):
        dyb = dx3b_ref[...]
        dhf = jnp.zeros((tm, d), F32)
        for j in range(N_CHIPS):
            r = jnp.maximum(a_ref[:, j * wr:(j + 1) * wr], 0.0)
            da = (_dot(dyb, w2_ref[j], NT) * (2.0 * r)).astype(BF16)
            da_ref[:, j * wr:(j + 1) * wr] = da
            u_ref[:, j * wr:(j + 1) * wr] = (r * r).astype(BF16)
            dhf = dhf + _dot(da, w1_ref[j], NT)
        g = g_ref[...]
        _, n, r2 = _rms_fwd(x2_ref[...], g)
        dxn, dg = _rms_bwd(dhf, n, r2, g)
        dx2 = dx3_ref[...] + dxn
        dx2_ref[...] = dx2
        dx2b_ref[...] = dx2.astype(BF16)
        _acc_rows(st_ref, pl.program_id(0), [dg])

    tile = lambda w: pl.BlockSpec((tm, w), lambda t: (t, 0))
    blk = lambda k: pl.BlockSpec((N_CHIPS, wr, d), lambda t: (0, k, 0), pipeline_mode=pl.Buffered(1))
    nf = N_CHIPS * wr
    return pl.pallas_call(
        body, name="mlp_bwd", grid=(s // tm,),
        in_specs=[tile(d), tile(d), tile(nf), tile(d), _const((1, d)), blk(0), blk(0)],
        out_specs=[tile(nf), tile(nf), tile(d), tile(d), _const_out((8, d))],
        out_shape=[jax.ShapeDtypeStruct((s, nf), BF16), jax.ShapeDtypeStruct((s, nf), BF16),
                   jax.ShapeDtypeStruct((s, d), F32), jax.ShapeDtypeStruct((s, d), BF16),
                   jax.ShapeDtypeStruct((8, d), F32)],
        compiler_params=_params(("arbitrary",)),
    )(dx3, dx3b, a, x2, gffn, w1_g, w2_g)


def _xattn_mix_bwd(dx2, x1, xq, xk, xv, gx, slab_g, wo_g, tm, plan=None, plan_args=()):
    s, d = x1.shape
    m = xk.shape[0]
    rows = d // N_CHIPS
    e = XATTN_HEAD_DIM

    def body(dx2_ref, x1_ref, xq_ref, xk_ref, xv_ref, gx_ref, wout_ref, wq_ref, wo_ref,
             dx1_ref, dx1b_ref, dxq_ref, dmix_ref, dxk_ref, dxv_ref, st_ref):
        t = pl.program_id(0)
        dx2 = dx2_ref[...]
        dx2b = dx2.astype(BF16)
        datt = jnp.zeros((tm, d), F32)
        for j in range(N_CHIPS):
            datt = datt + _dot(dx2b[:, j * rows:(j + 1) * rows], wo_ref[j], NT)
        dattb = datt.astype(BF16)
        dxqs, dxks, dxvs = [], [], []
        for h in range(XATTN_HEADS):
            cs = slice(h * e, (h + 1) * e)
            xq_h, xk_h, xv_h = xq_ref[:, cs], xk_ref[:, cs], xv_ref[:, cs]
            p = _softmax_rows(_dot(xq_h, xk_h, NT) * (e ** -0.5))
            dp = _dot(dattb[:, cs], xv_h, NT)
            ds = (p * (dp - jnp.sum(dp * p, axis=-1, keepdims=True)) * (e ** -0.5)).astype(BF16)
            dxqs.append(_dot(ds, xk_h).astype(BF16))
            dxks.append(_dot(ds, xq_h, TN))
            dxvs.append(_dot(p.astype(BF16), dattb[:, cs], TN))
        dxq = jnp.concatenate(dxqs, axis=1)
        dxq_ref[...] = dxq
        dxk = jnp.concatenate(dxks, axis=1)
        dxv = jnp.concatenate(dxvs, axis=1)

        @pl.when(t == 0)
        def _():
            dxk_ref[...] = dxk
            dxv_ref[...] = dxv

        @pl.when(t > 0)
        def _():
            dxk_ref[...] = dxk_ref[...] + dxk
            dxv_ref[...] = dxv_ref[...] + dxv

        dhq = jnp.concatenate([_dot(dxq, wq_ref[j], NT) for j in range(N_CHIPS)], axis=1)
        g = gx_ref[...]
        _, n, r1 = _rms_fwd(x1_ref[...], g)
        dxn, dg = _rms_bwd(dhq, n, r1, g)
        dx1 = dx2 + dxn
        dx1_ref[...] = dx1
        dx1b = dx1.astype(BF16)
        dx1b_ref[...] = dx1b
        for j in range(N_CHIPS):
            dmix_ref[:, j * rows:(j + 1) * rows] = _dot(dx1b, wout_ref[j], NT)
        _acc_rows(st_ref, t, [dg])

    tile = lambda: pl.BlockSpec((tm, d), lambda t: (t, 0))
    blk = lambda k: pl.BlockSpec((N_CHIPS, rows, d), lambda t: (0, k, 0), pipeline_mode=pl.Buffered(1))
    x_specs, x_shapes, x_scratch = _plan_extras(plan)
    return pl.pallas_call(
        _fuse_exchange(body, 9, 7, 0, plan, 1), name="xattn_mix_bwd", grid=(s // tm,),
        in_specs=[tile(), tile(), tile(), _const((m, d)), _const((m, d)), _const((1, d)), blk(0), blk(1),
                  _const(wo_g.shape)] + x_specs,
        out_specs=[tile(), tile(), tile(), tile(), _const_out((m, d)), _const_out((m, d)), _const_out((8, d))]
                  + x_specs,
        out_shape=[jax.ShapeDtypeStruct((s, d), F32), jax.ShapeDtypeStruct((s, d), BF16),
                   jax.ShapeDtypeStruct((s, d), BF16), jax.ShapeDtypeStruct((s, d), F32),
                   jax.ShapeDtypeStruct((m, d), F32), jax.ShapeDtypeStruct((m, d), F32),
                   jax.ShapeDtypeStruct((8, d), F32)] + x_shapes,
        scratch_shapes=x_scratch,
        compiler_params=_params(("arbitrary",)),
    )(dx2, x1, xq, xk, xv, gx, slab_g, slab_g, wo_g, *plan_args)


def _kv_bwd(mem, g, dxk, dxv, slab_g):
    m, d = mem.shape
    rows = d // N_CHIPS

    def body(mem_ref, g_ref, dxk_ref, dxv_ref, wk_ref, wv_ref, dwk_ref, dwv_ref, st_ref):
        gv = g_ref[...]
        hm, n, _ = _rms_fwd(mem_ref[...], gv)
        hb = hm.astype(BF16)
        dkb = dxk_ref[...].astype(BF16)
        dvb = dxv_ref[...].astype(BF16)
        dhm = []
        for j in range(N_CHIPS):
            hj = hb[:, j * rows:(j + 1) * rows]
            dwk_ref[j] = _dot(hj, dkb, TN)
            dwv_ref[j] = _dot(hj, dvb, TN)
            dhm.append(_dot(dkb, wk_ref[j], NT) + _dot(dvb, wv_ref[j], NT))
        dg = jnp.sum(jnp.concatenate(dhm, axis=1) * n, axis=0, keepdims=True)
        st_ref[...] = jnp.concatenate([dg, jnp.zeros((7, d), F32)], axis=0)

    blk = lambda k: pl.BlockSpec((N_CHIPS, rows, d), lambda i: (0, k, 0))
    return pl.pallas_call(
        body, name="kv_bwd", grid=(1,),
        in_specs=[_const((m, d)), _const((1, d)), _const((m, d)), _const((m, d)), blk(0), blk(1)],
        out_specs=[_const_out((N_CHIPS, rows, d)), _const_out((N_CHIPS, rows, d)), _const_out((8, d))],
        out_shape=[jax.ShapeDtypeStruct((N_CHIPS, rows, d), F32)] * 2 + [jax.ShapeDtypeStruct((8, d), F32)],
        compiler_params=_params(("arbitrary",)),
    )(mem, g, dxk, dxv, slab_g, slab_g)


def _pool_bwd(z, dmix, wp, scale, tm, plan=None, plan_args=()):
    s = z.shape[0]
    ng = len(POOL_WINDOWS)
    pw = ng * HEAD_DIM
    nb = tm // POOL_HALO
    nt = s // tm
    n_ext = tm + POOL_HALO

    def body(p_ref, prev_ref, dm_ref, dmn_ref, wp_ref, sc_ref, dp_ref, dwp_ref, st_ref):
        t = pl.program_id(0)
        p = p_ref[...]
        prev = jnp.where(t > 0, prev_ref[...], 0.0)
        pooled = _pooled(p, jnp.concatenate([prev, p], axis=0), t * tm)
        dm = dm_ref[...]
        dme = jnp.concatenate([dm, jnp.where(t < nt - 1, dmn_ref[...], 0.0)], axis=0) * sc_ref[...]
        tok = t * tm + lax.broadcasted_iota(jnp.int32, (n_ext, 1), 0)
        dsc, dps, dwps = [], [], []
        for g, w in enumerate(POOL_WINDOWS):
            cs = slice(g * HEAD_DIM, (g + 1) * HEAD_DIM)
            wpb = wp_ref[g].astype(BF16)
            pb = pooled[g].astype(BF16)
            dsc.append(jnp.sum(dm[:, cs] * _dot(pb, wpb), axis=0, keepdims=True))
            dye = dme[:, cs].astype(BF16)
            dwps.append(_dot(pb, dye[:tm], TN))
            dpe = _dot(dye, wpb, NT)
            acc = dpe / jnp.minimum(tok + 1, w).astype(F32)
            sh = 1
            while sh < w:
                acc = acc + pltpu.roll(acc, n_ext - sh, axis=0)
                sh *= 2
            dps.append(acc[:tm] - dpe[:tm])
        dp_ref[...] = jnp.concatenate(dps, axis=1)
        dsc_row = jnp.concatenate(dsc, axis=1)

        @pl.when(t == 0)
        def _():
            for g in range(ng):
                dwp_ref[g] = dwps[g]

        @pl.when(t > 0)
        def _():
            for g in range(ng):
                dwp_ref[g] = dwp_ref[g] + dwps[g]

        _acc_rows(st_ref, t, [dsc_row])

    x_specs, x_shapes, x_scratch = _plan_extras(plan)
    return pl.pallas_call(
        _fuse_exchange(body, 6, 3, 0, plan, 1), name="pool_bwd", grid=(nt,),
        in_specs=[pl.BlockSpec((tm, pw), lambda t: (t, 4)),
                  pl.BlockSpec((POOL_HALO, pw), lambda t: (jnp.maximum(t * nb - 1, 0), 4)),
                  pl.BlockSpec((tm, pw), lambda t: (t, 1)),
                  pl.BlockSpec((POOL_HALO, pw), lambda t: (jnp.minimum((t + 1) * nb, s // POOL_HALO - 1), 1)),
                  _const(wp.shape), _const((1, pw))] + x_specs,
        out_specs=[pl.BlockSpec((tm, pw), lambda t: (t, 0)), _const_out(wp.shape), _const_out((8, pw))] + x_specs,
        out_shape=[jax.ShapeDtypeStruct((s, pw), F32), jax.ShapeDtypeStruct(wp.shape, F32),
                   jax.ShapeDtypeStruct((8, pw), F32)] + x_shapes,
        scratch_shapes=x_scratch,
        compiler_params=_params(("arbitrary",)),
    )(z, z, dmix, dmix, wp, scale, *plan_args)


def _hgrn_bwd(z, o, dmix, st, l0, l1, gn, tc, unroll=1, plan=None, plan_args=()):
    s = z.shape[0]
    nsub = tc // SUB
    nt = s // tc
    hd = HEAD_DIM
    hp = HEADS_PER_STEP

    def body(q_ref, f_ref, v_ref, g_ref, l0_ref, l1_ref, gn_ref, o_ref, dm_ref, st_ref,
             tril_ref, triu_ref, trilc_ref, triuc_ref,
             dq_ref, df_ref, di_ref, dg_ref, stat_ref, dstate, qs, ks, bs, dos, dqs, dks, dbs, sts):
        t = pl.program_id(1)

        @pl.when(t == 0)
        def _():
            dstate[...] = jnp.zeros_like(dstate)

        cols = [slice(hh * hd, (hh + 1) * hd) for hh in range(hp)]
        heads = []
        for hh, cs in enumerate(cols):
            lb = _lower_bound(l0_ref[hh], l1_ref[hh])
            qp = q_ref[:, cs]
            q, sq, f, sf = _hgrn_gates(qp, f_ref[:, cs], lb)
            lf = jnp.log(f) * LOG2E
            o = o_ref[:, cs]
            r = lax.rsqrt(jnp.mean(o * o, axis=-1, keepdims=True) + EPS)
            n = o * r
            gnv = gn_ref[hh]
            gp = g_ref[:, cs]
            sg = _sigmoid(gp)
            dm = dm_ref[:, cs]
            dg_ref[:, cs] = dm * (n * gnv) * (sg * (1.0 + gp * (1.0 - sg)))
            don = dm * (gp * sg)
            dn = don * gnv
            heads.append(dict(lb=lb, qp=qp, q=q, sq=sq, f=f, sf=sf, k=1.0 - f, lf=lf,
                              bc=_group_cumsum(trilc_ref[...], lf), dgn=jnp.sum(don * n, axis=0, keepdims=True),
                              do=r * (dn - n * jnp.mean(dn * n, axis=-1, keepdims=True))))
        bounded = functools.reduce(jnp.minimum, [jnp.min(h["bc"]) for h in heads]) >= -MAX_LOG2_GROWTH

        def factored(hh, cs, q, k, bc, do_all):
            eb = jnp.exp2(bc)
            eib = jnp.exp2(-bc)
            qt = (q * eb).astype(BF16)
            ki = (k * eib).astype(BF16)
            vb = v_ref[:, cs].astype(BF16)
            dob = do_all.astype(BF16)
            mask = trilc_ref[...] > 0
            a = jnp.where(mask, _dot(qt, ki, NT), 0.0).astype(BF16)
            da = jnp.where(mask, _dot(dob, vb, NT), 0.0).astype(BF16)
            dq_in = _dot(da, ki)
            dk_in = _dot(da, qt, TN)
            dv_in = _dot(a, dob, TN)
            last_row = lax.broadcasted_iota(jnp.int32, (CHUNK, 1), 0) == CHUNK - 1
            for c in reversed(range(tc // CHUNK)):
                rs = slice(c * CHUNK, (c + 1) * CHUNK)
                stp = st_ref[hh, c]
                dst = dstate[hh]
                dstb = dst.astype(BF16)
                bl = bc[(c + 1) * CHUNK - 1:(c + 1) * CHUNK, :]
                ekl = jnp.exp2(bl - bc[rs])
                ebl = jnp.exp2(bl)
                kt = k[rs] * ekl
                dq_st = _dot(dob[rs], stp.astype(BF16)) * eb[rs]
                dkt = _dot(vb[rs], dstb)
                extra = jnp.sum(kt * dkt, axis=0, keepdims=True) + ebl * jnp.sum(stp * dst, axis=0, keepdims=True)
                dqs[rs, cs] = dq_st + dq_in[rs] * eb[rs]
                dks[rs, cs] = dkt * ekl + dk_in[rs] * eib[rs]
                di_ref[rs, cs] = _dot(kt.astype(BF16), dstb, NT) + dv_in[rs]
                dbs[rs, cs] = (q[rs] * dq_st - kt * dkt + jnp.where(last_row, extra, 0.0)
                               + (qt[rs].astype(F32) * dq_in[rs] - ki[rs].astype(F32) * dk_in[rs]))
                dstate[hh] = dst * ebl + _dot(dob[rs], qt[rs], TN)
            dbs[:, cs] = _group_cumsum(triuc_ref[...], dbs[:, cs])

        def exact(hh, cs, q, k, lf, do_all):
            qs[:, cs] = q
            ks[:, cs] = k
            bs[:, cs] = _group_cumsum(tril_ref[...], lf)
            dos[:, cs] = do_all
            per = CHUNK // SUB

            def restore(i, carry):
                @pl.when(i % per == 0)
                def _():
                    sts[i] = st_ref[hh, i // per]

                @pl.when(i % per != 0)
                def _():
                    rp = pl.multiple_of((i - 1) * SUB, SUB)
                    b_ = bs[pl.ds(rp, SUB), cs]
                    bl = b_[SUB - 1:SUB, :]
                    kt = (ks[pl.ds(rp, SUB), cs] * jnp.exp2(bl - b_)).astype(BF16)
                    sts[i] = sts[i - 1] * jnp.exp2(bl) + _dot(v_ref[pl.ds(rp, SUB), cs].astype(BF16), kt, TN)

                return carry

            lax.fori_loop(0, nsub, restore, 0)
            rows = lax.broadcasted_iota(jnp.int32, (HALF, 1), 0)
            last_row = lax.broadcasted_iota(jnp.int32, (SUB, 1), 0) == SUB - 1

            def step(i, carry):
                ii = nsub - 1 - i
                r0 = pl.multiple_of(ii * SUB, SUB)
                q_ = qs[pl.ds(r0, SUB), cs]
                k_ = ks[pl.ds(r0, SUB), cs]
                b_ = bs[pl.ds(r0, SUB), cs]
                v_ = v_ref[pl.ds(r0, SUB), cs]
                do_ = dos[pl.ds(r0, SUB), cs]
                stp = sts[ii]
                dst = dstate[hh]
                bl = b_[SUB - 1:SUB, :]
                eb = jnp.exp2(b_)
                ekl = jnp.exp2(bl - b_)
                ebl = jnp.exp2(bl)
                dob = do_.astype(BF16)
                dstb = dst.astype(BF16)
                kt = k_ * ekl
                dq = _dot(dob, stp.astype(BF16)) * eb
                dkt = _dot(v_.astype(BF16), dstb)
                dk = dkt * ekl
                dv = _dot(kt.astype(BF16), dstb, NT)
                extra = jnp.sum(kt * dkt, axis=0, keepdims=True) + ebl * jnp.sum(stp * dst, axis=0, keepdims=True)
                halves = lambda x: [x[:HALF], x[HALF:]]
                q_h, b_h, do_h, dq_h, dk_h, dv_h = (halves(x) for x in (q_, b_, do_, dq, dk, dv))
                for own in range(2):
                    dk_rows, dv_rows = _RowSums(rows), _RowSums(rows)
                    for jj in _RowSums.ORDER:
                        j = own * HALF + jj
                        bj, kj, vj = b_[j:j + 1, :], k_[j:j + 1, :], v_[j:j + 1, :]
                        dk_sum = dv_sum = None
                        for h in range(own, 2):
                            e = _decay(b_h[h], bj, rows, jj if h == own else None)
                            pe = q_h[h] * e
                            acol = jnp.sum(pe * kj, axis=-1, keepdims=True)
                            dacol = jnp.sum(do_h[h] * vj, axis=-1, keepdims=True)
                            dq_h[h] = dq_h[h] + dacol * (e * kj)
                            dk_sum = dacol * pe if dk_sum is None else dk_sum + dacol * pe
                            dv_sum = acol * do_h[h] if dv_sum is None else dv_sum + acol * do_h[h]
                        dk_rows.push(jj, dk_sum)
                        dv_rows.push(jj, dv_sum)
                    dk_h[own] = dk_h[own] + dk_rows.result()
                    dv_h[own] = dv_h[own] + dv_rows.result()
                dq, dk, dv = (jnp.concatenate(x, axis=0) for x in (dq_h, dk_h, dv_h))
                dqs[pl.ds(r0, SUB), cs] = dq
                dks[pl.ds(r0, SUB), cs] = dk
                di_ref[pl.ds(r0, SUB), cs] = dv
                dbs[pl.ds(r0, SUB), cs] = q_ * dq - k_ * dk + jnp.where(last_row, extra, 0.0)
                dstate[hh] = dst * ebl + _dot(dob, (q_ * eb).astype(BF16), TN)
                return carry

            lax.fori_loop(0, nsub, step, 0, unroll=unroll)
            dbs[:, cs] = _group_cumsum(triu_ref[...], dbs[:, cs])

        @pl.when(bounded)
        def _():
            for hh, cs in enumerate(cols):
                factored(hh, cs, heads[hh]["q"], heads[hh]["k"], heads[hh]["bc"], heads[hh]["do"])

        @pl.when(jnp.logical_not(bounded))
        def _():
            for hh, cs in enumerate(cols):
                exact(hh, cs, heads[hh]["q"], heads[hh]["k"], heads[hh]["lf"], heads[hh]["do"])

        for hh, cs in enumerate(cols):
            h = heads[hh]
            dfv = dbs[:, cs] / h["f"] - dks[:, cs]
            df_ref[:, cs] = dfv * (1.0 - h["lb"]) * h["sf"] * (1.0 - h["sf"])
            dlb = jnp.sum(dfv * (1.0 - h["sf"]), axis=0, keepdims=True)
            dq_ref[:, cs] = dqs[:, cs] * (h["sq"] * (1.0 + h["qp"] * (1.0 - h["sq"])))
            _acc_rows(stat_ref.at[hh], t, [h["dgn"], dlb])

    rev = lambda t: nt - 1 - t
    ng = HGRN_HEADS // hp
    col = lambda k: pl.BlockSpec((tc, hp * hd), lambda h, t: (rev(t), k * ng + h))
    vec = pl.BlockSpec((hp, 1, hd), lambda h, t: (h, 0, 0))
    head = pl.BlockSpec((tc, hp * hd), lambda h, t: (rev(t), h))
    x_specs, x_shapes, x_scratch = _plan_extras(plan)
    return pl.pallas_call(
        _fuse_exchange(body, 14, 5, 9, plan, 2), name="hgrn_bwd", grid=(ng, nt),
        in_specs=[col(0), col(1), col(2), col(3), vec, vec, vec, head, head,
                  pl.BlockSpec((hp, tc // CHUNK, hd, hd), lambda h, t: (h, rev(t), 0, 0))]
                 + [_const((tc, tc))] * 4 + x_specs,
        out_specs=[head, head, head, head, pl.BlockSpec((hp, 8, hd), lambda h, t: (h, 0, 0))] + x_specs,
        out_shape=[jax.ShapeDtypeStruct((s, HGRN_WIDTH), F32)] * 4 + [jax.ShapeDtypeStruct((HGRN_HEADS, 8, hd), F32)]
                  + x_shapes,
        scratch_shapes=[pltpu.VMEM((hp, hd, hd), F32)] + [pltpu.VMEM((tc, hp * hd), F32)] * 7
                       + [pltpu.VMEM((nsub, hd, hd), F32)] + x_scratch,
        compiler_params=_params(("arbitrary", "arbitrary")),
    )(z, z, z, z, l0, l1, gn, o, dmix, st, _block_tri(tc, SUB, False), _block_tri(tc, SUB, True),
      _block_tri(tc, CHUNK, False), _block_tri(tc, CHUNK, True), *plan_args)


def _in_bwd(dparts, dx1, x, g, win_g, tm, plan=None, plan_args=()):
    s, d = x.shape
    nsh, _, wc = win_g.shape
    pw = dparts[0].shape[1]

    def body(dq_ref, df_ref, di_ref, dg_ref, dp_ref, dx1_ref, x_ref, g_ref, w_ref, gx_ref, dz_ref, st_ref):
        dz = jnp.concatenate([dq_ref[...], df_ref[...], di_ref[...], dg_ref[...], dp_ref[...]], axis=1).astype(BF16)
        dz_ref[...] = dz
        dh = jnp.zeros((tm, d), F32)
        for j in range(nsh):
            dh = dh + _dot(dz[:, j * wc:(j + 1) * wc], w_ref[j], NT)
        gv = g_ref[...]
        _, n, r = _rms_fwd(x_ref[...], gv)
        dxn, dg = _rms_bwd(dh, n, r, gv)
        gx_ref[...] = dx1_ref[...] + dxn
        _acc_rows(st_ref, pl.program_id(0), [dg])

    tile = lambda w: pl.BlockSpec((tm, w), lambda t: (t, 0))
    x_specs, x_shapes, x_scratch = _plan_extras(plan)
    return pl.pallas_call(
        _fuse_exchange(body, 9, 3, 0, plan, 1), name="in_bwd", grid=(s // tm,),
        in_specs=[tile(pw)] * 5 + [tile(d), tile(d), _const((1, d)), _const(win_g.shape)] + x_specs,
        out_specs=[tile(d), tile(nsh * wc), _const_out((8, d))] + x_specs,
        out_shape=[jax.ShapeDtypeStruct((s, d), F32), jax.ShapeDtypeStruct((s, nsh * wc), BF16),
                   jax.ShapeDtypeStruct((8, d), F32)] + x_shapes,
        scratch_shapes=x_scratch,
        compiler_params=_params(("arbitrary",)),
    )(*dparts, dx1, x, g, win_g, *plan_args)


def _tn_grad(name, a, b, out_rows, out_cols, a_sharded, plan=None, plan_args=()):
    s = a.shape[0]
    tr, tc = min(out_rows, GRAD_TILE), min(out_cols, GRAD_TILE)
    nr, nc = out_rows // tr, out_cols // tc

    def body(a_ref, b_ref, o_ref):
        o_ref[...] = _dot(a_ref[...], b_ref[...], TN)

    a_map = (lambda j, i, k: (0, j * nr + i)) if a_sharded else (lambda j, i, k: (0, i))
    b_map = (lambda j, i, k: (0, k)) if a_sharded else (lambda j, i, k: (0, j * nc + k))
    x_specs, x_shapes, x_scratch = _plan_extras(plan)
    res = pl.pallas_call(
        _fuse_exchange(body, 2, 1, 0, plan, 3), name=name, grid=(N_CHIPS, nr, nc),
        in_specs=[pl.BlockSpec((s, tr), a_map), pl.BlockSpec((s, tc), b_map)] + x_specs,
        out_specs=[pl.BlockSpec((None, tr, tc), lambda j, i, k: (j, i, k))] + x_specs,
        out_shape=[jax.ShapeDtypeStruct((N_CHIPS, out_rows, out_cols), F32)] + x_shapes,
        scratch_shapes=x_scratch,
        compiler_params=_params(("arbitrary", "arbitrary", "arbitrary")),
    )(a, b, *plan_args)
    return res if plan else res[0]


FFN_NAMES = ("w_ff1", "w_ff2")
ATTN_NAMES = ("w_xo", "w_xq", "w_out", "w_xk", "w_xv")
EARLY_NAMES = FFN_NAMES + ATTN_NAMES
BIG_NAMES = EARLY_NAMES + ("w_in",)


def _halved(g):
    return g.reshape(N_CHIPS, 2, g.shape[1] // 2, g.shape[2])


def _pair_adds(names, gs, got, idx):
    pairs = [_grad_pair_add("grad_pair_add_" + k, g, r, idx, tr=min(256, g.shape[2])) for k, g, r in zip(names, gs, got)]
    return [p[0] for p in pairs], [p[1] for p in pairs]


def _step(x, mem, target, small, shards, idx):
    d = x.shape[1]
    l0 = small["lb_logits"][0].reshape(HGRN_HEADS, 1, HEAD_DIM)
    l1 = small["lb_logits"][1].reshape(HGRN_HEADS, 1, HEAD_DIM)
    gn = small["hgrn_norm_g"].reshape(HGRN_HEADS, 1, HEAD_DIM)
    wp = small["w_pool"].reshape(len(POOL_WINDOWS), HEAD_DIM, HEAD_DIM)
    psc = small["pool_scale"].reshape(1, -1)
    gmix, gx, gmem, gffn = (small[k].reshape(1, d) for k in ("norm_mix_g", "norm_x_g", "norm_mem_g", "norm_ffn_g"))
    gfin = small["final_norm_g"].reshape(1, d)

    (win_g,) = _run_exchange("gather_w_in", _WeightGather([shards["w_in"]]), [shards["w_in"]])
    z, h, kv_g = _in_proj(x, gmix, win_g, tm=TOKEN_TILE, plan=_WeightGather([shards["slab_kv"]]),
                          plan_args=[shards["slab_kv"]])
    mid_w = [shards["slab_oq"], shards["w_xo"], shards["w_ff1"]]
    o, oa, st, oq_g, wo_g, w1_g = _hgrn_fwd(z, l0, l1, gn, tc=HGRN_BLOCK, unroll=8,
                                            plan=_WeightGather(mid_w), plan_args=mid_w)
    ob = _pool_fwd(z, wp, psc, tm=TOKEN_TILE)
    xk, xv = _kv_proj(mem, gmem, kv_g)
    late_w = [shards["w_ff2"]]
    x1, mixed, hq, xq, att, x2, w2_g = _mix_xattn_fwd(x, oa, ob, gx, oq_g, wo_g, xk, xv, tm=TOKEN_TILE,
                                                      plan=_WeightGather(late_w), plan_args=late_w)
    a, hf, dx3, dx3b, st_loss = _mlp_loss_fwd(x2, gffn, gfin, w1_g, w2_g, target, tm=TOKEN_TILE)

    da, u, dx2, dx2b, st_ffn = _mlp_bwd(dx3, dx3b, a, x2, gffn, w1_g, w2_g, tm=MLP_BWD_TOKEN_TILE)
    g_ff1 = [_halved(_tn_grad("dw_ff1", hf, da, d, d, False))]
    dw_ff2, *got = _tn_grad("dw_ff2", u, dx3b, d, d, True, plan=_PairExchange(g_ff1), plan_args=g_ff1)
    keep_ff1, send_ff1 = _pair_adds(("w_ff1",), g_ff1, got, idx)
    g_ff2 = [_halved(dw_ff2)]
    dx1, dx1b, dxq, dmix, dxk, dxv, st_x, *got = _xattn_mix_bwd(
        dx2, x1, xq, xk, xv, gx, oq_g, wo_g, tm=TOKEN_TILE,
        plan=_Plans([_ChipExchange(send_ff1), _PairExchange(g_ff2)]), plan_args=send_ff1 + g_ff2)
    recv_ff1 = got[:1]
    keep_ff2, send_ff2 = _pair_adds(("w_ff2",), g_ff2, got[1:], idx)
    dw = {}
    dw["w_xo"] = _tn_grad("dw_xo", att, dx2b, d, d // N_CHIPS, False)
    dw["w_xq"] = _tn_grad("dw_xq", hq, dxq, d // N_CHIPS, d, True)
    dw["w_out"] = _tn_grad("dw_out", mixed, dx1b, d // N_CHIPS, d, True)
    dw["w_xk"], dw["w_xv"], st_mem = _kv_bwd(mem, gmem, dxk, dxv, kv_g)
    gs_attn = [_halved(dw[k]) for k in ATTN_NAMES]
    dp, d_wp, st_pool, *got_attn = _pool_bwd(z, dmix, wp, psc, tm=TOKEN_TILE, plan=_PairExchange(gs_attn), plan_args=gs_attn)
    keep_attn, send_attn = _pair_adds(ATTN_NAMES, gs_attn, got_attn, idx)
    sends = send_ff2 + send_attn
    dq, df, di, dg, st_hgrn, *received = _hgrn_bwd(z, o, dmix, st, l0, l1, gn, tc=HGRN_BLOCK, unroll=4,
                                                    plan=_ChipExchange(sends), plan_args=sends)
    grad_x, dz, st_mix = _in_bwd([dq, df, di, dg, dp], dx1, x, gmix, win_g, tm=TOKEN_TILE)
    keeps = keep_ff1 + keep_ff2 + keep_attn
    received = recv_ff1 + list(received)
    gs_in = [_halved(_tn_grad("dw_in", h, dz, d, win_g.shape[2], False))]
    got_in = _run_exchange("grad_pair_exchange_w_in", _PairExchange(gs_in), gs_in)
    keep_in, send_in = _pair_adds(("w_in",), gs_in, got_in, idx)

    partials = dict(zip(EARLY_NAMES, zip(keeps, received)))
    stats = dict(mix=st_mix, x=st_x, mem=st_mem, ffn=st_ffn, loss=st_loss, hgrn=st_hgrn, pool=st_pool)
    return grad_x, stats, d_wp, partials, keep_in, send_in


def _place():
    x, y, c = lax.axis_index("x"), lax.axis_index("y"), lax.axis_index("c")
    return x, y, c, [(x, 1 - y), (1 - x, y), (1 - x, 1 - y)]


def _rcopy(src, dst, ssem, rsem, dev):
    return pltpu.make_async_remote_copy(src_ref=src, dst_ref=dst, send_sem=ssem, recv_sem=rsem,
                                        device_id=dev, device_id_type=MESH)


class _WeightGather:
    def __init__(self, shards):
        self.n = len(shards)
        self.rows = [w.shape[0] for w in shards]
        self.out_shape = [jax.ShapeDtypeStruct((N_CHIPS,) + w.shape, w.dtype) for w in shards]
        dma = pltpu.SemaphoreType.DMA
        self.scratch_shapes = [dma((self.n,))] * 2 + [dma((self.n, 2))] * 4 + [dma((self.n, 3))] * 2

    def _copies(self, ins, outs, sems, phase):
        osem, orsem, ssem, rsem, qsem, qrsem, fsem, frsem = sems
        x, y, c, _ = _place()
        chip = 2 * x + y
        sib = (x, y, 1 - c)
        nbrs = [(x, 1 - y), (1 - x, y)]
        diag = 2 * (1 - x) + (1 - y)
        cps = []
        if phase in ("start", "finish"):
            cps += [("own", _rcopy(ins[a], outs[a].at[chip], osem.at[a], orsem.at[a], sib)) for a in range(self.n)]
        for a in range(self.n):
            hr = self.rows[a] // 2
            qr = hr // 2
            half = lambda who, hc, a=a, hr=hr: outs[a].at[who, pl.ds(hc * hr, hr), :]
            quarter = lambda who, k, a=a, hr=hr, qr=qr: outs[a].at[who, pl.ds(c * hr + k * qr, qr), :]
            for r, (px, py) in enumerate(nbrs):
                pc = 2 * px + py
                ox, oy = nbrs[1 - r]
                if phase in ("start", "finish"):
                    cps.append(("direct", _rcopy(ins[a].at[pl.ds(c * hr, hr), :], half(chip, c), ssem.at[a, r],
                                                 rsem.at[a, r], (px, py, c))))
                if phase == "middle":
                    cps.append(("arrived", _rcopy(half(pc, c), half(pc, c), ssem.at[a, r], rsem.at[a, r], (px, py, c))))
                if phase in ("middle", "finish"):
                    cps.append(("relay", _rcopy(quarter(pc, r), quarter(pc, r), qsem.at[a, r], qrsem.at[a, r],
                                                (ox, oy, c))))
                    cps.append(("pass", _rcopy(half(pc, c), half(pc, c), fsem.at[a, r], frsem.at[a, r], sib)))
                if phase == "finish":
                    cps.append(("relayed", _rcopy(quarter(diag, r), quarter(diag, r), qsem.at[a, r], qrsem.at[a, r],
                                                  (ox, oy, c))))
                    cps.append(("pass_in", _rcopy(half(pc, 1 - c), half(pc, 1 - c), fsem.at[a, r], frsem.at[a, r], sib)))
            if phase == "finish":
                cps.append(("pass_diag", _rcopy(half(diag, c), half(diag, c), fsem.at[a, 2], frsem.at[a, 2], sib)))
                cps.append(("pass_in", _rcopy(half(diag, 1 - c), half(diag, 1 - c), fsem.at[a, 2], frsem.at[a, 2], sib)))
        return cps

    def start(self, ins, outs, sems):
        for _, cp in self._copies(ins, outs, sems, "start"):
            cp.start()

    def middle(self, ins, outs, sems):
        for kind, cp in self._copies(ins, outs, sems, "middle"):
            if kind == "arrived":
                cp.wait_recv()
            else:
                cp.start()

    def finish(self, ins, outs, sems):
        cps = self._copies(ins, outs, sems, "finish")
        for kind, cp in cps:
            if kind == "relayed":
                cp.wait_recv()
        for kind, cp in cps:
            if kind == "pass_diag":
                cp.start()
        for kind, cp in cps:
            if kind == "pass_in":
                cp.wait_recv()
        for kind, cp in cps:
            if kind in ("direct", "relay", "pass", "pass_diag"):
                cp.wait_send()
        for kind, cp in cps:
            if kind == "own":
                cp.wait()


class _ChipExchange:
    def __init__(self, sends):
        self.n = len(sends)
        self.out_shape = [jax.ShapeDtypeStruct(g.shape, g.dtype) for g in sends]
        self.scratch_shapes = [pltpu.SemaphoreType.DMA((self.n, 3))] * 2

    def _copies(self, ins, outs, sems):
        ssem, rsem = sems
        _, _, c, peers = _place()
        return [_rcopy(ins[a].at[r], outs[a].at[r], ssem.at[a, r], rsem.at[a, r], (px, py, c))
                for a in range(self.n) for r, (px, py) in enumerate(peers)]

    def start(self, ins, outs, sems):
        for cp in self._copies(ins, outs, sems):
            cp.start()

    def finish(self, ins, outs, sems):
        for cp in self._copies(ins, outs, sems):
            cp.wait()


class _Plans:
    def __init__(self, plans):
        self.plans = plans
        self.n = sum(p.n for p in plans)
        self.out_shape = [s for p in plans for s in p.out_shape]
        self.scratch_shapes = [s for p in plans for s in p.scratch_shapes]

    def _each(self, ins, outs, sems):
        a = b = 0
        for p in self.plans:
            ns = len(p.scratch_shapes)
            yield p, ins[a:a + p.n], outs[a:a + p.n], sems[b:b + ns]
            a, b = a + p.n, b + ns

    def start(self, ins, outs, sems):
        for p, i, o, s in self._each(ins, outs, sems):
            p.start(i, o, s)

    def finish(self, ins, outs, sems):
        for p, i, o, s in self._each(ins, outs, sems):
            p.finish(i, o, s)


def _run_exchange(name, plan, arrays):
    n = plan.n

    def body(*refs):
        ins, outs, sems = refs[:n], refs[n:2 * n], refs[2 * n:]
        plan.start(ins, outs, sems)
        if hasattr(plan, "middle"):
            plan.middle(ins, outs, sems)
        plan.finish(ins, outs, sems)

    return pl.pallas_call(
        body, name=name, in_specs=[ANY] * n, out_specs=[ANY] * n,
        out_shape=plan.out_shape, scratch_shapes=plan.scratch_shapes,
    )(*arrays)


class _PairExchange:
    def __init__(self, gs):
        self.n = len(gs)
        self.out_shape = [jax.ShapeDtypeStruct((g.shape[0],) + g.shape[2:], g.dtype) for g in gs]
        self.scratch_shapes = [pltpu.SemaphoreType.DMA((self.n,))] * 2

    def _copies(self, ins, outs, sems):
        ssem, rsem = sems
        x, y, c, _ = _place()
        return [_rcopy(ins[a].at[:, 1 - c], outs[a], ssem.at[a], rsem.at[a], (x, y, 1 - c)) for a in range(self.n)]

    def start(self, ins, outs, sems):
        for cp in self._copies(ins, outs, sems):
            cp.start()

    def finish(self, ins, outs, sems):
        for cp in self._copies(ins, outs, sems):
            cp.wait()


def _grad_pair_add(name, g, got, idx, tr):
    _, _, hr, cc = g.shape

    def body(idx_ref, g0, g1, g2, g3, r0, r1, r2, r3, keep_ref, send_ref):
        keep_ref[...] = g0[...] + r0[...]
        for q, (gq, rq) in enumerate(((g1, r1), (g2, r2), (g3, r3))):
            send_ref[q] = (gq[...] + rq[...]).astype(BF16)

    gspec = lambda q: pl.BlockSpec((None, None, tr, cc), lambda i, idx: (idx[1 + q], idx[0], i, 0))
    rspec = lambda q: pl.BlockSpec((None, tr, cc), lambda i, idx: (idx[1 + q], i, 0))
    return pl.pallas_call(
        body, name=name,
        grid_spec=pltpu.PrefetchScalarGridSpec(
            num_scalar_prefetch=1, grid=(hr // tr,),
            in_specs=[gspec(q) for q in range(4)] + [rspec(q) for q in range(4)],
            out_specs=[pl.BlockSpec((tr, cc), lambda i, idx: (i, 0)), pl.BlockSpec((3, tr, cc), lambda i, idx: (0, i, 0))]),
        out_shape=[jax.ShapeDtypeStruct((hr, cc), F32), jax.ShapeDtypeStruct((3, hr, cc), BF16)],
        compiler_params=_params(("parallel",)),
    )(idx, g, g, g, g, got, got, got, got)


def _grad_chip_add(name, keep, got, tr):
    hr, cc = keep.shape

    def body(k_ref, g_ref, o_ref):
        o_ref[...] = ((k_ref[...] + g_ref[0].astype(F32)) + g_ref[1].astype(F32)) + g_ref[2].astype(F32)

    return pl.pallas_call(
        body, name=name, grid=(hr // tr,),
        in_specs=[pl.BlockSpec((tr, cc), lambda i: (i, 0)), pl.BlockSpec((3, tr, cc), lambda i: (0, i, 0))],
        out_specs=pl.BlockSpec((tr, cc), lambda i: (i, 0)),
        out_shape=jax.ShapeDtypeStruct((hr, cc), F32),
        compiler_params=_params(("parallel",)),
    )(keep, got)


class _HalfExchange:
    def __init__(self, ts):
        self.n = len(ts)
        self.out_shape = [jax.ShapeDtypeStruct(t.shape, t.dtype) for t in ts]
        self.scratch_shapes = [pltpu.SemaphoreType.DMA((self.n,))] * 2

    def _copies(self, ins, outs, sems):
        ssem, rsem = sems
        x, y, c, _ = _place()
        return [_rcopy(ins[a], outs[a], ssem.at[a], rsem.at[a], (x, y, 1 - c)) for a in range(self.n)]

    def start(self, ins, outs, sems):
        for cp in self._copies(ins, outs, sems):
            cp.start()

    def finish(self, ins, outs, sems):
        for cp in self._copies(ins, outs, sems):
            cp.wait()


def _small_allreduce(stats, d_wp, plan, plan_args):
    d = D_MODEL
    half = d // 2
    wps = d_wp.shape
    n = plan.n

    def body(mix_ref, x_ref, mem_ref, ffn_ref, loss_ref, hg_ref, pool_ref, wp_ref, *refs):
        cin, (slab_out, wp_out), cout = refs[:n], refs[n:n + 2], refs[n + 2:2 * n + 2]
        slab_buf, wp_buf, sib_s, sib_w, ssem, rsem = refs[2 * n + 2:2 * n + 8]
        csem = refs[2 * n + 8:]
        x, y, c, peers = _place()
        chip = 2 * x + y
        sib = (x, y, 1 - c)
        hgn = jnp.concatenate([hg_ref[h, 0:1, :] for h in range(HGRN_HEADS)], axis=1)
        dlb = jnp.concatenate([hg_ref[h, 1:2, :] for h in range(HGRN_HEADS)], axis=1)
        slab_buf[0] = jnp.concatenate([
            mix_ref[0:1, :], x_ref[0:1, :], mem_ref[0:1, :], ffn_ref[0:1, :], loss_ref[0:1, :],
            jnp.concatenate([dlb, hgn], axis=1),
            jnp.concatenate([pool_ref[0:1, :], jnp.zeros((1, half), F32)], axis=1),
            loss_ref[1:2, :]], axis=0)
        wp_buf[0] = wp_ref[...]
        pair = [_rcopy(slab_buf.at[0], sib_s, ssem.at[0], rsem.at[0], sib),
                _rcopy(wp_buf.at[0], sib_w, ssem.at[1], rsem.at[1], sib)]
        for cp in pair:
            cp.start()
        for cp in pair:
            cp.wait()
        slab_buf[0] = slab_buf[0] + sib_s[...]
        wp_buf[0] = wp_buf[0] + sib_w[...]
        cps = []
        for r, (px, py) in enumerate(peers):
            cps.append(_rcopy(slab_buf.at[0], slab_buf.at[r + 1], ssem.at[2 + 2 * r], rsem.at[2 + 2 * r], (px, py, c)))
            cps.append(_rcopy(wp_buf.at[0], wp_buf.at[r + 1], ssem.at[3 + 2 * r], rsem.at[3 + 2 * r], (px, py, c)))
        for cp in cps:
            cp.start()
        plan.start(cin, cout, csem)
        for cp in cps:
            cp.wait()
        tot_s, tot_w = slab_buf[chip], wp_buf[chip]
        for j in range(1, N_CHIPS):
            tot_s = tot_s + slab_buf[jnp.bitwise_xor(j, chip)]
            tot_w = tot_w + wp_buf[jnp.bitwise_xor(j, chip)]
        slab_out[...] = tot_s
        wp_out[...] = tot_w
        plan.finish(cin, cout, csem)

    return pl.pallas_call(
        body, name="small_allreduce",
        in_specs=[VMEM] * 8 + [ANY] * n, out_specs=[VMEM] * 2 + [ANY] * n,
        out_shape=[jax.ShapeDtypeStruct((8, d), F32), jax.ShapeDtypeStruct(wps, F32)] + list(plan.out_shape),
        scratch_shapes=[pltpu.VMEM((N_CHIPS, 8, d), F32), pltpu.VMEM((N_CHIPS,) + wps, F32),
                        pltpu.VMEM((8, d), F32), pltpu.VMEM(wps, F32),
                        pltpu.SemaphoreType.DMA((8,)), pltpu.SemaphoreType.DMA((8,))] + list(plan.scratch_shapes),
    )(stats["mix"], stats["x"], stats["mem"], stats["ffn"], stats["loss"], stats["hgrn"], stats["pool"], d_wp,
      *plan_args)


def _adamw_math(w, g, m, v):
    m = ADAM_B1 * m + (1.0 - ADAM_B1) * g
    v = ADAM_B2 * v + (1.0 - ADAM_B2) * (g * g)
    m_hat = m / (1.0 - ADAM_B1 ** ADAM_STEP)
    v_hat = v / (1.0 - ADAM_B2 ** ADAM_STEP)
    delta = -ADAM_LR * (m_hat / (jnp.sqrt(v_hat) + ADAM_EPS) + ADAM_WD * w)
    return delta, m, v


def _adamw(name, mine, theirs, w, m, v, idx, tr):
    rows = w.shape[0]
    cc = mine.shape[1]
    nb = rows // 2 // tr
    heads = w.shape[1] if w.ndim == 3 else 1
    e = cc // heads

    def body(idx_ref, a_ref, b_ref, w_ref, m_ref, v_ref, g_out, d_out, m_out, v_out):
        g = jnp.where(pl.program_id(0) // nb == idx_ref[0], a_ref[...], b_ref[...])
        if w.ndim == 2:
            g_out[...] = g
            d_out[...], m_out[...], v_out[...] = _adamw_math(w_ref[...], g, m_ref[...], v_ref[...])
        else:
            for h in range(heads):
                gh = g[:, h * e:(h + 1) * e]
                g_out[:, h, :] = gh
                d_out[:, h, :], m_out[:, h, :], v_out[:, h, :] = _adamw_math(
                    w_ref[:, h, :], gh, m_ref[:, h, :], v_ref[:, h, :])

    hspec = pl.BlockSpec((tr, cc), lambda i, idx: (i % nb, 0))
    spec = pl.BlockSpec((tr,) + w.shape[1:], lambda i, idx: (i,) + (0,) * (w.ndim - 1))
    return pl.pallas_call(
        body, name=name,
        grid_spec=pltpu.PrefetchScalarGridSpec(
            num_scalar_prefetch=1, grid=(rows // tr,),
            in_specs=[hspec, hspec, spec, spec, spec], out_specs=[spec] * 4),
        out_shape=[jax.ShapeDtypeStruct(w.shape, F32)] * 4,
        compiler_params=_params(("parallel",)),
    )(idx, mine, theirs, w, m, v)


SMALL_NAMES = ("norm_mix_g", "lb_logits", "hgrn_norm_g", "w_pool", "pool_scale", "norm_x_g", "norm_mem_g",
               "norm_ffn_g", "final_norm_g")


def _small_update(slab, d_wp, ws, ms, vs):
    n = len(SMALL_NAMES)
    half = D_MODEL // 2

    def body(slab_ref, wp_ref, *refs):
        w_refs, m_refs, v_refs, outs = refs[:n], refs[n:2 * n], refs[2 * n:3 * n], refs[3 * n:]
        row = lambda k: slab_ref[k:k + 1, :]
        lbl = w_refs[SMALL_NAMES.index("lb_logits")][...]
        s0 = _lower_bound(lbl[0:1, :], lbl[1:2, :])
        dl0 = row(ROW_LB_HGN)[:, :half] * s0 * (1.0 - s0)
        grads = dict(norm_mix_g=row(ROW_GMIX), lb_logits=jnp.concatenate([dl0, -dl0], axis=0),
                     hgrn_norm_g=row(ROW_LB_HGN)[:, half:], w_pool=wp_ref[...], pool_scale=row(ROW_PSCALE)[:, :half],
                     norm_x_g=row(ROW_GX), norm_mem_g=row(ROW_GMEM), norm_ffn_g=row(ROW_GFFN),
                     final_norm_g=row(ROW_GFIN))
        outs[0][...] = row(ROW_LOSS)[:, :128]
        for i, name in enumerate(SMALL_NAMES):
            g = grads[name]
            delta, m2, v2 = _adamw_math(w_refs[i][...], g, m_refs[i][...], v_refs[i][...])
            for o, val in zip(outs[1 + 4 * i:5 + 4 * i], (g, delta, m2, v2)):
                o[...] = val

    args = [ws[k] for k in SMALL_NAMES] + [ms[k] for k in SMALL_NAMES] + [vs[k] for k in SMALL_NAMES]
    out_shape = [jax.ShapeDtypeStruct((1, 128), F32)]
    for k in SMALL_NAMES:
        out_shape += [jax.ShapeDtypeStruct(ws[k].shape, F32)] * 4
    res = pl.pallas_call(
        body, name="small_update",
        in_specs=[VMEM] * (2 + 3 * n), out_specs=[VMEM] * len(out_shape), out_shape=out_shape,
    )(slab, d_wp, *args)
    return res[0], {k: res[1 + 4 * i:5 + 4 * i] for i, k in enumerate(SMALL_NAMES)}


ALL_NAMES = ("norm_mix_g", "w_in", "lb_logits", "hgrn_norm_g", "w_pool", "pool_scale", "w_out", "norm_x_g",
             "norm_mem_g", "w_xq", "w_xk", "w_xv", "w_xo", "norm_ffn_g", "w_ff1", "w_ff2", "final_norm_g")


def _shard_2d(name, a):
    a = a[0]
    if name in ("w_xq", "w_xk", "w_xv"):
        return a.reshape(a.shape[0], -1)
    if name == "w_xo":
        return a.reshape(-1, a.shape[-1])
    return a


def _small_2d(name, a):
    if name == "w_pool":
        return a.reshape(-1, HEAD_DIM)
    if name == "lb_logits":
        return a
    return a.reshape(1, -1)


def kernel(x, mem, norm_mix_g, w_in, lb_logits, hgrn_norm_g, w_pool, pool_scale, w_out, norm_x_g, norm_mem_g, w_xq, w_xk, w_xv, w_xo, norm_ffn_g, w_ff1, w_ff2, final_norm_g, loss_target, m_norm_mix_g, m_w_in, m_lb_logits, m_hgrn_norm_g, m_w_pool, m_pool_scale, m_w_out, m_norm_x_g, m_norm_mem_g, m_w_xq, m_w_xk, m_w_xv, m_w_xo, m_norm_ffn_g, m_w_ff1, m_w_ff2, m_final_norm_g, v_norm_mix_g, v_w_in, v_lb_logits, v_hgrn_norm_g, v_w_pool, v_pool_scale, v_w_out, v_norm_x_g, v_norm_mem_g, v_w_xq, v_w_xk, v_w_xv, v_w_xo, v_norm_ffn_g, v_w_ff1, v_w_ff2, v_final_norm_g):
    w = dict(norm_mix_g=norm_mix_g, w_in=w_in, lb_logits=lb_logits, hgrn_norm_g=hgrn_norm_g, w_pool=w_pool, pool_scale=pool_scale, w_out=w_out, norm_x_g=norm_x_g, norm_mem_g=norm_mem_g, w_xq=w_xq, w_xk=w_xk, w_xv=w_xv, w_xo=w_xo, norm_ffn_g=norm_ffn_g, w_ff1=w_ff1, w_ff2=w_ff2, final_norm_g=final_norm_g)
    m = dict(norm_mix_g=m_norm_mix_g, w_in=m_w_in, lb_logits=m_lb_logits, hgrn_norm_g=m_hgrn_norm_g, w_pool=m_w_pool, pool_scale=m_pool_scale, w_out=m_w_out, norm_x_g=m_norm_x_g, norm_mem_g=m_norm_mem_g, w_xq=m_w_xq, w_xk=m_w_xk, w_xv=m_w_xv, w_xo=m_w_xo, norm_ffn_g=m_norm_ffn_g, w_ff1=m_w_ff1, w_ff2=m_w_ff2, final_norm_g=m_final_norm_g)
    v = dict(norm_mix_g=v_norm_mix_g, w_in=v_w_in, lb_logits=v_lb_logits, hgrn_norm_g=v_hgrn_norm_g, w_pool=v_w_pool, pool_scale=v_pool_scale, w_out=v_w_out, norm_x_g=v_norm_x_g, norm_mem_g=v_norm_mem_g, w_xq=v_w_xq, w_xk=v_w_xk, w_xv=v_w_xv, w_xo=v_w_xo, norm_ffn_g=v_norm_ffn_g, w_ff1=v_w_ff1, w_ff2=v_w_ff2, final_norm_g=v_final_norm_g)

    big_w = {k: _shard_2d(k, w[k]) for k in BIG_NAMES}
    slab_oq = jnp.concatenate([big_w["w_out"], big_w["w_xq"]], axis=0).astype(BF16)
    slab_kv = jnp.concatenate([big_w["w_xk"], big_w["w_xv"]], axis=0).astype(BF16)
    shards = dict(slab_oq=slab_oq, slab_kv=slab_kv,
                  **{k: big_w[k].astype(BF16) for k in ("w_in", "w_xo", "w_ff1", "w_ff2")})

    cx, cy, cc = lax.axis_index("x"), lax.axis_index("y"), lax.axis_index("c")
    chip = 2 * cx + cy
    idx = jnp.stack([cc, chip, chip ^ 1, chip ^ 2, chip ^ 3]).astype(jnp.int32)
    small = {k: w[k] for k in SMALL_NAMES}
    grad_x, stats, d_wp, partials, keep_in, send_in = _step(x[0], mem[0], loss_target[0], small, shards, idx)

    chip_add = lambda k, keep, got: _grad_chip_add("grad_chip_add_" + k, keep, got, tr=min(256, keep.shape[0]))
    halves = {k: chip_add(k, *partials[k]) for k in EARLY_NAMES}
    early = [halves[k] for k in EARLY_NAMES]
    slab_sum, wp_sum, recv_in, *their_early = _small_allreduce(
        stats, d_wp.reshape(-1, HEAD_DIM), _Plans([_ChipExchange(send_in), _HalfExchange(early)]), send_in + early)
    halves["w_in"] = chip_add("w_in", keep_in[0], recv_in)
    theirs = dict(zip(EARLY_NAMES, their_early))
    (theirs["w_in"],) = _run_exchange("grad_half_exchange_w_in", _HalfExchange([halves["w_in"]]), [halves["w_in"]])

    grads, deltas, new_m, new_v = {}, {}, {}, {}
    for k in BIG_NAMES:
        as_held = (lambda a: a[0]) if k in ("w_xq", "w_xk", "w_xv") else functools.partial(_shard_2d, k)
        res = _adamw("adamw_" + k, halves[k], theirs[k], as_held(w[k]), as_held(m[k]), as_held(v[k]), idx,
                     tr=min(256, halves[k].shape[0]))
        for store, val in zip((grads, deltas, new_m, new_v), res):
            store[k] = val.reshape(w[k].shape)

    loss, upd = _small_update(slab_sum, wp_sum, {k: _small_2d(k, w[k]) for k in SMALL_NAMES},
                              {k: _small_2d(k, m[k]) for k in SMALL_NAMES}, {k: _small_2d(k, v[k]) for k in SMALL_NAMES})
    for k in SMALL_NAMES:
        for store, val in zip((grads, deltas, new_m, new_v), upd[k]):
            store[k] = val.reshape(w[k].shape)

    return (loss[0, 0], grad_x[None], *[grads[k] for k in ALL_NAMES], *[deltas[k] for k in ALL_NAMES],
            *[new_m[k] for k in ALL_NAMES], *[new_v[k] for k in ALL_NAMES])
```
